```python
import jax, jax.numpy as jnp
from jax import lax
import numpy as np

D_MODEL = 1024
BATCH = 8
SEQ = 4096
DEPTH = 4

CHUNK = 64
N_META = 16
D_CONV = D_MODEL // 2
N_SB_HEADS = 8
SB_HEAD_DIM = 64
D_SB = N_SB_HEADS * SB_HEAD_DIM
D_MIX = D_CONV + D_SB
CONV_WIDTH = 31
Q_BLOCK = 128
D_IN_PROJ = 3 * D_CONV + 4 * D_SB
RMS_EPS = 1e-6
LN_EPS = 1e-5

kernel_name = "hymba_conformer_stickbreaking_trunk"


def rms_norm(x, g):
    xf = x.astype(jnp.float32)
    y = xf * lax.rsqrt(jnp.mean(xf * xf, axis=-1, keepdims=True) + RMS_EPS)
    return (y * g.astype(jnp.float32)).astype(x.dtype)


def layer_norm(x, g, b):
    xf = x.astype(jnp.float32)
    mu = jnp.mean(xf, axis=-1, keepdims=True)
    var = jnp.mean(jnp.square(xf - mu), axis=-1, keepdims=True)
    y = (xf - mu) * lax.rsqrt(var + LN_EPS)
    return (y * g.astype(jnp.float32) + b.astype(jnp.float32)).astype(x.dtype)


def causal_depthwise_conv(x, w, b):
    c = x.shape[-1]
    y = lax.conv_general_dilated(
        x, w.astype(x.dtype)[:, None, :],
        window_strides=(1,), padding=[(CONV_WIDTH - 1, 0)],
        dimension_numbers=("NWC", "WIO", "NWC"), feature_group_count=c)
    return y + b.astype(x.dtype)


def stick_breaking_attention(q, k, v):
    bsz, seq_len, n_heads, head_dim = q.shape
    lp = -(-seq_len // Q_BLOCK) * Q_BLOCK
    pad = ((0, 0), (0, lp - seq_len), (0, 0), (0, 0))
    q, k, v = (jnp.pad(t, pad).transpose(0, 2, 1, 3) for t in (q, k, v))
    n_blocks = lp // Q_BLOCK
    q_blocks = q.reshape(bsz, n_heads, n_blocks, Q_BLOCK, head_dim).transpose(2, 0, 1, 3, 4)
    key_pos = jnp.arange(lp)
    scale = head_dim ** -0.5

    def one_block(args):
        q_blk, blk = args
        z = jnp.einsum("bhqd,bhkd->bhqk", q_blk, k).astype(jnp.float32) * scale
        q_pos = blk * Q_BLOCK + jnp.arange(Q_BLOCK)
        visible = key_pos[None, :] < q_pos[:, None]
        log_stay = jnp.where(visible, jax.nn.log_sigmoid(-z), 0.0)
        tail = lax.cumsum(log_stay, axis=3, reverse=True) - log_stay
        a = jnp.where(visible, jnp.exp(jax.nn.log_sigmoid(z) + tail), 0.0)
        return jnp.einsum("bhqk,bhkd->bhqd", a.astype(v.dtype), v)

    out = lax.map(one_block, (q_blocks, jnp.arange(n_blocks)))
    out = out.transpose(1, 0, 3, 2, 4).reshape(bsz, lp, n_heads, head_dim)
    return out[:, :seq_len]


def hybrid_layer(h, pre_g, post_g, w_in, conv_w, conv_b, conv_ln_g, conv_ln_b, w_pw2, b_pw2, w_out):
    bsz, seq_len, _ = h.shape
    u = rms_norm(h, pre_g)
    proj = u @ w_in.astype(u.dtype)
    glu_a, glu_b, conv_gate, q, k, v, sb_gate = jnp.split(
        proj, np.cumsum([D_CONV, D_CONV, D_CONV, D_SB, D_SB, D_SB]).tolist(), axis=-1)

    c = glu_a * jax.nn.sigmoid(glu_b)
    c = causal_depthwise_conv(c, conv_w, conv_b)
    c = jax.nn.silu(layer_norm(c, conv_ln_g, conv_ln_b))
    c = c @ w_pw2.astype(c.dtype) + b_pw2.astype(c.dtype)
    c = c * jax.nn.silu(conv_gate)

    heads = lambda t: t.reshape(bsz, seq_len, N_SB_HEADS, SB_HEAD_DIM)
    s = stick_breaking_attention(heads(q), heads(k), heads(v)).reshape(bsz, seq_len, D_SB)
    s = s * jax.nn.silu(sb_gate)

    mixed = jnp.concatenate([c, s], axis=-1) @ w_out.astype(h.dtype)
    return h + rms_norm(mixed, post_g)


def _fwd_setup_inputs(seed: int = 0) -> dict:
    key = jax.random.key(seed)
    ks = jax.random.split(key, 12)
    f32 = jnp.float32
    nrm = lambda k, shape, s: jax.random.normal(k, shape, f32) * s
    return {
        "x": nrm(ks[0], (BATCH, SEQ, D_MODEL), 1.0),
        "meta_tokens": nrm(ks[1], (N_META, D_MODEL), 1.0),
        "pre_norm_g": 1.0 + nrm(ks[2], (DEPTH, D_MODEL), 0.02),
        "post_norm_g": 1.0 + nrm(ks[3], (DEPTH, D_MODEL), 0.02),
        "w_in": nrm(ks[4], (DEPTH, D_MODEL, D_IN_PROJ), D_MODEL ** -0.5),
        "conv_w": nrm(ks[5], (DEPTH, CONV_WIDTH, D_CONV), CONV_WIDTH ** -0.5),
        "conv_b": nrm(ks[6], (DEPTH, D_CONV), 0.02),
        "conv_ln_g": 1.0 + nrm(ks[7], (DEPTH, D_CONV), 0.02),
        "conv_ln_b": nrm(ks[8], (DEPTH, D_CONV), 0.02),
        "w_pw2": nrm(ks[9], (DEPTH, D_CONV, D_CONV), D_CONV ** -0.5),
        "b_pw2": nrm(ks[10], (DEPTH, D_CONV), 0.02),
        "w_out": nrm(ks[11], (DEPTH, D_MIX, D_MODEL), D_MIX ** -0.5),
    }


def _fwd_reference(x, meta_tokens, pre_norm_g, post_norm_g, w_in, conv_w, conv_b, conv_ln_g,
              conv_ln_b, w_pw2, b_pw2, w_out):
    bsz = x.shape[0]
    meta = jnp.broadcast_to(meta_tokens.astype(x.dtype)[None], (bsz, N_META, D_MODEL))
    h = jnp.concatenate([meta, x], axis=1)
    for l in range(DEPTH):
        h = hybrid_layer(h, pre_norm_g[l], post_norm_g[l], w_in[l], conv_w[l], conv_b[l],
                         conv_ln_g[l], conv_ln_b[l], w_pw2[l], b_pw2[l], w_out[l])
    return h[:, N_META:]


import jax as _jax
import jax.numpy as _jnp

TWIN_FORMAT = 'train_step'
FWD_PARAMS = ['x', 'meta_tokens', 'pre_norm_g', 'post_norm_g', 'w_in', 'conv_w', 'conv_b', 'conv_ln_g', 'conv_ln_b', 'w_pw2', 'b_pw2', 'w_out']
TWIN_WEIGHTS = ['meta_tokens', 'pre_norm_g', 'post_norm_g', 'w_in', 'conv_w', 'conv_b', 'conv_ln_g', 'conv_ln_b', 'w_pw2', 'b_pw2', 'w_out']
TWIN_DIFF_INPUT = 'x'
TWIN_INPUTS = ['x', 'meta_tokens', 'pre_norm_g', 'post_norm_g', 'w_in', 'conv_w', 'conv_b', 'conv_ln_g', 'conv_ln_b', 'w_pw2', 'b_pw2', 'w_out', 'loss_target', 'm_meta_tokens', 'm_pre_norm_g', 'm_post_norm_g', 'm_w_in', 'm_conv_w', 'm_conv_b', 'm_conv_ln_g', 'm_conv_ln_b', 'm_w_pw2', 'm_b_pw2', 'm_w_out', 'v_meta_tokens', 'v_pre_norm_g', 'v_post_norm_g', 'v_w_in', 'v_conv_w', 'v_conv_b', 'v_conv_ln_g', 'v_conv_ln_b', 'v_w_pw2', 'v_b_pw2', 'v_w_out']
TWIN_OUTPUTS = ['loss', 'grad_x', 'grad_meta_tokens', 'grad_pre_norm_g', 'grad_post_norm_g', 'grad_w_in', 'grad_conv_w', 'grad_conv_b', 'grad_conv_ln_g', 'grad_conv_ln_b', 'grad_w_pw2', 'grad_b_pw2', 'grad_w_out', 'delta_meta_tokens', 'delta_pre_norm_g', 'delta_post_norm_g', 'delta_w_in', 'delta_conv_w', 'delta_conv_b', 'delta_conv_ln_g', 'delta_conv_ln_b', 'delta_w_pw2', 'delta_b_pw2', 'delta_w_out', 'new_m_meta_tokens', 'new_m_pre_norm_g', 'new_m_post_norm_g', 'new_m_w_in', 'new_m_conv_w', 'new_m_conv_b', 'new_m_conv_ln_g', 'new_m_conv_ln_b', 'new_m_w_pw2', 'new_m_b_pw2', 'new_m_w_out', 'new_v_meta_tokens', 'new_v_pre_norm_g', 'new_v_post_norm_g', 'new_v_w_in', 'new_v_conv_w', 'new_v_conv_b', 'new_v_conv_ln_g', 'new_v_conv_ln_b', 'new_v_w_pw2', 'new_v_b_pw2', 'new_v_w_out']
TWIN_LEAF_KINDS = {'loss': 'loss', 'grad_x': 'grad_x', 'grad_meta_tokens': 'grad_w', 'grad_pre_norm_g': 'grad_w', 'grad_post_norm_g': 'grad_w', 'grad_w_in': 'grad_w', 'grad_conv_w': 'grad_w', 'grad_conv_b': 'grad_w', 'grad_conv_ln_g': 'grad_w', 'grad_conv_ln_b': 'grad_w', 'grad_w_pw2': 'grad_w', 'grad_b_pw2': 'grad_w', 'grad_w_out': 'grad_w', 'delta_meta_tokens': 'delta_w', 'delta_pre_norm_g': 'delta_w', 'delta_post_norm_g': 'delta_w', 'delta_w_in': 'delta_w', 'delta_conv_w': 'delta_w', 'delta_conv_b': 'delta_w', 'delta_conv_ln_g': 'delta_w', 'delta_conv_ln_b': 'delta_w', 'delta_w_pw2': 'delta_w', 'delta_b_pw2': 'delta_w', 'delta_w_out': 'delta_w', 'new_m_meta_tokens': 'new_m', 'new_m_pre_norm_g': 'new_m', 'new_m_post_norm_g': 'new_m', 'new_m_w_in': 'new_m', 'new_m_conv_w': 'new_m', 'new_m_conv_b': 'new_m', 'new_m_conv_ln_g': 'new_m', 'new_m_conv_ln_b': 'new_m', 'new_m_w_pw2': 'new_m', 'new_m_b_pw2': 'new_m', 'new_m_w_out': 'new_m', 'new_v_meta_tokens': 'new_v', 'new_v_pre_norm_g': 'new_v', 'new_v_post_norm_g': 'new_v', 'new_v_w_in': 'new_v', 'new_v_conv_w': 'new_v', 'new_v_conv_b': 'new_v', 'new_v_conv_ln_g': 'new_v', 'new_v_conv_ln_b': 'new_v', 'new_v_w_pw2': 'new_v', 'new_v_b_pw2': 'new_v', 'new_v_w_out': 'new_v'}


def _forward(args):
    return _fwd_reference(*[args[k] for k in FWD_PARAMS])


def _output_shape():
    def fwd():
        inp = _fwd_setup_inputs(0)
        return _fwd_reference(*[inp[k] for k in FWD_PARAMS])
    out = _jax.eval_shape(fwd)
    return out.shape, out.dtype

N_MICROBATCH = 1
ADAM_LR = 0.001
ADAM_B1 = 0.9
ADAM_B2 = 0.999
ADAM_EPS = 1e-08
ADAM_WD = 0.01
ADAM_STEP = 10
PER_EXAMPLE_BATCH_AXIS = {'x': 0, 'loss_target': 0}
SHARED_INPUTS = []
_WEIGHT_DTYPES = {'meta_tokens': _jnp.float32, 'pre_norm_g': _jnp.float32, 'post_norm_g': _jnp.float32, 'w_in': _jnp.float32, 'conv_w': _jnp.float32, 'conv_b': _jnp.float32, 'conv_ln_g': _jnp.float32, 'conv_ln_b': _jnp.float32, 'w_pw2': _jnp.float32, 'b_pw2': _jnp.float32, 'w_out': _jnp.float32}
MOMENT_SCALE = {'meta_tokens': 8.123068e-02, 'pre_norm_g': 1.086757e+00, 'post_norm_g': 3.196771e+01, 'w_in': 5.869759e-01, 'conv_w': 6.992002e-01, 'conv_b': 3.202279e+00, 'conv_ln_g': 1.304474e+00, 'conv_ln_b': 1.779560e+00, 'w_pw2': 8.993278e-01, 'b_pw2': 3.621994e+00, 'w_out': 8.101478e-01}


def _to_microbatches(a, axis):
    t = _jnp.moveaxis(a, axis, 0)
    t = t.reshape((N_MICROBATCH, t.shape[0] // N_MICROBATCH) + t.shape[1:])
    return _jnp.moveaxis(t, 1, axis + 1)


def setup_inputs(seed: int = 0) -> dict:
    inp = _fwd_setup_inputs(seed)
    key = _jax.random.fold_in(_jax.random.key(seed), 7919)
    shape, _ = _output_shape()
    out = dict(inp)
    out["loss_target"] = _jax.random.normal(_jax.random.fold_in(key, 0), shape, _jnp.float32)
    for i, name in enumerate(TWIN_WEIGHTS):
        w = inp[name].astype(_jnp.float32)
        if MOMENT_SCALE is None:
            s = _jnp.sqrt(_jnp.mean(_jnp.square(w)) + 1e-30)
        else:
            s = MOMENT_SCALE[name]
        km, kv = _jax.random.split(_jax.random.fold_in(key, i + 1))
        out[name] = w
        out["m_" + name] = s * _jax.random.normal(km, w.shape, _jnp.float32)
        out["v_" + name] = (s * s) * _jax.random.uniform(kv, w.shape, _jnp.float32, 0.5, 1.5)
    if N_MICROBATCH > 1:
        for name, axis in PER_EXAMPLE_BATCH_AXIS.items():
            out[name] = _to_microbatches(out[name], axis)
    return {'x': out['x'], 'meta_tokens': out['meta_tokens'], 'pre_norm_g': out['pre_norm_g'], 'post_norm_g': out['post_norm_g'], 'w_in': out['w_in'], 'conv_w': out['conv_w'], 'conv_b': out['conv_b'], 'conv_ln_g': out['conv_ln_g'], 'conv_ln_b': out['conv_ln_b'], 'w_pw2': out['w_pw2'], 'b_pw2': out['b_pw2'], 'w_out': out['w_out'], 'loss_target': out['loss_target'], 'm_meta_tokens': out['m_meta_tokens'], 'm_pre_norm_g': out['m_pre_norm_g'], 'm_post_norm_g': out['m_post_norm_g'], 'm_w_in': out['m_w_in'], 'm_conv_w': out['m_conv_w'], 'm_conv_b': out['m_conv_b'], 'm_conv_ln_g': out['m_conv_ln_g'], 'm_conv_ln_b': out['m_conv_ln_b'], 'm_w_pw2': out['m_w_pw2'], 'm_b_pw2': out['m_b_pw2'], 'm_w_out': out['m_w_out'], 'v_meta_tokens': out['v_meta_tokens'], 'v_pre_norm_g': out['v_pre_norm_g'], 'v_post_norm_g': out['v_post_norm_g'], 'v_w_in': out['v_w_in'], 'v_conv_w': out['v_conv_w'], 'v_conv_b': out['v_conv_b'], 'v_conv_ln_g': out['v_conv_ln_g'], 'v_conv_ln_b': out['v_conv_ln_b'], 'v_w_pw2': out['v_w_pw2'], 'v_b_pw2': out['v_b_pw2'], 'v_w_out': out['v_w_out']}


def _loss(weights, diff, rest, loss_target):
    with _jax.named_scope("forward"):
        args = {**rest, TWIN_DIFF_INPUT: diff, **{k: w.astype(_WEIGHT_DTYPES[k]) for k, w in weights.items()}}
        y = _forward(args)
    with _jax.named_scope("loss_head"):
        err = _jnp.square(y.astype(_jnp.float32) - loss_target)
        return 0.5 * _jnp.sum(_jnp.mean(err, axis=-1)) if err.ndim else 0.5 * err


def _adamw(w, g, m, v):
    m = ADAM_B1 * m + (1.0 - ADAM_B1) * g
    v = ADAM_B2 * v + (1.0 - ADAM_B2) * _jnp.square(g)
    m_hat = m / (1.0 - ADAM_B1 ** ADAM_STEP)
    v_hat = v / (1.0 - ADAM_B2 ** ADAM_STEP)
    delta = -ADAM_LR * (m_hat / (_jnp.sqrt(v_hat) + ADAM_EPS) + ADAM_WD * w)
    return delta, m, v


def reference(x, meta_tokens, pre_norm_g, post_norm_g, w_in, conv_w, conv_b, conv_ln_g, conv_ln_b, w_pw2, b_pw2, w_out, loss_target, m_meta_tokens, m_pre_norm_g, m_post_norm_g, m_w_in, m_conv_w, m_conv_b, m_conv_ln_g, m_conv_ln_b, m_w_pw2, m_b_pw2, m_w_out, v_meta_tokens, v_pre_norm_g, v_post_norm_g, v_w_in, v_conv_w, v_conv_b, v_conv_ln_g, v_conv_ln_b, v_w_pw2, v_b_pw2, v_w_out):
    given = dict(x=x, meta_tokens=meta_tokens, pre_norm_g=pre_norm_g, post_norm_g=post_norm_g, w_in=w_in, conv_w=conv_w, conv_b=conv_b, conv_ln_g=conv_ln_g, conv_ln_b=conv_ln_b, w_pw2=w_pw2, b_pw2=b_pw2, w_out=w_out, loss_target=loss_target, m_meta_tokens=m_meta_tokens, m_pre_norm_g=m_pre_norm_g, m_post_norm_g=m_post_norm_g, m_w_in=m_w_in, m_conv_w=m_conv_w, m_conv_b=m_conv_b, m_conv_ln_g=m_conv_ln_g, m_conv_ln_b=m_conv_ln_b, m_w_pw2=m_w_pw2, m_b_pw2=m_b_pw2, m_w_out=m_w_out, v_meta_tokens=v_meta_tokens, v_pre_norm_g=v_pre_norm_g, v_post_norm_g=v_post_norm_g, v_w_in=v_w_in, v_conv_w=v_conv_w, v_conv_b=v_conv_b, v_conv_ln_g=v_conv_ln_g, v_conv_ln_b=v_conv_ln_b, v_w_pw2=v_w_pw2, v_b_pw2=v_b_pw2, v_w_out=v_w_out)
    weights = {n: given[n] for n in TWIN_WEIGHTS}
    shared = {n: given[n] for n in SHARED_INPUTS}
    per_example = {n: given[n] for n in ['x']}
    grad_fn = _jax.value_and_grad(_loss, argnums=(0, 1))

    def one_microbatch(ex, loss_target):
        ex = dict(ex)
        diff = ex.pop(TWIN_DIFF_INPUT)
        return grad_fn(weights, diff, {**shared, **ex}, loss_target)

    if N_MICROBATCH == 1:
        loss, (grad_w, grad_x) = one_microbatch(per_example, given["loss_target"])
    else:
        def body(carry, xs):
            loss_sum, grad_sum = carry
            l_k, (gw_k, gx_k) = one_microbatch(xs[0], xs[1])
            with _jax.named_scope("update"):
                return (loss_sum + l_k, _jax.tree.map(_jnp.add, grad_sum, gw_k)), gx_k

        init = (_jnp.zeros((), _jnp.float32), _jax.tree.map(_jnp.zeros_like, weights))
        (loss, grad_w), grad_x = _jax.lax.scan(body, init, (per_example, given["loss_target"]))
    with _jax.named_scope("update"):
        delta_w, new_m, new_v = {}, {}, {}
        for n in TWIN_WEIGHTS:
            delta_w[n], new_m[n], new_v[n] = _adamw(weights[n], grad_w[n], given["m_" + n], given["v_" + n])
    return (loss, grad_x, *[grad_w[n] for n in TWIN_WEIGHTS], *[delta_w[n] for n in TWIN_WEIGHTS],
            *[new_m[n] for n in TWIN_WEIGHTS], *[new_v[n] for n in TWIN_WEIGHTS])
```

```python
import functools

import jax
import jax.numpy as jnp
from jax import lax
from jax.experimental import pallas as pl
from jax.experimental.pallas import tpu as pltpu

F32 = jnp.float32
BF16 = jnp.bfloat16

N_META = 16
D_CONV = 512
D_SB = 512
HEAD_DIM = 64
CONV_WIDTH = 31
CONV_PAD = 32
RMS_EPS = 1e-6
LN_EPS = 1e-5
Q_SCALE = HEAD_DIM ** -0.5

ADAM_LR = 0.001
ADAM_B1 = 0.9
ADAM_B2 = 0.999
ADAM_EPS = 1e-08
ADAM_WD = 0.01
ADAM_STEP = 10

LANES = 128
ROW_TILE = 256
ATT_BLOCK = 256
VMEM_LIMIT = 56 * 1024 * 1024

MESH = pl.DeviceIdType.MESH


def _pcall(body, **kw):
    return pl.pallas_call(body, **kw)


def _params(*sem):
    return pltpu.CompilerParams(dimension_semantics=sem, vmem_limit_bytes=VMEM_LIMIT)


def _sigmoid(x):
    return 1.0 / (1.0 + jnp.exp(-x))


def _silu_fwd_bwd(x):
    s = _sigmoid(x)
    return x * s, s * (1.0 + x * (1.0 - s))


def _nt(a, b):
    return lax.dot_general(a, b, (((1,), (1,)), ((), ())), preferred_element_type=F32)


def _tn(a, b):
    return lax.dot_general(a, b, (((0,), (0,)), ((), ())), preferred_element_type=F32)


def _nn(a, b):
    return jnp.dot(a, b, preferred_element_type=F32)


def _inproj(h, g, w):
    lp, d = h.shape
    n = w.shape[1]

    def body(h_ref, g_ref, w_ref, ew_ref, qkv_ref):
        x = h_ref[...]
        rstd = lax.rsqrt(jnp.mean(x * x, axis=-1, keepdims=True) + RMS_EPS)
        u = ((x * rstd) * g_ref[...]).astype(BF16)
        p = _nn(u, w_ref[...])
        ew_ref[:, 0:1536] = p[:, 0:1536]
        ew_ref[:, 1536:2048] = p[:, 3072:3584]
        qkv_ref[:, 0:512] = (p[:, 1536:2048] * Q_SCALE).astype(BF16)
        qkv_ref[:, 512:1536] = p[:, 2048:3072].astype(BF16)

    return _pcall(
        body, name="inproj_fwd", grid=(lp // ROW_TILE,),
        in_specs=[pl.BlockSpec((ROW_TILE, d), lambda i: (i, 0)),
                  pl.BlockSpec((1, d), lambda i: (0, 0)),
                  pl.BlockSpec((d, n), lambda i: (0, 0))],
        out_specs=[pl.BlockSpec((ROW_TILE, 2048), lambda i: (i, 0)),
                   pl.BlockSpec((ROW_TILE, 1536), lambda i: (i, 0))],
        out_shape=[jax.ShapeDtypeStruct((lp, 2048), F32), jax.ShapeDtypeStruct((lp, 1536), BF16)],
        compiler_params=_params("parallel"),
    )(h, g, w)


def _layer_norm_stats(c1):
    mu = jnp.mean(c1, axis=-1, keepdims=True)
    xc = c1 - mu
    var = jnp.mean(xc * xc, axis=-1, keepdims=True)
    rstd = lax.rsqrt(var + LN_EPS)
    return xc * rstd, rstd


def _conv_fwd(ew, cw, cb, lng, lnb, wpw2, bpw2):
    lp = ew.shape[0]
    tm = ROW_TILE

    def body(ew_ref, cw_ref, cb_ref, lng_ref, lnb_ref, w_ref, b_ref, c1_ref, c4_ref, c5_ref, xbuf):
        @pl.when(pl.program_id(0) == 0)
        def _():
            xbuf[0:CONV_PAD, :] = jnp.zeros((CONV_PAD, D_CONV), F32)

        ga = ew_ref[:, 0:512]
        gb = ew_ref[:, 512:1024]
        cg = ew_ref[:, 1024:1536]
        xbuf[CONV_PAD:CONV_PAD + tm, :] = ga * _sigmoid(gb)
        acc = jnp.zeros((tm, D_CONV), F32) + cb_ref[...]
        for j in range(CONV_WIDTH):
            acc = acc + cw_ref[j:j + 1, :] * xbuf[pl.ds(CONV_PAD - (CONV_WIDTH - 1) + j, tm), :]
        c1_ref[...] = acc
        xbuf[0:CONV_PAD, :] = xbuf[tm:tm + CONV_PAD, :]
        xhat, _ = _layer_norm_stats(acc)
        c2 = xhat * lng_ref[...] + lnb_ref[...]
        c3 = c2 * _sigmoid(c2)
        c4 = _nn(c3.astype(BF16), w_ref[...]) + b_ref[...]
        c4_ref[...] = c4
        c5_ref[...] = (c4 * (cg * _sigmoid(cg))).astype(BF16)

    vec = pl.BlockSpec((1, D_CONV), lambda i: (0, 0))
    row = pl.BlockSpec((tm, D_CONV), lambda i: (i, 0))
    return _pcall(
        body, name="conv_fwd", grid=(lp // tm,),
        in_specs=[pl.BlockSpec((tm, 1536), lambda i: (i, 0)),
                  pl.BlockSpec((CONV_PAD, D_CONV), lambda i: (0, 0)),
                  vec, vec, vec,
                  pl.BlockSpec((D_CONV, D_CONV), lambda i: (0, 0)),
                  vec],
        out_specs=[row, row, row],
        out_shape=[jax.ShapeDtypeStruct((lp, D_CONV), F32), jax.ShapeDtypeStruct((lp, D_CONV), F32),
                   jax.ShapeDtypeStruct((lp, D_CONV), BF16)],
        scratch_shapes=[pltpu.VMEM((tm + CONV_PAD, D_CONV), F32)],
        compiler_params=_params("arbitrary"),
    )(ew, cw, cb, lng, lnb, wpw2, bpw2)


def _sb_block(qh, kj, z_mask, tri):
    z = _nt(qh, kj)
    ls = -(jnp.maximum(z, 0.0) + jnp.log(1.0 + jnp.exp(-jnp.abs(z))))
    if z_mask is not None:
        ls = jnp.where(z_mask, ls, 0.0)
    hi = ls.astype(BF16)
    lo = (ls - hi.astype(F32)).astype(BF16)
    tail = _nn(hi, tri) + _nn(lo, tri)
    return z, ls, tail


def _attn_masks():
    lane = lax.broadcasted_iota(jnp.int32, (1, LANES), 1)
    row = lax.broadcasted_iota(jnp.int32, (ATT_BLOCK, ATT_BLOCK), 0)
    col = lax.broadcasted_iota(jnp.int32, (ATT_BLOCK, ATT_BLOCK), 1)
    return lane < HEAD_DIM, col < row


def _attn_fwd(qkv, tri):
    lp = qkv.shape[0]
    bq = ATT_BLOCK

    assert lp // bq <= LANES

    def body(q_ref, k_ref, v_ref, tri_ref, o_ref, carry_ref):
        i = pl.program_id(1)
        first_head, vis = _attn_masks()
        lane = lax.broadcasted_iota(jnp.int32, (1, LANES), 1)
        q = q_ref[...]
        tri_m = tri_ref[...]

        def block(qh, j, c, acc, cmat, mask):
            off = pl.multiple_of(j * bq, bq)
            kj = k_ref[pl.ds(off, bq), :]
            vj = v_ref[pl.ds(off, bq), :]
            z, ls, tail = _sb_block(qh, kj, mask, tri_m)
            a = jnp.exp(z + ls + tail + c)
            if mask is not None:
                a = jnp.where(mask, a, 0.0)
            acc = acc + _nn(a.astype(BF16), vj)
            cmat = jnp.where(lane == j, c, cmat)
            return c + tail[:, 0:1] + ls[:, 0:1], acc, cmat

        outs = []
        for hd, sel in enumerate((first_head, jnp.logical_not(first_head))):
            qh = jnp.where(sel, q, jnp.zeros_like(q))
            zeros = jnp.zeros((bq, LANES), F32)
            state = block(qh, i, jnp.zeros((bq, 1), F32), zeros, zeros, vis)
            state = lax.fori_loop(0, i, lambda jj, st: block(qh, i - 1 - jj, st[0], st[1], st[2], None), state)
            outs.append(state[1])
            carry_ref[:, hd * LANES:(hd + 1) * LANES] = state[2]
        o_ref[...] = jnp.where(first_head, outs[0], outs[1])

    npair = D_SB // LANES
    return _pcall(
        body, name="attn_fwd", grid=(npair, lp // bq),
        in_specs=[pl.BlockSpec((bq, LANES), lambda p, i: (i, p)),
                  pl.BlockSpec((lp, LANES), lambda p, i: (0, npair + p)),
                  pl.BlockSpec((lp, LANES), lambda p, i: (0, 2 * npair + p)),
                  pl.BlockSpec((bq, bq), lambda p, i: (0, 0))],
        out_specs=[pl.BlockSpec((bq, LANES), lambda p, i: (i, p)),
                   pl.BlockSpec((bq, 2 * LANES), lambda p, i: (i, p))],
        out_shape=[jax.ShapeDtypeStruct((lp, D_SB), F32), jax.ShapeDtypeStruct((lp, 2 * D_SB), F32)],
        compiler_params=_params("parallel", "parallel"),
    )(qkv, qkv, qkv, tri)


def _outproj(c5, att, ew, h, w, g):
    lp, d = h.shape
    tm = ROW_TILE

    def body(c5_ref, att_ref, sg_ref, h_ref, w_ref, g_ref, hn_ref, cat_ref, mix_ref):
        sg = sg_ref[...]
        s = att_ref[...] * (sg * _sigmoid(sg))
        cat_ref[:, 0:D_CONV] = c5_ref[...]
        cat_ref[:, D_CONV:] = s.astype(BF16)
        mixed = _nn(cat_ref[...], w_ref[...])
        mix_ref[...] = mixed
        rstd = lax.rsqrt(jnp.mean(mixed * mixed, axis=-1, keepdims=True) + RMS_EPS)
        hn_ref[...] = h_ref[...] + (mixed * rstd) * g_ref[...]

    half = pl.BlockSpec((tm, 512), lambda i: (i, 0))
    full = pl.BlockSpec((tm, d), lambda i: (i, 0))
    return _pcall(
        body, name="outproj_fwd", grid=(lp // tm,),
        in_specs=[half, half, pl.BlockSpec((tm, 512), lambda i: (i, 3)), full,
                  pl.BlockSpec((d, d), lambda i: (0, 0)), pl.BlockSpec((1, d), lambda i: (0, 0))],
        out_specs=[full, full, full],
        out_shape=[jax.ShapeDtypeStruct((lp, d), F32), jax.ShapeDtypeStruct((lp, d), BF16),
                   jax.ShapeDtypeStruct((lp, d), F32)],
        compiler_params=_params("parallel"),
    )(c5, att, ew, h, w, g)


def _loss_head(h, target, seq):
    lp, d = h.shape
    tm = ROW_TILE

    def body(h_ref, t_ref, dh_ref, loss_ref):
        i = pl.program_id(0)

        @pl.when(i == 0)
        def _():
            loss_ref[...] = jnp.zeros_like(loss_ref)

        row = i * tm + lax.broadcasted_iota(jnp.int32, (tm, 1), 0)
        real = jnp.logical_and(row >= N_META, row < N_META + seq)
        diff = jnp.where(real, h_ref[...] - t_ref[...], 0.0)
        dh_ref[...] = diff * (1.0 / d)
        loss_ref[...] += 0.5 * jnp.sum(jnp.sum(diff * diff, axis=-1, keepdims=True) * (1.0 / d))

    full = pl.BlockSpec((tm, d), lambda i: (i, 0))
    return _pcall(
        body, name="loss_head", grid=(lp // tm,),
        in_specs=[full, full],
        out_specs=[full, pl.BlockSpec((8, LANES), lambda i: (0, 0))],
        out_shape=[jax.ShapeDtypeStruct((lp, d), F32), jax.ShapeDtypeStruct((8, LANES), F32)],
        compiler_params=_params("arbitrary"),
    )(h, target)


def _outproj_bwd(dh, mixed, g, w, att, ew, c4):
    lp, d = dh.shape
    tm = ROW_TILE

    def body(dh_ref, mix_ref, g_ref, w_ref, att_ref, cg_ref, sg_ref, c4_ref,
             dmix_ref, datt_ref, dsg_ref, dc4_ref, dcg_ref, dg_ref, db_ref):
        @pl.when(pl.program_id(0) == 0)
        def _():
            dg_ref[...] = jnp.zeros_like(dg_ref)
            db_ref[...] = jnp.zeros_like(db_ref)

        mixed = mix_ref[...]
        dhv = dh_ref[...]
        rstd = lax.rsqrt(jnp.mean(mixed * mixed, axis=-1, keepdims=True) + RMS_EPS)
        n = mixed * rstd
        dg_ref[...] += jnp.sum(dhv * n, axis=0, keepdims=True)
        dn = dhv * g_ref[...]
        dmix = (rstd * (dn - n * jnp.mean(dn * n, axis=-1, keepdims=True))).astype(BF16)
        dmix_ref[...] = dmix
        dcat = _nt(dmix, w_ref[...])
        dc5 = dcat[:, 0:D_CONV]
        ds = dcat[:, D_CONV:]
        silu_sg, dsilu_sg = _silu_fwd_bwd(sg_ref[...])
        datt_ref[...] = (ds * silu_sg).astype(BF16)
        dsg_ref[...] = (ds * att_ref[...] * dsilu_sg).astype(BF16)
        silu_cg, dsilu_cg = _silu_fwd_bwd(cg_ref[...])
        dc4 = dc5 * silu_cg
        db_ref[...] += jnp.sum(dc4, axis=0, keepdims=True)
        dc4_ref[...] = dc4.astype(BF16)
        dcg_ref[...] = (dc5 * c4_ref[...] * dsilu_cg).astype(BF16)

    half = pl.BlockSpec((tm, 512), lambda i: (i, 0))
    full = pl.BlockSpec((tm, d), lambda i: (i, 0))
    hb = jax.ShapeDtypeStruct((lp, 512), BF16)
    return _pcall(
        body, name="outproj_bwd", grid=(lp // tm,),
        in_specs=[full, full, pl.BlockSpec((1, d), lambda i: (0, 0)), pl.BlockSpec((d, d), lambda i: (0, 0)),
                  half, pl.BlockSpec((tm, 512), lambda i: (i, 2)), pl.BlockSpec((tm, 512), lambda i: (i, 3)), half],
        out_specs=[full, half, half, half, half,
                   pl.BlockSpec((1, d), lambda i: (0, 0)), pl.BlockSpec((1, 512), lambda i: (0, 0))],
        out_shape=[jax.ShapeDtypeStruct((lp, d), BF16), hb, hb, hb, hb,
                   jax.ShapeDtypeStruct((1, d), F32), jax.ShapeDtypeStruct((1, 512), F32)],
        compiler_params=_params("arbitrary"),
    )(dh, mixed, g, w, att, ew, ew, c4)


def _attn_bwd(qkv, carries, datt, tri, upper):
    lp = qkv.shape[0]
    bq = ATT_BLOCK

    def body(q_ref, k_ref, v_ref, carry_ref, do_ref, tri_ref, upper_ref, dq_ref, dk_ref, dv_ref):
        i = pl.program_id(1)

        @pl.when(i == 0)
        def _():
            dk_ref[...] = jnp.zeros_like(dk_ref)
            dv_ref[...] = jnp.zeros_like(dv_ref)

        first_head, vis = _attn_masks()
        lane = lax.broadcasted_iota(jnp.int32, (1, LANES), 1)
        q = q_ref[...]
        do = do_ref[...]
        tri_m = tri_ref[...]
        upper_m = upper_ref[...]

        def block(qh, doh, cmat, j, run, dq, mask):
            off = pl.multiple_of(j * bq, bq)
            kj = k_ref[pl.ds(off, bq), :]
            vj = v_ref[pl.ds(off, bq), :]
            c = jnp.sum(jnp.where(lane == j, cmat, 0.0), axis=-1, keepdims=True)
            z, ls, tail = _sb_block(qh, kj, mask, tri_m)
            a = jnp.exp(z + ls + tail + c)
            if mask is not None:
                a = jnp.where(mask, a, 0.0)
            de = _nt(doh, vj) * a
            hi = de.astype(BF16)
            lo = (de - hi.astype(F32)).astype(BF16)
            prefix = _nn(hi, upper_m) + _nn(lo, upper_m)
            beta = jnp.exp(z + ls)
            dz = de - beta * (de + run + prefix)
            if mask is not None:
                dz = jnp.where(mask, dz, 0.0)
            dzb = dz.astype(BF16)
            dq = dq + _nn(dzb, kj)
            dk_ref[pl.ds(off, bq), :] += _tn(dzb, qh)
            dv_ref[pl.ds(off, bq), :] += _tn(a.astype(BF16), doh)
            return run + prefix[:, bq - 1:bq] + de[:, bq - 1:bq], dq

        outs = []
        for hd, sel in enumerate((first_head, jnp.logical_not(first_head))):
            qh = jnp.where(sel, q, jnp.zeros_like(q))
            doh = jnp.where(sel, do, jnp.zeros_like(do))
            cmat = carry_ref[:, hd * LANES:(hd + 1) * LANES]
            state = (jnp.zeros((bq, 1), F32), jnp.zeros((bq, LANES), F32))
            state = lax.fori_loop(0, i, lambda j, st: block(qh, doh, cmat, j, st[0], st[1], None), state)
            state = block(qh, doh, cmat, i, state[0], state[1], vis)
            outs.append(state[1])
        dq_ref[...] = (jnp.where(first_head, outs[0], outs[1]) * Q_SCALE).astype(BF16)

    npair = D_SB // LANES
    qb = pl.BlockSpec((bq, LANES), lambda p, i: (i, p))
    colb = pl.BlockSpec((lp, LANES), lambda p, i: (0, p))
    sq = pl.BlockSpec((bq, bq), lambda p, i: (0, 0))
    return _pcall(
        body, name="attn_bwd", grid=(npair, lp // bq),
        in_specs=[qb,
                  pl.BlockSpec((lp, LANES), lambda p, i: (0, npair + p)),
                  pl.BlockSpec((lp, LANES), lambda p, i: (0, 2 * npair + p)),
                  pl.BlockSpec((bq, 2 * LANES), lambda p, i: (i, p)), qb, sq, sq],
        out_specs=[qb, colb, colb],
        out_shape=[jax.ShapeDtypeStruct((lp, D_SB), BF16), jax.ShapeDtypeStruct((lp, D_SB), F32),
                   jax.ShapeDtypeStruct((lp, D_SB), F32)],
        compiler_params=_params("parallel", "arbitrary"),
    )(qkv, qkv, qkv, carries, datt, tri, upper)


def _conv_bwd(dc4, c1, ew, cw, lng, lnb, wpw2):
    lp = ew.shape[0]
    tm = ROW_TILE
    nt = lp // tm
    halo_per_tile = tm // CONV_PAD

    def body(dc4_ref, c1_ref, ew_ref, halo_ref, cw_ref, lng_ref, lnb_ref, w_ref,
             dga_ref, dgb_ref, c3_ref, dcw_ref, dcb_ref, dlng_ref, dlnb_ref, xbuf, dbuf):
        step = pl.program_id(0)

        @pl.when(step == 0)
        def _():
            dcw_ref[...] = jnp.zeros_like(dcw_ref)
            dcb_ref[...] = jnp.zeros_like(dcb_ref)
            dlng_ref[...] = jnp.zeros_like(dlng_ref)
            dlnb_ref[...] = jnp.zeros_like(dlnb_ref)
            dbuf[tm:tm + CONV_PAD, :] = jnp.zeros((CONV_PAD, D_CONV), F32)

        dc3 = _nt(dc4_ref[...], w_ref[...])
        xhat, rstd = _layer_norm_stats(c1_ref[...])
        c2 = xhat * lng_ref[...] + lnb_ref[...]
        c3, dsilu = _silu_fwd_bwd(c2)
        c3_ref[...] = c3.astype(BF16)
        dc2 = dc3 * dsilu
        dlng_ref[...] += jnp.sum(dc2 * xhat, axis=0, keepdims=True)
        dlnb_ref[...] += jnp.sum(dc2, axis=0, keepdims=True)
        dxhat = dc2 * lng_ref[...]
        dc1 = rstd * (dxhat - jnp.mean(dxhat, axis=-1, keepdims=True)
                      - xhat * jnp.mean(dxhat * xhat, axis=-1, keepdims=True))
        dcb_ref[...] += jnp.sum(dc1, axis=0, keepdims=True)
        dbuf[0:tm, :] = dc1

        ga = ew_ref[:, 0:512]
        sgb = _sigmoid(ew_ref[:, 512:1024])
        xbuf[CONV_PAD:CONV_PAD + tm, :] = ga * sgb
        first_tile = step == nt - 1
        halo = halo_ref[:, 0:512] * _sigmoid(halo_ref[:, 512:1024])
        xbuf[0:CONV_PAD, :] = jnp.where(first_tile, 0.0, halo)

        dc0 = jnp.zeros((tm, D_CONV), F32)
        for j in range(CONV_WIDTH):
            dc0 = dc0 + cw_ref[j:j + 1, :] * dbuf[pl.ds(CONV_WIDTH - 1 - j, tm), :]
            tap = dc1 * xbuf[pl.ds(CONV_PAD - (CONV_WIDTH - 1) + j, tm), :]
            dcw_ref[j:j + 1, :] += jnp.sum(tap, axis=0, keepdims=True)
        dbuf[tm:tm + CONV_PAD, :] = dbuf[0:CONV_PAD, :]
        dga_ref[...] = (dc0 * sgb).astype(BF16)
        dgb_ref[...] = (dc0 * ga * sgb * (1.0 - sgb)).astype(BF16)

    rev = lambda i: (nt - 1 - i, 0)
    row = pl.BlockSpec((tm, D_CONV), rev)
    vec = pl.BlockSpec((1, D_CONV), lambda i: (0, 0))
    hb = jax.ShapeDtypeStruct((lp, D_CONV), BF16)
    vs = jax.ShapeDtypeStruct((1, D_CONV), F32)
    return _pcall(
        body, name="conv_bwd", grid=(nt,),
        in_specs=[row, row, pl.BlockSpec((tm, 1024), rev),
                  pl.BlockSpec((CONV_PAD, 1024), lambda i: (jnp.maximum((nt - 1 - i) * halo_per_tile - 1, 0), 0)),
                  pl.BlockSpec((CONV_PAD, D_CONV), lambda i: (0, 0)), vec, vec,
                  pl.BlockSpec((D_CONV, D_CONV), lambda i: (0, 0))],
        out_specs=[row, row, row, pl.BlockSpec((CONV_PAD, D_CONV), lambda i: (0, 0)), vec, vec, vec],
        out_shape=[hb, hb, hb, jax.ShapeDtypeStruct((CONV_PAD, D_CONV), F32), vs, vs, vs],
        scratch_shapes=[pltpu.VMEM((tm + CONV_PAD, D_CONV), F32), pltpu.VMEM((tm + CONV_PAD, D_CONV), F32)],
        compiler_params=_params("arbitrary"),
    )(dc4, c1, ew, ew, cw, lng, lnb, wpw2)


def _inproj_bwd(dga, dgb, dcg, dq, dk, dv, dsg, h, g, w, dh_out):
    lp, d = h.shape
    n = w.shape[1]
    tm = ROW_TILE

    def body(dga_ref, dgb_ref, dcg_ref, dq_ref, dk_ref, dv_ref, dsg_ref, h_ref, g_ref, w_ref, dho_ref,
             dh_ref, dproj_ref, u_ref, dg_ref):
        @pl.when(pl.program_id(0) == 0)
        def _():
            dg_ref[...] = jnp.zeros_like(dg_ref)

        dproj_ref[:, 0:512] = dga_ref[...]
        dproj_ref[:, 512:1024] = dgb_ref[...]
        dproj_ref[:, 1024:1536] = dcg_ref[...]
        dproj_ref[:, 1536:2048] = dq_ref[...]
        dproj_ref[:, 2048:2560] = dk_ref[...].astype(BF16)
        dproj_ref[:, 2560:3072] = dv_ref[...].astype(BF16)
        dproj_ref[:, 3072:3584] = dsg_ref[...]
        du = _nt(dproj_ref[...], w_ref[...])
        x = h_ref[...]
        rstd = lax.rsqrt(jnp.mean(x * x, axis=-1, keepdims=True) + RMS_EPS)
        nrm = x * rstd
        u_ref[...] = (nrm * g_ref[...]).astype(BF16)
        dg_ref[...] += jnp.sum(du * nrm, axis=0, keepdims=True)
        dn = du * g_ref[...]
        dh_ref[...] = dho_ref[...] + rstd * (dn - nrm * jnp.mean(dn * nrm, axis=-1, keepdims=True))

    half = pl.BlockSpec((tm, 512), lambda i: (i, 0))
    full = pl.BlockSpec((tm, d), lambda i: (i, 0))
    return _pcall(
        body, name="inproj_bwd", grid=(lp // tm,),
        in_specs=[half] * 7 + [full, pl.BlockSpec((1, d), lambda i: (0, 0)),
                               pl.BlockSpec((d, n), lambda i: (0, 0)), full],
        out_specs=[full, pl.BlockSpec((tm, n), lambda i: (i, 0)), full, pl.BlockSpec((1, d), lambda i: (0, 0))],
        out_shape=[jax.ShapeDtypeStruct((lp, d), F32), jax.ShapeDtypeStruct((lp, n), BF16),
                   jax.ShapeDtypeStruct((lp, d), BF16), jax.ShapeDtypeStruct((1, d), F32)],
        compiler_params=_params("arbitrary"),
    )(dga, dgb, dcg, dq, dk, dv, dsg, h, g, w, dh_out)


def _row_split(m, parts):
    tm = m // parts
    assert tm * parts == m and tm % 16 == 0, (m, parts)
    return tm


def _matmul_tn(x, dy, tn, name):
    m, k = x.shape
    n = dy.shape[1]
    tm = _row_split(m, 4 if m % 64 == 0 else 1)

    def body(x_ref, dy_ref, o_ref):
        @pl.when(pl.program_id(1) == 0)
        def _():
            o_ref[...] = jnp.zeros_like(o_ref)

        o_ref[...] += _tn(x_ref[...], dy_ref[...])

    return _pcall(
        body, name=name, grid=(n // tn, m // tm),
        in_specs=[pl.BlockSpec((tm, k), lambda j, r: (r, 0)), pl.BlockSpec((tm, tn), lambda j, r: (r, j))],
        out_specs=pl.BlockSpec((k, tn), lambda j, r: (0, j)),
        out_shape=jax.ShapeDtypeStruct((k, n), F32),
        compiler_params=_params("parallel", "arbitrary"),
    )(x, dy)


def _local_step(h0, target_p, seq, pre_g, post_g, w_in, conv_w, conv_b, ln_g, ln_b, w_pw2, b_pw2, w_out):
    depth = w_in.shape[0]
    ar = jnp.arange(ATT_BLOCK)
    tri = (ar[:, None] > ar[None, :]).astype(BF16)
    upper = (ar[:, None] < ar[None, :]).astype(BF16)
    row = lambda a, l: a[l][None, :]

    saved = []
    h = h0
    for l in range(depth):
        ew, qkv = _inproj(h, row(pre_g, l), w_in[l])
        c1, c4, c5 = _conv_fwd(ew, conv_w[l], row(conv_b, l), row(ln_g, l), row(ln_b, l), w_pw2[l], row(b_pw2, l))
        att, carries = _attn_fwd(qkv, tri)
        hn, cat, mixed = _outproj(c5, att, ew, h, w_out[l], row(post_g, l))
        saved.append((h, ew, qkv, c1, c4, att, carries, cat, mixed))
        h = hn

    dh, loss = _loss_head(h, target_p, seq)

    grads = [None] * depth
    for l in reversed(range(depth)):
        h_in, ew, qkv, c1, c4, att, carries, cat, mixed = saved[l]
        dmix, datt, dsg, dc4, dcg, dpost, dbpw2 = _outproj_bwd(dh, mixed, row(post_g, l), w_out[l], att, ew, c4)
        dw_out = _matmul_tn(cat, dmix, 512, "dw_out")
        dq, dk, dv = _attn_bwd(qkv, carries, datt, tri, upper)
        dga, dgb, c3, dcw, dcb, dlng, dlnb = _conv_bwd(dc4, c1, ew, conv_w[l], row(ln_g, l), row(ln_b, l), w_pw2[l])
        dw_pw2 = _matmul_tn(c3, dc4, 512, "dw_pw2")
        dh, dproj, u, dpre = _inproj_bwd(dga, dgb, dcg, dq, dk, dv, dsg, h_in, row(pre_g, l), w_in[l], dh)
        dw_in = _matmul_tn(u, dproj, 896, "dw_in")
        grads[l] = (dpre[0], dpost[0], dw_in, dcw, dcb[0], dlng[0], dlnb[0], dw_pw2, dbpw2[0], dw_out)

    stack = lambda k: jnp.stack([g[k] for g in grads])
    names = ("pre_norm_g", "post_norm_g", "w_in", "conv_w", "conv_b", "conv_ln_g", "conv_ln_b", "w_pw2", "b_pw2",
             "w_out")
    return loss[0, 0], dh, {nm: stack(k) for k, nm in enumerate(names)}


N_CHIPS = 4
ANY = pl.BlockSpec(memory_space=pl.ANY)


def _chip_peers():
    x, y, c = lax.axis_index("x"), lax.axis_index("y"), lax.axis_index("c")
    return x, y, c, [(x, 1 - y), (1 - x, y), (1 - x, 1 - y)]


def _shard_slices(refs, dims, idx):
    out = []
    for ref, (axis, size) in zip(refs, dims):
        assert size % LANES == 0
        start = pl.multiple_of(idx * size, LANES)
        sl = [slice(None)] * len(ref.shape)
        sl[axis] = pl.ds(start, size)
        out.append(ref.at[tuple(sl)])
    return out


def _gather_weights(shards, dims):
    n = len(shards)
    full_shapes = []
    for s, (axis, size) in zip(shards, dims):
        shp = list(s.shape)
        shp[axis] = size * N_CHIPS
        full_shapes.append(jax.ShapeDtypeStruct(tuple(shp), s.dtype))

    def body(*refs):
        srcs, outs = refs[:n], refs[n:2 * n]
        send, recv, loc = refs[2 * n:]
        x, y, c, peers = _chip_peers()
        mine = _shard_slices(outs, dims, 2 * x + y)
        local = [pltpu.make_async_copy(s, d, loc.at[a]) for a, (s, d) in enumerate(zip(srcs, mine))]
        for cp in local:
            cp.start()
        sends = []
        for k, (px, py) in enumerate(peers):
            for a, (s, d) in enumerate(zip(srcs, mine)):
                cp = pltpu.make_async_remote_copy(s, d, send.at[k, a], recv.at[k, a],
                                                  device_id=(px, py, c), device_id_type=MESH)
                cp.start()
                sends.append(cp)
        for k, (px, py) in enumerate(peers):
            for a, (s, d) in enumerate(zip(srcs, _shard_slices(outs, dims, 2 * px + py))):
                pltpu.make_async_remote_copy(s, d, send.at[k, a], recv.at[k, a],
                                             device_id=(px, py, c), device_id_type=MESH).wait_recv()
        for cp in sends:
            cp.wait_send()
        for cp in local:
            cp.wait()

    return _pcall(
        body, name="gather_weights", in_specs=[ANY] * n, out_specs=[ANY] * n, out_shape=full_shapes,
        scratch_shapes=[pltpu.SemaphoreType.DMA((3, n)), pltpu.SemaphoreType.DMA((3, n)),
                        pltpu.SemaphoreType.DMA((n,))],
    )(*shards)


def _reduce_grads(grads, dims):
    n = len(grads)
    piece_shapes = []
    for g, (axis, size) in zip(grads, dims):
        shp = list(g.shape)
        shp[axis] = size
        piece_shapes.append(jax.ShapeDtypeStruct((N_CHIPS,) + tuple(shp), g.dtype))

    def body(*refs):
        srcs, mine, theirs = refs[:n], refs[n:2 * n], refs[2 * n:3 * n]
        send, recv, loc = refs[3 * n:]
        x, y, c, peers = _chip_peers()
        sibling = (x, y, 1 - c)
        own = _shard_slices(srcs, dims, 2 * x + y)
        started = []
        local = [pltpu.make_async_copy(own[a], mine[a].at[3], loc.at[a]) for a in range(n)]
        for cp in local:
            cp.start()

        def remote(src, dst, slot, a, dev):
            return pltpu.make_async_remote_copy(src, dst, send.at[slot, a], recv.at[slot, a],
                                                device_id=dev, device_id_type=MESH)

        for a in range(n):
            cp = remote(own[a], theirs[a].at[3], 3, a, sibling)
            cp.start()
            started.append(cp)
        for k, (px, py) in enumerate(peers):
            for a, src in enumerate(_shard_slices(srcs, dims, 2 * px + py)):
                cp = remote(src, mine[a].at[k], k, a, (px, py, c))
                cp.start()
                started.append(cp)
        for k, (px, py) in enumerate(peers):
            for a in range(n):
                remote(mine[a].at[k], mine[a].at[k], k, a, (px, py, c)).wait_recv()
                cp = remote(mine[a].at[k], theirs[a].at[k], 4 + k, a, sibling)
                cp.start()
                started.append(cp)
        for a in range(n):
            remote(own[a], theirs[a].at[3], 3, a, sibling).wait_recv()
            for k in range(3):
                remote(mine[a].at[k], theirs[a].at[k], 4 + k, a, sibling).wait_recv()
        for cp in started:
            cp.wait_send()
        for cp in local:
            cp.wait()

    return _pcall(
        body, name="reduce_grads", in_specs=[ANY] * n, out_specs=[ANY] * (2 * n), out_shape=piece_shapes * 2,
        scratch_shapes=[pltpu.SemaphoreType.DMA((7, n)), pltpu.SemaphoreType.DMA((7, n)),
                        pltpu.SemaphoreType.DMA((n,))],
    )(*grads)


def _allsum_small(pack):
    rows, cols = pack.shape
    ndev = 8

    def body(p_ref, o_ref, buf, send, recv):
        x, y, c = lax.axis_index("x"), lax.axis_index("y"), lax.axis_index("c")
        me = 4 * x + 2 * y + c
        buf[me] = p_ref[...]
        started = []
        for r in range(1, ndev):
            bx, by, bc = (r >> 2) & 1, (r >> 1) & 1, r & 1
            dev = (x ^ bx, y ^ by, c ^ bc)
            cp = pltpu.make_async_remote_copy(p_ref, buf.at[me], send.at[r], recv.at[r],
                                              device_id=dev, device_id_type=MESH)
            cp.start()
            started.append(cp)
        for r in range(1, ndev):
            pltpu.make_async_remote_copy(p_ref, buf.at[me ^ r], send.at[r], recv.at[r],
                                         device_id=(x, y, c), device_id_type=MESH).wait_recv()
        for cp in started:
            cp.wait_send()
        acc = buf[0]
        for d in range(1, ndev):
            acc = acc + buf[d]
        o_ref[...] = acc

    vm = pl.BlockSpec(memory_space=pltpu.VMEM)
    return _pcall(
        body, name="allsum_small", in_specs=[vm], out_specs=vm,
        out_shape=jax.ShapeDtypeStruct((rows, cols), F32),
        scratch_shapes=[pltpu.VMEM((ndev, rows, cols), F32), pltpu.SemaphoreType.DMA((ndev,)),
                        pltpu.SemaphoreType.DMA((ndev,))],
    )(pack)


def _adamw(parts, w, m, v, name):
    rows, cols = w.shape
    tr = ROW_TILE if rows % ROW_TILE == 0 else rows
    counts = [p.shape[0] for p in parts]

    def body(*refs):
        part_refs = refs[:len(parts)]
        w_ref, m_ref, v_ref, g_ref, d_ref, nm_ref, nv_ref = refs[len(parts):]
        g = None
        for p_ref, cnt in zip(part_refs, counts):
            s = p_ref[0]
            for k in range(1, cnt):
                s = s + p_ref[k]
            g = s if g is None else g + s
        m2 = ADAM_B1 * m_ref[...] + (1.0 - ADAM_B1) * g
        v2 = ADAM_B2 * v_ref[...] + (1.0 - ADAM_B2) * (g * g)
        m_hat = m2 / (1.0 - ADAM_B1 ** ADAM_STEP)
        v_hat = v2 / (1.0 - ADAM_B2 ** ADAM_STEP)
        g_ref[...] = g
        d_ref[...] = -ADAM_LR * (m_hat / (jnp.sqrt(v_hat) + ADAM_EPS) + ADAM_WD * w_ref[...])
        nm_ref[...] = m2
        nv_ref[...] = v2

    blk = pl.BlockSpec((tr, cols), lambda i: (i, 0))
    shp = jax.ShapeDtypeStruct((rows, cols), F32)
    return _pcall(
        body, name=name, grid=(rows // tr,),
        in_specs=[pl.BlockSpec((cnt, tr, cols), lambda i: (0, i, 0)) for cnt in counts] + [blk] * 3,
        out_specs=[blk] * 4, out_shape=[shp] * 4,
        compiler_params=_params("parallel"),
    )(*parts, w, m, v)


def kernel(x, meta_tokens, pre_norm_g, post_norm_g, w_in, conv_w, conv_b, conv_ln_g, conv_ln_b, w_pw2, b_pw2, w_out, loss_target, m_meta_tokens, m_pre_norm_g, m_post_norm_g, m_w_in, m_conv_w, m_conv_b, m_conv_ln_g, m_conv_ln_b, m_w_pw2, m_b_pw2, m_w_out, v_meta_tokens, v_pre_norm_g, v_post_norm_g, v_w_in, v_conv_w, v_conv_b, v_conv_ln_g, v_conv_ln_b, v_w_pw2, v_b_pw2, v_w_out):
    seq, d = x.shape[1], x.shape[2]
    depth = w_in.shape[0]
    length = N_META + seq
    lp = -(-length // ATT_BLOCK) * ATT_BLOCK
    tap_pad = ((0, 0), (0, CONV_PAD - CONV_WIDTH), (0, 0))

    shard_dims = [(2, w_in.shape[2]), (1, w_pw2.shape[1]), (1, w_out.shape[1]), (2, conv_w.shape[2]),
                  (1, meta_tokens.shape[1])]
    w_in_f, w_pw2_f, w_out_f, conv_w_f, meta_f = _gather_weights(
        [w_in.astype(BF16), w_pw2.astype(BF16), w_out.astype(BF16), jnp.pad(conv_w, tap_pad), meta_tokens], shard_dims)

    h0 = jnp.concatenate([meta_f, x[0], jnp.zeros((lp - length, d), F32)], axis=0)
    target_p = jnp.pad(loss_target[0], ((N_META, lp - length), (0, 0)))
    loss, dh0, g = _local_step(h0, target_p, seq, pre_norm_g, post_norm_g, w_in_f, conv_w_f, conv_b, conv_ln_g,
                               conv_ln_b, w_pw2_f, b_pw2, w_out_f)

    pieces = _reduce_grads([g["w_in"], g["w_pw2"], g["w_out"], g["conv_w"]], shard_dims[:4])
    mine, theirs = pieces[:4], pieces[4:]

    def update(k, w, m, v, name):
        shp = w.shape
        flat = lambda a: a.reshape(-1, shp[-1])
        parts = [p.reshape(N_CHIPS, -1, shp[-1]) for p in (mine[k], theirs[k])]
        return [o.reshape(shp) for o in _adamw(parts, flat(w), flat(m), flat(v), name)]

    up_w_in = update(0, w_in, m_w_in, v_w_in, "adamw_w_in")
    up_w_pw2 = update(1, w_pw2, m_w_pw2, v_w_pw2, "adamw_w_pw2")
    up_w_out = update(2, w_out, m_w_out, v_w_out, "adamw_w_out")
    up_conv_w = [o[:, :CONV_WIDTH] for o in update(3, jnp.pad(conv_w, tap_pad), jnp.pad(m_conv_w, tap_pad),
                                                   jnp.pad(v_conv_w, tap_pad, constant_values=1.0), "adamw_conv_w")]

    two = lambda a: a.reshape(-1, d)
    vec_rows = [g["pre_norm_g"], g["post_norm_g"], two(g["conv_b"]), two(g["conv_ln_g"]), two(g["conv_ln_b"]),
                two(g["b_pw2"])]
    n_vec = sum(a.shape[0] for a in vec_rows)
    pack = jnp.concatenate(vec_rows + [dh0[:N_META], jnp.full((8, d), loss, F32)], axis=0)
    pack = jnp.pad(pack, ((0, -pack.shape[0] % 8), (0, 0)))
    tot = _allsum_small(pack)
    loss_all = tot[n_vec + N_META, 0]

    cat = lambda *a: jnp.concatenate([two(t) for t in a], axis=0)
    small = (pre_norm_g, post_norm_g, conv_b, conv_ln_g, conv_ln_b, b_pw2)
    small_m = (m_pre_norm_g, m_post_norm_g, m_conv_b, m_conv_ln_g, m_conv_ln_b, m_b_pw2)
    small_v = (v_pre_norm_g, v_post_norm_g, v_conv_b, v_conv_ln_g, v_conv_ln_b, v_b_pw2)
    up_small = _adamw([tot[None, :n_vec]], cat(*small), cat(*small_m), cat(*small_v), "adamw_vectors")

    def unpack(o):
        res, r0 = [], 0
        for t in small:
            nrow = t.size // d
            res.append(o[r0:r0 + nrow].reshape(t.shape))
            r0 += nrow
        return res

    up_small = [unpack(o) for o in up_small]
    chip = 2 * lax.axis_index("x") + lax.axis_index("y")
    mcols = meta_tokens.shape[1]
    g_meta = lax.dynamic_slice_in_dim(tot[n_vec:n_vec + N_META], chip * mcols, mcols, axis=1)
    up_meta = _adamw([g_meta[None]], meta_tokens, m_meta_tokens, v_meta_tokens, "adamw_meta")

    grad_x = dh0[N_META:length][None]
    outs = [loss_all, grad_x]
    for j in range(4):
        pre, post, cb, lg, lb, bp = up_small[j]
        outs += [up_meta[j], pre, post, up_w_in[j], up_conv_w[j], cb, lg, lb, up_w_pw2[j], bp, up_w_out[j]]
    return tuple(outs)
```

```python
import functools

import jax
import jax.numpy as jnp
from jax import lax
from jax.experimental import pallas as pl
from jax.experimental.pallas import tpu as pltpu

F32 = jnp.float32
BF16 = jnp.bfloat16

N_META = 16
D_CONV = 512
D_SB = 512
HEAD_DIM = 64
CONV_WIDTH = 31
CONV_PAD = 32
RMS_EPS = 1e-6
LN_EPS = 1e-5
Q_SCALE = HEAD_DIM ** -0.5

ADAM_LR = 0.001
ADAM_B1 = 0.9
ADAM_B2 = 0.999
ADAM_EPS = 1e-08
ADAM_WD = 0.01
ADAM_STEP = 10

LANES = 128
ROW_TILE = 256
ATT_BLOCK = 256
VMEM_LIMIT = 56 * 1024 * 1024

MESH = pl.DeviceIdType.MESH


def _pcall(body, **kw):
    return pl.pallas_call(body, **kw)


def _params(*sem):
    return pltpu.CompilerParams(dimension_semantics=sem, vmem_limit_bytes=VMEM_LIMIT)


def _sigmoid(x):
    return 1.0 / (1.0 + jnp.exp(-x))


def _silu_fwd_bwd(x):
    s = _sigmoid(x)
    return x * s, s * (1.0 + x * (1.0 - s))


def _nt(a, b):
    return lax.dot_general(a, b, (((1,), (1,)), ((), ())), preferred_element_type=F32)


def _tn(a, b):
    return lax.dot_general(a, b, (((0,), (0,)), ((), ())), preferred_element_type=F32)


def _nn(a, b):
    return jnp.dot(a, b, preferred_element_type=F32)


def _inproj(h, g, w):
    lp, d = h.shape
    n = w.shape[1]

    def body(h_ref, g_ref, w_ref, ew_ref, qkv_ref):
        x = h_ref[...]
        rstd = lax.rsqrt(jnp.mean(x * x, axis=-1, keepdims=True) + RMS_EPS)
        u = ((x * rstd) * g_ref[...]).astype(BF16)
        p = _nn(u, w_ref[...])
        ew_ref[:, 0:1536] = p[:, 0:1536]
        ew_ref[:, 1536:2048] = p[:, 3072:3584]
        qkv_ref[:, 0:512] = (p[:, 1536:2048] * Q_SCALE).astype(BF16)
        qkv_ref[:, 512:1536] = p[:, 2048:3072].astype(BF16)

    return _pcall(
        body, name="inproj_fwd", grid=(lp // ROW_TILE,),
        in_specs=[pl.BlockSpec((ROW_TILE, d), lambda i: (i, 0)),
                  pl.BlockSpec((1, d), lambda i: (0, 0)),
                  pl.BlockSpec((d, n), lambda i: (0, 0))],
        out_specs=[pl.BlockSpec((ROW_TILE, 2048), lambda i: (i, 0)),
                   pl.BlockSpec((ROW_TILE, 1536), lambda i: (i, 0))],
        out_shape=[jax.ShapeDtypeStruct((lp, 2048), F32), jax.ShapeDtypeStruct((lp, 1536), BF16)],
        compiler_params=_params("parallel"),
    )(h, g, w)


def _layer_norm_stats(c1):
    mu = jnp.mean(c1, axis=-1, keepdims=True)
    xc = c1 - mu
    var = jnp.mean(xc * xc, axis=-1, keepdims=True)
    rstd = lax.rsqrt(var + LN_EPS)
    return xc * rstd, rstd


def _conv_fwd(ew, cw, cb, lng, lnb, wpw2, bpw2):
    lp = ew.shape[0]
    tm = ROW_TILE

    def body(ew_ref, cw_ref, cb_ref, lng_ref, lnb_ref, w_ref, b_ref, c1_ref, c4_ref, c5_ref, xbuf):
        @pl.when(pl.program_id(0) == 0)
        def _():
            xbuf[0:CONV_PAD, :] = jnp.zeros((CONV_PAD, D_CONV), F32)

        ga = ew_ref[:, 0:512]
        gb = ew_ref[:, 512:1024]
        cg = ew_ref[:, 1024:1536]
        xbuf[CONV_PAD:CONV_PAD + tm, :] = ga * _sigmoid(gb)
        acc = jnp.zeros((tm, D_CONV), F32) + cb_ref[...]
        for j in range(CONV_WIDTH):
            acc = acc + cw_ref[j:j + 1, :] * xbuf[pl.ds(CONV_PAD - (CONV_WIDTH - 1) + j, tm), :]
        c1_ref[...] = acc
        xbuf[0:CONV_PAD, :] = xbuf[tm:tm + CONV_PAD, :]
        xhat, _ = _layer_norm_stats(acc)
        c2 = xhat * lng_ref[...] + lnb_ref[...]
        c3 = c2 * _sigmoid(c2)
        c4 = _nn(c3.astype(BF16), w_ref[...]) + b_ref[...]
        c4_ref[...] = c4
        c5_ref[...] = (c4 * (cg * _sigmoid(cg))).astype(BF16)

    vec = pl.BlockSpec((1, D_CONV), lambda i: (0, 0))
    row = pl.BlockSpec((tm, D_CONV), lambda i: (i, 0))
    return _pcall(
        body, name="conv_fwd", grid=(lp // tm,),
        in_specs=[pl.BlockSpec((tm, 1536), lambda i: (i, 0)),
                  pl.BlockSpec((CONV_PAD, D_CONV), lambda i: (0, 0)),
                  vec, vec, vec,
                  pl.BlockSpec((D_CONV, D_CONV), lambda i: (0, 0)),
                  vec],
        out_specs=[row, row, row],
        out_shape=[jax.ShapeDtypeStruct((lp, D_CONV), F32), jax.ShapeDtypeStruct((lp, D_CONV), F32),
                   jax.ShapeDtypeStruct((lp, D_CONV), BF16)],
        scratch_shapes=[pltpu.VMEM((tm + CONV_PAD, D_CONV), F32)],
        compiler_params=_params("arbitrary"),
    )(ew, cw, cb, lng, lnb, wpw2, bpw2)


def _split_matmul(x, m01):
    hi = x.astype(BF16)
    lo = (x - hi.astype(F32)).astype(BF16)
    t = _nn(jnp.concatenate([hi, lo], axis=0), m01)
    m = x.shape[0]
    return t[:m] + t[m:]


def _sb_block(q2, kj, z_mask, tri):
    z = _nt(q2, kj)
    ls = -(jnp.maximum(z, 0.0) + jnp.log(1.0 + jnp.exp(-jnp.abs(z))))
    if z_mask is not None:
        ls = jnp.where(z_mask, ls, 0.0)
    return z, ls, _split_matmul(ls, tri)


def _attn_masks():
    lane = lax.broadcasted_iota(jnp.int32, (1, LANES), 1)
    row = lax.broadcasted_iota(jnp.int32, (2 * ATT_BLOCK, ATT_BLOCK), 0)
    col = lax.broadcasted_iota(jnp.int32, (2 * ATT_BLOCK, ATT_BLOCK), 1)
    return lane < HEAD_DIM, col < (row & (ATT_BLOCK - 1))


def _stack_heads(x, first_head):
    zero = jnp.zeros_like(x)
    return jnp.concatenate([jnp.where(first_head, x, zero), jnp.where(first_head, zero, x)], axis=0)


def _unstack_heads(x2, first_head):
    rows = x2.shape[0] // 2
    return jnp.where(first_head, x2[:rows], x2[rows:])


def _attn_fwd(qkv, tri):
    lp = qkv.shape[0]
    bq = ATT_BLOCK

    assert lp // bq <= LANES

    def body(q_ref, k_ref, v_ref, tri_ref, o_ref, carry_ref):
        i = pl.program_id(1)
        first_head, vis = _attn_masks()
        lane = lax.broadcasted_iota(jnp.int32, (1, LANES), 1)
        q2 = _stack_heads(q_ref[...], first_head)
        tri_m = tri_ref[...]

        def blocks(js, state, mask):
            c, acc, cmat = state
            offs = [pl.multiple_of(j * bq, bq) for j in js]
            zs = [_nt(q2, k_ref[pl.ds(off, bq), :]) for off in offs]
            lss = []
            for z in zs:
                ls = -(jnp.maximum(z, 0.0) + jnp.log(1.0 + jnp.exp(-jnp.abs(z))))
                lss.append(ls if mask is None else jnp.where(mask, ls, 0.0))
            tails = [_split_matmul(ls, tri_m) for ls in lss]
            probs = []
            for j, z, ls, tail in zip(js, zs, lss, tails):
                a = jnp.exp(z + ls + tail + c)
                probs.append(a if mask is None else jnp.where(mask, a, 0.0))
                cmat = jnp.where(lane == j, c, cmat)
                c = c + tail[:, 0:1] + ls[:, 0:1]
            for off, a in zip(offs, probs):
                acc = acc + _nn(a.astype(BF16), v_ref[pl.ds(off, bq), :])
            return c, acc, cmat

        zeros = jnp.zeros((2 * bq, LANES), F32)
        state = blocks([i], (jnp.zeros((2 * bq, 1), F32), zeros, zeros), vis)
        state = lax.cond(i % 2 == 1, lambda st: blocks([i - 1], st, None), lambda st: st, state)
        first = i - 1 - i % 2
        state = lax.fori_loop(0, i // 2, lambda t, st: blocks([first - 2 * t, first - 2 * t - 1], st, None), state)
        carry_ref[:, 0:LANES] = state[2][:bq]
        carry_ref[:, LANES:] = state[2][bq:]
        o_ref[...] = _unstack_heads(state[1], first_head)

    npair = D_SB // LANES
    return _pcall(
        body, name="attn_fwd", grid=(npair, lp // bq),
        in_specs=[pl.BlockSpec((bq, LANES), lambda p, i: (i, p)),
                  pl.BlockSpec((lp, LANES), lambda p, i: (0, npair + p)),
                  pl.BlockSpec((lp, LANES), lambda p, i: (0, 2 * npair + p)),
                  pl.BlockSpec((bq, bq), lambda p, i: (0, 0))],
        out_specs=[pl.BlockSpec((bq, LANES), lambda p, i: (i, p)),
                   pl.BlockSpec((bq, 2 * LANES), lambda p, i: (i, p))],
        out_shape=[jax.ShapeDtypeStruct((lp, D_SB), F32), jax.ShapeDtypeStruct((lp, 2 * D_SB), F32)],
        compiler_params=_params("parallel", "parallel"),
    )(qkv, qkv, qkv, tri)


def _outproj(c5, att, ew, h, w, g):
    lp, d = h.shape
    tm = ROW_TILE

    def body(c5_ref, att_ref, sg_ref, h_ref, w_ref, g_ref, hn_ref, cat_ref, mix_ref):
        sg = sg_ref[...]
        s = att_ref[...] * (sg * _sigmoid(sg))
        cat_ref[:, 0:D_CONV] = c5_ref[...]
        cat_ref[:, D_CONV:] = s.astype(BF16)
        mixed = _nn(cat_ref[...], w_ref[...])
        mix_ref[...] = mixed
        rstd = lax.rsqrt(jnp.mean(mixed * mixed, axis=-1, keepdims=True) + RMS_EPS)
        hn_ref[...] = h_ref[...] + (mixed * rstd) * g_ref[...]

    half = pl.BlockSpec((tm, 512), lambda i: (i, 0))
    full = pl.BlockSpec((tm, d), lambda i: (i, 0))
    return _pcall(
        body, name="outproj_fwd", grid=(lp // tm,),
        in_specs=[half, half, pl.BlockSpec((tm, 512), lambda i: (i, 3)), full,
                  pl.BlockSpec((d, d), lambda i: (0, 0)), pl.BlockSpec((1, d), lambda i: (0, 0))],
        out_specs=[full, full, full],
        out_shape=[jax.ShapeDtypeStruct((lp, d), F32), jax.ShapeDtypeStruct((lp, d), BF16),
                   jax.ShapeDtypeStruct((lp, d), F32)],
        compiler_params=_params("parallel"),
    )(c5, att, ew, h, w, g)


def _loss_head(h, target, seq):
    lp, d = h.shape
    tm = ROW_TILE

    def body(h_ref, t_ref, dh_ref, loss_ref):
        i = pl.program_id(0)

        @pl.when(i == 0)
        def _():
            loss_ref[...] = jnp.zeros_like(loss_ref)

        row = i * tm + lax.broadcasted_iota(jnp.int32, (tm, 1), 0)
        real = jnp.logical_and(row >= N_META, row < N_META + seq)
        diff = jnp.where(real, h_ref[...] - t_ref[...], 0.0)
        dh_ref[...] = diff * (1.0 / d)
        loss_ref[...] += 0.5 * jnp.sum(jnp.sum(diff * diff, axis=-1, keepdims=True) * (1.0 / d))

    full = pl.BlockSpec((tm, d), lambda i: (i, 0))
    return _pcall(
        body, name="loss_head", grid=(lp // tm,),
        in_specs=[full, full],
        out_specs=[full, pl.BlockSpec((8, LANES), lambda i: (0, 0))],
        out_shape=[jax.ShapeDtypeStruct((lp, d), F32), jax.ShapeDtypeStruct((8, LANES), F32)],
        compiler_params=_params("arbitrary"),
    )(h, target)


def _outproj_bwd(dh, mixed, g, w, att, ew, c4):
    lp, d = dh.shape
    tm = ROW_TILE

    def body(dh_ref, mix_ref, g_ref, w_ref, att_ref, cg_ref, sg_ref, c4_ref,
             dmix_ref, datt_ref, dsg_ref, dc4_ref, dcg_ref, dg_ref, db_ref):
        @pl.when(pl.program_id(0) == 0)
        def _():
            dg_ref[...] = jnp.zeros_like(dg_ref)
            db_ref[...] = jnp.zeros_like(db_ref)

        mixed = mix_ref[...]
        dhv = dh_ref[...]
        rstd = lax.rsqrt(jnp.mean(mixed * mixed, axis=-1, keepdims=True) + RMS_EPS)
        n = mixed * rstd
        dg_ref[...] += jnp.sum(dhv * n, axis=0, keepdims=True)
        dn = dhv * g_ref[...]
        dmix = (rstd * (dn - n * jnp.mean(dn * n, axis=-1, keepdims=True))).astype(BF16)
        dmix_ref[...] = dmix
        dcat = _nt(dmix, w_ref[...])
        dc5 = dcat[:, 0:D_CONV]
        ds = dcat[:, D_CONV:]
        silu_sg, dsilu_sg = _silu_fwd_bwd(sg_ref[...])
        datt_ref[...] = (ds * silu_sg).astype(BF16)
        dsg_ref[...] = (ds * att_ref[...] * dsilu_sg).astype(BF16)
        silu_cg, dsilu_cg = _silu_fwd_bwd(cg_ref[...])
        dc4 = dc5 * silu_cg
        db_ref[...] += jnp.sum(dc4, axis=0, keepdims=True)
        dc4_ref[...] = dc4.astype(BF16)
        dcg_ref[...] = (dc5 * c4_ref[...] * dsilu_cg).astype(BF16)

    half = pl.BlockSpec((tm, 512), lambda i: (i, 0))
    full = pl.BlockSpec((tm, d), lambda i: (i, 0))
    hb = jax.ShapeDtypeStruct((lp, 512), BF16)
    return _pcall(
        body, name="outproj_bwd", grid=(lp // tm,),
        in_specs=[full, full, pl.BlockSpec((1, d), lambda i: (0, 0)), pl.BlockSpec((d, d), lambda i: (0, 0)),
                  half, pl.BlockSpec((tm, 512), lambda i: (i, 2)), pl.BlockSpec((tm, 512), lambda i: (i, 3)), half],
        out_specs=[full, half, half, half, half,
                   pl.BlockSpec((1, d), lambda i: (0, 0)), pl.BlockSpec((1, 512), lambda i: (0, 0))],
        out_shape=[jax.ShapeDtypeStruct((lp, d), BF16), hb, hb, hb, hb,
                   jax.ShapeDtypeStruct((1, d), F32), jax.ShapeDtypeStruct((1, 512), F32)],
        compiler_params=_params("arbitrary"),
    )(dh, mixed, g, w, att, ew, ew, c4)


def _attn_bwd(qkv, carries, datt, tri, upper):
    lp = qkv.shape[0]
    bq = ATT_BLOCK

    def body(q_ref, k_ref, v_ref, carry_ref, do_ref, tri_ref, upper_ref, dq_ref, dk_ref, dv_ref):
        i = pl.program_id(1)

        @pl.when(i == 0)
        def _():
            dk_ref[...] = jnp.zeros_like(dk_ref)
            dv_ref[...] = jnp.zeros_like(dv_ref)

        first_head, vis = _attn_masks()
        lane = lax.broadcasted_iota(jnp.int32, (1, LANES), 1)
        q2 = _stack_heads(q_ref[...], first_head)
        do2 = _stack_heads(do_ref[...], first_head)
        cmat = jnp.concatenate([carry_ref[:, 0:LANES], carry_ref[:, LANES:]], axis=0)
        tri_m = tri_ref[...]
        upper_m = upper_ref[...]

        def blocks(js, state, mask):
            run, dq = state
            offs = [pl.multiple_of(j * bq, bq) for j in js]
            zs = [_nt(q2, k_ref[pl.ds(off, bq), :]) for off in offs]
            lss = []
            for z in zs:
                ls = -(jnp.maximum(z, 0.0) + jnp.log(1.0 + jnp.exp(-jnp.abs(z))))
                lss.append(ls if mask is None else jnp.where(mask, ls, 0.0))
            tails = [_split_matmul(ls, tri_m) for ls in lss]
            das = [_nt(do2, v_ref[pl.ds(off, bq), :]) for off in offs]
            probs, des = [], []
            for j, z, ls, tail, da in zip(js, zs, lss, tails, das):
                c = jnp.sum(jnp.where(lane == j, cmat, 0.0), axis=-1, keepdims=True)
                a = jnp.exp(z + ls + tail + c)
                a = a if mask is None else jnp.where(mask, a, 0.0)
                probs.append(a.astype(BF16))
                des.append(da * a)
            prefixes = [_split_matmul(de, upper_m) for de in des]
            dzs = []
            for z, ls, de, prefix in zip(zs, lss, des, prefixes):
                beta = jnp.exp(z + ls)
                dz = de - beta * (de + run + prefix)
                dzs.append((dz if mask is None else jnp.where(mask, dz, 0.0)).astype(BF16))
                run = run + prefix[:, bq - 1:bq] + de[:, bq - 1:bq]
            for off, dzb, ab in zip(offs, dzs, probs):
                dq = dq + _nn(dzb, k_ref[pl.ds(off, bq), :])
                dk_ref[pl.ds(off, bq), :] += _tn(dzb, q2)
                dv_ref[pl.ds(off, bq), :] += _tn(ab, do2)
            return run, dq

        state = (jnp.zeros((2 * bq, 1), F32), jnp.zeros((2 * bq, LANES), F32))
        state = lax.fori_loop(0, i // 2, lambda t, st: blocks([2 * t, 2 * t + 1], st, None), state)
        state = lax.cond(i % 2 == 1, lambda st: blocks([i - 1], st, None), lambda st: st, state)
        state = blocks([i], state, vis)
        dq_ref[...] = (_unstack_heads(state[1], first_head) * Q_SCALE).astype(BF16)

    npair = D_SB // LANES
    qb = pl.BlockSpec((bq, LANES), lambda p, i: (i, p))
    colb = pl.BlockSpec((lp, LANES), lambda p, i: (0, p))
    sq = pl.BlockSpec((bq, bq), lambda p, i: (0, 0))
    return _pcall(
        body, name="attn_bwd", grid=(npair, lp // bq),
        in_specs=[qb,
                  pl.BlockSpec((lp, LANES), lambda p, i: (0, npair + p)),
                  pl.BlockSpec((lp, LANES), lambda p, i: (0, 2 * npair + p)),
                  pl.BlockSpec((bq, 2 * LANES), lambda p, i: (i, p)), qb, sq, sq],
        out_specs=[qb, colb, colb],
        out_shape=[jax.ShapeDtypeStruct((lp, D_SB), BF16), jax.ShapeDtypeStruct((lp, D_SB), F32),
                   jax.ShapeDtypeStruct((lp, D_SB), F32)],
        compiler_params=_params("parallel", "arbitrary"),
    )(qkv, qkv, qkv, carries, datt, tri, upper)


def _conv_bwd(dc4, c1, ew, cw, lng, lnb, wpw2):
    lp = ew.shape[0]
    tm = ROW_TILE
    nt = lp // tm
    halo_per_tile = tm // CONV_PAD

    def body(dc4_ref, c1_ref, ew_ref, halo_ref, cw_ref, lng_ref, lnb_ref, w_ref,
             dga_ref, dgb_ref, c3_ref, dcw_ref, dcb_ref, dlng_ref, dlnb_ref, xbuf, dbuf):
        step = pl.program_id(0)

        @pl.when(step == 0)
        def _():
            dcw_ref[...] = jnp.zeros_like(dcw_ref)
            dcb_ref[...] = jnp.zeros_like(dcb_ref)
            dlng_ref[...] = jnp.zeros_like(dlng_ref)
            dlnb_ref[...] = jnp.zeros_like(dlnb_ref)
            dbuf[tm:tm + CONV_PAD, :] = jnp.zeros((CONV_PAD, D_CONV), F32)

        dc3 = _nt(dc4_ref[...], w_ref[...])
        xhat, rstd = _layer_norm_stats(c1_ref[...])
        c2 = xhat * lng_ref[...] + lnb_ref[...]
        c3, dsilu = _silu_fwd_bwd(c2)
        c3_ref[...] = c3.astype(BF16)
        dc2 = dc3 * dsilu
        dlng_ref[...] += jnp.sum(dc2 * xhat, axis=0, keepdims=True)
        dlnb_ref[...] += jnp.sum(dc2, axis=0, keepdims=True)
        dxhat = dc2 * lng_ref[...]
        dc1 = rstd * (dxhat - jnp.mean(dxhat, axis=-1, keepdims=True)
                      - xhat * jnp.mean(dxhat * xhat, axis=-1, keepdims=True))
        dcb_ref[...] += jnp.sum(dc1, axis=0, keepdims=True)
        dbuf[0:tm, :] = dc1

        ga = ew_ref[:, 0:512]
        sgb = _sigmoid(ew_ref[:, 512:1024])
        xbuf[CONV_PAD:CONV_PAD + tm, :] = ga * sgb
        first_tile = step == nt - 1
        halo = halo_ref[:, 0:512] * _sigmoid(halo_ref[:, 512:1024])
        xbuf[0:CONV_PAD, :] = jnp.where(first_tile, 0.0, halo)

        dc0 = jnp.zeros((tm, D_CONV), F32)
        for j in range(CONV_WIDTH):
            dc0 = dc0 + cw_ref[j:j + 1, :] * dbuf[pl.ds(CONV_WIDTH - 1 - j, tm), :]
            tap = dc1 * xbuf[pl.ds(CONV_PAD - (CONV_WIDTH - 1) + j, tm), :]
            dcw_ref[j:j + 1, :] += jnp.sum(tap, axis=0, keepdims=True)
        dbuf[tm:tm + CONV_PAD, :] = dbuf[0:CONV_PAD, :]
        dga_ref[...] = (dc0 * sgb).astype(BF16)
        dgb_ref[...] = (dc0 * ga * sgb * (1.0 - sgb)).astype(BF16)

    rev = lambda i: (nt - 1 - i, 0)
    row = pl.BlockSpec((tm, D_CONV), rev)
    vec = pl.BlockSpec((1, D_CONV), lambda i: (0, 0))
    hb = jax.ShapeDtypeStruct((lp, D_CONV), BF16)
    vs = jax.ShapeDtypeStruct((1, D_CONV), F32)
    return _pcall(
        body, name="conv_bwd", grid=(nt,),
        in_specs=[row, row, pl.BlockSpec((tm, 1024), rev),
                  pl.BlockSpec((CONV_PAD, 1024), lambda i: (jnp.maximum((nt - 1 - i) * halo_per_tile - 1, 0), 0)),
                  pl.BlockSpec((CONV_PAD, D_CONV), lambda i: (0, 0)), vec, vec,
                  pl.BlockSpec((D_CONV, D_CONV), lambda i: (0, 0))],
        out_specs=[row, row, row, pl.BlockSpec((CONV_PAD, D_CONV), lambda i: (0, 0)), vec, vec, vec],
        out_shape=[hb, hb, hb, jax.ShapeDtypeStruct((CONV_PAD, D_CONV), F32), vs, vs, vs],
        scratch_shapes=[pltpu.VMEM((tm + CONV_PAD, D_CONV), F32), pltpu.VMEM((tm + CONV_PAD, D_CONV), F32)],
        compiler_params=_params("arbitrary"),
    )(dc4, c1, ew, ew, cw, lng, lnb, wpw2)


def _inproj_bwd(dga, dgb, dcg, dq, dk, dv, dsg, h, g, w, dh_out):
    lp, d = h.shape
    n = w.shape[1]
    tm = ROW_TILE

    def body(dga_ref, dgb_ref, dcg_ref, dq_ref, dk_ref, dv_ref, dsg_ref, h_ref, g_ref, w_ref, dho_ref,
             dh_ref, dproj_ref, u_ref, dg_ref):
        @pl.when(pl.program_id(0) == 0)
        def _():
            dg_ref[...] = jnp.zeros_like(dg_ref)

        dproj_ref[:, 0:512] = dga_ref[...]
        dproj_ref[:, 512:1024] = dgb_ref[...]
        dproj_ref[:, 1024:1536] = dcg_ref[...]
        dproj_ref[:, 1536:2048] = dq_ref[...]
        dproj_ref[:, 2048:2560] = dk_ref[...].astype(BF16)
        dproj_ref[:, 2560:3072] = dv_ref[...].astype(BF16)
        dproj_ref[:, 3072:3584] = dsg_ref[...]
        du = _nt(dproj_ref[...], w_ref[...])
        x = h_ref[...]
        rstd = lax.rsqrt(jnp.mean(x * x, axis=-1, keepdims=True) + RMS_EPS)
        nrm = x * rstd
        u_ref[...] = (nrm * g_ref[...]).astype(BF16)
        dg_ref[...] += jnp.sum(du * nrm, axis=0, keepdims=True)
        dn = du * g_ref[...]
        dh_ref[...] = dho_ref[...] + rstd * (dn - nrm * jnp.mean(dn * nrm, axis=-1, keepdims=True))

    half = pl.BlockSpec((tm, 512), lambda i: (i, 0))
    full = pl.BlockSpec((tm, d), lambda i: (i, 0))
    return _pcall(
        body, name="inproj_bwd", grid=(lp // tm,),
        in_specs=[half] * 7 + [full, pl.BlockSpec((1, d), lambda i: (0, 0)),
                               pl.BlockSpec((d, n), lambda i: (0, 0)), full],
        out_specs=[full, pl.BlockSpec((tm, n), lambda i: (i, 0)), full, pl.BlockSpec((1, d), lambda i: (0, 0))],
        out_shape=[jax.ShapeDtypeStruct((lp, d), F32), jax.ShapeDtypeStruct((lp, n), BF16),
                   jax.ShapeDtypeStruct((lp, d), BF16), jax.ShapeDtypeStruct((1, d), F32)],
        compiler_params=_params("arbitrary"),
    )(dga, dgb, dcg, dq, dk, dv, dsg, h, g, w, dh_out)


def _row_split(m, parts):
    tm = m // parts
    assert tm * parts == m and tm % 16 == 0, (m, parts)
    return tm


def _matmul_tn(x, dy, tn, name):
    m, k = x.shape
    n = dy.shape[1]
    tm = _row_split(m, 4 if m % 64 == 0 else 1)

    def body(x_ref, dy_ref, o_ref):
        @pl.when(pl.program_id(1) == 0)
        def _():
            o_ref[...] = jnp.zeros_like(o_ref)

        o_ref[...] += _tn(x_ref[...], dy_ref[...])

    return _pcall(
        body, name=name, grid=(n // tn, m // tm),
        in_specs=[pl.BlockSpec((tm, k), lambda j, r: (r, 0)), pl.BlockSpec((tm, tn), lambda j, r: (r, j))],
        out_specs=pl.BlockSpec((k, tn), lambda j, r: (0, j)),
        out_shape=jax.ShapeDtypeStruct((k, n), F32),
        compiler_params=_params("parallel", "arbitrary"),
    )(x, dy)


def _local_step(h0, target_p, seq, pre_g, post_g, w_in, conv_w, conv_b, ln_g, ln_b, w_pw2, b_pw2, w_out):
    depth = w_in.shape[0]
    ar = jnp.arange(ATT_BLOCK)
    tri = (ar[:, None] > ar[None, :]).astype(BF16)
    upper = (ar[:, None] < ar[None, :]).astype(BF16)
    row = lambda a, l: a[l][None, :]

    saved = []
    h = h0
    for l in range(depth):
        ew, qkv = _inproj(h, row(pre_g, l), w_in[l])
        c1, c4, c5 = _conv_fwd(ew, conv_w[l], row(conv_b, l), row(ln_g, l), row(ln_b, l), w_pw2[l], row(b_pw2, l))
        att, carries = _attn_fwd(qkv, tri)
        hn, cat, mixed = _outproj(c5, att, ew, h, w_out[l], row(post_g, l))
        saved.append((h, ew, qkv, c1, c4, att, carries, cat, mixed))
        h = hn

    dh, loss = _loss_head(h, target_p, seq)

    grads = [None] * depth
    for l in reversed(range(depth)):
        h_in, ew, qkv, c1, c4, att, carries, cat, mixed = saved[l]
        dmix, datt, dsg, dc4, dcg, dpost, dbpw2 = _outproj_bwd(dh, mixed, row(post_g, l), w_out[l], att, ew, c4)
        dw_out = _matmul_tn(cat, dmix, 512, "dw_out")
        dq, dk, dv = _attn_bwd(qkv, carries, datt, tri, upper)
        dga, dgb, c3, dcw, dcb, dlng, dlnb = _conv_bwd(dc4, c1, ew, conv_w[l], row(ln_g, l), row(ln_b, l), w_pw2[l])
        dw_pw2 = _matmul_tn(c3, dc4, 512, "dw_pw2")
        dh, dproj, u, dpre = _inproj_bwd(dga, dgb, dcg, dq, dk, dv, dsg, h_in, row(pre_g, l), w_in[l], dh)
        dw_in = _matmul_tn(u, dproj, 896, "dw_in")
        grads[l] = (dpre[0], dpost[0], dw_in, dcw, dcb[0], dlng[0], dlnb[0], dw_pw2, dbpw2[0], dw_out)

    stack = lambda k: jnp.stack([g[k] for g in grads])
    names = ("pre_norm_g", "post_norm_g", "w_in", "conv_w", "conv_b", "conv_ln_g", "conv_ln_b", "w_pw2", "b_pw2",
             "w_out")
    return loss[0, 0], dh, {nm: stack(k) for k, nm in enumerate(names)}


N_CHIPS = 4
ANY = pl.BlockSpec(memory_space=pl.ANY)


def _chip_peers():
    x, y, c = lax.axis_index("x"), lax.axis_index("y"), lax.axis_index("c")
    return x, y, c, [(x, 1 - y), (1 - x, y), (1 - x, 1 - y)]


def _shard_slices(refs, dims, idx):
    out = []
    for ref, (axis, size) in zip(refs, dims):
        assert size % LANES == 0
        start = pl.multiple_of(idx * size, LANES)
        sl = [slice(None)] * len(ref.shape)
        sl[axis] = pl.ds(start, size)
        out.append(ref.at[tuple(sl)])
    return out


def _gather_weights(shards, dims):
    n = len(shards)
    full_shapes = []
    for s, (axis, size) in zip(shards, dims):
        shp = list(s.shape)
        shp[axis] = size * N_CHIPS
        full_shapes.append(jax.ShapeDtypeStruct(tuple(shp), s.dtype))

    def body(*refs):
        srcs, outs = refs[:n], refs[n:2 * n]
        send, recv, loc = refs[2 * n:]
        x, y, c, peers = _chip_peers()
        mine = _shard_slices(outs, dims, 2 * x + y)
        local = [pltpu.make_async_copy(s, d, loc.at[a]) for a, (s, d) in enumerate(zip(srcs, mine))]
        for cp in local:
            cp.start()
        sends = []
        for k, (px, py) in enumerate(peers):
            for a, (s, d) in enumerate(zip(srcs, mine)):
                cp = pltpu.make_async_remote_copy(s, d, send.at[k, a], recv.at[k, a],
                                                  device_id=(px, py, c), device_id_type=MESH)
                cp.start()
                sends.append(cp)
        for k, (px, py) in enumerate(peers):
            for a, (s, d) in enumerate(zip(srcs, _shard_slices(outs, dims, 2 * px + py))):
                pltpu.make_async_remote_copy(s, d, send.at[k, a], recv.at[k, a],
                                             device_id=(px, py, c), device_id_type=MESH).wait_recv()
        for cp in sends:
            cp.wait_send()
        for cp in local:
            cp.wait()

    return _pcall(
        body, name="gather_weights", in_specs=[ANY] * n, out_specs=[ANY] * n, out_shape=full_shapes,
        scratch_shapes=[pltpu.SemaphoreType.DMA((3, n)), pltpu.SemaphoreType.DMA((3, n)),
                        pltpu.SemaphoreType.DMA((n,))],
    )(*shards)


def _reduce_grads(grads, dims):
    n = len(grads)
    piece_shapes = []
    for g, (axis, size) in zip(grads, dims):
        shp = list(g.shape)
        shp[axis] = size
        piece_shapes.append(jax.ShapeDtypeStruct((N_CHIPS,) + tuple(shp), g.dtype))

    def body(*refs):
        srcs, mine, theirs = refs[:n], refs[n:2 * n], refs[2 * n:3 * n]
        send, recv, loc = refs[3 * n:]
        x, y, c, peers = _chip_peers()
        sibling = (x, y, 1 - c)
        own = _shard_slices(srcs, dims, 2 * x + y)
        started = []
        local = [pltpu.make_async_copy(own[a], mine[a].at[3], loc.at[a]) for a in range(n)]
        for cp in local:
            cp.start()

        def remote(src, dst, slot, a, dev):
            return pltpu.make_async_remote_copy(src, dst, send.at[slot, a], recv.at[slot, a],
                                                device_id=dev, device_id_type=MESH)

        for a in range(n):
            cp = remote(own[a], theirs[a].at[3], 3, a, sibling)
            cp.start()
            started.append(cp)
        for k, (px, py) in enumerate(peers):
            for a, src in enumerate(_shard_slices(srcs, dims, 2 * px + py)):
                cp = remote(src, mine[a].at[k], k, a, (px, py, c))
                cp.start()
                started.append(cp)
        for k, (px, py) in enumerate(peers):
            for a in range(n):
                remote(mine[a].at[k], mine[a].at[k], k, a, (px, py, c)).wait_recv()
                cp = remote(mine[a].at[k], theirs[a].at[k], 4 + k, a, sibling)
                cp.start()
                started.append(cp)
        for a in range(n):
            remote(own[a], theirs[a].at[3], 3, a, sibling).wait_recv()
            for k in range(3):
                remote(mine[a].at[k], theirs[a].at[k], 4 + k, a, sibling).wait_recv()
        for cp in started:
            cp.wait_send()
        for cp in local:
            cp.wait()

    return _pcall(
        body, name="reduce_grads", in_specs=[ANY] * n, out_specs=[ANY] * (2 * n), out_shape=piece_shapes * 2,
        scratch_shapes=[pltpu.SemaphoreType.DMA((7, n)), pltpu.SemaphoreType.DMA((7, n)),
                        pltpu.SemaphoreType.DMA((n,))],
    )(*grads)


def _allsum_small(pack):
    rows, cols = pack.shape
    ndev = 8

    def body(p_ref, o_ref, buf, send, recv):
        x, y, c = lax.axis_index("x"), lax.axis_index("y"), lax.axis_index("c")
        me = 4 * x + 2 * y + c
        buf[me] = p_ref[...]
        started = []
        for r in range(1, ndev):
            bx, by, bc = (r >> 2) & 1, (r >> 1) & 1, r & 1
            dev = (x ^ bx, y ^ by, c ^ bc)
            cp = pltpu.make_async_remote_copy(p_ref, buf.at[me], send.at[r], recv.at[r],
                                              device_id=dev, device_id_type=MESH)
            cp.start()
            started.append(cp)
        for r in range(1, ndev):
            pltpu.make_async_remote_copy(p_ref, buf.at[me ^ r], send.at[r], recv.at[r],
                                         device_id=(x, y, c), device_id_type=MESH).wait_recv()
        for cp in started:
            cp.wait_send()
        acc = buf[0]
        for d in range(1, ndev):
            acc = acc + buf[d]
        o_ref[...] = acc

    vm = pl.BlockSpec(memory_space=pltpu.VMEM)
    return _pcall(
        body, name="allsum_small", in_specs=[vm], out_specs=vm,
        out_shape=jax.ShapeDtypeStruct((rows, cols), F32),
        scratch_shapes=[pltpu.VMEM((ndev, rows, cols), F32), pltpu.SemaphoreType.DMA((ndev,)),
                        pltpu.SemaphoreType.DMA((ndev,))],
    )(pack)


def _adamw(parts, w, m, v, name):
    rows, cols = w.shape
    tr = ROW_TILE if rows % ROW_TILE == 0 else rows
    counts = [p.shape[0] for p in parts]

    def body(*refs):
        part_refs = refs[:len(parts)]
        w_ref, m_ref, v_ref, g_ref, d_ref, nm_ref, nv_ref = refs[len(parts):]
        g = None
        for p_ref, cnt in zip(part_refs, counts):
            s = p_ref[0]
            for k in range(1, cnt):
                s = s + p_ref[k]
            g = s if g is None else g + s
        m2 = ADAM_B1 * m_ref[...] + (1.0 - ADAM_B1) * g
        v2 = ADAM_B2 * v_ref[...] + (1.0 - ADAM_B2) * (g * g)
        m_hat = m2 / (1.0 - ADAM_B1 ** ADAM_STEP)
        v_hat = v2 / (1.0 - ADAM_B2 ** ADAM_STEP)
        g_ref[...] = g
        d_ref[...] = -ADAM_LR * (m_hat / (jnp.sqrt(v_hat) + ADAM_EPS) + ADAM_WD * w_ref[...])
        nm_ref[...] = m2
        nv_ref[...] = v2

    blk = pl.BlockSpec((tr, cols), lambda i: (i, 0))
    shp = jax.ShapeDtypeStruct((rows, cols), F32)
    return _pcall(
        body, name=name, grid=(rows // tr,),
        in_specs=[pl.BlockSpec((cnt, tr, cols), lambda i: (0, i, 0)) for cnt in counts] + [blk] * 3,
        out_specs=[blk] * 4, out_shape=[shp] * 4,
        compiler_params=_params("parallel"),
    )(*parts, w, m, v)


def kernel(x, meta_tokens, pre_norm_g, post_norm_g, w_in, conv_w, conv_b, conv_ln_g, conv_ln_b, w_pw2, b_pw2, w_out, loss_target, m_meta_tokens, m_pre_norm_g, m_post_norm_g, m_w_in, m_conv_w, m_conv_b, m_conv_ln_g, m_conv_ln_b, m_w_pw2, m_b_pw2, m_w_out, v_meta_tokens, v_pre_norm_g, v_post_norm_g, v_w_in, v_conv_w, v_conv_b, v_conv_ln_g, v_conv_ln_b, v_w_pw2, v_b_pw2, v_w_out):
    seq, d = x.shape[1], x.shape[2]
    depth = w_in.shape[0]
    length = N_META + seq
    lp = -(-length // ATT_BLOCK) * ATT_BLOCK
    tap_pad = ((0, 0), (0, CONV_PAD - CONV_WIDTH), (0, 0))

    shard_dims = [(2, w_in.shape[2]), (1, w_pw2.shape[1]), (1, w_out.shape[1]), (2, conv_w.shape[2]),
                  (1, meta_tokens.shape[1])]
    w_in_f, w_pw2_f, w_out_f, conv_w_f, meta_f = _gather_weights(
        [w_in.astype(BF16), w_pw2.astype(BF16), w_out.astype(BF16), jnp.pad(conv_w, tap_pad), meta_tokens], shard_dims)

    h0 = jnp.concatenate([meta_f, x[0], jnp.zeros((lp - length, d), F32)], axis=0)
    target_p = jnp.pad(loss_target[0], ((N_META, lp - length), (0, 0)))
    loss, dh0, g = _local_step(h0, target_p, seq, pre_norm_g, post_norm_g, w_in_f, conv_w_f, conv_b, conv_ln_g,
                               conv_ln_b, w_pw2_f, b_pw2, w_out_f)

    pieces = _reduce_grads([g["w_in"], g["w_pw2"], g["w_out"], g["conv_w"]], shard_dims[:4])
    mine, theirs = pieces[:4], pieces[4:]

    def update(k, w, m, v, name):
        shp = w.shape
        flat = lambda a: a.reshape(-1, shp[-1])
        parts = [p.reshape(N_CHIPS, -1, shp[-1]) for p in (mine[k], theirs[k])]
        return [o.reshape(shp) for o in _adamw(parts, flat(w), flat(m), flat(v), name)]

    up_w_in = update(0, w_in, m_w_in, v_w_in, "adamw_w_in")
    up_w_pw2 = update(1, w_pw2, m_w_pw2, v_w_pw2, "adamw_w_pw2")
    up_w_out = update(2, w_out, m_w_out, v_w_out, "adamw_w_out")
    up_conv_w = [o[:, :CONV_WIDTH] for o in update(3, jnp.pad(conv_w, tap_pad), jnp.pad(m_conv_w, tap_pad),
                                                   jnp.pad(v_conv_w, tap_pad, constant_values=1.0), "adamw_conv_w")]

    two = lambda a: a.reshape(-1, d)
    vec_rows = [g["pre_norm_g"], g["post_norm_g"], two(g["conv_b"]), two(g["conv_ln_g"]), two(g["conv_ln_b"]),
                two(g["b_pw2"])]
    n_vec = sum(a.shape[0] for a in vec_rows)
    pack = jnp.concatenate(vec_rows + [dh0[:N_META], jnp.full((8, d), loss, F32)], axis=0)
    pack = jnp.pad(pack, ((0, -pack.shape[0] % 8), (0, 0)))
    tot = _allsum_small(pack)
    loss_all = tot[n_vec + N_META, 0]

    cat = lambda *a: jnp.concatenate([two(t) for t in a], axis=0)
    small = (pre_norm_g, post_norm_g, conv_b, conv_ln_g, conv_ln_b, b_pw2)
    small_m = (m_pre_norm_g, m_post_norm_g, m_conv_b, m_conv_ln_g, m_conv_ln_b, m_b_pw2)
    small_v = (v_pre_norm_g, v_post_norm_g, v_conv_b, v_conv_ln_g, v_conv_ln_b, v_b_pw2)
    up_small = _adamw([tot[None, :n_vec]], cat(*small), cat(*small_m), cat(*small_v), "adamw_vectors")

    def unpack(o):
        res, r0 = [], 0
        for t in small:
            nrow = t.size // d
            res.append(o[r0:r0 + nrow].reshape(t.shape))
            r0 += nrow
        return res

    up_small = [unpack(o) for o in up_small]
    chip = 2 * lax.axis_index("x") + lax.axis_index("y")
    mcols = meta_tokens.shape[1]
    g_meta = lax.dynamic_slice_in_dim(tot[n_vec:n_vec + N_META], chip * mcols, mcols, axis=1)
    up_meta = _adamw([g_meta[None]], meta_tokens, m_meta_tokens, v_meta_tokens, "adamw_meta")

    grad_x = dh0[N_META:length][None]
    outs = [loss_all, grad_x]
    for j in range(4):
        pre, post, cb, lg, lb, bp = up_small[j]
        outs += [up_meta[j], pre, post, up_w_in[j], up_conv_w[j], cb, lg, lb, up_w_pw2[j], bp, up_w_out[j]]
    return tuple(outs)
```

```python
import functools

import jax
import jax.numpy as jnp
from jax import lax
from jax.experimental import pallas as pl
from jax.experimental.pallas import tpu as pltpu

F32 = jnp.float32
BF16 = jnp.bfloat16

N_META = 16
D_CONV = 512
D_SB = 512
HEAD_DIM = 64
CONV_WIDTH = 31
CONV_PAD = 32
RMS_EPS = 1e-6
LN_EPS = 1e-5
Q_SCALE = HEAD_DIM ** -0.5

ADAM_LR = 0.001
ADAM_B1 = 0.9
ADAM_B2 = 0.999
ADAM_EPS = 1e-08
ADAM_WD = 0.01
ADAM_STEP = 10

LANES = 128
ROW_TILE = 256
ATT_BLOCK = 256
VMEM_LIMIT = 56 * 1024 * 1024
EXP_ZERO = -104.0
COUNT_LANE = LANES - 1

MESH = pl.DeviceIdType.MESH


def _pcall(body, **kw):
    return pl.pallas_call(body, **kw)


def _params(*sem):
    return pltpu.CompilerParams(dimension_semantics=sem, vmem_limit_bytes=VMEM_LIMIT)


def _sigmoid(x):
    return 1.0 / (1.0 + jnp.exp(-x))


def _silu_fwd_bwd(x):
    s = _sigmoid(x)
    return x * s, s * (1.0 + x * (1.0 - s))


def _nt(a, b):
    return lax.dot_general(a, b, (((1,), (1,)), ((), ())), preferred_element_type=F32)


def _tn(a, b):
    return lax.dot_general(a, b, (((0,), (0,)), ((), ())), preferred_element_type=F32)


def _nn(a, b):
    return jnp.dot(a, b, preferred_element_type=F32)


def _inproj(h, g, w):
    lp, d = h.shape
    n = w.shape[1]

    def body(h_ref, g_ref, w_ref, ew_ref, qkv_ref):
        x = h_ref[...]
        rstd = lax.rsqrt(jnp.mean(x * x, axis=-1, keepdims=True) + RMS_EPS)
        u = ((x * rstd) * g_ref[...]).astype(BF16)
        p = _nn(u, w_ref[...])
        ew_ref[:, 0:1536] = p[:, 0:1536]
        ew_ref[:, 1536:2048] = p[:, 3072:3584]
        qkv_ref[:, 0:512] = (p[:, 1536:2048] * Q_SCALE).astype(BF16)
        qkv_ref[:, 512:1536] = p[:, 2048:3072].astype(BF16)

    return _pcall(
        body, name="inproj_fwd", grid=(lp // ROW_TILE,),
        in_specs=[pl.BlockSpec((ROW_TILE, d), lambda i: (i, 0)),
                  pl.BlockSpec((1, d), lambda i: (0, 0)),
                  pl.BlockSpec((d, n), lambda i: (0, 0))],
        out_specs=[pl.BlockSpec((ROW_TILE, 2048), lambda i: (i, 0)),
                   pl.BlockSpec((ROW_TILE, 1536), lambda i: (i, 0))],
        out_shape=[jax.ShapeDtypeStruct((lp, 2048), F32), jax.ShapeDtypeStruct((lp, 1536), BF16)],
        compiler_params=_params("parallel"),
    )(h, g, w)


def _layer_norm_stats(c1):
    mu = jnp.mean(c1, axis=-1, keepdims=True)
    xc = c1 - mu
    var = jnp.mean(xc * xc, axis=-1, keepdims=True)
    rstd = lax.rsqrt(var + LN_EPS)
    return xc * rstd, rstd


def _conv_fwd(ew, cw, cb, lng, lnb, wpw2, bpw2):
    lp = ew.shape[0]
    tm = ROW_TILE

    def body(ew_ref, cw_ref, cb_ref, lng_ref, lnb_ref, w_ref, b_ref, c1_ref, c4_ref, c5_ref, xbuf):
        @pl.when(pl.program_id(0) == 0)
        def _():
            xbuf[0:CONV_PAD, :] = jnp.zeros((CONV_PAD, D_CONV), F32)

        ga = ew_ref[:, 0:512]
        gb = ew_ref[:, 512:1024]
        cg = ew_ref[:, 1024:1536]
        xbuf[CONV_PAD:CONV_PAD + tm, :] = ga * _sigmoid(gb)
        acc = jnp.zeros((tm, D_CONV), F32) + cb_ref[...]
        for j in range(CONV_WIDTH):
            acc = acc + cw_ref[j:j + 1, :] * xbuf[pl.ds(CONV_PAD - (CONV_WIDTH - 1) + j, tm), :]
        c1_ref[...] = acc
        xbuf[0:CONV_PAD, :] = xbuf[tm:tm + CONV_PAD, :]
        xhat, _ = _layer_norm_stats(acc)
        c2 = xhat * lng_ref[...] + lnb_ref[...]
        c3 = c2 * _sigmoid(c2)
        c4 = _nn(c3.astype(BF16), w_ref[...]) + b_ref[...]
        c4_ref[...] = c4
        c5_ref[...] = (c4 * (cg * _sigmoid(cg))).astype(BF16)

    vec = pl.BlockSpec((1, D_CONV), lambda i: (0, 0))
    row = pl.BlockSpec((tm, D_CONV), lambda i: (i, 0))
    return _pcall(
        body, name="conv_fwd", grid=(lp // tm,),
        in_specs=[pl.BlockSpec((tm, 1536), lambda i: (i, 0)),
                  pl.BlockSpec((CONV_PAD, D_CONV), lambda i: (0, 0)),
                  vec, vec, vec,
                  pl.BlockSpec((D_CONV, D_CONV), lambda i: (0, 0)),
                  vec],
        out_specs=[row, row, row],
        out_shape=[jax.ShapeDtypeStruct((lp, D_CONV), F32), jax.ShapeDtypeStruct((lp, D_CONV), F32),
                   jax.ShapeDtypeStruct((lp, D_CONV), BF16)],
        scratch_shapes=[pltpu.VMEM((tm + CONV_PAD, D_CONV), F32)],
        compiler_params=_params("arbitrary"),
    )(ew, cw, cb, lng, lnb, wpw2, bpw2)


def _split_matmul(x, m01):
    hi = x.astype(BF16)
    lo = (x - hi.astype(F32)).astype(BF16)
    t = _nn(jnp.concatenate([hi, lo], axis=0), m01)
    m = x.shape[0]
    return t[:m] + t[m:]


def _sb_block(q2, kj, z_mask, tri):
    z = _nt(q2, kj)
    ls = -(jnp.maximum(z, 0.0) + jnp.log(1.0 + jnp.exp(-jnp.abs(z))))
    if z_mask is not None:
        ls = jnp.where(z_mask, ls, 0.0)
    return z, ls, _split_matmul(ls, tri)


def _attn_masks():
    lane = lax.broadcasted_iota(jnp.int32, (1, LANES), 1)
    row = lax.broadcasted_iota(jnp.int32, (2 * ATT_BLOCK, ATT_BLOCK), 0)
    col = lax.broadcasted_iota(jnp.int32, (2 * ATT_BLOCK, ATT_BLOCK), 1)
    return lane < HEAD_DIM, col < (row & (ATT_BLOCK - 1))


def _stack_heads(x, first_head):
    zero = jnp.zeros_like(x)
    return jnp.concatenate([jnp.where(first_head, x, zero), jnp.where(first_head, zero, x)], axis=0)


def _unstack_heads(x2, first_head):
    rows = x2.shape[0] // 2
    return jnp.where(first_head, x2[:rows], x2[rows:])


def _attn_fwd(qkv, tri):
    lp = qkv.shape[0]
    bq = ATT_BLOCK

    assert lp // bq <= COUNT_LANE

    def body(q_ref, k_ref, v_ref, tri_ref, o_ref, carry_ref):
        i = pl.program_id(1)
        first_head, vis = _attn_masks()
        lane = lax.broadcasted_iota(jnp.int32, (1, LANES), 1)
        q2 = _stack_heads(q_ref[...], first_head)
        tri_m = tri_ref[...]

        def blocks(js, state, masks):
            c, acc, cmat = state
            offs = [pl.multiple_of(j * bq, bq) for j in js]
            zs = [_nt(q2, k_ref[pl.ds(off, bq), :]) for off in offs]
            lss = []
            for z, mask in zip(zs, masks):
                ls = -(jnp.maximum(z, 0.0) + jnp.log(1.0 + jnp.exp(-jnp.abs(z))))
                lss.append(ls if mask is None else jnp.where(mask, ls, 0.0))
            tails = [_split_matmul(ls, tri_m) for ls in lss]
            probs = []
            for j, z, ls, tail, mask in zip(js, zs, lss, tails, masks):
                a = jnp.exp(z + ls + tail + c)
                probs.append(a if mask is None else jnp.where(mask, a, 0.0))
                cmat = jnp.where(lane == j, c, cmat)
                c = c + tail[:, 0:1] + ls[:, 0:1]
            for off, a in zip(offs, probs):
                acc = acc + _nn(a.astype(BF16), v_ref[pl.ds(off, bq), :])
            return c, acc, cmat

        def live(state):
            return jnp.max(state[0]) > EXP_ZERO

        zeros = jnp.zeros((2 * bq, LANES), F32)
        init = (jnp.zeros((2 * bq, 1), F32), zeros, zeros)
        state = lax.cond(i > 0, lambda st: blocks([i, i - 1], st, [vis, None]),
                         lambda st: blocks([i], st, [vis]), init)
        rest = jnp.maximum(i - 1, 0)

        def pair(carry):
            t, st = carry
            j = i - 2 - 2 * t
            return t + 1, blocks([j, j - 1], st, [None, None])

        trips, state = lax.while_loop(lambda ca: jnp.logical_and(ca[0] < rest // 2, live(ca[1])), pair, (0, state))
        last = jnp.logical_and(jnp.logical_and(rest % 2 == 1, trips == rest // 2), live(state))
        state = lax.cond(last, lambda st: blocks([0], st, [None]), lambda st: st, state)
        n_done = jnp.minimum(i + 1, 2) + 2 * trips + last.astype(jnp.int32)
        cmat = jnp.where(lane == COUNT_LANE, n_done.astype(F32), state[2])
        carry_ref[:, 0:LANES] = cmat[:bq]
        carry_ref[:, LANES:] = cmat[bq:]
        o_ref[...] = _unstack_heads(state[1], first_head)

    npair = D_SB // LANES
    return _pcall(
        body, name="attn_fwd", grid=(npair, lp // bq),
        in_specs=[pl.BlockSpec((bq, LANES), lambda p, i: (i, p)),
                  pl.BlockSpec((lp, LANES), lambda p, i: (0, npair + p)),
                  pl.BlockSpec((lp, LANES), lambda p, i: (0, 2 * npair + p)),
                  pl.BlockSpec((bq, bq), lambda p, i: (0, 0))],
        out_specs=[pl.BlockSpec((bq, LANES), lambda p, i: (i, p)),
                   pl.BlockSpec((bq, 2 * LANES), lambda p, i: (i, p))],
        out_shape=[jax.ShapeDtypeStruct((lp, D_SB), F32), jax.ShapeDtypeStruct((lp, 2 * D_SB), F32)],
        compiler_params=_params("parallel", "parallel"),
    )(qkv, qkv, qkv, tri)


def _outproj(c5, att, ew, h, w, g):
    lp, d = h.shape
    tm = ROW_TILE

    def body(c5_ref, att_ref, sg_ref, h_ref, w_ref, g_ref, hn_ref, cat_ref, mix_ref):
        sg = sg_ref[...]
        s = att_ref[...] * (sg * _sigmoid(sg))
        cat_ref[:, 0:D_CONV] = c5_ref[...]
        cat_ref[:, D_CONV:] = s.astype(BF16)
        mixed = _nn(cat_ref[...], w_ref[...])
        mix_ref[...] = mixed
        rstd = lax.rsqrt(jnp.mean(mixed * mixed, axis=-1, keepdims=True) + RMS_EPS)
        hn_ref[...] = h_ref[...] + (mixed * rstd) * g_ref[...]

    half = pl.BlockSpec((tm, 512), lambda i: (i, 0))
    full = pl.BlockSpec((tm, d), lambda i: (i, 0))
    return _pcall(
        body, name="outproj_fwd", grid=(lp // tm,),
        in_specs=[half, half, pl.BlockSpec((tm, 512), lambda i: (i, 3)), full,
                  pl.BlockSpec((d, d), lambda i: (0, 0)), pl.BlockSpec((1, d), lambda i: (0, 0))],
        out_specs=[full, full, full],
        out_shape=[jax.ShapeDtypeStruct((lp, d), F32), jax.ShapeDtypeStruct((lp, d), BF16),
                   jax.ShapeDtypeStruct((lp, d), F32)],
        compiler_params=_params("parallel"),
    )(c5, att, ew, h, w, g)


def _loss_head(h, target, seq):
    lp, d = h.shape
    tm = ROW_TILE

    def body(h_ref, t_ref, dh_ref, loss_ref):
        i = pl.program_id(0)

        @pl.when(i == 0)
        def _():
            loss_ref[...] = jnp.zeros_like(loss_ref)

        row = i * tm + lax.broadcasted_iota(jnp.int32, (tm, 1), 0)
        real = jnp.logical_and(row >= N_META, row < N_META + seq)
        diff = jnp.where(real, h_ref[...] - t_ref[...], 0.0)
        dh_ref[...] = diff * (1.0 / d)
        loss_ref[...] += 0.5 * jnp.sum(jnp.sum(diff * diff, axis=-1, keepdims=True) * (1.0 / d))

    full = pl.BlockSpec((tm, d), lambda i: (i, 0))
    return _pcall(
        body, name="loss_head", grid=(lp // tm,),
        in_specs=[full, full],
        out_specs=[full, pl.BlockSpec((8, LANES), lambda i: (0, 0))],
        out_shape=[jax.ShapeDtypeStruct((lp, d), F32), jax.ShapeDtypeStruct((8, LANES), F32)],
        compiler_params=_params("arbitrary"),
    )(h, target)


def _outproj_bwd(dh, mixed, g, w, att, ew, c4):
    lp, d = dh.shape
    tm = ROW_TILE

    def body(dh_ref, mix_ref, g_ref, w_ref, att_ref, cg_ref, sg_ref, c4_ref,
             dmix_ref, datt_ref, dsg_ref, dc4_ref, dcg_ref, dg_ref, db_ref):
        @pl.when(pl.program_id(0) == 0)
        def _():
            dg_ref[...] = jnp.zeros_like(dg_ref)
            db_ref[...] = jnp.zeros_like(db_ref)

        mixed = mix_ref[...]
        dhv = dh_ref[...]
        rstd = lax.rsqrt(jnp.mean(mixed * mixed, axis=-1, keepdims=True) + RMS_EPS)
        n = mixed * rstd
        dg_ref[...] += jnp.sum(dhv * n, axis=0, keepdims=True)
        dn = dhv * g_ref[...]
        dmix = (rstd * (dn - n * jnp.mean(dn * n, axis=-1, keepdims=True))).astype(BF16)
        dmix_ref[...] = dmix
        dcat = _nt(dmix, w_ref[...])
        dc5 = dcat[:, 0:D_CONV]
        ds = dcat[:, D_CONV:]
        silu_sg, dsilu_sg = _silu_fwd_bwd(sg_ref[...])
        datt_ref[...] = (ds * silu_sg).astype(BF16)
        dsg_ref[...] = (ds * att_ref[...] * dsilu_sg).astype(BF16)
        silu_cg, dsilu_cg = _silu_fwd_bwd(cg_ref[...])
        dc4 = dc5 * silu_cg
        db_ref[...] += jnp.sum(dc4, axis=0, keepdims=True)
        dc4_ref[...] = dc4.astype(BF16)
        dcg_ref[...] = (dc5 * c4_ref[...] * dsilu_cg).astype(BF16)

    half = pl.BlockSpec((tm, 512), lambda i: (i, 0))
    full = pl.BlockSpec((tm, d), lambda i: (i, 0))
    hb = jax.ShapeDtypeStruct((lp, 512), BF16)
    return _pcall(
        body, name="outproj_bwd", grid=(lp // tm,),
        in_specs=[full, full, pl.BlockSpec((1, d), lambda i: (0, 0)), pl.BlockSpec((d, d), lambda i: (0, 0)),
                  half, pl.BlockSpec((tm, 512), lambda i: (i, 2)), pl.BlockSpec((tm, 512), lambda i: (i, 3)), half],
        out_specs=[full, half, half, half, half,
                   pl.BlockSpec((1, d), lambda i: (0, 0)), pl.BlockSpec((1, 512), lambda i: (0, 0))],
        out_shape=[jax.ShapeDtypeStruct((lp, d), BF16), hb, hb, hb, hb,
                   jax.ShapeDtypeStruct((1, d), F32), jax.ShapeDtypeStruct((1, 512), F32)],
        compiler_params=_params("arbitrary"),
    )(dh, mixed, g, w, att, ew, ew, c4)


def _attn_bwd(qkv, carries, datt, tri, upper):
    lp = qkv.shape[0]
    bq = ATT_BLOCK

    def body(q_ref, k_ref, v_ref, carry_ref, do_ref, tri_ref, upper_ref, dq_ref, dk_ref, dv_ref):
        i = pl.program_id(1)

        @pl.when(i == 0)
        def _():
            dk_ref[...] = jnp.zeros_like(dk_ref)
            dv_ref[...] = jnp.zeros_like(dv_ref)

        first_head, vis = _attn_masks()
        lane = lax.broadcasted_iota(jnp.int32, (1, LANES), 1)
        q2 = _stack_heads(q_ref[...], first_head)
        do2 = _stack_heads(do_ref[...], first_head)
        cmat = jnp.concatenate([carry_ref[:, 0:LANES], carry_ref[:, LANES:]], axis=0)
        tri_m = tri_ref[...]
        upper_m = upper_ref[...]

        def blocks(js, state, masks):
            run, dq = state
            offs = [pl.multiple_of(j * bq, bq) for j in js]
            zs = [_nt(q2, k_ref[pl.ds(off, bq), :]) for off in offs]
            lss = []
            for z, mask in zip(zs, masks):
                ls = -(jnp.maximum(z, 0.0) + jnp.log(1.0 + jnp.exp(-jnp.abs(z))))
                lss.append(ls if mask is None else jnp.where(mask, ls, 0.0))
            tails = [_split_matmul(ls, tri_m) for ls in lss]
            das = [_nt(do2, v_ref[pl.ds(off, bq), :]) for off in offs]
            probs, des = [], []
            for j, z, ls, tail, da, mask in zip(js, zs, lss, tails, das, masks):
                c = jnp.sum(jnp.where(lane == j, cmat, 0.0), axis=-1, keepdims=True)
                a = jnp.exp(z + ls + tail + c)
                a = a if mask is None else jnp.where(mask, a, 0.0)
                probs.append(a.astype(BF16))
                des.append(da * a)
            prefixes = [_split_matmul(de, upper_m) for de in des]
            dzs = []
            for z, ls, de, prefix, mask in zip(zs, lss, des, prefixes, masks):
                beta = jnp.exp(z + ls)
                dz = de - beta * (de + run + prefix)
                dzs.append((dz if mask is None else jnp.where(mask, dz, 0.0)).astype(BF16))
                run = run + prefix[:, bq - 1:bq] + de[:, bq - 1:bq]
            for off, dzb, ab in zip(offs, dzs, probs):
                dq = dq + _nn(dzb, k_ref[pl.ds(off, bq), :])
                dk_ref[pl.ds(off, bq), :] += _tn(dzb, q2)
                dv_ref[pl.ds(off, bq), :] += _tn(ab, do2)
            return run, dq

        n_done = jnp.max(carry_ref[:, COUNT_LANE:COUNT_LANE + 1]).astype(jnp.int32)
        n_done = jnp.clip(n_done, 1, i + 1)
        before = jnp.maximum(n_done - 2, 0)
        j0 = i - n_done + 1
        odd = before % 2
        state = (jnp.zeros((2 * bq, 1), F32), jnp.zeros((2 * bq, LANES), F32))
        state = lax.cond(odd == 1, lambda st: blocks([j0], st, [None]), lambda st: st, state)
        state = lax.fori_loop(
            0, before // 2, lambda t, st: blocks([j0 + odd + 2 * t, j0 + odd + 2 * t + 1], st, [None, None]), state)
        state = lax.cond(n_done > 1, lambda st: blocks([i - 1, i], st, [None, vis]),
                         lambda st: blocks([i], st, [vis]), state)
        dq_ref[...] = (_unstack_heads(state[1], first_head) * Q_SCALE).astype(BF16)

    npair = D_SB // LANES
    qb = pl.BlockSpec((bq, LANES), lambda p, i: (i, p))
    colb = pl.BlockSpec((lp, LANES), lambda p, i: (0, p))
    sq = pl.BlockSpec((bq, bq), lambda p, i: (0, 0))
    return _pcall(
        body, name="attn_bwd", grid=(npair, lp // bq),
        in_specs=[qb,
                  pl.BlockSpec((lp, LANES), lambda p, i: (0, npair + p)),
                  pl.BlockSpec((lp, LANES), lambda p, i: (0, 2 * npair + p)),
                  pl.BlockSpec((bq, 2 * LANES), lambda p, i: (i, p)), qb, sq, sq],
        out_specs=[qb, colb, colb],
        out_shape=[jax.ShapeDtypeStruct((lp, D_SB), BF16), jax.ShapeDtypeStruct((lp, D_SB), F32),
                   jax.ShapeDtypeStruct((lp, D_SB), F32)],
        compiler_params=_params("parallel", "arbitrary"),
    )(qkv, qkv, qkv, carries, datt, tri, upper)


def _conv_bwd(dc4, c1, ew, cw, lng, lnb, wpw2):
    lp = ew.shape[0]
    tm = ROW_TILE
    nt = lp // tm
    halo_per_tile = tm // CONV_PAD

    def body(dc4_ref, c1_ref, ew_ref, halo_ref, cw_ref, lng_ref, lnb_ref, w_ref,
             dga_ref, dgb_ref, c3_ref, dcw_ref, dcb_ref, dlng_ref, dlnb_ref, xbuf, dbuf):
        step = pl.program_id(0)

        @pl.when(step == 0)
        def _():
            dcw_ref[...] = jnp.zeros_like(dcw_ref)
            dcb_ref[...] = jnp.zeros_like(dcb_ref)
            dlng_ref[...] = jnp.zeros_like(dlng_ref)
            dlnb_ref[...] = jnp.zeros_like(dlnb_ref)
            dbuf[tm:tm + CONV_PAD, :] = jnp.zeros((CONV_PAD, D_CONV), F32)

        dc3 = _nt(dc4_ref[...], w_ref[...])
        xhat, rstd = _layer_norm_stats(c1_ref[...])
        c2 = xhat * lng_ref[...] + lnb_ref[...]
        c3, dsilu = _silu_fwd_bwd(c2)
        c3_ref[...] = c3.astype(BF16)
        dc2 = dc3 * dsilu
        dlng_ref[...] += jnp.sum(dc2 * xhat, axis=0, keepdims=True)
        dlnb_ref[...] += jnp.sum(dc2, axis=0, keepdims=True)
        dxhat = dc2 * lng_ref[...]
        dc1 = rstd * (dxhat - jnp.mean(dxhat, axis=-1, keepdims=True)
                      - xhat * jnp.mean(dxhat * xhat, axis=-1, keepdims=True))
        dcb_ref[...] += jnp.sum(dc1, axis=0, keepdims=True)
        dbuf[0:tm, :] = dc1

        ga = ew_ref[:, 0:512]
        sgb = _sigmoid(ew_ref[:, 512:1024])
        xbuf[CONV_PAD:CONV_PAD + tm, :] = ga * sgb
        first_tile = step == nt - 1
        halo = halo_ref[:, 0:512] * _sigmoid(halo_ref[:, 512:1024])
        xbuf[0:CONV_PAD, :] = jnp.where(first_tile, 0.0, halo)

        dc0 = jnp.zeros((tm, D_CONV), F32)
        for j in range(CONV_WIDTH):
            dc0 = dc0 + cw_ref[j:j + 1, :] * dbuf[pl.ds(CONV_WIDTH - 1 - j, tm), :]
            tap = dc1 * xbuf[pl.ds(CONV_PAD - (CONV_WIDTH - 1) + j, tm), :]
            dcw_ref[j:j + 1, :] += jnp.sum(tap, axis=0, keepdims=True)
        dbuf[tm:tm + CONV_PAD, :] = dbuf[0:CONV_PAD, :]
        dga_ref[...] = (dc0 * sgb).astype(BF16)
        dgb_ref[...] = (dc0 * ga * sgb * (1.0 - sgb)).astype(BF16)

    rev = lambda i: (nt - 1 - i, 0)
    row = pl.BlockSpec((tm, D_CONV), rev)
    vec = pl.BlockSpec((1, D_CONV), lambda i: (0, 0))
    hb = jax.ShapeDtypeStruct((lp, D_CONV), BF16)
    vs = jax.ShapeDtypeStruct((1, D_CONV), F32)
    return _pcall(
        body, name="conv_bwd", grid=(nt,),
        in_specs=[row, row, pl.BlockSpec((tm, 1024), rev),
                  pl.BlockSpec((CONV_PAD, 1024), lambda i: (jnp.maximum((nt - 1 - i) * halo_per_tile - 1, 0), 0)),
                  pl.BlockSpec((CONV_PAD, D_CONV), lambda i: (0, 0)), vec, vec,
                  pl.BlockSpec((D_CONV, D_CONV), lambda i: (0, 0))],
        out_specs=[row, row, row, pl.BlockSpec((CONV_PAD, D_CONV), lambda i: (0, 0)), vec, vec, vec],
        out_shape=[hb, hb, hb, jax.ShapeDtypeStruct((CONV_PAD, D_CONV), F32), vs, vs, vs],
        scratch_shapes=[pltpu.VMEM((tm + CONV_PAD, D_CONV), F32), pltpu.VMEM((tm + CONV_PAD, D_CONV), F32)],
        compiler_params=_params("arbitrary"),
    )(dc4, c1, ew, ew, cw, lng, lnb, wpw2)


def _inproj_bwd(dga, dgb, dcg, dq, dk, dv, dsg, h, g, w, dh_out):
    lp, d = h.shape
    n = w.shape[1]
    tm = ROW_TILE

    def body(dga_ref, dgb_ref, dcg_ref, dq_ref, dk_ref, dv_ref, dsg_ref, h_ref, g_ref, w_ref, dho_ref,
             dh_ref, dproj_ref, u_ref, dg_ref):
        @pl.when(pl.program_id(0) == 0)
        def _():
            dg_ref[...] = jnp.zeros_like(dg_ref)

        dproj_ref[:, 0:512] = dga_ref[...]
        dproj_ref[:, 512:1024] = dgb_ref[...]
        dproj_ref[:, 1024:1536] = dcg_ref[...]
        dproj_ref[:, 1536:2048] = dq_ref[...]
        dproj_ref[:, 2048:2560] = dk_ref[...].astype(BF16)
        dproj_ref[:, 2560:3072] = dv_ref[...].astype(BF16)
        dproj_ref[:, 3072:3584] = dsg_ref[...]
        du = _nt(dproj_ref[...], w_ref[...])
        x = h_ref[...]
        rstd = lax.rsqrt(jnp.mean(x * x, axis=-1, keepdims=True) + RMS_EPS)
        nrm = x * rstd
        u_ref[...] = (nrm * g_ref[...]).astype(BF16)
        dg_ref[...] += jnp.sum(du * nrm, axis=0, keepdims=True)
        dn = du * g_ref[...]
        dh_ref[...] = dho_ref[...] + rstd * (dn - nrm * jnp.mean(dn * nrm, axis=-1, keepdims=True))

    half = pl.BlockSpec((tm, 512), lambda i: (i, 0))
    full = pl.BlockSpec((tm, d), lambda i: (i, 0))
    return _pcall(
        body, name="inproj_bwd", grid=(lp // tm,),
        in_specs=[half] * 7 + [full, pl.BlockSpec((1, d), lambda i: (0, 0)),
                               pl.BlockSpec((d, n), lambda i: (0, 0)), full],
        out_specs=[full, pl.BlockSpec((tm, n), lambda i: (i, 0)), full, pl.BlockSpec((1, d), lambda i: (0, 0))],
        out_shape=[jax.ShapeDtypeStruct((lp, d), F32), jax.ShapeDtypeStruct((lp, n), BF16),
                   jax.ShapeDtypeStruct((lp, d), BF16), jax.ShapeDtypeStruct((1, d), F32)],
        compiler_params=_params("arbitrary"),
    )(dga, dgb, dcg, dq, dk, dv, dsg, h, g, w, dh_out)


def _row_split(m, parts):
    tm = m // parts
    assert tm * parts == m and tm % 16 == 0, (m, parts)
    return tm


def _matmul_tn(x, dy, tn, name):
    m, k = x.shape
    n = dy.shape[1]
    steps = 4 if m % 64 == 0 else 1
    tm = _row_split(m, steps)

    def body(x_ref, dy_ref, o_ref, acc_ref):
        r = pl.program_id(1)

        @pl.when(r == 0)
        def _():
            acc_ref[...] = jnp.zeros_like(acc_ref)

        acc_ref[...] += _tn(x_ref[...], dy_ref[...])

        @pl.when(r == steps - 1)
        def _():
            o_ref[...] = acc_ref[...].astype(BF16)

    return _pcall(
        body, name=name, grid=(n // tn, steps),
        in_specs=[pl.BlockSpec((tm, k), lambda j, r: (r, 0)), pl.BlockSpec((tm, tn), lambda j, r: (r, j))],
        out_specs=pl.BlockSpec((k, tn), lambda j, r: (0, j)),
        out_shape=jax.ShapeDtypeStruct((k, n), BF16),
        scratch_shapes=[pltpu.VMEM((k, tn), F32)],
        compiler_params=_params("parallel", "arbitrary"),
    )(x, dy)


def _local_step(h0, target_p, seq, pre_g, post_g, w_in, conv_w, conv_b, ln_g, ln_b, w_pw2, b_pw2, w_out):
    depth = w_in.shape[0]
    ar = jnp.arange(ATT_BLOCK)
    tri = (ar[:, None] > ar[None, :]).astype(BF16)
    upper = (ar[:, None] < ar[None, :]).astype(BF16)
    row = lambda a, l: a[l][None, :]

    saved = []
    h = h0
    for l in range(depth):
        ew, qkv = _inproj(h, row(pre_g, l), w_in[l])
        c1, c4, c5 = _conv_fwd(ew, conv_w[l], row(conv_b, l), row(ln_g, l), row(ln_b, l), w_pw2[l], row(b_pw2, l))
        att, carries = _attn_fwd(qkv, tri)
        hn, cat, mixed = _outproj(c5, att, ew, h, w_out[l], row(post_g, l))
        saved.append((h, ew, qkv, c1, c4, att, carries, cat, mixed))
        h = hn

    dh, loss = _loss_head(h, target_p, seq)

    grads = [None] * depth
    for l in reversed(range(depth)):
        h_in, ew, qkv, c1, c4, att, carries, cat, mixed = saved[l]
        dmix, datt, dsg, dc4, dcg, dpost, dbpw2 = _outproj_bwd(dh, mixed, row(post_g, l), w_out[l], att, ew, c4)
        dw_out = _matmul_tn(cat, dmix, 512, "dw_out")
        dq, dk, dv = _attn_bwd(qkv, carries, datt, tri, upper)
        dga, dgb, c3, dcw, dcb, dlng, dlnb = _conv_bwd(dc4, c1, ew, conv_w[l], row(ln_g, l), row(ln_b, l), w_pw2[l])
        dw_pw2 = _matmul_tn(c3, dc4, 512, "dw_pw2")
        dh, dproj, u, dpre = _inproj_bwd(dga, dgb, dcg, dq, dk, dv, dsg, h_in, row(pre_g, l), w_in[l], dh)
        dw_in = _matmul_tn(u, dproj, 896, "dw_in")
        grads[l] = (dpre[0], dpost[0], dw_in, dcw, dcb[0], dlng[0], dlnb[0], dw_pw2, dbpw2[0], dw_out)

    stack = lambda k: jnp.stack([g[k] for g in grads])
    names = ("pre_norm_g", "post_norm_g", "w_in", "conv_w", "conv_b", "conv_ln_g", "conv_ln_b", "w_pw2", "b_pw2",
             "w_out")
    return loss[0, 0], dh, {nm: stack(k) for k, nm in enumerate(names)}


N_CHIPS = 4
ANY = pl.BlockSpec(memory_space=pl.ANY)


def _chip_peers():
    x, y, c = lax.axis_index("x"), lax.axis_index("y"), lax.axis_index("c")
    return x, y, c, [(x, 1 - y), (1 - x, y), (1 - x, 1 - y)]


def _shard_slices(refs, dims, idx):
    out = []
    for ref, (axis, size) in zip(refs, dims):
        assert size % LANES == 0
        start = pl.multiple_of(idx * size, LANES)
        sl = [slice(None)] * len(ref.shape)
        sl[axis] = pl.ds(start, size)
        out.append(ref.at[tuple(sl)])
    return out


def _gather_weights(shards, dims):
    n = len(shards)
    full_shapes = []
    for s, (axis, size) in zip(shards, dims):
        shp = list(s.shape)
        shp[axis] = size * N_CHIPS
        full_shapes.append(jax.ShapeDtypeStruct(tuple(shp), s.dtype))

    def body(*refs):
        srcs, outs = refs[:n], refs[n:2 * n]
        send, recv, loc = refs[2 * n:]
        x, y, c, peers = _chip_peers()
        mine = _shard_slices(outs, dims, 2 * x + y)
        local = [pltpu.make_async_copy(s, d, loc.at[a]) for a, (s, d) in enumerate(zip(srcs, mine))]
        for cp in local:
            cp.start()
        sends = []
        for k, (px, py) in enumerate(peers):
            for a, (s, d) in enumerate(zip(srcs, mine)):
                cp = pltpu.make_async_remote_copy(s, d, send.at[k, a], recv.at[k, a],
                                                  device_id=(px, py, c), device_id_type=MESH)
                cp.start()
                sends.append(cp)
        for k, (px, py) in enumerate(peers):
            for a, (s, d) in enumerate(zip(srcs, _shard_slices(outs, dims, 2 * px + py))):
                pltpu.make_async_remote_copy(s, d, send.at[k, a], recv.at[k, a],
                                             device_id=(px, py, c), device_id_type=MESH).wait_recv()
        for cp in sends:
            cp.wait_send()
        for cp in local:
            cp.wait()

    return _pcall(
        body, name="gather_weights", in_specs=[ANY] * n, out_specs=[ANY] * n, out_shape=full_shapes,
        scratch_shapes=[pltpu.SemaphoreType.DMA((3, n)), pltpu.SemaphoreType.DMA((3, n)),
                        pltpu.SemaphoreType.DMA((n,))],
    )(*shards)


def _reduce_grads(grads, dims):
    n = len(grads)
    piece_shapes = []
    for g, (axis, size) in zip(grads, dims):
        shp = list(g.shape)
        shp[axis] = size
        piece_shapes.append(jax.ShapeDtypeStruct((N_CHIPS,) + tuple(shp), g.dtype))

    def body(*refs):
        srcs, mine, theirs = refs[:n], refs[n:2 * n], refs[2 * n:3 * n]
        send, recv, loc = refs[3 * n:]
        x, y, c, peers = _chip_peers()
        sibling = (x, y, 1 - c)
        own = _shard_slices(srcs, dims, 2 * x + y)
        started = []
        local = [pltpu.make_async_copy(own[a], mine[a].at[3], loc.at[a]) for a in range(n)]
        for cp in local:
            cp.start()

        def remote(src, dst, slot, a, dev):
            return pltpu.make_async_remote_copy(src, dst, send.at[slot, a], recv.at[slot, a],
                                                device_id=dev, device_id_type=MESH)

        for a in range(n):
            cp = remote(own[a], theirs[a].at[3], 3, a, sibling)
            cp.start()
            started.append(cp)
        for k, (px, py) in enumerate(peers):
            for a, src in enumerate(_shard_slices(srcs, dims, 2 * px + py)):
                cp = remote(src, mine[a].at[k], k, a, (px, py, c))
                cp.start()
                started.append(cp)
        for k, (px, py) in enumerate(peers):
            for a in range(n):
                remote(mine[a].at[k], mine[a].at[k], k, a, (px, py, c)).wait_recv()
                cp = remote(mine[a].at[k], theirs[a].at[k], 4 + k, a, sibling)
                cp.start()
                started.append(cp)
        for a in range(n):
            remote(own[a], theirs[a].at[3], 3, a, sibling).wait_recv()
            for k in range(3):
                remote(mine[a].at[k], theirs[a].at[k], 4 + k, a, sibling).wait_recv()
        for cp in started:
            cp.wait_send()
        for cp in local:
            cp.wait()

    return _pcall(
        body, name="reduce_grads", in_specs=[ANY] * n, out_specs=[ANY] * (2 * n), out_shape=piece_shapes * 2,
        scratch_shapes=[pltpu.SemaphoreType.DMA((7, n)), pltpu.SemaphoreType.DMA((7, n)),
                        pltpu.SemaphoreType.DMA((n,))],
    )(*grads)


def _allsum_small(pack):
    rows, cols = pack.shape
    ndev = 8

    def body(p_ref, o_ref, buf, send, recv):
        x, y, c = lax.axis_index("x"), lax.axis_index("y"), lax.axis_index("c")
        me = 4 * x + 2 * y + c
        buf[me] = p_ref[...]
        started = []
        for r in range(1, ndev):
            bx, by, bc = (r >> 2) & 1, (r >> 1) & 1, r & 1
            dev = (x ^ bx, y ^ by, c ^ bc)
            cp = pltpu.make_async_remote_copy(p_ref, buf.at[me], send.at[r], recv.at[r],
                                              device_id=dev, device_id_type=MESH)
            cp.start()
            started.append(cp)
        for r in range(1, ndev):
            pltpu.make_async_remote_copy(p_ref, buf.at[me ^ r], send.at[r], recv.at[r],
                                         device_id=(x, y, c), device_id_type=MESH).wait_recv()
        for cp in started:
            cp.wait_send()
        acc = buf[0]
        for d in range(1, ndev):
            acc = acc + buf[d]
        o_ref[...] = acc

    vm = pl.BlockSpec(memory_space=pltpu.VMEM)
    return _pcall(
        body, name="allsum_small", in_specs=[vm], out_specs=vm,
        out_shape=jax.ShapeDtypeStruct((rows, cols), F32),
        scratch_shapes=[pltpu.VMEM((ndev, rows, cols), F32), pltpu.SemaphoreType.DMA((ndev,)),
                        pltpu.SemaphoreType.DMA((ndev,))],
    )(pack)


def _adamw(parts, w, m, v, name):
    rows, cols = w.shape
    tr = ROW_TILE if rows % ROW_TILE == 0 else rows
    counts = [p.shape[0] for p in parts]

    def body(*refs):
        part_refs = refs[:len(parts)]
        w_ref, m_ref, v_ref, g_ref, d_ref, nm_ref, nv_ref = refs[len(parts):]
        g = None
        for p_ref, cnt in zip(part_refs, counts):
            s = p_ref[0].astype(F32)
            for k in range(1, cnt):
                s = s + p_ref[k].astype(F32)
            g = s if g is None else g + s
        m2 = ADAM_B1 * m_ref[...] + (1.0 - ADAM_B1) * g
        v2 = ADAM_B2 * v_ref[...] + (1.0 - ADAM_B2) * (g * g)
        m_hat = m2 / (1.0 - ADAM_B1 ** ADAM_STEP)
        v_hat = v2 / (1.0 - ADAM_B2 ** ADAM_STEP)
        g_ref[...] = g
        d_ref[...] = -ADAM_LR * (m_hat / (jnp.sqrt(v_hat) + ADAM_EPS) + ADAM_WD * w_ref[...])
        nm_ref[...] = m2
        nv_ref[...] = v2

    blk = pl.BlockSpec((tr, cols), lambda i: (i, 0))
    shp = jax.ShapeDtypeStruct((rows, cols), F32)
    return _pcall(
        body, name=name, grid=(rows // tr,),
        in_specs=[pl.BlockSpec((cnt, tr, cols), lambda i: (0, i, 0)) for cnt in counts] + [blk] * 3,
        out_specs=[blk] * 4, out_shape=[shp] * 4,
        compiler_params=_params("parallel"),
    )(*parts, w, m, v)


def kernel(x, meta_tokens, pre_norm_g, post_norm_g, w_in, conv_w, conv_b, conv_ln_g, conv_ln_b, w_pw2, b_pw2, w_out, loss_target, m_meta_tokens, m_pre_norm_g, m_post_norm_g, m_w_in, m_conv_w, m_conv_b, m_conv_ln_g, m_conv_ln_b, m_w_pw2, m_b_pw2, m_w_out, v_meta_tokens, v_pre_norm_g, v_post_norm_g, v_w_in, v_conv_w, v_conv_b, v_conv_ln_g, v_conv_ln_b, v_w_pw2, v_b_pw2, v_w_out):
    seq, d = x.shape[1], x.shape[2]
    depth = w_in.shape[0]
    length = N_META + seq
    lp = -(-length // ATT_BLOCK) * ATT_BLOCK
    tap_pad = ((0, 0), (0, CONV_PAD - CONV_WIDTH), (0, 0))

    shard_dims = [(2, w_in.shape[2]), (1, w_pw2.shape[1]), (1, w_out.shape[1]), (2, conv_w.shape[2]),
                  (1, meta_tokens.shape[1])]
    w_in_f, w_pw2_f, w_out_f, conv_w_f, meta_f = _gather_weights(
        [w_in.astype(BF16), w_pw2.astype(BF16), w_out.astype(BF16), jnp.pad(conv_w, tap_pad), meta_tokens], shard_dims)

    h0 = jnp.concatenate([meta_f, x[0], jnp.zeros((lp - length, d), F32)], axis=0)
    target_p = jnp.pad(loss_target[0], ((N_META, lp - length), (0, 0)))
    loss, dh0, g = _local_step(h0, target_p, seq, pre_norm_g, post_norm_g, w_in_f, conv_w_f, conv_b, conv_ln_g,
                               conv_ln_b, w_pw2_f, b_pw2, w_out_f)

    pieces = _reduce_grads([g["w_in"], g["w_pw2"], g["w_out"], g["conv_w"]], shard_dims[:4])
    mine, theirs = pieces[:4], pieces[4:]

    def update(k, w, m, v, name):
        shp = w.shape
        flat = lambda a: a.reshape(-1, shp[-1])
        parts = [p.reshape(N_CHIPS, -1, shp[-1]) for p in (mine[k], theirs[k])]
        return [o.reshape(shp) for o in _adamw(parts, flat(w), flat(m), flat(v), name)]

    up_w_in = update(0, w_in, m_w_in, v_w_in, "adamw_w_in")
    up_w_pw2 = update(1, w_pw2, m_w_pw2, v_w_pw2, "adamw_w_pw2")
    up_w_out = update(2, w_out, m_w_out, v_w_out, "adamw_w_out")
    up_conv_w = [o[:, :CONV_WIDTH] for o in update(3, jnp.pad(conv_w, tap_pad), jnp.pad(m_conv_w, tap_pad),
                                                   jnp.pad(v_conv_w, tap_pad, constant_values=1.0), "adamw_conv_w")]

    two = lambda a: a.reshape(-1, d)
    vec_rows = [g["pre_norm_g"], g["post_norm_g"], two(g["conv_b"]), two(g["conv_ln_g"]), two(g["conv_ln_b"]),
                two(g["b_pw2"])]
    n_vec = sum(a.shape[0] for a in vec_rows)
    pack = jnp.concatenate(vec_rows + [dh0[:N_META], jnp.full((8, d), loss, F32)], axis=0)
    pack = jnp.pad(pack, ((0, -pack.shape[0] % 8), (0, 0)))
    tot = _allsum_small(pack)
    loss_all = tot[n_vec + N_META, 0]

    cat = lambda *a: jnp.concatenate([two(t) for t in a], axis=0)
    small = (pre_norm_g, post_norm_g, conv_b, conv_ln_g, conv_ln_b, b_pw2)
    small_m = (m_pre_norm_g, m_post_norm_g, m_conv_b, m_conv_ln_g, m_conv_ln_b, m_b_pw2)
    small_v = (v_pre_norm_g, v_post_norm_g, v_conv_b, v_conv_ln_g, v_conv_ln_b, v_b_pw2)
    up_small = _adamw([tot[None, :n_vec]], cat(*small), cat(*small_m), cat(*small_v), "adamw_vectors")

    def unpack(o):
        res, r0 = [], 0
        for t in small:
            nrow = t.size // d
            res.append(o[r0:r0 + nrow].reshape(t.shape))
            r0 += nrow
        return res

    up_small = [unpack(o) for o in up_small]
    chip = 2 * lax.axis_index("x") + lax.axis_index("y")
    mcols = meta_tokens.shape[1]
    g_meta = lax.dynamic_slice_in_dim(tot[n_vec:n_vec + N_META], chip * mcols, mcols, axis=1)
    up_meta = _adamw([g_meta[None]], meta_tokens, m_meta_tokens, v_meta_tokens, "adamw_meta")

    grad_x = dh0[N_META:length][None]
    outs = [loss_all, grad_x]
    for j in range(4):
        pre, post, cb, lg, lb, bp = up_small[j]
        outs += [up_meta[j], pre, post, up_w_in[j], up_conv_w[j], cb, lg, lb, up_w_pw2[j], bp, up_w_out[j]]
    return tuple(outs)
```

```python
from typing import Callable, NamedTuple

import jax
import jax.numpy as jnp
from jax import lax
from jax.experimental import pallas as pl
from jax.experimental.pallas import tpu as pltpu

F32 = jnp.float32
BF16 = jnp.bfloat16

N_META = 16
D_CONV = 512
D_SB = 512
HEAD_DIM = 64
CONV_WIDTH = 31
CONV_PAD = 32
RMS_EPS = 1e-6
LN_EPS = 1e-5
Q_SCALE = HEAD_DIM ** -0.5

ADAM_LR = 0.001
ADAM_B1 = 0.9
ADAM_B2 = 0.999
ADAM_EPS = 1e-08
ADAM_WD = 0.01
ADAM_STEP = 10

LANES = 128
ROW_TILE = 256
ATT_BLOCK = 256
VMEM_LIMIT = 56 * 1024 * 1024
EXP_ZERO = -104.0
COUNT_LANE = LANES - 1

MESH = pl.DeviceIdType.MESH


def _pcall(body, **kw):
    return pl.pallas_call(body, **kw)


def _params(*sem):
    return pltpu.CompilerParams(dimension_semantics=sem, vmem_limit_bytes=VMEM_LIMIT)


def _sigmoid(x):
    return 1.0 / (1.0 + jnp.exp(-x))


def _silu_fwd_bwd(x):
    s = _sigmoid(x)
    return x * s, s * (1.0 + x * (1.0 - s))


def _nt(a, b):
    return lax.dot_general(a, b, (((1,), (1,)), ((), ())), preferred_element_type=F32)


def _tn(a, b):
    return lax.dot_general(a, b, (((0,), (0,)), ((), ())), preferred_element_type=F32)


def _nn(a, b):
    return jnp.dot(a, b, preferred_element_type=F32)


def _inproj(h, g, w):
    lp, d = h.shape
    n = w.shape[1]

    def body(h_ref, g_ref, w_ref, ew_ref, qkv_ref):
        x = h_ref[...]
        rstd = lax.rsqrt(jnp.mean(x * x, axis=-1, keepdims=True) + RMS_EPS)
        u = ((x * rstd) * g_ref[...]).astype(BF16)
        p = _nn(u, w_ref[...])
        ew_ref[:, 0:1536] = p[:, 0:1536]
        ew_ref[:, 1536:2048] = p[:, 3072:3584]
        qkv_ref[:, 0:512] = (p[:, 1536:2048] * Q_SCALE).astype(BF16)
        qkv_ref[:, 512:1536] = p[:, 2048:3072].astype(BF16)

    return _pcall(
        body, name="inproj_fwd", grid=(lp // ROW_TILE,),
        in_specs=[pl.BlockSpec((ROW_TILE, d), lambda i: (i, 0)),
                  pl.BlockSpec((1, d), lambda i: (0, 0)),
                  pl.BlockSpec((d, n), lambda i: (0, 0))],
        out_specs=[pl.BlockSpec((ROW_TILE, 2048), lambda i: (i, 0)),
                   pl.BlockSpec((ROW_TILE, 1536), lambda i: (i, 0))],
        out_shape=[jax.ShapeDtypeStruct((lp, 2048), F32), jax.ShapeDtypeStruct((lp, 1536), BF16)],
        compiler_params=_params("parallel"),
    )(h, g, w)


def _layer_norm_stats(c1):
    mu = jnp.mean(c1, axis=-1, keepdims=True)
    xc = c1 - mu
    var = jnp.mean(xc * xc, axis=-1, keepdims=True)
    rstd = lax.rsqrt(var + LN_EPS)
    return xc * rstd, rstd


def _conv_fwd(ew, cw, cb, lng, lnb, wpw2, bpw2):
    lp = ew.shape[0]
    tm = ROW_TILE

    def body(ew_ref, cw_ref, cb_ref, lng_ref, lnb_ref, w_ref, b_ref, c1_ref, c4_ref, c5_ref, xbuf):
        @pl.when(pl.program_id(0) == 0)
        def _():
            xbuf[0:CONV_PAD, :] = jnp.zeros((CONV_PAD, D_CONV), F32)

        ga = ew_ref[:, 0:512]
        gb = ew_ref[:, 512:1024]
        cg = ew_ref[:, 1024:1536]
        xbuf[CONV_PAD:CONV_PAD + tm, :] = ga * _sigmoid(gb)
        acc = jnp.zeros((tm, D_CONV), F32) + cb_ref[...]
        for j in range(CONV_WIDTH):
            acc = acc + cw_ref[j:j + 1, :] * xbuf[pl.ds(CONV_PAD - (CONV_WIDTH - 1) + j, tm), :]
        c1_ref[...] = acc
        xbuf[0:CONV_PAD, :] = xbuf[tm:tm + CONV_PAD, :]
        xhat, _ = _layer_norm_stats(acc)
        c2 = xhat * lng_ref[...] + lnb_ref[...]
        c3 = c2 * _sigmoid(c2)
        c4 = _nn(c3.astype(BF16), w_ref[...]) + b_ref[...]
        c4_ref[...] = c4
        c5_ref[...] = (c4 * (cg * _sigmoid(cg))).astype(BF16)

    vec = pl.BlockSpec((1, D_CONV), lambda i: (0, 0))
    row = pl.BlockSpec((tm, D_CONV), lambda i: (i, 0))
    return _pcall(
        body, name="conv_fwd", grid=(lp // tm,),
        in_specs=[pl.BlockSpec((tm, 1536), lambda i: (i, 0)),
                  pl.BlockSpec((CONV_PAD, D_CONV), lambda i: (0, 0)),
                  vec, vec, vec,
                  pl.BlockSpec((D_CONV, D_CONV), lambda i: (0, 0)),
                  vec],
        out_specs=[row, row, row],
        out_shape=[jax.ShapeDtypeStruct((lp, D_CONV), F32), jax.ShapeDtypeStruct((lp, D_CONV), F32),
                   jax.ShapeDtypeStruct((lp, D_CONV), BF16)],
        scratch_shapes=[pltpu.VMEM((tm + CONV_PAD, D_CONV), F32)],
        compiler_params=_params("arbitrary"),
    )(ew, cw, cb, lng, lnb, wpw2, bpw2)


def _split_matmul(x, m01):
    hi = x.astype(BF16)
    lo = (x - hi.astype(F32)).astype(BF16)
    t = _nn(jnp.concatenate([hi, lo], axis=0), m01)
    m = x.shape[0]
    return t[:m] + t[m:]


def _sb_block(q2, kj, z_mask, tri):
    z = _nt(q2, kj)
    ls = -(jnp.maximum(z, 0.0) + jnp.log(1.0 + jnp.exp(-jnp.abs(z))))
    if z_mask is not None:
        ls = jnp.where(z_mask, ls, 0.0)
    return z, ls, _split_matmul(ls, tri)


def _attn_masks():
    lane = lax.broadcasted_iota(jnp.int32, (1, LANES), 1)
    row = lax.broadcasted_iota(jnp.int32, (2 * ATT_BLOCK, ATT_BLOCK), 0)
    col = lax.broadcasted_iota(jnp.int32, (2 * ATT_BLOCK, ATT_BLOCK), 1)
    return lane < HEAD_DIM, col < (row & (ATT_BLOCK - 1))


def _stack_heads(x, first_head):
    zero = jnp.zeros_like(x)
    return jnp.concatenate([jnp.where(first_head, x, zero), jnp.where(first_head, zero, x)], axis=0)


def _unstack_heads(x2, first_head):
    rows = x2.shape[0] // 2
    return jnp.where(first_head, x2[:rows], x2[rows:])


def _hosted(plan, n_out):
    if plan is None:
        return (lambda rest: ((), rest, (), ())), [], [], [], [], []
    n_in, n_x = len(plan.inputs), len(plan.out_shapes)

    def split(rest):
        return (rest[:n_in], rest[n_in:n_in + n_out], rest[n_in + n_out:n_in + n_out + n_x],
                rest[n_in + n_out + n_x:])

    return split, list(plan.inputs), [ANY] * n_in, [ANY] * n_x, list(plan.out_shapes), list(plan.scratch)


def _attn_fwd(qkv, tri, plan=None):
    lp = qkv.shape[0]
    bq = ATT_BLOCK
    npair = D_SB // LANES
    nq = lp // bq
    assert nq <= COUNT_LANE
    split, x_args, x_in_specs, x_out_specs, x_out_shapes, x_scratch = _hosted(plan, 2)

    def body(q_ref, k_ref, v_ref, tri_ref, *rest):
        x_in, (o_ref, carry_ref), x_out, x_sems = split(rest)
        i = pl.program_id(1)
        if plan is not None:
            @pl.when(jnp.logical_and(pl.program_id(0) == 0, i == 0))
            def _():
                plan.start(x_in, x_out, x_sems)

        first_head, vis = _attn_masks()
        lane = lax.broadcasted_iota(jnp.int32, (1, LANES), 1)
        q2 = _stack_heads(q_ref[...], first_head)
        tri_m = tri_ref[...]

        def blocks(js, state, masks):
            c, acc, cmat = state
            offs = [pl.multiple_of(j * bq, bq) for j in js]
            zs = [_nt(q2, k_ref[pl.ds(off, bq), :]) for off in offs]
            lss = []
            for z, mask in zip(zs, masks):
                ls = -(jnp.maximum(z, 0.0) + jnp.log(1.0 + jnp.exp(-jnp.abs(z))))
                lss.append(ls if mask is None else jnp.where(mask, ls, 0.0))
            tails = [_split_matmul(ls, tri_m) for ls in lss]
            probs = []
            for j, z, ls, tail, mask in zip(js, zs, lss, tails, masks):
                a = jnp.exp(z + ls + tail + c)
                probs.append(a if mask is None else jnp.where(mask, a, 0.0))
                cmat = jnp.where(lane == j, c, cmat)
                c = c + tail[:, 0:1] + ls[:, 0:1]
            for off, a in zip(offs, probs):
                acc = acc + _nn(a.astype(BF16), v_ref[pl.ds(off, bq), :])
            return c, acc, cmat

        def live(state):
            return jnp.max(state[0]) > EXP_ZERO

        zeros = jnp.zeros((2 * bq, LANES), F32)
        init = (jnp.zeros((2 * bq, 1), F32), zeros, zeros)
        state = lax.cond(i > 0, lambda st: blocks([i, i - 1], st, [vis, None]),
                         lambda st: blocks([i], st, [vis]), init)
        rest = jnp.maximum(i - 1, 0)

        def pair(carry):
            t, st = carry
            j = i - 2 - 2 * t
            return t + 1, blocks([j, j - 1], st, [None, None])

        trips, state = lax.while_loop(lambda ca: jnp.logical_and(ca[0] < rest // 2, live(ca[1])), pair, (0, state))
        last = jnp.logical_and(jnp.logical_and(rest % 2 == 1, trips == rest // 2), live(state))
        state = lax.cond(last, lambda st: blocks([0], st, [None]), lambda st: st, state)
        n_done = jnp.minimum(i + 1, 2) + 2 * trips + last.astype(jnp.int32)
        cmat = jnp.where(lane == COUNT_LANE, n_done.astype(F32), state[2])
        carry_ref[:, 0:LANES] = cmat[:bq]
        carry_ref[:, LANES:] = cmat[bq:]
        o_ref[...] = _unstack_heads(state[1], first_head)
        if plan is not None:
            @pl.when(jnp.logical_and(pl.program_id(0) == npair - 1, i == nq - 1))
            def _():
                plan.finish(x_in, x_out, x_sems)

    outs = _pcall(
        body, name="attn_fwd" if plan is None else "attn_fwd_gather", grid=(npair, nq),
        in_specs=[pl.BlockSpec((bq, LANES), lambda p, i: (i, p)),
                  pl.BlockSpec((lp, LANES), lambda p, i: (0, npair + p)),
                  pl.BlockSpec((lp, LANES), lambda p, i: (0, 2 * npair + p)),
                  pl.BlockSpec((bq, bq), lambda p, i: (0, 0))] + x_in_specs,
        out_specs=[pl.BlockSpec((bq, LANES), lambda p, i: (i, p)),
                   pl.BlockSpec((bq, 2 * LANES), lambda p, i: (i, p))] + x_out_specs,
        out_shape=[jax.ShapeDtypeStruct((lp, D_SB), F32), jax.ShapeDtypeStruct((lp, 2 * D_SB), F32)] + x_out_shapes,
        scratch_shapes=x_scratch,
        compiler_params=_params("arbitrary", "arbitrary"),
    )(qkv, qkv, qkv, tri, *x_args)
    return outs[0], outs[1], outs[2:]


def _outproj(c5, att, ew, h, w, g):
    lp, d = h.shape
    tm = ROW_TILE

    def body(c5_ref, att_ref, sg_ref, h_ref, w_ref, g_ref, hn_ref, cat_ref, mix_ref):
        sg = sg_ref[...]
        s = att_ref[...] * (sg * _sigmoid(sg))
        cat_ref[:, 0:D_CONV] = c5_ref[...]
        cat_ref[:, D_CONV:] = s.astype(BF16)
        mixed = _nn(cat_ref[...], w_ref[...])
        mix_ref[...] = mixed
        rstd = lax.rsqrt(jnp.mean(mixed * mixed, axis=-1, keepdims=True) + RMS_EPS)
        hn_ref[...] = h_ref[...] + (mixed * rstd) * g_ref[...]

    half = pl.BlockSpec((tm, 512), lambda i: (i, 0))
    full = pl.BlockSpec((tm, d), lambda i: (i, 0))
    return _pcall(
        body, name="outproj_fwd", grid=(lp // tm,),
        in_specs=[half, half, pl.BlockSpec((tm, 512), lambda i: (i, 3)), full,
                  pl.BlockSpec((d, d), lambda i: (0, 0)), pl.BlockSpec((1, d), lambda i: (0, 0))],
        out_specs=[full, full, full],
        out_shape=[jax.ShapeDtypeStruct((lp, d), F32), jax.ShapeDtypeStruct((lp, d), BF16),
                   jax.ShapeDtypeStruct((lp, d), F32)],
        compiler_params=_params("parallel"),
    )(c5, att, ew, h, w, g)


def _loss_head(h, target, seq):
    lp, d = h.shape
    tm = ROW_TILE

    def body(h_ref, t_ref, dh_ref, loss_ref):
        i = pl.program_id(0)

        @pl.when(i == 0)
        def _():
            loss_ref[...] = jnp.zeros_like(loss_ref)

        row = i * tm + lax.broadcasted_iota(jnp.int32, (tm, 1), 0)
        real = jnp.logical_and(row >= N_META, row < N_META + seq)
        diff = jnp.where(real, h_ref[...] - t_ref[...], 0.0)
        dh_ref[...] = diff * (1.0 / d)
        loss_ref[...] += 0.5 * jnp.sum(jnp.sum(diff * diff, axis=-1, keepdims=True) * (1.0 / d))

    full = pl.BlockSpec((tm, d), lambda i: (i, 0))
    return _pcall(
        body, name="loss_head", grid=(lp // tm,),
        in_specs=[full, full],
        out_specs=[full, pl.BlockSpec((8, LANES), lambda i: (0, 0))],
        out_shape=[jax.ShapeDtypeStruct((lp, d), F32), jax.ShapeDtypeStruct((8, LANES), F32)],
        compiler_params=_params("arbitrary"),
    )(h, target)


def _outproj_bwd(dh, mixed, g, w, att, ew, c4):
    lp, d = dh.shape
    tm = ROW_TILE

    def body(dh_ref, mix_ref, g_ref, w_ref, att_ref, cg_ref, sg_ref, c4_ref,
             dmix_ref, datt_ref, dsg_ref, dc4_ref, dcg_ref, dg_ref, db_ref):
        @pl.when(pl.program_id(0) == 0)
        def _():
            dg_ref[...] = jnp.zeros_like(dg_ref)
            db_ref[...] = jnp.zeros_like(db_ref)

        mixed = mix_ref[...]
        dhv = dh_ref[...]
        rstd = lax.rsqrt(jnp.mean(mixed * mixed, axis=-1, keepdims=True) + RMS_EPS)
        n = mixed * rstd
        dg_ref[...] += jnp.sum(dhv * n, axis=0, keepdims=True)
        dn = dhv * g_ref[...]
        dmix = (rstd * (dn - n * jnp.mean(dn * n, axis=-1, keepdims=True))).astype(BF16)
        dmix_ref[...] = dmix
        dcat = _nt(dmix, w_ref[...])
        dc5 = dcat[:, 0:D_CONV]
        ds = dcat[:, D_CONV:]
        silu_sg, dsilu_sg = _silu_fwd_bwd(sg_ref[...])
        datt_ref[...] = (ds * silu_sg).astype(BF16)
        dsg_ref[...] = (ds * att_ref[...] * dsilu_sg).astype(BF16)
        silu_cg, dsilu_cg = _silu_fwd_bwd(cg_ref[...])
        dc4 = dc5 * silu_cg
        db_ref[...] += jnp.sum(dc4, axis=0, keepdims=True)
        dc4_ref[...] = dc4.astype(BF16)
        dcg_ref[...] = (dc5 * c4_ref[...] * dsilu_cg).astype(BF16)

    half = pl.BlockSpec((tm, 512), lambda i: (i, 0))
    full = pl.BlockSpec((tm, d), lambda i: (i, 0))
    hb = jax.ShapeDtypeStruct((lp, 512), BF16)
    return _pcall(
        body, name="outproj_bwd", grid=(lp // tm,),
        in_specs=[full, full, pl.BlockSpec((1, d), lambda i: (0, 0)), pl.BlockSpec((d, d), lambda i: (0, 0)),
                  half, pl.BlockSpec((tm, 512), lambda i: (i, 2)), pl.BlockSpec((tm, 512), lambda i: (i, 3)), half],
        out_specs=[full, half, half, half, half,
                   pl.BlockSpec((1, d), lambda i: (0, 0)), pl.BlockSpec((1, 512), lambda i: (0, 0))],
        out_shape=[jax.ShapeDtypeStruct((lp, d), BF16), hb, hb, hb, hb,
                   jax.ShapeDtypeStruct((1, d), F32), jax.ShapeDtypeStruct((1, 512), F32)],
        compiler_params=_params("arbitrary"),
    )(dh, mixed, g, w, att, ew, ew, c4)


def _attn_bwd(qkv, carries, datt, tri, upper, plan=None):
    lp = qkv.shape[0]
    bq = ATT_BLOCK
    npair = D_SB // LANES
    nq = lp // bq
    split, x_args, x_in_specs, x_out_specs, x_out_shapes, x_scratch = _hosted(plan, 3)

    def body(q_ref, k_ref, v_ref, carry_ref, do_ref, tri_ref, upper_ref, *rest):
        x_in, (dq_ref, dk_ref, dv_ref), x_out, x_sems = split(rest)
        i = pl.program_id(1)
        if plan is not None:
            @pl.when(jnp.logical_and(pl.program_id(0) == 0, i == 0))
            def _():
                plan.start(x_in, x_out, x_sems)

        @pl.when(i == 0)
        def _():
            dk_ref[...] = jnp.zeros_like(dk_ref)
            dv_ref[...] = jnp.zeros_like(dv_ref)

        first_head, vis = _attn_masks()
        lane = lax.broadcasted_iota(jnp.int32, (1, LANES), 1)
        q2 = _stack_heads(q_ref[...], first_head)
        do2 = _stack_heads(do_ref[...], first_head)
        cmat = jnp.concatenate([carry_ref[:, 0:LANES], carry_ref[:, LANES:]], axis=0)
        tri_m = tri_ref[...]
        upper_m = upper_ref[...]

        def blocks(js, state, masks):
            run, dq = state
            offs = [pl.multiple_of(j * bq, bq) for j in js]
            zs = [_nt(q2, k_ref[pl.ds(off, bq), :]) for off in offs]
            lss = []
            for z, mask in zip(zs, masks):
                ls = -(jnp.maximum(z, 0.0) + jnp.log(1.0 + jnp.exp(-jnp.abs(z))))
                lss.append(ls if mask is None else jnp.where(mask, ls, 0.0))
            tails = [_split_matmul(ls, tri_m) for ls in lss]
            das = [_nt(do2, v_ref[pl.ds(off, bq), :]) for off in offs]
            probs, des = [], []
            for j, z, ls, tail, da, mask in zip(js, zs, lss, tails, das, masks):
                c = jnp.sum(jnp.where(lane == j, cmat, 0.0), axis=-1, keepdims=True)
                a = jnp.exp(z + ls + tail + c)
                a = a if mask is None else jnp.where(mask, a, 0.0)
                probs.append(a.astype(BF16))
                des.append(da * a)
            prefixes = [_split_matmul(de, upper_m) for de in des]
            dzs = []
            for z, ls, de, prefix, mask in zip(zs, lss, des, prefixes, masks):
                beta = jnp.exp(z + ls)
                dz = de - beta * (de + run + prefix)
                dzs.append((dz if mask is None else jnp.where(mask, dz, 0.0)).astype(BF16))
                run = run + prefix[:, bq - 1:bq] + de[:, bq - 1:bq]
            for off, dzb, ab in zip(offs, dzs, probs):
                dq = dq + _nn(dzb, k_ref[pl.ds(off, bq), :])
                dk_ref[pl.ds(off, bq), :] += _tn(dzb, q2)
                dv_ref[pl.ds(off, bq), :] += _tn(ab, do2)
            return run, dq

        n_done = jnp.max(carry_ref[:, COUNT_LANE:COUNT_LANE + 1]).astype(jnp.int32)
        n_done = jnp.clip(n_done, 1, i + 1)
        before = jnp.maximum(n_done - 2, 0)
        j0 = i - n_done + 1
        odd = before % 2
        state = (jnp.zeros((2 * bq, 1), F32), jnp.zeros((2 * bq, LANES), F32))
        state = lax.cond(odd == 1, lambda st: blocks([j0], st, [None]), lambda st: st, state)
        state = lax.fori_loop(
            0, before // 2, lambda t, st: blocks([j0 + odd + 2 * t, j0 + odd + 2 * t + 1], st, [None, None]), state)
        state = lax.cond(n_done > 1, lambda st: blocks([i - 1, i], st, [None, vis]),
                         lambda st: blocks([i], st, [vis]), state)
        dq_ref[...] = (_unstack_heads(state[1], first_head) * Q_SCALE).astype(BF16)
        if plan is not None:
            @pl.when(jnp.logical_and(pl.program_id(0) == npair - 1, i == nq - 1))
            def _():
                plan.finish(x_in, x_out, x_sems)

    qb = pl.BlockSpec((bq, LANES), lambda p, i: (i, p))
    colb = pl.BlockSpec((lp, LANES), lambda p, i: (0, p))
    sq = pl.BlockSpec((bq, bq), lambda p, i: (0, 0))
    outs = _pcall(
        body, name="attn_bwd" if plan is None else "attn_bwd_reduce", grid=(npair, nq),
        in_specs=[qb,
                  pl.BlockSpec((lp, LANES), lambda p, i: (0, npair + p)),
                  pl.BlockSpec((lp, LANES), lambda p, i: (0, 2 * npair + p)),
                  pl.BlockSpec((bq, 2 * LANES), lambda p, i: (i, p)), qb, sq, sq] + x_in_specs,
        out_specs=[qb, colb, colb] + x_out_specs,
        out_shape=[jax.ShapeDtypeStruct((lp, D_SB), BF16), jax.ShapeDtypeStruct((lp, D_SB), F32),
                   jax.ShapeDtypeStruct((lp, D_SB), F32)] + x_out_shapes,
        scratch_shapes=x_scratch,
        compiler_params=_params("arbitrary", "arbitrary"),
    )(qkv, qkv, qkv, carries, datt, tri, upper, *x_args)
    return outs[0], outs[1], outs[2], outs[3:]


def _conv_bwd(dc4, c1, ew, cw, lng, lnb, wpw2):
    lp = ew.shape[0]
    tm = ROW_TILE
    nt = lp // tm
    halo_per_tile = tm // CONV_PAD

    def body(dc4_ref, c1_ref, ew_ref, halo_ref, cw_ref, lng_ref, lnb_ref, w_ref,
             dga_ref, dgb_ref, c3_ref, dcw_ref, dcb_ref, dlng_ref, dlnb_ref, xbuf, dbuf):
        step = pl.program_id(0)

        @pl.when(step == 0)
        def _():
            dcw_ref[...] = jnp.zeros_like(dcw_ref)
            dcb_ref[...] = jnp.zeros_like(dcb_ref)
            dlng_ref[...] = jnp.zeros_like(dlng_ref)
            dlnb_ref[...] = jnp.zeros_like(dlnb_ref)
            dbuf[tm:tm + CONV_PAD, :] = jnp.zeros((CONV_PAD, D_CONV), F32)

        dc3 = _nt(dc4_ref[...], w_ref[...])
        xhat, rstd = _layer_norm_stats(c1_ref[...])
        c2 = xhat * lng_ref[...] + lnb_ref[...]
        c3, dsilu = _silu_fwd_bwd(c2)
        c3_ref[...] = c3.astype(BF16)
        dc2 = dc3 * dsilu
        dlng_ref[...] += jnp.sum(dc2 * xhat, axis=0, keepdims=True)
        dlnb_ref[...] += jnp.sum(dc2, axis=0, keepdims=True)
        dxhat = dc2 * lng_ref[...]
        dc1 = rstd * (dxhat - jnp.mean(dxhat, axis=-1, keepdims=True)
                      - xhat * jnp.mean(dxhat * xhat, axis=-1, keepdims=True))
        dcb_ref[...] += jnp.sum(dc1, axis=0, keepdims=True)
        dbuf[0:tm, :] = dc1

        ga = ew_ref[:, 0:512]
        sgb = _sigmoid(ew_ref[:, 512:1024])
        xbuf[CONV_PAD:CONV_PAD + tm, :] = ga * sgb
        first_tile = step == nt - 1
        halo = halo_ref[:, 0:512] * _sigmoid(halo_ref[:, 512:1024])
        xbuf[0:CONV_PAD, :] = jnp.where(first_tile, 0.0, halo)

        dc0 = jnp.zeros((tm, D_CONV), F32)
        for j in range(CONV_WIDTH):
            dc0 = dc0 + cw_ref[j:j + 1, :] * dbuf[pl.ds(CONV_WIDTH - 1 - j, tm), :]
            tap = dc1 * xbuf[pl.ds(CONV_PAD - (CONV_WIDTH - 1) + j, tm), :]
            dcw_ref[j:j + 1, :] += jnp.sum(tap, axis=0, keepdims=True)
        dbuf[tm:tm + CONV_PAD, :] = dbuf[0:CONV_PAD, :]
        dga_ref[...] = (dc0 * sgb).astype(BF16)
        dgb_ref[...] = (dc0 * ga * sgb * (1.0 - sgb)).astype(BF16)

    rev = lambda i: (nt - 1 - i, 0)
    row = pl.BlockSpec((tm, D_CONV), rev)
    vec = pl.BlockSpec((1, D_CONV), lambda i: (0, 0))
    hb = jax.ShapeDtypeStruct((lp, D_CONV), BF16)
    vs = jax.ShapeDtypeStruct((1, D_CONV), F32)
    return _pcall(
        body, name="conv_bwd", grid=(nt,),
        in_specs=[row, row, pl.BlockSpec((tm, 1024), rev),
                  pl.BlockSpec((CONV_PAD, 1024), lambda i: (jnp.maximum((nt - 1 - i) * halo_per_tile - 1, 0), 0)),
                  pl.BlockSpec((CONV_PAD, D_CONV), lambda i: (0, 0)), vec, vec,
                  pl.BlockSpec((D_CONV, D_CONV), lambda i: (0, 0))],
        out_specs=[row, row, row, pl.BlockSpec((CONV_PAD, D_CONV), lambda i: (0, 0)), vec, vec, vec],
        out_shape=[hb, hb, hb, jax.ShapeDtypeStruct((CONV_PAD, D_CONV), F32), vs, vs, vs],
        scratch_shapes=[pltpu.VMEM((tm + CONV_PAD, D_CONV), F32), pltpu.VMEM((tm + CONV_PAD, D_CONV), F32)],
        compiler_params=_params("arbitrary"),
    )(dc4, c1, ew, ew, cw, lng, lnb, wpw2)


def _inproj_bwd(dga, dgb, dcg, dq, dk, dv, dsg, h, g, w, dh_out):
    lp, d = h.shape
    n = w.shape[1]
    tm = ROW_TILE

    def body(dga_ref, dgb_ref, dcg_ref, dq_ref, dk_ref, dv_ref, dsg_ref, h_ref, g_ref, w_ref, dho_ref,
             dh_ref, dproj_ref, u_ref, dg_ref):
        @pl.when(pl.program_id(0) == 0)
        def _():
            dg_ref[...] = jnp.zeros_like(dg_ref)

        dproj_ref[:, 0:512] = dga_ref[...]
        dproj_ref[:, 512:1024] = dgb_ref[...]
        dproj_ref[:, 1024:1536] = dcg_ref[...]
        dproj_ref[:, 1536:2048] = dq_ref[...]
        dproj_ref[:, 2048:2560] = dk_ref[...].astype(BF16)
        dproj_ref[:, 2560:3072] = dv_ref[...].astype(BF16)
        dproj_ref[:, 3072:3584] = dsg_ref[...]
        du = _nt(dproj_ref[...], w_ref[...])
        x = h_ref[...]
        rstd = lax.rsqrt(jnp.mean(x * x, axis=-1, keepdims=True) + RMS_EPS)
        nrm = x * rstd
        u_ref[...] = (nrm * g_ref[...]).astype(BF16)
        dg_ref[...] += jnp.sum(du * nrm, axis=0, keepdims=True)
        dn = du * g_ref[...]
        dh_ref[...] = dho_ref[...] + rstd * (dn - nrm * jnp.mean(dn * nrm, axis=-1, keepdims=True))

    half = pl.BlockSpec((tm, 512), lambda i: (i, 0))
    full = pl.BlockSpec((tm, d), lambda i: (i, 0))
    return _pcall(
        body, name="inproj_bwd", grid=(lp // tm,),
        in_specs=[half] * 7 + [full, pl.BlockSpec((1, d), lambda i: (0, 0)),
                               pl.BlockSpec((d, n), lambda i: (0, 0)), full],
        out_specs=[full, pl.BlockSpec((tm, n), lambda i: (i, 0)), full, pl.BlockSpec((1, d), lambda i: (0, 0))],
        out_shape=[jax.ShapeDtypeStruct((lp, d), F32), jax.ShapeDtypeStruct((lp, n), BF16),
                   jax.ShapeDtypeStruct((lp, d), BF16), jax.ShapeDtypeStruct((1, d), F32)],
        compiler_params=_params("arbitrary"),
    )(dga, dgb, dcg, dq, dk, dv, dsg, h, g, w, dh_out)


def _row_split(m, parts):
    tm = m // parts
    assert tm * parts == m and tm % 16 == 0, (m, parts)
    return tm


def _matmul_tn(x, dy, tn, name):
    m, k = x.shape
    n = dy.shape[1]
    steps = 4 if m % 64 == 0 else 1
    tm = _row_split(m, steps)

    def body(x_ref, dy_ref, o_ref, acc_ref):
        r = pl.program_id(1)

        @pl.when(r == 0)
        def _():
            acc_ref[...] = jnp.zeros_like(acc_ref)

        acc_ref[...] += _tn(x_ref[...], dy_ref[...])

        @pl.when(r == steps - 1)
        def _():
            o_ref[...] = acc_ref[...].astype(BF16)

    return _pcall(
        body, name=name, grid=(n // tn, steps),
        in_specs=[pl.BlockSpec((tm, k), lambda j, r: (r, 0)), pl.BlockSpec((tm, tn), lambda j, r: (r, j))],
        out_specs=pl.BlockSpec((k, tn), lambda j, r: (0, j)),
        out_shape=jax.ShapeDtypeStruct((k, n), BF16),
        scratch_shapes=[pltpu.VMEM((k, tn), F32)],
        compiler_params=_params("parallel", "arbitrary"),
    )(x, dy)


def _local_step(h0, target_p, seq, vecs, depth, all_weights=None, weights0=None, gather_next=None,
                reduce_layer=None):
    pre_g, post_g, conv_b, ln_g, ln_b, b_pw2 = vecs
    ar = jnp.arange(ATT_BLOCK)
    tri = (ar[:, None] > ar[None, :]).astype(BF16)
    upper = (ar[:, None] < ar[None, :]).astype(BF16)
    row = lambda a, l: a[l][None, :]

    weights = list(all_weights) if all_weights is not None else [weights0] + [None] * (depth - 1)
    saved = []
    h = h0
    for l in range(depth):
        w_in, w_pw2, w_out, conv_w = weights[l]
        ew, qkv = _inproj(h, row(pre_g, l), w_in)
        c1, c4, c5 = _conv_fwd(ew, conv_w, row(conv_b, l), row(ln_g, l), row(ln_b, l), w_pw2, row(b_pw2, l))
        plan = gather_next(l) if gather_next is not None and l + 1 < depth else None
        att, carries, gathered = _attn_fwd(qkv, tri, plan)
        if plan is not None:
            weights[l + 1] = tuple(gathered)
        hn, cat, mixed = _outproj(c5, att, ew, h, w_out, row(post_g, l))
        saved.append((h, ew, qkv, c1, c4, att, carries, cat, mixed))
        h = hn

    dh, loss = _loss_head(h, target_p, seq)

    vec_grads = [None] * depth
    mat_grads = [None] * depth
    pending = None
    for l in reversed(range(depth)):
        w_in, w_pw2, w_out, conv_w = weights[l]
        h_in, ew, qkv, c1, c4, att, carries, cat, mixed = saved[l]
        dmix, datt, dsg, dc4, dcg, dpost, dbpw2 = _outproj_bwd(dh, mixed, row(post_g, l), w_out, att, ew, c4)
        dw_out = _matmul_tn(cat, dmix, 512, "dw_out")
        dq, dk, dv, landed = _attn_bwd(qkv, carries, datt, tri, upper, pending)
        if pending is not None:
            mat_grads[l + 1] = landed
        dga, dgb, c3, dcw, dcb, dlng, dlnb = _conv_bwd(dc4, c1, ew, conv_w, row(ln_g, l), row(ln_b, l), w_pw2)
        dw_pw2 = _matmul_tn(c3, dc4, 512, "dw_pw2")
        dh, dproj, u, dpre = _inproj_bwd(dga, dgb, dcg, dq, dk, dv, dsg, h_in, row(pre_g, l), w_in, dh)
        dw_in = _matmul_tn(u, dproj, 896, "dw_in")
        vec_grads[l] = (dpre[0], dpost[0], dcb[0], dlng[0], dlnb[0], dbpw2[0])
        mats = (dw_in, dw_pw2, dw_out, dcw)
        if reduce_layer is None:
            mat_grads[l] = mats
        else:
            pending = reduce_layer(mats)
    if pending is not None:
        mat_grads[0] = _run_exchange(pending, "reduce_grads")

    vec_grads = [jnp.stack([g[k] for g in vec_grads]) for k in range(len(vecs))]
    return loss[0, 0], dh, vec_grads, mat_grads


N_CHIPS = 4
ANY = pl.BlockSpec(memory_space=pl.ANY)


def _chip_peers():
    x, y, c = lax.axis_index("x"), lax.axis_index("y"), lax.axis_index("c")
    return x, y, c, [(x, 1 - y), (1 - x, y), (1 - x, 1 - y)]


def _shard_slices(refs, dims, idx):
    out = []
    for ref, (axis, size) in zip(refs, dims):
        assert size % LANES == 0
        start = pl.multiple_of(idx * size, LANES)
        sl = [slice(None)] * len(ref.shape)
        sl[axis] = pl.ds(start, size)
        out.append(ref.at[tuple(sl)])
    return out


class _Exchange(NamedTuple):
    inputs: list
    out_shapes: list
    scratch: list
    start: Callable
    finish: Callable


def _run_exchange(plan, name):
    n_in, n_out = len(plan.inputs), len(plan.out_shapes)

    def body(*refs):
        parts = refs[:n_in], refs[n_in:n_in + n_out], refs[n_in + n_out:]
        plan.start(*parts)
        plan.finish(*parts)

    return _pcall(body, name=name, in_specs=[ANY] * n_in, out_specs=[ANY] * n_out, out_shape=plan.out_shapes,
                  scratch_shapes=plan.scratch)(*plan.inputs)


def _gather_plan(shards, dims):
    n = len(shards)
    full_shapes = []
    for s, (axis, size) in zip(shards, dims):
        shp = list(s.shape)
        shp[axis] = size * N_CHIPS
        full_shapes.append(jax.ShapeDtypeStruct(tuple(shp), s.dtype))

    def copies(srcs, outs, sems):
        send, recv, loc = sems
        x, y, c, peers = _chip_peers()
        mine = _shard_slices(outs, dims, 2 * x + y)
        local = [pltpu.make_async_copy(s, d, loc.at[a]) for a, (s, d) in enumerate(zip(srcs, mine))]
        sends = [pltpu.make_async_remote_copy(s, d, send.at[k, a], recv.at[k, a],
                                              device_id=(px, py, c), device_id_type=MESH)
                 for k, (px, py) in enumerate(peers) for a, (s, d) in enumerate(zip(srcs, mine))]
        arrivals = [pltpu.make_async_remote_copy(s, d, send.at[k, a], recv.at[k, a],
                                                 device_id=(px, py, c), device_id_type=MESH)
                    for k, (px, py) in enumerate(peers)
                    for a, (s, d) in enumerate(zip(srcs, _shard_slices(outs, dims, 2 * px + py)))]
        return local, sends, arrivals

    def start(srcs, outs, sems):
        local, sends, _ = copies(srcs, outs, sems)
        for cp in local + sends:
            cp.start()

    def finish(srcs, outs, sems):
        local, sends, arrivals = copies(srcs, outs, sems)
        for cp in arrivals:
            cp.wait_recv()
        for cp in sends:
            cp.wait_send()
        for cp in local:
            cp.wait()

    scratch = [pltpu.SemaphoreType.DMA((3, n)), pltpu.SemaphoreType.DMA((3, n)), pltpu.SemaphoreType.DMA((n,))]
    return _Exchange(list(shards), full_shapes, scratch, start, finish)


def _reduce_plan(grads, dims):
    n = len(grads)
    piece_shapes = []
    for g, (axis, size) in zip(grads, dims):
        shp = list(g.shape)
        shp[axis] = size
        piece_shapes.append(jax.ShapeDtypeStruct((N_CHIPS,) + tuple(shp), g.dtype))

    def copies(srcs, outs, sems):
        mine, theirs = outs[:n], outs[n:]
        send, recv, loc = sems
        x, y, c, peers = _chip_peers()
        sibling = (x, y, 1 - c)
        own = _shard_slices(srcs, dims, 2 * x + y)

        def remote(src, dst, slot, a, dev):
            return pltpu.make_async_remote_copy(src, dst, send.at[slot, a], recv.at[slot, a],
                                                device_id=dev, device_id_type=MESH)

        local = [pltpu.make_async_copy(own[a], mine[a].at[3], loc.at[a]) for a in range(n)]
        to_sibling = [remote(own[a], theirs[a].at[3], 3, a, sibling) for a in range(n)]
        to_chips = [remote(src, mine[a].at[k], k, a, (px, py, c))
                    for k, (px, py) in enumerate(peers)
                    for a, src in enumerate(_shard_slices(srcs, dims, 2 * px + py))]
        passed_on = [remote(mine[a].at[k], theirs[a].at[k], 4 + k, a, sibling) for k in range(3) for a in range(n)]
        return local, to_sibling, to_chips, passed_on

    def start(srcs, outs, sems):
        local, to_sibling, to_chips, _ = copies(srcs, outs, sems)
        for cp in local + to_sibling + to_chips:
            cp.start()

    def finish(srcs, outs, sems):
        local, to_sibling, to_chips, passed_on = copies(srcs, outs, sems)
        for arrived, onward in zip(to_chips, passed_on):
            arrived.wait_recv()
            onward.start()
        for cp in to_sibling + passed_on:
            cp.wait_recv()
        for cp in to_sibling + to_chips + passed_on:
            cp.wait_send()
        for cp in local:
            cp.wait()

    scratch = [pltpu.SemaphoreType.DMA((7, n)), pltpu.SemaphoreType.DMA((7, n)), pltpu.SemaphoreType.DMA((n,))]
    return _Exchange(list(grads), piece_shapes * 2, scratch, start, finish)


def _allsum_small(pack):
    rows, cols = pack.shape
    ndev = 8

    def body(p_ref, o_ref, buf, send, recv):
        x, y, c = lax.axis_index("x"), lax.axis_index("y"), lax.axis_index("c")
        me = 4 * x + 2 * y + c
        buf[me] = p_ref[...]
        started = []
        for r in range(1, ndev):
            bx, by, bc = (r >> 2) & 1, (r >> 1) & 1, r & 1
            dev = (x ^ bx, y ^ by, c ^ bc)
            cp = pltpu.make_async_remote_copy(p_ref, buf.at[me], send.at[r], recv.at[r],
                                              device_id=dev, device_id_type=MESH)
            cp.start()
            started.append(cp)
        for r in range(1, ndev):
            pltpu.make_async_remote_copy(p_ref, buf.at[me ^ r], send.at[r], recv.at[r],
                                         device_id=(x, y, c), device_id_type=MESH).wait_recv()
        for cp in started:
            cp.wait_send()
        acc = buf[0]
        for d in range(1, ndev):
            acc = acc + buf[d]
        o_ref[...] = acc

    vm = pl.BlockSpec(memory_space=pltpu.VMEM)
    return _pcall(
        body, name="allsum_small", in_specs=[vm], out_specs=vm,
        out_shape=jax.ShapeDtypeStruct((rows, cols), F32),
        scratch_shapes=[pltpu.VMEM((ndev, rows, cols), F32), pltpu.SemaphoreType.DMA((ndev,)),
                        pltpu.SemaphoreType.DMA((ndev,))],
    )(pack)


def _adamw(parts, w, m, v, layer, prev, name):
    _, rows, cols = w.shape
    tr = ROW_TILE if rows % ROW_TILE == 0 else rows
    counts = [p.shape[0] for p in parts]
    n_parts = len(parts)
    n_prev = 0 if prev is None else 4

    def body(*refs):
        part_refs = refs[:n_parts]
        w_ref, m_ref, v_ref = refs[n_parts:n_parts + 3]
        g_ref, d_ref, nm_ref, nv_ref = refs[n_parts + 3 + n_prev:]
        g = None
        for p_ref, cnt in zip(part_refs, counts):
            s = p_ref[0].astype(F32)
            for k in range(1, cnt):
                s = s + p_ref[k].astype(F32)
            g = s if g is None else g + s
        m2 = ADAM_B1 * m_ref[0] + (1.0 - ADAM_B1) * g
        v2 = ADAM_B2 * v_ref[0] + (1.0 - ADAM_B2) * (g * g)
        m_hat = m2 / (1.0 - ADAM_B1 ** ADAM_STEP)
        v_hat = v2 / (1.0 - ADAM_B2 ** ADAM_STEP)
        g_ref[0] = g
        d_ref[0] = -ADAM_LR * (m_hat / (jnp.sqrt(v_hat) + ADAM_EPS) + ADAM_WD * w_ref[0])
        nm_ref[0] = m2
        nv_ref[0] = v2

    blk = pl.BlockSpec((1, tr, cols), lambda i: (layer, i, 0))
    shp = jax.ShapeDtypeStruct(w.shape, F32)
    return _pcall(
        body, name=name, grid=(rows // tr,),
        in_specs=[pl.BlockSpec((cnt, tr, cols), lambda i: (0, i, 0)) for cnt in counts] + [blk] * 3 + [ANY] * n_prev,
        out_specs=[blk] * 4, out_shape=[shp] * 4,
        input_output_aliases={n_parts + 3 + k: k for k in range(n_prev)},
        compiler_params=_params("parallel"),
    )(*parts, w, m, v, *(prev or ()))


def kernel(x, meta_tokens, pre_norm_g, post_norm_g, w_in, conv_w, conv_b, conv_ln_g, conv_ln_b, w_pw2, b_pw2, w_out, loss_target, m_meta_tokens, m_pre_norm_g, m_post_norm_g, m_w_in, m_conv_w, m_conv_b, m_conv_ln_g, m_conv_ln_b, m_w_pw2, m_b_pw2, m_w_out, v_meta_tokens, v_pre_norm_g, v_post_norm_g, v_w_in, v_conv_w, v_conv_b, v_conv_ln_g, v_conv_ln_b, v_w_pw2, v_b_pw2, v_w_out):
    seq, d = x.shape[1], x.shape[2]
    depth = w_in.shape[0]
    length = N_META + seq
    lp = -(-length // ATT_BLOCK) * ATT_BLOCK
    tap_pad = ((0, 0), (0, CONV_PAD - CONV_WIDTH), (0, 0))

    shards = (w_in.astype(BF16), w_pw2.astype(BF16), w_out.astype(BF16), jnp.pad(conv_w, tap_pad))
    dims = [(1, w_in.shape[2]), (0, w_pw2.shape[1]), (0, w_out.shape[1]), (1, conv_w.shape[2])]
    layer_shards = lambda l: [s[l] for s in shards]

    first = _run_exchange(_gather_plan(layer_shards(0) + [meta_tokens], dims + [(1, meta_tokens.shape[1])]),
                          "gather_weights")
    weights0, meta_f = tuple(first[:4]), first[4]

    h0 = jnp.concatenate([meta_f, x[0], jnp.zeros((lp - length, d), F32)], axis=0)
    target_p = jnp.pad(loss_target[0], ((N_META, lp - length), (0, 0)))
    vecs = (pre_norm_g, post_norm_g, conv_b, conv_ln_g, conv_ln_b, b_pw2)
    loss, dh0, vec_grads, pieces = _local_step(
        h0, target_p, seq, vecs, depth, weights0=weights0,
        gather_next=lambda l: _gather_plan(layer_shards(l + 1), dims),
        reduce_layer=lambda grads: _reduce_plan(list(grads), dims))

    def update(k, w, m, v, name):
        outs = None
        for l in reversed(range(depth)):
            outs = _adamw([pieces[l][k], pieces[l][4 + k]], w, m, v, l, outs, name)
        return outs

    up_w_in = update(0, w_in, m_w_in, v_w_in, "adamw_w_in")
    up_w_pw2 = update(1, w_pw2, m_w_pw2, v_w_pw2, "adamw_w_pw2")
    up_w_out = update(2, w_out, m_w_out, v_w_out, "adamw_w_out")
    up_conv_w = [o[:, :CONV_WIDTH] for o in update(3, jnp.pad(conv_w, tap_pad), jnp.pad(m_conv_w, tap_pad),
                                                   jnp.pad(v_conv_w, tap_pad, constant_values=1.0), "adamw_conv_w")]

    two = lambda a: a.reshape(-1, d)
    vec_rows = [two(g) for g in vec_grads]
    n_vec = sum(a.shape[0] for a in vec_rows)
    pack = jnp.concatenate(vec_rows + [dh0[:N_META], jnp.full((8, d), loss, F32)], axis=0)
    pack = jnp.pad(pack, ((0, -pack.shape[0] % 8), (0, 0)))
    tot = _allsum_small(pack)
    loss_all = tot[n_vec + N_META, 0]

    cat = lambda arrs: jnp.concatenate([two(t) for t in arrs], axis=0)[None]
    small_m = (m_pre_norm_g, m_post_norm_g, m_conv_b, m_conv_ln_g, m_conv_ln_b, m_b_pw2)
    small_v = (v_pre_norm_g, v_post_norm_g, v_conv_b, v_conv_ln_g, v_conv_ln_b, v_b_pw2)
    up_small = _adamw([tot[None, :n_vec]], cat(vecs), cat(small_m), cat(small_v), 0, None, "adamw_vectors")

    def unpack(o):
        res, r0 = [], 0
        for t in vecs:
            nrow = t.size // d
            res.append(o[0, r0:r0 + nrow].reshape(t.shape))
            r0 += nrow
        return res

    up_small = [unpack(o) for o in up_small]
    chip = 2 * lax.axis_index("x") + lax.axis_index("y")
    mcols = meta_tokens.shape[1]
    g_meta = lax.dynamic_slice_in_dim(tot[n_vec:n_vec + N_META], chip * mcols, mcols, axis=1)
    up_meta = [o[0] for o in _adamw([g_meta[None]], meta_tokens[None], m_meta_tokens[None], v_meta_tokens[None],
                                    0, None, "adamw_meta")]

    grad_x = dh0[N_META:length][None]
    outs = [loss_all, grad_x]
    for j in range(4):
        pre, post, cb, lg, lb, bp = up_small[j]
        outs += [up_meta[j], pre, post, up_w_in[j], up_conv_w[j], cb, lg, lb, up_w_pw2[j], bp, up_w_out[j]]
    return tuple(outs)
```

```python
from typing import Callable, NamedTuple

import jax
import jax.numpy as jnp
from jax import lax
from jax.experimental import pallas as pl
from jax.experimental.pallas import tpu as pltpu

F32 = jnp.float32
BF16 = jnp.bfloat16

N_META = 16
D_CONV = 512
D_SB = 512
HEAD_DIM = 64
CONV_WIDTH = 31
CONV_PAD = 32
CONV_ROWS = 128
RMS_EPS = 1e-6
LN_EPS = 1e-5
Q_SCALE = HEAD_DIM ** -0.5

ADAM_LR = 0.001
ADAM_B1 = 0.9
ADAM_B2 = 0.999
ADAM_EPS = 1e-08
ADAM_WD = 0.01
ADAM_STEP = 10

LANES = 128
ROW_TILE = 256
ATT_BLOCK = 256
VMEM_LIMIT = 56 * 1024 * 1024
EXP_ZERO = -104.0
COUNT_LANE = LANES - 1

MESH = pl.DeviceIdType.MESH


def _pcall(body, **kw):
    return pl.pallas_call(body, **kw)


def _params(*sem):
    return pltpu.CompilerParams(dimension_semantics=sem, vmem_limit_bytes=VMEM_LIMIT)


def _sigmoid(x):
    return 1.0 / (1.0 + jnp.exp(-x))


def _silu_fwd_bwd(x):
    s = _sigmoid(x)
    return x * s, s * (1.0 + x * (1.0 - s))


def _nt(a, b):
    return lax.dot_general(a, b, (((1,), (1,)), ((), ())), preferred_element_type=F32)


def _tn(a, b):
    return lax.dot_general(a, b, (((0,), (0,)), ((), ())), preferred_element_type=F32)


def _nn(a, b):
    return jnp.dot(a, b, preferred_element_type=F32)


def _inproj(h, g, w):
    lp, d = h.shape
    n = w.shape[1]

    def body(h_ref, g_ref, w_ref, ew_ref, qkv_ref):
        x = h_ref[...]
        rstd = lax.rsqrt(jnp.mean(x * x, axis=-1, keepdims=True) + RMS_EPS)
        u = ((x * rstd) * g_ref[...]).astype(BF16)
        p = _nn(u, w_ref[...])
        ew_ref[:, 0:1536] = p[:, 0:1536]
        ew_ref[:, 1536:2048] = p[:, 3072:3584]
        qkv_ref[:, 0:512] = (p[:, 1536:2048] * Q_SCALE).astype(BF16)
        qkv_ref[:, 512:1536] = p[:, 2048:3072].astype(BF16)

    return _pcall(
        body, name="inproj_fwd", grid=(lp // ROW_TILE,),
        in_specs=[pl.BlockSpec((ROW_TILE, d), lambda i: (i, 0)),
                  pl.BlockSpec((1, d), lambda i: (0, 0)),
                  pl.BlockSpec((d, n), lambda i: (0, 0))],
        out_specs=[pl.BlockSpec((ROW_TILE, 2048), lambda i: (i, 0)),
                   pl.BlockSpec((ROW_TILE, 1536), lambda i: (i, 0))],
        out_shape=[jax.ShapeDtypeStruct((lp, 2048), F32), jax.ShapeDtypeStruct((lp, 1536), BF16)],
        compiler_params=_params("parallel"),
    )(h, g, w)


def _layer_norm_stats(c1):
    mu = jnp.mean(c1, axis=-1, keepdims=True)
    xc = c1 - mu
    var = jnp.mean(xc * xc, axis=-1, keepdims=True)
    rstd = lax.rsqrt(var + LN_EPS)
    return xc * rstd, rstd


def _shifted_copies(window, cols, shifted, tm):
    shifted[0] = window[:, cols]
    rows = tm + CONV_PAD - 8
    for b in range(1, 8):
        shifted[b, 0:rows, :] = window[pl.ds(b, rows), cols]


def _shifted_rows(shifted, shift, tm):
    b = shift % 8
    return shifted[b, pl.ds(pl.multiple_of(shift - b, 8), tm), :]


def _weighted(cw8_ref, j, cols, rows):
    w8 = cw8_ref[pl.ds(pl.multiple_of(j * 8, 8), 8), cols]
    r = rows.shape[0]
    return (rows.reshape(r // 8, 8, LANES) * w8[None]).reshape(r, LANES)


def _conv_fwd(ew, cw, cb, lng, lnb, wpw2, bpw2):
    lp = ew.shape[0]
    tm = ROW_TILE

    def body(ew_ref, cw_ref, cb_ref, lng_ref, lnb_ref, w_ref, b_ref, c1_ref, c4_ref, c5_ref, xbuf, shifted):
        @pl.when(pl.program_id(0) == 0)
        def _():
            xbuf[0:CONV_PAD, :] = jnp.zeros((CONV_PAD, D_CONV), F32)

        ga = ew_ref[:, 0:512]
        gb = ew_ref[:, 512:1024]
        cg = ew_ref[:, 1024:1536]
        xbuf[CONV_PAD:CONV_PAD + tm, :] = ga * _sigmoid(gb)
        for blk in range(D_CONV // LANES):
            cs = slice(blk * LANES, (blk + 1) * LANES)
            _shifted_copies(xbuf, cs, shifted, tm)
            for r0 in range(0, tm, CONV_ROWS):
                acc = jnp.zeros((CONV_ROWS, LANES), F32) + cb_ref[:, cs]
                for j in range(CONV_WIDTH):
                    acc = acc + _weighted(cw_ref, j, cs, _shifted_rows(
                        shifted, r0 + CONV_PAD - (CONV_WIDTH - 1) + j, CONV_ROWS))
                c1_ref[r0:r0 + CONV_ROWS, cs] = acc
        xbuf[0:CONV_PAD, :] = xbuf[tm:tm + CONV_PAD, :]
        xhat, _ = _layer_norm_stats(c1_ref[...])
        c2 = xhat * lng_ref[...] + lnb_ref[...]
        c3 = c2 * _sigmoid(c2)
        c4 = _nn(c3.astype(BF16), w_ref[...]) + b_ref[...]
        c4_ref[...] = c4
        c5_ref[...] = (c4 * (cg * _sigmoid(cg))).astype(BF16)

    vec = pl.BlockSpec((1, D_CONV), lambda i: (0, 0))
    row = pl.BlockSpec((tm, D_CONV), lambda i: (i, 0))
    return _pcall(
        body, name="conv_fwd", grid=(lp // tm,),
        in_specs=[pl.BlockSpec((tm, 1536), lambda i: (i, 0)),
                  pl.BlockSpec((8 * CONV_PAD, D_CONV), lambda i: (0, 0)),
                  vec, vec, vec,
                  pl.BlockSpec((D_CONV, D_CONV), lambda i: (0, 0)),
                  vec],
        out_specs=[row, row, row],
        out_shape=[jax.ShapeDtypeStruct((lp, D_CONV), F32), jax.ShapeDtypeStruct((lp, D_CONV), F32),
                   jax.ShapeDtypeStruct((lp, D_CONV), BF16)],
        scratch_shapes=[pltpu.VMEM((tm + CONV_PAD, D_CONV), F32), pltpu.VMEM((8, tm + CONV_PAD, LANES), F32)],
        compiler_params=_params("arbitrary"),
    )(ew, cw, cb, lng, lnb, wpw2, bpw2)


def _split_matmul(x, m01):
    hi = x.astype(BF16)
    lo = (x - hi.astype(F32)).astype(BF16)
    t = _nn(jnp.concatenate([hi, lo], axis=0), m01)
    m = x.shape[0]
    return t[:m] + t[m:]


def _sb_block(q2, kj, z_mask, tri):
    z = _nt(q2, kj)
    ls = -(jnp.maximum(z, 0.0) + jnp.log(1.0 + jnp.exp(-jnp.abs(z))))
    if z_mask is not None:
        ls = jnp.where(z_mask, ls, 0.0)
    return z, ls, _split_matmul(ls, tri)


def _attn_masks():
    lane = lax.broadcasted_iota(jnp.int32, (1, LANES), 1)
    row = lax.broadcasted_iota(jnp.int32, (2 * ATT_BLOCK, ATT_BLOCK), 0)
    col = lax.broadcasted_iota(jnp.int32, (2 * ATT_BLOCK, ATT_BLOCK), 1)
    return lane < HEAD_DIM, col < (row & (ATT_BLOCK - 1))


def _stack_heads(x, first_head):
    zero = jnp.zeros_like(x)
    return jnp.concatenate([jnp.where(first_head, x, zero), jnp.where(first_head, zero, x)], axis=0)


def _unstack_heads(x2, first_head):
    rows = x2.shape[0] // 2
    return jnp.where(first_head, x2[:rows], x2[rows:])


def _hosted(plan, n_out):
    if plan is None:
        return (lambda rest: ((), rest, (), ())), [], [], [], [], []
    n_in, n_x = len(plan.inputs), len(plan.out_shapes)

    def split(rest):
        return (rest[:n_in], rest[n_in:n_in + n_out], rest[n_in + n_out:n_in + n_out + n_x],
                rest[n_in + n_out + n_x:])

    return split, list(plan.inputs), [ANY] * n_in, [ANY] * n_x, list(plan.out_shapes), list(plan.scratch)


def _attn_fwd(qkv, tri, plan=None):
    lp = qkv.shape[0]
    bq = ATT_BLOCK
    npair = D_SB // LANES
    nq = lp // bq
    assert nq <= COUNT_LANE
    split, x_args, x_in_specs, x_out_specs, x_out_shapes, x_scratch = _hosted(plan, 2)

    def body(q_ref, k_ref, v_ref, tri_ref, *rest):
        x_in, (o_ref, carry_ref), x_out, x_sems = split(rest)
        i = pl.program_id(1)
        if plan is not None:
            @pl.when(jnp.logical_and(pl.program_id(0) == 0, i == 0))
            def _():
                plan.start(x_in, x_out, x_sems)

        first_head, vis = _attn_masks()
        lane = lax.broadcasted_iota(jnp.int32, (1, LANES), 1)
        q2 = _stack_heads(q_ref[...], first_head)
        tri_m = tri_ref[...]

        def blocks(js, state, masks):
            c, acc, cmat = state
            offs = [pl.multiple_of(j * bq, bq) for j in js]
            zs = [_nt(q2, k_ref[pl.ds(off, bq), :]) for off in offs]
            lss = []
            for z, mask in zip(zs, masks):
                ls = -(jnp.maximum(z, 0.0) + jnp.log(1.0 + jnp.exp(-jnp.abs(z))))
                lss.append(ls if mask is None else jnp.where(mask, ls, 0.0))
            tails = [_split_matmul(ls, tri_m) for ls in lss]
            probs = []
            for j, z, ls, tail, mask in zip(js, zs, lss, tails, masks):
                a = jnp.exp(z + ls + tail + c)
                probs.append(a if mask is None else jnp.where(mask, a, 0.0))
                cmat = jnp.where(lane == j, c, cmat)
                c = c + tail[:, 0:1] + ls[:, 0:1]
            for off, a in zip(offs, probs):
                acc = acc + _nn(a.astype(BF16), v_ref[pl.ds(off, bq), :])
            return c, acc, cmat

        def live(state):
            return jnp.max(state[0]) > EXP_ZERO

        zeros = jnp.zeros((2 * bq, LANES), F32)
        init = (jnp.zeros((2 * bq, 1), F32), zeros, zeros)
        state = lax.cond(i > 0, lambda st: blocks([i, i - 1], st, [vis, None]),
                         lambda st: blocks([i], st, [vis]), init)
        rest = jnp.maximum(i - 1, 0)

        def pair(carry):
            t, st = carry
            j = i - 2 - 2 * t
            return t + 1, blocks([j, j - 1], st, [None, None])

        trips, state = lax.while_loop(lambda ca: jnp.logical_and(ca[0] < rest // 2, live(ca[1])), pair, (0, state))
        last = jnp.logical_and(jnp.logical_and(rest % 2 == 1, trips == rest // 2), live(state))
        state = lax.cond(last, lambda st: blocks([0], st, [None]), lambda st: st, state)
        n_done = jnp.minimum(i + 1, 2) + 2 * trips + last.astype(jnp.int32)
        cmat = jnp.where(lane == COUNT_LANE, n_done.astype(F32), state[2])
        carry_ref[:, 0:LANES] = cmat[:bq]
        carry_ref[:, LANES:] = cmat[bq:]
        o_ref[...] = _unstack_heads(state[1], first_head)
        if plan is not None:
            @pl.when(jnp.logical_and(pl.program_id(0) == npair - 1, i == nq - 1))
            def _():
                plan.finish(x_in, x_out, x_sems)

    outs = _pcall(
        body, name="attn_fwd" if plan is None else "attn_fwd_gather", grid=(npair, nq),
        in_specs=[pl.BlockSpec((bq, LANES), lambda p, i: (i, p)),
                  pl.BlockSpec((lp, LANES), lambda p, i: (0, npair + p)),
                  pl.BlockSpec((lp, LANES), lambda p, i: (0, 2 * npair + p)),
                  pl.BlockSpec((bq, bq), lambda p, i: (0, 0))] + x_in_specs,
        out_specs=[pl.BlockSpec((bq, LANES), lambda p, i: (i, p)),
                   pl.BlockSpec((bq, 2 * LANES), lambda p, i: (i, p))] + x_out_specs,
        out_shape=[jax.ShapeDtypeStruct((lp, D_SB), F32), jax.ShapeDtypeStruct((lp, 2 * D_SB), F32)] + x_out_shapes,
        scratch_shapes=x_scratch,
        compiler_params=_params("arbitrary", "arbitrary"),
    )(qkv, qkv, qkv, tri, *x_args)
    return outs[0], outs[1], outs[2:]


def _outproj(c5, att, ew, h, w, g):
    lp, d = h.shape
    tm = ROW_TILE

    def body(c5_ref, att_ref, sg_ref, h_ref, w_ref, g_ref, hn_ref, cat_ref, mix_ref):
        sg = sg_ref[...]
        s = att_ref[...] * (sg * _sigmoid(sg))
        cat_ref[:, 0:D_CONV] = c5_ref[...]
        cat_ref[:, D_CONV:] = s.astype(BF16)
        mixed = _nn(cat_ref[...], w_ref[...])
        mix_ref[...] = mixed
        rstd = lax.rsqrt(jnp.mean(mixed * mixed, axis=-1, keepdims=True) + RMS_EPS)
        hn_ref[...] = h_ref[...] + (mixed * rstd) * g_ref[...]

    half = pl.BlockSpec((tm, 512), lambda i: (i, 0))
    full = pl.BlockSpec((tm, d), lambda i: (i, 0))
    return _pcall(
        body, name="outproj_fwd", grid=(lp // tm,),
        in_specs=[half, half, pl.BlockSpec((tm, 512), lambda i: (i, 3)), full,
                  pl.BlockSpec((d, d), lambda i: (0, 0)), pl.BlockSpec((1, d), lambda i: (0, 0))],
        out_specs=[full, full, full],
        out_shape=[jax.ShapeDtypeStruct((lp, d), F32), jax.ShapeDtypeStruct((lp, d), BF16),
                   jax.ShapeDtypeStruct((lp, d), F32)],
        compiler_params=_params("parallel"),
    )(c5, att, ew, h, w, g)


def _loss_head(h, target, seq):
    lp, d = h.shape
    tm = ROW_TILE

    def body(h_ref, t_ref, dh_ref, loss_ref):
        i = pl.program_id(0)

        @pl.when(i == 0)
        def _():
            loss_ref[...] = jnp.zeros_like(loss_ref)

        row = i * tm + lax.broadcasted_iota(jnp.int32, (tm, 1), 0)
        real = jnp.logical_and(row >= N_META, row < N_META + seq)
        diff = jnp.where(real, h_ref[...] - t_ref[...], 0.0)
        dh_ref[...] = diff * (1.0 / d)
        loss_ref[...] += 0.5 * jnp.sum(jnp.sum(diff * diff, axis=-1, keepdims=True) * (1.0 / d))

    full = pl.BlockSpec((tm, d), lambda i: (i, 0))
    return _pcall(
        body, name="loss_head", grid=(lp // tm,),
        in_specs=[full, full],
        out_specs=[full, pl.BlockSpec((8, LANES), lambda i: (0, 0))],
        out_shape=[jax.ShapeDtypeStruct((lp, d), F32), jax.ShapeDtypeStruct((8, LANES), F32)],
        compiler_params=_params("arbitrary"),
    )(h, target)


def _outproj_bwd(dh, mixed, g, w, att, ew, c4):
    lp, d = dh.shape
    tm = ROW_TILE

    def body(dh_ref, mix_ref, g_ref, w_ref, att_ref, cg_ref, sg_ref, c4_ref,
             dmix_ref, datt_ref, dsg_ref, dc4_ref, dcg_ref, dg_ref, db_ref):
        @pl.when(pl.program_id(0) == 0)
        def _():
            dg_ref[...] = jnp.zeros_like(dg_ref)
            db_ref[...] = jnp.zeros_like(db_ref)

        mixed = mix_ref[...]
        dhv = dh_ref[...]
        rstd = lax.rsqrt(jnp.mean(mixed * mixed, axis=-1, keepdims=True) + RMS_EPS)
        n = mixed * rstd
        dg_ref[...] += jnp.sum(dhv * n, axis=0, keepdims=True)
        dn = dhv * g_ref[...]
        dmix = (rstd * (dn - n * jnp.mean(dn * n, axis=-1, keepdims=True))).astype(BF16)
        dmix_ref[...] = dmix
        dcat = _nt(dmix, w_ref[...])
        dc5 = dcat[:, 0:D_CONV]
        ds = dcat[:, D_CONV:]
        silu_sg, dsilu_sg = _silu_fwd_bwd(sg_ref[...])
        datt_ref[...] = (ds * silu_sg).astype(BF16)
        dsg_ref[...] = (ds * att_ref[...] * dsilu_sg).astype(BF16)
        silu_cg, dsilu_cg = _silu_fwd_bwd(cg_ref[...])
        dc4 = dc5 * silu_cg
        db_ref[...] += jnp.sum(dc4, axis=0, keepdims=True)
        dc4_ref[...] = dc4.astype(BF16)
        dcg_ref[...] = (dc5 * c4_ref[...] * dsilu_cg).astype(BF16)

    half = pl.BlockSpec((tm, 512), lambda i: (i, 0))
    full = pl.BlockSpec((tm, d), lambda i: (i, 0))
    hb = jax.ShapeDtypeStruct((lp, 512), BF16)
    return _pcall(
        body, name="outproj_bwd", grid=(lp // tm,),
        in_specs=[full, full, pl.BlockSpec((1, d), lambda i: (0, 0)), pl.BlockSpec((d, d), lambda i: (0, 0)),
                  half, pl.BlockSpec((tm, 512), lambda i: (i, 2)), pl.BlockSpec((tm, 512), lambda i: (i, 3)), half],
        out_specs=[full, half, half, half, half,
                   pl.BlockSpec((1, d), lambda i: (0, 0)), pl.BlockSpec((1, 512), lambda i: (0, 0))],
        out_shape=[jax.ShapeDtypeStruct((lp, d), BF16), hb, hb, hb, hb,
                   jax.ShapeDtypeStruct((1, d), F32), jax.ShapeDtypeStruct((1, 512), F32)],
        compiler_params=_params("arbitrary"),
    )(dh, mixed, g, w, att, ew, ew, c4)


def _attn_bwd(qkv, carries, datt, tri, upper, plan=None):
    lp = qkv.shape[0]
    bq = ATT_BLOCK
    npair = D_SB // LANES
    nq = lp // bq
    split, x_args, x_in_specs, x_out_specs, x_out_shapes, x_scratch = _hosted(plan, 3)

    def body(q_ref, k_ref, v_ref, carry_ref, do_ref, tri_ref, upper_ref, *rest):
        x_in, (dq_ref, dk_ref, dv_ref), x_out, x_sems = split(rest)
        i = pl.program_id(1)
        if plan is not None:
            @pl.when(jnp.logical_and(pl.program_id(0) == 0, i == 0))
            def _():
                plan.start(x_in, x_out, x_sems)

        @pl.when(i == 0)
        def _():
            dk_ref[...] = jnp.zeros_like(dk_ref)
            dv_ref[...] = jnp.zeros_like(dv_ref)

        first_head, vis = _attn_masks()
        lane = lax.broadcasted_iota(jnp.int32, (1, LANES), 1)
        q2 = _stack_heads(q_ref[...], first_head)
        do2 = _stack_heads(do_ref[...], first_head)
        cmat = jnp.concatenate([carry_ref[:, 0:LANES], carry_ref[:, LANES:]], axis=0)
        tri_m = tri_ref[...]
        upper_m = upper_ref[...]

        def blocks(js, state, masks):
            run, dq = state
            offs = [pl.multiple_of(j * bq, bq) for j in js]
            zs = [_nt(q2, k_ref[pl.ds(off, bq), :]) for off in offs]
            lss = []
            for z, mask in zip(zs, masks):
                ls = -(jnp.maximum(z, 0.0) + jnp.log(1.0 + jnp.exp(-jnp.abs(z))))
                lss.append(ls if mask is None else jnp.where(mask, ls, 0.0))
            tails = [_split_matmul(ls, tri_m) for ls in lss]
            das = [_nt(do2, v_ref[pl.ds(off, bq), :]) for off in offs]
            probs, des = [], []
            for j, z, ls, tail, da, mask in zip(js, zs, lss, tails, das, masks):
                c = jnp.sum(jnp.where(lane == j, cmat, 0.0), axis=-1, keepdims=True)
                a = jnp.exp(z + ls + tail + c)
                a = a if mask is None else jnp.where(mask, a, 0.0)
                probs.append(a.astype(BF16))
                des.append(da * a)
            prefixes = [_split_matmul(de, upper_m) for de in des]
            dzs = []
            for z, ls, de, prefix, mask in zip(zs, lss, des, prefixes, masks):
                beta = jnp.exp(z + ls)
                dz = de - beta * (de + run + prefix)
                dzs.append((dz if mask is None else jnp.where(mask, dz, 0.0)).astype(BF16))
                run = run + prefix[:, bq - 1:bq] + de[:, bq - 1:bq]
            for off, dzb, ab in zip(offs, dzs, probs):
                dq = dq + _nn(dzb, k_ref[pl.ds(off, bq), :])
                dk_ref[pl.ds(off, bq), :] += _tn(dzb, q2)
                dv_ref[pl.ds(off, bq), :] += _tn(ab, do2)
            return run, dq

        n_done = jnp.max(carry_ref[:, COUNT_LANE:COUNT_LANE + 1]).astype(jnp.int32)
        n_done = jnp.clip(n_done, 1, i + 1)
        before = jnp.maximum(n_done - 2, 0)
        j0 = i - n_done + 1
        odd = before % 2
        state = (jnp.zeros((2 * bq, 1), F32), jnp.zeros((2 * bq, LANES), F32))
        state = lax.cond(odd == 1, lambda st: blocks([j0], st, [None]), lambda st: st, state)
        state = lax.fori_loop(
            0, before // 2, lambda t, st: blocks([j0 + odd + 2 * t, j0 + odd + 2 * t + 1], st, [None, None]), state)
        state = lax.cond(n_done > 1, lambda st: blocks([i - 1, i], st, [None, vis]),
                         lambda st: blocks([i], st, [vis]), state)
        dq_ref[...] = (_unstack_heads(state[1], first_head) * Q_SCALE).astype(BF16)
        if plan is not None:
            @pl.when(jnp.logical_and(pl.program_id(0) == npair - 1, i == nq - 1))
            def _():
                plan.finish(x_in, x_out, x_sems)

    qb = pl.BlockSpec((bq, LANES), lambda p, i: (i, p))
    colb = pl.BlockSpec((lp, LANES), lambda p, i: (0, p))
    sq = pl.BlockSpec((bq, bq), lambda p, i: (0, 0))
    outs = _pcall(
        body, name="attn_bwd" if plan is None else "attn_bwd_reduce", grid=(npair, nq),
        in_specs=[qb,
                  pl.BlockSpec((lp, LANES), lambda p, i: (0, npair + p)),
                  pl.BlockSpec((lp, LANES), lambda p, i: (0, 2 * npair + p)),
                  pl.BlockSpec((bq, 2 * LANES), lambda p, i: (i, p)), qb, sq, sq] + x_in_specs,
        out_specs=[qb, colb, colb] + x_out_specs,
        out_shape=[jax.ShapeDtypeStruct((lp, D_SB), BF16), jax.ShapeDtypeStruct((lp, D_SB), F32),
                   jax.ShapeDtypeStruct((lp, D_SB), F32)] + x_out_shapes,
        scratch_shapes=x_scratch,
        compiler_params=_params("arbitrary", "arbitrary"),
    )(qkv, qkv, qkv, carries, datt, tri, upper, *x_args)
    return outs[0], outs[1], outs[2], outs[3:]


def _conv_bwd(dc4, c1, ew, cw, lng, lnb, wpw2):
    lp = ew.shape[0]
    tm = ROW_TILE
    nt = lp // tm
    halo_per_tile = tm // CONV_PAD

    def body(dc4_ref, c1_ref, ew_ref, halo_ref, cw_ref, lng_ref, lnb_ref, w_ref,
             dga_ref, dgb_ref, c3_ref, dcw_ref, dcb_ref, dlng_ref, dlnb_ref, xbuf, dbuf, shifted, wacc):
        step = pl.program_id(0)

        @pl.when(step == 0)
        def _():
            wacc[...] = jnp.zeros_like(wacc)
            dcb_ref[...] = jnp.zeros_like(dcb_ref)
            dlng_ref[...] = jnp.zeros_like(dlng_ref)
            dlnb_ref[...] = jnp.zeros_like(dlnb_ref)
            dbuf[tm:tm + CONV_PAD, :] = jnp.zeros((CONV_PAD, D_CONV), F32)

        dc3 = _nt(dc4_ref[...], w_ref[...])
        xhat, rstd = _layer_norm_stats(c1_ref[...])
        c2 = xhat * lng_ref[...] + lnb_ref[...]
        c3, dsilu = _silu_fwd_bwd(c2)
        c3_ref[...] = c3.astype(BF16)
        dc2 = dc3 * dsilu
        dlng_ref[...] += jnp.sum(dc2 * xhat, axis=0, keepdims=True)
        dlnb_ref[...] += jnp.sum(dc2, axis=0, keepdims=True)
        dxhat = dc2 * lng_ref[...]
        dc1 = rstd * (dxhat - jnp.mean(dxhat, axis=-1, keepdims=True)
                      - xhat * jnp.mean(dxhat * xhat, axis=-1, keepdims=True))
        dcb_ref[...] += jnp.sum(dc1, axis=0, keepdims=True)
        dbuf[0:tm, :] = dc1

        ga = ew_ref[:, 0:512]
        sgb = _sigmoid(ew_ref[:, 512:1024])
        xbuf[CONV_PAD:CONV_PAD + tm, :] = ga * sgb
        first_tile = step == nt - 1
        halo = halo_ref[:, 0:512] * _sigmoid(halo_ref[:, 512:1024])
        xbuf[0:CONV_PAD, :] = jnp.where(first_tile, 0.0, halo)

        for cb in range(D_CONV // LANES):
            cs = slice(cb * LANES, (cb + 1) * LANES)
            _shifted_copies(dbuf, cs, shifted, tm)
            for r0 in range(0, tm, CONV_ROWS):
                rs = slice(r0, r0 + CONV_ROWS)
                dc0 = jnp.zeros((CONV_ROWS, LANES), F32)
                for j in range(CONV_WIDTH):
                    dc0 = dc0 + _weighted(cw_ref, j, cs, _shifted_rows(
                        shifted, r0 + CONV_WIDTH - 1 - j, CONV_ROWS))
                dga_ref[rs, cs] = (dc0 * sgb[rs, cs]).astype(BF16)
                dgb_ref[rs, cs] = (dc0 * ga[rs, cs] * sgb[rs, cs] * (1.0 - sgb[rs, cs])).astype(BF16)
            _shifted_copies(xbuf, cs, shifted, tm)
            for r0 in range(0, tm, CONV_ROWS):
                d1 = dbuf[r0:r0 + CONV_ROWS, cs]

                for j in range(CONV_WIDTH):
                    prod = d1 * _shifted_rows(shifted, r0 + CONV_PAD - (CONV_WIDTH - 1) + j, CONV_ROWS)
                    wacc[j * 8:(j + 1) * 8, cs] += jnp.sum(prod.reshape(CONV_ROWS // 8, 8, LANES), axis=0)
        dbuf[tm:tm + CONV_PAD, :] = dbuf[0:CONV_PAD, :]

        @pl.when(step == nt - 1)
        def _():
            dcw_ref[...] = jnp.sum(wacc[...].reshape(CONV_PAD, 8, D_CONV), axis=1)

    rev = lambda i: (nt - 1 - i, 0)
    row = pl.BlockSpec((tm, D_CONV), rev)
    vec = pl.BlockSpec((1, D_CONV), lambda i: (0, 0))
    hb = jax.ShapeDtypeStruct((lp, D_CONV), BF16)
    vs = jax.ShapeDtypeStruct((1, D_CONV), F32)
    return _pcall(
        body, name="conv_bwd", grid=(nt,),
        in_specs=[row, row, pl.BlockSpec((tm, 1024), rev),
                  pl.BlockSpec((CONV_PAD, 1024), lambda i: (jnp.maximum((nt - 1 - i) * halo_per_tile - 1, 0), 0)),
                  pl.BlockSpec((8 * CONV_PAD, D_CONV), lambda i: (0, 0)), vec, vec,
                  pl.BlockSpec((D_CONV, D_CONV), lambda i: (0, 0))],
        out_specs=[row, row, row, pl.BlockSpec((CONV_PAD, D_CONV), lambda i: (0, 0)), vec, vec, vec],
        out_shape=[hb, hb, hb, jax.ShapeDtypeStruct((CONV_PAD, D_CONV), F32), vs, vs, vs],
        scratch_shapes=[pltpu.VMEM((tm + CONV_PAD, D_CONV), F32), pltpu.VMEM((tm + CONV_PAD, D_CONV), F32),
                        pltpu.VMEM((8, tm + CONV_PAD, LANES), F32), pltpu.VMEM((8 * CONV_PAD, D_CONV), F32)],
        compiler_params=_params("arbitrary"),
    )(dc4, c1, ew, ew, cw, lng, lnb, wpw2)


def _inproj_bwd(dga, dgb, dcg, dq, dk, dv, dsg, h, g, w, dh_out):
    lp, d = h.shape
    n = w.shape[1]
    tm = ROW_TILE

    def body(dga_ref, dgb_ref, dcg_ref, dq_ref, dk_ref, dv_ref, dsg_ref, h_ref, g_ref, w_ref, dho_ref,
             dh_ref, dproj_ref, u_ref, dg_ref):
        @pl.when(pl.program_id(0) == 0)
        def _():
            dg_ref[...] = jnp.zeros_like(dg_ref)

        dproj_ref[:, 0:512] = dga_ref[...]
        dproj_ref[:, 512:1024] = dgb_ref[...]
        dproj_ref[:, 1024:1536] = dcg_ref[...]
        dproj_ref[:, 1536:2048] = dq_ref[...]
        dproj_ref[:, 2048:2560] = dk_ref[...].astype(BF16)
        dproj_ref[:, 2560:3072] = dv_ref[...].astype(BF16)
        dproj_ref[:, 3072:3584] = dsg_ref[...]
        du = _nt(dproj_ref[...], w_ref[...])
        x = h_ref[...]
        rstd = lax.rsqrt(jnp.mean(x * x, axis=-1, keepdims=True) + RMS_EPS)
        nrm = x * rstd
        u_ref[...] = (nrm * g_ref[...]).astype(BF16)
        dg_ref[...] += jnp.sum(du * nrm, axis=0, keepdims=True)
        dn = du * g_ref[...]
        dh_ref[...] = dho_ref[...] + rstd * (dn - nrm * jnp.mean(dn * nrm, axis=-1, keepdims=True))

    half = pl.BlockSpec((tm, 512), lambda i: (i, 0))
    full = pl.BlockSpec((tm, d), lambda i: (i, 0))
    return _pcall(
        body, name="inproj_bwd", grid=(lp // tm,),
        in_specs=[half] * 7 + [full, pl.BlockSpec((1, d), lambda i: (0, 0)),
                               pl.BlockSpec((d, n), lambda i: (0, 0)), full],
        out_specs=[full, pl.BlockSpec((tm, n), lambda i: (i, 0)), full, pl.BlockSpec((1, d), lambda i: (0, 0))],
        out_shape=[jax.ShapeDtypeStruct((lp, d), F32), jax.ShapeDtypeStruct((lp, n), BF16),
                   jax.ShapeDtypeStruct((lp, d), BF16), jax.ShapeDtypeStruct((1, d), F32)],
        compiler_params=_params("arbitrary"),
    )(dga, dgb, dcg, dq, dk, dv, dsg, h, g, w, dh_out)


def _row_split(m, parts):
    tm = m // parts
    assert tm * parts == m and tm % 16 == 0, (m, parts)
    return tm


def _matmul_tn(x, dy, tn, name):
    m, k = x.shape
    n = dy.shape[1]
    steps = 4 if m % 64 == 0 else 1
    tm = _row_split(m, steps)

    def body(x_ref, dy_ref, o_ref, acc_ref):
        r = pl.program_id(1)

        @pl.when(r == 0)
        def _():
            acc_ref[...] = jnp.zeros_like(acc_ref)

        acc_ref[...] += _tn(x_ref[...], dy_ref[...])

        @pl.when(r == steps - 1)
        def _():
            o_ref[...] = acc_ref[...].astype(BF16)

    return _pcall(
        body, name=name, grid=(n // tn, steps),
        in_specs=[pl.BlockSpec((tm, k), lambda j, r: (r, 0)), pl.BlockSpec((tm, tn), lambda j, r: (r, j))],
        out_specs=pl.BlockSpec((k, tn), lambda j, r: (0, j)),
        out_shape=jax.ShapeDtypeStruct((k, n), BF16),
        scratch_shapes=[pltpu.VMEM((k, tn), F32)],
        compiler_params=_params("parallel", "arbitrary"),
    )(x, dy)


def _local_step(h0, target_p, seq, vecs, depth, all_weights=None, weights0=None, gather_next=None,
                reduce_layer=None):
    pre_g, post_g, conv_b, ln_g, ln_b, b_pw2 = vecs
    ar = jnp.arange(ATT_BLOCK)
    tri = (ar[:, None] > ar[None, :]).astype(BF16)
    upper = (ar[:, None] < ar[None, :]).astype(BF16)
    row = lambda a, l: a[l][None, :]

    weights = list(all_weights) if all_weights is not None else [weights0] + [None] * (depth - 1)
    saved = []
    h = h0
    for l in range(depth):
        w_in, w_pw2, w_out, conv_w = weights[l]
        ew, qkv = _inproj(h, row(pre_g, l), w_in)
        conv_w = jnp.repeat(conv_w, 8, axis=0)
        weights[l] = (w_in, w_pw2, w_out, conv_w)
        c1, c4, c5 = _conv_fwd(ew, conv_w, row(conv_b, l), row(ln_g, l), row(ln_b, l), w_pw2, row(b_pw2, l))
        plan = gather_next(l) if gather_next is not None and l + 1 < depth else None
        att, carries, gathered = _attn_fwd(qkv, tri, plan)
        if plan is not None:
            weights[l + 1] = tuple(gathered)
        hn, cat, mixed = _outproj(c5, att, ew, h, w_out, row(post_g, l))
        saved.append((h, ew, qkv, c1, c4, att, carries, cat, mixed))
        h = hn

    dh, loss = _loss_head(h, target_p, seq)

    vec_grads = [None] * depth
    mat_grads = [None] * depth
    pending = None
    for l in reversed(range(depth)):
        w_in, w_pw2, w_out, conv_w = weights[l]
        h_in, ew, qkv, c1, c4, att, carries, cat, mixed = saved[l]
        dmix, datt, dsg, dc4, dcg, dpost, dbpw2 = _outproj_bwd(dh, mixed, row(post_g, l), w_out, att, ew, c4)
        dw_out = _matmul_tn(cat, dmix, 512, "dw_out")
        dq, dk, dv, landed = _attn_bwd(qkv, carries, datt, tri, upper, pending)
        if pending is not None:
            mat_grads[l + 1] = landed
        dga, dgb, c3, dcw, dcb, dlng, dlnb = _conv_bwd(dc4, c1, ew, conv_w, row(ln_g, l), row(ln_b, l), w_pw2)
        dw_pw2 = _matmul_tn(c3, dc4, 512, "dw_pw2")
        dh, dproj, u, dpre = _inproj_bwd(dga, dgb, dcg, dq, dk, dv, dsg, h_in, row(pre_g, l), w_in, dh)
        dw_in = _matmul_tn(u, dproj, 896, "dw_in")
        vec_grads[l] = (dpre[0], dpost[0], dcb[0], dlng[0], dlnb[0], dbpw2[0])
        mats = (dw_in, dw_pw2, dw_out, dcw)
        if reduce_layer is None:
            mat_grads[l] = mats
        else:
            pending = reduce_layer(mats)
    if pending is not None:
        mat_grads[0] = _run_exchange(pending, "reduce_grads")

    vec_grads = [jnp.stack([g[k] for g in vec_grads]) for k in range(len(vecs))]
    return loss[0, 0], dh, vec_grads, mat_grads


N_CHIPS = 4
ANY = pl.BlockSpec(memory_space=pl.ANY)


def _chip_peers():
    x, y, c = lax.axis_index("x"), lax.axis_index("y"), lax.axis_index("c")
    return x, y, c, [(x, 1 - y), (1 - x, y), (1 - x, 1 - y)]


def _shard_slices(refs, dims, idx):
    out = []
    for ref, (axis, size) in zip(refs, dims):
        assert size % LANES == 0
        start = pl.multiple_of(idx * size, LANES)
        sl = [slice(None)] * len(ref.shape)
        sl[axis] = pl.ds(start, size)
        out.append(ref.at[tuple(sl)])
    return out


class _Exchange(NamedTuple):
    inputs: list
    out_shapes: list
    scratch: list
    start: Callable
    finish: Callable


def _run_exchange(plan, name):
    n_in, n_out = len(plan.inputs), len(plan.out_shapes)

    def body(*refs):
        parts = refs[:n_in], refs[n_in:n_in + n_out], refs[n_in + n_out:]
        plan.start(*parts)
        plan.finish(*parts)

    return _pcall(body, name=name, in_specs=[ANY] * n_in, out_specs=[ANY] * n_out, out_shape=plan.out_shapes,
                  scratch_shapes=plan.scratch)(*plan.inputs)


def _gather_plan(shards, dims):
    n = len(shards)
    full_shapes = []
    for s, (axis, size) in zip(shards, dims):
        shp = list(s.shape)
        shp[axis] = size * N_CHIPS
        full_shapes.append(jax.ShapeDtypeStruct(tuple(shp), s.dtype))

    def copies(srcs, outs, sems):
        send, recv, loc = sems
        x, y, c, peers = _chip_peers()
        mine = _shard_slices(outs, dims, 2 * x + y)
        local = [pltpu.make_async_copy(s, d, loc.at[a]) for a, (s, d) in enumerate(zip(srcs, mine))]
        sends = [pltpu.make_async_remote_copy(s, d, send.at[k, a], recv.at[k, a],
                                              device_id=(px, py, c), device_id_type=MESH)
                 for k, (px, py) in enumerate(peers) for a, (s, d) in enumerate(zip(srcs, mine))]
        arrivals = [pltpu.make_async_remote_copy(s, d, send.at[k, a], recv.at[k, a],
                                                 device_id=(px, py, c), device_id_type=MESH)
                    for k, (px, py) in enumerate(peers)
                    for a, (s, d) in enumerate(zip(srcs, _shard_slices(outs, dims, 2 * px + py)))]
        return local, sends, arrivals

    def start(srcs, outs, sems):
        local, sends, _ = copies(srcs, outs, sems)
        for cp in local + sends:
            cp.start()

    def finish(srcs, outs, sems):
        local, sends, arrivals = copies(srcs, outs, sems)
        for cp in arrivals:
            cp.wait_recv()
        for cp in sends:
            cp.wait_send()
        for cp in local:
            cp.wait()

    scratch = [pltpu.SemaphoreType.DMA((3, n)), pltpu.SemaphoreType.DMA((3, n)), pltpu.SemaphoreType.DMA((n,))]
    return _Exchange(list(shards), full_shapes, scratch, start, finish)


def _reduce_plan(grads, dims):
    n = len(grads)
    piece_shapes = []
    for g, (axis, size) in zip(grads, dims):
        shp = list(g.shape)
        shp[axis] = size
        piece_shapes.append(jax.ShapeDtypeStruct((N_CHIPS,) + tuple(shp), g.dtype))

    def copies(srcs, outs, sems):
        mine, theirs = outs[:n], outs[n:]
        send, recv, loc = sems
        x, y, c, peers = _chip_peers()
        sibling = (x, y, 1 - c)
        own = _shard_slices(srcs, dims, 2 * x + y)

        def remote(src, dst, slot, a, dev):
            return pltpu.make_async_remote_copy(src, dst, send.at[slot, a], recv.at[slot, a],
                                                device_id=dev, device_id_type=MESH)

        local = [pltpu.make_async_copy(own[a], mine[a].at[3], loc.at[a]) for a in range(n)]
        to_sibling = [remote(own[a], theirs[a].at[3], 3, a, sibling) for a in range(n)]
        to_chips = [remote(src, mine[a].at[k], k, a, (px, py, c))
                    for k, (px, py) in enumerate(peers)
                    for a, src in enumerate(_shard_slices(srcs, dims, 2 * px + py))]
        passed_on = [remote(mine[a].at[k], theirs[a].at[k], 4 + k, a, sibling) for k in range(3) for a in range(n)]
        return local, to_sibling, to_chips, passed_on

    def start(srcs, outs, sems):
        local, to_sibling, to_chips, _ = copies(srcs, outs, sems)
        for cp in local + to_sibling + to_chips:
            cp.start()

    def finish(srcs, outs, sems):
        local, to_sibling, to_chips, passed_on = copies(srcs, outs, sems)
        for arrived, onward in zip(to_chips, passed_on):
            arrived.wait_recv()
            onward.start()
        for cp in to_sibling + passed_on:
            cp.wait_recv()
        for cp in to_sibling + to_chips + passed_on:
            cp.wait_send()
        for cp in local:
            cp.wait()

    scratch = [pltpu.SemaphoreType.DMA((7, n)), pltpu.SemaphoreType.DMA((7, n)), pltpu.SemaphoreType.DMA((n,))]
    return _Exchange(list(grads), piece_shapes * 2, scratch, start, finish)


def _allsum_small(pack):
    rows, cols = pack.shape
    ndev = 8

    def body(p_ref, o_ref, buf, send, recv):
        x, y, c = lax.axis_index("x"), lax.axis_index("y"), lax.axis_index("c")
        me = 4 * x + 2 * y + c
        buf[me] = p_ref[...]
        started = []
        for r in range(1, ndev):
            bx, by, bc = (r >> 2) & 1, (r >> 1) & 1, r & 1
            dev = (x ^ bx, y ^ by, c ^ bc)
            cp = pltpu.make_async_remote_copy(p_ref, buf.at[me], send.at[r], recv.at[r],
                                              device_id=dev, device_id_type=MESH)
            cp.start()
            started.append(cp)
        for r in range(1, ndev):
            pltpu.make_async_remote_copy(p_ref, buf.at[me ^ r], send.at[r], recv.at[r],
                                         device_id=(x, y, c), device_id_type=MESH).wait_recv()
        for cp in started:
            cp.wait_send()
        acc = buf[0]
        for d in range(1, ndev):
            acc = acc + buf[d]
        o_ref[...] = acc

    vm = pl.BlockSpec(memory_space=pltpu.VMEM)
    return _pcall(
        body, name="allsum_small", in_specs=[vm], out_specs=vm,
        out_shape=jax.ShapeDtypeStruct((rows, cols), F32),
        scratch_shapes=[pltpu.VMEM((ndev, rows, cols), F32), pltpu.SemaphoreType.DMA((ndev,)),
                        pltpu.SemaphoreType.DMA((ndev,))],
    )(pack)


def _adamw(parts, w, m, v, layer, prev, name):
    _, rows, cols = w.shape
    tr = ROW_TILE if rows % ROW_TILE == 0 else rows
    counts = [p.shape[0] for p in parts]
    n_parts = len(parts)
    n_prev = 0 if prev is None else 4

    def body(*refs):
        part_refs = refs[:n_parts]
        w_ref, m_ref, v_ref = refs[n_parts:n_parts + 3]
        g_ref, d_ref, nm_ref, nv_ref = refs[n_parts + 3 + n_prev:]
        g = None
        for p_ref, cnt in zip(part_refs, counts):
            s = p_ref[0].astype(F32)
            for k in range(1, cnt):
                s = s + p_ref[k].astype(F32)
            g = s if g is None else g + s
        m2 = ADAM_B1 * m_ref[0] + (1.0 - ADAM_B1) * g
        v2 = ADAM_B2 * v_ref[0] + (1.0 - ADAM_B2) * (g * g)
        m_hat = m2 / (1.0 - ADAM_B1 ** ADAM_STEP)
        v_hat = v2 / (1.0 - ADAM_B2 ** ADAM_STEP)
        g_ref[0] = g
        d_ref[0] = -ADAM_LR * (m_hat / (jnp.sqrt(v_hat) + ADAM_EPS) + ADAM_WD * w_ref[0])
        nm_ref[0] = m2
        nv_ref[0] = v2

    blk = pl.BlockSpec((1, tr, cols), lambda i: (layer, i, 0))
    shp = jax.ShapeDtypeStruct(w.shape, F32)
    return _pcall(
        body, name=name, grid=(rows // tr,),
        in_specs=[pl.BlockSpec((cnt, tr, cols), lambda i: (0, i, 0)) for cnt in counts] + [blk] * 3 + [ANY] * n_prev,
        out_specs=[blk] * 4, out_shape=[shp] * 4,
        input_output_aliases={n_parts + 3 + k: k for k in range(n_prev)},
        compiler_params=_params("parallel"),
    )(*parts, w, m, v, *(prev or ()))


def kernel(x, meta_tokens, pre_norm_g, post_norm_g, w_in, conv_w, conv_b, conv_ln_g, conv_ln_b, w_pw2, b_pw2, w_out, loss_target, m_meta_tokens, m_pre_norm_g, m_post_norm_g, m_w_in, m_conv_w, m_conv_b, m_conv_ln_g, m_conv_ln_b, m_w_pw2, m_b_pw2, m_w_out, v_meta_tokens, v_pre_norm_g, v_post_norm_g, v_w_in, v_conv_w, v_conv_b, v_conv_ln_g, v_conv_ln_b, v_w_pw2, v_b_pw2, v_w_out):
    seq, d = x.shape[1], x.shape[2]
    depth = w_in.shape[0]
    length = N_META + seq
    lp = -(-length // ATT_BLOCK) * ATT_BLOCK
    tap_pad = ((0, 0), (0, CONV_PAD - CONV_WIDTH), (0, 0))

    shards = (w_in.astype(BF16), w_pw2.astype(BF16), w_out.astype(BF16), jnp.pad(conv_w, tap_pad))
    dims = [(1, w_in.shape[2]), (0, w_pw2.shape[1]), (0, w_out.shape[1]), (1, conv_w.shape[2])]
    layer_shards = lambda l: [s[l] for s in shards]

    first = _run_exchange(_gather_plan(layer_shards(0) + [meta_tokens], dims + [(1, meta_tokens.shape[1])]),
                          "gather_weights")
    weights0, meta_f = tuple(first[:4]), first[4]

    h0 = jnp.concatenate([meta_f, x[0], jnp.zeros((lp - length, d), F32)], axis=0)
    target_p = jnp.pad(loss_target[0], ((N_META, lp - length), (0, 0)))
    vecs = (pre_norm_g, post_norm_g, conv_b, conv_ln_g, conv_ln_b, b_pw2)
    loss, dh0, vec_grads, pieces = _local_step(
        h0, target_p, seq, vecs, depth, weights0=weights0,
        gather_next=lambda l: _gather_plan(layer_shards(l + 1), dims),
        reduce_layer=lambda grads: _reduce_plan(list(grads), dims))

    def update(k, w, m, v, name):
        outs = None
        for l in reversed(range(depth)):
            outs = _adamw([pieces[l][k], pieces[l][4 + k]], w, m, v, l, outs, name)
        return outs

    up_w_in = update(0, w_in, m_w_in, v_w_in, "adamw_w_in")
    up_w_pw2 = update(1, w_pw2, m_w_pw2, v_w_pw2, "adamw_w_pw2")
    up_w_out = update(2, w_out, m_w_out, v_w_out, "adamw_w_out")
    up_conv_w = [o[:, :CONV_WIDTH] for o in update(3, jnp.pad(conv_w, tap_pad), jnp.pad(m_conv_w, tap_pad),
                                                   jnp.pad(v_conv_w, tap_pad, constant_values=1.0), "adamw_conv_w")]

    two = lambda a: a.reshape(-1, d)
    vec_rows = [two(g) for g in vec_grads]
    n_vec = sum(a.shape[0] for a in vec_rows)
    pack = jnp.concatenate(vec_rows + [dh0[:N_META], jnp.full((8, d), loss, F32)], axis=0)
    pack = jnp.pad(pack, ((0, -pack.shape[0] % 8), (0, 0)))
    tot = _allsum_small(pack)
    loss_all = tot[n_vec + N_META, 0]

    cat = lambda arrs: jnp.concatenate([two(t) for t in arrs], axis=0)[None]
    small_m = (m_pre_norm_g, m_post_norm_g, m_conv_b, m_conv_ln_g, m_conv_ln_b, m_b_pw2)
    small_v = (v_pre_norm_g, v_post_norm_g, v_conv_b, v_conv_ln_g, v_conv_ln_b, v_b_pw2)
    up_small = _adamw([tot[None, :n_vec]], cat(vecs), cat(small_m), cat(small_v), 0, None, "adamw_vectors")

    def unpack(o):
        res, r0 = [], 0
        for t in vecs:
            nrow = t.size // d
            res.append(o[0, r0:r0 + nrow].reshape(t.shape))
            r0 += nrow
        return res

    up_small = [unpack(o) for o in up_small]
    chip = 2 * lax.axis_index("x") + lax.axis_index("y")
    mcols = meta_tokens.shape[1]
    g_meta = lax.dynamic_slice_in_dim(tot[n_vec:n_vec + N_META], chip * mcols, mcols, axis=1)
    up_meta = [o[0] for o in _adamw([g_meta[None]], meta_tokens[None], m_meta_tokens[None], v_meta_tokens[None],
                                    0, None, "adamw_meta")]

    grad_x = dh0[N_META:length][None]
    outs = [loss_all, grad_x]
    for j in range(4):
        pre, post, cb, lg, lb, bp = up_small[j]
        outs += [up_meta[j], pre, post, up_w_in[j], up_conv_w[j], cb, lg, lb, up_w_pw2[j], bp, up_w_out[j]]
    return tuple(outs)
```

```python
from typing import Callable, NamedTuple

import jax
import jax.numpy as jnp
from jax import lax
from jax.experimental import pallas as pl
from jax.experimental.pallas import tpu as pltpu

F32 = jnp.float32
BF16 = jnp.bfloat16

N_META = 16
D_CONV = 512
D_SB = 512
HEAD_DIM = 64
CONV_WIDTH = 31
CONV_PAD = 32
CONV_ROWS = 128
RMS_EPS = 1e-6
LN_EPS = 1e-5
Q_SCALE = HEAD_DIM ** -0.5

ADAM_LR = 0.001
ADAM_B1 = 0.9
ADAM_B2 = 0.999
ADAM_EPS = 1e-08
ADAM_WD = 0.01
ADAM_STEP = 10

LANES = 128
ROW_TILE = 256
ATT_BLOCK = 256
ATT_PAIRS = 2
VMEM_LIMIT = 56 * 1024 * 1024
EXP_ZERO = -104.0
COUNT_LANE = LANES - 1

MESH = pl.DeviceIdType.MESH


def _pcall(body, **kw):
    return pl.pallas_call(body, **kw)


def _params(*sem):
    return pltpu.CompilerParams(dimension_semantics=sem, vmem_limit_bytes=VMEM_LIMIT)


def _sigmoid(x):
    return 1.0 / (1.0 + jnp.exp(-x))


def _silu_fwd_bwd(x):
    s = _sigmoid(x)
    return x * s, s * (1.0 + x * (1.0 - s))


def _nt(a, b):
    return lax.dot_general(a, b, (((1,), (1,)), ((), ())), preferred_element_type=F32)


def _tn(a, b):
    return lax.dot_general(a, b, (((0,), (0,)), ((), ())), preferred_element_type=F32)


def _nn(a, b):
    return jnp.dot(a, b, preferred_element_type=F32)


def _inproj(h, g, w):
    lp, d = h.shape
    n = w.shape[1]

    def body(h_ref, g_ref, w_ref, ew_ref, qkv_ref):
        x = h_ref[...]
        rstd = lax.rsqrt(jnp.mean(x * x, axis=-1, keepdims=True) + RMS_EPS)
        u = ((x * rstd) * g_ref[...]).astype(BF16)
        p = _nn(u, w_ref[...])
        ew_ref[:, 0:1536] = p[:, 0:1536]
        ew_ref[:, 1536:2048] = p[:, 3072:3584]
        qkv_ref[:, 0:512] = (p[:, 1536:2048] * Q_SCALE).astype(BF16)
        qkv_ref[:, 512:1536] = p[:, 2048:3072].astype(BF16)

    return _pcall(
        body, name="inproj_fwd", grid=(lp // ROW_TILE,),
        in_specs=[pl.BlockSpec((ROW_TILE, d), lambda i: (i, 0)),
                  pl.BlockSpec((1, d), lambda i: (0, 0)),
                  pl.BlockSpec((d, n), lambda i: (0, 0))],
        out_specs=[pl.BlockSpec((ROW_TILE, 2048), lambda i: (i, 0)),
                   pl.BlockSpec((ROW_TILE, 1536), lambda i: (i, 0))],
        out_shape=[jax.ShapeDtypeStruct((lp, 2048), F32), jax.ShapeDtypeStruct((lp, 1536), BF16)],
        compiler_params=_params("parallel"),
    )(h, g, w)


def _layer_norm_stats(c1):
    mu = jnp.mean(c1, axis=-1, keepdims=True)
    xc = c1 - mu
    var = jnp.mean(xc * xc, axis=-1, keepdims=True)
    rstd = lax.rsqrt(var + LN_EPS)
    return xc * rstd, rstd


def _shifted_copies(window, cols, shifted, tm):
    shifted[0] = window[:, cols]
    rows = tm + CONV_PAD - 8
    for b in range(1, 8):
        shifted[b, 0:rows, :] = window[pl.ds(b, rows), cols]


def _shifted_rows(shifted, shift, tm):
    b = shift % 8
    return shifted[b, pl.ds(pl.multiple_of(shift - b, 8), tm), :]


def _weighted(cw8_ref, j, cols, rows):
    w8 = cw8_ref[pl.ds(pl.multiple_of(j * 8, 8), 8), cols]
    r = rows.shape[0]
    return (rows.reshape(r // 8, 8, LANES) * w8[None]).reshape(r, LANES)


def _conv_fwd(ew, cw, cb, lng, lnb, wpw2, bpw2):
    lp = ew.shape[0]
    tm = ROW_TILE

    def body(ew_ref, cw_ref, cb_ref, lng_ref, lnb_ref, w_ref, b_ref, c1_ref, c4_ref, c5_ref, xbuf, shifted):
        @pl.when(pl.program_id(0) == 0)
        def _():
            xbuf[0:CONV_PAD, :] = jnp.zeros((CONV_PAD, D_CONV), F32)

        ga = ew_ref[:, 0:512]
        gb = ew_ref[:, 512:1024]
        cg = ew_ref[:, 1024:1536]
        xbuf[CONV_PAD:CONV_PAD + tm, :] = ga * _sigmoid(gb)
        for blk in range(D_CONV // LANES):
            cs = slice(blk * LANES, (blk + 1) * LANES)
            _shifted_copies(xbuf, cs, shifted, tm)
            for r0 in range(0, tm, CONV_ROWS):
                acc = jnp.zeros((CONV_ROWS, LANES), F32) + cb_ref[:, cs]
                for j in range(CONV_WIDTH):
                    acc = acc + _weighted(cw_ref, j, cs, _shifted_rows(
                        shifted, r0 + CONV_PAD - (CONV_WIDTH - 1) + j, CONV_ROWS))
                c1_ref[r0:r0 + CONV_ROWS, cs] = acc
        xbuf[0:CONV_PAD, :] = xbuf[tm:tm + CONV_PAD, :]
        xhat, _ = _layer_norm_stats(c1_ref[...])
        c2 = xhat * lng_ref[...] + lnb_ref[...]
        c3 = c2 * _sigmoid(c2)
        c4 = _nn(c3.astype(BF16), w_ref[...]) + b_ref[...]
        c4_ref[...] = c4
        c5_ref[...] = (c4 * (cg * _sigmoid(cg))).astype(BF16)

    vec = pl.BlockSpec((1, D_CONV), lambda i: (0, 0))
    row = pl.BlockSpec((tm, D_CONV), lambda i: (i, 0))
    return _pcall(
        body, name="conv_fwd", grid=(lp // tm,),
        in_specs=[pl.BlockSpec((tm, 1536), lambda i: (i, 0)),
                  pl.BlockSpec((8 * CONV_PAD, D_CONV), lambda i: (0, 0)),
                  vec, vec, vec,
                  pl.BlockSpec((D_CONV, D_CONV), lambda i: (0, 0)),
                  vec],
        out_specs=[row, row, row],
        out_shape=[jax.ShapeDtypeStruct((lp, D_CONV), F32), jax.ShapeDtypeStruct((lp, D_CONV), F32),
                   jax.ShapeDtypeStruct((lp, D_CONV), BF16)],
        scratch_shapes=[pltpu.VMEM((tm + CONV_PAD, D_CONV), F32), pltpu.VMEM((8, tm + CONV_PAD, LANES), F32)],
        compiler_params=_params("arbitrary"),
    )(ew, cw, cb, lng, lnb, wpw2, bpw2)


def _split_matmul(x, m01):
    hi = x.astype(BF16)
    lo = (x - hi.astype(F32)).astype(BF16)
    t = _nn(jnp.concatenate([hi, lo], axis=0), m01)
    m = x.shape[0]
    return t[:m] + t[m:]


def _sb_block(q2, kj, z_mask, tri):
    z = _nt(q2, kj)
    ls = -(jnp.maximum(z, 0.0) + jnp.log(1.0 + jnp.exp(-jnp.abs(z))))
    if z_mask is not None:
        ls = jnp.where(z_mask, ls, 0.0)
    return z, ls, _split_matmul(ls, tri)


def _attn_masks():
    lane = lax.broadcasted_iota(jnp.int32, (1, LANES), 1)
    row = lax.broadcasted_iota(jnp.int32, (2 * ATT_BLOCK, ATT_BLOCK), 0)
    col = lax.broadcasted_iota(jnp.int32, (2 * ATT_BLOCK, ATT_BLOCK), 1)
    return lane < HEAD_DIM, col < (row & (ATT_BLOCK - 1))


def _stack_heads(x, first_head):
    zero = jnp.zeros_like(x)
    return jnp.concatenate([jnp.where(first_head, x, zero), jnp.where(first_head, zero, x)], axis=0)


def _unstack_heads(x2, first_head):
    rows = x2.shape[0] // 2
    return jnp.where(first_head, x2[:rows], x2[rows:])


def _hosted(plan, n_out, n_scratch):
    n_in = 0 if plan is None else len(plan.inputs)
    n_x = 0 if plan is None else len(plan.out_shapes)

    def split(rest):
        a, b, c = n_in + n_out, n_in + n_out + n_x, n_in + n_out + n_x + n_scratch
        return rest[:n_in], rest[n_in:a], rest[a:b], rest[b:c], rest[c:]

    if plan is None:
        return split, [], [], [], [], []
    return split, list(plan.inputs), [ANY] * n_in, [ANY] * n_x, list(plan.out_shapes), list(plan.scratch)


def _attn_fwd(qkv, tri, plan=None):
    lp = qkv.shape[0]
    bq = ATT_BLOCK
    ngrp = ATT_PAIRS
    nstep = D_SB // (LANES * ngrp)
    nq = lp // bq
    assert nq <= COUNT_LANE
    split, x_args, x_in_specs, x_out_specs, x_out_shapes, x_scratch = _hosted(plan, 2, 3)

    def body(q_ref, k_ref, v_ref, tri_ref, *rest):
        x_in, (o_ref, carry_ref), x_out, (c_s, acc_s, cm_s), x_sems = split(rest)
        i = pl.program_id(1)
        if plan is not None:
            @pl.when(jnp.logical_and(pl.program_id(0) == 0, i == 0))
            def _():
                plan.start(x_in, x_out, x_sems)

        first_head, vis = _attn_masks()
        lane = lax.broadcasted_iota(jnp.int32, (1, LANES), 1)
        cols = [slice(g * LANES, (g + 1) * LANES) for g in range(ngrp)]
        q2s = [_stack_heads(q_ref[:, cs], first_head) for cs in cols]
        tri_m = tri_ref[...]

        c_s[...] = jnp.zeros_like(c_s)
        acc_s[...] = jnp.zeros_like(acc_s)
        cm_s[...] = jnp.zeros_like(cm_s)

        def blocks(js, masks):
            offs = [pl.multiple_of(j * bq, bq) for j in js]
            work = [(g, b) for b in range(len(js)) for g in range(ngrp)]
            zs = {(g, b): _nt(q2s[g], k_ref[pl.ds(offs[b], bq), cols[g]]) for g, b in work}
            lss = {}
            for g, b in work:
                z = zs[g, b]
                ls = -(jnp.maximum(z, 0.0) + jnp.log(1.0 + jnp.exp(-jnp.abs(z))))
                lss[g, b] = ls if masks[b] is None else jnp.where(masks[b], ls, 0.0)
            tails = {gb: _split_matmul(lss[gb], tri_m) for gb in work}
            probs = {}
            carry = [c_s[g] for g in range(ngrp)]
            saved = [cm_s[g] for g in range(ngrp)]
            for g, b in work:
                a = jnp.exp(zs[g, b] + lss[g, b] + tails[g, b] + carry[g])
                probs[g, b] = (a if masks[b] is None else jnp.where(masks[b], a, 0.0)).astype(BF16)
                saved[g] = jnp.where(lane == js[b], carry[g], saved[g])
                carry[g] = carry[g] + tails[g, b][:, 0:1] + lss[g, b][:, 0:1]
            top = None
            for g in range(ngrp):
                c_s[g] = carry[g]
                cm_s[g] = saved[g]
                acc = acc_s[g]
                for b in range(len(js)):
                    acc = acc + _nn(probs[g, b], v_ref[pl.ds(offs[b], bq), cols[g]])
                acc_s[g] = acc
                top = carry[g] if top is None else jnp.maximum(top, carry[g])
            return jnp.max(top) > EXP_ZERO

        alive = lax.cond(i > 0, lambda: blocks([i, i - 1], [vis, None]), lambda: blocks([i], [vis]))
        rest = jnp.maximum(i - 1, 0)

        def pair(carry):
            t, _ = carry
            j = i - 2 - 2 * t
            return t + 1, blocks([j, j - 1], [None, None])

        trips, alive = lax.while_loop(lambda ca: jnp.logical_and(ca[0] < rest // 2, ca[1]), pair, (0, alive))
        last = jnp.logical_and(jnp.logical_and(rest % 2 == 1, trips == rest // 2), alive)

        @pl.when(last)
        def _():
            blocks([0], [None])

        n_done = (jnp.minimum(i + 1, 2) + 2 * trips + last.astype(jnp.int32)).astype(F32)
        for g in range(ngrp):
            cmat = jnp.where(lane == COUNT_LANE, n_done, cm_s[g])
            carry_ref[:, 2 * g * LANES:(2 * g + 1) * LANES] = cmat[:bq]
            carry_ref[:, (2 * g + 1) * LANES:(2 * g + 2) * LANES] = cmat[bq:]
            o_ref[:, cols[g]] = _unstack_heads(acc_s[g], first_head)
        if plan is not None:
            @pl.when(jnp.logical_and(pl.program_id(0) == nstep - 1, i == nq - 1))
            def _():
                plan.finish(x_in, x_out, x_sems)

    width = ngrp * LANES
    outs = _pcall(
        body, name="attn_fwd" if plan is None else "attn_fwd_gather", grid=(nstep, nq),
        in_specs=[pl.BlockSpec((bq, width), lambda p, i: (i, p)),
                  pl.BlockSpec((lp, width), lambda p, i: (0, nstep + p)),
                  pl.BlockSpec((lp, width), lambda p, i: (0, 2 * nstep + p)),
                  pl.BlockSpec((bq, bq), lambda p, i: (0, 0))] + x_in_specs,
        out_specs=[pl.BlockSpec((bq, width), lambda p, i: (i, p)),
                   pl.BlockSpec((bq, 2 * width), lambda p, i: (i, p))] + x_out_specs,
        out_shape=[jax.ShapeDtypeStruct((lp, D_SB), F32), jax.ShapeDtypeStruct((lp, 2 * D_SB), F32)] + x_out_shapes,
        scratch_shapes=[pltpu.VMEM((ngrp, 2 * bq, 1), F32), pltpu.VMEM((ngrp, 2 * bq, LANES), F32),
                        pltpu.VMEM((ngrp, 2 * bq, LANES), F32)] + x_scratch,
        compiler_params=_params("arbitrary", "arbitrary"),
    )(qkv, qkv, qkv, tri, *x_args)
    return outs[0], outs[1], outs[2:]


def _outproj(c5, att, ew, h, w, g):
    lp, d = h.shape
    tm = ROW_TILE

    def body(c5_ref, att_ref, sg_ref, h_ref, w_ref, g_ref, hn_ref, cat_ref, mix_ref):
        sg = sg_ref[...]
        s = att_ref[...] * (sg * _sigmoid(sg))
        cat_ref[:, 0:D_CONV] = c5_ref[...]
        cat_ref[:, D_CONV:] = s.astype(BF16)
        mixed = _nn(cat_ref[...], w_ref[...])
        mix_ref[...] = mixed
        rstd = lax.rsqrt(jnp.mean(mixed * mixed, axis=-1, keepdims=True) + RMS_EPS)
        hn_ref[...] = h_ref[...] + (mixed * rstd) * g_ref[...]

    half = pl.BlockSpec((tm, 512), lambda i: (i, 0))
    full = pl.BlockSpec((tm, d), lambda i: (i, 0))
    return _pcall(
        body, name="outproj_fwd", grid=(lp // tm,),
        in_specs=[half, half, pl.BlockSpec((tm, 512), lambda i: (i, 3)), full,
                  pl.BlockSpec((d, d), lambda i: (0, 0)), pl.BlockSpec((1, d), lambda i: (0, 0))],
        out_specs=[full, full, full],
        out_shape=[jax.ShapeDtypeStruct((lp, d), F32), jax.ShapeDtypeStruct((lp, d), BF16),
                   jax.ShapeDtypeStruct((lp, d), F32)],
        compiler_params=_params("parallel"),
    )(c5, att, ew, h, w, g)


def _loss_head(h, target, seq):
    lp, d = h.shape
    tm = ROW_TILE

    def body(h_ref, t_ref, dh_ref, loss_ref):
        i = pl.program_id(0)

        @pl.when(i == 0)
        def _():
            loss_ref[...] = jnp.zeros_like(loss_ref)

        row = i * tm + lax.broadcasted_iota(jnp.int32, (tm, 1), 0)
        real = jnp.logical_and(row >= N_META, row < N_META + seq)
        diff = jnp.where(real, h_ref[...] - t_ref[...], 0.0)
        dh_ref[...] = diff * (1.0 / d)
        loss_ref[...] += 0.5 * jnp.sum(jnp.sum(diff * diff, axis=-1, keepdims=True) * (1.0 / d))

    full = pl.BlockSpec((tm, d), lambda i: (i, 0))
    return _pcall(
        body, name="loss_head", grid=(lp // tm,),
        in_specs=[full, full],
        out_specs=[full, pl.BlockSpec((8, LANES), lambda i: (0, 0))],
        out_shape=[jax.ShapeDtypeStruct((lp, d), F32), jax.ShapeDtypeStruct((8, LANES), F32)],
        compiler_params=_params("arbitrary"),
    )(h, target)


def _outproj_bwd(dh, mixed, g, w, att, ew, c4):
    lp, d = dh.shape
    tm = ROW_TILE

    def body(dh_ref, mix_ref, g_ref, w_ref, att_ref, cg_ref, sg_ref, c4_ref,
             dmix_ref, datt_ref, dsg_ref, dc4_ref, dcg_ref, dg_ref, db_ref):
        @pl.when(pl.program_id(0) == 0)
        def _():
            dg_ref[...] = jnp.zeros_like(dg_ref)
            db_ref[...] = jnp.zeros_like(db_ref)

        mixed = mix_ref[...]
        dhv = dh_ref[...]
        rstd = lax.rsqrt(jnp.mean(mixed * mixed, axis=-1, keepdims=True) + RMS_EPS)
        n = mixed * rstd
        dg_ref[...] += jnp.sum(dhv * n, axis=0, keepdims=True)
        dn = dhv * g_ref[...]
        dmix = (rstd * (dn - n * jnp.mean(dn * n, axis=-1, keepdims=True))).astype(BF16)
        dmix_ref[...] = dmix
        dcat = _nt(dmix, w_ref[...])
        dc5 = dcat[:, 0:D_CONV]
        ds = dcat[:, D_CONV:]
        silu_sg, dsilu_sg = _silu_fwd_bwd(sg_ref[...])
        datt_ref[...] = (ds * silu_sg).astype(BF16)
        dsg_ref[...] = (ds * att_ref[...] * dsilu_sg).astype(BF16)
        silu_cg, dsilu_cg = _silu_fwd_bwd(cg_ref[...])
        dc4 = dc5 * silu_cg
        db_ref[...] += jnp.sum(dc4, axis=0, keepdims=True)
        dc4_ref[...] = dc4.astype(BF16)
        dcg_ref[...] = (dc5 * c4_ref[...] * dsilu_cg).astype(BF16)

    half = pl.BlockSpec((tm, 512), lambda i: (i, 0))
    full = pl.BlockSpec((tm, d), lambda i: (i, 0))
    hb = jax.ShapeDtypeStruct((lp, 512), BF16)
    return _pcall(
        body, name="outproj_bwd", grid=(lp // tm,),
        in_specs=[full, full, pl.BlockSpec((1, d), lambda i: (0, 0)), pl.BlockSpec((d, d), lambda i: (0, 0)),
                  half, pl.BlockSpec((tm, 512), lambda i: (i, 2)), pl.BlockSpec((tm, 512), lambda i: (i, 3)), half],
        out_specs=[full, half, half, half, half,
                   pl.BlockSpec((1, d), lambda i: (0, 0)), pl.BlockSpec((1, 512), lambda i: (0, 0))],
        out_shape=[jax.ShapeDtypeStruct((lp, d), BF16), hb, hb, hb, hb,
                   jax.ShapeDtypeStruct((1, d), F32), jax.ShapeDtypeStruct((1, 512), F32)],
        compiler_params=_params("arbitrary"),
    )(dh, mixed, g, w, att, ew, ew, c4)


def _attn_bwd(qkv, carries, datt, tri, upper, plan=None):
    lp = qkv.shape[0]
    bq = ATT_BLOCK
    ngrp = ATT_PAIRS
    nstep = D_SB // (LANES * ngrp)
    nq = lp // bq
    split, x_args, x_in_specs, x_out_specs, x_out_shapes, x_scratch = _hosted(plan, 3, 2)

    def body(q_ref, k_ref, v_ref, carry_ref, do_ref, tri_ref, upper_ref, *rest):
        x_in, (dq_ref, dk_ref, dv_ref), x_out, (run_s, dq_s), x_sems = split(rest)
        i = pl.program_id(1)
        if plan is not None:
            @pl.when(jnp.logical_and(pl.program_id(0) == 0, i == 0))
            def _():
                plan.start(x_in, x_out, x_sems)

        @pl.when(i == 0)
        def _():
            dk_ref[...] = jnp.zeros_like(dk_ref)
            dv_ref[...] = jnp.zeros_like(dv_ref)

        first_head, vis = _attn_masks()
        lane = lax.broadcasted_iota(jnp.int32, (1, LANES), 1)
        cols = [slice(g * LANES, (g + 1) * LANES) for g in range(ngrp)]
        q2s = [_stack_heads(q_ref[:, cs], first_head) for cs in cols]
        do2s = [_stack_heads(do_ref[:, cs], first_head) for cs in cols]
        cmats = [jnp.concatenate([carry_ref[:, 2 * g * LANES:(2 * g + 1) * LANES],
                                  carry_ref[:, (2 * g + 1) * LANES:(2 * g + 2) * LANES]], axis=0)
                 for g in range(ngrp)]
        tri_m = tri_ref[...]
        upper_m = upper_ref[...]

        def blocks(js, masks):
            offs = [pl.multiple_of(j * bq, bq) for j in js]
            work = [(g, b) for b in range(len(js)) for g in range(ngrp)]
            zs = {(g, b): _nt(q2s[g], k_ref[pl.ds(offs[b], bq), cols[g]]) for g, b in work}
            lss = {}
            for g, b in work:
                z = zs[g, b]
                ls = -(jnp.maximum(z, 0.0) + jnp.log(1.0 + jnp.exp(-jnp.abs(z))))
                lss[g, b] = ls if masks[b] is None else jnp.where(masks[b], ls, 0.0)
            tails = {gb: _split_matmul(lss[gb], tri_m) for gb in work}
            das = {(g, b): _nt(do2s[g], v_ref[pl.ds(offs[b], bq), cols[g]]) for g, b in work}
            probs, des = {}, {}
            for g, b in work:
                c = jnp.sum(jnp.where(lane == js[b], cmats[g], 0.0), axis=-1, keepdims=True)
                a = jnp.exp(zs[g, b] + lss[g, b] + tails[g, b] + c)
                a = a if masks[b] is None else jnp.where(masks[b], a, 0.0)
                probs[g, b] = a.astype(BF16)
                des[g, b] = das[g, b] * a
            prefixes = {gb: _split_matmul(des[gb], upper_m) for gb in work}
            runs = [run_s[g] for g in range(ngrp)]
            dzs = {}
            for g, b in work:
                beta = jnp.exp(zs[g, b] + lss[g, b])
                dz = des[g, b] - beta * (des[g, b] + runs[g] + prefixes[g, b])
                dzs[g, b] = (dz if masks[b] is None else jnp.where(masks[b], dz, 0.0)).astype(BF16)
                runs[g] = runs[g] + prefixes[g, b][:, bq - 1:bq] + des[g, b][:, bq - 1:bq]
            for g in range(ngrp):
                run_s[g] = runs[g]
                dq = dq_s[g]
                for b in range(len(js)):
                    rows = pl.ds(offs[b], bq)
                    dq = dq + _nn(dzs[g, b], k_ref[rows, cols[g]])
                    dk_ref[rows, cols[g]] += _tn(dzs[g, b], q2s[g])
                    dv_ref[rows, cols[g]] += _tn(probs[g, b], do2s[g])
                dq_s[g] = dq

        n_done = jnp.max(carry_ref[:, COUNT_LANE:COUNT_LANE + 1]).astype(jnp.int32)
        n_done = jnp.clip(n_done, 1, i + 1)
        before = jnp.maximum(n_done - 2, 0)
        j0 = i - n_done + 1
        odd = before % 2
        run_s[...] = jnp.zeros_like(run_s)
        dq_s[...] = jnp.zeros_like(dq_s)

        @pl.when(odd == 1)
        def _():
            blocks([j0], [None])

        @pl.loop(0, before // 2)
        def _(t):
            blocks([j0 + odd + 2 * t, j0 + odd + 2 * t + 1], [None, None])

        @pl.when(n_done > 1)
        def _():
            blocks([i - 1, i], [None, vis])

        @pl.when(n_done <= 1)
        def _():
            blocks([i], [vis])

        for g in range(ngrp):
            dq_ref[:, cols[g]] = (_unstack_heads(dq_s[g], first_head) * Q_SCALE).astype(BF16)
        if plan is not None:
            @pl.when(jnp.logical_and(pl.program_id(0) == nstep - 1, i == nq - 1))
            def _():
                plan.finish(x_in, x_out, x_sems)

    width = ngrp * LANES
    qb = pl.BlockSpec((bq, width), lambda p, i: (i, p))
    colb = pl.BlockSpec((lp, width), lambda p, i: (0, p))
    sq = pl.BlockSpec((bq, bq), lambda p, i: (0, 0))
    outs = _pcall(
        body, name="attn_bwd" if plan is None else "attn_bwd_reduce", grid=(nstep, nq),
        in_specs=[qb,
                  pl.BlockSpec((lp, width), lambda p, i: (0, nstep + p)),
                  pl.BlockSpec((lp, width), lambda p, i: (0, 2 * nstep + p)),
                  pl.BlockSpec((bq, 2 * width), lambda p, i: (i, p)), qb, sq, sq] + x_in_specs,
        out_specs=[qb, colb, colb] + x_out_specs,
        out_shape=[jax.ShapeDtypeStruct((lp, D_SB), BF16), jax.ShapeDtypeStruct((lp, D_SB), F32),
                   jax.ShapeDtypeStruct((lp, D_SB), F32)] + x_out_shapes,
        scratch_shapes=[pltpu.VMEM((ngrp, 2 * bq, 1), F32), pltpu.VMEM((ngrp, 2 * bq, LANES), F32)] + x_scratch,
        compiler_params=_params("arbitrary", "arbitrary"),
    )(qkv, qkv, qkv, carries, datt, tri, upper, *x_args)
    return outs[0], outs[1], outs[2], outs[3:]


def _conv_bwd(dc4, c1, ew, cw, lng, lnb, wpw2):
    lp = ew.shape[0]
    tm = ROW_TILE
    nt = lp // tm
    halo_per_tile = tm // CONV_PAD

    def body(dc4_ref, c1_ref, ew_ref, halo_ref, cw_ref, lng_ref, lnb_ref, w_ref,
             dga_ref, dgb_ref, c3_ref, dcw_ref, dcb_ref, dlng_ref, dlnb_ref, xbuf, dbuf, shifted, wacc):
        step = pl.program_id(0)

        @pl.when(step == 0)
        def _():
            wacc[...] = jnp.zeros_like(wacc)
            dcb_ref[...] = jnp.zeros_like(dcb_ref)
            dlng_ref[...] = jnp.zeros_like(dlng_ref)
            dlnb_ref[...] = jnp.zeros_like(dlnb_ref)
            dbuf[tm:tm + CONV_PAD, :] = jnp.zeros((CONV_PAD, D_CONV), F32)

        dc3 = _nt(dc4_ref[...], w_ref[...])
        xhat, rstd = _layer_norm_stats(c1_ref[...])
        c2 = xhat * lng_ref[...] + lnb_ref[...]
        c3, dsilu = _silu_fwd_bwd(c2)
        c3_ref[...] = c3.astype(BF16)
        dc2 = dc3 * dsilu
        dlng_ref[...] += jnp.sum(dc2 * xhat, axis=0, keepdims=True)
        dlnb_ref[...] += jnp.sum(dc2, axis=0, keepdims=True)
        dxhat = dc2 * lng_ref[...]
        dc1 = rstd * (dxhat - jnp.mean(dxhat, axis=-1, keepdims=True)
                      - xhat * jnp.mean(dxhat * xhat, axis=-1, keepdims=True))
        dcb_ref[...] += jnp.sum(dc1, axis=0, keepdims=True)
        dbuf[0:tm, :] = dc1

        ga = ew_ref[:, 0:512]
        sgb = _sigmoid(ew_ref[:, 512:1024])
        xbuf[CONV_PAD:CONV_PAD + tm, :] = ga * sgb
        first_tile = step == nt - 1
        halo = halo_ref[:, 0:512] * _sigmoid(halo_ref[:, 512:1024])
        xbuf[0:CONV_PAD, :] = jnp.where(first_tile, 0.0, halo)

        for cb in range(D_CONV // LANES):
            cs = slice(cb * LANES, (cb + 1) * LANES)
            _shifted_copies(dbuf, cs, shifted, tm)
            for r0 in range(0, tm, CONV_ROWS):
                rs = slice(r0, r0 + CONV_ROWS)
                dc0 = jnp.zeros((CONV_ROWS, LANES), F32)
                for j in range(CONV_WIDTH):
                    dc0 = dc0 + _weighted(cw_ref, j, cs, _shifted_rows(
                        shifted, r0 + CONV_WIDTH - 1 - j, CONV_ROWS))
                dga_ref[rs, cs] = (dc0 * sgb[rs, cs]).astype(BF16)
                dgb_ref[rs, cs] = (dc0 * ga[rs, cs] * sgb[rs, cs] * (1.0 - sgb[rs, cs])).astype(BF16)
            _shifted_copies(xbuf, cs, shifted, tm)
            for r0 in range(0, tm, CONV_ROWS):
                d1 = dbuf[r0:r0 + CONV_ROWS, cs]

                for j in range(CONV_WIDTH):
                    prod = d1 * _shifted_rows(shifted, r0 + CONV_PAD - (CONV_WIDTH - 1) + j, CONV_ROWS)
                    wacc[j * 8:(j + 1) * 8, cs] += jnp.sum(prod.reshape(CONV_ROWS // 8, 8, LANES), axis=0)
        dbuf[tm:tm + CONV_PAD, :] = dbuf[0:CONV_PAD, :]

        @pl.when(step == nt - 1)
        def _():
            dcw_ref[...] = jnp.sum(wacc[...].reshape(CONV_PAD, 8, D_CONV), axis=1)

    rev = lambda i: (nt - 1 - i, 0)
    row = pl.BlockSpec((tm, D_CONV), rev)
    vec = pl.BlockSpec((1, D_CONV), lambda i: (0, 0))
    hb = jax.ShapeDtypeStruct((lp, D_CONV), BF16)
    vs = jax.ShapeDtypeStruct((1, D_CONV), F32)
    return _pcall(
        body, name="conv_bwd", grid=(nt,),
        in_specs=[row, row, pl.BlockSpec((tm, 1024), rev),
                  pl.BlockSpec((CONV_PAD, 1024), lambda i: (jnp.maximum((nt - 1 - i) * halo_per_tile - 1, 0), 0)),
                  pl.BlockSpec((8 * CONV_PAD, D_CONV), lambda i: (0, 0)), vec, vec,
                  pl.BlockSpec((D_CONV, D_CONV), lambda i: (0, 0))],
        out_specs=[row, row, row, pl.BlockSpec((CONV_PAD, D_CONV), lambda i: (0, 0)), vec, vec, vec],
        out_shape=[hb, hb, hb, jax.ShapeDtypeStruct((CONV_PAD, D_CONV), F32), vs, vs, vs],
        scratch_shapes=[pltpu.VMEM((tm + CONV_PAD, D_CONV), F32), pltpu.VMEM((tm + CONV_PAD, D_CONV), F32),
                        pltpu.VMEM((8, tm + CONV_PAD, LANES), F32), pltpu.VMEM((8 * CONV_PAD, D_CONV), F32)],
        compiler_params=_params("arbitrary"),
    )(dc4, c1, ew, ew, cw, lng, lnb, wpw2)


def _inproj_bwd(dga, dgb, dcg, dq, dk, dv, dsg, h, g, w, dh_out):
    lp, d = h.shape
    n = w.shape[1]
    tm = ROW_TILE

    def body(dga_ref, dgb_ref, dcg_ref, dq_ref, dk_ref, dv_ref, dsg_ref, h_ref, g_ref, w_ref, dho_ref,
             dh_ref, dproj_ref, u_ref, dg_ref):
        @pl.when(pl.program_id(0) == 0)
        def _():
            dg_ref[...] = jnp.zeros_like(dg_ref)

        dproj_ref[:, 0:512] = dga_ref[...]
        dproj_ref[:, 512:1024] = dgb_ref[...]
        dproj_ref[:, 1024:1536] = dcg_ref[...]
        dproj_ref[:, 1536:2048] = dq_ref[...]
        dproj_ref[:, 2048:2560] = dk_ref[...].astype(BF16)
        dproj_ref[:, 2560:3072] = dv_ref[...].astype(BF16)
        dproj_ref[:, 3072:3584] = dsg_ref[...]
        du = _nt(dproj_ref[...], w_ref[...])
        x = h_ref[...]
        rstd = lax.rsqrt(jnp.mean(x * x, axis=-1, keepdims=True) + RMS_EPS)
        nrm = x * rstd
        u_ref[...] = (nrm * g_ref[...]).astype(BF16)
        dg_ref[...] += jnp.sum(du * nrm, axis=0, keepdims=True)
        dn = du * g_ref[...]
        dh_ref[...] = dho_ref[...] + rstd * (dn - nrm * jnp.mean(dn * nrm, axis=-1, keepdims=True))

    half = pl.BlockSpec((tm, 512), lambda i: (i, 0))
    full = pl.BlockSpec((tm, d), lambda i: (i, 0))
    return _pcall(
        body, name="inproj_bwd", grid=(lp // tm,),
        in_specs=[half] * 7 + [full, pl.BlockSpec((1, d), lambda i: (0, 0)),
                               pl.BlockSpec((d, n), lambda i: (0, 0)), full],
        out_specs=[full, pl.BlockSpec((tm, n), lambda i: (i, 0)), full, pl.BlockSpec((1, d), lambda i: (0, 0))],
        out_shape=[jax.ShapeDtypeStruct((lp, d), F32), jax.ShapeDtypeStruct((lp, n), BF16),
                   jax.ShapeDtypeStruct((lp, d), BF16), jax.ShapeDtypeStruct((1, d), F32)],
        compiler_params=_params("arbitrary"),
    )(dga, dgb, dcg, dq, dk, dv, dsg, h, g, w, dh_out)


def _row_split(m, parts):
    tm = m // parts
    assert tm * parts == m and tm % 16 == 0, (m, parts)
    return tm


def _matmul_tn(x, dy, tn, name):
    m, k = x.shape
    n = dy.shape[1]
    steps = 4 if m % 64 == 0 else 1
    tm = _row_split(m, steps)

    def body(x_ref, dy_ref, o_ref, acc_ref):
        r = pl.program_id(1)

        @pl.when(r == 0)
        def _():
            acc_ref[...] = jnp.zeros_like(acc_ref)

        acc_ref[...] += _tn(x_ref[...], dy_ref[...])

        @pl.when(r == steps - 1)
        def _():
            o_ref[...] = acc_ref[...].astype(BF16)

    return _pcall(
        body, name=name, grid=(n // tn, steps),
        in_specs=[pl.BlockSpec((tm, k), lambda j, r: (r, 0)), pl.BlockSpec((tm, tn), lambda j, r: (r, j))],
        out_specs=pl.BlockSpec((k, tn), lambda j, r: (0, j)),
        out_shape=jax.ShapeDtypeStruct((k, n), BF16),
        scratch_shapes=[pltpu.VMEM((k, tn), F32)],
        compiler_params=_params("parallel", "arbitrary"),
    )(x, dy)


def _local_step(h0, target_p, seq, vecs, depth, all_weights=None, weights0=None, gather_next=None,
                reduce_layer=None):
    pre_g, post_g, conv_b, ln_g, ln_b, b_pw2 = vecs
    ar = jnp.arange(ATT_BLOCK)
    tri = (ar[:, None] > ar[None, :]).astype(BF16)
    upper = (ar[:, None] < ar[None, :]).astype(BF16)
    row = lambda a, l: a[l][None, :]

    weights = list(all_weights) if all_weights is not None else [weights0] + [None] * (depth - 1)
    saved = []
    h = h0
    for l in range(depth):
        w_in, w_pw2, w_out, conv_w = weights[l]
        ew, qkv = _inproj(h, row(pre_g, l), w_in)
        conv_w = jnp.repeat(conv_w, 8, axis=0)
        weights[l] = (w_in, w_pw2, w_out, conv_w)
        c1, c4, c5 = _conv_fwd(ew, conv_w, row(conv_b, l), row(ln_g, l), row(ln_b, l), w_pw2, row(b_pw2, l))
        plan = gather_next(l) if gather_next is not None and l + 1 < depth else None
        att, carries, gathered = _attn_fwd(qkv, tri, plan)
        if plan is not None:
            weights[l + 1] = tuple(gathered)
        hn, cat, mixed = _outproj(c5, att, ew, h, w_out, row(post_g, l))
        saved.append((h, ew, qkv, c1, c4, att, carries, cat, mixed))
        h = hn

    dh, loss = _loss_head(h, target_p, seq)

    vec_grads = [None] * depth
    mat_grads = [None] * depth
    pending = None
    for l in reversed(range(depth)):
        w_in, w_pw2, w_out, conv_w = weights[l]
        h_in, ew, qkv, c1, c4, att, carries, cat, mixed = saved[l]
        dmix, datt, dsg, dc4, dcg, dpost, dbpw2 = _outproj_bwd(dh, mixed, row(post_g, l), w_out, att, ew, c4)
        dw_out = _matmul_tn(cat, dmix, 512, "dw_out")
        dq, dk, dv, landed = _attn_bwd(qkv, carries, datt, tri, upper, pending)
        if pending is not None:
            mat_grads[l + 1] = landed
        dga, dgb, c3, dcw, dcb, dlng, dlnb = _conv_bwd(dc4, c1, ew, conv_w, row(ln_g, l), row(ln_b, l), w_pw2)
        dw_pw2 = _matmul_tn(c3, dc4, 512, "dw_pw2")
        dh, dproj, u, dpre = _inproj_bwd(dga, dgb, dcg, dq, dk, dv, dsg, h_in, row(pre_g, l), w_in, dh)
        dw_in = _matmul_tn(u, dproj, 896, "dw_in")
        vec_grads[l] = (dpre[0], dpost[0], dcb[0], dlng[0], dlnb[0], dbpw2[0])
        mats = (dw_in, dw_pw2, dw_out, dcw)
        if reduce_layer is None:
            mat_grads[l] = mats
        else:
            pending = reduce_layer(mats)
    if pending is not None:
        mat_grads[0] = _run_exchange(pending, "reduce_grads")

    vec_grads = [jnp.stack([g[k] for g in vec_grads]) for k in range(len(vecs))]
    return loss[0, 0], dh, vec_grads, mat_grads


N_CHIPS = 4
ANY = pl.BlockSpec(memory_space=pl.ANY)


def _chip_peers():
    x, y, c = lax.axis_index("x"), lax.axis_index("y"), lax.axis_index("c")
    return x, y, c, [(x, 1 - y), (1 - x, y), (1 - x, 1 - y)]


def _shard_slices(refs, dims, idx):
    out = []
    for ref, (axis, size) in zip(refs, dims):
        assert size % LANES == 0
        start = pl.multiple_of(idx * size, LANES)
        sl = [slice(None)] * len(ref.shape)
        sl[axis] = pl.ds(start, size)
        out.append(ref.at[tuple(sl)])
    return out


class _Exchange(NamedTuple):
    inputs: list
    out_shapes: list
    scratch: list
    start: Callable
    finish: Callable


def _run_exchange(plan, name):
    n_in, n_out = len(plan.inputs), len(plan.out_shapes)

    def body(*refs):
        parts = refs[:n_in], refs[n_in:n_in + n_out], refs[n_in + n_out:]
        plan.start(*parts)
        plan.finish(*parts)

    return _pcall(body, name=name, in_specs=[ANY] * n_in, out_specs=[ANY] * n_out, out_shape=plan.out_shapes,
                  scratch_shapes=plan.scratch)(*plan.inputs)


def _gather_plan(shards, dims):
    n = len(shards)
    full_shapes = []
    for s, (axis, size) in zip(shards, dims):
        shp = list(s.shape)
        shp[axis] = size * N_CHIPS
        full_shapes.append(jax.ShapeDtypeStruct(tuple(shp), s.dtype))

    def copies(srcs, outs, sems):
        send, recv, loc = sems
        x, y, c, peers = _chip_peers()
        mine = _shard_slices(outs, dims, 2 * x + y)
        local = [pltpu.make_async_copy(s, d, loc.at[a]) for a, (s, d) in enumerate(zip(srcs, mine))]
        sends = [pltpu.make_async_remote_copy(s, d, send.at[k, a], recv.at[k, a],
                                              device_id=(px, py, c), device_id_type=MESH)
                 for k, (px, py) in enumerate(peers) for a, (s, d) in enumerate(zip(srcs, mine))]
        arrivals = [pltpu.make_async_remote_copy(s, d, send.at[k, a], recv.at[k, a],
                                                 device_id=(px, py, c), device_id_type=MESH)
                    for k, (px, py) in enumerate(peers)
                    for a, (s, d) in enumerate(zip(srcs, _shard_slices(outs, dims, 2 * px + py)))]
        return local, sends, arrivals

    def start(srcs, outs, sems):
        local, sends, _ = copies(srcs, outs, sems)
        for cp in local + sends:
            cp.start()

    def finish(srcs, outs, sems):
        local, sends, arrivals = copies(srcs, outs, sems)
        for cp in arrivals:
            cp.wait_recv()
        for cp in sends:
            cp.wait_send()
        for cp in local:
            cp.wait()

    scratch = [pltpu.SemaphoreType.DMA((3, n)), pltpu.SemaphoreType.DMA((3, n)), pltpu.SemaphoreType.DMA((n,))]
    return _Exchange(list(shards), full_shapes, scratch, start, finish)


def _reduce_plan(grads, dims):
    n = len(grads)
    piece_shapes = []
    for g, (axis, size) in zip(grads, dims):
        shp = list(g.shape)
        shp[axis] = size
        piece_shapes.append(jax.ShapeDtypeStruct((N_CHIPS,) + tuple(shp), g.dtype))

    def copies(srcs, outs, sems):
        mine, theirs = outs[:n], outs[n:]
        send, recv, loc = sems
        x, y, c, peers = _chip_peers()
        sibling = (x, y, 1 - c)
        own = _shard_slices(srcs, dims, 2 * x + y)

        def remote(src, dst, slot, a, dev):
            return pltpu.make_async_remote_copy(src, dst, send.at[slot, a], recv.at[slot, a],
                                                device_id=dev, device_id_type=MESH)

        local = [pltpu.make_async_copy(own[a], mine[a].at[3], loc.at[a]) for a in range(n)]
        to_sibling = [remote(own[a], theirs[a].at[3], 3, a, sibling) for a in range(n)]
        to_chips = [remote(src, mine[a].at[k], k, a, (px, py, c))
                    for k, (px, py) in enumerate(peers)
                    for a, src in enumerate(_shard_slices(srcs, dims, 2 * px + py))]
        passed_on = [remote(mine[a].at[k], theirs[a].at[k], 4 + k, a, sibling) for k in range(3) for a in range(n)]
        return local, to_sibling, to_chips, passed_on

    def start(srcs, outs, sems):
        local, to_sibling, to_chips, _ = copies(srcs, outs, sems)
        for cp in local + to_sibling + to_chips:
            cp.start()

    def finish(srcs, outs, sems):
        local, to_sibling, to_chips, passed_on = copies(srcs, outs, sems)
        for arrived, onward in zip(to_chips, passed_on):
            arrived.wait_recv()
            onward.start()
        for cp in to_sibling + passed_on:
            cp.wait_recv()
        for cp in to_sibling + to_chips + passed_on:
            cp.wait_send()
        for cp in local:
            cp.wait()

    scratch = [pltpu.SemaphoreType.DMA((7, n)), pltpu.SemaphoreType.DMA((7, n)), pltpu.SemaphoreType.DMA((n,))]
    return _Exchange(list(grads), piece_shapes * 2, scratch, start, finish)


def _allsum_small(pack):
    rows, cols = pack.shape
    ndev = 8

    def body(p_ref, o_ref, buf, send, recv):
        x, y, c = lax.axis_index("x"), lax.axis_index("y"), lax.axis_index("c")
        me = 4 * x + 2 * y + c
        buf[me] = p_ref[...]
        started = []
        for r in range(1, ndev):
            bx, by, bc = (r >> 2) & 1, (r >> 1) & 1, r & 1
            dev = (x ^ bx, y ^ by, c ^ bc)
            cp = pltpu.make_async_remote_copy(p_ref, buf.at[me], send.at[r], recv.at[r],
                                              device_id=dev, device_id_type=MESH)
            cp.start()
            started.append(cp)
        for r in range(1, ndev):
            pltpu.make_async_remote_copy(p_ref, buf.at[me ^ r], send.at[r], recv.at[r],
                                         device_id=(x, y, c), device_id_type=MESH).wait_recv()
        for cp in started:
            cp.wait_send()
        acc = buf[0]
        for d in range(1, ndev):
            acc = acc + buf[d]
        o_ref[...] = acc

    vm = pl.BlockSpec(memory_space=pltpu.VMEM)
    return _pcall(
        body, name="allsum_small", in_specs=[vm], out_specs=vm,
        out_shape=jax.ShapeDtypeStruct((rows, cols), F32),
        scratch_shapes=[pltpu.VMEM((ndev, rows, cols), F32), pltpu.SemaphoreType.DMA((ndev,)),
                        pltpu.SemaphoreType.DMA((ndev,))],
    )(pack)


def _adamw(parts, w, m, v, layer, prev, name):
    _, rows, cols = w.shape
    tr = ROW_TILE if rows % ROW_TILE == 0 else rows
    counts = [p.shape[0] for p in parts]
    n_parts = len(parts)
    n_prev = 0 if prev is None else 4

    def body(*refs):
        part_refs = refs[:n_parts]
        w_ref, m_ref, v_ref = refs[n_parts:n_parts + 3]
        g_ref, d_ref, nm_ref, nv_ref = refs[n_parts + 3 + n_prev:]
        g = None
        for p_ref, cnt in zip(part_refs, counts):
            s = p_ref[0].astype(F32)
            for k in range(1, cnt):
                s = s + p_ref[k].astype(F32)
            g = s if g is None else g + s
        m2 = ADAM_B1 * m_ref[0] + (1.0 - ADAM_B1) * g
        v2 = ADAM_B2 * v_ref[0] + (1.0 - ADAM_B2) * (g * g)
        m_hat = m2 / (1.0 - ADAM_B1 ** ADAM_STEP)
        v_hat = v2 / (1.0 - ADAM_B2 ** ADAM_STEP)
        g_ref[0] = g
        d_ref[0] = -ADAM_LR * (m_hat / (jnp.sqrt(v_hat) + ADAM_EPS) + ADAM_WD * w_ref[0])
        nm_ref[0] = m2
        nv_ref[0] = v2

    blk = pl.BlockSpec((1, tr, cols), lambda i: (layer, i, 0))
    shp = jax.ShapeDtypeStruct(w.shape, F32)
    return _pcall(
        body, name=name, grid=(rows // tr,),
        in_specs=[pl.BlockSpec((cnt, tr, cols), lambda i: (0, i, 0)) for cnt in counts] + [blk] * 3 + [ANY] * n_prev,
        out_specs=[blk] * 4, out_shape=[shp] * 4,
        input_output_aliases={n_parts + 3 + k: k for k in range(n_prev)},
        compiler_params=_params("parallel"),
    )(*parts, w, m, v, *(prev or ()))


def kernel(x, meta_tokens, pre_norm_g, post_norm_g, w_in, conv_w, conv_b, conv_ln_g, conv_ln_b, w_pw2, b_pw2, w_out, loss_target, m_meta_tokens, m_pre_norm_g, m_post_norm_g, m_w_in, m_conv_w, m_conv_b, m_conv_ln_g, m_conv_ln_b, m_w_pw2, m_b_pw2, m_w_out, v_meta_tokens, v_pre_norm_g, v_post_norm_g, v_w_in, v_conv_w, v_conv_b, v_conv_ln_g, v_conv_ln_b, v_w_pw2, v_b_pw2, v_w_out):
    seq, d = x.shape[1], x.shape[2]
    depth = w_in.shape[0]
    length = N_META + seq
    lp = -(-length // ATT_BLOCK) * ATT_BLOCK
    tap_pad = ((0, 0), (0, CONV_PAD - CONV_WIDTH), (0, 0))

    shards = (w_in.astype(BF16), w_pw2.astype(BF16), w_out.astype(BF16), jnp.pad(conv_w, tap_pad))
    dims = [(1, w_in.shape[2]), (0, w_pw2.shape[1]), (0, w_out.shape[1]), (1, conv_w.shape[2])]
    layer_shards = lambda l: [s[l] for s in shards]

    first = _run_exchange(_gather_plan(layer_shards(0) + [meta_tokens], dims + [(1, meta_tokens.shape[1])]),
                          "gather_weights")
    weights0, meta_f = tuple(first[:4]), first[4]

    h0 = jnp.concatenate([meta_f, x[0], jnp.zeros((lp - length, d), F32)], axis=0)
    target_p = jnp.pad(loss_target[0], ((N_META, lp - length), (0, 0)))
    vecs = (pre_norm_g, post_norm_g, conv_b, conv_ln_g, conv_ln_b, b_pw2)
    loss, dh0, vec_grads, pieces = _local_step(
        h0, target_p, seq, vecs, depth, weights0=weights0,
        gather_next=lambda l: _gather_plan(layer_shards(l + 1), dims),
        reduce_layer=lambda grads: _reduce_plan(list(grads), dims))

    def update(k, w, m, v, name):
        outs = None
        for l in reversed(range(depth)):
            outs = _adamw([pieces[l][k], pieces[l][4 + k]], w, m, v, l, outs, name)
        return outs

    up_w_in = update(0, w_in, m_w_in, v_w_in, "adamw_w_in")
    up_w_pw2 = update(1, w_pw2, m_w_pw2, v_w_pw2, "adamw_w_pw2")
    up_w_out = update(2, w_out, m_w_out, v_w_out, "adamw_w_out")
    up_conv_w = [o[:, :CONV_WIDTH] for o in update(3, jnp.pad(conv_w, tap_pad), jnp.pad(m_conv_w, tap_pad),
                                                   jnp.pad(v_conv_w, tap_pad, constant_values=1.0), "adamw_conv_w")]

    two = lambda a: a.reshape(-1, d)
    vec_rows = [two(g) for g in vec_grads]
    n_vec = sum(a.shape[0] for a in vec_rows)
    pack = jnp.concatenate(vec_rows + [dh0[:N_META], jnp.full((8, d), loss, F32)], axis=0)
    pack = jnp.pad(pack, ((0, -pack.shape[0] % 8), (0, 0)))
    tot = _allsum_small(pack)
    loss_all = tot[n_vec + N_META, 0]

    cat = lambda arrs: jnp.concatenate([two(t) for t in arrs], axis=0)[None]
    small_m = (m_pre_norm_g, m_post_norm_g, m_conv_b, m_conv_ln_g, m_conv_ln_b, m_b_pw2)
    small_v = (v_pre_norm_g, v_post_norm_g, v_conv_b, v_conv_ln_g, v_conv_ln_b, v_b_pw2)
    up_small = _adamw([tot[None, :n_vec]], cat(vecs), cat(small_m), cat(small_v), 0, None, "adamw_vectors")

    def unpack(o):
        res, r0 = [], 0
        for t in vecs:
            nrow = t.size // d
            res.append(o[0, r0:r0 + nrow].reshape(t.shape))
            r0 += nrow
        return res

    up_small = [unpack(o) for o in up_small]
    chip = 2 * lax.axis_index("x") + lax.axis_index("y")
    mcols = meta_tokens.shape[1]
    g_meta = lax.dynamic_slice_in_dim(tot[n_vec:n_vec + N_META], chip * mcols, mcols, axis=1)
    up_meta = [o[0] for o in _adamw([g_meta[None]], meta_tokens[None], m_meta_tokens[None], v_meta_tokens[None],
                                    0, None, "adamw_meta")]

    grad_x = dh0[N_META:length][None]
    outs = [loss_all, grad_x]
    for j in range(4):
        pre, post, cb, lg, lb, bp = up_small[j]
        outs += [up_meta[j], pre, post, up_w_in[j], up_conv_w[j], cb, lg, lb, up_w_pw2[j], bp, up_w_out[j]]
    return tuple(outs)
```

```python
from typing import Callable, NamedTuple

import jax
import jax.numpy as jnp
from jax import lax
from jax.experimental import pallas as pl
from jax.experimental.pallas import tpu as pltpu

F32 = jnp.float32
BF16 = jnp.bfloat16

N_META = 16
D_CONV = 512
D_SB = 512
HEAD_DIM = 64
CONV_WIDTH = 31
CONV_PAD = 32
CONV_ROWS = 128
RMS_EPS = 1e-6
LN_EPS = 1e-5
Q_SCALE = HEAD_DIM ** -0.5

ADAM_LR = 0.001
ADAM_B1 = 0.9
ADAM_B2 = 0.999
ADAM_EPS = 1e-08
ADAM_WD = 0.01
ADAM_STEP = 10

LANES = 128
ROW_TILE = 256
ATT_BLOCK = 256
ATT_PAIRS = 2
VMEM_LIMIT = 56 * 1024 * 1024
EXP_ZERO = -104.0
COUNT_LANE = LANES - 1

MESH = pl.DeviceIdType.MESH


def _pcall(body, **kw):
    return pl.pallas_call(body, **kw)


def _params(*sem):
    return pltpu.CompilerParams(dimension_semantics=sem, vmem_limit_bytes=VMEM_LIMIT)


def _sigmoid(x):
    return 1.0 / (1.0 + jnp.exp(-x))


def _silu_fwd_bwd(x):
    s = _sigmoid(x)
    return x * s, s * (1.0 + x * (1.0 - s))


def _nt(a, b):
    return lax.dot_general(a, b, (((1,), (1,)), ((), ())), preferred_element_type=F32)


def _tn(a, b):
    return lax.dot_general(a, b, (((0,), (0,)), ((), ())), preferred_element_type=F32)


def _nn(a, b):
    return jnp.dot(a, b, preferred_element_type=F32)


def _inproj(h, g, w):
    lp, d = h.shape
    n = w.shape[1]

    def body(h_ref, g_ref, w_ref, ew_ref, qkv_ref):
        x = h_ref[...]
        rstd = lax.rsqrt(jnp.mean(x * x, axis=-1, keepdims=True) + RMS_EPS)
        u = ((x * rstd) * g_ref[...]).astype(BF16)
        p = _nn(u, w_ref[...])
        ew_ref[:, 0:1536] = p[:, 0:1536]
        ew_ref[:, 1536:2048] = p[:, 3072:3584]
        qkv_ref[:, 0:512] = (p[:, 1536:2048] * Q_SCALE).astype(BF16)
        qkv_ref[:, 512:1536] = p[:, 2048:3072].astype(BF16)

    return _pcall(
        body, name="inproj_fwd", grid=(lp // ROW_TILE,),
        in_specs=[pl.BlockSpec((ROW_TILE, d), lambda i: (i, 0)),
                  pl.BlockSpec((1, d), lambda i: (0, 0)),
                  pl.BlockSpec((d, n), lambda i: (0, 0))],
        out_specs=[pl.BlockSpec((ROW_TILE, 2048), lambda i: (i, 0)),
                   pl.BlockSpec((ROW_TILE, 1536), lambda i: (i, 0))],
        out_shape=[jax.ShapeDtypeStruct((lp, 2048), F32), jax.ShapeDtypeStruct((lp, 1536), BF16)],
        compiler_params=_params("parallel"),
    )(h, g, w)


def _layer_norm_stats(c1):
    mu = jnp.mean(c1, axis=-1, keepdims=True)
    xc = c1 - mu
    var = jnp.mean(xc * xc, axis=-1, keepdims=True)
    rstd = lax.rsqrt(var + LN_EPS)
    return xc * rstd, rstd


def _shifted_copies(window, cols, shifted, tm):
    shifted[0] = window[:, cols]
    rows = tm + CONV_PAD - 8
    for b in range(1, 8):
        shifted[b, 0:rows, :] = window[pl.ds(b, rows), cols]


def _shifted_rows(shifted, shift, tm):
    b = shift % 8
    return shifted[b, pl.ds(pl.multiple_of(shift - b, 8), tm), :]


def _weighted(cw8_ref, j, cols, rows):
    w8 = cw8_ref[pl.ds(pl.multiple_of(j * 8, 8), 8), cols]
    r = rows.shape[0]
    return (rows.reshape(r // 8, 8, LANES) * w8[None]).reshape(r, LANES)


def _conv_fwd(ew, cw, cb, lng, lnb, wpw2, bpw2):
    lp = ew.shape[0]
    tm = ROW_TILE

    def body(ew_ref, cw_ref, cb_ref, lng_ref, lnb_ref, w_ref, b_ref, c1_ref, c4_ref, c5_ref, xbuf, shifted):
        @pl.when(pl.program_id(0) == 0)
        def _():
            xbuf[0:CONV_PAD, :] = jnp.zeros((CONV_PAD, D_CONV), F32)

        ga = ew_ref[:, 0:512]
        gb = ew_ref[:, 512:1024]
        cg = ew_ref[:, 1024:1536]
        xbuf[CONV_PAD:CONV_PAD + tm, :] = ga * _sigmoid(gb)
        for blk in range(D_CONV // LANES):
            cs = slice(blk * LANES, (blk + 1) * LANES)
            _shifted_copies(xbuf, cs, shifted, tm)
            for r0 in range(0, tm, CONV_ROWS):
                acc = jnp.zeros((CONV_ROWS, LANES), F32) + cb_ref[:, cs]
                for j in range(CONV_WIDTH):
                    acc = acc + _weighted(cw_ref, j, cs, _shifted_rows(
                        shifted, r0 + CONV_PAD - (CONV_WIDTH - 1) + j, CONV_ROWS))
                c1_ref[r0:r0 + CONV_ROWS, cs] = acc
        xbuf[0:CONV_PAD, :] = xbuf[tm:tm + CONV_PAD, :]
        xhat, _ = _layer_norm_stats(c1_ref[...])
        c2 = xhat * lng_ref[...] + lnb_ref[...]
        c3 = c2 * _sigmoid(c2)
        c4 = _nn(c3.astype(BF16), w_ref[...]) + b_ref[...]
        c4_ref[...] = c4
        c5_ref[...] = (c4 * (cg * _sigmoid(cg))).astype(BF16)

    vec = pl.BlockSpec((1, D_CONV), lambda i: (0, 0))
    row = pl.BlockSpec((tm, D_CONV), lambda i: (i, 0))
    return _pcall(
        body, name="conv_fwd", grid=(lp // tm,),
        in_specs=[pl.BlockSpec((tm, 1536), lambda i: (i, 0)),
                  pl.BlockSpec((8 * CONV_PAD, D_CONV), lambda i: (0, 0)),
                  vec, vec, vec,
                  pl.BlockSpec((D_CONV, D_CONV), lambda i: (0, 0)),
                  vec],
        out_specs=[row, row, row],
        out_shape=[jax.ShapeDtypeStruct((lp, D_CONV), F32), jax.ShapeDtypeStruct((lp, D_CONV), F32),
                   jax.ShapeDtypeStruct((lp, D_CONV), BF16)],
        scratch_shapes=[pltpu.VMEM((tm + CONV_PAD, D_CONV), F32), pltpu.VMEM((8, tm + CONV_PAD, LANES), F32)],
        compiler_params=_params("arbitrary"),
    )(ew, cw, cb, lng, lnb, wpw2, bpw2)


def _split_matmul(x, m01):
    hi = x.astype(BF16)
    lo = (x - hi.astype(F32)).astype(BF16)
    t = _nn(jnp.concatenate([hi, lo], axis=0), m01)
    m = x.shape[0]
    return t[:m] + t[m:]


def _sb_block(q2, kj, z_mask, tri):
    z = _nt(q2, kj)
    ls = -(jnp.maximum(z, 0.0) + jnp.log(1.0 + jnp.exp(-jnp.abs(z))))
    if z_mask is not None:
        ls = jnp.where(z_mask, ls, 0.0)
    return z, ls, _split_matmul(ls, tri)


def _attn_masks():
    lane = lax.broadcasted_iota(jnp.int32, (1, LANES), 1)
    row = lax.broadcasted_iota(jnp.int32, (2 * ATT_BLOCK, ATT_BLOCK), 0)
    col = lax.broadcasted_iota(jnp.int32, (2 * ATT_BLOCK, ATT_BLOCK), 1)
    return lane < HEAD_DIM, col < (row & (ATT_BLOCK - 1))


def _stack_heads(x, first_head):
    zero = jnp.zeros_like(x)
    return jnp.concatenate([jnp.where(first_head, x, zero), jnp.where(first_head, zero, x)], axis=0)


def _unstack_heads(x2, first_head):
    rows = x2.shape[0] // 2
    return jnp.where(first_head, x2[:rows], x2[rows:])


def _hosted(plan, n_out, n_scratch):
    n_in = 0 if plan is None else len(plan.inputs)
    n_x = 0 if plan is None else len(plan.out_shapes)

    def split(rest):
        a, b, c = n_in + n_out, n_in + n_out + n_x, n_in + n_out + n_x + n_scratch
        return rest[:n_in], rest[n_in:a], rest[a:b], rest[b:c], rest[c:]

    if plan is None:
        return split, [], [], [], [], []
    return split, list(plan.inputs), [ANY] * n_in, [ANY] * n_x, list(plan.out_shapes), list(plan.scratch)


def _attn_fwd(qkv, tri, plan=None):
    lp = qkv.shape[0]
    bq = ATT_BLOCK
    ngrp = ATT_PAIRS
    nstep = D_SB // (LANES * ngrp)
    nq = lp // bq
    assert nq <= COUNT_LANE
    split, x_args, x_in_specs, x_out_specs, x_out_shapes, x_scratch = _hosted(plan, 2, 3)

    def body(q_ref, k_ref, v_ref, tri_ref, *rest):
        x_in, (o_ref, carry_ref), x_out, (c_s, acc_s, cm_s), x_sems = split(rest)
        i = pl.program_id(1)
        if plan is not None:
            @pl.when(jnp.logical_and(pl.program_id(0) == 0, i == 0))
            def _():
                plan.start(x_in, x_out, x_sems)

        first_head, vis = _attn_masks()
        lane = lax.broadcasted_iota(jnp.int32, (1, LANES), 1)
        cols = [slice(g * LANES, (g + 1) * LANES) for g in range(ngrp)]
        q2s = [_stack_heads(q_ref[:, cs], first_head) for cs in cols]
        tri_m = tri_ref[...]

        c_s[...] = jnp.zeros_like(c_s)
        acc_s[...] = jnp.zeros_like(acc_s)
        cm_s[...] = jnp.zeros_like(cm_s)

        def blocks(js, masks):
            offs = [pl.multiple_of(j * bq, bq) for j in js]
            work = [(g, b) for b in range(len(js)) for g in range(ngrp)]
            zs = {(g, b): _nt(q2s[g], k_ref[pl.ds(offs[b], bq), cols[g]]) for g, b in work}
            lss = {}
            for g, b in work:
                z = zs[g, b]
                ls = -(jnp.maximum(z, 0.0) + jnp.log(1.0 + jnp.exp(-jnp.abs(z))))
                lss[g, b] = ls if masks[b] is None else jnp.where(masks[b], ls, 0.0)
            tails = {gb: _split_matmul(lss[gb], tri_m) for gb in work}
            probs = {}
            carry = [c_s[g] for g in range(ngrp)]
            saved = [cm_s[g] for g in range(ngrp)]
            for g, b in work:
                a = jnp.exp(zs[g, b] + lss[g, b] + tails[g, b] + carry[g])
                probs[g, b] = (a if masks[b] is None else jnp.where(masks[b], a, 0.0)).astype(BF16)
                saved[g] = jnp.where(lane == js[b], carry[g], saved[g])
                carry[g] = carry[g] + tails[g, b][:, 0:1] + lss[g, b][:, 0:1]
            top = None
            for g in range(ngrp):
                c_s[g] = carry[g]
                cm_s[g] = saved[g]
                acc = acc_s[g]
                for b in range(len(js)):
                    acc = acc + _nn(probs[g, b], v_ref[pl.ds(offs[b], bq), cols[g]])
                acc_s[g] = acc
                top = carry[g] if top is None else jnp.maximum(top, carry[g])
            return jnp.max(top) > EXP_ZERO

        alive = lax.cond(i > 0, lambda: blocks([i, i - 1], [vis, None]), lambda: blocks([i], [vis]))
        rest = jnp.maximum(i - 1, 0)

        def pair(carry):
            t, _ = carry
            j = i - 2 - 2 * t
            return t + 1, blocks([j, j - 1], [None, None])

        trips, alive = lax.while_loop(lambda ca: jnp.logical_and(ca[0] < rest // 2, ca[1]), pair, (0, alive))
        last = jnp.logical_and(jnp.logical_and(rest % 2 == 1, trips == rest // 2), alive)

        @pl.when(last)
        def _():
            blocks([0], [None])

        n_done = (jnp.minimum(i + 1, 2) + 2 * trips + last.astype(jnp.int32)).astype(F32)
        for g in range(ngrp):
            cmat = jnp.where(lane == COUNT_LANE, n_done, cm_s[g])
            carry_ref[:, 2 * g * LANES:(2 * g + 1) * LANES] = cmat[:bq]
            carry_ref[:, (2 * g + 1) * LANES:(2 * g + 2) * LANES] = cmat[bq:]
            o_ref[:, cols[g]] = _unstack_heads(acc_s[g], first_head)
        if plan is not None:
            @pl.when(jnp.logical_and(pl.program_id(0) == nstep - 1, i == nq - 1))
            def _():
                plan.finish(x_in, x_out, x_sems)

    width = ngrp * LANES
    outs = _pcall(
        body, name="attn_fwd" if plan is None else "attn_fwd_gather", grid=(nstep, nq),
        in_specs=[pl.BlockSpec((bq, width), lambda p, i: (i, p)),
                  pl.BlockSpec((lp, width), lambda p, i: (0, nstep + p)),
                  pl.BlockSpec((lp, width), lambda p, i: (0, 2 * nstep + p)),
                  pl.BlockSpec((bq, bq), lambda p, i: (0, 0))] + x_in_specs,
        out_specs=[pl.BlockSpec((bq, width), lambda p, i: (i, p)),
                   pl.BlockSpec((bq, 2 * width), lambda p, i: (i, p))] + x_out_specs,
        out_shape=[jax.ShapeDtypeStruct((lp, D_SB), F32), jax.ShapeDtypeStruct((lp, 2 * D_SB), F32)] + x_out_shapes,
        scratch_shapes=[pltpu.VMEM((ngrp, 2 * bq, 1), F32), pltpu.VMEM((ngrp, 2 * bq, LANES), F32),
                        pltpu.VMEM((ngrp, 2 * bq, LANES), F32)] + x_scratch,
        compiler_params=_params("arbitrary", "arbitrary"),
    )(qkv, qkv, qkv, tri, *x_args)
    return outs[0], outs[1], outs[2:]


def _outproj(c5, att, ew, h, w, g):
    lp, d = h.shape
    tm = ROW_TILE

    def body(c5_ref, att_ref, sg_ref, h_ref, w_ref, g_ref, hn_ref, cat_ref, mix_ref):
        sg = sg_ref[...]
        s = att_ref[...] * (sg * _sigmoid(sg))
        cat_ref[:, 0:D_CONV] = c5_ref[...]
        cat_ref[:, D_CONV:] = s.astype(BF16)
        mixed = _nn(cat_ref[...], w_ref[...])
        mix_ref[...] = mixed
        rstd = lax.rsqrt(jnp.mean(mixed * mixed, axis=-1, keepdims=True) + RMS_EPS)
        hn_ref[...] = h_ref[...] + (mixed * rstd) * g_ref[...]

    half = pl.BlockSpec((tm, 512), lambda i: (i, 0))
    full = pl.BlockSpec((tm, d), lambda i: (i, 0))
    return _pcall(
        body, name="outproj_fwd", grid=(lp // tm,),
        in_specs=[half, half, pl.BlockSpec((tm, 512), lambda i: (i, 3)), full,
                  pl.BlockSpec((d, d), lambda i: (0, 0)), pl.BlockSpec((1, d), lambda i: (0, 0))],
        out_specs=[full, full, full],
        out_shape=[jax.ShapeDtypeStruct((lp, d), F32), jax.ShapeDtypeStruct((lp, d), BF16),
                   jax.ShapeDtypeStruct((lp, d), F32)],
        compiler_params=_params("parallel"),
    )(c5, att, ew, h, w, g)


def _loss_head(h, target, seq):
    lp, d = h.shape
    tm = ROW_TILE

    def body(h_ref, t_ref, dh_ref, loss_ref):
        i = pl.program_id(0)

        @pl.when(i == 0)
        def _():
            loss_ref[...] = jnp.zeros_like(loss_ref)

        row = i * tm + lax.broadcasted_iota(jnp.int32, (tm, 1), 0)
        real = jnp.logical_and(row >= N_META, row < N_META + seq)
        diff = jnp.where(real, h_ref[...] - t_ref[...], 0.0)
        dh_ref[...] = diff * (1.0 / d)
        loss_ref[...] += 0.5 * jnp.sum(jnp.sum(diff * diff, axis=-1, keepdims=True) * (1.0 / d))

    full = pl.BlockSpec((tm, d), lambda i: (i, 0))
    return _pcall(
        body, name="loss_head", grid=(lp // tm,),
        in_specs=[full, full],
        out_specs=[full, pl.BlockSpec((8, LANES), lambda i: (0, 0))],
        out_shape=[jax.ShapeDtypeStruct((lp, d), F32), jax.ShapeDtypeStruct((8, LANES), F32)],
        compiler_params=_params("arbitrary"),
    )(h, target)


def _outproj_bwd(dh, mixed, g, w, att, ew, c4):
    lp, d = dh.shape
    tm = ROW_TILE

    def body(dh_ref, mix_ref, g_ref, w_ref, att_ref, cg_ref, sg_ref, c4_ref,
             dmix_ref, datt_ref, dsg_ref, dc4_ref, dcg_ref, dg_ref, db_ref):
        @pl.when(pl.program_id(0) == 0)
        def _():
            dg_ref[...] = jnp.zeros_like(dg_ref)
            db_ref[...] = jnp.zeros_like(db_ref)

        mixed = mix_ref[...]
        dhv = dh_ref[...]
        rstd = lax.rsqrt(jnp.mean(mixed * mixed, axis=-1, keepdims=True) + RMS_EPS)
        n = mixed * rstd
        dg_ref[...] += jnp.sum(dhv * n, axis=0, keepdims=True)
        dn = dhv * g_ref[...]
        dmix = (rstd * (dn - n * jnp.mean(dn * n, axis=-1, keepdims=True))).astype(BF16)
        dmix_ref[...] = dmix
        dcat = _nt(dmix, w_ref[...])
        dc5 = dcat[:, 0:D_CONV]
        ds = dcat[:, D_CONV:]
        silu_sg, dsilu_sg = _silu_fwd_bwd(sg_ref[...])
        datt_ref[...] = (ds * silu_sg).astype(BF16)
        dsg_ref[...] = (ds * att_ref[...] * dsilu_sg).astype(BF16)
        silu_cg, dsilu_cg = _silu_fwd_bwd(cg_ref[...])
        dc4 = dc5 * silu_cg
        db_ref[...] += jnp.sum(dc4, axis=0, keepdims=True)
        dc4_ref[...] = dc4.astype(BF16)
        dcg_ref[...] = (dc5 * c4_ref[...] * dsilu_cg).astype(BF16)

    half = pl.BlockSpec((tm, 512), lambda i: (i, 0))
    full = pl.BlockSpec((tm, d), lambda i: (i, 0))
    hb = jax.ShapeDtypeStruct((lp, 512), BF16)
    return _pcall(
        body, name="outproj_bwd", grid=(lp // tm,),
        in_specs=[full, full, pl.BlockSpec((1, d), lambda i: (0, 0)), pl.BlockSpec((d, d), lambda i: (0, 0)),
                  half, pl.BlockSpec((tm, 512), lambda i: (i, 2)), pl.BlockSpec((tm, 512), lambda i: (i, 3)), half],
        out_specs=[full, half, half, half, half,
                   pl.BlockSpec((1, d), lambda i: (0, 0)), pl.BlockSpec((1, 512), lambda i: (0, 0))],
        out_shape=[jax.ShapeDtypeStruct((lp, d), BF16), hb, hb, hb, hb,
                   jax.ShapeDtypeStruct((1, d), F32), jax.ShapeDtypeStruct((1, 512), F32)],
        compiler_params=_params("arbitrary"),
    )(dh, mixed, g, w, att, ew, ew, c4)


def _attn_bwd(qkv, carries, datt, tri, upper, plan=None):
    lp = qkv.shape[0]
    bq = ATT_BLOCK
    ngrp = ATT_PAIRS
    nstep = D_SB // (LANES * ngrp)
    nq = lp // bq
    split, x_args, x_in_specs, x_out_specs, x_out_shapes, x_scratch = _hosted(plan, 3, 2)

    def body(q_ref, k_ref, v_ref, carry_ref, do_ref, tri_ref, upper_ref, *rest):
        x_in, (dq_ref, dk_ref, dv_ref), x_out, (run_s, dq_s), x_sems = split(rest)
        i = pl.program_id(1)
        if plan is not None:
            @pl.when(jnp.logical_and(pl.program_id(0) == 0, i == 0))
            def _():
                plan.start(x_in, x_out, x_sems)

        @pl.when(i == 0)
        def _():
            dk_ref[...] = jnp.zeros_like(dk_ref)
            dv_ref[...] = jnp.zeros_like(dv_ref)

        first_head, vis = _attn_masks()
        lane = lax.broadcasted_iota(jnp.int32, (1, LANES), 1)
        cols = [slice(g * LANES, (g + 1) * LANES) for g in range(ngrp)]
        q2s = [_stack_heads(q_ref[:, cs], first_head) for cs in cols]
        do2s = [_stack_heads(do_ref[:, cs], first_head) for cs in cols]
        cmats = [jnp.concatenate([carry_ref[:, 2 * g * LANES:(2 * g + 1) * LANES],
                                  carry_ref[:, (2 * g + 1) * LANES:(2 * g + 2) * LANES]], axis=0)
                 for g in range(ngrp)]
        tri_m = tri_ref[...]
        upper_m = upper_ref[...]

        def blocks(js, masks):
            offs = [pl.multiple_of(j * bq, bq) for j in js]
            work = [(g, b) for b in range(len(js)) for g in range(ngrp)]
            zs = {(g, b): _nt(q2s[g], k_ref[pl.ds(offs[b], bq), cols[g]]) for g, b in work}
            lss = {}
            for g, b in work:
                z = zs[g, b]
                ls = -(jnp.maximum(z, 0.0) + jnp.log(1.0 + jnp.exp(-jnp.abs(z))))
                lss[g, b] = ls if masks[b] is None else jnp.where(masks[b], ls, 0.0)
            tails = {gb: _split_matmul(lss[gb], tri_m) for gb in work}
            das = {(g, b): _nt(do2s[g], v_ref[pl.ds(offs[b], bq), cols[g]]) for g, b in work}
            probs, des = {}, {}
            for g, b in work:
                c = jnp.sum(jnp.where(lane == js[b], cmats[g], 0.0), axis=-1, keepdims=True)
                a = jnp.exp(zs[g, b] + lss[g, b] + tails[g, b] + c)
                a = a if masks[b] is None else jnp.where(masks[b], a, 0.0)
                probs[g, b] = a.astype(BF16)
                des[g, b] = das[g, b] * a
            prefixes = {gb: _split_matmul(des[gb], upper_m) for gb in work}
            runs = [run_s[g] for g in range(ngrp)]
            dzs = {}
            for g, b in work:
                beta = jnp.exp(zs[g, b] + lss[g, b])
                dz = des[g, b] - beta * (des[g, b] + runs[g] + prefixes[g, b])
                dzs[g, b] = (dz if masks[b] is None else jnp.where(masks[b], dz, 0.0)).astype(BF16)
                runs[g] = runs[g] + prefixes[g, b][:, bq - 1:bq] + des[g, b][:, bq - 1:bq]
            for g in range(ngrp):
                run_s[g] = runs[g]
                dq = dq_s[g]
                for b in range(len(js)):
                    rows = pl.ds(offs[b], bq)
                    dq = dq + _nn(dzs[g, b], k_ref[rows, cols[g]])
                    dk_ref[rows, cols[g]] += _tn(dzs[g, b], q2s[g])
                    dv_ref[rows, cols[g]] += _tn(probs[g, b], do2s[g])
                dq_s[g] = dq

        n_done = jnp.max(carry_ref[:, COUNT_LANE:COUNT_LANE + 1]).astype(jnp.int32)
        n_done = jnp.clip(n_done, 1, i + 1)
        before = jnp.maximum(n_done - 2, 0)
        j0 = i - n_done + 1
        odd = before % 2
        run_s[...] = jnp.zeros_like(run_s)
        dq_s[...] = jnp.zeros_like(dq_s)

        @pl.when(odd == 1)
        def _():
            blocks([j0], [None])

        @pl.loop(0, before // 2)
        def _(t):
            blocks([j0 + odd + 2 * t, j0 + odd + 2 * t + 1], [None, None])

        @pl.when(n_done > 1)
        def _():
            blocks([i - 1, i], [None, vis])

        @pl.when(n_done <= 1)
        def _():
            blocks([i], [vis])

        for g in range(ngrp):
            dq_ref[:, cols[g]] = (_unstack_heads(dq_s[g], first_head) * Q_SCALE).astype(BF16)
        if plan is not None:
            @pl.when(jnp.logical_and(pl.program_id(0) == nstep - 1, i == nq - 1))
            def _():
                plan.finish(x_in, x_out, x_sems)

    width = ngrp * LANES
    qb = pl.BlockSpec((bq, width), lambda p, i: (i, p))
    colb = pl.BlockSpec((lp, width), lambda p, i: (0, p))
    sq = pl.BlockSpec((bq, bq), lambda p, i: (0, 0))
    outs = _pcall(
        body, name="attn_bwd" if plan is None else "attn_bwd_reduce", grid=(nstep, nq),
        in_specs=[qb,
                  pl.BlockSpec((lp, width), lambda p, i: (0, nstep + p)),
                  pl.BlockSpec((lp, width), lambda p, i: (0, 2 * nstep + p)),
                  pl.BlockSpec((bq, 2 * width), lambda p, i: (i, p)), qb, sq, sq] + x_in_specs,
        out_specs=[qb, colb, colb] + x_out_specs,
        out_shape=[jax.ShapeDtypeStruct((lp, D_SB), BF16), jax.ShapeDtypeStruct((lp, D_SB), F32),
                   jax.ShapeDtypeStruct((lp, D_SB), F32)] + x_out_shapes,
        scratch_shapes=[pltpu.VMEM((ngrp, 2 * bq, 1), F32), pltpu.VMEM((ngrp, 2 * bq, LANES), F32)] + x_scratch,
        compiler_params=_params("arbitrary", "arbitrary"),
    )(qkv, qkv, qkv, carries, datt, tri, upper, *x_args)
    return outs[0], outs[1], outs[2], outs[3:]


def _conv_bwd(dc4, c1, ew, cw, lng, lnb, wpw2):
    lp = ew.shape[0]
    tm = ROW_TILE
    nt = lp // tm
    halo_per_tile = tm // CONV_PAD

    def body(dc4_ref, c1_ref, ew_ref, halo_ref, cw_ref, lng_ref, lnb_ref, w_ref,
             dga_ref, dgb_ref, c3_ref, dcw_ref, dcb_ref, dlng_ref, dlnb_ref, xbuf, dbuf, shifted, wacc):
        step = pl.program_id(0)

        @pl.when(step == 0)
        def _():
            wacc[...] = jnp.zeros_like(wacc)
            dcb_ref[...] = jnp.zeros_like(dcb_ref)
            dlng_ref[...] = jnp.zeros_like(dlng_ref)
            dlnb_ref[...] = jnp.zeros_like(dlnb_ref)
            dbuf[tm:tm + CONV_PAD, :] = jnp.zeros((CONV_PAD, D_CONV), F32)

        dc3 = _nt(dc4_ref[...], w_ref[...])
        xhat, rstd = _layer_norm_stats(c1_ref[...])
        c2 = xhat * lng_ref[...] + lnb_ref[...]
        c3, dsilu = _silu_fwd_bwd(c2)
        c3_ref[...] = c3.astype(BF16)
        dc2 = dc3 * dsilu
        dlng_ref[...] += jnp.sum(dc2 * xhat, axis=0, keepdims=True)
        dlnb_ref[...] += jnp.sum(dc2, axis=0, keepdims=True)
        dxhat = dc2 * lng_ref[...]
        dc1 = rstd * (dxhat - jnp.mean(dxhat, axis=-1, keepdims=True)
                      - xhat * jnp.mean(dxhat * xhat, axis=-1, keepdims=True))
        dcb_ref[...] += jnp.sum(dc1, axis=0, keepdims=True)
        dbuf[0:tm, :] = dc1

        ga = ew_ref[:, 0:512]
        sgb = _sigmoid(ew_ref[:, 512:1024])
        xbuf[CONV_PAD:CONV_PAD + tm, :] = ga * sgb
        first_tile = step == nt - 1
        halo = halo_ref[:, 0:512] * _sigmoid(halo_ref[:, 512:1024])
        xbuf[0:CONV_PAD, :] = jnp.where(first_tile, 0.0, halo)

        for cb in range(D_CONV // LANES):
            cs = slice(cb * LANES, (cb + 1) * LANES)
            _shifted_copies(dbuf, cs, shifted, tm)
            for r0 in range(0, tm, CONV_ROWS):
                rs = slice(r0, r0 + CONV_ROWS)
                dc0 = jnp.zeros((CONV_ROWS, LANES), F32)
                for j in range(CONV_WIDTH):
                    dc0 = dc0 + _weighted(cw_ref, j, cs, _shifted_rows(
                        shifted, r0 + CONV_WIDTH - 1 - j, CONV_ROWS))
                dga_ref[rs, cs] = (dc0 * sgb[rs, cs]).astype(BF16)
                dgb_ref[rs, cs] = (dc0 * ga[rs, cs] * sgb[rs, cs] * (1.0 - sgb[rs, cs])).astype(BF16)
            _shifted_copies(xbuf, cs, shifted, tm)
            for r0 in range(0, tm, CONV_ROWS):
                d1 = dbuf[r0:r0 + CONV_ROWS, cs]

                for j in range(CONV_WIDTH):
                    prod = d1 * _shifted_rows(shifted, r0 + CONV_PAD - (CONV_WIDTH - 1) + j, CONV_ROWS)
                    wacc[j * 8:(j + 1) * 8, cs] += jnp.sum(prod.reshape(CONV_ROWS // 8, 8, LANES), axis=0)
        dbuf[tm:tm + CONV_PAD, :] = dbuf[0:CONV_PAD, :]

        @pl.when(step == nt - 1)
        def _():
            dcw_ref[...] = jnp.sum(wacc[...].reshape(CONV_PAD, 8, D_CONV), axis=1)

    rev = lambda i: (nt - 1 - i, 0)
    row = pl.BlockSpec((tm, D_CONV), rev)
    vec = pl.BlockSpec((1, D_CONV), lambda i: (0, 0))
    hb = jax.ShapeDtypeStruct((lp, D_CONV), BF16)
    vs = jax.ShapeDtypeStruct((1, D_CONV), F32)
    return _pcall(
        body, name="conv_bwd", grid=(nt,),
        in_specs=[row, row, pl.BlockSpec((tm, 1024), rev),
                  pl.BlockSpec((CONV_PAD, 1024), lambda i: (jnp.maximum((nt - 1 - i) * halo_per_tile - 1, 0), 0)),
                  pl.BlockSpec((8 * CONV_PAD, D_CONV), lambda i: (0, 0)), vec, vec,
                  pl.BlockSpec((D_CONV, D_CONV), lambda i: (0, 0))],
        out_specs=[row, row, row, pl.BlockSpec((CONV_PAD, D_CONV), lambda i: (0, 0)), vec, vec, vec],
        out_shape=[hb, hb, hb, jax.ShapeDtypeStruct((CONV_PAD, D_CONV), F32), vs, vs, vs],
        scratch_shapes=[pltpu.VMEM((tm + CONV_PAD, D_CONV), F32), pltpu.VMEM((tm + CONV_PAD, D_CONV), F32),
                        pltpu.VMEM((8, tm + CONV_PAD, LANES), F32), pltpu.VMEM((8 * CONV_PAD, D_CONV), F32)],
        compiler_params=_params("arbitrary"),
    )(dc4, c1, ew, ew, cw, lng, lnb, wpw2)


def _inproj_bwd(dga, dgb, dcg, dq, dk, dv, dsg, h, g, w, dh_out):
    lp, d = h.shape
    n = w.shape[1]
    tm = ROW_TILE

    def body(dga_ref, dgb_ref, dcg_ref, dq_ref, dk_ref, dv_ref, dsg_ref, h_ref, g_ref, w_ref, dho_ref,
             dh_ref, dproj_ref, u_ref, dg_ref):
        @pl.when(pl.program_id(0) == 0)
        def _():
            dg_ref[...] = jnp.zeros_like(dg_ref)

        dproj_ref[:, 0:512] = dga_ref[...]
        dproj_ref[:, 512:1024] = dgb_ref[...]
        dproj_ref[:, 1024:1536] = dcg_ref[...]
        dproj_ref[:, 1536:2048] = dq_ref[...]
        dproj_ref[:, 2048:2560] = dk_ref[...].astype(BF16)
        dproj_ref[:, 2560:3072] = dv_ref[...].astype(BF16)
        dproj_ref[:, 3072:3584] = dsg_ref[...]
        du = _nt(dproj_ref[...], w_ref[...])
        x = h_ref[...]
        rstd = lax.rsqrt(jnp.mean(x * x, axis=-1, keepdims=True) + RMS_EPS)
        nrm = x * rstd
        u_ref[...] = (nrm * g_ref[...]).astype(BF16)
        dg_ref[...] += jnp.sum(du * nrm, axis=0, keepdims=True)
        dn = du * g_ref[...]
        dh_ref[...] = dho_ref[...] + rstd * (dn - nrm * jnp.mean(dn * nrm, axis=-1, keepdims=True))

    half = pl.BlockSpec((tm, 512), lambda i: (i, 0))
    full = pl.BlockSpec((tm, d), lambda i: (i, 0))
    return _pcall(
        body, name="inproj_bwd", grid=(lp // tm,),
        in_specs=[half] * 7 + [full, pl.BlockSpec((1, d), lambda i: (0, 0)),
                               pl.BlockSpec((d, n), lambda i: (0, 0)), full],
        out_specs=[full, pl.BlockSpec((tm, n), lambda i: (i, 0)), full, pl.BlockSpec((1, d), lambda i: (0, 0))],
        out_shape=[jax.ShapeDtypeStruct((lp, d), F32), jax.ShapeDtypeStruct((lp, n), BF16),
                   jax.ShapeDtypeStruct((lp, d), BF16), jax.ShapeDtypeStruct((1, d), F32)],
        compiler_params=_params("arbitrary"),
    )(dga, dgb, dcg, dq, dk, dv, dsg, h, g, w, dh_out)


def _row_split(m, parts):
    tm = m // parts
    assert tm * parts == m and tm % 16 == 0, (m, parts)
    return tm


def _matmul_tn(x, dy, tn, name):
    m, k = x.shape
    n = dy.shape[1]
    steps = 4 if m % 64 == 0 else 1
    tm = _row_split(m, steps)

    def body(x_ref, dy_ref, o_ref, acc_ref):
        r = pl.program_id(1)

        @pl.when(r == 0)
        def _():
            acc_ref[...] = jnp.zeros_like(acc_ref)

        acc_ref[...] += _tn(x_ref[...], dy_ref[...])

        @pl.when(r == steps - 1)
        def _():
            o_ref[...] = acc_ref[...].astype(BF16)

    return _pcall(
        body, name=name, grid=(n // tn, steps),
        in_specs=[pl.BlockSpec((tm, k), lambda j, r: (r, 0)), pl.BlockSpec((tm, tn), lambda j, r: (r, j))],
        out_specs=pl.BlockSpec((k, tn), lambda j, r: (0, j)),
        out_shape=jax.ShapeDtypeStruct((k, n), BF16),
        scratch_shapes=[pltpu.VMEM((k, tn), F32)],
        compiler_params=_params("parallel", "arbitrary"),
    )(x, dy)


def _local_step(h0, target_p, seq, vecs, depth, all_weights=None, weights0=None, gather_next=None,
                reduce_layer=None):
    pre_g, post_g, conv_b, ln_g, ln_b, b_pw2 = vecs
    ar = jnp.arange(ATT_BLOCK)
    tri = (ar[:, None] > ar[None, :]).astype(BF16)
    upper = (ar[:, None] < ar[None, :]).astype(BF16)
    row = lambda a, l: a[l][None, :]

    weights = list(all_weights) if all_weights is not None else [weights0] + [None] * (depth - 1)
    saved = []
    h = h0
    for l in range(depth):
        w_in, w_pw2, w_out, conv_w = weights[l]
        ew, qkv = _inproj(h, row(pre_g, l), w_in)
        conv_w = jnp.repeat(conv_w, 8, axis=0)
        weights[l] = (w_in, w_pw2, w_out, conv_w)
        c1, c4, c5 = _conv_fwd(ew, conv_w, row(conv_b, l), row(ln_g, l), row(ln_b, l), w_pw2, row(b_pw2, l))
        plan = gather_next(l) if gather_next is not None and l + 1 < depth else None
        att, carries, gathered = _attn_fwd(qkv, tri, plan)
        if plan is not None:
            weights[l + 1] = tuple(gathered)
        hn, cat, mixed = _outproj(c5, att, ew, h, w_out, row(post_g, l))
        saved.append((h, ew, qkv, c1, c4, att, carries, cat, mixed))
        h = hn

    dh, loss = _loss_head(h, target_p, seq)

    vec_grads = [None] * depth
    mat_grads = [None] * depth
    pending = None
    for l in reversed(range(depth)):
        w_in, w_pw2, w_out, conv_w = weights[l]
        h_in, ew, qkv, c1, c4, att, carries, cat, mixed = saved[l]
        dmix, datt, dsg, dc4, dcg, dpost, dbpw2 = _outproj_bwd(dh, mixed, row(post_g, l), w_out, att, ew, c4)
        dw_out = _matmul_tn(cat, dmix, 512, "dw_out")
        dq, dk, dv, landed = _attn_bwd(qkv, carries, datt, tri, upper, pending)
        if pending is not None:
            mat_grads[l + 1] = landed
        dga, dgb, c3, dcw, dcb, dlng, dlnb = _conv_bwd(dc4, c1, ew, conv_w, row(ln_g, l), row(ln_b, l), w_pw2)
        dw_pw2 = _matmul_tn(c3, dc4, 512, "dw_pw2")
        dh, dproj, u, dpre = _inproj_bwd(dga, dgb, dcg, dq, dk, dv, dsg, h_in, row(pre_g, l), w_in, dh)
        dw_in = _matmul_tn(u, dproj, 896, "dw_in")
        vec_grads[l] = (dpre[0], dpost[0], dcb[0], dlng[0], dlnb[0], dbpw2[0])
        mats = (dw_in, dw_pw2, dw_out, dcw)
        if reduce_layer is None:
            mat_grads[l] = mats
        else:
            pending = reduce_layer(mats)
    if pending is not None:
        mat_grads[0] = _run_exchange(pending, "reduce_grads")

    vec_grads = [jnp.stack([g[k] for g in vec_grads]) for k in range(len(vecs))]
    return loss[0, 0], dh, vec_grads, mat_grads


N_CHIPS = 4
ANY = pl.BlockSpec(memory_space=pl.ANY)


def _chip_peers():
    x, y, c = lax.axis_index("x"), lax.axis_index("y"), lax.axis_index("c")
    return x, y, c, [(x, 1 - y), (1 - x, y), (1 - x, 1 - y)]


def _shard_slices(refs, dims, idx):
    out = []
    for ref, (axis, size) in zip(refs, dims):
        assert size % LANES == 0
        start = pl.multiple_of(idx * size, LANES)
        sl = [slice(None)] * len(ref.shape)
        sl[axis] = pl.ds(start, size)
        out.append(ref.at[tuple(sl)])
    return out


class _Exchange(NamedTuple):
    inputs: list
    out_shapes: list
    scratch: list
    start: Callable
    finish: Callable


def _run_exchange(plan, name):
    n_in, n_out = len(plan.inputs), len(plan.out_shapes)

    def body(*refs):
        parts = refs[:n_in], refs[n_in:n_in + n_out], refs[n_in + n_out:]
        plan.start(*parts)
        plan.finish(*parts)

    return _pcall(body, name=name, in_specs=[ANY] * n_in, out_specs=[ANY] * n_out, out_shape=plan.out_shapes,
                  scratch_shapes=plan.scratch)(*plan.inputs)


def _gather_plan(shards, dims):
    n = len(shards)
    full_shapes = []
    halves = []
    for s, (axis, size) in zip(shards, dims):
        shp = list(s.shape)
        shp[axis] = size * N_CHIPS
        full_shapes.append(jax.ShapeDtypeStruct(tuple(shp), s.dtype))
        tile_rows = 32 // s.dtype.itemsize
        assert s.shape[0] % (2 * tile_rows) == 0
        halves.append((s.shape[0] // 2, tile_rows))

    def half(refs, which):
        return [r.at[pl.ds(pl.multiple_of(which * h, t), h)] for r, (h, t) in zip(refs, halves)]

    def copies(srcs, outs, sems):
        send, recv, loc = sems
        x, y, c, peers = _chip_peers()
        sibling = (x, y, 1 - c)
        mine = _shard_slices(outs, dims, 2 * x + y)
        local = [pltpu.make_async_copy(s, d, loc.at[a]) for a, (s, d) in enumerate(zip(srcs, mine))]

        def remote(src, dst, slot, a, dev):
            return pltpu.make_async_remote_copy(src, dst, send.at[slot, a], recv.at[slot, a],
                                                device_id=dev, device_id_type=MESH)

        sends = [remote(s, d, k, a, (px, py, c))
                 for k, (px, py) in enumerate(peers) for a, (s, d) in enumerate(zip(half(srcs, c), half(mine, c)))]
        theirs = [_shard_slices(outs, dims, 2 * px + py) for px, py in peers]
        arrivals = [remote(s, d, k, a, (px, py, c))
                    for k, (px, py) in enumerate(peers)
                    for a, (s, d) in enumerate(zip(half(srcs, c), half(theirs[k], c)))]
        passed_on = [remote(d, d, 3 + k, a, sibling) for k in range(3) for a, d in enumerate(half(theirs[k], c))]
        from_sibling = [remote(d, d, 3 + k, a, sibling)
                        for k in range(3) for a, d in enumerate(half(theirs[k], 1 - c))]
        return local, sends, arrivals, passed_on, from_sibling

    def start(srcs, outs, sems):
        local, sends = copies(srcs, outs, sems)[:2]
        for cp in local + sends:
            cp.start()

    def finish(srcs, outs, sems):
        local, sends, arrivals, passed_on, from_sibling = copies(srcs, outs, sems)
        for arrived, onward in zip(arrivals, passed_on):
            arrived.wait_recv()
            onward.start()
        for cp in from_sibling:
            cp.wait_recv()
        for cp in sends + passed_on:
            cp.wait_send()
        for cp in local:
            cp.wait()

    scratch = [pltpu.SemaphoreType.DMA((6, n)), pltpu.SemaphoreType.DMA((6, n)), pltpu.SemaphoreType.DMA((n,))]
    return _Exchange(list(shards), full_shapes, scratch, start, finish)


def _reduce_plan(grads, dims):
    n = len(grads)
    piece_shapes = []
    for g, (axis, size) in zip(grads, dims):
        shp = list(g.shape)
        shp[axis] = size
        piece_shapes.append(jax.ShapeDtypeStruct((N_CHIPS,) + tuple(shp), g.dtype))

    def copies(srcs, outs, sems):
        mine, theirs = outs[:n], outs[n:]
        send, recv, loc = sems
        x, y, c, peers = _chip_peers()
        sibling = (x, y, 1 - c)
        own = _shard_slices(srcs, dims, 2 * x + y)

        def remote(src, dst, slot, a, dev):
            return pltpu.make_async_remote_copy(src, dst, send.at[slot, a], recv.at[slot, a],
                                                device_id=dev, device_id_type=MESH)

        local = [pltpu.make_async_copy(own[a], mine[a].at[3], loc.at[a]) for a in range(n)]
        to_sibling = [remote(own[a], theirs[a].at[3], 3, a, sibling) for a in range(n)]
        to_chips = [remote(src, mine[a].at[k], k, a, (px, py, c))
                    for k, (px, py) in enumerate(peers)
                    for a, src in enumerate(_shard_slices(srcs, dims, 2 * px + py))]
        passed_on = [remote(mine[a].at[k], theirs[a].at[k], 4 + k, a, sibling) for k in range(3) for a in range(n)]
        return local, to_sibling, to_chips, passed_on

    def start(srcs, outs, sems):
        local, to_sibling, to_chips, _ = copies(srcs, outs, sems)
        for cp in local + to_sibling + to_chips:
            cp.start()

    def finish(srcs, outs, sems):
        local, to_sibling, to_chips, passed_on = copies(srcs, outs, sems)
        for arrived, onward in zip(to_chips, passed_on):
            arrived.wait_recv()
            onward.start()
        for cp in to_sibling + passed_on:
            cp.wait_recv()
        for cp in to_sibling + to_chips + passed_on:
            cp.wait_send()
        for cp in local:
            cp.wait()

    scratch = [pltpu.SemaphoreType.DMA((7, n)), pltpu.SemaphoreType.DMA((7, n)), pltpu.SemaphoreType.DMA((n,))]
    return _Exchange(list(grads), piece_shapes * 2, scratch, start, finish)


def _allsum_small(pack):
    rows, cols = pack.shape
    ndev = 8

    def body(p_ref, o_ref, buf, send, recv):
        x, y, c = lax.axis_index("x"), lax.axis_index("y"), lax.axis_index("c")
        me = 4 * x + 2 * y + c
        buf[me] = p_ref[...]
        started = []
        for r in range(1, ndev):
            bx, by, bc = (r >> 2) & 1, (r >> 1) & 1, r & 1
            dev = (x ^ bx, y ^ by, c ^ bc)
            cp = pltpu.make_async_remote_copy(p_ref, buf.at[me], send.at[r], recv.at[r],
                                              device_id=dev, device_id_type=MESH)
            cp.start()
            started.append(cp)
        for r in range(1, ndev):
            pltpu.make_async_remote_copy(p_ref, buf.at[me ^ r], send.at[r], recv.at[r],
                                         device_id=(x, y, c), device_id_type=MESH).wait_recv()
        for cp in started:
            cp.wait_send()
        acc = buf[0]
        for d in range(1, ndev):
            acc = acc + buf[d]
        o_ref[...] = acc

    vm = pl.BlockSpec(memory_space=pltpu.VMEM)
    return _pcall(
        body, name="allsum_small", in_specs=[vm], out_specs=vm,
        out_shape=jax.ShapeDtypeStruct((rows, cols), F32),
        scratch_shapes=[pltpu.VMEM((ndev, rows, cols), F32), pltpu.SemaphoreType.DMA((ndev,)),
                        pltpu.SemaphoreType.DMA((ndev,))],
    )(pack)


def _adamw(parts, w, m, v, layer, prev, name):
    _, rows, cols = w.shape
    tr = ROW_TILE if rows % ROW_TILE == 0 else rows
    counts = [p.shape[0] for p in parts]
    n_parts = len(parts)
    n_prev = 0 if prev is None else 4

    def body(*refs):
        part_refs = refs[:n_parts]
        w_ref, m_ref, v_ref = refs[n_parts:n_parts + 3]
        g_ref, d_ref, nm_ref, nv_ref = refs[n_parts + 3 + n_prev:]
        g = None
        for p_ref, cnt in zip(part_refs, counts):
            s = p_ref[0].astype(F32)
            for k in range(1, cnt):
                s = s + p_ref[k].astype(F32)
            g = s if g is None else g + s
        m2 = ADAM_B1 * m_ref[0] + (1.0 - ADAM_B1) * g
        v2 = ADAM_B2 * v_ref[0] + (1.0 - ADAM_B2) * (g * g)
        m_hat = m2 / (1.0 - ADAM_B1 ** ADAM_STEP)
        v_hat = v2 / (1.0 - ADAM_B2 ** ADAM_STEP)
        g_ref[0] = g
        d_ref[0] = -ADAM_LR * (m_hat / (jnp.sqrt(v_hat) + ADAM_EPS) + ADAM_WD * w_ref[0])
        nm_ref[0] = m2
        nv_ref[0] = v2

    blk = pl.BlockSpec((1, tr, cols), lambda i: (layer, i, 0))
    shp = jax.ShapeDtypeStruct(w.shape, F32)
    return _pcall(
        body, name=name, grid=(rows // tr,),
        in_specs=[pl.BlockSpec((cnt, tr, cols), lambda i: (0, i, 0)) for cnt in counts] + [blk] * 3 + [ANY] * n_prev,
        out_specs=[blk] * 4, out_shape=[shp] * 4,
        input_output_aliases={n_parts + 3 + k: k for k in range(n_prev)},
        compiler_params=_params("parallel"),
    )(*parts, w, m, v, *(prev or ()))


def kernel(x, meta_tokens, pre_norm_g, post_norm_g, w_in, conv_w, conv_b, conv_ln_g, conv_ln_b, w_pw2, b_pw2, w_out, loss_target, m_meta_tokens, m_pre_norm_g, m_post_norm_g, m_w_in, m_conv_w, m_conv_b, m_conv_ln_g, m_conv_ln_b, m_w_pw2, m_b_pw2, m_w_out, v_meta_tokens, v_pre_norm_g, v_post_norm_g, v_w_in, v_conv_w, v_conv_b, v_conv_ln_g, v_conv_ln_b, v_w_pw2, v_b_pw2, v_w_out):
    seq, d = x.shape[1], x.shape[2]
    depth = w_in.shape[0]
    length = N_META + seq
    lp = -(-length // ATT_BLOCK) * ATT_BLOCK
    tap_pad = ((0, 0), (0, CONV_PAD - CONV_WIDTH), (0, 0))

    shards = (w_in.astype(BF16), w_pw2.astype(BF16), w_out.astype(BF16), jnp.pad(conv_w, tap_pad))
    dims = [(1, w_in.shape[2]), (0, w_pw2.shape[1]), (0, w_out.shape[1]), (1, conv_w.shape[2])]
    layer_shards = lambda l: [s[l] for s in shards]

    first = _run_exchange(_gather_plan(layer_shards(0) + [meta_tokens], dims + [(1, meta_tokens.shape[1])]),
                          "gather_weights")
    weights0, meta_f = tuple(first[:4]), first[4]

    h0 = jnp.concatenate([meta_f, x[0], jnp.zeros((lp - length, d), F32)], axis=0)
    target_p = jnp.pad(loss_target[0], ((N_META, lp - length), (0, 0)))
    vecs = (pre_norm_g, post_norm_g, conv_b, conv_ln_g, conv_ln_b, b_pw2)
    loss, dh0, vec_grads, pieces = _local_step(
        h0, target_p, seq, vecs, depth, weights0=weights0,
        gather_next=lambda l: _gather_plan(layer_shards(l + 1), dims),
        reduce_layer=lambda grads: _reduce_plan(list(grads), dims))

    def update(k, w, m, v, name):
        outs = None
        for l in reversed(range(depth)):
            outs = _adamw([pieces[l][k], pieces[l][4 + k]], w, m, v, l, outs, name)
        return outs

    up_w_in = update(0, w_in, m_w_in, v_w_in, "adamw_w_in")
    up_w_pw2 = update(1, w_pw2, m_w_pw2, v_w_pw2, "adamw_w_pw2")
    up_w_out = update(2, w_out, m_w_out, v_w_out, "adamw_w_out")
    up_conv_w = [o[:, :CONV_WIDTH] for o in update(3, jnp.pad(conv_w, tap_pad), jnp.pad(m_conv_w, tap_pad),
                                                   jnp.pad(v_conv_w, tap_pad, constant_values=1.0), "adamw_conv_w")]

    two = lambda a: a.reshape(-1, d)
    vec_rows = [two(g) for g in vec_grads]
    n_vec = sum(a.shape[0] for a in vec_rows)
    pack = jnp.concatenate(vec_rows + [dh0[:N_META], jnp.full((8, d), loss, F32)], axis=0)
    pack = jnp.pad(pack, ((0, -pack.shape[0] % 8), (0, 0)))
    tot = _allsum_small(pack)
    loss_all = tot[n_vec + N_META, 0]

    cat = lambda arrs: jnp.concatenate([two(t) for t in arrs], axis=0)[None]
    small_m = (m_pre_norm_g, m_post_norm_g, m_conv_b, m_conv_ln_g, m_conv_ln_b, m_b_pw2)
    small_v = (v_pre_norm_g, v_post_norm_g, v_conv_b, v_conv_ln_g, v_conv_ln_b, v_b_pw2)
    up_small = _adamw([tot[None, :n_vec]], cat(vecs), cat(small_m), cat(small_v), 0, None, "adamw_vectors")

    def unpack(o):
        res, r0 = [], 0
        for t in vecs:
            nrow = t.size // d
            res.append(o[0, r0:r0 + nrow].reshape(t.shape))
            r0 += nrow
        return res

    up_small = [unpack(o) for o in up_small]
    chip = 2 * lax.axis_index("x") + lax.axis_index("y")
    mcols = meta_tokens.shape[1]
    g_meta = lax.dynamic_slice_in_dim(tot[n_vec:n_vec + N_META], chip * mcols, mcols, axis=1)
    up_meta = [o[0] for o in _adamw([g_meta[None]], meta_tokens[None], m_meta_tokens[None], v_meta_tokens[None],
                                    0, None, "adamw_meta")]

    grad_x = dh0[N_META:length][None]
    outs = [loss_all, grad_x]
    for j in range(4):
        pre, post, cb, lg, lb, bp = up_small[j]
        outs += [up_meta[j], pre, post, up_w_in[j], up_conv_w[j], cb, lg, lb, up_w_pw2[j], bp, up_w_out[j]]
    return tuple(outs)
```

```python
from typing import Callable, NamedTuple

import jax
import jax.numpy as jnp
from jax import lax
from jax.experimental import pallas as pl
from jax.experimental.pallas import tpu as pltpu

F32 = jnp.float32
BF16 = jnp.bfloat16

N_META = 16
D_CONV = 512
D_SB = 512
HEAD_DIM = 64
CONV_WIDTH = 31
CONV_PAD = 32
CONV_ROWS = 128
RMS_EPS = 1e-6
LN_EPS = 1e-5
Q_SCALE = HEAD_DIM ** -0.5

ADAM_LR = 0.001
ADAM_B1 = 0.9
ADAM_B2 = 0.999
ADAM_EPS = 1e-08
ADAM_WD = 0.01
ADAM_STEP = 10

LANES = 128
ROW_TILE = 256
ATT_BLOCK = 256
ATT_PAIRS = 2
ATT_PAIRS_FWD = 4
VMEM_LIMIT = 56 * 1024 * 1024
EXP_ZERO = -104.0
COUNT_LANE = LANES - 1
RELAY_AT = 0.75

MESH = pl.DeviceIdType.MESH


def _pcall(body, **kw):
    return pl.pallas_call(body, **kw)


def _params(*sem):
    return pltpu.CompilerParams(dimension_semantics=sem, vmem_limit_bytes=VMEM_LIMIT)


def _sigmoid(x):
    return 1.0 / (1.0 + jnp.exp(-x))


def _silu_fwd_bwd(x):
    s = _sigmoid(x)
    return x * s, s * (1.0 + x * (1.0 - s))


def _nt(a, b):
    return lax.dot_general(a, b, (((1,), (1,)), ((), ())), preferred_element_type=F32)


def _tn(a, b):
    return lax.dot_general(a, b, (((0,), (0,)), ((), ())), preferred_element_type=F32)


def _nn(a, b):
    return jnp.dot(a, b, preferred_element_type=F32)


def _inproj(h, g, w):
    lp, d = h.shape
    n = w.shape[1]

    def body(h_ref, g_ref, w_ref, ew_ref, qkv_ref):
        x = h_ref[...]
        rstd = lax.rsqrt(jnp.mean(x * x, axis=-1, keepdims=True) + RMS_EPS)
        u = ((x * rstd) * g_ref[...]).astype(BF16)
        p = _nn(u, w_ref[...])
        ew_ref[:, 0:1536] = p[:, 0:1536]
        ew_ref[:, 1536:2048] = p[:, 3072:3584]
        qkv_ref[:, 0:512] = (p[:, 1536:2048] * Q_SCALE).astype(BF16)
        qkv_ref[:, 512:1536] = p[:, 2048:3072].astype(BF16)

    return _pcall(
        body, name="inproj_fwd", grid=(lp // ROW_TILE,),
        in_specs=[pl.BlockSpec((ROW_TILE, d), lambda i: (i, 0)),
                  pl.BlockSpec((1, d), lambda i: (0, 0)),
                  pl.BlockSpec((d, n), lambda i: (0, 0))],
        out_specs=[pl.BlockSpec((ROW_TILE, 2048), lambda i: (i, 0)),
                   pl.BlockSpec((ROW_TILE, 1536), lambda i: (i, 0))],
        out_shape=[jax.ShapeDtypeStruct((lp, 2048), F32), jax.ShapeDtypeStruct((lp, 1536), BF16)],
        compiler_params=_params("parallel"),
    )(h, g, w)


def _layer_norm_stats(c1):
    mu = jnp.mean(c1, axis=-1, keepdims=True)
    xc = c1 - mu
    var = jnp.mean(xc * xc, axis=-1, keepdims=True)
    rstd = lax.rsqrt(var + LN_EPS)
    return xc * rstd, rstd


def _shifted_copies(window, cols, shifted, tm):
    shifted[0] = window[:, cols]
    rows = tm + CONV_PAD - 8
    for b in range(1, 8):
        shifted[b, 0:rows, :] = window[pl.ds(b, rows), cols]


def _shifted_rows(shifted, shift, tm):
    b = shift % 8
    return shifted[b, pl.ds(pl.multiple_of(shift - b, 8), tm), :]


def _weighted(cw8_ref, j, cols, rows):
    w8 = cw8_ref[pl.ds(pl.multiple_of(j * 8, 8), 8), cols]
    r = rows.shape[0]
    return (rows.reshape(r // 8, 8, LANES) * w8[None]).reshape(r, LANES)


def _conv_fwd(ew, cw, cb, lng, lnb, wpw2, bpw2):
    lp = ew.shape[0]
    tm = ROW_TILE

    def body(ew_ref, cw_ref, cb_ref, lng_ref, lnb_ref, w_ref, b_ref, c1_ref, c4_ref, c5_ref, xbuf, shifted):
        @pl.when(pl.program_id(0) == 0)
        def _():
            xbuf[0:CONV_PAD, :] = jnp.zeros((CONV_PAD, D_CONV), F32)

        ga = ew_ref[:, 0:512]
        gb = ew_ref[:, 512:1024]
        cg = ew_ref[:, 1024:1536]
        xbuf[CONV_PAD:CONV_PAD + tm, :] = ga * _sigmoid(gb)
        for blk in range(D_CONV // LANES):
            cs = slice(blk * LANES, (blk + 1) * LANES)
            _shifted_copies(xbuf, cs, shifted, tm)
            for r0 in range(0, tm, CONV_ROWS):
                acc = jnp.zeros((CONV_ROWS, LANES), F32) + cb_ref[:, cs]
                for j in range(CONV_WIDTH):
                    acc = acc + _weighted(cw_ref, j, cs, _shifted_rows(
                        shifted, r0 + CONV_PAD - (CONV_WIDTH - 1) + j, CONV_ROWS))
                c1_ref[r0:r0 + CONV_ROWS, cs] = acc
        xbuf[0:CONV_PAD, :] = xbuf[tm:tm + CONV_PAD, :]
        xhat, _ = _layer_norm_stats(c1_ref[...])
        c2 = xhat * lng_ref[...] + lnb_ref[...]
        c3 = c2 * _sigmoid(c2)
        c4 = _nn(c3.astype(BF16), w_ref[...]) + b_ref[...]
        c4_ref[...] = c4
        c5_ref[...] = (c4 * (cg * _sigmoid(cg))).astype(BF16)

    vec = pl.BlockSpec((1, D_CONV), lambda i: (0, 0))
    row = pl.BlockSpec((tm, D_CONV), lambda i: (i, 0))
    return _pcall(
        body, name="conv_fwd", grid=(lp // tm,),
        in_specs=[pl.BlockSpec((tm, 1536), lambda i: (i, 0)),
                  pl.BlockSpec((8 * CONV_PAD, D_CONV), lambda i: (0, 0)),
                  vec, vec, vec,
                  pl.BlockSpec((D_CONV, D_CONV), lambda i: (0, 0)),
                  vec],
        out_specs=[row, row, row],
        out_shape=[jax.ShapeDtypeStruct((lp, D_CONV), F32), jax.ShapeDtypeStruct((lp, D_CONV), F32),
                   jax.ShapeDtypeStruct((lp, D_CONV), BF16)],
        scratch_shapes=[pltpu.VMEM((tm + CONV_PAD, D_CONV), F32), pltpu.VMEM((8, tm + CONV_PAD, LANES), F32)],
        compiler_params=_params("arbitrary"),
    )(ew, cw, cb, lng, lnb, wpw2, bpw2)


def _split_matmul(x, m01):
    hi = x.astype(BF16)
    lo = (x - hi.astype(F32)).astype(BF16)
    t = _nn(jnp.concatenate([hi, lo], axis=0), m01)
    m = x.shape[0]
    return t[:m] + t[m:]


def _sb_block(q2, kj, z_mask, tri):
    z = _nt(q2, kj)
    ls = -(jnp.maximum(z, 0.0) + jnp.log(1.0 + jnp.exp(-jnp.abs(z))))
    if z_mask is not None:
        ls = jnp.where(z_mask, ls, 0.0)
    return z, ls, _split_matmul(ls, tri)


def _attn_masks():
    lane = lax.broadcasted_iota(jnp.int32, (1, LANES), 1)
    row = lax.broadcasted_iota(jnp.int32, (2 * ATT_BLOCK, ATT_BLOCK), 0)
    col = lax.broadcasted_iota(jnp.int32, (2 * ATT_BLOCK, ATT_BLOCK), 1)
    return lane < HEAD_DIM, col < (row & (ATT_BLOCK - 1))


def _stack_heads(x, first_head):
    zero = jnp.zeros_like(x)
    return jnp.concatenate([jnp.where(first_head, x, zero), jnp.where(first_head, zero, x)], axis=0)


def _unstack_heads(x2, first_head):
    rows = x2.shape[0] // 2
    return jnp.where(first_head, x2[:rows], x2[rows:])


def _hosted(plan, n_out, n_scratch):
    n_in = 0 if plan is None else len(plan.inputs)
    n_x = 0 if plan is None else len(plan.out_shapes)

    def split(rest):
        a, b, c = n_in + n_out, n_in + n_out + n_x, n_in + n_out + n_x + n_scratch
        return rest[:n_in], rest[n_in:a], rest[a:b], rest[b:c], rest[c:]

    if plan is None:
        return split, [], [], [], [], []
    return split, list(plan.inputs), [ANY] * n_in, [ANY] * n_x, list(plan.out_shapes), list(plan.scratch)


def _attn_fwd(qkv, tri, plan=None):
    lp = qkv.shape[0]
    bq = ATT_BLOCK
    ngrp = ATT_PAIRS_FWD
    nstep = D_SB // (LANES * ngrp)
    nq = lp // bq
    assert nq <= COUNT_LANE
    split, x_args, x_in_specs, x_out_specs, x_out_shapes, x_scratch = _hosted(plan, 2, 3)

    def body(q_ref, k_ref, v_ref, tri_ref, *rest):
        x_in, (o_ref, carry_ref), x_out, (c_s, acc_s, cm_s), x_sems = split(rest)
        i = pl.program_id(1)
        if plan is not None:
            @pl.when(jnp.logical_and(pl.program_id(0) == 0, i == 0))
            def _():
                plan.start(x_in, x_out, x_sems)

            @pl.when(jnp.logical_and(pl.program_id(0) == nstep - 1, i == int(RELAY_AT * nq)))
            def _():
                plan.relay(x_in, x_out, x_sems)

        first_head, vis = _attn_masks()
        lane = lax.broadcasted_iota(jnp.int32, (1, LANES), 1)
        cols = [slice(g * LANES, (g + 1) * LANES) for g in range(ngrp)]
        q2s = [_stack_heads(q_ref[:, cs], first_head) for cs in cols]
        tri_m = tri_ref[...]

        c_s[...] = jnp.zeros_like(c_s)
        acc_s[...] = jnp.zeros_like(acc_s)
        cm_s[...] = jnp.zeros_like(cm_s)

        def blocks(js, masks):
            offs = [pl.multiple_of(j * bq, bq) for j in js]
            work = [(g, b) for b in range(len(js)) for g in range(ngrp)]
            zs = {(g, b): _nt(q2s[g], k_ref[pl.ds(offs[b], bq), cols[g]]) for g, b in work}
            lss = {}
            for g, b in work:
                z = zs[g, b]
                ls = -(jnp.maximum(z, 0.0) + jnp.log(1.0 + jnp.exp(-jnp.abs(z))))
                lss[g, b] = ls if masks[b] is None else jnp.where(masks[b], ls, 0.0)
            tails = {gb: _split_matmul(lss[gb], tri_m) for gb in work}
            probs = {}
            carry = [c_s[g] for g in range(ngrp)]
            saved = [cm_s[g] for g in range(ngrp)]
            for g, b in work:
                a = jnp.exp(zs[g, b] + lss[g, b] + tails[g, b] + carry[g])
                probs[g, b] = (a if masks[b] is None else jnp.where(masks[b], a, 0.0)).astype(BF16)
                saved[g] = jnp.where(lane == js[b], carry[g], saved[g])
                carry[g] = carry[g] + tails[g, b][:, 0:1] + lss[g, b][:, 0:1]
            top = None
            for g in range(ngrp):
                c_s[g] = carry[g]
                cm_s[g] = saved[g]
                acc = acc_s[g]
                for b in range(len(js)):
                    acc = acc + _nn(probs[g, b], v_ref[pl.ds(offs[b], bq), cols[g]])
                acc_s[g] = acc
                top = carry[g] if top is None else jnp.maximum(top, carry[g])
            return jnp.max(top) > EXP_ZERO

        alive = lax.cond(i > 0, lambda: blocks([i, i - 1], [vis, None]), lambda: blocks([i], [vis]))
        rest = jnp.maximum(i - 1, 0)

        def pair(carry):
            t, _ = carry
            j = i - 2 - 2 * t
            return t + 1, blocks([j, j - 1], [None, None])

        trips, alive = lax.while_loop(lambda ca: jnp.logical_and(ca[0] < rest // 2, ca[1]), pair, (0, alive))
        last = jnp.logical_and(jnp.logical_and(rest % 2 == 1, trips == rest // 2), alive)

        @pl.when(last)
        def _():
            blocks([0], [None])

        n_done = (jnp.minimum(i + 1, 2) + 2 * trips + last.astype(jnp.int32)).astype(F32)
        for g in range(ngrp):
            cmat = jnp.where(lane == COUNT_LANE, n_done, cm_s[g])
            carry_ref[:, 2 * g * LANES:(2 * g + 1) * LANES] = cmat[:bq]
            carry_ref[:, (2 * g + 1) * LANES:(2 * g + 2) * LANES] = cmat[bq:]
            o_ref[:, cols[g]] = _unstack_heads(acc_s[g], first_head)
        if plan is not None:
            @pl.when(jnp.logical_and(pl.program_id(0) == nstep - 1, i == nq - 1))
            def _():
                plan.finish(x_in, x_out, x_sems)

    width = ngrp * LANES
    outs = _pcall(
        body, name="attn_fwd" if plan is None else "attn_fwd_gather", grid=(nstep, nq),
        in_specs=[pl.BlockSpec((bq, width), lambda p, i: (i, p)),
                  pl.BlockSpec((lp, width), lambda p, i: (0, nstep + p)),
                  pl.BlockSpec((lp, width), lambda p, i: (0, 2 * nstep + p)),
                  pl.BlockSpec((bq, bq), lambda p, i: (0, 0))] + x_in_specs,
        out_specs=[pl.BlockSpec((bq, width), lambda p, i: (i, p)),
                   pl.BlockSpec((bq, 2 * width), lambda p, i: (i, p))] + x_out_specs,
        out_shape=[jax.ShapeDtypeStruct((lp, D_SB), F32), jax.ShapeDtypeStruct((lp, 2 * D_SB), F32)] + x_out_shapes,
        scratch_shapes=[pltpu.VMEM((ngrp, 2 * bq, 1), F32), pltpu.VMEM((ngrp, 2 * bq, LANES), F32),
                        pltpu.VMEM((ngrp, 2 * bq, LANES), F32)] + x_scratch,
        compiler_params=_params("arbitrary", "arbitrary"),
    )(qkv, qkv, qkv, tri, *x_args)
    return outs[0], outs[1], outs[2:]


def _outproj(c5, att, ew, h, w, g):
    lp, d = h.shape
    tm = ROW_TILE

    def body(c5_ref, att_ref, sg_ref, h_ref, w_ref, g_ref, hn_ref, cat_ref, mix_ref):
        sg = sg_ref[...]
        s = att_ref[...] * (sg * _sigmoid(sg))
        cat_ref[:, 0:D_CONV] = c5_ref[...]
        cat_ref[:, D_CONV:] = s.astype(BF16)
        mixed = _nn(cat_ref[...], w_ref[...])
        mix_ref[...] = mixed
        rstd = lax.rsqrt(jnp.mean(mixed * mixed, axis=-1, keepdims=True) + RMS_EPS)
        hn_ref[...] = h_ref[...] + (mixed * rstd) * g_ref[...]

    half = pl.BlockSpec((tm, 512), lambda i: (i, 0))
    full = pl.BlockSpec((tm, d), lambda i: (i, 0))
    return _pcall(
        body, name="outproj_fwd", grid=(lp // tm,),
        in_specs=[half, half, pl.BlockSpec((tm, 512), lambda i: (i, 3)), full,
                  pl.BlockSpec((d, d), lambda i: (0, 0)), pl.BlockSpec((1, d), lambda i: (0, 0))],
        out_specs=[full, full, full],
        out_shape=[jax.ShapeDtypeStruct((lp, d), F32), jax.ShapeDtypeStruct((lp, d), BF16),
                   jax.ShapeDtypeStruct((lp, d), F32)],
        compiler_params=_params("parallel"),
    )(c5, att, ew, h, w, g)


def _loss_head(h, target, seq):
    lp, d = h.shape
    tm = ROW_TILE

    def body(h_ref, t_ref, dh_ref, loss_ref):
        i = pl.program_id(0)

        @pl.when(i == 0)
        def _():
            loss_ref[...] = jnp.zeros_like(loss_ref)

        row = i * tm + lax.broadcasted_iota(jnp.int32, (tm, 1), 0)
        real = jnp.logical_and(row >= N_META, row < N_META + seq)
        diff = jnp.where(real, h_ref[...] - t_ref[...], 0.0)
        dh_ref[...] = diff * (1.0 / d)
        loss_ref[...] += 0.5 * jnp.sum(jnp.sum(diff * diff, axis=-1, keepdims=True) * (1.0 / d))

    full = pl.BlockSpec((tm, d), lambda i: (i, 0))
    return _pcall(
        body, name="loss_head", grid=(lp // tm,),
        in_specs=[full, full],
        out_specs=[full, pl.BlockSpec((8, LANES), lambda i: (0, 0))],
        out_shape=[jax.ShapeDtypeStruct((lp, d), F32), jax.ShapeDtypeStruct((8, LANES), F32)],
        compiler_params=_params("arbitrary"),
    )(h, target)


def _outproj_bwd(dh, mixed, g, w, att, ew, c4):
    lp, d = dh.shape
    tm = ROW_TILE

    def body(dh_ref, mix_ref, g_ref, w_ref, att_ref, cg_ref, sg_ref, c4_ref,
             dmix_ref, datt_ref, dsg_ref, dc4_ref, dcg_ref, dg_ref, db_ref):
        @pl.when(pl.program_id(0) == 0)
        def _():
            dg_ref[...] = jnp.zeros_like(dg_ref)
            db_ref[...] = jnp.zeros_like(db_ref)

        mixed = mix_ref[...]
        dhv = dh_ref[...]
        rstd = lax.rsqrt(jnp.mean(mixed * mixed, axis=-1, keepdims=True) + RMS_EPS)
        n = mixed * rstd
        dg_ref[...] += jnp.sum(dhv * n, axis=0, keepdims=True)
        dn = dhv * g_ref[...]
        dmix = (rstd * (dn - n * jnp.mean(dn * n, axis=-1, keepdims=True))).astype(BF16)
        dmix_ref[...] = dmix
        dcat = _nt(dmix, w_ref[...])
        dc5 = dcat[:, 0:D_CONV]
        ds = dcat[:, D_CONV:]
        silu_sg, dsilu_sg = _silu_fwd_bwd(sg_ref[...])
        datt_ref[...] = (ds * silu_sg).astype(BF16)
        dsg_ref[...] = (ds * att_ref[...] * dsilu_sg).astype(BF16)
        silu_cg, dsilu_cg = _silu_fwd_bwd(cg_ref[...])
        dc4 = dc5 * silu_cg
        db_ref[...] += jnp.sum(dc4, axis=0, keepdims=True)
        dc4_ref[...] = dc4.astype(BF16)
        dcg_ref[...] = (dc5 * c4_ref[...] * dsilu_cg).astype(BF16)

    half = pl.BlockSpec((tm, 512), lambda i: (i, 0))
    full = pl.BlockSpec((tm, d), lambda i: (i, 0))
    hb = jax.ShapeDtypeStruct((lp, 512), BF16)
    return _pcall(
        body, name="outproj_bwd", grid=(lp // tm,),
        in_specs=[full, full, pl.BlockSpec((1, d), lambda i: (0, 0)), pl.BlockSpec((d, d), lambda i: (0, 0)),
                  half, pl.BlockSpec((tm, 512), lambda i: (i, 2)), pl.BlockSpec((tm, 512), lambda i: (i, 3)), half],
        out_specs=[full, half, half, half, half,
                   pl.BlockSpec((1, d), lambda i: (0, 0)), pl.BlockSpec((1, 512), lambda i: (0, 0))],
        out_shape=[jax.ShapeDtypeStruct((lp, d), BF16), hb, hb, hb, hb,
                   jax.ShapeDtypeStruct((1, d), F32), jax.ShapeDtypeStruct((1, 512), F32)],
        compiler_params=_params("arbitrary"),
    )(dh, mixed, g, w, att, ew, ew, c4)


def _attn_bwd(qkv, carries, datt, tri, upper, plan=None):
    lp = qkv.shape[0]
    bq = ATT_BLOCK
    ngrp = ATT_PAIRS
    nstep = D_SB // (LANES * ngrp)
    nq = lp // bq
    split, x_args, x_in_specs, x_out_specs, x_out_shapes, x_scratch = _hosted(plan, 3, 2)

    def body(q_ref, k_ref, v_ref, carry_ref, do_ref, tri_ref, upper_ref, *rest):
        x_in, (dq_ref, dk_ref, dv_ref), x_out, (run_s, dq_s), x_sems = split(rest)
        i = pl.program_id(1)
        if plan is not None:
            @pl.when(jnp.logical_and(pl.program_id(0) == 0, i == 0))
            def _():
                plan.start(x_in, x_out, x_sems)

            @pl.when(jnp.logical_and(pl.program_id(0) == nstep - 1, i == int(RELAY_AT * nq)))
            def _():
                plan.relay(x_in, x_out, x_sems)

        @pl.when(i == 0)
        def _():
            dk_ref[...] = jnp.zeros_like(dk_ref)
            dv_ref[...] = jnp.zeros_like(dv_ref)

        first_head, vis = _attn_masks()
        lane = lax.broadcasted_iota(jnp.int32, (1, LANES), 1)
        cols = [slice(g * LANES, (g + 1) * LANES) for g in range(ngrp)]
        q2s = [_stack_heads(q_ref[:, cs], first_head) for cs in cols]
        do2s = [_stack_heads(do_ref[:, cs], first_head) for cs in cols]
        cmats = [jnp.concatenate([carry_ref[:, 2 * g * LANES:(2 * g + 1) * LANES],
                                  carry_ref[:, (2 * g + 1) * LANES:(2 * g + 2) * LANES]], axis=0)
                 for g in range(ngrp)]
        tri_m = tri_ref[...]
        upper_m = upper_ref[...]

        def blocks(js, masks):
            offs = [pl.multiple_of(j * bq, bq) for j in js]
            work = [(g, b) for b in range(len(js)) for g in range(ngrp)]
            zs = {(g, b): _nt(q2s[g], k_ref[pl.ds(offs[b], bq), cols[g]]) for g, b in work}
            lss = {}
            for g, b in work:
                z = zs[g, b]
                ls = -(jnp.maximum(z, 0.0) + jnp.log(1.0 + jnp.exp(-jnp.abs(z))))
                lss[g, b] = ls if masks[b] is None else jnp.where(masks[b], ls, 0.0)
            tails = {gb: _split_matmul(lss[gb], tri_m) for gb in work}
            das = {(g, b): _nt(do2s[g], v_ref[pl.ds(offs[b], bq), cols[g]]) for g, b in work}
            probs, des = {}, {}
            for g, b in work:
                c = jnp.sum(jnp.where(lane == js[b], cmats[g], 0.0), axis=-1, keepdims=True)
                a = jnp.exp(zs[g, b] + lss[g, b] + tails[g, b] + c)
                a = a if masks[b] is None else jnp.where(masks[b], a, 0.0)
                probs[g, b] = a.astype(BF16)
                des[g, b] = das[g, b] * a
            prefixes = {gb: _split_matmul(des[gb], upper_m) for gb in work}
            runs = [run_s[g] for g in range(ngrp)]
            dzs = {}
            for g, b in work:
                beta = jnp.exp(zs[g, b] + lss[g, b])
                dz = des[g, b] - beta * (des[g, b] + runs[g] + prefixes[g, b])
                dzs[g, b] = (dz if masks[b] is None else jnp.where(masks[b], dz, 0.0)).astype(BF16)
                runs[g] = runs[g] + prefixes[g, b][:, bq - 1:bq] + des[g, b][:, bq - 1:bq]
            for g in range(ngrp):
                run_s[g] = runs[g]
                dq = dq_s[g]
                for b in range(len(js)):
                    rows = pl.ds(offs[b], bq)
                    dq = dq + _nn(dzs[g, b], k_ref[rows, cols[g]])
                    dk_ref[rows, cols[g]] += _tn(dzs[g, b], q2s[g])
                    dv_ref[rows, cols[g]] += _tn(probs[g, b], do2s[g])
                dq_s[g] = dq

        n_done = jnp.max(carry_ref[:, COUNT_LANE:COUNT_LANE + 1]).astype(jnp.int32)
        n_done = jnp.clip(n_done, 1, i + 1)
        before = jnp.maximum(n_done - 2, 0)
        j0 = i - n_done + 1
        odd = before % 2
        run_s[...] = jnp.zeros_like(run_s)
        dq_s[...] = jnp.zeros_like(dq_s)

        @pl.when(odd == 1)
        def _():
            blocks([j0], [None])

        @pl.loop(0, before // 2)
        def _(t):
            blocks([j0 + odd + 2 * t, j0 + odd + 2 * t + 1], [None, None])

        @pl.when(n_done > 1)
        def _():
            blocks([i - 1, i], [None, vis])

        @pl.when(n_done <= 1)
        def _():
            blocks([i], [vis])

        for g in range(ngrp):
            dq_ref[:, cols[g]] = (_unstack_heads(dq_s[g], first_head) * Q_SCALE).astype(BF16)
        if plan is not None:
            @pl.when(jnp.logical_and(pl.program_id(0) == nstep - 1, i == nq - 1))
            def _():
                plan.finish(x_in, x_out, x_sems)

    width = ngrp * LANES
    qb = pl.BlockSpec((bq, width), lambda p, i: (i, p))
    colb = pl.BlockSpec((lp, width), lambda p, i: (0, p))
    sq = pl.BlockSpec((bq, bq), lambda p, i: (0, 0))
    outs = _pcall(
        body, name="attn_bwd" if plan is None else "attn_bwd_reduce", grid=(nstep, nq),
        in_specs=[qb,
                  pl.BlockSpec((lp, width), lambda p, i: (0, nstep + p)),
                  pl.BlockSpec((lp, width), lambda p, i: (0, 2 * nstep + p)),
                  pl.BlockSpec((bq, 2 * width), lambda p, i: (i, p)), qb, sq, sq] + x_in_specs,
        out_specs=[qb, colb, colb] + x_out_specs,
        out_shape=[jax.ShapeDtypeStruct((lp, D_SB), BF16), jax.ShapeDtypeStruct((lp, D_SB), F32),
                   jax.ShapeDtypeStruct((lp, D_SB), F32)] + x_out_shapes,
        scratch_shapes=[pltpu.VMEM((ngrp, 2 * bq, 1), F32), pltpu.VMEM((ngrp, 2 * bq, LANES), F32)] + x_scratch,
        compiler_params=_params("arbitrary", "arbitrary"),
    )(qkv, qkv, qkv, carries, datt, tri, upper, *x_args)
    return outs[0], outs[1], outs[2], outs[3:]


def _conv_bwd(dc4, c1, ew, cw, lng, lnb, wpw2):
    lp = ew.shape[0]
    tm = ROW_TILE
    nt = lp // tm
    halo_per_tile = tm // CONV_PAD

    def body(dc4_ref, c1_ref, ew_ref, halo_ref, cw_ref, lng_ref, lnb_ref, w_ref,
             dga_ref, dgb_ref, c3_ref, dcw_ref, dcb_ref, dlng_ref, dlnb_ref, xbuf, dbuf, shifted, wacc):
        step = pl.program_id(0)

        @pl.when(step == 0)
        def _():
            wacc[...] = jnp.zeros_like(wacc)
            dcb_ref[...] = jnp.zeros_like(dcb_ref)
            dlng_ref[...] = jnp.zeros_like(dlng_ref)
            dlnb_ref[...] = jnp.zeros_like(dlnb_ref)
            dbuf[tm:tm + CONV_PAD, :] = jnp.zeros((CONV_PAD, D_CONV), F32)

        dc3 = _nt(dc4_ref[...], w_ref[...])
        xhat, rstd = _layer_norm_stats(c1_ref[...])
        c2 = xhat * lng_ref[...] + lnb_ref[...]
        c3, dsilu = _silu_fwd_bwd(c2)
        c3_ref[...] = c3.astype(BF16)
        dc2 = dc3 * dsilu
        dlng_ref[...] += jnp.sum(dc2 * xhat, axis=0, keepdims=True)
        dlnb_ref[...] += jnp.sum(dc2, axis=0, keepdims=True)
        dxhat = dc2 * lng_ref[...]
        dc1 = rstd * (dxhat - jnp.mean(dxhat, axis=-1, keepdims=True)
                      - xhat * jnp.mean(dxhat * xhat, axis=-1, keepdims=True))
        dcb_ref[...] += jnp.sum(dc1, axis=0, keepdims=True)
        dbuf[0:tm, :] = dc1

        ga = ew_ref[:, 0:512]
        sgb = _sigmoid(ew_ref[:, 512:1024])
        xbuf[CONV_PAD:CONV_PAD + tm, :] = ga * sgb
        first_tile = step == nt - 1
        halo = halo_ref[:, 0:512] * _sigmoid(halo_ref[:, 512:1024])
        xbuf[0:CONV_PAD, :] = jnp.where(first_tile, 0.0, halo)

        for cb in range(D_CONV // LANES):
            cs = slice(cb * LANES, (cb + 1) * LANES)
            _shifted_copies(dbuf, cs, shifted, tm)
            for r0 in range(0, tm, CONV_ROWS):
                rs = slice(r0, r0 + CONV_ROWS)
                dc0 = jnp.zeros((CONV_ROWS, LANES), F32)
                for j in range(CONV_WIDTH):
                    dc0 = dc0 + _weighted(cw_ref, j, cs, _shifted_rows(
                        shifted, r0 + CONV_WIDTH - 1 - j, CONV_ROWS))
                dga_ref[rs, cs] = (dc0 * sgb[rs, cs]).astype(BF16)
                dgb_ref[rs, cs] = (dc0 * ga[rs, cs] * sgb[rs, cs] * (1.0 - sgb[rs, cs])).astype(BF16)
            _shifted_copies(xbuf, cs, shifted, tm)
            for r0 in range(0, tm, CONV_ROWS):
                d1 = dbuf[r0:r0 + CONV_ROWS, cs]

                for j in range(CONV_WIDTH):
                    prod = d1 * _shifted_rows(shifted, r0 + CONV_PAD - (CONV_WIDTH - 1) + j, CONV_ROWS)
                    wacc[j * 8:(j + 1) * 8, cs] += jnp.sum(prod.reshape(CONV_ROWS // 8, 8, LANES), axis=0)
        dbuf[tm:tm + CONV_PAD, :] = dbuf[0:CONV_PAD, :]

        @pl.when(step == nt - 1)
        def _():
            dcw_ref[...] = jnp.sum(wacc[...].reshape(CONV_PAD, 8, D_CONV), axis=1)

    rev = lambda i: (nt - 1 - i, 0)
    row = pl.BlockSpec((tm, D_CONV), rev)
    vec = pl.BlockSpec((1, D_CONV), lambda i: (0, 0))
    hb = jax.ShapeDtypeStruct((lp, D_CONV), BF16)
    vs = jax.ShapeDtypeStruct((1, D_CONV), F32)
    return _pcall(
        body, name="conv_bwd", grid=(nt,),
        in_specs=[row, row, pl.BlockSpec((tm, 1024), rev),
                  pl.BlockSpec((CONV_PAD, 1024), lambda i: (jnp.maximum((nt - 1 - i) * halo_per_tile - 1, 0), 0)),
                  pl.BlockSpec((8 * CONV_PAD, D_CONV), lambda i: (0, 0)), vec, vec,
                  pl.BlockSpec((D_CONV, D_CONV), lambda i: (0, 0))],
        out_specs=[row, row, row, pl.BlockSpec((CONV_PAD, D_CONV), lambda i: (0, 0)), vec, vec, vec],
        out_shape=[hb, hb, hb, jax.ShapeDtypeStruct((CONV_PAD, D_CONV), F32), vs, vs, vs],
        scratch_shapes=[pltpu.VMEM((tm + CONV_PAD, D_CONV), F32), pltpu.VMEM((tm + CONV_PAD, D_CONV), F32),
                        pltpu.VMEM((8, tm + CONV_PAD, LANES), F32), pltpu.VMEM((8 * CONV_PAD, D_CONV), F32)],
        compiler_params=_params("arbitrary"),
    )(dc4, c1, ew, ew, cw, lng, lnb, wpw2)


def _inproj_bwd(dga, dgb, dcg, dq, dk, dv, dsg, h, g, w, dh_out):
    lp, d = h.shape
    n = w.shape[1]
    tm = ROW_TILE

    def body(dga_ref, dgb_ref, dcg_ref, dq_ref, dk_ref, dv_ref, dsg_ref, h_ref, g_ref, w_ref, dho_ref,
             dh_ref, dproj_ref, u_ref, dg_ref):
        @pl.when(pl.program_id(0) == 0)
        def _():
            dg_ref[...] = jnp.zeros_like(dg_ref)

        dproj_ref[:, 0:512] = dga_ref[...]
        dproj_ref[:, 512:1024] = dgb_ref[...]
        dproj_ref[:, 1024:1536] = dcg_ref[...]
        dproj_ref[:, 1536:2048] = dq_ref[...]
        dproj_ref[:, 2048:2560] = dk_ref[...].astype(BF16)
        dproj_ref[:, 2560:3072] = dv_ref[...].astype(BF16)
        dproj_ref[:, 3072:3584] = dsg_ref[...]
        du = _nt(dproj_ref[...], w_ref[...])
        x = h_ref[...]
        rstd = lax.rsqrt(jnp.mean(x * x, axis=-1, keepdims=True) + RMS_EPS)
        nrm = x * rstd
        u_ref[...] = (nrm * g_ref[...]).astype(BF16)
        dg_ref[...] += jnp.sum(du * nrm, axis=0, keepdims=True)
        dn = du * g_ref[...]
        dh_ref[...] = dho_ref[...] + rstd * (dn - nrm * jnp.mean(dn * nrm, axis=-1, keepdims=True))

    half = pl.BlockSpec((tm, 512), lambda i: (i, 0))
    full = pl.BlockSpec((tm, d), lambda i: (i, 0))
    return _pcall(
        body, name="inproj_bwd", grid=(lp // tm,),
        in_specs=[half] * 7 + [full, pl.BlockSpec((1, d), lambda i: (0, 0)),
                               pl.BlockSpec((d, n), lambda i: (0, 0)), full],
        out_specs=[full, pl.BlockSpec((tm, n), lambda i: (i, 0)), full, pl.BlockSpec((1, d), lambda i: (0, 0))],
        out_shape=[jax.ShapeDtypeStruct((lp, d), F32), jax.ShapeDtypeStruct((lp, n), BF16),
                   jax.ShapeDtypeStruct((lp, d), BF16), jax.ShapeDtypeStruct((1, d), F32)],
        compiler_params=_params("arbitrary"),
    )(dga, dgb, dcg, dq, dk, dv, dsg, h, g, w, dh_out)


def _row_split(m, parts):
    tm = m // parts
    assert tm * parts == m and tm % 16 == 0, (m, parts)
    return tm


def _matmul_tn(x, dy, tn, name):
    m, k = x.shape
    n = dy.shape[1]
    steps = 4 if m % 64 == 0 else 1
    tm = _row_split(m, steps)

    def body(x_ref, dy_ref, o_ref, acc_ref):
        r = pl.program_id(1)

        @pl.when(r == 0)
        def _():
            acc_ref[...] = jnp.zeros_like(acc_ref)

        acc_ref[...] += _tn(x_ref[...], dy_ref[...])

        @pl.when(r == steps - 1)
        def _():
            o_ref[...] = acc_ref[...].astype(BF16)

    return _pcall(
        body, name=name, grid=(n // tn, steps),
        in_specs=[pl.BlockSpec((tm, k), lambda j, r: (r, 0)), pl.BlockSpec((tm, tn), lambda j, r: (r, j))],
        out_specs=pl.BlockSpec((k, tn), lambda j, r: (0, j)),
        out_shape=jax.ShapeDtypeStruct((k, n), BF16),
        scratch_shapes=[pltpu.VMEM((k, tn), F32)],
        compiler_params=_params("parallel", "arbitrary"),
    )(x, dy)


def _local_step(h0, target_p, seq, vecs, depth, all_weights=None, weights0=None, gather_next=None,
                reduce_layer=None):
    pre_g, post_g, conv_b, ln_g, ln_b, b_pw2 = vecs
    ar = jnp.arange(ATT_BLOCK)
    tri = (ar[:, None] > ar[None, :]).astype(BF16)
    upper = (ar[:, None] < ar[None, :]).astype(BF16)
    row = lambda a, l: a[l][None, :]

    weights = list(all_weights) if all_weights is not None else [weights0] + [None] * (depth - 1)
    saved = []
    h = h0
    for l in range(depth):
        w_in, w_pw2, w_out, conv_w = weights[l]
        ew, qkv = _inproj(h, row(pre_g, l), w_in)
        conv_w = jnp.repeat(conv_w, 8, axis=0)
        weights[l] = (w_in, w_pw2, w_out, conv_w)
        c1, c4, c5 = _conv_fwd(ew, conv_w, row(conv_b, l), row(ln_g, l), row(ln_b, l), w_pw2, row(b_pw2, l))
        plan = gather_next(l) if gather_next is not None and l + 1 < depth else None
        att, carries, gathered = _attn_fwd(qkv, tri, plan)
        if plan is not None:
            weights[l + 1] = tuple(gathered)
        hn, cat, mixed = _outproj(c5, att, ew, h, w_out, row(post_g, l))
        saved.append((h, ew, qkv, c1, c4, att, carries, cat, mixed))
        h = hn

    dh, loss = _loss_head(h, target_p, seq)

    vec_grads = [None] * depth
    mat_grads = [None] * depth
    pending = None
    for l in reversed(range(depth)):
        w_in, w_pw2, w_out, conv_w = weights[l]
        h_in, ew, qkv, c1, c4, att, carries, cat, mixed = saved[l]
        dmix, datt, dsg, dc4, dcg, dpost, dbpw2 = _outproj_bwd(dh, mixed, row(post_g, l), w_out, att, ew, c4)
        dw_out = _matmul_tn(cat, dmix, 512, "dw_out")
        dq, dk, dv, landed = _attn_bwd(qkv, carries, datt, tri, upper, pending)
        if pending is not None:
            mat_grads[l + 1] = landed
        dga, dgb, c3, dcw, dcb, dlng, dlnb = _conv_bwd(dc4, c1, ew, conv_w, row(ln_g, l), row(ln_b, l), w_pw2)
        dw_pw2 = _matmul_tn(c3, dc4, 512, "dw_pw2")
        dh, dproj, u, dpre = _inproj_bwd(dga, dgb, dcg, dq, dk, dv, dsg, h_in, row(pre_g, l), w_in, dh)
        dw_in = _matmul_tn(u, dproj, 1792, "dw_in")
        vec_grads[l] = (dpre[0], dpost[0], dcb[0], dlng[0], dlnb[0], dbpw2[0])
        mats = (dw_in, dw_pw2, dw_out, dcw)
        if reduce_layer is None:
            mat_grads[l] = mats
        else:
            pending = reduce_layer(mats)
    if pending is not None:
        mat_grads[0] = _run_exchange(pending, "reduce_grads")

    vec_grads = [jnp.stack([g[k] for g in vec_grads]) for k in range(len(vecs))]
    return loss[0, 0], dh, vec_grads, mat_grads


N_CHIPS = 4
ANY = pl.BlockSpec(memory_space=pl.ANY)


def _chip_peers():
    x, y, c = lax.axis_index("x"), lax.axis_index("y"), lax.axis_index("c")
    return x, y, c, [(x, 1 - y), (1 - x, y), (1 - x, 1 - y)]


def _shard_slices(refs, dims, idx):
    out = []
    for ref, (axis, size) in zip(refs, dims):
        assert size % LANES == 0
        start = pl.multiple_of(idx * size, LANES)
        sl = [slice(None)] * len(ref.shape)
        sl[axis] = pl.ds(start, size)
        out.append(ref.at[tuple(sl)])
    return out


class _Exchange(NamedTuple):
    inputs: list
    out_shapes: list
    scratch: list
    start: Callable
    relay: Callable
    finish: Callable


def _run_exchange(plan, name):
    n_in, n_out = len(plan.inputs), len(plan.out_shapes)

    def body(*refs):
        parts = refs[:n_in], refs[n_in:n_in + n_out], refs[n_in + n_out:]
        plan.start(*parts)
        plan.relay(*parts)
        plan.finish(*parts)

    return _pcall(body, name=name, in_specs=[ANY] * n_in, out_specs=[ANY] * n_out, out_shape=plan.out_shapes,
                  scratch_shapes=plan.scratch)(*plan.inputs)


def _gather_plan(shards, dims):
    n = len(shards)
    full_shapes = []
    halves = []
    for s, (axis, size) in zip(shards, dims):
        shp = list(s.shape)
        shp[axis] = size * N_CHIPS
        full_shapes.append(jax.ShapeDtypeStruct(tuple(shp), s.dtype))
        tile_rows = 32 // s.dtype.itemsize
        assert s.shape[0] % (2 * tile_rows) == 0
        halves.append((s.shape[0] // 2, tile_rows))

    def half(refs, which):
        return [r.at[pl.ds(pl.multiple_of(which * h, t), h)] for r, (h, t) in zip(refs, halves)]

    def copies(srcs, outs, sems):
        send, recv, loc = sems
        x, y, c, peers = _chip_peers()
        sibling = (x, y, 1 - c)
        mine = _shard_slices(outs, dims, 2 * x + y)
        local = [pltpu.make_async_copy(s, d, loc.at[a]) for a, (s, d) in enumerate(zip(srcs, mine))]

        def remote(src, dst, slot, a, dev):
            return pltpu.make_async_remote_copy(src, dst, send.at[slot, a], recv.at[slot, a],
                                                device_id=dev, device_id_type=MESH)

        sends = [remote(s, d, k, a, (px, py, c))
                 for k, (px, py) in enumerate(peers) for a, (s, d) in enumerate(zip(half(srcs, c), half(mine, c)))]
        theirs = [_shard_slices(outs, dims, 2 * px + py) for px, py in peers]
        arrivals = [remote(s, d, k, a, (px, py, c))
                    for k, (px, py) in enumerate(peers)
                    for a, (s, d) in enumerate(zip(half(srcs, c), half(theirs[k], c)))]
        passed_on = [remote(d, d, 3 + k, a, sibling) for k in range(3) for a, d in enumerate(half(theirs[k], c))]
        from_sibling = [remote(d, d, 3 + k, a, sibling)
                        for k in range(3) for a, d in enumerate(half(theirs[k], 1 - c))]
        return local, sends, arrivals, passed_on, from_sibling

    def start(srcs, outs, sems):
        local, sends = copies(srcs, outs, sems)[:2]
        for cp in local + sends:
            cp.start()

    def relay(srcs, outs, sems):
        _, _, arrivals, passed_on, _ = copies(srcs, outs, sems)
        for arrived, onward in zip(arrivals, passed_on):
            arrived.wait_recv()
            onward.start()

    def finish(srcs, outs, sems):
        local, sends, _, passed_on, from_sibling = copies(srcs, outs, sems)
        for cp in from_sibling:
            cp.wait_recv()
        for cp in sends + passed_on:
            cp.wait_send()
        for cp in local:
            cp.wait()

    scratch = [pltpu.SemaphoreType.DMA((6, n)), pltpu.SemaphoreType.DMA((6, n)), pltpu.SemaphoreType.DMA((n,))]
    return _Exchange(list(shards), full_shapes, scratch, start, relay, finish)


def _reduce_plan(grads, dims):
    n = len(grads)
    piece_shapes = []
    for g, (axis, size) in zip(grads, dims):
        shp = list(g.shape)
        shp[axis] = size
        piece_shapes.append(jax.ShapeDtypeStruct((N_CHIPS,) + tuple(shp), g.dtype))

    def copies(srcs, outs, sems):
        mine, theirs = outs[:n], outs[n:]
        send, recv, loc = sems
        x, y, c, peers = _chip_peers()
        sibling = (x, y, 1 - c)
        own = _shard_slices(srcs, dims, 2 * x + y)

        def remote(src, dst, slot, a, dev):
            return pltpu.make_async_remote_copy(src, dst, send.at[slot, a], recv.at[slot, a],
                                                device_id=dev, device_id_type=MESH)

        local = [pltpu.make_async_copy(own[a], mine[a].at[3], loc.at[a]) for a in range(n)]
        to_sibling = [remote(own[a], theirs[a].at[3], 3, a, sibling) for a in range(n)]
        to_chips = [remote(src, mine[a].at[k], k, a, (px, py, c))
                    for k, (px, py) in enumerate(peers)
                    for a, src in enumerate(_shard_slices(srcs, dims, 2 * px + py))]
        passed_on = [remote(mine[a].at[k], theirs[a].at[k], 4 + k, a, sibling) for k in range(3) for a in range(n)]
        return local, to_sibling, to_chips, passed_on

    def start(srcs, outs, sems):
        local, to_sibling, to_chips, _ = copies(srcs, outs, sems)
        for cp in local + to_sibling + to_chips:
            cp.start()

    def relay(srcs, outs, sems):
        _, _, to_chips, passed_on = copies(srcs, outs, sems)
        for arrived, onward in zip(to_chips, passed_on):
            arrived.wait_recv()
            onward.start()

    def finish(srcs, outs, sems):
        local, to_sibling, to_chips, passed_on = copies(srcs, outs, sems)
        for cp in to_sibling + passed_on:
            cp.wait_recv()
        for cp in to_sibling + to_chips + passed_on:
            cp.wait_send()
        for cp in local:
            cp.wait()

    scratch = [pltpu.SemaphoreType.DMA((7, n)), pltpu.SemaphoreType.DMA((7, n)), pltpu.SemaphoreType.DMA((n,))]
    return _Exchange(list(grads), piece_shapes * 2, scratch, start, relay, finish)


def _allsum_small(pack):
    rows, cols = pack.shape
    ndev = 8

    def body(p_ref, o_ref, buf, send, recv):
        x, y, c = lax.axis_index("x"), lax.axis_index("y"), lax.axis_index("c")
        me = 4 * x + 2 * y + c
        buf[me] = p_ref[...]
        started = []
        for r in range(1, ndev):
            bx, by, bc = (r >> 2) & 1, (r >> 1) & 1, r & 1
            dev = (x ^ bx, y ^ by, c ^ bc)
            cp = pltpu.make_async_remote_copy(p_ref, buf.at[me], send.at[r], recv.at[r],
                                              device_id=dev, device_id_type=MESH)
            cp.start()
            started.append(cp)
        for r in range(1, ndev):
            pltpu.make_async_remote_copy(p_ref, buf.at[me ^ r], send.at[r], recv.at[r],
                                         device_id=(x, y, c), device_id_type=MESH).wait_recv()
        for cp in started:
            cp.wait_send()
        acc = buf[0]
        for d in range(1, ndev):
            acc = acc + buf[d]
        o_ref[...] = acc

    vm = pl.BlockSpec(memory_space=pltpu.VMEM)
    return _pcall(
        body, name="allsum_small", in_specs=[vm], out_specs=vm,
        out_shape=jax.ShapeDtypeStruct((rows, cols), F32),
        scratch_shapes=[pltpu.VMEM((ndev, rows, cols), F32), pltpu.SemaphoreType.DMA((ndev,)),
                        pltpu.SemaphoreType.DMA((ndev,))],
    )(pack)


def _adamw(parts, w, m, v, layer, prev, name):
    _, rows, cols = w.shape
    tr = ROW_TILE if rows % ROW_TILE == 0 else rows
    counts = [p.shape[0] for p in parts]
    n_parts = len(parts)
    n_prev = 0 if prev is None else 4

    def body(*refs):
        part_refs = refs[:n_parts]
        w_ref, m_ref, v_ref = refs[n_parts:n_parts + 3]
        g_ref, d_ref, nm_ref, nv_ref = refs[n_parts + 3 + n_prev:]
        g = None
        for p_ref, cnt in zip(part_refs, counts):
            s = p_ref[0].astype(F32)
            for k in range(1, cnt):
                s = s + p_ref[k].astype(F32)
            g = s if g is None else g + s
        m2 = ADAM_B1 * m_ref[0] + (1.0 - ADAM_B1) * g
        v2 = ADAM_B2 * v_ref[0] + (1.0 - ADAM_B2) * (g * g)
        m_hat = m2 / (1.0 - ADAM_B1 ** ADAM_STEP)
        v_hat = v2 / (1.0 - ADAM_B2 ** ADAM_STEP)
        g_ref[0] = g
        d_ref[0] = -ADAM_LR * (m_hat / (jnp.sqrt(v_hat) + ADAM_EPS) + ADAM_WD * w_ref[0])
        nm_ref[0] = m2
        nv_ref[0] = v2

    blk = pl.BlockSpec((1, tr, cols), lambda i: (layer, i, 0))
    shp = jax.ShapeDtypeStruct(w.shape, F32)
    return _pcall(
        body, name=name, grid=(rows // tr,),
        in_specs=[pl.BlockSpec((cnt, tr, cols), lambda i: (0, i, 0)) for cnt in counts] + [blk] * 3 + [ANY] * n_prev,
        out_specs=[blk] * 4, out_shape=[shp] * 4,
        input_output_aliases={n_parts + 3 + k: k for k in range(n_prev)},
        compiler_params=_params("parallel"),
    )(*parts, w, m, v, *(prev or ()))


def kernel(x, meta_tokens, pre_norm_g, post_norm_g, w_in, conv_w, conv_b, conv_ln_g, conv_ln_b, w_pw2, b_pw2, w_out, loss_target, m_meta_tokens, m_pre_norm_g, m_post_norm_g, m_w_in, m_conv_w, m_conv_b, m_conv_ln_g, m_conv_ln_b, m_w_pw2, m_b_pw2, m_w_out, v_meta_tokens, v_pre_norm_g, v_post_norm_g, v_w_in, v_conv_w, v_conv_b, v_conv_ln_g, v_conv_ln_b, v_w_pw2, v_b_pw2, v_w_out):
    seq, d = x.shape[1], x.shape[2]
    depth = w_in.shape[0]
    length = N_META + seq
    lp = -(-length // ATT_BLOCK) * ATT_BLOCK
    tap_pad = ((0, 0), (0, CONV_PAD - CONV_WIDTH), (0, 0))

    shards = (w_in.astype(BF16), w_pw2.astype(BF16), w_out.astype(BF16), jnp.pad(conv_w, tap_pad))
    dims = [(1, w_in.shape[2]), (0, w_pw2.shape[1]), (0, w_out.shape[1]), (1, conv_w.shape[2])]
    layer_shards = lambda l: [s[l] for s in shards]

    first = _run_exchange(_gather_plan(layer_shards(0) + [meta_tokens], dims + [(1, meta_tokens.shape[1])]),
                          "gather_weights")
    weights0, meta_f = tuple(first[:4]), first[4]

    h0 = jnp.concatenate([meta_f, x[0], jnp.zeros((lp - length, d), F32)], axis=0)
    target_p = jnp.pad(loss_target[0], ((N_META, lp - length), (0, 0)))
    vecs = (pre_norm_g, post_norm_g, conv_b, conv_ln_g, conv_ln_b, b_pw2)
    loss, dh0, vec_grads, pieces = _local_step(
        h0, target_p, seq, vecs, depth, weights0=weights0,
        gather_next=lambda l: _gather_plan(layer_shards(l + 1), dims),
        reduce_layer=lambda grads: _reduce_plan(list(grads), dims))

    def update(k, w, m, v, name):
        outs = None
        for l in reversed(range(depth)):
            outs = _adamw([pieces[l][k], pieces[l][4 + k]], w, m, v, l, outs, name)
        return outs

    up_w_in = update(0, w_in, m_w_in, v_w_in, "adamw_w_in")
    up_w_pw2 = update(1, w_pw2, m_w_pw2, v_w_pw2, "adamw_w_pw2")
    up_w_out = update(2, w_out, m_w_out, v_w_out, "adamw_w_out")
    up_conv_w = [o[:, :CONV_WIDTH] for o in update(3, jnp.pad(conv_w, tap_pad), jnp.pad(m_conv_w, tap_pad),
                                                   jnp.pad(v_conv_w, tap_pad, constant_values=1.0), "adamw_conv_w")]

    two = lambda a: a.reshape(-1, d)
    vec_rows = [two(g) for g in vec_grads]
    n_vec = sum(a.shape[0] for a in vec_rows)
    pack = jnp.concatenate(vec_rows + [dh0[:N_META], jnp.full((8, d), loss, F32)], axis=0)
    pack = jnp.pad(pack, ((0, -pack.shape[0] % 8), (0, 0)))
    tot = _allsum_small(pack)
    loss_all = tot[n_vec + N_META, 0]

    cat = lambda arrs: jnp.concatenate([two(t) for t in arrs], axis=0)[None]
    small_m = (m_pre_norm_g, m_post_norm_g, m_conv_b, m_conv_ln_g, m_conv_ln_b, m_b_pw2)
    small_v = (v_pre_norm_g, v_post_norm_g, v_conv_b, v_conv_ln_g, v_conv_ln_b, v_b_pw2)
    up_small = _adamw([tot[None, :n_vec]], cat(vecs), cat(small_m), cat(small_v), 0, None, "adamw_vectors")

    def unpack(o):
        res, r0 = [], 0
        for t in vecs:
            nrow = t.size // d
            res.append(o[0, r0:r0 + nrow].reshape(t.shape))
            r0 += nrow
        return res

    up_small = [unpack(o) for o in up_small]
    chip = 2 * lax.axis_index("x") + lax.axis_index("y")
    mcols = meta_tokens.shape[1]
    g_meta = lax.dynamic_slice_in_dim(tot[n_vec:n_vec + N_META], chip * mcols, mcols, axis=1)
    up_meta = [o[0] for o in _adamw([g_meta[None]], meta_tokens[None], m_meta_tokens[None], v_meta_tokens[None],
                                    0, None, "adamw_meta")]

    grad_x = dh0[N_META:length][None]
    outs = [loss_all, grad_x]
    for j in range(4):
        pre, post, cb, lg, lb, bp = up_small[j]
        outs += [up_meta[j], pre, post, up_w_in[j], up_conv_w[j], cb, lg, lb, up_w_pw2[j], bp, up_w_out[j]]
    return tuple(outs)
```

```python
from typing import Callable, NamedTuple

import jax
import jax.numpy as jnp
from jax import lax
from jax.experimental import pallas as pl
from jax.experimental.pallas import tpu as pltpu

F32 = jnp.float32
BF16 = jnp.bfloat16

N_META = 16
D_CONV = 512
D_SB = 512
HEAD_DIM = 64
CONV_WIDTH = 31
CONV_PAD = 32
CONV_ROWS = 128
RMS_EPS = 1e-6
LN_EPS = 1e-5
Q_SCALE = HEAD_DIM ** -0.5

ADAM_LR = 0.001
ADAM_B1 = 0.9
ADAM_B2 = 0.999
ADAM_EPS = 1e-08
ADAM_WD = 0.01
ADAM_STEP = 10

LANES = 128
ROW_TILE = 256
ATT_BLOCK = 256
ATT_PAIRS = 2
ATT_PAIRS_FWD = 4
VMEM_LIMIT = 56 * 1024 * 1024
EXP_ZERO = -104.0
COUNT_LANE = LANES - 1
RELAY_AT = 0.75

MESH = pl.DeviceIdType.MESH


def _pcall(body, **kw):
    return pl.pallas_call(body, **kw)


def _params(*sem):
    return pltpu.CompilerParams(dimension_semantics=sem, vmem_limit_bytes=VMEM_LIMIT)


def _sigmoid(x):
    return 1.0 / (1.0 + jnp.exp(-x))


def _silu_fwd_bwd(x):
    s = _sigmoid(x)
    return x * s, s * (1.0 + x * (1.0 - s))


def _nt(a, b):
    return lax.dot_general(a, b, (((1,), (1,)), ((), ())), preferred_element_type=F32)


def _tn(a, b):
    return lax.dot_general(a, b, (((0,), (0,)), ((), ())), preferred_element_type=F32)


def _nn(a, b):
    return jnp.dot(a, b, preferred_element_type=F32)


def _inproj(h, g, w):
    lp, d = h.shape
    n = w.shape[1]

    def body(h_ref, g_ref, w_ref, ew_ref, qkv_ref):
        x = h_ref[...]
        rstd = lax.rsqrt(jnp.mean(x * x, axis=-1, keepdims=True) + RMS_EPS)
        u = ((x * rstd) * g_ref[...]).astype(BF16)
        p = _nn(u, w_ref[...])
        ew_ref[:, 0:1536] = p[:, 0:1536]
        ew_ref[:, 1536:2048] = p[:, 3072:3584]
        qkv_ref[:, 0:512] = (p[:, 1536:2048] * Q_SCALE).astype(BF16)
        qkv_ref[:, 512:1536] = p[:, 2048:3072].astype(BF16)

    return _pcall(
        body, name="inproj_fwd", grid=(lp // ROW_TILE,),
        in_specs=[pl.BlockSpec((ROW_TILE, d), lambda i: (i, 0)),
                  pl.BlockSpec((1, d), lambda i: (0, 0)),
                  pl.BlockSpec((d, n), lambda i: (0, 0))],
        out_specs=[pl.BlockSpec((ROW_TILE, 2048), lambda i: (i, 0)),
                   pl.BlockSpec((ROW_TILE, 1536), lambda i: (i, 0))],
        out_shape=[jax.ShapeDtypeStruct((lp, 2048), F32), jax.ShapeDtypeStruct((lp, 1536), BF16)],
        compiler_params=_params("parallel"),
    )(h, g, w)


def _layer_norm_stats(c1):
    mu = jnp.mean(c1, axis=-1, keepdims=True)
    xc = c1 - mu
    var = jnp.mean(xc * xc, axis=-1, keepdims=True)
    rstd = lax.rsqrt(var + LN_EPS)
    return xc * rstd, rstd


def _shifted_copies(window, cols, shifted, tm):
    shifted[0] = window[:, cols]
    rows = tm + CONV_PAD - 8
    for b in range(1, 8):
        shifted[b, 0:rows, :] = window[pl.ds(b, rows), cols]


def _shifted_rows(shifted, shift, tm):
    b = shift % 8
    return shifted[b, pl.ds(pl.multiple_of(shift - b, 8), tm), :]


def _weighted(cw8_ref, j, cols, rows):
    w8 = cw8_ref[pl.ds(pl.multiple_of(j * 8, 8), 8), cols]
    r = rows.shape[0]
    return (rows.reshape(r // 8, 8, LANES) * w8[None]).reshape(r, LANES)


def _conv_fwd(ew, cw, cb, lng, lnb, wpw2, bpw2):
    lp = ew.shape[0]
    tm = ROW_TILE

    def body(ew_ref, cw_ref, cb_ref, lng_ref, lnb_ref, w_ref, b_ref, c1_ref, c4_ref, c5_ref, xbuf, shifted):
        @pl.when(pl.program_id(0) == 0)
        def _():
            xbuf[0:CONV_PAD, :] = jnp.zeros((CONV_PAD, D_CONV), F32)

        ga = ew_ref[:, 0:512]
        gb = ew_ref[:, 512:1024]
        cg = ew_ref[:, 1024:1536]
        xbuf[CONV_PAD:CONV_PAD + tm, :] = ga * _sigmoid(gb)
        for blk in range(D_CONV // LANES):
            cs = slice(blk * LANES, (blk + 1) * LANES)
            _shifted_copies(xbuf, cs, shifted, tm)
            for r0 in range(0, tm, CONV_ROWS):
                acc = jnp.zeros((CONV_ROWS, LANES), F32) + cb_ref[:, cs]
                for j in range(CONV_WIDTH):
                    acc = acc + _weighted(cw_ref, j, cs, _shifted_rows(
                        shifted, r0 + CONV_PAD - (CONV_WIDTH - 1) + j, CONV_ROWS))
                c1_ref[r0:r0 + CONV_ROWS, cs] = acc
        xbuf[0:CONV_PAD, :] = xbuf[tm:tm + CONV_PAD, :]
        xhat, _ = _layer_norm_stats(c1_ref[...])
        c2 = xhat * lng_ref[...] + lnb_ref[...]
        c3 = c2 * _sigmoid(c2)
        c4 = _nn(c3.astype(BF16), w_ref[...]) + b_ref[...]
        c4_ref[...] = c4
        c5_ref[...] = (c4 * (cg * _sigmoid(cg))).astype(BF16)

    vec = pl.BlockSpec((1, D_CONV), lambda i: (0, 0))
    row = pl.BlockSpec((tm, D_CONV), lambda i: (i, 0))
    return _pcall(
        body, name="conv_fwd", grid=(lp // tm,),
        in_specs=[pl.BlockSpec((tm, 1536), lambda i: (i, 0)),
                  pl.BlockSpec((8 * CONV_PAD, D_CONV), lambda i: (0, 0)),
                  vec, vec, vec,
                  pl.BlockSpec((D_CONV, D_CONV), lambda i: (0, 0)),
                  vec],
        out_specs=[row, row, row],
        out_shape=[jax.ShapeDtypeStruct((lp, D_CONV), F32), jax.ShapeDtypeStruct((lp, D_CONV), F32),
                   jax.ShapeDtypeStruct((lp, D_CONV), BF16)],
        scratch_shapes=[pltpu.VMEM((tm + CONV_PAD, D_CONV), F32), pltpu.VMEM((8, tm + CONV_PAD, LANES), F32)],
        compiler_params=_params("arbitrary"),
    )(ew, cw, cb, lng, lnb, wpw2, bpw2)


def _block_sums(x, m01):
    return _nn(x.astype(BF16), m01)


def _attn_masks():
    lane = lax.broadcasted_iota(jnp.int32, (1, LANES), 1)
    row = lax.broadcasted_iota(jnp.int32, (2 * ATT_BLOCK, ATT_BLOCK), 0)
    col = lax.broadcasted_iota(jnp.int32, (2 * ATT_BLOCK, ATT_BLOCK), 1)
    return lane < HEAD_DIM, col < (row & (ATT_BLOCK - 1))


def _stack_heads(x, first_head):
    zero = jnp.zeros_like(x)
    return jnp.concatenate([jnp.where(first_head, x, zero), jnp.where(first_head, zero, x)], axis=0)


def _unstack_heads(x2, first_head):
    rows = x2.shape[0] // 2
    return jnp.where(first_head, x2[:rows], x2[rows:])


def _hosted(plan, n_out, n_scratch):
    n_in = 0 if plan is None else len(plan.inputs)
    n_x = 0 if plan is None else len(plan.out_shapes)

    def split(rest):
        a, b, c = n_in + n_out, n_in + n_out + n_x, n_in + n_out + n_x + n_scratch
        return rest[:n_in], rest[n_in:a], rest[a:b], rest[b:c], rest[c:]

    if plan is None:
        return split, [], [], [], [], []
    return split, list(plan.inputs), [ANY] * n_in, [ANY] * n_x, list(plan.out_shapes), list(plan.scratch)


def _attn_fwd(qkv, tri, plan=None):
    lp = qkv.shape[0]
    bq = ATT_BLOCK
    ngrp = ATT_PAIRS_FWD
    nstep = D_SB // (LANES * ngrp)
    nq = lp // bq
    assert nq <= COUNT_LANE
    split, x_args, x_in_specs, x_out_specs, x_out_shapes, x_scratch = _hosted(plan, 2, 3)

    def body(q_ref, k_ref, v_ref, tri_ref, *rest):
        x_in, (o_ref, carry_ref), x_out, (c_s, acc_s, cm_s), x_sems = split(rest)
        i = pl.program_id(1)
        if plan is not None:
            @pl.when(jnp.logical_and(pl.program_id(0) == 0, i == 0))
            def _():
                plan.start(x_in, x_out, x_sems)

            @pl.when(jnp.logical_and(pl.program_id(0) == nstep - 1, i == int(RELAY_AT * nq)))
            def _():
                plan.relay(x_in, x_out, x_sems)

        first_head, vis = _attn_masks()
        lane = lax.broadcasted_iota(jnp.int32, (1, LANES), 1)
        cols = [slice(g * LANES, (g + 1) * LANES) for g in range(ngrp)]
        q2s = [_stack_heads(q_ref[:, cs], first_head) for cs in cols]
        tri_m = tri_ref[...]

        c_s[...] = jnp.zeros_like(c_s)
        acc_s[...] = jnp.zeros_like(acc_s)
        cm_s[...] = jnp.zeros_like(cm_s)

        def blocks(js, masks):
            offs = [pl.multiple_of(j * bq, bq) for j in js]
            work = [(g, b) for b in range(len(js)) for g in range(ngrp)]
            zs = {(g, b): _nt(q2s[g], k_ref[pl.ds(offs[b], bq), cols[g]]) for g, b in work}
            lss = {}
            for g, b in work:
                z = zs[g, b]
                ls = -(jnp.maximum(z, 0.0) + jnp.log(1.0 + jnp.exp(-jnp.abs(z))))
                lss[g, b] = ls if masks[b] is None else jnp.where(masks[b], ls, 0.0)
            tails = {gb: _block_sums(lss[gb], tri_m) for gb in work}
            probs = {}
            carry = [c_s[g] for g in range(ngrp)]
            saved = [cm_s[g] for g in range(ngrp)]
            for g, b in work:
                a = jnp.exp(zs[g, b] + lss[g, b] + tails[g, b] + carry[g])
                probs[g, b] = (a if masks[b] is None else jnp.where(masks[b], a, 0.0)).astype(BF16)
                saved[g] = jnp.where(lane == js[b], carry[g], saved[g])
                carry[g] = carry[g] + tails[g, b][:, 0:1] + lss[g, b][:, 0:1]
            top = None
            for g in range(ngrp):
                c_s[g] = carry[g]
                cm_s[g] = saved[g]
                acc = acc_s[g]
                for b in range(len(js)):
                    acc = acc + _nn(probs[g, b], v_ref[pl.ds(offs[b], bq), cols[g]])
                acc_s[g] = acc
                top = carry[g] if top is None else jnp.maximum(top, carry[g])
            return jnp.max(top) > EXP_ZERO

        alive = lax.cond(i > 0, lambda: blocks([i, i - 1], [vis, None]), lambda: blocks([i], [vis]))
        rest = jnp.maximum(i - 1, 0)

        def pair(carry):
            t, _ = carry
            j = i - 2 - 2 * t
            return t + 1, blocks([j, j - 1], [None, None])

        trips, alive = lax.while_loop(lambda ca: jnp.logical_and(ca[0] < rest // 2, ca[1]), pair, (0, alive))
        last = jnp.logical_and(jnp.logical_and(rest % 2 == 1, trips == rest // 2), alive)

        @pl.when(last)
        def _():
            blocks([0], [None])

        n_done = (jnp.minimum(i + 1, 2) + 2 * trips + last.astype(jnp.int32)).astype(F32)
        for g in range(ngrp):
            cmat = jnp.where(lane == COUNT_LANE, n_done, cm_s[g])
            carry_ref[:, 2 * g * LANES:(2 * g + 1) * LANES] = cmat[:bq]
            carry_ref[:, (2 * g + 1) * LANES:(2 * g + 2) * LANES] = cmat[bq:]
            o_ref[:, cols[g]] = _unstack_heads(acc_s[g], first_head)
        if plan is not None:
            @pl.when(jnp.logical_and(pl.program_id(0) == nstep - 1, i == nq - 1))
            def _():
                plan.finish(x_in, x_out, x_sems)

    width = ngrp * LANES
    outs = _pcall(
        body, name="attn_fwd" if plan is None else "attn_fwd_gather", grid=(nstep, nq),
        in_specs=[pl.BlockSpec((bq, width), lambda p, i: (i, p)),
                  pl.BlockSpec((lp, width), lambda p, i: (0, nstep + p)),
                  pl.BlockSpec((lp, width), lambda p, i: (0, 2 * nstep + p)),
                  pl.BlockSpec((bq, bq), lambda p, i: (0, 0))] + x_in_specs,
        out_specs=[pl.BlockSpec((bq, width), lambda p, i: (i, p)),
                   pl.BlockSpec((bq, 2 * width), lambda p, i: (i, p))] + x_out_specs,
        out_shape=[jax.ShapeDtypeStruct((lp, D_SB), F32), jax.ShapeDtypeStruct((lp, 2 * D_SB), F32)] + x_out_shapes,
        scratch_shapes=[pltpu.VMEM((ngrp, 2 * bq, 1), F32), pltpu.VMEM((ngrp, 2 * bq, LANES), F32),
                        pltpu.VMEM((ngrp, 2 * bq, LANES), F32)] + x_scratch,
        compiler_params=_params("arbitrary", "arbitrary"),
    )(qkv, qkv, qkv, tri, *x_args)
    return outs[0], outs[1], outs[2:]


def _outproj(c5, att, ew, h, w, g):
    lp, d = h.shape
    tm = ROW_TILE

    def body(c5_ref, att_ref, sg_ref, h_ref, w_ref, g_ref, hn_ref, cat_ref, mix_ref):
        sg = sg_ref[...]
        s = att_ref[...] * (sg * _sigmoid(sg))
        cat_ref[:, 0:D_CONV] = c5_ref[...]
        cat_ref[:, D_CONV:] = s.astype(BF16)
        mixed = _nn(cat_ref[...], w_ref[...])
        mix_ref[...] = mixed
        rstd = lax.rsqrt(jnp.mean(mixed * mixed, axis=-1, keepdims=True) + RMS_EPS)
        hn_ref[...] = h_ref[...] + (mixed * rstd) * g_ref[...]

    half = pl.BlockSpec((tm, 512), lambda i: (i, 0))
    full = pl.BlockSpec((tm, d), lambda i: (i, 0))
    return _pcall(
        body, name="outproj_fwd", grid=(lp // tm,),
        in_specs=[half, half, pl.BlockSpec((tm, 512), lambda i: (i, 3)), full,
                  pl.BlockSpec((d, d), lambda i: (0, 0)), pl.BlockSpec((1, d), lambda i: (0, 0))],
        out_specs=[full, full, full],
        out_shape=[jax.ShapeDtypeStruct((lp, d), F32), jax.ShapeDtypeStruct((lp, d), BF16),
                   jax.ShapeDtypeStruct((lp, d), F32)],
        compiler_params=_params("parallel"),
    )(c5, att, ew, h, w, g)


def _loss_head(h, target, seq):
    lp, d = h.shape
    tm = ROW_TILE

    def body(h_ref, t_ref, dh_ref, loss_ref):
        i = pl.program_id(0)

        @pl.when(i == 0)
        def _():
            loss_ref[...] = jnp.zeros_like(loss_ref)

        row = i * tm + lax.broadcasted_iota(jnp.int32, (tm, 1), 0)
        real = jnp.logical_and(row >= N_META, row < N_META + seq)
        diff = jnp.where(real, h_ref[...] - t_ref[...], 0.0)
        dh_ref[...] = diff * (1.0 / d)
        loss_ref[...] += 0.5 * jnp.sum(jnp.sum(diff * diff, axis=-1, keepdims=True) * (1.0 / d))

    full = pl.BlockSpec((tm, d), lambda i: (i, 0))
    return _pcall(
        body, name="loss_head", grid=(lp // tm,),
        in_specs=[full, full],
        out_specs=[full, pl.BlockSpec((8, LANES), lambda i: (0, 0))],
        out_shape=[jax.ShapeDtypeStruct((lp, d), F32), jax.ShapeDtypeStruct((8, LANES), F32)],
        compiler_params=_params("arbitrary"),
    )(h, target)


def _outproj_bwd(dh, mixed, g, w, att, ew, c4):
    lp, d = dh.shape
    tm = ROW_TILE

    def body(dh_ref, mix_ref, g_ref, w_ref, att_ref, cg_ref, sg_ref, c4_ref,
             dmix_ref, datt_ref, dsg_ref, dc4_ref, dcg_ref, dg_ref, db_ref):
        @pl.when(pl.program_id(0) == 0)
        def _():
            dg_ref[...] = jnp.zeros_like(dg_ref)
            db_ref[...] = jnp.zeros_like(db_ref)

        mixed = mix_ref[...]
        dhv = dh_ref[...]
        rstd = lax.rsqrt(jnp.mean(mixed * mixed, axis=-1, keepdims=True) + RMS_EPS)
        n = mixed * rstd
        dg_ref[...] += jnp.sum(dhv * n, axis=0, keepdims=True)
        dn = dhv * g_ref[...]
        dmix = (rstd * (dn - n * jnp.mean(dn * n, axis=-1, keepdims=True))).astype(BF16)
        dmix_ref[...] = dmix
        dcat = _nt(dmix, w_ref[...])
        dc5 = dcat[:, 0:D_CONV]
        ds = dcat[:, D_CONV:]
        silu_sg, dsilu_sg = _silu_fwd_bwd(sg_ref[...])
        datt_ref[...] = (ds * silu_sg).astype(BF16)
        dsg_ref[...] = (ds * att_ref[...] * dsilu_sg).astype(BF16)
        silu_cg, dsilu_cg = _silu_fwd_bwd(cg_ref[...])
        dc4 = dc5 * silu_cg
        db_ref[...] += jnp.sum(dc4, axis=0, keepdims=True)
        dc4_ref[...] = dc4.astype(BF16)
        dcg_ref[...] = (dc5 * c4_ref[...] * dsilu_cg).astype(BF16)

    half = pl.BlockSpec((tm, 512), lambda i: (i, 0))
    full = pl.BlockSpec((tm, d), lambda i: (i, 0))
    hb = jax.ShapeDtypeStruct((lp, 512), BF16)
    return _pcall(
        body, name="outproj_bwd", grid=(lp // tm,),
        in_specs=[full, full, pl.BlockSpec((1, d), lambda i: (0, 0)), pl.BlockSpec((d, d), lambda i: (0, 0)),
                  half, pl.BlockSpec((tm, 512), lambda i: (i, 2)), pl.BlockSpec((tm, 512), lambda i: (i, 3)), half],
        out_specs=[full, half, half, half, half,
                   pl.BlockSpec((1, d), lambda i: (0, 0)), pl.BlockSpec((1, 512), lambda i: (0, 0))],
        out_shape=[jax.ShapeDtypeStruct((lp, d), BF16), hb, hb, hb, hb,
                   jax.ShapeDtypeStruct((1, d), F32), jax.ShapeDtypeStruct((1, 512), F32)],
        compiler_params=_params("arbitrary"),
    )(dh, mixed, g, w, att, ew, ew, c4)


def _attn_bwd(qkv, carries, datt, tri, upper, plan=None):
    lp = qkv.shape[0]
    bq = ATT_BLOCK
    ngrp = ATT_PAIRS
    nstep = D_SB // (LANES * ngrp)
    nq = lp // bq
    split, x_args, x_in_specs, x_out_specs, x_out_shapes, x_scratch = _hosted(plan, 3, 2)

    def body(q_ref, k_ref, v_ref, carry_ref, do_ref, tri_ref, upper_ref, *rest):
        x_in, (dq_ref, dk_ref, dv_ref), x_out, (run_s, dq_s), x_sems = split(rest)
        i = pl.program_id(1)
        if plan is not None:
            @pl.when(jnp.logical_and(pl.program_id(0) == 0, i == 0))
            def _():
                plan.start(x_in, x_out, x_sems)

            @pl.when(jnp.logical_and(pl.program_id(0) == nstep - 1, i == int(RELAY_AT * nq)))
            def _():
                plan.relay(x_in, x_out, x_sems)

        @pl.when(i == 0)
        def _():
            dk_ref[...] = jnp.zeros_like(dk_ref)
            dv_ref[...] = jnp.zeros_like(dv_ref)

        first_head, vis = _attn_masks()
        lane = lax.broadcasted_iota(jnp.int32, (1, LANES), 1)
        cols = [slice(g * LANES, (g + 1) * LANES) for g in range(ngrp)]
        q2s = [_stack_heads(q_ref[:, cs], first_head) for cs in cols]
        do2s = [_stack_heads(do_ref[:, cs], first_head) for cs in cols]
        cmats = [jnp.concatenate([carry_ref[:, 2 * g * LANES:(2 * g + 1) * LANES],
                                  carry_ref[:, (2 * g + 1) * LANES:(2 * g + 2) * LANES]], axis=0)
                 for g in range(ngrp)]
        tri_m = tri_ref[...]
        upper_m = upper_ref[...]

        def blocks(js, masks):
            offs = [pl.multiple_of(j * bq, bq) for j in js]
            work = [(g, b) for b in range(len(js)) for g in range(ngrp)]
            zs = {(g, b): _nt(q2s[g], k_ref[pl.ds(offs[b], bq), cols[g]]) for g, b in work}
            lss = {}
            for g, b in work:
                z = zs[g, b]
                ls = -(jnp.maximum(z, 0.0) + jnp.log(1.0 + jnp.exp(-jnp.abs(z))))
                lss[g, b] = ls if masks[b] is None else jnp.where(masks[b], ls, 0.0)
            tails = {gb: _block_sums(lss[gb], tri_m) for gb in work}
            das = {(g, b): _nt(do2s[g], v_ref[pl.ds(offs[b], bq), cols[g]]) for g, b in work}
            probs, des = {}, {}
            for g, b in work:
                c = jnp.sum(jnp.where(lane == js[b], cmats[g], 0.0), axis=-1, keepdims=True)
                a = jnp.exp(zs[g, b] + lss[g, b] + tails[g, b] + c)
                a = a if masks[b] is None else jnp.where(masks[b], a, 0.0)
                probs[g, b] = a.astype(BF16)
                des[g, b] = das[g, b] * a
            prefixes = {gb: _block_sums(des[gb], upper_m) for gb in work}
            runs = [run_s[g] for g in range(ngrp)]
            dzs = {}
            for g, b in work:
                beta = jnp.exp(zs[g, b] + lss[g, b])
                dz = des[g, b] - beta * (des[g, b] + runs[g] + prefixes[g, b])
                dzs[g, b] = (dz if masks[b] is None else jnp.where(masks[b], dz, 0.0)).astype(BF16)
                runs[g] = runs[g] + prefixes[g, b][:, bq - 1:bq] + des[g, b][:, bq - 1:bq]
            for g in range(ngrp):
                run_s[g] = runs[g]
                dq = dq_s[g]
                for b in range(len(js)):
                    rows = pl.ds(offs[b], bq)
                    dq = dq + _nn(dzs[g, b], k_ref[rows, cols[g]])
                    dk_ref[rows, cols[g]] += _tn(dzs[g, b], q2s[g])
                    dv_ref[rows, cols[g]] += _tn(probs[g, b], do2s[g])
                dq_s[g] = dq

        n_done = jnp.max(carry_ref[:, COUNT_LANE:COUNT_LANE + 1]).astype(jnp.int32)
        n_done = jnp.clip(n_done, 1, i + 1)
        before = jnp.maximum(n_done - 2, 0)
        j0 = i - n_done + 1
        odd = before % 2
        run_s[...] = jnp.zeros_like(run_s)
        dq_s[...] = jnp.zeros_like(dq_s)

        @pl.when(odd == 1)
        def _():
            blocks([j0], [None])

        @pl.loop(0, before // 2)
        def _(t):
            blocks([j0 + odd + 2 * t, j0 + odd + 2 * t + 1], [None, None])

        @pl.when(n_done > 1)
        def _():
            blocks([i - 1, i], [None, vis])

        @pl.when(n_done <= 1)
        def _():
            blocks([i], [vis])

        for g in range(ngrp):
            dq_ref[:, cols[g]] = (_unstack_heads(dq_s[g], first_head) * Q_SCALE).astype(BF16)
        if plan is not None:
            @pl.when(jnp.logical_and(pl.program_id(0) == nstep - 1, i == nq - 1))
            def _():
                plan.finish(x_in, x_out, x_sems)

    width = ngrp * LANES
    qb = pl.BlockSpec((bq, width), lambda p, i: (i, p))
    colb = pl.BlockSpec((lp, width), lambda p, i: (0, p))
    sq = pl.BlockSpec((bq, bq), lambda p, i: (0, 0))
    outs = _pcall(
        body, name="attn_bwd" if plan is None else "attn_bwd_reduce", grid=(nstep, nq),
        in_specs=[qb,
                  pl.BlockSpec((lp, width), lambda p, i: (0, nstep + p)),
                  pl.BlockSpec((lp, width), lambda p, i: (0, 2 * nstep + p)),
                  pl.BlockSpec((bq, 2 * width), lambda p, i: (i, p)), qb, sq, sq] + x_in_specs,
        out_specs=[qb, colb, colb] + x_out_specs,
        out_shape=[jax.ShapeDtypeStruct((lp, D_SB), BF16), jax.ShapeDtypeStruct((lp, D_SB), F32),
                   jax.ShapeDtypeStruct((lp, D_SB), F32)] + x_out_shapes,
        scratch_shapes=[pltpu.VMEM((ngrp, 2 * bq, 1), F32), pltpu.VMEM((ngrp, 2 * bq, LANES), F32)] + x_scratch,
        compiler_params=_params("arbitrary", "arbitrary"),
    )(qkv, qkv, qkv, carries, datt, tri, upper, *x_args)
    return outs[0], outs[1], outs[2], outs[3:]


def _conv_bwd(dc4, c1, ew, cw, lng, lnb, wpw2):
    lp = ew.shape[0]
    tm = ROW_TILE
    nt = lp // tm
    halo_per_tile = tm // CONV_PAD

    def body(dc4_ref, c1_ref, ew_ref, halo_ref, cw_ref, lng_ref, lnb_ref, w_ref,
             dga_ref, dgb_ref, c3_ref, dcw_ref, dcb_ref, dlng_ref, dlnb_ref, xbuf, dbuf, shifted, wacc):
        step = pl.program_id(0)

        @pl.when(step == 0)
        def _():
            wacc[...] = jnp.zeros_like(wacc)
            dcb_ref[...] = jnp.zeros_like(dcb_ref)
            dlng_ref[...] = jnp.zeros_like(dlng_ref)
            dlnb_ref[...] = jnp.zeros_like(dlnb_ref)
            dbuf[tm:tm + CONV_PAD, :] = jnp.zeros((CONV_PAD, D_CONV), F32)

        dc3 = _nt(dc4_ref[...], w_ref[...])
        xhat, rstd = _layer_norm_stats(c1_ref[...])
        c2 = xhat * lng_ref[...] + lnb_ref[...]
        c3, dsilu = _silu_fwd_bwd(c2)
        c3_ref[...] = c3.astype(BF16)
        dc2 = dc3 * dsilu
        dlng_ref[...] += jnp.sum(dc2 * xhat, axis=0, keepdims=True)
        dlnb_ref[...] += jnp.sum(dc2, axis=0, keepdims=True)
        dxhat = dc2 * lng_ref[...]
        dc1 = rstd * (dxhat - jnp.mean(dxhat, axis=-1, keepdims=True)
                      - xhat * jnp.mean(dxhat * xhat, axis=-1, keepdims=True))
        dcb_ref[...] += jnp.sum(dc1, axis=0, keepdims=True)
        dbuf[0:tm, :] = dc1

        ga = ew_ref[:, 0:512]
        sgb = _sigmoid(ew_ref[:, 512:1024])
        xbuf[CONV_PAD:CONV_PAD + tm, :] = ga * sgb
        first_tile = step == nt - 1
        halo = halo_ref[:, 0:512] * _sigmoid(halo_ref[:, 512:1024])
        xbuf[0:CONV_PAD, :] = jnp.where(first_tile, 0.0, halo)

        for cb in range(D_CONV // LANES):
            cs = slice(cb * LANES, (cb + 1) * LANES)
            _shifted_copies(dbuf, cs, shifted, tm)
            for r0 in range(0, tm, CONV_ROWS):
                rs = slice(r0, r0 + CONV_ROWS)
                dc0 = jnp.zeros((CONV_ROWS, LANES), F32)
                for j in range(CONV_WIDTH):
                    dc0 = dc0 + _weighted(cw_ref, j, cs, _shifted_rows(
                        shifted, r0 + CONV_WIDTH - 1 - j, CONV_ROWS))
                dga_ref[rs, cs] = (dc0 * sgb[rs, cs]).astype(BF16)
                dgb_ref[rs, cs] = (dc0 * ga[rs, cs] * sgb[rs, cs] * (1.0 - sgb[rs, cs])).astype(BF16)
            _shifted_copies(xbuf, cs, shifted, tm)
            for r0 in range(0, tm, CONV_ROWS):
                d1 = dbuf[r0:r0 + CONV_ROWS, cs]

                for j in range(CONV_WIDTH):
                    prod = d1 * _shifted_rows(shifted, r0 + CONV_PAD - (CONV_WIDTH - 1) + j, CONV_ROWS)
                    wacc[j * 8:(j + 1) * 8, cs] += jnp.sum(prod.reshape(CONV_ROWS // 8, 8, LANES), axis=0)
        dbuf[tm:tm + CONV_PAD, :] = dbuf[0:CONV_PAD, :]

        @pl.when(step == nt - 1)
        def _():
            dcw_ref[...] = jnp.sum(wacc[...].reshape(CONV_PAD, 8, D_CONV), axis=1)

    rev = lambda i: (nt - 1 - i, 0)
    row = pl.BlockSpec((tm, D_CONV), rev)
    vec = pl.BlockSpec((1, D_CONV), lambda i: (0, 0))
    hb = jax.ShapeDtypeStruct((lp, D_CONV), BF16)
    vs = jax.ShapeDtypeStruct((1, D_CONV), F32)
    return _pcall(
        body, name="conv_bwd", grid=(nt,),
        in_specs=[row, row, pl.BlockSpec((tm, 1024), rev),
                  pl.BlockSpec((CONV_PAD, 1024), lambda i: (jnp.maximum((nt - 1 - i) * halo_per_tile - 1, 0), 0)),
                  pl.BlockSpec((8 * CONV_PAD, D_CONV), lambda i: (0, 0)), vec, vec,
                  pl.BlockSpec((D_CONV, D_CONV), lambda i: (0, 0))],
        out_specs=[row, row, row, pl.BlockSpec((CONV_PAD, D_CONV), lambda i: (0, 0)), vec, vec, vec],
        out_shape=[hb, hb, hb, jax.ShapeDtypeStruct((CONV_PAD, D_CONV), F32), vs, vs, vs],
        scratch_shapes=[pltpu.VMEM((tm + CONV_PAD, D_CONV), F32), pltpu.VMEM((tm + CONV_PAD, D_CONV), F32),
                        pltpu.VMEM((8, tm + CONV_PAD, LANES), F32), pltpu.VMEM((8 * CONV_PAD, D_CONV), F32)],
        compiler_params=_params("arbitrary"),
    )(dc4, c1, ew, ew, cw, lng, lnb, wpw2)


def _inproj_bwd(dga, dgb, dcg, dq, dk, dv, dsg, h, g, w, dh_out):
    lp, d = h.shape
    n = w.shape[1]
    tm = ROW_TILE

    def body(dga_ref, dgb_ref, dcg_ref, dq_ref, dk_ref, dv_ref, dsg_ref, h_ref, g_ref, w_ref, dho_ref,
             dh_ref, dproj_ref, u_ref, dg_ref):
        @pl.when(pl.program_id(0) == 0)
        def _():
            dg_ref[...] = jnp.zeros_like(dg_ref)

        dproj_ref[:, 0:512] = dga_ref[...]
        dproj_ref[:, 512:1024] = dgb_ref[...]
        dproj_ref[:, 1024:1536] = dcg_ref[...]
        dproj_ref[:, 1536:2048] = dq_ref[...]
        dproj_ref[:, 2048:2560] = dk_ref[...].astype(BF16)
        dproj_ref[:, 2560:3072] = dv_ref[...].astype(BF16)
        dproj_ref[:, 3072:3584] = dsg_ref[...]
        du = _nt(dproj_ref[...], w_ref[...])
        x = h_ref[...]
        rstd = lax.rsqrt(jnp.mean(x * x, axis=-1, keepdims=True) + RMS_EPS)
        nrm = x * rstd
        u_ref[...] = (nrm * g_ref[...]).astype(BF16)
        dg_ref[...] += jnp.sum(du * nrm, axis=0, keepdims=True)
        dn = du * g_ref[...]
        dh_ref[...] = dho_ref[...] + rstd * (dn - nrm * jnp.mean(dn * nrm, axis=-1, keepdims=True))

    half = pl.BlockSpec((tm, 512), lambda i: (i, 0))
    full = pl.BlockSpec((tm, d), lambda i: (i, 0))
    return _pcall(
        body, name="inproj_bwd", grid=(lp // tm,),
        in_specs=[half] * 7 + [full, pl.BlockSpec((1, d), lambda i: (0, 0)),
                               pl.BlockSpec((d, n), lambda i: (0, 0)), full],
        out_specs=[full, pl.BlockSpec((tm, n), lambda i: (i, 0)), full, pl.BlockSpec((1, d), lambda i: (0, 0))],
        out_shape=[jax.ShapeDtypeStruct((lp, d), F32), jax.ShapeDtypeStruct((lp, n), BF16),
                   jax.ShapeDtypeStruct((lp, d), BF16), jax.ShapeDtypeStruct((1, d), F32)],
        compiler_params=_params("arbitrary"),
    )(dga, dgb, dcg, dq, dk, dv, dsg, h, g, w, dh_out)


def _row_split(m, parts):
    tm = m // parts
    assert tm * parts == m and tm % 16 == 0, (m, parts)
    return tm


def _matmul_tn(x, dy, tn, name):
    m, k = x.shape
    n = dy.shape[1]
    steps = 4 if m % 64 == 0 else 1
    tm = _row_split(m, steps)

    def body(x_ref, dy_ref, o_ref, acc_ref):
        r = pl.program_id(1)

        @pl.when(r == 0)
        def _():
            acc_ref[...] = jnp.zeros_like(acc_ref)

        acc_ref[...] += _tn(x_ref[...], dy_ref[...])

        @pl.when(r == steps - 1)
        def _():
            o_ref[...] = acc_ref[...].astype(BF16)

    return _pcall(
        body, name=name, grid=(n // tn, steps),
        in_specs=[pl.BlockSpec((tm, k), lambda j, r: (r, 0)), pl.BlockSpec((tm, tn), lambda j, r: (r, j))],
        out_specs=pl.BlockSpec((k, tn), lambda j, r: (0, j)),
        out_shape=jax.ShapeDtypeStruct((k, n), BF16),
        scratch_shapes=[pltpu.VMEM((k, tn), F32)],
        compiler_params=_params("parallel", "arbitrary"),
    )(x, dy)


def _local_step(h0, target_p, seq, vecs, depth, all_weights=None, weights0=None, gather_next=None,
                reduce_layer=None):
    pre_g, post_g, conv_b, ln_g, ln_b, b_pw2 = vecs
    ar = jnp.arange(ATT_BLOCK)
    tri = (ar[:, None] > ar[None, :]).astype(BF16)
    upper = (ar[:, None] < ar[None, :]).astype(BF16)
    row = lambda a, l: a[l][None, :]

    weights = list(all_weights) if all_weights is not None else [weights0] + [None] * (depth - 1)
    saved = []
    h = h0
    for l in range(depth):
        w_in, w_pw2, w_out, conv_w = weights[l]
        ew, qkv = _inproj(h, row(pre_g, l), w_in)
        conv_w = jnp.repeat(conv_w, 8, axis=0)
        weights[l] = (w_in, w_pw2, w_out, conv_w)
        c1, c4, c5 = _conv_fwd(ew, conv_w, row(conv_b, l), row(ln_g, l), row(ln_b, l), w_pw2, row(b_pw2, l))
        plan = gather_next(l) if gather_next is not None and l + 1 < depth else None
        att, carries, gathered = _attn_fwd(qkv, tri, plan)
        if plan is not None:
            weights[l + 1] = tuple(gathered)
        hn, cat, mixed = _outproj(c5, att, ew, h, w_out, row(post_g, l))
        saved.append((h, ew, qkv, c1, c4, att, carries, cat, mixed))
        h = hn

    dh, loss = _loss_head(h, target_p, seq)

    vec_grads = [None] * depth
    mat_grads = [None] * depth
    pending = None
    for l in reversed(range(depth)):
        w_in, w_pw2, w_out, conv_w = weights[l]
        h_in, ew, qkv, c1, c4, att, carries, cat, mixed = saved[l]
        dmix, datt, dsg, dc4, dcg, dpost, dbpw2 = _outproj_bwd(dh, mixed, row(post_g, l), w_out, att, ew, c4)
        dw_out = _matmul_tn(cat, dmix, 512, "dw_out")
        dq, dk, dv, landed = _attn_bwd(qkv, carries, datt, tri, upper, pending)
        if pending is not None:
            mat_grads[l + 1] = landed
        dga, dgb, c3, dcw, dcb, dlng, dlnb = _conv_bwd(dc4, c1, ew, conv_w, row(ln_g, l), row(ln_b, l), w_pw2)
        dw_pw2 = _matmul_tn(c3, dc4, 512, "dw_pw2")
        dh, dproj, u, dpre = _inproj_bwd(dga, dgb, dcg, dq, dk, dv, dsg, h_in, row(pre_g, l), w_in, dh)
        dw_in = _matmul_tn(u, dproj, 1792, "dw_in")
        vec_grads[l] = (dpre[0], dpost[0], dcb[0], dlng[0], dlnb[0], dbpw2[0])
        mats = (dw_in, dw_pw2, dw_out, dcw)
        if reduce_layer is None:
            mat_grads[l] = mats
        else:
            pending = reduce_layer(mats)
    if pending is not None:
        mat_grads[0] = _run_exchange(pending, "reduce_grads")

    vec_grads = [jnp.stack([g[k] for g in vec_grads]) for k in range(len(vecs))]
    return loss[0, 0], dh, vec_grads, mat_grads


N_CHIPS = 4
ANY = pl.BlockSpec(memory_space=pl.ANY)


def _chip_peers():
    x, y, c = lax.axis_index("x"), lax.axis_index("y"), lax.axis_index("c")
    return x, y, c, [(x, 1 - y), (1 - x, y), (1 - x, 1 - y)]


def _shard_slices(refs, dims, idx):
    out = []
    for ref, (axis, size) in zip(refs, dims):
        assert size % LANES == 0
        start = pl.multiple_of(idx * size, LANES)
        sl = [slice(None)] * len(ref.shape)
        sl[axis] = pl.ds(start, size)
        out.append(ref.at[tuple(sl)])
    return out


class _Exchange(NamedTuple):
    inputs: list
    out_shapes: list
    scratch: list
    start: Callable
    relay: Callable
    finish: Callable


def _run_exchange(plan, name):
    n_in, n_out = len(plan.inputs), len(plan.out_shapes)

    def body(*refs):
        parts = refs[:n_in], refs[n_in:n_in + n_out], refs[n_in + n_out:]
        plan.start(*parts)
        plan.relay(*parts)
        plan.finish(*parts)

    return _pcall(body, name=name, in_specs=[ANY] * n_in, out_specs=[ANY] * n_out, out_shape=plan.out_shapes,
                  scratch_shapes=plan.scratch)(*plan.inputs)


def _gather_plan(shards, dims):
    n = len(shards)
    full_shapes = []
    halves = []
    for s, (axis, size) in zip(shards, dims):
        shp = list(s.shape)
        shp[axis] = size * N_CHIPS
        full_shapes.append(jax.ShapeDtypeStruct(tuple(shp), s.dtype))
        tile_rows = 32 // s.dtype.itemsize
        assert s.shape[0] % (2 * tile_rows) == 0
        halves.append((s.shape[0] // 2, tile_rows))

    def half(refs, which):
        return [r.at[pl.ds(pl.multiple_of(which * h, t), h)] for r, (h, t) in zip(refs, halves)]

    def copies(srcs, outs, sems):
        send, recv, loc = sems
        x, y, c, peers = _chip_peers()
        sibling = (x, y, 1 - c)
        mine = _shard_slices(outs, dims, 2 * x + y)
        local = [pltpu.make_async_copy(s, d, loc.at[a]) for a, (s, d) in enumerate(zip(srcs, mine))]

        def remote(src, dst, slot, a, dev):
            return pltpu.make_async_remote_copy(src, dst, send.at[slot, a], recv.at[slot, a],
                                                device_id=dev, device_id_type=MESH)

        sends = [remote(s, d, k, a, (px, py, c))
                 for k, (px, py) in enumerate(peers) for a, (s, d) in enumerate(zip(half(srcs, c), half(mine, c)))]
        theirs = [_shard_slices(outs, dims, 2 * px + py) for px, py in peers]
        arrivals = [remote(s, d, k, a, (px, py, c))
                    for k, (px, py) in enumerate(peers)
                    for a, (s, d) in enumerate(zip(half(srcs, c), half(theirs[k], c)))]
        passed_on = [remote(d, d, 3 + k, a, sibling) for k in range(3) for a, d in enumerate(half(theirs[k], c))]
        from_sibling = [remote(d, d, 3 + k, a, sibling)
                        for k in range(3) for a, d in enumerate(half(theirs[k], 1 - c))]
        return local, sends, arrivals, passed_on, from_sibling

    def start(srcs, outs, sems):
        local, sends = copies(srcs, outs, sems)[:2]
        for cp in local + sends:
            cp.start()

    def relay(srcs, outs, sems):
        _, _, arrivals, passed_on, _ = copies(srcs, outs, sems)
        for arrived, onward in zip(arrivals, passed_on):
            arrived.wait_recv()
            onward.start()

    def finish(srcs, outs, sems):
        local, sends, _, passed_on, from_sibling = copies(srcs, outs, sems)
        for cp in from_sibling:
            cp.wait_recv()
        for cp in sends + passed_on:
            cp.wait_send()
        for cp in local:
            cp.wait()

    scratch = [pltpu.SemaphoreType.DMA((6, n)), pltpu.SemaphoreType.DMA((6, n)), pltpu.SemaphoreType.DMA((n,))]
    return _Exchange(list(shards), full_shapes, scratch, start, relay, finish)


def _reduce_plan(grads, dims):
    n = len(grads)
    piece_shapes = []
    for g, (axis, size) in zip(grads, dims):
        shp = list(g.shape)
        shp[axis] = size
        piece_shapes.append(jax.ShapeDtypeStruct((N_CHIPS,) + tuple(shp), g.dtype))

    def copies(srcs, outs, sems):
        mine, theirs = outs[:n], outs[n:]
        send, recv, loc = sems
        x, y, c, peers = _chip_peers()
        sibling = (x, y, 1 - c)
        own = _shard_slices(srcs, dims, 2 * x + y)

        def remote(src, dst, slot, a, dev):
            return pltpu.make_async_remote_copy(src, dst, send.at[slot, a], recv.at[slot, a],
                                                device_id=dev, device_id_type=MESH)

        local = [pltpu.make_async_copy(own[a], mine[a].at[3], loc.at[a]) for a in range(n)]
        to_sibling = [remote(own[a], theirs[a].at[3], 3, a, sibling) for a in range(n)]
        to_chips = [remote(src, mine[a].at[k], k, a, (px, py, c))
                    for k, (px, py) in enumerate(peers)
                    for a, src in enumerate(_shard_slices(srcs, dims, 2 * px + py))]
        passed_on = [remote(mine[a].at[k], theirs[a].at[k], 4 + k, a, sibling) for k in range(3) for a in range(n)]
        return local, to_sibling, to_chips, passed_on

    def start(srcs, outs, sems):
        local, to_sibling, to_chips, _ = copies(srcs, outs, sems)
        for cp in local + to_sibling + to_chips:
            cp.start()

    def relay(srcs, outs, sems):
        _, _, to_chips, passed_on = copies(srcs, outs, sems)
        for arrived, onward in zip(to_chips, passed_on):
            arrived.wait_recv()
            onward.start()

    def finish(srcs, outs, sems):
        local, to_sibling, to_chips, passed_on = copies(srcs, outs, sems)
        for cp in to_sibling + passed_on:
            cp.wait_recv()
        for cp in to_sibling + to_chips + passed_on:
            cp.wait_send()
        for cp in local:
            cp.wait()

    scratch = [pltpu.SemaphoreType.DMA((7, n)), pltpu.SemaphoreType.DMA((7, n)), pltpu.SemaphoreType.DMA((n,))]
    return _Exchange(list(grads), piece_shapes * 2, scratch, start, relay, finish)


def _allsum_small(pack):
    rows, cols = pack.shape
    ndev = 8

    def body(p_ref, o_ref, buf, send, recv):
        x, y, c = lax.axis_index("x"), lax.axis_index("y"), lax.axis_index("c")
        me = 4 * x + 2 * y + c
        buf[me] = p_ref[...]
        started = []
        for r in range(1, ndev):
            bx, by, bc = (r >> 2) & 1, (r >> 1) & 1, r & 1
            dev = (x ^ bx, y ^ by, c ^ bc)
            cp = pltpu.make_async_remote_copy(p_ref, buf.at[me], send.at[r], recv.at[r],
                                              device_id=dev, device_id_type=MESH)
            cp.start()
            started.append(cp)
        for r in range(1, ndev):
            pltpu.make_async_remote_copy(p_ref, buf.at[me ^ r], send.at[r], recv.at[r],
                                         device_id=(x, y, c), device_id_type=MESH).wait_recv()
        for cp in started:
            cp.wait_send()
        acc = buf[0]
        for d in range(1, ndev):
            acc = acc + buf[d]
        o_ref[...] = acc

    vm = pl.BlockSpec(memory_space=pltpu.VMEM)
    return _pcall(
        body, name="allsum_small", in_specs=[vm], out_specs=vm,
        out_shape=jax.ShapeDtypeStruct((rows, cols), F32),
        scratch_shapes=[pltpu.VMEM((ndev, rows, cols), F32), pltpu.SemaphoreType.DMA((ndev,)),
                        pltpu.SemaphoreType.DMA((ndev,))],
    )(pack)


def _adamw(parts, w, m, v, layer, prev, name):
    _, rows, cols = w.shape
    tr = ROW_TILE if rows % ROW_TILE == 0 else rows
    counts = [p.shape[0] for p in parts]
    n_parts = len(parts)
    n_prev = 0 if prev is None else 4

    def body(*refs):
        part_refs = refs[:n_parts]
        w_ref, m_ref, v_ref = refs[n_parts:n_parts + 3]
        g_ref, d_ref, nm_ref, nv_ref = refs[n_parts + 3 + n_prev:]
        g = None
        for p_ref, cnt in zip(part_refs, counts):
            s = p_ref[0].astype(F32)
            for k in range(1, cnt):
                s = s + p_ref[k].astype(F32)
            g = s if g is None else g + s
        m2 = ADAM_B1 * m_ref[0] + (1.0 - ADAM_B1) * g
        v2 = ADAM_B2 * v_ref[0] + (1.0 - ADAM_B2) * (g * g)
        m_hat = m2 / (1.0 - ADAM_B1 ** ADAM_STEP)
        v_hat = v2 / (1.0 - ADAM_B2 ** ADAM_STEP)
        g_ref[0] = g
        d_ref[0] = -ADAM_LR * (m_hat / (jnp.sqrt(v_hat) + ADAM_EPS) + ADAM_WD * w_ref[0])
        nm_ref[0] = m2
        nv_ref[0] = v2

    blk = pl.BlockSpec((1, tr, cols), lambda i: (layer, i, 0))
    shp = jax.ShapeDtypeStruct(w.shape, F32)
    return _pcall(
        body, name=name, grid=(rows // tr,),
        in_specs=[pl.BlockSpec((cnt, tr, cols), lambda i: (0, i, 0)) for cnt in counts] + [blk] * 3 + [ANY] * n_prev,
        out_specs=[blk] * 4, out_shape=[shp] * 4,
        input_output_aliases={n_parts + 3 + k: k for k in range(n_prev)},
        compiler_params=_params("parallel"),
    )(*parts, w, m, v, *(prev or ()))


def kernel(x, meta_tokens, pre_norm_g, post_norm_g, w_in, conv_w, conv_b, conv_ln_g, conv_ln_b, w_pw2, b_pw2, w_out, loss_target, m_meta_tokens, m_pre_norm_g, m_post_norm_g, m_w_in, m_conv_w, m_conv_b, m_conv_ln_g, m_conv_ln_b, m_w_pw2, m_b_pw2, m_w_out, v_meta_tokens, v_pre_norm_g, v_post_norm_g, v_w_in, v_conv_w, v_conv_b, v_conv_ln_g, v_conv_ln_b, v_w_pw2, v_b_pw2, v_w_out):
    seq, d = x.shape[1], x.shape[2]
    depth = w_in.shape[0]
    length = N_META + seq
    lp = -(-length // ATT_BLOCK) * ATT_BLOCK
    tap_pad = ((0, 0), (0, CONV_PAD - CONV_WIDTH), (0, 0))

    shards = (w_in.astype(BF16), w_pw2.astype(BF16), w_out.astype(BF16), jnp.pad(conv_w, tap_pad))
    dims = [(1, w_in.shape[2]), (0, w_pw2.shape[1]), (0, w_out.shape[1]), (1, conv_w.shape[2])]
    layer_shards = lambda l: [s[l] for s in shards]

    first = _run_exchange(_gather_plan(layer_shards(0) + [meta_tokens], dims + [(1, meta_tokens.shape[1])]),
                          "gather_weights")
    weights0, meta_f = tuple(first[:4]), first[4]

    h0 = jnp.concatenate([meta_f, x[0], jnp.zeros((lp - length, d), F32)], axis=0)
    target_p = jnp.pad(loss_target[0], ((N_META, lp - length), (0, 0)))
    vecs = (pre_norm_g, post_norm_g, conv_b, conv_ln_g, conv_ln_b, b_pw2)
    loss, dh0, vec_grads, pieces = _local_step(
        h0, target_p, seq, vecs, depth, weights0=weights0,
        gather_next=lambda l: _gather_plan(layer_shards(l + 1), dims),
        reduce_layer=lambda grads: _reduce_plan(list(grads), dims))

    def update(k, w, m, v, name):
        outs = None
        for l in reversed(range(depth)):
            outs = _adamw([pieces[l][k], pieces[l][4 + k]], w, m, v, l, outs, name)
        return outs

    up_w_in = update(0, w_in, m_w_in, v_w_in, "adamw_w_in")
    up_w_pw2 = update(1, w_pw2, m_w_pw2, v_w_pw2, "adamw_w_pw2")
    up_w_out = update(2, w_out, m_w_out, v_w_out, "adamw_w_out")
    up_conv_w = [o[:, :CONV_WIDTH] for o in update(3, jnp.pad(conv_w, tap_pad), jnp.pad(m_conv_w, tap_pad),
                                                   jnp.pad(v_conv_w, tap_pad, constant_values=1.0), "adamw_conv_w")]

    two = lambda a: a.reshape(-1, d)
    vec_rows = [two(g) for g in vec_grads]
    n_vec = sum(a.shape[0] for a in vec_rows)
    pack = jnp.concatenate(vec_rows + [dh0[:N_META], jnp.full((8, d), loss, F32)], axis=0)
    pack = jnp.pad(pack, ((0, -pack.shape[0] % 8), (0, 0)))
    tot = _allsum_small(pack)
    loss_all = tot[n_vec + N_META, 0]

    cat = lambda arrs: jnp.concatenate([two(t) for t in arrs], axis=0)[None]
    small_m = (m_pre_norm_g, m_post_norm_g, m_conv_b, m_conv_ln_g, m_conv_ln_b, m_b_pw2)
    small_v = (v_pre_norm_g, v_post_norm_g, v_conv_b, v_conv_ln_g, v_conv_ln_b, v_b_pw2)
    up_small = _adamw([tot[None, :n_vec]], cat(vecs), cat(small_m), cat(small_v), 0, None, "adamw_vectors")

    def unpack(o):
        res, r0 = [], 0
        for t in vecs:
            nrow = t.size // d
            res.append(o[0, r0:r0 + nrow].reshape(t.shape))
            r0 += nrow
        return res

    up_small = [unpack(o) for o in up_small]
    chip = 2 * lax.axis_index("x") + lax.axis_index("y")
    mcols = meta_tokens.shape[1]
    g_meta = lax.dynamic_slice_in_dim(tot[n_vec:n_vec + N_META], chip * mcols, mcols, axis=1)
    up_meta = [o[0] for o in _adamw([g_meta[None]], meta_tokens[None], m_meta_tokens[None], v_meta_tokens[None],
                                    0, None, "adamw_meta")]

    grad_x = dh0[N_META:length][None]
    outs = [loss_all, grad_x]
    for j in range(4):
        pre, post, cb, lg, lb, bp = up_small[j]
        outs += [up_meta[j], pre, post, up_w_in[j], up_conv_w[j], cb, lg, lb, up_w_pw2[j], bp, up_w_out[j]]
    return tuple(outs)
```

```python
from typing import Callable, NamedTuple

import jax
import jax.numpy as jnp
from jax import lax
from jax.experimental import pallas as pl
from jax.experimental.pallas import tpu as pltpu

F32 = jnp.float32
BF16 = jnp.bfloat16

N_META = 16
D_CONV = 512
D_SB = 512
HEAD_DIM = 64
CONV_WIDTH = 31
CONV_PAD = 32
CONV_ROWS = 128
RMS_EPS = 1e-6
LN_EPS = 1e-5
Q_SCALE = HEAD_DIM ** -0.5

ADAM_LR = 0.001
ADAM_B1 = 0.9
ADAM_B2 = 0.999
ADAM_EPS = 1e-08
ADAM_WD = 0.01
ADAM_STEP = 10

LANES = 128
ROW_TILE = 256
ATT_BLOCK = 256
ATT_PAIRS = 2
ATT_PAIRS_FWD = 4
VMEM_LIMIT = 56 * 1024 * 1024
EXP_ZERO = -104.0
COUNT_LANE = LANES - 1
RELAY_AT = 0.75

MESH = pl.DeviceIdType.MESH


def _pcall(body, **kw):
    return pl.pallas_call(body, **kw)


def _params(*sem):
    return pltpu.CompilerParams(dimension_semantics=sem, vmem_limit_bytes=VMEM_LIMIT)


def _sigmoid(x):
    return 1.0 / (1.0 + jnp.exp(-x))


def _silu_fwd_bwd(x):
    s = _sigmoid(x)
    return x * s, s * (1.0 + x * (1.0 - s))


def _nt(a, b):
    return lax.dot_general(a, b, (((1,), (1,)), ((), ())), preferred_element_type=F32)


def _tn(a, b):
    return lax.dot_general(a, b, (((0,), (0,)), ((), ())), preferred_element_type=F32)


def _nn(a, b):
    return jnp.dot(a, b, preferred_element_type=F32)


def _host_begin(plan, parts, steps):
    if plan is not None:
        x_in, _, x_out, _, x_sems = parts

        @pl.when(pl.program_id(0) == 0)
        def _():
            plan.start(x_in, x_out, x_sems)

        @pl.when(pl.program_id(0) == int(RELAY_AT * steps))
        def _():
            plan.relay(x_in, x_out, x_sems)


def _host_end(plan, parts, steps):
    if plan is not None:
        x_in, _, x_out, _, x_sems = parts

        @pl.when(pl.program_id(0) == steps - 1)
        def _():
            plan.finish(x_in, x_out, x_sems)


def _inproj(h, g, w, plan=None):
    lp, d = h.shape
    n = w.shape[1]
    steps = lp // ROW_TILE
    split, x_args, x_in_specs, x_out_specs, x_out_shapes, x_scratch = _hosted(plan, 2, 0)

    def body(h_ref, g_ref, w_ref, *rest):
        parts = split(rest)
        ew_ref, qkv_ref = parts[1]
        _host_begin(plan, parts, steps)
        x = h_ref[...]
        rstd = lax.rsqrt(jnp.mean(x * x, axis=-1, keepdims=True) + RMS_EPS)
        u = ((x * rstd) * g_ref[...]).astype(BF16)
        p = _nn(u, w_ref[...])
        ew_ref[:, 0:1536] = p[:, 0:1536]
        ew_ref[:, 1536:2048] = p[:, 3072:3584]
        qkv_ref[:, 0:512] = (p[:, 1536:2048] * Q_SCALE).astype(BF16)
        qkv_ref[:, 512:1536] = p[:, 2048:3072].astype(BF16)
        _host_end(plan, parts, steps)

    outs = _pcall(
        body, name="inproj_fwd" if plan is None else "inproj_fwd_gather", grid=(steps,),
        in_specs=[pl.BlockSpec((ROW_TILE, d), lambda i: (i, 0)),
                  pl.BlockSpec((1, d), lambda i: (0, 0)),
                  pl.BlockSpec((d, n), lambda i: (0, 0))] + x_in_specs,
        out_specs=[pl.BlockSpec((ROW_TILE, 2048), lambda i: (i, 0)),
                   pl.BlockSpec((ROW_TILE, 1536), lambda i: (i, 0))] + x_out_specs,
        out_shape=[jax.ShapeDtypeStruct((lp, 2048), F32), jax.ShapeDtypeStruct((lp, 1536), BF16)] + x_out_shapes,
        scratch_shapes=x_scratch,
        compiler_params=_params("parallel" if plan is None else "arbitrary"),
    )(h, g, w, *x_args)
    return outs[0], outs[1], outs[2:]


def _layer_norm_stats(c1):
    mu = jnp.mean(c1, axis=-1, keepdims=True)
    xc = c1 - mu
    var = jnp.mean(xc * xc, axis=-1, keepdims=True)
    rstd = lax.rsqrt(var + LN_EPS)
    return xc * rstd, rstd


def _shifted_copies(window, cols, shifted, tm):
    shifted[0] = window[:, cols]
    rows = tm + CONV_PAD - 8
    for b in range(1, 8):
        shifted[b, 0:rows, :] = window[pl.ds(b, rows), cols]


def _shifted_rows(shifted, shift, tm):
    b = shift % 8
    return shifted[b, pl.ds(pl.multiple_of(shift - b, 8), tm), :]


def _weighted(cw8_ref, j, cols, rows):
    w8 = cw8_ref[pl.ds(pl.multiple_of(j * 8, 8), 8), cols]
    r = rows.shape[0]
    return (rows.reshape(r // 8, 8, LANES) * w8[None]).reshape(r, LANES)


def _conv_fwd(ew, cw, cb, lng, lnb, wpw2, bpw2, plan=None):
    lp = ew.shape[0]
    tm = ROW_TILE
    steps = lp // tm
    split, x_args, x_in_specs, x_out_specs, x_out_shapes, x_scratch = _hosted(plan, 3, 2)

    def body(ew_ref, cw_ref, cb_ref, lng_ref, lnb_ref, w_ref, b_ref, *rest):
        parts = split(rest)
        (c1_ref, c4_ref, c5_ref), (xbuf, shifted) = parts[1], parts[3]
        _host_begin(plan, parts, steps)
        @pl.when(pl.program_id(0) == 0)
        def _():
            xbuf[0:CONV_PAD, :] = jnp.zeros((CONV_PAD, D_CONV), F32)

        ga = ew_ref[:, 0:512]
        gb = ew_ref[:, 512:1024]
        cg = ew_ref[:, 1024:1536]
        xbuf[CONV_PAD:CONV_PAD + tm, :] = ga * _sigmoid(gb)
        for blk in range(D_CONV // LANES):
            cs = slice(blk * LANES, (blk + 1) * LANES)
            _shifted_copies(xbuf, cs, shifted, tm)
            for r0 in range(0, tm, CONV_ROWS):
                acc = jnp.zeros((CONV_ROWS, LANES), F32) + cb_ref[:, cs]
                for j in range(CONV_WIDTH):
                    acc = acc + _weighted(cw_ref, j, cs, _shifted_rows(
                        shifted, r0 + CONV_PAD - (CONV_WIDTH - 1) + j, CONV_ROWS))
                c1_ref[r0:r0 + CONV_ROWS, cs] = acc
        xbuf[0:CONV_PAD, :] = xbuf[tm:tm + CONV_PAD, :]
        xhat, _ = _layer_norm_stats(c1_ref[...])
        c2 = xhat * lng_ref[...] + lnb_ref[...]
        c3 = c2 * _sigmoid(c2)
        c4 = _nn(c3.astype(BF16), w_ref[...]) + b_ref[...]
        c4_ref[...] = c4
        c5_ref[...] = (c4 * (cg * _sigmoid(cg))).astype(BF16)
        _host_end(plan, parts, steps)

    vec = pl.BlockSpec((1, D_CONV), lambda i: (0, 0))
    row = pl.BlockSpec((tm, D_CONV), lambda i: (i, 0))
    outs = _pcall(
        body, name="conv_fwd" if plan is None else "conv_fwd_gather", grid=(steps,),
        in_specs=[pl.BlockSpec((tm, 1536), lambda i: (i, 0)),
                  pl.BlockSpec((8 * CONV_PAD, D_CONV), lambda i: (0, 0)),
                  vec, vec, vec,
                  pl.BlockSpec((D_CONV, D_CONV), lambda i: (0, 0)),
                  vec] + x_in_specs,
        out_specs=[row, row, row] + x_out_specs,
        out_shape=[jax.ShapeDtypeStruct((lp, D_CONV), F32), jax.ShapeDtypeStruct((lp, D_CONV), F32),
                   jax.ShapeDtypeStruct((lp, D_CONV), BF16)] + x_out_shapes,
        scratch_shapes=[pltpu.VMEM((tm + CONV_PAD, D_CONV), F32),
                        pltpu.VMEM((8, tm + CONV_PAD, LANES), F32)] + x_scratch,
        compiler_params=_params("arbitrary"),
    )(ew, cw, cb, lng, lnb, wpw2, bpw2, *x_args)
    return outs[0], outs[1], outs[2], outs[3:]


def _block_sums(x, m01):
    return _nn(x.astype(BF16), m01)


def _attn_masks():
    lane = lax.broadcasted_iota(jnp.int32, (1, LANES), 1)
    row = lax.broadcasted_iota(jnp.int32, (2 * ATT_BLOCK, ATT_BLOCK), 0)
    col = lax.broadcasted_iota(jnp.int32, (2 * ATT_BLOCK, ATT_BLOCK), 1)
    return lane < HEAD_DIM, col < (row & (ATT_BLOCK - 1))


def _stack_heads(x, first_head):
    zero = jnp.zeros_like(x)
    return jnp.concatenate([jnp.where(first_head, x, zero), jnp.where(first_head, zero, x)], axis=0)


def _unstack_heads(x2, first_head):
    rows = x2.shape[0] // 2
    return jnp.where(first_head, x2[:rows], x2[rows:])


def _hosted(plan, n_out, n_scratch):
    n_in = 0 if plan is None else len(plan.inputs)
    n_x = 0 if plan is None else len(plan.out_shapes)

    def split(rest):
        a, b, c = n_in + n_out, n_in + n_out + n_x, n_in + n_out + n_x + n_scratch
        return rest[:n_in], rest[n_in:a], rest[a:b], rest[b:c], rest[c:]

    if plan is None:
        return split, [], [], [], [], []
    return split, list(plan.inputs), [ANY] * n_in, [ANY] * n_x, list(plan.out_shapes), list(plan.scratch)


def _attn_fwd(qkv, tri, plan=None):
    lp = qkv.shape[0]
    bq = ATT_BLOCK
    ngrp = ATT_PAIRS_FWD
    nstep = D_SB // (LANES * ngrp)
    nq = lp // bq
    assert nq <= COUNT_LANE
    split, x_args, x_in_specs, x_out_specs, x_out_shapes, x_scratch = _hosted(plan, 2, 3)

    def body(q_ref, k_ref, v_ref, tri_ref, *rest):
        x_in, (o_ref, carry_ref), x_out, (c_s, acc_s, cm_s), x_sems = split(rest)
        i = pl.program_id(1)
        if plan is not None:
            @pl.when(jnp.logical_and(pl.program_id(0) == 0, i == 0))
            def _():
                plan.start(x_in, x_out, x_sems)

            @pl.when(jnp.logical_and(pl.program_id(0) == nstep - 1, i == int(RELAY_AT * nq)))
            def _():
                plan.relay(x_in, x_out, x_sems)

        first_head, vis = _attn_masks()
        lane = lax.broadcasted_iota(jnp.int32, (1, LANES), 1)
        cols = [slice(g * LANES, (g + 1) * LANES) for g in range(ngrp)]
        q2s = [_stack_heads(q_ref[:, cs], first_head) for cs in cols]
        tri_m = tri_ref[...]

        c_s[...] = jnp.zeros_like(c_s)
        acc_s[...] = jnp.zeros_like(acc_s)
        cm_s[...] = jnp.zeros_like(cm_s)

        def blocks(js, masks):
            offs = [pl.multiple_of(j * bq, bq) for j in js]
            work = [(g, b) for b in range(len(js)) for g in range(ngrp)]
            zs = {(g, b): _nt(q2s[g], k_ref[pl.ds(offs[b], bq), cols[g]]) for g, b in work}
            lss = {}
            for g, b in work:
                z = zs[g, b]
                ls = -(jnp.maximum(z, 0.0) + jnp.log(1.0 + jnp.exp(-jnp.abs(z))))
                lss[g, b] = ls if masks[b] is None else jnp.where(masks[b], ls, 0.0)
            tails = {gb: _block_sums(lss[gb], tri_m) for gb in work}
            probs = {}
            carry = [c_s[g] for g in range(ngrp)]
            saved = [cm_s[g] for g in range(ngrp)]
            for g, b in work:
                a = jnp.exp(zs[g, b] + lss[g, b] + tails[g, b] + carry[g])
                probs[g, b] = (a if masks[b] is None else jnp.where(masks[b], a, 0.0)).astype(BF16)
                saved[g] = jnp.where(lane == js[b], carry[g], saved[g])
                carry[g] = carry[g] + tails[g, b][:, 0:1] + lss[g, b][:, 0:1]
            top = None
            for g in range(ngrp):
                c_s[g] = carry[g]
                cm_s[g] = saved[g]
                acc = acc_s[g]
                for b in range(len(js)):
                    acc = acc + _nn(probs[g, b], v_ref[pl.ds(offs[b], bq), cols[g]])
                acc_s[g] = acc
                top = carry[g] if top is None else jnp.maximum(top, carry[g])
            return jnp.max(top) > EXP_ZERO

        alive = lax.cond(i > 0, lambda: blocks([i, i - 1], [vis, None]), lambda: blocks([i], [vis]))
        rest = jnp.maximum(i - 1, 0)

        def pair(carry):
            t, _ = carry
            j = i - 2 - 2 * t
            return t + 1, blocks([j, j - 1], [None, None])

        trips, alive = lax.while_loop(lambda ca: jnp.logical_and(ca[0] < rest // 2, ca[1]), pair, (0, alive))
        last = jnp.logical_and(jnp.logical_and(rest % 2 == 1, trips == rest // 2), alive)

        @pl.when(last)
        def _():
            blocks([0], [None])

        n_done = (jnp.minimum(i + 1, 2) + 2 * trips + last.astype(jnp.int32)).astype(F32)
        for g in range(ngrp):
            cmat = jnp.where(lane == COUNT_LANE, n_done, cm_s[g])
            carry_ref[:, 2 * g * LANES:(2 * g + 1) * LANES] = cmat[:bq]
            carry_ref[:, (2 * g + 1) * LANES:(2 * g + 2) * LANES] = cmat[bq:]
            o_ref[:, cols[g]] = _unstack_heads(acc_s[g], first_head)
        if plan is not None:
            @pl.when(jnp.logical_and(pl.program_id(0) == nstep - 1, i == nq - 1))
            def _():
                plan.finish(x_in, x_out, x_sems)

    width = ngrp * LANES
    outs = _pcall(
        body, name="attn_fwd" if plan is None else "attn_fwd_gather", grid=(nstep, nq),
        in_specs=[pl.BlockSpec((bq, width), lambda p, i: (i, p)),
                  pl.BlockSpec((lp, width), lambda p, i: (0, nstep + p)),
                  pl.BlockSpec((lp, width), lambda p, i: (0, 2 * nstep + p)),
                  pl.BlockSpec((bq, bq), lambda p, i: (0, 0))] + x_in_specs,
        out_specs=[pl.BlockSpec((bq, width), lambda p, i: (i, p)),
                   pl.BlockSpec((bq, 2 * width), lambda p, i: (i, p))] + x_out_specs,
        out_shape=[jax.ShapeDtypeStruct((lp, D_SB), F32), jax.ShapeDtypeStruct((lp, 2 * D_SB), F32)] + x_out_shapes,
        scratch_shapes=[pltpu.VMEM((ngrp, 2 * bq, 1), F32), pltpu.VMEM((ngrp, 2 * bq, LANES), F32),
                        pltpu.VMEM((ngrp, 2 * bq, LANES), F32)] + x_scratch,
        compiler_params=_params("arbitrary", "arbitrary"),
    )(qkv, qkv, qkv, tri, *x_args)
    return outs[0], outs[1], outs[2:]


def _outproj(c5, att, ew, h, w, g):
    lp, d = h.shape
    tm = ROW_TILE

    def body(c5_ref, att_ref, sg_ref, h_ref, w_ref, g_ref, hn_ref, cat_ref, mix_ref):
        sg = sg_ref[...]
        s = att_ref[...] * (sg * _sigmoid(sg))
        cat_ref[:, 0:D_CONV] = c5_ref[...]
        cat_ref[:, D_CONV:] = s.astype(BF16)
        mixed = _nn(cat_ref[...], w_ref[...])
        mix_ref[...] = mixed
        rstd = lax.rsqrt(jnp.mean(mixed * mixed, axis=-1, keepdims=True) + RMS_EPS)
        hn_ref[...] = h_ref[...] + (mixed * rstd) * g_ref[...]

    half = pl.BlockSpec((tm, 512), lambda i: (i, 0))
    full = pl.BlockSpec((tm, d), lambda i: (i, 0))
    return _pcall(
        body, name="outproj_fwd", grid=(lp // tm,),
        in_specs=[half, half, pl.BlockSpec((tm, 512), lambda i: (i, 3)), full,
                  pl.BlockSpec((d, d), lambda i: (0, 0)), pl.BlockSpec((1, d), lambda i: (0, 0))],
        out_specs=[full, full, full],
        out_shape=[jax.ShapeDtypeStruct((lp, d), F32), jax.ShapeDtypeStruct((lp, d), BF16),
                   jax.ShapeDtypeStruct((lp, d), F32)],
        compiler_params=_params("parallel"),
    )(c5, att, ew, h, w, g)


def _loss_head(h, target, seq):
    lp, d = h.shape
    tm = ROW_TILE

    def body(h_ref, t_ref, dh_ref, loss_ref):
        i = pl.program_id(0)

        @pl.when(i == 0)
        def _():
            loss_ref[...] = jnp.zeros_like(loss_ref)

        row = i * tm + lax.broadcasted_iota(jnp.int32, (tm, 1), 0)
        real = jnp.logical_and(row >= N_META, row < N_META + seq)
        diff = jnp.where(real, h_ref[...] - t_ref[...], 0.0)
        dh_ref[...] = diff * (1.0 / d)
        loss_ref[...] += 0.5 * jnp.sum(jnp.sum(diff * diff, axis=-1, keepdims=True) * (1.0 / d))

    full = pl.BlockSpec((tm, d), lambda i: (i, 0))
    return _pcall(
        body, name="loss_head", grid=(lp // tm,),
        in_specs=[full, full],
        out_specs=[full, pl.BlockSpec((8, LANES), lambda i: (0, 0))],
        out_shape=[jax.ShapeDtypeStruct((lp, d), F32), jax.ShapeDtypeStruct((8, LANES), F32)],
        compiler_params=_params("arbitrary"),
    )(h, target)


def _outproj_bwd(dh, mixed, g, w, att, ew, c4):
    lp, d = dh.shape
    tm = ROW_TILE

    def body(dh_ref, mix_ref, g_ref, w_ref, att_ref, cg_ref, sg_ref, c4_ref,
             dmix_ref, datt_ref, dsg_ref, dc4_ref, dcg_ref, dg_ref, db_ref):
        @pl.when(pl.program_id(0) == 0)
        def _():
            dg_ref[...] = jnp.zeros_like(dg_ref)
            db_ref[...] = jnp.zeros_like(db_ref)

        mixed = mix_ref[...]
        dhv = dh_ref[...]
        rstd = lax.rsqrt(jnp.mean(mixed * mixed, axis=-1, keepdims=True) + RMS_EPS)
        n = mixed * rstd
        dg_ref[...] += jnp.sum(dhv * n, axis=0, keepdims=True)
        dn = dhv * g_ref[...]
        dmix = (rstd * (dn - n * jnp.mean(dn * n, axis=-1, keepdims=True))).astype(BF16)
        dmix_ref[...] = dmix
        dcat = _nt(dmix, w_ref[...])
        dc5 = dcat[:, 0:D_CONV]
        ds = dcat[:, D_CONV:]
        silu_sg, dsilu_sg = _silu_fwd_bwd(sg_ref[...])
        datt_ref[...] = (ds * silu_sg).astype(BF16)
        dsg_ref[...] = (ds * att_ref[...] * dsilu_sg).astype(BF16)
        silu_cg, dsilu_cg = _silu_fwd_bwd(cg_ref[...])
        dc4 = dc5 * silu_cg
        db_ref[...] += jnp.sum(dc4, axis=0, keepdims=True)
        dc4_ref[...] = dc4.astype(BF16)
        dcg_ref[...] = (dc5 * c4_ref[...] * dsilu_cg).astype(BF16)

    half = pl.BlockSpec((tm, 512), lambda i: (i, 0))
    full = pl.BlockSpec((tm, d), lambda i: (i, 0))
    hb = jax.ShapeDtypeStruct((lp, 512), BF16)
    return _pcall(
        body, name="outproj_bwd", grid=(lp // tm,),
        in_specs=[full, full, pl.BlockSpec((1, d), lambda i: (0, 0)), pl.BlockSpec((d, d), lambda i: (0, 0)),
                  half, pl.BlockSpec((tm, 512), lambda i: (i, 2)), pl.BlockSpec((tm, 512), lambda i: (i, 3)), half],
        out_specs=[full, half, half, half, half,
                   pl.BlockSpec((1, d), lambda i: (0, 0)), pl.BlockSpec((1, 512), lambda i: (0, 0))],
        out_shape=[jax.ShapeDtypeStruct((lp, d), BF16), hb, hb, hb, hb,
                   jax.ShapeDtypeStruct((1, d), F32), jax.ShapeDtypeStruct((1, 512), F32)],
        compiler_params=_params("arbitrary"),
    )(dh, mixed, g, w, att, ew, ew, c4)


def _attn_bwd(qkv, carries, datt, tri, upper, plan=None):
    lp = qkv.shape[0]
    bq = ATT_BLOCK
    ngrp = ATT_PAIRS
    nstep = D_SB // (LANES * ngrp)
    nq = lp // bq
    split, x_args, x_in_specs, x_out_specs, x_out_shapes, x_scratch = _hosted(plan, 3, 2)

    def body(q_ref, k_ref, v_ref, carry_ref, do_ref, tri_ref, upper_ref, *rest):
        x_in, (dq_ref, dk_ref, dv_ref), x_out, (run_s, dq_s), x_sems = split(rest)
        i = pl.program_id(1)
        if plan is not None:
            @pl.when(jnp.logical_and(pl.program_id(0) == 0, i == 0))
            def _():
                plan.start(x_in, x_out, x_sems)

            @pl.when(jnp.logical_and(pl.program_id(0) == nstep - 1, i == int(RELAY_AT * nq)))
            def _():
                plan.relay(x_in, x_out, x_sems)

        @pl.when(i == 0)
        def _():
            dk_ref[...] = jnp.zeros_like(dk_ref)
            dv_ref[...] = jnp.zeros_like(dv_ref)

        first_head, vis = _attn_masks()
        lane = lax.broadcasted_iota(jnp.int32, (1, LANES), 1)
        cols = [slice(g * LANES, (g + 1) * LANES) for g in range(ngrp)]
        q2s = [_stack_heads(q_ref[:, cs], first_head) for cs in cols]
        do2s = [_stack_heads(do_ref[:, cs], first_head) for cs in cols]
        cmats = [jnp.concatenate([carry_ref[:, 2 * g * LANES:(2 * g + 1) * LANES],
                                  carry_ref[:, (2 * g + 1) * LANES:(2 * g + 2) * LANES]], axis=0)
                 for g in range(ngrp)]
        tri_m = tri_ref[...]
        upper_m = upper_ref[...]

        def blocks(js, masks):
            offs = [pl.multiple_of(j * bq, bq) for j in js]
            work = [(g, b) for b in range(len(js)) for g in range(ngrp)]
            zs = {(g, b): _nt(q2s[g], k_ref[pl.ds(offs[b], bq), cols[g]]) for g, b in work}
            lss = {}
            for g, b in work:
                z = zs[g, b]
                ls = -(jnp.maximum(z, 0.0) + jnp.log(1.0 + jnp.exp(-jnp.abs(z))))
                lss[g, b] = ls if masks[b] is None else jnp.where(masks[b], ls, 0.0)
            tails = {gb: _block_sums(lss[gb], tri_m) for gb in work}
            das = {(g, b): _nt(do2s[g], v_ref[pl.ds(offs[b], bq), cols[g]]) for g, b in work}
            probs, des = {}, {}
            for g, b in work:
                c = jnp.sum(jnp.where(lane == js[b], cmats[g], 0.0), axis=-1, keepdims=True)
                a = jnp.exp(zs[g, b] + lss[g, b] + tails[g, b] + c)
                a = a if masks[b] is None else jnp.where(masks[b], a, 0.0)
                probs[g, b] = a.astype(BF16)
                des[g, b] = das[g, b] * a
            prefixes = {gb: _block_sums(des[gb], upper_m) for gb in work}
            runs = [run_s[g] for g in range(ngrp)]
            dzs = {}
            for g, b in work:
                beta = jnp.exp(zs[g, b] + lss[g, b])
                dz = des[g, b] - beta * (des[g, b] + runs[g] + prefixes[g, b])
                dzs[g, b] = (dz if masks[b] is None else jnp.where(masks[b], dz, 0.0)).astype(BF16)
                runs[g] = runs[g] + prefixes[g, b][:, bq - 1:bq] + des[g, b][:, bq - 1:bq]
            for g in range(ngrp):
                run_s[g] = runs[g]
                dq = dq_s[g]
                for b in range(len(js)):
                    rows = pl.ds(offs[b], bq)
                    dq = dq + _nn(dzs[g, b], k_ref[rows, cols[g]])
                    dk_ref[rows, cols[g]] += _tn(dzs[g, b], q2s[g])
                    dv_ref[rows, cols[g]] += _tn(probs[g, b], do2s[g])
                dq_s[g] = dq

        n_done = jnp.max(carry_ref[:, COUNT_LANE:COUNT_LANE + 1]).astype(jnp.int32)
        n_done = jnp.clip(n_done, 1, i + 1)
        before = jnp.maximum(n_done - 2, 0)
        j0 = i - n_done + 1
        odd = before % 2
        run_s[...] = jnp.zeros_like(run_s)
        dq_s[...] = jnp.zeros_like(dq_s)

        @pl.when(odd == 1)
        def _():
            blocks([j0], [None])

        @pl.loop(0, before // 2)
        def _(t):
            blocks([j0 + odd + 2 * t, j0 + odd + 2 * t + 1], [None, None])

        @pl.when(n_done > 1)
        def _():
            blocks([i - 1, i], [None, vis])

        @pl.when(n_done <= 1)
        def _():
            blocks([i], [vis])

        for g in range(ngrp):
            dq_ref[:, cols[g]] = (_unstack_heads(dq_s[g], first_head) * Q_SCALE).astype(BF16)
        if plan is not None:
            @pl.when(jnp.logical_and(pl.program_id(0) == nstep - 1, i == nq - 1))
            def _():
                plan.finish(x_in, x_out, x_sems)

    width = ngrp * LANES
    qb = pl.BlockSpec((bq, width), lambda p, i: (i, p))
    colb = pl.BlockSpec((lp, width), lambda p, i: (0, p))
    sq = pl.BlockSpec((bq, bq), lambda p, i: (0, 0))
    outs = _pcall(
        body, name="attn_bwd" if plan is None else "attn_bwd_reduce", grid=(nstep, nq),
        in_specs=[qb,
                  pl.BlockSpec((lp, width), lambda p, i: (0, nstep + p)),
                  pl.BlockSpec((lp, width), lambda p, i: (0, 2 * nstep + p)),
                  pl.BlockSpec((bq, 2 * width), lambda p, i: (i, p)), qb, sq, sq] + x_in_specs,
        out_specs=[qb, colb, colb] + x_out_specs,
        out_shape=[jax.ShapeDtypeStruct((lp, D_SB), BF16), jax.ShapeDtypeStruct((lp, D_SB), F32),
                   jax.ShapeDtypeStruct((lp, D_SB), F32)] + x_out_shapes,
        scratch_shapes=[pltpu.VMEM((ngrp, 2 * bq, 1), F32), pltpu.VMEM((ngrp, 2 * bq, LANES), F32)] + x_scratch,
        compiler_params=_params("arbitrary", "arbitrary"),
    )(qkv, qkv, qkv, carries, datt, tri, upper, *x_args)
    return outs[0], outs[1], outs[2], outs[3:]


def _conv_bwd(dc4, c1, ew, cw, lng, lnb, wpw2):
    lp = ew.shape[0]
    tm = ROW_TILE
    nt = lp // tm
    halo_per_tile = tm // CONV_PAD

    def body(dc4_ref, c1_ref, ew_ref, halo_ref, cw_ref, lng_ref, lnb_ref, w_ref,
             dga_ref, dgb_ref, c3_ref, dcw_ref, dcb_ref, dlng_ref, dlnb_ref, xbuf, dbuf, shifted, wacc):
        step = pl.program_id(0)

        @pl.when(step == 0)
        def _():
            wacc[...] = jnp.zeros_like(wacc)
            dcb_ref[...] = jnp.zeros_like(dcb_ref)
            dlng_ref[...] = jnp.zeros_like(dlng_ref)
            dlnb_ref[...] = jnp.zeros_like(dlnb_ref)
            dbuf[tm:tm + CONV_PAD, :] = jnp.zeros((CONV_PAD, D_CONV), F32)

        dc3 = _nt(dc4_ref[...], w_ref[...])
        xhat, rstd = _layer_norm_stats(c1_ref[...])
        c2 = xhat * lng_ref[...] + lnb_ref[...]
        c3, dsilu = _silu_fwd_bwd(c2)
        c3_ref[...] = c3.astype(BF16)
        dc2 = dc3 * dsilu
        dlng_ref[...] += jnp.sum(dc2 * xhat, axis=0, keepdims=True)
        dlnb_ref[...] += jnp.sum(dc2, axis=0, keepdims=True)
        dxhat = dc2 * lng_ref[...]
        dc1 = rstd * (dxhat - jnp.mean(dxhat, axis=-1, keepdims=True)
                      - xhat * jnp.mean(dxhat * xhat, axis=-1, keepdims=True))
        dcb_ref[...] += jnp.sum(dc1, axis=0, keepdims=True)
        dbuf[0:tm, :] = dc1

        ga = ew_ref[:, 0:512]
        sgb = _sigmoid(ew_ref[:, 512:1024])
        xbuf[CONV_PAD:CONV_PAD + tm, :] = ga * sgb
        first_tile = step == nt - 1
        halo = halo_ref[:, 0:512] * _sigmoid(halo_ref[:, 512:1024])
        xbuf[0:CONV_PAD, :] = jnp.where(first_tile, 0.0, halo)

        for cb in range(D_CONV // LANES):
            cs = slice(cb * LANES, (cb + 1) * LANES)
            _shifted_copies(dbuf, cs, shifted, tm)
            for r0 in range(0, tm, CONV_ROWS):
                rs = slice(r0, r0 + CONV_ROWS)
                dc0 = jnp.zeros((CONV_ROWS, LANES), F32)
                for j in range(CONV_WIDTH):
                    dc0 = dc0 + _weighted(cw_ref, j, cs, _shifted_rows(
                        shifted, r0 + CONV_WIDTH - 1 - j, CONV_ROWS))
                dga_ref[rs, cs] = (dc0 * sgb[rs, cs]).astype(BF16)
                dgb_ref[rs, cs] = (dc0 * ga[rs, cs] * sgb[rs, cs] * (1.0 - sgb[rs, cs])).astype(BF16)
            _shifted_copies(xbuf, cs, shifted, tm)
            for r0 in range(0, tm, CONV_ROWS):
                d1 = dbuf[r0:r0 + CONV_ROWS, cs]

                for j in range(CONV_WIDTH):
                    prod = d1 * _shifted_rows(shifted, r0 + CONV_PAD - (CONV_WIDTH - 1) + j, CONV_ROWS)
                    wacc[j * 8:(j + 1) * 8, cs] += jnp.sum(prod.reshape(CONV_ROWS // 8, 8, LANES), axis=0)
        dbuf[tm:tm + CONV_PAD, :] = dbuf[0:CONV_PAD, :]

        @pl.when(step == nt - 1)
        def _():
            dcw_ref[...] = jnp.sum(wacc[...].reshape(CONV_PAD, 8, D_CONV), axis=1)

    rev = lambda i: (nt - 1 - i, 0)
    row = pl.BlockSpec((tm, D_CONV), rev)
    vec = pl.BlockSpec((1, D_CONV), lambda i: (0, 0))
    hb = jax.ShapeDtypeStruct((lp, D_CONV), BF16)
    vs = jax.ShapeDtypeStruct((1, D_CONV), F32)
    return _pcall(
        body, name="conv_bwd", grid=(nt,),
        in_specs=[row, row, pl.BlockSpec((tm, 1024), rev),
                  pl.BlockSpec((CONV_PAD, 1024), lambda i: (jnp.maximum((nt - 1 - i) * halo_per_tile - 1, 0), 0)),
                  pl.BlockSpec((8 * CONV_PAD, D_CONV), lambda i: (0, 0)), vec, vec,
                  pl.BlockSpec((D_CONV, D_CONV), lambda i: (0, 0))],
        out_specs=[row, row, row, pl.BlockSpec((CONV_PAD, D_CONV), lambda i: (0, 0)), vec, vec, vec],
        out_shape=[hb, hb, hb, jax.ShapeDtypeStruct((CONV_PAD, D_CONV), F32), vs, vs, vs],
        scratch_shapes=[pltpu.VMEM((tm + CONV_PAD, D_CONV), F32), pltpu.VMEM((tm + CONV_PAD, D_CONV), F32),
                        pltpu.VMEM((8, tm + CONV_PAD, LANES), F32), pltpu.VMEM((8 * CONV_PAD, D_CONV), F32)],
        compiler_params=_params("arbitrary"),
    )(dc4, c1, ew, ew, cw, lng, lnb, wpw2)


def _inproj_bwd(dga, dgb, dcg, dq, dk, dv, dsg, h, g, w, dh_out):
    lp, d = h.shape
    n = w.shape[1]
    tm = ROW_TILE

    def body(dga_ref, dgb_ref, dcg_ref, dq_ref, dk_ref, dv_ref, dsg_ref, h_ref, g_ref, w_ref, dho_ref,
             dh_ref, dproj_ref, u_ref, dg_ref):
        @pl.when(pl.program_id(0) == 0)
        def _():
            dg_ref[...] = jnp.zeros_like(dg_ref)

        dproj_ref[:, 0:512] = dga_ref[...]
        dproj_ref[:, 512:1024] = dgb_ref[...]
        dproj_ref[:, 1024:1536] = dcg_ref[...]
        dproj_ref[:, 1536:2048] = dq_ref[...]
        dproj_ref[:, 2048:2560] = dk_ref[...].astype(BF16)
        dproj_ref[:, 2560:3072] = dv_ref[...].astype(BF16)
        dproj_ref[:, 3072:3584] = dsg_ref[...]
        du = _nt(dproj_ref[...], w_ref[...])
        x = h_ref[...]
        rstd = lax.rsqrt(jnp.mean(x * x, axis=-1, keepdims=True) + RMS_EPS)
        nrm = x * rstd
        u_ref[...] = (nrm * g_ref[...]).astype(BF16)
        dg_ref[...] += jnp.sum(du * nrm, axis=0, keepdims=True)
        dn = du * g_ref[...]
        dh_ref[...] = dho_ref[...] + rstd * (dn - nrm * jnp.mean(dn * nrm, axis=-1, keepdims=True))

    half = pl.BlockSpec((tm, 512), lambda i: (i, 0))
    full = pl.BlockSpec((tm, d), lambda i: (i, 0))
    return _pcall(
        body, name="inproj_bwd", grid=(lp // tm,),
        in_specs=[half] * 7 + [full, pl.BlockSpec((1, d), lambda i: (0, 0)),
                               pl.BlockSpec((d, n), lambda i: (0, 0)), full],
        out_specs=[full, pl.BlockSpec((tm, n), lambda i: (i, 0)), full, pl.BlockSpec((1, d), lambda i: (0, 0))],
        out_shape=[jax.ShapeDtypeStruct((lp, d), F32), jax.ShapeDtypeStruct((lp, n), BF16),
                   jax.ShapeDtypeStruct((lp, d), BF16), jax.ShapeDtypeStruct((1, d), F32)],
        compiler_params=_params("arbitrary"),
    )(dga, dgb, dcg, dq, dk, dv, dsg, h, g, w, dh_out)


def _row_split(m, parts):
    tm = m // parts
    assert tm * parts == m and tm % 16 == 0, (m, parts)
    return tm


def _matmul_tn(x, dy, tn, name):
    m, k = x.shape
    n = dy.shape[1]
    steps = 4 if m % 64 == 0 else 1
    tm = _row_split(m, steps)

    def body(x_ref, dy_ref, o_ref, acc_ref):
        r = pl.program_id(1)

        @pl.when(r == 0)
        def _():
            acc_ref[...] = jnp.zeros_like(acc_ref)

        acc_ref[...] += _tn(x_ref[...], dy_ref[...])

        @pl.when(r == steps - 1)
        def _():
            o_ref[...] = acc_ref[...].astype(BF16)

    return _pcall(
        body, name=name, grid=(n // tn, steps),
        in_specs=[pl.BlockSpec((tm, k), lambda j, r: (r, 0)), pl.BlockSpec((tm, tn), lambda j, r: (r, j))],
        out_specs=pl.BlockSpec((k, tn), lambda j, r: (0, j)),
        out_shape=jax.ShapeDtypeStruct((k, n), BF16),
        scratch_shapes=[pltpu.VMEM((k, tn), F32)],
        compiler_params=_params("parallel", "arbitrary"),
    )(x, dy)


def _local_step(h0, target_p, seq, vecs, depth, all_weights=None, w_in0=None, gather_w_in=None, gather_rest=None,
                reduce_layer=None):
    pre_g, post_g, conv_b, ln_g, ln_b, b_pw2 = vecs
    ar = jnp.arange(ATT_BLOCK)
    tri = (ar[:, None] > ar[None, :]).astype(BF16)
    upper = (ar[:, None] < ar[None, :]).astype(BF16)
    row = lambda a, l: a[l][None, :]

    hosted = all_weights is None
    weights = [None] * depth if hosted else list(all_weights)
    next_w_in, next_rest = w_in0, None
    saved = []
    h = h0
    for l in range(depth):
        more = hosted and l + 1 < depth
        if hosted:
            ew, qkv, rest = _inproj(h, row(pre_g, l), next_w_in, gather_rest(0) if l == 0 else None)
            w_in, (w_pw2, w_out, conv_w) = next_w_in, (rest if l == 0 else next_rest)
        else:
            w_in, w_pw2, w_out, conv_w = weights[l]
            ew, qkv, _ = _inproj(h, row(pre_g, l), w_in)
        conv_w = jnp.repeat(conv_w, 8, axis=0)
        weights[l] = (w_in, w_pw2, w_out, conv_w)
        c1, c4, c5, next_rest = _conv_fwd(ew, conv_w, row(conv_b, l), row(ln_g, l), row(ln_b, l), w_pw2,
                                          row(b_pw2, l), gather_rest(l + 1) if more else None)
        att, carries, gathered = _attn_fwd(qkv, tri, gather_w_in(l + 1) if more else None)
        if more:
            next_w_in = gathered[0]
        hn, cat, mixed = _outproj(c5, att, ew, h, w_out, row(post_g, l))
        saved.append((h, ew, qkv, c1, c4, att, carries, cat, mixed))
        h = hn

    dh, loss = _loss_head(h, target_p, seq)

    vec_grads = [None] * depth
    mat_grads = [None] * depth
    pending = None
    for l in reversed(range(depth)):
        w_in, w_pw2, w_out, conv_w = weights[l]
        h_in, ew, qkv, c1, c4, att, carries, cat, mixed = saved[l]
        dmix, datt, dsg, dc4, dcg, dpost, dbpw2 = _outproj_bwd(dh, mixed, row(post_g, l), w_out, att, ew, c4)
        dw_out = _matmul_tn(cat, dmix, 512, "dw_out")
        dq, dk, dv, landed = _attn_bwd(qkv, carries, datt, tri, upper, pending)
        if pending is not None:
            mat_grads[l + 1] = landed
        dga, dgb, c3, dcw, dcb, dlng, dlnb = _conv_bwd(dc4, c1, ew, conv_w, row(ln_g, l), row(ln_b, l), w_pw2)
        dw_pw2 = _matmul_tn(c3, dc4, 512, "dw_pw2")
        dh, dproj, u, dpre = _inproj_bwd(dga, dgb, dcg, dq, dk, dv, dsg, h_in, row(pre_g, l), w_in, dh)
        dw_in = _matmul_tn(u, dproj, 1792, "dw_in")
        vec_grads[l] = (dpre[0], dpost[0], dcb[0], dlng[0], dlnb[0], dbpw2[0])
        mats = (dw_in, dw_pw2, dw_out, dcw)
        if reduce_layer is None:
            mat_grads[l] = mats
        else:
            pending = reduce_layer(mats)
    if pending is not None:
        mat_grads[0] = _run_exchange(pending, "reduce_grads")

    vec_grads = [jnp.stack([g[k] for g in vec_grads]) for k in range(len(vecs))]
    return loss[0, 0], dh, vec_grads, mat_grads


N_CHIPS = 4
ANY = pl.BlockSpec(memory_space=pl.ANY)


def _chip_peers():
    x, y, c = lax.axis_index("x"), lax.axis_index("y"), lax.axis_index("c")
    return x, y, c, [(x, 1 - y), (1 - x, y), (1 - x, 1 - y)]


def _shard_slices(refs, dims, idx):
    out = []
    for ref, (axis, size) in zip(refs, dims):
        assert size % LANES == 0
        start = pl.multiple_of(idx * size, LANES)
        sl = [slice(None)] * len(ref.shape)
        sl[axis] = pl.ds(start, size)
        out.append(ref.at[tuple(sl)])
    return out


class _Exchange(NamedTuple):
    inputs: list
    out_shapes: list
    scratch: list
    start: Callable
    relay: Callable
    finish: Callable


def _run_exchange(plan, name):
    n_in, n_out = len(plan.inputs), len(plan.out_shapes)

    def body(*refs):
        parts = refs[:n_in], refs[n_in:n_in + n_out], refs[n_in + n_out:]
        plan.start(*parts)
        plan.relay(*parts)
        plan.finish(*parts)

    return _pcall(body, name=name, in_specs=[ANY] * n_in, out_specs=[ANY] * n_out, out_shape=plan.out_shapes,
                  scratch_shapes=plan.scratch)(*plan.inputs)


def _gather_plan(shards, dims):
    n = len(shards)
    full_shapes = []
    halves = []
    for s, (axis, size) in zip(shards, dims):
        shp = list(s.shape)
        shp[axis] = size * N_CHIPS
        full_shapes.append(jax.ShapeDtypeStruct(tuple(shp), s.dtype))
        tile_rows = 32 // s.dtype.itemsize
        assert s.shape[0] % (2 * tile_rows) == 0
        halves.append((s.shape[0] // 2, tile_rows))

    def half(refs, which):
        return [r.at[pl.ds(pl.multiple_of(which * h, t), h)] for r, (h, t) in zip(refs, halves)]

    def copies(srcs, outs, sems):
        send, recv, loc = sems
        x, y, c, peers = _chip_peers()
        sibling = (x, y, 1 - c)
        mine = _shard_slices(outs, dims, 2 * x + y)
        local = [pltpu.make_async_copy(s, d, loc.at[a]) for a, (s, d) in enumerate(zip(srcs, mine))]

        def remote(src, dst, slot, a, dev):
            return pltpu.make_async_remote_copy(src, dst, send.at[slot, a], recv.at[slot, a],
                                                device_id=dev, device_id_type=MESH)

        sends = [remote(s, d, k, a, (px, py, c))
                 for k, (px, py) in enumerate(peers) for a, (s, d) in enumerate(zip(half(srcs, c), half(mine, c)))]
        theirs = [_shard_slices(outs, dims, 2 * px + py) for px, py in peers]
        arrivals = [remote(s, d, k, a, (px, py, c))
                    for k, (px, py) in enumerate(peers)
                    for a, (s, d) in enumerate(zip(half(srcs, c), half(theirs[k], c)))]
        passed_on = [remote(d, d, 3 + k, a, sibling) for k in range(3) for a, d in enumerate(half(theirs[k], c))]
        from_sibling = [remote(d, d, 3 + k, a, sibling)
                        for k in range(3) for a, d in enumerate(half(theirs[k], 1 - c))]
        return local, sends, arrivals, passed_on, from_sibling

    def start(srcs, outs, sems):
        local, sends = copies(srcs, outs, sems)[:2]
        for cp in local + sends:
            cp.start()

    def relay(srcs, outs, sems):
        _, _, arrivals, passed_on, _ = copies(srcs, outs, sems)
        for arrived, onward in zip(arrivals, passed_on):
            arrived.wait_recv()
            onward.start()

    def finish(srcs, outs, sems):
        local, sends, _, passed_on, from_sibling = copies(srcs, outs, sems)
        for cp in from_sibling:
            cp.wait_recv()
        for cp in sends + passed_on:
            cp.wait_send()
        for cp in local:
            cp.wait()

    scratch = [pltpu.SemaphoreType.DMA((6, n)), pltpu.SemaphoreType.DMA((6, n)), pltpu.SemaphoreType.DMA((n,))]
    return _Exchange(list(shards), full_shapes, scratch, start, relay, finish)


def _reduce_plan(grads, dims):
    n = len(grads)
    piece_shapes = []
    for g, (axis, size) in zip(grads, dims):
        shp = list(g.shape)
        shp[axis] = size
        piece_shapes.append(jax.ShapeDtypeStruct((N_CHIPS,) + tuple(shp), g.dtype))

    def copies(srcs, outs, sems):
        mine, theirs = outs[:n], outs[n:]
        send, recv, loc = sems
        x, y, c, peers = _chip_peers()
        sibling = (x, y, 1 - c)
        own = _shard_slices(srcs, dims, 2 * x + y)

        def remote(src, dst, slot, a, dev):
            return pltpu.make_async_remote_copy(src, dst, send.at[slot, a], recv.at[slot, a],
                                                device_id=dev, device_id_type=MESH)

        local = [pltpu.make_async_copy(own[a], mine[a].at[3], loc.at[a]) for a in range(n)]
        to_sibling = [remote(own[a], theirs[a].at[3], 3, a, sibling) for a in range(n)]
        to_chips = [remote(src, mine[a].at[k], k, a, (px, py, c))
                    for k, (px, py) in enumerate(peers)
                    for a, src in enumerate(_shard_slices(srcs, dims, 2 * px + py))]
        passed_on = [remote(mine[a].at[k], theirs[a].at[k], 4 + k, a, sibling) for k in range(3) for a in range(n)]
        return local, to_sibling, to_chips, passed_on

    def start(srcs, outs, sems):
        local, to_sibling, to_chips, _ = copies(srcs, outs, sems)
        for cp in local + to_sibling + to_chips:
            cp.start()

    def relay(srcs, outs, sems):
        _, _, to_chips, passed_on = copies(srcs, outs, sems)
        for arrived, onward in zip(to_chips, passed_on):
            arrived.wait_recv()
            onward.start()

    def finish(srcs, outs, sems):
        local, to_sibling, to_chips, passed_on = copies(srcs, outs, sems)
        for cp in to_sibling + passed_on:
            cp.wait_recv()
        for cp in to_sibling + to_chips + passed_on:
            cp.wait_send()
        for cp in local:
            cp.wait()

    scratch = [pltpu.SemaphoreType.DMA((7, n)), pltpu.SemaphoreType.DMA((7, n)), pltpu.SemaphoreType.DMA((n,))]
    return _Exchange(list(grads), piece_shapes * 2, scratch, start, relay, finish)


def _allsum_small(pack):
    rows, cols = pack.shape
    ndev = 8

    def body(p_ref, o_ref, buf, send, recv):
        x, y, c = lax.axis_index("x"), lax.axis_index("y"), lax.axis_index("c")
        me = 4 * x + 2 * y + c
        buf[me] = p_ref[...]
        started = []
        for r in range(1, ndev):
            bx, by, bc = (r >> 2) & 1, (r >> 1) & 1, r & 1
            dev = (x ^ bx, y ^ by, c ^ bc)
            cp = pltpu.make_async_remote_copy(p_ref, buf.at[me], send.at[r], recv.at[r],
                                              device_id=dev, device_id_type=MESH)
            cp.start()
            started.append(cp)
        for r in range(1, ndev):
            pltpu.make_async_remote_copy(p_ref, buf.at[me ^ r], send.at[r], recv.at[r],
                                         device_id=(x, y, c), device_id_type=MESH).wait_recv()
        for cp in started:
            cp.wait_send()
        acc = buf[0]
        for d in range(1, ndev):
            acc = acc + buf[d]
        o_ref[...] = acc

    vm = pl.BlockSpec(memory_space=pltpu.VMEM)
    return _pcall(
        body, name="allsum_small", in_specs=[vm], out_specs=vm,
        out_shape=jax.ShapeDtypeStruct((rows, cols), F32),
        scratch_shapes=[pltpu.VMEM((ndev, rows, cols), F32), pltpu.SemaphoreType.DMA((ndev,)),
                        pltpu.SemaphoreType.DMA((ndev,))],
    )(pack)


def _adamw(parts, w, m, v, layer, prev, name):
    _, rows, cols = w.shape
    tr = ROW_TILE if rows % ROW_TILE == 0 else rows
    counts = [p.shape[0] for p in parts]
    n_parts = len(parts)
    n_prev = 0 if prev is None else 4

    def body(*refs):
        part_refs = refs[:n_parts]
        w_ref, m_ref, v_ref = refs[n_parts:n_parts + 3]
        g_ref, d_ref, nm_ref, nv_ref = refs[n_parts + 3 + n_prev:]
        g = None
        for p_ref, cnt in zip(part_refs, counts):
            s = p_ref[0].astype(F32)
            for k in range(1, cnt):
                s = s + p_ref[k].astype(F32)
            g = s if g is None else g + s
        m2 = ADAM_B1 * m_ref[0] + (1.0 - ADAM_B1) * g
        v2 = ADAM_B2 * v_ref[0] + (1.0 - ADAM_B2) * (g * g)
        m_hat = m2 / (1.0 - ADAM_B1 ** ADAM_STEP)
        v_hat = v2 / (1.0 - ADAM_B2 ** ADAM_STEP)
        g_ref[0] = g
        d_ref[0] = -ADAM_LR * (m_hat / (jnp.sqrt(v_hat) + ADAM_EPS) + ADAM_WD * w_ref[0])
        nm_ref[0] = m2
        nv_ref[0] = v2

    blk = pl.BlockSpec((1, tr, cols), lambda i: (layer, i, 0))
    shp = jax.ShapeDtypeStruct(w.shape, F32)
    return _pcall(
        body, name=name, grid=(rows // tr,),
        in_specs=[pl.BlockSpec((cnt, tr, cols), lambda i: (0, i, 0)) for cnt in counts] + [blk] * 3 + [ANY] * n_prev,
        out_specs=[blk] * 4, out_shape=[shp] * 4,
        input_output_aliases={n_parts + 3 + k: k for k in range(n_prev)},
        compiler_params=_params("parallel"),
    )(*parts, w, m, v, *(prev or ()))


def kernel(x, meta_tokens, pre_norm_g, post_norm_g, w_in, conv_w, conv_b, conv_ln_g, conv_ln_b, w_pw2, b_pw2, w_out, loss_target, m_meta_tokens, m_pre_norm_g, m_post_norm_g, m_w_in, m_conv_w, m_conv_b, m_conv_ln_g, m_conv_ln_b, m_w_pw2, m_b_pw2, m_w_out, v_meta_tokens, v_pre_norm_g, v_post_norm_g, v_w_in, v_conv_w, v_conv_b, v_conv_ln_g, v_conv_ln_b, v_w_pw2, v_b_pw2, v_w_out):
    seq, d = x.shape[1], x.shape[2]
    depth = w_in.shape[0]
    length = N_META + seq
    lp = -(-length // ATT_BLOCK) * ATT_BLOCK
    tap_pad = ((0, 0), (0, CONV_PAD - CONV_WIDTH), (0, 0))

    shards = (w_in.astype(BF16), w_pw2.astype(BF16), w_out.astype(BF16), jnp.pad(conv_w, tap_pad))
    dims = [(1, w_in.shape[2]), (0, w_pw2.shape[1]), (0, w_out.shape[1]), (1, conv_w.shape[2])]
    layer_shards = lambda l: [s[l] for s in shards]

    w_in0, meta_f = _run_exchange(_gather_plan([shards[0][0], meta_tokens], [dims[0], (1, meta_tokens.shape[1])]),
                                  "gather_weights")

    h0 = jnp.concatenate([meta_f, x[0], jnp.zeros((lp - length, d), F32)], axis=0)
    target_p = jnp.pad(loss_target[0], ((N_META, lp - length), (0, 0)))
    vecs = (pre_norm_g, post_norm_g, conv_b, conv_ln_g, conv_ln_b, b_pw2)
    loss, dh0, vec_grads, pieces = _local_step(
        h0, target_p, seq, vecs, depth, w_in0=w_in0,
        gather_w_in=lambda l: _gather_plan([shards[0][l]], dims[:1]),
        gather_rest=lambda l: _gather_plan(layer_shards(l)[1:], dims[1:]),
        reduce_layer=lambda grads: _reduce_plan(list(grads), dims))

    def update(k, w, m, v, name):
        outs = None
        for l in reversed(range(depth)):
            outs = _adamw([pieces[l][k], pieces[l][4 + k]], w, m, v, l, outs, name)
        return outs

    up_w_in = update(0, w_in, m_w_in, v_w_in, "adamw_w_in")
    up_w_pw2 = update(1, w_pw2, m_w_pw2, v_w_pw2, "adamw_w_pw2")
    up_w_out = update(2, w_out, m_w_out, v_w_out, "adamw_w_out")
    up_conv_w = [o[:, :CONV_WIDTH] for o in update(3, jnp.pad(conv_w, tap_pad), jnp.pad(m_conv_w, tap_pad),
                                                   jnp.pad(v_conv_w, tap_pad, constant_values=1.0), "adamw_conv_w")]

    two = lambda a: a.reshape(-1, d)
    vec_rows = [two(g) for g in vec_grads]
    n_vec = sum(a.shape[0] for a in vec_rows)
    pack = jnp.concatenate(vec_rows + [dh0[:N_META], jnp.full((8, d), loss, F32)], axis=0)
    pack = jnp.pad(pack, ((0, -pack.shape[0] % 8), (0, 0)))
    tot = _allsum_small(pack)
    loss_all = tot[n_vec + N_META, 0]

    cat = lambda arrs: jnp.concatenate([two(t) for t in arrs], axis=0)[None]
    small_m = (m_pre_norm_g, m_post_norm_g, m_conv_b, m_conv_ln_g, m_conv_ln_b, m_b_pw2)
    small_v = (v_pre_norm_g, v_post_norm_g, v_conv_b, v_conv_ln_g, v_conv_ln_b, v_b_pw2)
    up_small = _adamw([tot[None, :n_vec]], cat(vecs), cat(small_m), cat(small_v), 0, None, "adamw_vectors")

    def unpack(o):
        res, r0 = [], 0
        for t in vecs:
            nrow = t.size // d
            res.append(o[0, r0:r0 + nrow].reshape(t.shape))
            r0 += nrow
        return res

    up_small = [unpack(o) for o in up_small]
    chip = 2 * lax.axis_index("x") + lax.axis_index("y")
    mcols = meta_tokens.shape[1]
    g_meta = lax.dynamic_slice_in_dim(tot[n_vec:n_vec + N_META], chip * mcols, mcols, axis=1)
    up_meta = [o[0] for o in _adamw([g_meta[None]], meta_tokens[None], m_meta_tokens[None], v_meta_tokens[None],
                                    0, None, "adamw_meta")]

    grad_x = dh0[N_META:length][None]
    outs = [loss_all, grad_x]
    for j in range(4):
        pre, post, cb, lg, lb, bp = up_small[j]
        outs += [up_meta[j], pre, post, up_w_in[j], up_conv_w[j], cb, lg, lb, up_w_pw2[j], bp, up_w_out[j]]
    return tuple(outs)
```

```python
from typing import Callable, NamedTuple

import jax
import jax.numpy as jnp
from jax import lax
from jax.experimental import pallas as pl
from jax.experimental.pallas import tpu as pltpu

F32 = jnp.float32
BF16 = jnp.bfloat16

N_META = 16
D_CONV = 512
D_SB = 512
HEAD_DIM = 64
CONV_WIDTH = 31
CONV_PAD = 32
CONV_ROWS = 128
RMS_EPS = 1e-6
LN_EPS = 1e-5
Q_SCALE = HEAD_DIM ** -0.5

ADAM_LR = 0.001
ADAM_B1 = 0.9
ADAM_B2 = 0.999
ADAM_EPS = 1e-08
ADAM_WD = 0.01
ADAM_STEP = 10

LANES = 128
ROW_TILE = 256
MM_TILE_MAX = 544
ATT_BLOCK = 256
ATT_PAIRS = 2
ATT_PAIRS_FWD = 4
VMEM_LIMIT = 56 * 1024 * 1024
EXP_ZERO = -104.0
COUNT_LANE = LANES - 1
RELAY_AT = 0.75

MESH = pl.DeviceIdType.MESH


def _pcall(body, **kw):
    return pl.pallas_call(body, **kw)


def _params(*sem):
    return pltpu.CompilerParams(dimension_semantics=sem, vmem_limit_bytes=VMEM_LIMIT)


def _sigmoid(x):
    return 1.0 / (1.0 + jnp.exp(-x))


def _silu_fwd_bwd(x):
    s = _sigmoid(x)
    return x * s, s * (1.0 + x * (1.0 - s))


def _nt(a, b):
    return lax.dot_general(a, b, (((1,), (1,)), ((), ())), preferred_element_type=F32)


def _tn(a, b):
    return lax.dot_general(a, b, (((0,), (0,)), ((), ())), preferred_element_type=F32)


def _nn(a, b):
    return jnp.dot(a, b, preferred_element_type=F32)


def _mm_tile(rows):
    return max(t for t in range(16, MM_TILE_MAX + 1, 16) if rows % t == 0)


def _host_begin(plan, parts, steps):
    if plan is not None:
        x_in, _, x_out, _, x_sems = parts

        @pl.when(pl.program_id(0) == 0)
        def _():
            plan.start(x_in, x_out, x_sems)

        @pl.when(pl.program_id(0) == int(RELAY_AT * steps))
        def _():
            plan.relay(x_in, x_out, x_sems)


def _host_end(plan, parts, steps):
    if plan is not None:
        x_in, _, x_out, _, x_sems = parts

        @pl.when(pl.program_id(0) == steps - 1)
        def _():
            plan.finish(x_in, x_out, x_sems)


def _inproj(h, g, w, plan=None):
    lp, d = h.shape
    n = w.shape[1]
    tm = _mm_tile(lp)
    steps = lp // tm
    split, x_args, x_in_specs, x_out_specs, x_out_shapes, x_scratch = _hosted(plan, 2, 0)

    def body(h_ref, g_ref, w_ref, *rest):
        parts = split(rest)
        ew_ref, qkv_ref = parts[1]
        _host_begin(plan, parts, steps)
        x = h_ref[...]
        rstd = lax.rsqrt(jnp.mean(x * x, axis=-1, keepdims=True) + RMS_EPS)
        u = ((x * rstd) * g_ref[...]).astype(BF16)
        p = _nn(u, w_ref[...])
        ew_ref[:, 0:1536] = p[:, 0:1536]
        ew_ref[:, 1536:2048] = p[:, 3072:3584]
        qkv_ref[:, 0:512] = (p[:, 1536:2048] * Q_SCALE).astype(BF16)
        qkv_ref[:, 512:1536] = p[:, 2048:3072].astype(BF16)
        _host_end(plan, parts, steps)

    outs = _pcall(
        body, name="inproj_fwd" if plan is None else "inproj_fwd_gather", grid=(steps,),
        in_specs=[pl.BlockSpec((tm, d), lambda i: (i, 0)),
                  pl.BlockSpec((1, d), lambda i: (0, 0)),
                  pl.BlockSpec((d, n), lambda i: (0, 0))] + x_in_specs,
        out_specs=[pl.BlockSpec((tm, 2048), lambda i: (i, 0)),
                   pl.BlockSpec((tm, 1536), lambda i: (i, 0))] + x_out_specs,
        out_shape=[jax.ShapeDtypeStruct((lp, 2048), F32), jax.ShapeDtypeStruct((lp, 1536), BF16)] + x_out_shapes,
        scratch_shapes=x_scratch,
        compiler_params=_params("parallel" if plan is None else "arbitrary"),
    )(h, g, w, *x_args)
    return outs[0], outs[1], outs[2:]


def _layer_norm_stats(c1):
    mu = jnp.mean(c1, axis=-1, keepdims=True)
    xc = c1 - mu
    var = jnp.mean(xc * xc, axis=-1, keepdims=True)
    rstd = lax.rsqrt(var + LN_EPS)
    return xc * rstd, rstd


def _shifted_copies(window, cols, shifted, tm):
    shifted[0] = window[:, cols]
    rows = tm + CONV_PAD - 8
    for b in range(1, 8):
        shifted[b, 0:rows, :] = window[pl.ds(b, rows), cols]


def _shifted_rows(shifted, shift, tm):
    b = shift % 8
    return shifted[b, pl.ds(pl.multiple_of(shift - b, 8), tm), :]


def _weighted(cw8_ref, j, cols, rows):
    w8 = cw8_ref[pl.ds(pl.multiple_of(j * 8, 8), 8), cols]
    r = rows.shape[0]
    return (rows.reshape(r // 8, 8, LANES) * w8[None]).reshape(r, LANES)


def _conv_fwd(ew, cw, cb, lng, lnb, wpw2, bpw2, plan=None):
    lp = ew.shape[0]
    tm = ROW_TILE
    steps = lp // tm
    split, x_args, x_in_specs, x_out_specs, x_out_shapes, x_scratch = _hosted(plan, 3, 2)

    def body(ew_ref, cw_ref, cb_ref, lng_ref, lnb_ref, w_ref, b_ref, *rest):
        parts = split(rest)
        (c1_ref, c4_ref, c5_ref), (xbuf, shifted) = parts[1], parts[3]
        _host_begin(plan, parts, steps)
        @pl.when(pl.program_id(0) == 0)
        def _():
            xbuf[0:CONV_PAD, :] = jnp.zeros((CONV_PAD, D_CONV), F32)

        ga = ew_ref[:, 0:512]
        gb = ew_ref[:, 512:1024]
        cg = ew_ref[:, 1024:1536]
        xbuf[CONV_PAD:CONV_PAD + tm, :] = ga * _sigmoid(gb)
        for blk in range(D_CONV // LANES):
            cs = slice(blk * LANES, (blk + 1) * LANES)
            _shifted_copies(xbuf, cs, shifted, tm)
            for r0 in range(0, tm, CONV_ROWS):
                acc = jnp.zeros((CONV_ROWS, LANES), F32) + cb_ref[:, cs]
                for j in range(CONV_WIDTH):
                    acc = acc + _weighted(cw_ref, j, cs, _shifted_rows(
                        shifted, r0 + CONV_PAD - (CONV_WIDTH - 1) + j, CONV_ROWS))
                c1_ref[r0:r0 + CONV_ROWS, cs] = acc
        xbuf[0:CONV_PAD, :] = xbuf[tm:tm + CONV_PAD, :]
        xhat, _ = _layer_norm_stats(c1_ref[...])
        c2 = xhat * lng_ref[...] + lnb_ref[...]
        c3 = c2 * _sigmoid(c2)
        c4 = _nn(c3.astype(BF16), w_ref[...]) + b_ref[...]
        c4_ref[...] = c4
        c5_ref[...] = (c4 * (cg * _sigmoid(cg))).astype(BF16)
        _host_end(plan, parts, steps)

    vec = pl.BlockSpec((1, D_CONV), lambda i: (0, 0))
    row = pl.BlockSpec((tm, D_CONV), lambda i: (i, 0))
    outs = _pcall(
        body, name="conv_fwd" if plan is None else "conv_fwd_gather", grid=(steps,),
        in_specs=[pl.BlockSpec((tm, 1536), lambda i: (i, 0)),
                  pl.BlockSpec((8 * CONV_PAD, D_CONV), lambda i: (0, 0)),
                  vec, vec, vec,
                  pl.BlockSpec((D_CONV, D_CONV), lambda i: (0, 0)),
                  vec] + x_in_specs,
        out_specs=[row, row, row] + x_out_specs,
        out_shape=[jax.ShapeDtypeStruct((lp, D_CONV), F32), jax.ShapeDtypeStruct((lp, D_CONV), F32),
                   jax.ShapeDtypeStruct((lp, D_CONV), BF16)] + x_out_shapes,
        scratch_shapes=[pltpu.VMEM((tm + CONV_PAD, D_CONV), F32),
                        pltpu.VMEM((8, tm + CONV_PAD, LANES), F32)] + x_scratch,
        compiler_params=_params("arbitrary"),
    )(ew, cw, cb, lng, lnb, wpw2, bpw2, *x_args)
    return outs[0], outs[1], outs[2], outs[3:]


def _block_sums(x, m01):
    return _nn(x.astype(BF16), m01)


def _attn_masks():
    lane = lax.broadcasted_iota(jnp.int32, (1, LANES), 1)
    row = lax.broadcasted_iota(jnp.int32, (2 * ATT_BLOCK, ATT_BLOCK), 0)
    col = lax.broadcasted_iota(jnp.int32, (2 * ATT_BLOCK, ATT_BLOCK), 1)
    return lane < HEAD_DIM, col < (row & (ATT_BLOCK - 1))


def _stack_heads(x, first_head):
    zero = jnp.zeros_like(x)
    return jnp.concatenate([jnp.where(first_head, x, zero), jnp.where(first_head, zero, x)], axis=0)


def _unstack_heads(x2, first_head):
    rows = x2.shape[0] // 2
    return jnp.where(first_head, x2[:rows], x2[rows:])


def _hosted(plan, n_out, n_scratch):
    n_in = 0 if plan is None else len(plan.inputs)
    n_x = 0 if plan is None else len(plan.out_shapes)

    def split(rest):
        a, b, c = n_in + n_out, n_in + n_out + n_x, n_in + n_out + n_x + n_scratch
        return rest[:n_in], rest[n_in:a], rest[a:b], rest[b:c], rest[c:]

    if plan is None:
        return split, [], [], [], [], []
    return split, list(plan.inputs), [ANY] * n_in, [ANY] * n_x, list(plan.out_shapes), list(plan.scratch)


def _attn_fwd(qkv, tri, plan=None):
    lp = qkv.shape[0]
    bq = ATT_BLOCK
    ngrp = ATT_PAIRS_FWD
    nstep = D_SB // (LANES * ngrp)
    nq = lp // bq
    assert nq <= COUNT_LANE
    split, x_args, x_in_specs, x_out_specs, x_out_shapes, x_scratch = _hosted(plan, 2, 3)

    def body(q_ref, k_ref, v_ref, tri_ref, *rest):
        x_in, (o_ref, carry_ref), x_out, (c_s, acc_s, cm_s), x_sems = split(rest)
        i = pl.program_id(1)
        if plan is not None:
            @pl.when(jnp.logical_and(pl.program_id(0) == 0, i == 0))
            def _():
                plan.start(x_in, x_out, x_sems)

            @pl.when(jnp.logical_and(pl.program_id(0) == nstep - 1, i == int(RELAY_AT * nq)))
            def _():
                plan.relay(x_in, x_out, x_sems)

        first_head, vis = _attn_masks()
        lane = lax.broadcasted_iota(jnp.int32, (1, LANES), 1)
        cols = [slice(g * LANES, (g + 1) * LANES) for g in range(ngrp)]
        q2s = [_stack_heads(q_ref[:, cs], first_head) for cs in cols]
        tri_m = tri_ref[...]

        c_s[...] = jnp.zeros_like(c_s)
        acc_s[...] = jnp.zeros_like(acc_s)
        cm_s[...] = jnp.zeros_like(cm_s)

        def blocks(js, masks):
            offs = [pl.multiple_of(j * bq, bq) for j in js]
            work = [(g, b) for b in range(len(js)) for g in range(ngrp)]
            zs = {(g, b): _nt(q2s[g], k_ref[pl.ds(offs[b], bq), cols[g]]) for g, b in work}
            lss = {}
            for g, b in work:
                z = zs[g, b]
                ls = -(jnp.maximum(z, 0.0) + jnp.log(1.0 + jnp.exp(-jnp.abs(z))))
                lss[g, b] = ls if masks[b] is None else jnp.where(masks[b], ls, 0.0)
            tails = {gb: _block_sums(lss[gb], tri_m) for gb in work}
            probs = {}
            carry = [c_s[g] for g in range(ngrp)]
            saved = [cm_s[g] for g in range(ngrp)]
            for g, b in work:
                a = jnp.exp(zs[g, b] + lss[g, b] + tails[g, b] + carry[g])
                probs[g, b] = (a if masks[b] is None else jnp.where(masks[b], a, 0.0)).astype(BF16)
                saved[g] = jnp.where(lane == js[b], carry[g], saved[g])
                carry[g] = carry[g] + tails[g, b][:, 0:1] + lss[g, b][:, 0:1]
            top = None
            for g in range(ngrp):
                c_s[g] = carry[g]
                cm_s[g] = saved[g]
                acc = acc_s[g]
                for b in range(len(js)):
                    acc = acc + _nn(probs[g, b], v_ref[pl.ds(offs[b], bq), cols[g]])
                acc_s[g] = acc
                top = carry[g] if top is None else jnp.maximum(top, carry[g])
            return jnp.max(top) > EXP_ZERO

        alive = lax.cond(i > 0, lambda: blocks([i, i - 1], [vis, None]), lambda: blocks([i], [vis]))
        rest = jnp.maximum(i - 1, 0)

        def pair(carry):
            t, _ = carry
            j = i - 2 - 2 * t
            return t + 1, blocks([j, j - 1], [None, None])

        trips, alive = lax.while_loop(lambda ca: jnp.logical_and(ca[0] < rest // 2, ca[1]), pair, (0, alive))
        last = jnp.logical_and(jnp.logical_and(rest % 2 == 1, trips == rest // 2), alive)

        @pl.when(last)
        def _():
            blocks([0], [None])

        n_done = (jnp.minimum(i + 1, 2) + 2 * trips + last.astype(jnp.int32)).astype(F32)
        for g in range(ngrp):
            cmat = jnp.where(lane == COUNT_LANE, n_done, cm_s[g])
            carry_ref[:, 2 * g * LANES:(2 * g + 1) * LANES] = cmat[:bq]
            carry_ref[:, (2 * g + 1) * LANES:(2 * g + 2) * LANES] = cmat[bq:]
            o_ref[:, cols[g]] = _unstack_heads(acc_s[g], first_head)
        if plan is not None:
            @pl.when(jnp.logical_and(pl.program_id(0) == nstep - 1, i == nq - 1))
            def _():
                plan.finish(x_in, x_out, x_sems)

    width = ngrp * LANES
    outs = _pcall(
        body, name="attn_fwd" if plan is None else "attn_fwd_gather", grid=(nstep, nq),
        in_specs=[pl.BlockSpec((bq, width), lambda p, i: (i, p)),
                  pl.BlockSpec((lp, width), lambda p, i: (0, nstep + p)),
                  pl.BlockSpec((lp, width), lambda p, i: (0, 2 * nstep + p)),
                  pl.BlockSpec((bq, bq), lambda p, i: (0, 0))] + x_in_specs,
        out_specs=[pl.BlockSpec((bq, width), lambda p, i: (i, p)),
                   pl.BlockSpec((bq, 2 * width), lambda p, i: (i, p))] + x_out_specs,
        out_shape=[jax.ShapeDtypeStruct((lp, D_SB), F32), jax.ShapeDtypeStruct((lp, 2 * D_SB), F32)] + x_out_shapes,
        scratch_shapes=[pltpu.VMEM((ngrp, 2 * bq, 1), F32), pltpu.VMEM((ngrp, 2 * bq, LANES), F32),
                        pltpu.VMEM((ngrp, 2 * bq, LANES), F32)] + x_scratch,
        compiler_params=_params("arbitrary", "arbitrary"),
    )(qkv, qkv, qkv, tri, *x_args)
    return outs[0], outs[1], outs[2:]


def _outproj(c5, att, ew, h, w, g):
    lp, d = h.shape
    tm = _mm_tile(lp)

    def body(c5_ref, att_ref, sg_ref, h_ref, w_ref, g_ref, hn_ref, cat_ref, mix_ref):
        sg = sg_ref[...]
        s = att_ref[...] * (sg * _sigmoid(sg))
        cat_ref[:, 0:D_CONV] = c5_ref[...]
        cat_ref[:, D_CONV:] = s.astype(BF16)
        mixed = _nn(cat_ref[...], w_ref[...])
        mix_ref[...] = mixed
        rstd = lax.rsqrt(jnp.mean(mixed * mixed, axis=-1, keepdims=True) + RMS_EPS)
        hn_ref[...] = h_ref[...] + (mixed * rstd) * g_ref[...]

    half = pl.BlockSpec((tm, 512), lambda i: (i, 0))
    full = pl.BlockSpec((tm, d), lambda i: (i, 0))
    return _pcall(
        body, name="outproj_fwd", grid=(lp // tm,),
        in_specs=[half, half, pl.BlockSpec((tm, 512), lambda i: (i, 3)), full,
                  pl.BlockSpec((d, d), lambda i: (0, 0)), pl.BlockSpec((1, d), lambda i: (0, 0))],
        out_specs=[full, full, full],
        out_shape=[jax.ShapeDtypeStruct((lp, d), F32), jax.ShapeDtypeStruct((lp, d), BF16),
                   jax.ShapeDtypeStruct((lp, d), F32)],
        compiler_params=_params("parallel"),
    )(c5, att, ew, h, w, g)


def _loss_head(h, target, seq):
    lp, d = h.shape
    tm = ROW_TILE

    def body(h_ref, t_ref, dh_ref, loss_ref):
        i = pl.program_id(0)

        @pl.when(i == 0)
        def _():
            loss_ref[...] = jnp.zeros_like(loss_ref)

        row = i * tm + lax.broadcasted_iota(jnp.int32, (tm, 1), 0)
        real = jnp.logical_and(row >= N_META, row < N_META + seq)
        diff = jnp.where(real, h_ref[...] - t_ref[...], 0.0)
        dh_ref[...] = diff * (1.0 / d)
        loss_ref[...] += 0.5 * jnp.sum(jnp.sum(diff * diff, axis=-1, keepdims=True) * (1.0 / d))

    full = pl.BlockSpec((tm, d), lambda i: (i, 0))
    return _pcall(
        body, name="loss_head", grid=(lp // tm,),
        in_specs=[full, full],
        out_specs=[full, pl.BlockSpec((8, LANES), lambda i: (0, 0))],
        out_shape=[jax.ShapeDtypeStruct((lp, d), F32), jax.ShapeDtypeStruct((8, LANES), F32)],
        compiler_params=_params("arbitrary"),
    )(h, target)


def _outproj_bwd(dh, mixed, g, w, att, ew, c4):
    lp, d = dh.shape
    tm = _mm_tile(lp)

    def body(dh_ref, mix_ref, g_ref, w_ref, att_ref, cg_ref, sg_ref, c4_ref,
             dmix_ref, datt_ref, dsg_ref, dc4_ref, dcg_ref, dg_ref, db_ref):
        @pl.when(pl.program_id(0) == 0)
        def _():
            dg_ref[...] = jnp.zeros_like(dg_ref)
            db_ref[...] = jnp.zeros_like(db_ref)

        mixed = mix_ref[...]
        dhv = dh_ref[...]
        rstd = lax.rsqrt(jnp.mean(mixed * mixed, axis=-1, keepdims=True) + RMS_EPS)
        n = mixed * rstd
        dg_ref[...] += jnp.sum(dhv * n, axis=0, keepdims=True)
        dn = dhv * g_ref[...]
        dmix = (rstd * (dn - n * jnp.mean(dn * n, axis=-1, keepdims=True))).astype(BF16)
        dmix_ref[...] = dmix
        dcat = _nt(dmix, w_ref[...])
        dc5 = dcat[:, 0:D_CONV]
        ds = dcat[:, D_CONV:]
        silu_sg, dsilu_sg = _silu_fwd_bwd(sg_ref[...])
        datt_ref[...] = (ds * silu_sg).astype(BF16)
        dsg_ref[...] = (ds * att_ref[...] * dsilu_sg).astype(BF16)
        silu_cg, dsilu_cg = _silu_fwd_bwd(cg_ref[...])
        dc4 = dc5 * silu_cg
        db_ref[...] += jnp.sum(dc4, axis=0, keepdims=True)
        dc4_ref[...] = dc4.astype(BF16)
        dcg_ref[...] = (dc5 * c4_ref[...] * dsilu_cg).astype(BF16)

    half = pl.BlockSpec((tm, 512), lambda i: (i, 0))
    full = pl.BlockSpec((tm, d), lambda i: (i, 0))
    hb = jax.ShapeDtypeStruct((lp, 512), BF16)
    return _pcall(
        body, name="outproj_bwd", grid=(lp // tm,),
        in_specs=[full, full, pl.BlockSpec((1, d), lambda i: (0, 0)), pl.BlockSpec((d, d), lambda i: (0, 0)),
                  half, pl.BlockSpec((tm, 512), lambda i: (i, 2)), pl.BlockSpec((tm, 512), lambda i: (i, 3)), half],
        out_specs=[full, half, half, half, half,
                   pl.BlockSpec((1, d), lambda i: (0, 0)), pl.BlockSpec((1, 512), lambda i: (0, 0))],
        out_shape=[jax.ShapeDtypeStruct((lp, d), BF16), hb, hb, hb, hb,
                   jax.ShapeDtypeStruct((1, d), F32), jax.ShapeDtypeStruct((1, 512), F32)],
        compiler_params=_params("arbitrary"),
    )(dh, mixed, g, w, att, ew, ew, c4)


def _attn_bwd(qkv, carries, datt, tri, upper, plan=None):
    lp = qkv.shape[0]
    bq = ATT_BLOCK
    ngrp = ATT_PAIRS
    nstep = D_SB // (LANES * ngrp)
    nq = lp // bq
    split, x_args, x_in_specs, x_out_specs, x_out_shapes, x_scratch = _hosted(plan, 3, 2)

    def body(q_ref, k_ref, v_ref, carry_ref, do_ref, tri_ref, upper_ref, *rest):
        x_in, (dq_ref, dk_ref, dv_ref), x_out, (run_s, dq_s), x_sems = split(rest)
        i = pl.program_id(1)
        if plan is not None:
            @pl.when(jnp.logical_and(pl.program_id(0) == 0, i == 0))
            def _():
                plan.start(x_in, x_out, x_sems)

            @pl.when(jnp.logical_and(pl.program_id(0) == nstep - 1, i == int(RELAY_AT * nq)))
            def _():
                plan.relay(x_in, x_out, x_sems)

        @pl.when(i == 0)
        def _():
            dk_ref[...] = jnp.zeros_like(dk_ref)
            dv_ref[...] = jnp.zeros_like(dv_ref)

        first_head, vis = _attn_masks()
        lane = lax.broadcasted_iota(jnp.int32, (1, LANES), 1)
        cols = [slice(g * LANES, (g + 1) * LANES) for g in range(ngrp)]
        q2s = [_stack_heads(q_ref[:, cs], first_head) for cs in cols]
        do2s = [_stack_heads(do_ref[:, cs], first_head) for cs in cols]
        cmats = [jnp.concatenate([carry_ref[:, 2 * g * LANES:(2 * g + 1) * LANES],
                                  carry_ref[:, (2 * g + 1) * LANES:(2 * g + 2) * LANES]], axis=0)
                 for g in range(ngrp)]
        tri_m = tri_ref[...]
        upper_m = upper_ref[...]

        def blocks(js, masks):
            offs = [pl.multiple_of(j * bq, bq) for j in js]
            work = [(g, b) for b in range(len(js)) for g in range(ngrp)]
            zs = {(g, b): _nt(q2s[g], k_ref[pl.ds(offs[b], bq), cols[g]]) for g, b in work}
            lss = {}
            for g, b in work:
                z = zs[g, b]
                ls = -(jnp.maximum(z, 0.0) + jnp.log(1.0 + jnp.exp(-jnp.abs(z))))
                lss[g, b] = ls if masks[b] is None else jnp.where(masks[b], ls, 0.0)
            tails = {gb: _block_sums(lss[gb], tri_m) for gb in work}
            das = {(g, b): _nt(do2s[g], v_ref[pl.ds(offs[b], bq), cols[g]]) for g, b in work}
            probs, des = {}, {}
            for g, b in work:
                c = jnp.sum(jnp.where(lane == js[b], cmats[g], 0.0), axis=-1, keepdims=True)
                a = jnp.exp(zs[g, b] + lss[g, b] + tails[g, b] + c)
                a = a if masks[b] is None else jnp.where(masks[b], a, 0.0)
                probs[g, b] = a.astype(BF16)
                des[g, b] = das[g, b] * a
            prefixes = {gb: _block_sums(des[gb], upper_m) for gb in work}
            runs = [run_s[g] for g in range(ngrp)]
            dzs = {}
            for g, b in work:
                beta = jnp.exp(zs[g, b] + lss[g, b])
                dz = des[g, b] - beta * (des[g, b] + runs[g] + prefixes[g, b])
                dzs[g, b] = (dz if masks[b] is None else jnp.where(masks[b], dz, 0.0)).astype(BF16)
                runs[g] = runs[g] + prefixes[g, b][:, bq - 1:bq] + des[g, b][:, bq - 1:bq]
            for g in range(ngrp):
                run_s[g] = runs[g]
                dq = dq_s[g]
                for b in range(len(js)):
                    rows = pl.ds(offs[b], bq)
                    dq = dq + _nn(dzs[g, b], k_ref[rows, cols[g]])
                    dk_ref[rows, cols[g]] += _tn(dzs[g, b], q2s[g])
                    dv_ref[rows, cols[g]] += _tn(probs[g, b], do2s[g])
                dq_s[g] = dq

        n_done = jnp.max(carry_ref[:, COUNT_LANE:COUNT_LANE + 1]).astype(jnp.int32)
        n_done = jnp.clip(n_done, 1, i + 1)
        before = jnp.maximum(n_done - 2, 0)
        j0 = i - n_done + 1
        odd = before % 2
        run_s[...] = jnp.zeros_like(run_s)
        dq_s[...] = jnp.zeros_like(dq_s)

        @pl.when(odd == 1)
        def _():
            blocks([j0], [None])

        @pl.loop(0, before // 2)
        def _(t):
            blocks([j0 + odd + 2 * t, j0 + odd + 2 * t + 1], [None, None])

        @pl.when(n_done > 1)
        def _():
            blocks([i - 1, i], [None, vis])

        @pl.when(n_done <= 1)
        def _():
            blocks([i], [vis])

        for g in range(ngrp):
            dq_ref[:, cols[g]] = (_unstack_heads(dq_s[g], first_head) * Q_SCALE).astype(BF16)
        if plan is not None:
            @pl.when(jnp.logical_and(pl.program_id(0) == nstep - 1, i == nq - 1))
            def _():
                plan.finish(x_in, x_out, x_sems)

    width = ngrp * LANES
    qb = pl.BlockSpec((bq, width), lambda p, i: (i, p))
    colb = pl.BlockSpec((lp, width), lambda p, i: (0, p))
    sq = pl.BlockSpec((bq, bq), lambda p, i: (0, 0))
    outs = _pcall(
        body, name="attn_bwd" if plan is None else "attn_bwd_reduce", grid=(nstep, nq),
        in_specs=[qb,
                  pl.BlockSpec((lp, width), lambda p, i: (0, nstep + p)),
                  pl.BlockSpec((lp, width), lambda p, i: (0, 2 * nstep + p)),
                  pl.BlockSpec((bq, 2 * width), lambda p, i: (i, p)), qb, sq, sq] + x_in_specs,
        out_specs=[qb, colb, colb] + x_out_specs,
        out_shape=[jax.ShapeDtypeStruct((lp, D_SB), BF16), jax.ShapeDtypeStruct((lp, D_SB), F32),
                   jax.ShapeDtypeStruct((lp, D_SB), F32)] + x_out_shapes,
        scratch_shapes=[pltpu.VMEM((ngrp, 2 * bq, 1), F32), pltpu.VMEM((ngrp, 2 * bq, LANES), F32)] + x_scratch,
        compiler_params=_params("arbitrary", "arbitrary"),
    )(qkv, qkv, qkv, carries, datt, tri, upper, *x_args)
    return outs[0], outs[1], outs[2], outs[3:]


def _conv_bwd(dc4, c1, ew, cw, lng, lnb, wpw2):
    lp = ew.shape[0]
    tm = ROW_TILE
    nt = lp // tm
    halo_per_tile = tm // CONV_PAD

    def body(dc4_ref, c1_ref, ew_ref, halo_ref, cw_ref, lng_ref, lnb_ref, w_ref,
             dga_ref, dgb_ref, c3_ref, dcw_ref, dcb_ref, dlng_ref, dlnb_ref, xbuf, dbuf, shifted, wacc):
        step = pl.program_id(0)

        @pl.when(step == 0)
        def _():
            wacc[...] = jnp.zeros_like(wacc)
            dcb_ref[...] = jnp.zeros_like(dcb_ref)
            dlng_ref[...] = jnp.zeros_like(dlng_ref)
            dlnb_ref[...] = jnp.zeros_like(dlnb_ref)
            dbuf[tm:tm + CONV_PAD, :] = jnp.zeros((CONV_PAD, D_CONV), F32)

        dc3 = _nt(dc4_ref[...], w_ref[...])
        xhat, rstd = _layer_norm_stats(c1_ref[...])
        c2 = xhat * lng_ref[...] + lnb_ref[...]
        c3, dsilu = _silu_fwd_bwd(c2)
        c3_ref[...] = c3.astype(BF16)
        dc2 = dc3 * dsilu
        dlng_ref[...] += jnp.sum(dc2 * xhat, axis=0, keepdims=True)
        dlnb_ref[...] += jnp.sum(dc2, axis=0, keepdims=True)
        dxhat = dc2 * lng_ref[...]
        dc1 = rstd * (dxhat - jnp.mean(dxhat, axis=-1, keepdims=True)
                      - xhat * jnp.mean(dxhat * xhat, axis=-1, keepdims=True))
        dcb_ref[...] += jnp.sum(dc1, axis=0, keepdims=True)
        dbuf[0:tm, :] = dc1

        ga = ew_ref[:, 0:512]
        sgb = _sigmoid(ew_ref[:, 512:1024])
        xbuf[CONV_PAD:CONV_PAD + tm, :] = ga * sgb
        first_tile = step == nt - 1
        halo = halo_ref[:, 0:512] * _sigmoid(halo_ref[:, 512:1024])
        xbuf[0:CONV_PAD, :] = jnp.where(first_tile, 0.0, halo)

        for cb in range(D_CONV // LANES):
            cs = slice(cb * LANES, (cb + 1) * LANES)
            _shifted_copies(dbuf, cs, shifted, tm)
            for r0 in range(0, tm, CONV_ROWS):
                rs = slice(r0, r0 + CONV_ROWS)
                dc0 = jnp.zeros((CONV_ROWS, LANES), F32)
                for j in range(CONV_WIDTH):
                    dc0 = dc0 + _weighted(cw_ref, j, cs, _shifted_rows(
                        shifted, r0 + CONV_WIDTH - 1 - j, CONV_ROWS))
                dga_ref[rs, cs] = (dc0 * sgb[rs, cs]).astype(BF16)
                dgb_ref[rs, cs] = (dc0 * ga[rs, cs] * sgb[rs, cs] * (1.0 - sgb[rs, cs])).astype(BF16)
            _shifted_copies(xbuf, cs, shifted, tm)
            for r0 in range(0, tm, CONV_ROWS):
                d1 = dbuf[r0:r0 + CONV_ROWS, cs]

                for j in range(CONV_WIDTH):
                    prod = d1 * _shifted_rows(shifted, r0 + CONV_PAD - (CONV_WIDTH - 1) + j, CONV_ROWS)
                    wacc[j * 8:(j + 1) * 8, cs] += jnp.sum(prod.reshape(CONV_ROWS // 8, 8, LANES), axis=0)
        dbuf[tm:tm + CONV_PAD, :] = dbuf[0:CONV_PAD, :]

        @pl.when(step == nt - 1)
        def _():
            dcw_ref[...] = jnp.sum(wacc[...].reshape(CONV_PAD, 8, D_CONV), axis=1)

    rev = lambda i: (nt - 1 - i, 0)
    row = pl.BlockSpec((tm, D_CONV), rev)
    vec = pl.BlockSpec((1, D_CONV), lambda i: (0, 0))
    hb = jax.ShapeDtypeStruct((lp, D_CONV), BF16)
    vs = jax.ShapeDtypeStruct((1, D_CONV), F32)
    return _pcall(
        body, name="conv_bwd", grid=(nt,),
        in_specs=[row, row, pl.BlockSpec((tm, 1024), rev),
                  pl.BlockSpec((CONV_PAD, 1024), lambda i: (jnp.maximum((nt - 1 - i) * halo_per_tile - 1, 0), 0)),
                  pl.BlockSpec((8 * CONV_PAD, D_CONV), lambda i: (0, 0)), vec, vec,
                  pl.BlockSpec((D_CONV, D_CONV), lambda i: (0, 0))],
        out_specs=[row, row, row, pl.BlockSpec((CONV_PAD, D_CONV), lambda i: (0, 0)), vec, vec, vec],
        out_shape=[hb, hb, hb, jax.ShapeDtypeStruct((CONV_PAD, D_CONV), F32), vs, vs, vs],
        scratch_shapes=[pltpu.VMEM((tm + CONV_PAD, D_CONV), F32), pltpu.VMEM((tm + CONV_PAD, D_CONV), F32),
                        pltpu.VMEM((8, tm + CONV_PAD, LANES), F32), pltpu.VMEM((8 * CONV_PAD, D_CONV), F32)],
        compiler_params=_params("arbitrary"),
    )(dc4, c1, ew, ew, cw, lng, lnb, wpw2)


def _inproj_bwd(dga, dgb, dcg, dq, dk, dv, dsg, h, g, w, dh_out):
    lp, d = h.shape
    n = w.shape[1]
    tm = _mm_tile(lp)

    def body(dga_ref, dgb_ref, dcg_ref, dq_ref, dk_ref, dv_ref, dsg_ref, h_ref, g_ref, w_ref, dho_ref,
             dh_ref, dproj_ref, u_ref, dg_ref):
        @pl.when(pl.program_id(0) == 0)
        def _():
            dg_ref[...] = jnp.zeros_like(dg_ref)

        dproj_ref[:, 0:512] = dga_ref[...]
        dproj_ref[:, 512:1024] = dgb_ref[...]
        dproj_ref[:, 1024:1536] = dcg_ref[...]
        dproj_ref[:, 1536:2048] = dq_ref[...]
        dproj_ref[:, 2048:2560] = dk_ref[...].astype(BF16)
        dproj_ref[:, 2560:3072] = dv_ref[...].astype(BF16)
        dproj_ref[:, 3072:3584] = dsg_ref[...]
        du = _nt(dproj_ref[...], w_ref[...])
        x = h_ref[...]
        rstd = lax.rsqrt(jnp.mean(x * x, axis=-1, keepdims=True) + RMS_EPS)
        nrm = x * rstd
        u_ref[...] = (nrm * g_ref[...]).astype(BF16)
        dg_ref[...] += jnp.sum(du * nrm, axis=0, keepdims=True)
        dn = du * g_ref[...]
        dh_ref[...] = dho_ref[...] + rstd * (dn - nrm * jnp.mean(dn * nrm, axis=-1, keepdims=True))

    half = pl.BlockSpec((tm, 512), lambda i: (i, 0))
    full = pl.BlockSpec((tm, d), lambda i: (i, 0))
    return _pcall(
        body, name="inproj_bwd", grid=(lp // tm,),
        in_specs=[half] * 7 + [full, pl.BlockSpec((1, d), lambda i: (0, 0)),
                               pl.BlockSpec((d, n), lambda i: (0, 0)), full],
        out_specs=[full, pl.BlockSpec((tm, n), lambda i: (i, 0)), full, pl.BlockSpec((1, d), lambda i: (0, 0))],
        out_shape=[jax.ShapeDtypeStruct((lp, d), F32), jax.ShapeDtypeStruct((lp, n), BF16),
                   jax.ShapeDtypeStruct((lp, d), BF16), jax.ShapeDtypeStruct((1, d), F32)],
        compiler_params=_params("arbitrary"),
    )(dga, dgb, dcg, dq, dk, dv, dsg, h, g, w, dh_out)


def _row_split(m, parts):
    tm = m // parts
    assert tm * parts == m and tm % 16 == 0, (m, parts)
    return tm


def _matmul_tn(x, dy, tn, name):
    m, k = x.shape
    n = dy.shape[1]
    steps = 4 if m % 64 == 0 else 1
    tm = _row_split(m, steps)

    def body(x_ref, dy_ref, o_ref, acc_ref):
        r = pl.program_id(1)

        @pl.when(r == 0)
        def _():
            acc_ref[...] = jnp.zeros_like(acc_ref)

        acc_ref[...] += _tn(x_ref[...], dy_ref[...])

        @pl.when(r == steps - 1)
        def _():
            o_ref[...] = acc_ref[...].astype(BF16)

    return _pcall(
        body, name=name, grid=(n // tn, steps),
        in_specs=[pl.BlockSpec((tm, k), lambda j, r: (r, 0)), pl.BlockSpec((tm, tn), lambda j, r: (r, j))],
        out_specs=pl.BlockSpec((k, tn), lambda j, r: (0, j)),
        out_shape=jax.ShapeDtypeStruct((k, n), BF16),
        scratch_shapes=[pltpu.VMEM((k, tn), F32)],
        compiler_params=_params("parallel", "arbitrary"),
    )(x, dy)


def _local_step(h0, target_p, seq, vecs, depth, all_weights=None, w_in0=None, gather_w_in=None, gather_rest=None,
                reduce_layer=None):
    pre_g, post_g, conv_b, ln_g, ln_b, b_pw2 = vecs
    ar = jnp.arange(ATT_BLOCK)
    tri = (ar[:, None] > ar[None, :]).astype(BF16)
    upper = (ar[:, None] < ar[None, :]).astype(BF16)
    row = lambda a, l: a[l][None, :]

    hosted = all_weights is None
    weights = [None] * depth if hosted else list(all_weights)
    next_w_in, next_rest = w_in0, None
    saved = []
    h = h0
    for l in range(depth):
        more = hosted and l + 1 < depth
        if hosted:
            ew, qkv, rest = _inproj(h, row(pre_g, l), next_w_in, gather_rest(0) if l == 0 else None)
            w_in, (w_pw2, w_out, conv_w) = next_w_in, (rest if l == 0 else next_rest)
        else:
            w_in, w_pw2, w_out, conv_w = weights[l]
            ew, qkv, _ = _inproj(h, row(pre_g, l), w_in)
        conv_w = jnp.repeat(conv_w, 8, axis=0)
        weights[l] = (w_in, w_pw2, w_out, conv_w)
        c1, c4, c5, next_rest = _conv_fwd(ew, conv_w, row(conv_b, l), row(ln_g, l), row(ln_b, l), w_pw2,
                                          row(b_pw2, l), gather_rest(l + 1) if more else None)
        att, carries, gathered = _attn_fwd(qkv, tri, gather_w_in(l + 1) if more else None)
        if more:
            next_w_in = gathered[0]
        hn, cat, mixed = _outproj(c5, att, ew, h, w_out, row(post_g, l))
        saved.append((h, ew, qkv, c1, c4, att, carries, cat, mixed))
        h = hn

    dh, loss = _loss_head(h, target_p, seq)

    vec_grads = [None] * depth
    mat_grads = [None] * depth
    pending = None
    for l in reversed(range(depth)):
        w_in, w_pw2, w_out, conv_w = weights[l]
        h_in, ew, qkv, c1, c4, att, carries, cat, mixed = saved[l]
        dmix, datt, dsg, dc4, dcg, dpost, dbpw2 = _outproj_bwd(dh, mixed, row(post_g, l), w_out, att, ew, c4)
        dw_out = _matmul_tn(cat, dmix, 512, "dw_out")
        dq, dk, dv, landed = _attn_bwd(qkv, carries, datt, tri, upper, pending)
        if pending is not None:
            mat_grads[l + 1] = landed
        dga, dgb, c3, dcw, dcb, dlng, dlnb = _conv_bwd(dc4, c1, ew, conv_w, row(ln_g, l), row(ln_b, l), w_pw2)
        dw_pw2 = _matmul_tn(c3, dc4, 512, "dw_pw2")
        dh, dproj, u, dpre = _inproj_bwd(dga, dgb, dcg, dq, dk, dv, dsg, h_in, row(pre_g, l), w_in, dh)
        dw_in = _matmul_tn(u, dproj, 1792, "dw_in")
        vec_grads[l] = (dpre[0], dpost[0], dcb[0], dlng[0], dlnb[0], dbpw2[0])
        mats = (dw_in, dw_pw2, dw_out, dcw)
        if reduce_layer is None:
            mat_grads[l] = mats
        else:
            pending = reduce_layer(mats)
    if pending is not None:
        mat_grads[0] = _run_exchange(pending, "reduce_grads")

    vec_grads = [jnp.stack([g[k] for g in vec_grads]) for k in range(len(vecs))]
    return loss[0, 0], dh, vec_grads, mat_grads


N_CHIPS = 4
ANY = pl.BlockSpec(memory_space=pl.ANY)


def _chip_peers():
    x, y, c = lax.axis_index("x"), lax.axis_index("y"), lax.axis_index("c")
    return x, y, c, [(x, 1 - y), (1 - x, y), (1 - x, 1 - y)]


def _shard_slices(refs, dims, idx):
    out = []
    for ref, (axis, size) in zip(refs, dims):
        assert size % LANES == 0
        start = pl.multiple_of(idx * size, LANES)
        sl = [slice(None)] * len(ref.shape)
        sl[axis] = pl.ds(start, size)
        out.append(ref.at[tuple(sl)])
    return out


class _Exchange(NamedTuple):
    inputs: list
    out_shapes: list
    scratch: list
    start: Callable
    relay: Callable
    finish: Callable


def _run_exchange(plan, name):
    n_in, n_out = len(plan.inputs), len(plan.out_shapes)

    def body(*refs):
        parts = refs[:n_in], refs[n_in:n_in + n_out], refs[n_in + n_out:]
        plan.start(*parts)
        plan.relay(*parts)
        plan.finish(*parts)

    return _pcall(body, name=name, in_specs=[ANY] * n_in, out_specs=[ANY] * n_out, out_shape=plan.out_shapes,
                  scratch_shapes=plan.scratch)(*plan.inputs)


def _gather_plan(shards, dims):
    n = len(shards)
    full_shapes = []
    halves = []
    for s, (axis, size) in zip(shards, dims):
        shp = list(s.shape)
        shp[axis] = size * N_CHIPS
        full_shapes.append(jax.ShapeDtypeStruct(tuple(shp), s.dtype))
        tile_rows = 32 // s.dtype.itemsize
        assert s.shape[0] % (2 * tile_rows) == 0
        halves.append((s.shape[0] // 2, tile_rows))

    def half(refs, which):
        return [r.at[pl.ds(pl.multiple_of(which * h, t), h)] for r, (h, t) in zip(refs, halves)]

    def copies(srcs, outs, sems):
        send, recv, loc = sems
        x, y, c, peers = _chip_peers()
        sibling = (x, y, 1 - c)
        mine = _shard_slices(outs, dims, 2 * x + y)
        local = [pltpu.make_async_copy(s, d, loc.at[a]) for a, (s, d) in enumerate(zip(srcs, mine))]

        def remote(src, dst, slot, a, dev):
            return pltpu.make_async_remote_copy(src, dst, send.at[slot, a], recv.at[slot, a],
                                                device_id=dev, device_id_type=MESH)

        sends = [remote(s, d, k, a, (px, py, c))
                 for k, (px, py) in enumerate(peers) for a, (s, d) in enumerate(zip(half(srcs, c), half(mine, c)))]
        theirs = [_shard_slices(outs, dims, 2 * px + py) for px, py in peers]
        arrivals = [remote(s, d, k, a, (px, py, c))
                    for k, (px, py) in enumerate(peers)
                    for a, (s, d) in enumerate(zip(half(srcs, c), half(theirs[k], c)))]
        passed_on = [remote(d, d, 3 + k, a, sibling) for k in range(3) for a, d in enumerate(half(theirs[k], c))]
        from_sibling = [remote(d, d, 3 + k, a, sibling)
                        for k in range(3) for a, d in enumerate(half(theirs[k], 1 - c))]
        return local, sends, arrivals, passed_on, from_sibling

    def start(srcs, outs, sems):
        local, sends = copies(srcs, outs, sems)[:2]
        for cp in local + sends:
            cp.start()

    def relay(srcs, outs, sems):
        _, _, arrivals, passed_on, _ = copies(srcs, outs, sems)
        for arrived, onward in zip(arrivals, passed_on):
            arrived.wait_recv()
            onward.start()

    def finish(srcs, outs, sems):
        local, sends, _, passed_on, from_sibling = copies(srcs, outs, sems)
        for cp in from_sibling:
            cp.wait_recv()
        for cp in sends + passed_on:
            cp.wait_send()
        for cp in local:
            cp.wait()

    scratch = [pltpu.SemaphoreType.DMA((6, n)), pltpu.SemaphoreType.DMA((6, n)), pltpu.SemaphoreType.DMA((n,))]
    return _Exchange(list(shards), full_shapes, scratch, start, relay, finish)


def _reduce_plan(grads, dims):
    n = len(grads)
    piece_shapes = []
    for g, (axis, size) in zip(grads, dims):
        shp = list(g.shape)
        shp[axis] = size
        piece_shapes.append(jax.ShapeDtypeStruct((N_CHIPS,) + tuple(shp), g.dtype))

    def copies(srcs, outs, sems):
        mine, theirs = outs[:n], outs[n:]
        send, recv, loc = sems
        x, y, c, peers = _chip_peers()
        sibling = (x, y, 1 - c)
        own = _shard_slices(srcs, dims, 2 * x + y)

        def remote(src, dst, slot, a, dev):
            return pltpu.make_async_remote_copy(src, dst, send.at[slot, a], recv.at[slot, a],
                                                device_id=dev, device_id_type=MESH)

        local = [pltpu.make_async_copy(own[a], mine[a].at[3], loc.at[a]) for a in range(n)]
        to_sibling = [remote(own[a], theirs[a].at[3], 3, a, sibling) for a in range(n)]
        to_chips = [remote(src, mine[a].at[k], k, a, (px, py, c))
                    for k, (px, py) in enumerate(peers)
                    for a, src in enumerate(_shard_slices(srcs, dims, 2 * px + py))]
        passed_on = [remote(mine[a].at[k], theirs[a].at[k], 4 + k, a, sibling) for k in range(3) for a in range(n)]
        return local, to_sibling, to_chips, passed_on

    def start(srcs, outs, sems):
        local, to_sibling, to_chips, _ = copies(srcs, outs, sems)
        for cp in local + to_sibling + to_chips:
            cp.start()

    def relay(srcs, outs, sems):
        _, _, to_chips, passed_on = copies(srcs, outs, sems)
        for arrived, onward in zip(to_chips, passed_on):
            arrived.wait_recv()
            onward.start()

    def finish(srcs, outs, sems):
        local, to_sibling, to_chips, passed_on = copies(srcs, outs, sems)
        for cp in to_sibling + passed_on:
            cp.wait_recv()
        for cp in to_sibling + to_chips + passed_on:
            cp.wait_send()
        for cp in local:
            cp.wait()

    scratch = [pltpu.SemaphoreType.DMA((7, n)), pltpu.SemaphoreType.DMA((7, n)), pltpu.SemaphoreType.DMA((n,))]
    return _Exchange(list(grads), piece_shapes * 2, scratch, start, relay, finish)


def _allsum_small(pack):
    rows, cols = pack.shape
    ndev = 8

    def body(p_ref, o_ref, buf, send, recv):
        x, y, c = lax.axis_index("x"), lax.axis_index("y"), lax.axis_index("c")
        me = 4 * x + 2 * y + c
        buf[me] = p_ref[...]
        started = []
        for r in range(1, ndev):
            bx, by, bc = (r >> 2) & 1, (r >> 1) & 1, r & 1
            dev = (x ^ bx, y ^ by, c ^ bc)
            cp = pltpu.make_async_remote_copy(p_ref, buf.at[me], send.at[r], recv.at[r],
                                              device_id=dev, device_id_type=MESH)
            cp.start()
            started.append(cp)
        for r in range(1, ndev):
            pltpu.make_async_remote_copy(p_ref, buf.at[me ^ r], send.at[r], recv.at[r],
                                         device_id=(x, y, c), device_id_type=MESH).wait_recv()
        for cp in started:
            cp.wait_send()
        acc = buf[0]
        for d in range(1, ndev):
            acc = acc + buf[d]
        o_ref[...] = acc

    vm = pl.BlockSpec(memory_space=pltpu.VMEM)
    return _pcall(
        body, name="allsum_small", in_specs=[vm], out_specs=vm,
        out_shape=jax.ShapeDtypeStruct((rows, cols), F32),
        scratch_shapes=[pltpu.VMEM((ndev, rows, cols), F32), pltpu.SemaphoreType.DMA((ndev,)),
                        pltpu.SemaphoreType.DMA((ndev,))],
    )(pack)


def _adamw(parts, w, m, v, layer, prev, name):
    _, rows, cols = w.shape
    tr = ROW_TILE if rows % ROW_TILE == 0 else rows
    counts = [p.shape[0] for p in parts]
    n_parts = len(parts)
    n_prev = 0 if prev is None else 4

    def body(*refs):
        part_refs = refs[:n_parts]
        w_ref, m_ref, v_ref = refs[n_parts:n_parts + 3]
        g_ref, d_ref, nm_ref, nv_ref = refs[n_parts + 3 + n_prev:]
        g = None
        for p_ref, cnt in zip(part_refs, counts):
            s = p_ref[0].astype(F32)
            for k in range(1, cnt):
                s = s + p_ref[k].astype(F32)
            g = s if g is None else g + s
        m2 = ADAM_B1 * m_ref[0] + (1.0 - ADAM_B1) * g
        v2 = ADAM_B2 * v_ref[0] + (1.0 - ADAM_B2) * (g * g)
        m_hat = m2 / (1.0 - ADAM_B1 ** ADAM_STEP)
        v_hat = v2 / (1.0 - ADAM_B2 ** ADAM_STEP)
        g_ref[0] = g
        d_ref[0] = -ADAM_LR * (m_hat / (jnp.sqrt(v_hat) + ADAM_EPS) + ADAM_WD * w_ref[0])
        nm_ref[0] = m2
        nv_ref[0] = v2

    blk = pl.BlockSpec((1, tr, cols), lambda i: (layer, i, 0))
    shp = jax.ShapeDtypeStruct(w.shape, F32)
    return _pcall(
        body, name=name, grid=(rows // tr,),
        in_specs=[pl.BlockSpec((cnt, tr, cols), lambda i: (0, i, 0)) for cnt in counts] + [blk] * 3 + [ANY] * n_prev,
        out_specs=[blk] * 4, out_shape=[shp] * 4,
        input_output_aliases={n_parts + 3 + k: k for k in range(n_prev)},
        compiler_params=_params("parallel"),
    )(*parts, w, m, v, *(prev or ()))


def kernel(x, meta_tokens, pre_norm_g, post_norm_g, w_in, conv_w, conv_b, conv_ln_g, conv_ln_b, w_pw2, b_pw2, w_out, loss_target, m_meta_tokens, m_pre_norm_g, m_post_norm_g, m_w_in, m_conv_w, m_conv_b, m_conv_ln_g, m_conv_ln_b, m_w_pw2, m_b_pw2, m_w_out, v_meta_tokens, v_pre_norm_g, v_post_norm_g, v_w_in, v_conv_w, v_conv_b, v_conv_ln_g, v_conv_ln_b, v_w_pw2, v_b_pw2, v_w_out):
    seq, d = x.shape[1], x.shape[2]
    depth = w_in.shape[0]
    length = N_META + seq
    lp = -(-length // ATT_BLOCK) * ATT_BLOCK
    tap_pad = ((0, 0), (0, CONV_PAD - CONV_WIDTH), (0, 0))

    shards = (w_in.astype(BF16), w_pw2.astype(BF16), w_out.astype(BF16), jnp.pad(conv_w, tap_pad))
    dims = [(1, w_in.shape[2]), (0, w_pw2.shape[1]), (0, w_out.shape[1]), (1, conv_w.shape[2])]
    layer_shards = lambda l: [s[l] for s in shards]

    w_in0, meta_f = _run_exchange(_gather_plan([shards[0][0], meta_tokens], [dims[0], (1, meta_tokens.shape[1])]),
                                  "gather_weights")

    h0 = jnp.concatenate([meta_f, x[0], jnp.zeros((lp - length, d), F32)], axis=0)
    target_p = jnp.pad(loss_target[0], ((N_META, lp - length), (0, 0)))
    vecs = (pre_norm_g, post_norm_g, conv_b, conv_ln_g, conv_ln_b, b_pw2)
    loss, dh0, vec_grads, pieces = _local_step(
        h0, target_p, seq, vecs, depth, w_in0=w_in0,
        gather_w_in=lambda l: _gather_plan([shards[0][l]], dims[:1]),
        gather_rest=lambda l: _gather_plan(layer_shards(l)[1:], dims[1:]),
        reduce_layer=lambda grads: _reduce_plan(list(grads), dims))

    def update(k, w, m, v, name):
        outs = None
        for l in reversed(range(depth)):
            outs = _adamw([pieces[l][k], pieces[l][4 + k]], w, m, v, l, outs, name)
        return outs

    up_w_in = update(0, w_in, m_w_in, v_w_in, "adamw_w_in")
    up_w_pw2 = update(1, w_pw2, m_w_pw2, v_w_pw2, "adamw_w_pw2")
    up_w_out = update(2, w_out, m_w_out, v_w_out, "adamw_w_out")
    up_conv_w = [o[:, :CONV_WIDTH] for o in update(3, jnp.pad(conv_w, tap_pad), jnp.pad(m_conv_w, tap_pad),
                                                   jnp.pad(v_conv_w, tap_pad, constant_values=1.0), "adamw_conv_w")]

    two = lambda a: a.reshape(-1, d)
    vec_rows = [two(g) for g in vec_grads]
    n_vec = sum(a.shape[0] for a in vec_rows)
    pack = jnp.concatenate(vec_rows + [dh0[:N_META], jnp.full((8, d), loss, F32)], axis=0)
    pack = jnp.pad(pack, ((0, -pack.shape[0] % 8), (0, 0)))
    tot = _allsum_small(pack)
    loss_all = tot[n_vec + N_META, 0]

    cat = lambda arrs: jnp.concatenate([two(t) for t in arrs], axis=0)[None]
    small_m = (m_pre_norm_g, m_post_norm_g, m_conv_b, m_conv_ln_g, m_conv_ln_b, m_b_pw2)
    small_v = (v_pre_norm_g, v_post_norm_g, v_conv_b, v_conv_ln_g, v_conv_ln_b, v_b_pw2)
    up_small = _adamw([tot[None, :n_vec]], cat(vecs), cat(small_m), cat(small_v), 0, None, "adamw_vectors")

    def unpack(o):
        res, r0 = [], 0
        for t in vecs:
            nrow = t.size // d
            res.append(o[0, r0:r0 + nrow].reshape(t.shape))
            r0 += nrow
        return res

    up_small = [unpack(o) for o in up_small]
    chip = 2 * lax.axis_index("x") + lax.axis_index("y")
    mcols = meta_tokens.shape[1]
    g_meta = lax.dynamic_slice_in_dim(tot[n_vec:n_vec + N_META], chip * mcols, mcols, axis=1)
    up_meta = [o[0] for o in _adamw([g_meta[None]], meta_tokens[None], m_meta_tokens[None], v_meta_tokens[None],
                                    0, None, "adamw_meta")]

    grad_x = dh0[N_META:length][None]
    outs = [loss_all, grad_x]
    for j in range(4):
        pre, post, cb, lg, lb, bp = up_small[j]
        outs += [up_meta[j], pre, post, up_w_in[j], up_conv_w[j], cb, lg, lb, up_w_pw2[j], bp, up_w_out[j]]
    return tuple(outs)
```

```python
from typing import Callable, NamedTuple

import jax
import jax.numpy as jnp
from jax import lax
from jax.experimental import pallas as pl
from jax.experimental.pallas import tpu as pltpu

F32 = jnp.float32
BF16 = jnp.bfloat16

N_META = 16
D_CONV = 512
D_SB = 512
HEAD_DIM = 64
CONV_WIDTH = 31
CONV_PAD = 32
CONV_ROWS = 128
RMS_EPS = 1e-6
LN_EPS = 1e-5
Q_SCALE = HEAD_DIM ** -0.5

ADAM_LR = 0.001
ADAM_B1 = 0.9
ADAM_B2 = 0.999
ADAM_EPS = 1e-08
ADAM_WD = 0.01
ADAM_STEP = 10

LANES = 128
ROW_TILE = 256
MM_TILE_MAX = 544
ATT_BLOCK = 256
ATT_PAIRS = 2
ATT_PAIRS_FWD = 4
VMEM_LIMIT = 56 * 1024 * 1024
EXP_ZERO = -104.0
COUNT_LANE = LANES - 1
RELAY_AT = 0.75

MESH = pl.DeviceIdType.MESH


def _pcall(body, **kw):
    return pl.pallas_call(body, **kw)


def _params(*sem):
    return pltpu.CompilerParams(dimension_semantics=sem, vmem_limit_bytes=VMEM_LIMIT)


def _sigmoid(x):
    return 1.0 / (1.0 + jnp.exp(-x))


def _silu_fwd_bwd(x):
    s = _sigmoid(x)
    return x * s, s * (1.0 + x * (1.0 - s))


def _nt(a, b):
    return lax.dot_general(a, b, (((1,), (1,)), ((), ())), preferred_element_type=F32)


def _tn(a, b):
    return lax.dot_general(a, b, (((0,), (0,)), ((), ())), preferred_element_type=F32)


def _nn(a, b):
    return jnp.dot(a, b, preferred_element_type=F32)


def _mm_tile(rows):
    return max(t for t in range(16, MM_TILE_MAX + 1, 16) if rows % t == 0)


def _host_begin(plan, parts, steps):
    if plan is not None:
        x_in, _, x_out, _, x_sems = parts

        @pl.when(pl.program_id(0) == 0)
        def _():
            plan.start(x_in, x_out, x_sems)

        @pl.when(pl.program_id(0) == int(RELAY_AT * steps))
        def _():
            plan.relay(x_in, x_out, x_sems)


def _host_end(plan, parts, steps):
    if plan is not None:
        x_in, _, x_out, _, x_sems = parts

        @pl.when(pl.program_id(0) == steps - 1)
        def _():
            plan.finish(x_in, x_out, x_sems)


def _inproj(h, g, w, plan=None):
    lp, d = h.shape
    n = w.shape[1]
    tm = _mm_tile(lp)
    steps = lp // tm
    split, x_args, x_in_specs, x_out_specs, x_out_shapes, x_scratch = _hosted(plan, 2, 0)

    def body(h_ref, g_ref, w_ref, *rest):
        parts = split(rest)
        ew_ref, qkv_ref = parts[1]
        _host_begin(plan, parts, steps)
        x = h_ref[...]
        rstd = lax.rsqrt(jnp.mean(x * x, axis=-1, keepdims=True) + RMS_EPS)
        u = ((x * rstd) * g_ref[...]).astype(BF16)
        p = _nn(u, w_ref[...])
        ew_ref[:, 0:1536] = p[:, 0:1536].astype(BF16)
        ew_ref[:, 1536:2048] = p[:, 3072:3584].astype(BF16)
        qkv_ref[:, 0:512] = (p[:, 1536:2048] * Q_SCALE).astype(BF16)
        qkv_ref[:, 512:1536] = p[:, 2048:3072].astype(BF16)
        _host_end(plan, parts, steps)

    outs = _pcall(
        body, name="inproj_fwd" if plan is None else "inproj_fwd_gather", grid=(steps,),
        in_specs=[pl.BlockSpec((tm, d), lambda i: (i, 0)),
                  pl.BlockSpec((1, d), lambda i: (0, 0)),
                  pl.BlockSpec((d, n), lambda i: (0, 0))] + x_in_specs,
        out_specs=[pl.BlockSpec((tm, 2048), lambda i: (i, 0)),
                   pl.BlockSpec((tm, 1536), lambda i: (i, 0))] + x_out_specs,
        out_shape=[jax.ShapeDtypeStruct((lp, 2048), BF16), jax.ShapeDtypeStruct((lp, 1536), BF16)] + x_out_shapes,
        scratch_shapes=x_scratch,
        compiler_params=_params("parallel" if plan is None else "arbitrary"),
    )(h, g, w, *x_args)
    return outs[0], outs[1], outs[2:]


def _layer_norm_stats(c1):
    mu = jnp.mean(c1, axis=-1, keepdims=True)
    xc = c1 - mu
    var = jnp.mean(xc * xc, axis=-1, keepdims=True)
    rstd = lax.rsqrt(var + LN_EPS)
    return xc * rstd, rstd


def _shifted_copies(window, cols, shifted, tm):
    shifted[0] = window[:, cols]
    rows = tm + CONV_PAD - 8
    for b in range(1, 8):
        shifted[b, 0:rows, :] = window[pl.ds(b, rows), cols]


def _shifted_rows(shifted, shift, tm):
    b = shift % 8
    return shifted[b, pl.ds(pl.multiple_of(shift - b, 8), tm), :]


def _weighted(cw8_ref, j, cols, rows):
    w8 = cw8_ref[pl.ds(pl.multiple_of(j * 8, 8), 8), cols]
    r = rows.shape[0]
    return (rows.reshape(r // 8, 8, LANES) * w8[None]).reshape(r, LANES)


def _conv_fwd(ew, cw, cb, lng, lnb, wpw2, bpw2, plan=None):
    lp = ew.shape[0]
    tm = ROW_TILE
    steps = lp // tm
    split, x_args, x_in_specs, x_out_specs, x_out_shapes, x_scratch = _hosted(plan, 3, 2)

    def body(ew_ref, cw_ref, cb_ref, lng_ref, lnb_ref, w_ref, b_ref, *rest):
        parts = split(rest)
        (c1_ref, c4_ref, c5_ref), (xbuf, shifted) = parts[1], parts[3]
        _host_begin(plan, parts, steps)
        @pl.when(pl.program_id(0) == 0)
        def _():
            xbuf[0:CONV_PAD, :] = jnp.zeros((CONV_PAD, D_CONV), F32)

        ga = ew_ref[:, 0:512].astype(F32)
        gb = ew_ref[:, 512:1024].astype(F32)
        cg = ew_ref[:, 1024:1536].astype(F32)
        xbuf[CONV_PAD:CONV_PAD + tm, :] = ga * _sigmoid(gb)
        for blk in range(D_CONV // LANES):
            cs = slice(blk * LANES, (blk + 1) * LANES)
            _shifted_copies(xbuf, cs, shifted, tm)
            for r0 in range(0, tm, CONV_ROWS):
                acc = jnp.zeros((CONV_ROWS, LANES), F32) + cb_ref[:, cs]
                for j in range(CONV_WIDTH):
                    acc = acc + _weighted(cw_ref, j, cs, _shifted_rows(
                        shifted, r0 + CONV_PAD - (CONV_WIDTH - 1) + j, CONV_ROWS))
                c1_ref[r0:r0 + CONV_ROWS, cs] = acc
        xbuf[0:CONV_PAD, :] = xbuf[tm:tm + CONV_PAD, :]
        xhat, _ = _layer_norm_stats(c1_ref[...])
        c2 = xhat * lng_ref[...] + lnb_ref[...]
        c3 = c2 * _sigmoid(c2)
        c4 = _nn(c3.astype(BF16), w_ref[...]) + b_ref[...]
        c4_ref[...] = c4
        c5_ref[...] = (c4 * (cg * _sigmoid(cg))).astype(BF16)
        _host_end(plan, parts, steps)

    vec = pl.BlockSpec((1, D_CONV), lambda i: (0, 0))
    row = pl.BlockSpec((tm, D_CONV), lambda i: (i, 0))
    outs = _pcall(
        body, name="conv_fwd" if plan is None else "conv_fwd_gather", grid=(steps,),
        in_specs=[pl.BlockSpec((tm, 1536), lambda i: (i, 0)),
                  pl.BlockSpec((8 * CONV_PAD, D_CONV), lambda i: (0, 0)),
                  vec, vec, vec,
                  pl.BlockSpec((D_CONV, D_CONV), lambda i: (0, 0)),
                  vec] + x_in_specs,
        out_specs=[row, row, row] + x_out_specs,
        out_shape=[jax.ShapeDtypeStruct((lp, D_CONV), F32), jax.ShapeDtypeStruct((lp, D_CONV), F32),
                   jax.ShapeDtypeStruct((lp, D_CONV), BF16)] + x_out_shapes,
        scratch_shapes=[pltpu.VMEM((tm + CONV_PAD, D_CONV), F32),
                        pltpu.VMEM((8, tm + CONV_PAD, LANES), F32)] + x_scratch,
        compiler_params=_params("arbitrary"),
    )(ew, cw, cb, lng, lnb, wpw2, bpw2, *x_args)
    return outs[0], outs[1], outs[2], outs[3:]


def _block_sums(x, m01):
    return _nn(x.astype(BF16), m01)


def _attn_masks():
    lane = lax.broadcasted_iota(jnp.int32, (1, LANES), 1)
    row = lax.broadcasted_iota(jnp.int32, (2 * ATT_BLOCK, ATT_BLOCK), 0)
    col = lax.broadcasted_iota(jnp.int32, (2 * ATT_BLOCK, ATT_BLOCK), 1)
    return lane < HEAD_DIM, col < (row & (ATT_BLOCK - 1))


def _stack_heads(x, first_head):
    zero = jnp.zeros_like(x)
    return jnp.concatenate([jnp.where(first_head, x, zero), jnp.where(first_head, zero, x)], axis=0)


def _unstack_heads(x2, first_head):
    rows = x2.shape[0] // 2
    return jnp.where(first_head, x2[:rows], x2[rows:])


def _hosted(plan, n_out, n_scratch):
    n_in = 0 if plan is None else len(plan.inputs)
    n_x = 0 if plan is None else len(plan.out_shapes)

    def split(rest):
        a, b, c = n_in + n_out, n_in + n_out + n_x, n_in + n_out + n_x + n_scratch
        return rest[:n_in], rest[n_in:a], rest[a:b], rest[b:c], rest[c:]

    if plan is None:
        return split, [], [], [], [], []
    return split, list(plan.inputs), [ANY] * n_in, [ANY] * n_x, list(plan.out_shapes), list(plan.scratch)


def _attn_fwd(qkv, tri, plan=None):
    lp = qkv.shape[0]
    bq = ATT_BLOCK
    ngrp = ATT_PAIRS_FWD
    nstep = D_SB // (LANES * ngrp)
    nq = lp // bq
    assert nq <= COUNT_LANE
    split, x_args, x_in_specs, x_out_specs, x_out_shapes, x_scratch = _hosted(plan, 2, 3)

    def body(q_ref, k_ref, v_ref, tri_ref, *rest):
        x_in, (o_ref, carry_ref), x_out, (c_s, acc_s, cm_s), x_sems = split(rest)
        i = pl.program_id(1)
        if plan is not None:
            @pl.when(jnp.logical_and(pl.program_id(0) == 0, i == 0))
            def _():
                plan.start(x_in, x_out, x_sems)

            @pl.when(jnp.logical_and(pl.program_id(0) == nstep - 1, i == int(RELAY_AT * nq)))
            def _():
                plan.relay(x_in, x_out, x_sems)

        first_head, vis = _attn_masks()
        lane = lax.broadcasted_iota(jnp.int32, (1, LANES), 1)
        cols = [slice(g * LANES, (g + 1) * LANES) for g in range(ngrp)]
        q2s = [_stack_heads(q_ref[:, cs], first_head) for cs in cols]
        tri_m = tri_ref[...]

        c_s[...] = jnp.zeros_like(c_s)
        acc_s[...] = jnp.zeros_like(acc_s)
        cm_s[...] = jnp.zeros_like(cm_s)

        def blocks(js, masks):
            offs = [pl.multiple_of(j * bq, bq) for j in js]
            work = [(g, b) for b in range(len(js)) for g in range(ngrp)]
            zs = {(g, b): _nt(q2s[g], k_ref[pl.ds(offs[b], bq), cols[g]]) for g, b in work}
            lss = {}
            for g, b in work:
                z = zs[g, b]
                ls = -(jnp.maximum(z, 0.0) + jnp.log(1.0 + jnp.exp(-jnp.abs(z))))
                lss[g, b] = ls if masks[b] is None else jnp.where(masks[b], ls, 0.0)
            tails = {gb: _block_sums(lss[gb], tri_m) for gb in work}
            probs = {}
            carry = [c_s[g] for g in range(ngrp)]
            saved = [cm_s[g] for g in range(ngrp)]
            for g, b in work:
                a = jnp.exp(zs[g, b] + lss[g, b] + tails[g, b] + carry[g])
                probs[g, b] = (a if masks[b] is None else jnp.where(masks[b], a, 0.0)).astype(BF16)
                saved[g] = jnp.where(lane == js[b], carry[g], saved[g])
                carry[g] = carry[g] + tails[g, b][:, 0:1] + lss[g, b][:, 0:1]
            top = None
            for g in range(ngrp):
                c_s[g] = carry[g]
                cm_s[g] = saved[g]
                acc = acc_s[g]
                for b in range(len(js)):
                    acc = acc + _nn(probs[g, b], v_ref[pl.ds(offs[b], bq), cols[g]])
                acc_s[g] = acc
                top = carry[g] if top is None else jnp.maximum(top, carry[g])
            return jnp.max(top) > EXP_ZERO

        alive = lax.cond(i > 0, lambda: blocks([i, i - 1], [vis, None]), lambda: blocks([i], [vis]))
        rest = jnp.maximum(i - 1, 0)

        def pair(carry):
            t, _ = carry
            j = i - 2 - 2 * t
            return t + 1, blocks([j, j - 1], [None, None])

        trips, alive = lax.while_loop(lambda ca: jnp.logical_and(ca[0] < rest // 2, ca[1]), pair, (0, alive))
        last = jnp.logical_and(jnp.logical_and(rest % 2 == 1, trips == rest // 2), alive)

        @pl.when(last)
        def _():
            blocks([0], [None])

        n_done = (jnp.minimum(i + 1, 2) + 2 * trips + last.astype(jnp.int32)).astype(F32)
        for g in range(ngrp):
            cmat = jnp.where(lane == COUNT_LANE, n_done, cm_s[g])
            carry_ref[:, 2 * g * LANES:(2 * g + 1) * LANES] = cmat[:bq]
            carry_ref[:, (2 * g + 1) * LANES:(2 * g + 2) * LANES] = cmat[bq:]
            o_ref[:, cols[g]] = _unstack_heads(acc_s[g], first_head)
        if plan is not None:
            @pl.when(jnp.logical_and(pl.program_id(0) == nstep - 1, i == nq - 1))
            def _():
                plan.finish(x_in, x_out, x_sems)

    width = ngrp * LANES
    outs = _pcall(
        body, name="attn_fwd" if plan is None else "attn_fwd_gather", grid=(nstep, nq),
        in_specs=[pl.BlockSpec((bq, width), lambda p, i: (i, p)),
                  pl.BlockSpec((lp, width), lambda p, i: (0, nstep + p)),
                  pl.BlockSpec((lp, width), lambda p, i: (0, 2 * nstep + p)),
                  pl.BlockSpec((bq, bq), lambda p, i: (0, 0))] + x_in_specs,
        out_specs=[pl.BlockSpec((bq, width), lambda p, i: (i, p)),
                   pl.BlockSpec((bq, 2 * width), lambda p, i: (i, p))] + x_out_specs,
        out_shape=[jax.ShapeDtypeStruct((lp, D_SB), F32), jax.ShapeDtypeStruct((lp, 2 * D_SB), F32)] + x_out_shapes,
        scratch_shapes=[pltpu.VMEM((ngrp, 2 * bq, 1), F32), pltpu.VMEM((ngrp, 2 * bq, LANES), F32),
                        pltpu.VMEM((ngrp, 2 * bq, LANES), F32)] + x_scratch,
        compiler_params=_params("arbitrary", "arbitrary"),
    )(qkv, qkv, qkv, tri, *x_args)
    return outs[0], outs[1], outs[2:]


def _outproj(c5, att, ew, h, w, g):
    lp, d = h.shape
    tm = _mm_tile(lp)

    def body(c5_ref, att_ref, sg_ref, h_ref, w_ref, g_ref, hn_ref, cat_ref, mix_ref):
        sg = sg_ref[...].astype(F32)
        s = att_ref[...] * (sg * _sigmoid(sg))
        cat_ref[:, 0:D_CONV] = c5_ref[...]
        cat_ref[:, D_CONV:] = s.astype(BF16)
        mixed = _nn(cat_ref[...], w_ref[...])
        mix_ref[...] = mixed
        rstd = lax.rsqrt(jnp.mean(mixed * mixed, axis=-1, keepdims=True) + RMS_EPS)
        hn_ref[...] = h_ref[...] + (mixed * rstd) * g_ref[...]

    half = pl.BlockSpec((tm, 512), lambda i: (i, 0))
    full = pl.BlockSpec((tm, d), lambda i: (i, 0))
    return _pcall(
        body, name="outproj_fwd", grid=(lp // tm,),
        in_specs=[half, half, pl.BlockSpec((tm, 512), lambda i: (i, 3)), full,
                  pl.BlockSpec((d, d), lambda i: (0, 0)), pl.BlockSpec((1, d), lambda i: (0, 0))],
        out_specs=[full, full, full],
        out_shape=[jax.ShapeDtypeStruct((lp, d), F32), jax.ShapeDtypeStruct((lp, d), BF16),
                   jax.ShapeDtypeStruct((lp, d), F32)],
        compiler_params=_params("parallel"),
    )(c5, att, ew, h, w, g)


def _loss_head(h, target, seq):
    lp, d = h.shape
    tm = ROW_TILE

    def body(h_ref, t_ref, dh_ref, loss_ref):
        i = pl.program_id(0)

        @pl.when(i == 0)
        def _():
            loss_ref[...] = jnp.zeros_like(loss_ref)

        row = i * tm + lax.broadcasted_iota(jnp.int32, (tm, 1), 0)
        real = jnp.logical_and(row >= N_META, row < N_META + seq)
        diff = jnp.where(real, h_ref[...] - t_ref[...], 0.0)
        dh_ref[...] = diff * (1.0 / d)
        loss_ref[...] += 0.5 * jnp.sum(jnp.sum(diff * diff, axis=-1, keepdims=True) * (1.0 / d))

    full = pl.BlockSpec((tm, d), lambda i: (i, 0))
    return _pcall(
        body, name="loss_head", grid=(lp // tm,),
        in_specs=[full, full],
        out_specs=[full, pl.BlockSpec((8, LANES), lambda i: (0, 0))],
        out_shape=[jax.ShapeDtypeStruct((lp, d), F32), jax.ShapeDtypeStruct((8, LANES), F32)],
        compiler_params=_params("arbitrary"),
    )(h, target)


def _outproj_bwd(dh, mixed, g, w, att, ew, c4):
    lp, d = dh.shape
    tm = _mm_tile(lp)

    def body(dh_ref, mix_ref, g_ref, w_ref, att_ref, cg_ref, sg_ref, c4_ref,
             dmix_ref, datt_ref, dsg_ref, dc4_ref, dcg_ref, dg_ref, db_ref):
        @pl.when(pl.program_id(0) == 0)
        def _():
            dg_ref[...] = jnp.zeros_like(dg_ref)
            db_ref[...] = jnp.zeros_like(db_ref)

        mixed = mix_ref[...]
        dhv = dh_ref[...]
        rstd = lax.rsqrt(jnp.mean(mixed * mixed, axis=-1, keepdims=True) + RMS_EPS)
        n = mixed * rstd
        dg_ref[...] += jnp.sum(dhv * n, axis=0, keepdims=True)
        dn = dhv * g_ref[...]
        dmix = (rstd * (dn - n * jnp.mean(dn * n, axis=-1, keepdims=True))).astype(BF16)
        dmix_ref[...] = dmix
        dcat = _nt(dmix, w_ref[...])
        dc5 = dcat[:, 0:D_CONV]
        ds = dcat[:, D_CONV:]
        silu_sg, dsilu_sg = _silu_fwd_bwd(sg_ref[...].astype(F32))
        datt_ref[...] = (ds * silu_sg).astype(BF16)
        dsg_ref[...] = (ds * att_ref[...] * dsilu_sg).astype(BF16)
        silu_cg, dsilu_cg = _silu_fwd_bwd(cg_ref[...].astype(F32))
        dc4 = dc5 * silu_cg
        db_ref[...] += jnp.sum(dc4, axis=0, keepdims=True)
        dc4_ref[...] = dc4.astype(BF16)
        dcg_ref[...] = (dc5 * c4_ref[...] * dsilu_cg).astype(BF16)

    half = pl.BlockSpec((tm, 512), lambda i: (i, 0))
    full = pl.BlockSpec((tm, d), lambda i: (i, 0))
    hb = jax.ShapeDtypeStruct((lp, 512), BF16)
    return _pcall(
        body, name="outproj_bwd", grid=(lp // tm,),
        in_specs=[full, full, pl.BlockSpec((1, d), lambda i: (0, 0)), pl.BlockSpec((d, d), lambda i: (0, 0)),
                  half, pl.BlockSpec((tm, 512), lambda i: (i, 2)), pl.BlockSpec((tm, 512), lambda i: (i, 3)), half],
        out_specs=[full, half, half, half, half,
                   pl.BlockSpec((1, d), lambda i: (0, 0)), pl.BlockSpec((1, 512), lambda i: (0, 0))],
        out_shape=[jax.ShapeDtypeStruct((lp, d), BF16), hb, hb, hb, hb,
                   jax.ShapeDtypeStruct((1, d), F32), jax.ShapeDtypeStruct((1, 512), F32)],
        compiler_params=_params("arbitrary"),
    )(dh, mixed, g, w, att, ew, ew, c4)


def _attn_bwd(qkv, carries, datt, tri, upper, plan=None):
    lp = qkv.shape[0]
    bq = ATT_BLOCK
    ngrp = ATT_PAIRS
    nstep = D_SB // (LANES * ngrp)
    nq = lp // bq
    split, x_args, x_in_specs, x_out_specs, x_out_shapes, x_scratch = _hosted(plan, 3, 2)

    def body(q_ref, k_ref, v_ref, carry_ref, do_ref, tri_ref, upper_ref, *rest):
        x_in, (dq_ref, dk_ref, dv_ref), x_out, (run_s, dq_s), x_sems = split(rest)
        i = pl.program_id(1)
        if plan is not None:
            @pl.when(jnp.logical_and(pl.program_id(0) == 0, i == 0))
            def _():
                plan.start(x_in, x_out, x_sems)

            @pl.when(jnp.logical_and(pl.program_id(0) == nstep - 1, i == int(RELAY_AT * nq)))
            def _():
                plan.relay(x_in, x_out, x_sems)

        @pl.when(i == 0)
        def _():
            dk_ref[...] = jnp.zeros_like(dk_ref)
            dv_ref[...] = jnp.zeros_like(dv_ref)

        first_head, vis = _attn_masks()
        lane = lax.broadcasted_iota(jnp.int32, (1, LANES), 1)
        cols = [slice(g * LANES, (g + 1) * LANES) for g in range(ngrp)]
        q2s = [_stack_heads(q_ref[:, cs], first_head) for cs in cols]
        do2s = [_stack_heads(do_ref[:, cs], first_head) for cs in cols]
        cmats = [jnp.concatenate([carry_ref[:, 2 * g * LANES:(2 * g + 1) * LANES],
                                  carry_ref[:, (2 * g + 1) * LANES:(2 * g + 2) * LANES]], axis=0)
                 for g in range(ngrp)]
        tri_m = tri_ref[...]
        upper_m = upper_ref[...]

        def blocks(js, masks):
            offs = [pl.multiple_of(j * bq, bq) for j in js]
            work = [(g, b) for b in range(len(js)) for g in range(ngrp)]
            zs = {(g, b): _nt(q2s[g], k_ref[pl.ds(offs[b], bq), cols[g]]) for g, b in work}
            lss = {}
            for g, b in work:
                z = zs[g, b]
                ls = -(jnp.maximum(z, 0.0) + jnp.log(1.0 + jnp.exp(-jnp.abs(z))))
                lss[g, b] = ls if masks[b] is None else jnp.where(masks[b], ls, 0.0)
            tails = {gb: _block_sums(lss[gb], tri_m) for gb in work}
            das = {(g, b): _nt(do2s[g], v_ref[pl.ds(offs[b], bq), cols[g]]) for g, b in work}
            probs, des = {}, {}
            for g, b in work:
                c = jnp.sum(jnp.where(lane == js[b], cmats[g], 0.0), axis=-1, keepdims=True)
                a = jnp.exp(zs[g, b] + lss[g, b] + tails[g, b] + c)
                a = a if masks[b] is None else jnp.where(masks[b], a, 0.0)
                probs[g, b] = a.astype(BF16)
                des[g, b] = das[g, b] * a
            prefixes = {gb: _block_sums(des[gb], upper_m) for gb in work}
            runs = [run_s[g] for g in range(ngrp)]
            dzs = {}
            for g, b in work:
                beta = jnp.exp(zs[g, b] + lss[g, b])
                dz = des[g, b] - beta * (des[g, b] + runs[g] + prefixes[g, b])
                dzs[g, b] = (dz if masks[b] is None else jnp.where(masks[b], dz, 0.0)).astype(BF16)
                runs[g] = runs[g] + prefixes[g, b][:, bq - 1:bq] + des[g, b][:, bq - 1:bq]
            for g in range(ngrp):
                run_s[g] = runs[g]
                dq = dq_s[g]
                for b in range(len(js)):
                    rows = pl.ds(offs[b], bq)
                    dq = dq + _nn(dzs[g, b], k_ref[rows, cols[g]])
                    dk_ref[rows, cols[g]] += _tn(dzs[g, b], q2s[g])
                    dv_ref[rows, cols[g]] += _tn(probs[g, b], do2s[g])
                dq_s[g] = dq

        n_done = jnp.max(carry_ref[:, COUNT_LANE:COUNT_LANE + 1]).astype(jnp.int32)
        n_done = jnp.clip(n_done, 1, i + 1)
        before = jnp.maximum(n_done - 2, 0)
        j0 = i - n_done + 1
        odd = before % 2
        run_s[...] = jnp.zeros_like(run_s)
        dq_s[...] = jnp.zeros_like(dq_s)

        @pl.when(odd == 1)
        def _():
            blocks([j0], [None])

        @pl.loop(0, before // 2)
        def _(t):
            blocks([j0 + odd + 2 * t, j0 + odd + 2 * t + 1], [None, None])

        @pl.when(n_done > 1)
        def _():
            blocks([i - 1, i], [None, vis])

        @pl.when(n_done <= 1)
        def _():
            blocks([i], [vis])

        for g in range(ngrp):
            dq_ref[:, cols[g]] = (_unstack_heads(dq_s[g], first_head) * Q_SCALE).astype(BF16)
        if plan is not None:
            @pl.when(jnp.logical_and(pl.program_id(0) == nstep - 1, i == nq - 1))
            def _():
                plan.finish(x_in, x_out, x_sems)

    width = ngrp * LANES
    qb = pl.BlockSpec((bq, width), lambda p, i: (i, p))
    colb = pl.BlockSpec((lp, width), lambda p, i: (0, p))
    sq = pl.BlockSpec((bq, bq), lambda p, i: (0, 0))
    outs = _pcall(
        body, name="attn_bwd" if plan is None else "attn_bwd_reduce", grid=(nstep, nq),
        in_specs=[qb,
                  pl.BlockSpec((lp, width), lambda p, i: (0, nstep + p)),
                  pl.BlockSpec((lp, width), lambda p, i: (0, 2 * nstep + p)),
                  pl.BlockSpec((bq, 2 * width), lambda p, i: (i, p)), qb, sq, sq] + x_in_specs,
        out_specs=[qb, colb, colb] + x_out_specs,
        out_shape=[jax.ShapeDtypeStruct((lp, D_SB), BF16), jax.ShapeDtypeStruct((lp, D_SB), F32),
                   jax.ShapeDtypeStruct((lp, D_SB), F32)] + x_out_shapes,
        scratch_shapes=[pltpu.VMEM((ngrp, 2 * bq, 1), F32), pltpu.VMEM((ngrp, 2 * bq, LANES), F32)] + x_scratch,
        compiler_params=_params("arbitrary", "arbitrary"),
    )(qkv, qkv, qkv, carries, datt, tri, upper, *x_args)
    return outs[0], outs[1], outs[2], outs[3:]


def _conv_bwd(dc4, c1, ew, cw, lng, lnb, wpw2):
    lp = ew.shape[0]
    tm = ROW_TILE
    nt = lp // tm
    halo_per_tile = tm // CONV_PAD

    def body(dc4_ref, c1_ref, ew_ref, halo_ref, cw_ref, lng_ref, lnb_ref, w_ref,
             dga_ref, dgb_ref, c3_ref, dcw_ref, dcb_ref, dlng_ref, dlnb_ref, xbuf, dbuf, shifted, wacc):
        step = pl.program_id(0)

        @pl.when(step == 0)
        def _():
            wacc[...] = jnp.zeros_like(wacc)
            dcb_ref[...] = jnp.zeros_like(dcb_ref)
            dlng_ref[...] = jnp.zeros_like(dlng_ref)
            dlnb_ref[...] = jnp.zeros_like(dlnb_ref)
            dbuf[tm:tm + CONV_PAD, :] = jnp.zeros((CONV_PAD, D_CONV), F32)

        dc3 = _nt(dc4_ref[...], w_ref[...])
        xhat, rstd = _layer_norm_stats(c1_ref[...])
        c2 = xhat * lng_ref[...] + lnb_ref[...]
        c3, dsilu = _silu_fwd_bwd(c2)
        c3_ref[...] = c3.astype(BF16)
        dc2 = dc3 * dsilu
        dlng_ref[...] += jnp.sum(dc2 * xhat, axis=0, keepdims=True)
        dlnb_ref[...] += jnp.sum(dc2, axis=0, keepdims=True)
        dxhat = dc2 * lng_ref[...]
        dc1 = rstd * (dxhat - jnp.mean(dxhat, axis=-1, keepdims=True)
                      - xhat * jnp.mean(dxhat * xhat, axis=-1, keepdims=True))
        dcb_ref[...] += jnp.sum(dc1, axis=0, keepdims=True)
        dbuf[0:tm, :] = dc1

        ga = ew_ref[:, 0:512].astype(F32)
        sgb = _sigmoid(ew_ref[:, 512:1024].astype(F32))
        xbuf[CONV_PAD:CONV_PAD + tm, :] = ga * sgb
        first_tile = step == nt - 1
        halo = halo_ref[:, 0:512].astype(F32) * _sigmoid(halo_ref[:, 512:1024].astype(F32))
        xbuf[0:CONV_PAD, :] = jnp.where(first_tile, 0.0, halo)

        for cb in range(D_CONV // LANES):
            cs = slice(cb * LANES, (cb + 1) * LANES)
            _shifted_copies(dbuf, cs, shifted, tm)
            for r0 in range(0, tm, CONV_ROWS):
                rs = slice(r0, r0 + CONV_ROWS)
                dc0 = jnp.zeros((CONV_ROWS, LANES), F32)
                for j in range(CONV_WIDTH):
                    dc0 = dc0 + _weighted(cw_ref, j, cs, _shifted_rows(
                        shifted, r0 + CONV_WIDTH - 1 - j, CONV_ROWS))
                dga_ref[rs, cs] = (dc0 * sgb[rs, cs]).astype(BF16)
                dgb_ref[rs, cs] = (dc0 * ga[rs, cs] * sgb[rs, cs] * (1.0 - sgb[rs, cs])).astype(BF16)
            _shifted_copies(xbuf, cs, shifted, tm)
            for r0 in range(0, tm, CONV_ROWS):
                d1 = dbuf[r0:r0 + CONV_ROWS, cs]

                for j in range(CONV_WIDTH):
                    prod = d1 * _shifted_rows(shifted, r0 + CONV_PAD - (CONV_WIDTH - 1) + j, CONV_ROWS)
                    wacc[j * 8:(j + 1) * 8, cs] += jnp.sum(prod.reshape(CONV_ROWS // 8, 8, LANES), axis=0)
        dbuf[tm:tm + CONV_PAD, :] = dbuf[0:CONV_PAD, :]

        @pl.when(step == nt - 1)
        def _():
            dcw_ref[...] = jnp.sum(wacc[...].reshape(CONV_PAD, 8, D_CONV), axis=1)

    rev = lambda i: (nt - 1 - i, 0)
    row = pl.BlockSpec((tm, D_CONV), rev)
    vec = pl.BlockSpec((1, D_CONV), lambda i: (0, 0))
    hb = jax.ShapeDtypeStruct((lp, D_CONV), BF16)
    vs = jax.ShapeDtypeStruct((1, D_CONV), F32)
    return _pcall(
        body, name="conv_bwd", grid=(nt,),
        in_specs=[row, row, pl.BlockSpec((tm, 1024), rev),
                  pl.BlockSpec((CONV_PAD, 1024), lambda i: (jnp.maximum((nt - 1 - i) * halo_per_tile - 1, 0), 0)),
                  pl.BlockSpec((8 * CONV_PAD, D_CONV), lambda i: (0, 0)), vec, vec,
                  pl.BlockSpec((D_CONV, D_CONV), lambda i: (0, 0))],
        out_specs=[row, row, row, pl.BlockSpec((CONV_PAD, D_CONV), lambda i: (0, 0)), vec, vec, vec],
        out_shape=[hb, hb, hb, jax.ShapeDtypeStruct((CONV_PAD, D_CONV), F32), vs, vs, vs],
        scratch_shapes=[pltpu.VMEM((tm + CONV_PAD, D_CONV), F32), pltpu.VMEM((tm + CONV_PAD, D_CONV), F32),
                        pltpu.VMEM((8, tm + CONV_PAD, LANES), F32), pltpu.VMEM((8 * CONV_PAD, D_CONV), F32)],
        compiler_params=_params("arbitrary"),
    )(dc4, c1, ew, ew, cw, lng, lnb, wpw2)


def _inproj_bwd(dga, dgb, dcg, dq, dk, dv, dsg, h, g, w, dh_out):
    lp, d = h.shape
    n = w.shape[1]
    tm = _mm_tile(lp)

    def body(dga_ref, dgb_ref, dcg_ref, dq_ref, dk_ref, dv_ref, dsg_ref, h_ref, g_ref, w_ref, dho_ref,
             dh_ref, dproj_ref, u_ref, dg_ref):
        @pl.when(pl.program_id(0) == 0)
        def _():
            dg_ref[...] = jnp.zeros_like(dg_ref)

        dproj_ref[:, 0:512] = dga_ref[...]
        dproj_ref[:, 512:1024] = dgb_ref[...]
        dproj_ref[:, 1024:1536] = dcg_ref[...]
        dproj_ref[:, 1536:2048] = dq_ref[...]
        dproj_ref[:, 2048:2560] = dk_ref[...].astype(BF16)
        dproj_ref[:, 2560:3072] = dv_ref[...].astype(BF16)
        dproj_ref[:, 3072:3584] = dsg_ref[...]
        du = _nt(dproj_ref[...], w_ref[...])
        x = h_ref[...]
        rstd = lax.rsqrt(jnp.mean(x * x, axis=-1, keepdims=True) + RMS_EPS)
        nrm = x * rstd
        u_ref[...] = (nrm * g_ref[...]).astype(BF16)
        dg_ref[...] += jnp.sum(du * nrm, axis=0, keepdims=True)
        dn = du * g_ref[...]
        dh_ref[...] = dho_ref[...] + rstd * (dn - nrm * jnp.mean(dn * nrm, axis=-1, keepdims=True))

    half = pl.BlockSpec((tm, 512), lambda i: (i, 0))
    full = pl.BlockSpec((tm, d), lambda i: (i, 0))
    return _pcall(
        body, name="inproj_bwd", grid=(lp // tm,),
        in_specs=[half] * 7 + [full, pl.BlockSpec((1, d), lambda i: (0, 0)),
                               pl.BlockSpec((d, n), lambda i: (0, 0)), full],
        out_specs=[full, pl.BlockSpec((tm, n), lambda i: (i, 0)), full, pl.BlockSpec((1, d), lambda i: (0, 0))],
        out_shape=[jax.ShapeDtypeStruct((lp, d), F32), jax.ShapeDtypeStruct((lp, n), BF16),
                   jax.ShapeDtypeStruct((lp, d), BF16), jax.ShapeDtypeStruct((1, d), F32)],
        compiler_params=_params("arbitrary"),
    )(dga, dgb, dcg, dq, dk, dv, dsg, h, g, w, dh_out)


def _row_split(m, parts):
    tm = m // parts
    assert tm * parts == m and tm % 16 == 0, (m, parts)
    return tm


def _matmul_tn(x, dy, tn, name):
    m, k = x.shape
    n = dy.shape[1]
    steps = 4 if m % 64 == 0 else 1
    tm = _row_split(m, steps)

    def body(x_ref, dy_ref, o_ref, acc_ref):
        r = pl.program_id(1)

        @pl.when(r == 0)
        def _():
            acc_ref[...] = jnp.zeros_like(acc_ref)

        acc_ref[...] += _tn(x_ref[...], dy_ref[...])

        @pl.when(r == steps - 1)
        def _():
            o_ref[...] = acc_ref[...].astype(BF16)

    return _pcall(
        body, name=name, grid=(n // tn, steps),
        in_specs=[pl.BlockSpec((tm, k), lambda j, r: (r, 0)), pl.BlockSpec((tm, tn), lambda j, r: (r, j))],
        out_specs=pl.BlockSpec((k, tn), lambda j, r: (0, j)),
        out_shape=jax.ShapeDtypeStruct((k, n), BF16),
        scratch_shapes=[pltpu.VMEM((k, tn), F32)],
        compiler_params=_params("parallel", "arbitrary"),
    )(x, dy)


def _local_step(h0, target_p, seq, vecs, depth, all_weights=None, w_in0=None, gather_w_in=None, gather_rest=None,
                reduce_layer=None):
    pre_g, post_g, conv_b, ln_g, ln_b, b_pw2 = vecs
    ar = jnp.arange(ATT_BLOCK)
    tri = (ar[:, None] > ar[None, :]).astype(BF16)
    upper = (ar[:, None] < ar[None, :]).astype(BF16)
    row = lambda a, l: a[l][None, :]

    hosted = all_weights is None
    weights = [None] * depth if hosted else list(all_weights)
    next_w_in, next_rest = w_in0, None
    saved = []
    h = h0
    for l in range(depth):
        more = hosted and l + 1 < depth
        if hosted:
            ew, qkv, rest = _inproj(h, row(pre_g, l), next_w_in, gather_rest(0) if l == 0 else None)
            w_in, (w_pw2, w_out, conv_w) = next_w_in, (rest if l == 0 else next_rest)
        else:
            w_in, w_pw2, w_out, conv_w = weights[l]
            ew, qkv, _ = _inproj(h, row(pre_g, l), w_in)
        conv_w = jnp.repeat(conv_w, 8, axis=0)
        weights[l] = (w_in, w_pw2, w_out, conv_w)
        c1, c4, c5, next_rest = _conv_fwd(ew, conv_w, row(conv_b, l), row(ln_g, l), row(ln_b, l), w_pw2,
                                          row(b_pw2, l), gather_rest(l + 1) if more else None)
        att, carries, gathered = _attn_fwd(qkv, tri, gather_w_in(l + 1) if more else None)
        if more:
            next_w_in = gathered[0]
        hn, cat, mixed = _outproj(c5, att, ew, h, w_out, row(post_g, l))
        saved.append((h, ew, qkv, c1, c4, att, carries, cat, mixed))
        h = hn

    dh, loss = _loss_head(h, target_p, seq)

    vec_grads = [None] * depth
    mat_grads = [None] * depth
    pending = None
    for l in reversed(range(depth)):
        w_in, w_pw2, w_out, conv_w = weights[l]
        h_in, ew, qkv, c1, c4, att, carries, cat, mixed = saved[l]
        dmix, datt, dsg, dc4, dcg, dpost, dbpw2 = _outproj_bwd(dh, mixed, row(post_g, l), w_out, att, ew, c4)
        dw_out = _matmul_tn(cat, dmix, 512, "dw_out")
        dq, dk, dv, landed = _attn_bwd(qkv, carries, datt, tri, upper, pending)
        if pending is not None:
            mat_grads[l + 1] = landed
        dga, dgb, c3, dcw, dcb, dlng, dlnb = _conv_bwd(dc4, c1, ew, conv_w, row(ln_g, l), row(ln_b, l), w_pw2)
        dw_pw2 = _matmul_tn(c3, dc4, 512, "dw_pw2")
        dh, dproj, u, dpre = _inproj_bwd(dga, dgb, dcg, dq, dk, dv, dsg, h_in, row(pre_g, l), w_in, dh)
        dw_in = _matmul_tn(u, dproj, 1792, "dw_in")
        vec_grads[l] = (dpre[0], dpost[0], dcb[0], dlng[0], dlnb[0], dbpw2[0])
        mats = (dw_in, dw_pw2, dw_out, dcw)
        if reduce_layer is None:
            mat_grads[l] = mats
        else:
            pending = reduce_layer(mats)
    if pending is not None:
        mat_grads[0] = _run_exchange(pending, "reduce_grads")

    vec_grads = [jnp.stack([g[k] for g in vec_grads]) for k in range(len(vecs))]
    return loss[0, 0], dh, vec_grads, mat_grads


N_CHIPS = 4
ANY = pl.BlockSpec(memory_space=pl.ANY)


def _chip_peers():
    x, y, c = lax.axis_index("x"), lax.axis_index("y"), lax.axis_index("c")
    return x, y, c, [(x, 1 - y), (1 - x, y), (1 - x, 1 - y)]


def _shard_slices(refs, dims, idx):
    out = []
    for ref, (axis, size) in zip(refs, dims):
        assert size % LANES == 0
        start = pl.multiple_of(idx * size, LANES)
        sl = [slice(None)] * len(ref.shape)
        sl[axis] = pl.ds(start, size)
        out.append(ref.at[tuple(sl)])
    return out


class _Exchange(NamedTuple):
    inputs: list
    out_shapes: list
    scratch: list
    start: Callable
    relay: Callable
    finish: Callable


def _run_exchange(plan, name):
    n_in, n_out = len(plan.inputs), len(plan.out_shapes)

    def body(*refs):
        parts = refs[:n_in], refs[n_in:n_in + n_out], refs[n_in + n_out:]
        plan.start(*parts)
        plan.relay(*parts)
        plan.finish(*parts)

    return _pcall(body, name=name, in_specs=[ANY] * n_in, out_specs=[ANY] * n_out, out_shape=plan.out_shapes,
                  scratch_shapes=plan.scratch)(*plan.inputs)


def _gather_plan(shards, dims):
    n = len(shards)
    full_shapes = []
    halves = []
    for s, (axis, size) in zip(shards, dims):
        shp = list(s.shape)
        shp[axis] = size * N_CHIPS
        full_shapes.append(jax.ShapeDtypeStruct(tuple(shp), s.dtype))
        tile_rows = 32 // s.dtype.itemsize
        assert s.shape[0] % (2 * tile_rows) == 0
        halves.append((s.shape[0] // 2, tile_rows))

    def half(refs, which):
        return [r.at[pl.ds(pl.multiple_of(which * h, t), h)] for r, (h, t) in zip(refs, halves)]

    def copies(srcs, outs, sems):
        send, recv, loc = sems
        x, y, c, peers = _chip_peers()
        sibling = (x, y, 1 - c)
        mine = _shard_slices(outs, dims, 2 * x + y)
        local = [pltpu.make_async_copy(s, d, loc.at[a]) for a, (s, d) in enumerate(zip(srcs, mine))]

        def remote(src, dst, slot, a, dev):
            return pltpu.make_async_remote_copy(src, dst, send.at[slot, a], recv.at[slot, a],
                                                device_id=dev, device_id_type=MESH)

        sends = [remote(s, d, k, a, (px, py, c))
                 for k, (px, py) in enumerate(peers) for a, (s, d) in enumerate(zip(half(srcs, c), half(mine, c)))]
        theirs = [_shard_slices(outs, dims, 2 * px + py) for px, py in peers]
        arrivals = [remote(s, d, k, a, (px, py, c))
                    for k, (px, py) in enumerate(peers)
                    for a, (s, d) in enumerate(zip(half(srcs, c), half(theirs[k], c)))]
        passed_on = [remote(d, d, 3 + k, a, sibling) for k in range(3) for a, d in enumerate(half(theirs[k], c))]
        from_sibling = [remote(d, d, 3 + k, a, sibling)
                        for k in range(3) for a, d in enumerate(half(theirs[k], 1 - c))]
        return local, sends, arrivals, passed_on, from_sibling

    def start(srcs, outs, sems):
        local, sends = copies(srcs, outs, sems)[:2]
        for cp in local + sends:
            cp.start()

    def relay(srcs, outs, sems):
        _, _, arrivals, passed_on, _ = copies(srcs, outs, sems)
        for arrived, onward in zip(arrivals, passed_on):
            arrived.wait_recv()
            onward.start()

    def finish(srcs, outs, sems):
        local, sends, _, passed_on, from_sibling = copies(srcs, outs, sems)
        for cp in from_sibling:
            cp.wait_recv()
        for cp in sends + passed_on:
            cp.wait_send()
        for cp in local:
            cp.wait()

    scratch = [pltpu.SemaphoreType.DMA((6, n)), pltpu.SemaphoreType.DMA((6, n)), pltpu.SemaphoreType.DMA((n,))]
    return _Exchange(list(shards), full_shapes, scratch, start, relay, finish)


def _reduce_plan(grads, dims):
    n = len(grads)
    piece_shapes = []
    for g, (axis, size) in zip(grads, dims):
        shp = list(g.shape)
        shp[axis] = size
        piece_shapes.append(jax.ShapeDtypeStruct((N_CHIPS,) + tuple(shp), g.dtype))

    def copies(srcs, outs, sems):
        mine, theirs = outs[:n], outs[n:]
        send, recv, loc = sems
        x, y, c, peers = _chip_peers()
        sibling = (x, y, 1 - c)
        own = _shard_slices(srcs, dims, 2 * x + y)

        def remote(src, dst, slot, a, dev):
            return pltpu.make_async_remote_copy(src, dst, send.at[slot, a], recv.at[slot, a],
                                                device_id=dev, device_id_type=MESH)

        local = [pltpu.make_async_copy(own[a], mine[a].at[3], loc.at[a]) for a in range(n)]
        to_sibling = [remote(own[a], theirs[a].at[3], 3, a, sibling) for a in range(n)]
        to_chips = [remote(src, mine[a].at[k], k, a, (px, py, c))
                    for k, (px, py) in enumerate(peers)
                    for a, src in enumerate(_shard_slices(srcs, dims, 2 * px + py))]
        passed_on = [remote(mine[a].at[k], theirs[a].at[k], 4 + k, a, sibling) for k in range(3) for a in range(n)]
        return local, to_sibling, to_chips, passed_on

    def start(srcs, outs, sems):
        local, to_sibling, to_chips, _ = copies(srcs, outs, sems)
        for cp in local + to_sibling + to_chips:
            cp.start()

    def relay(srcs, outs, sems):
        _, _, to_chips, passed_on = copies(srcs, outs, sems)
        for arrived, onward in zip(to_chips, passed_on):
            arrived.wait_recv()
            onward.start()

    def finish(srcs, outs, sems):
        local, to_sibling, to_chips, passed_on = copies(srcs, outs, sems)
        for cp in to_sibling + passed_on:
            cp.wait_recv()
        for cp in to_sibling + to_chips + passed_on:
            cp.wait_send()
        for cp in local:
            cp.wait()

    scratch = [pltpu.SemaphoreType.DMA((7, n)), pltpu.SemaphoreType.DMA((7, n)), pltpu.SemaphoreType.DMA((n,))]
    return _Exchange(list(grads), piece_shapes * 2, scratch, start, relay, finish)


def _allsum_small(pack):
    rows, cols = pack.shape
    ndev = 8

    def body(p_ref, o_ref, buf, send, recv):
        x, y, c = lax.axis_index("x"), lax.axis_index("y"), lax.axis_index("c")
        me = 4 * x + 2 * y + c
        buf[me] = p_ref[...]
        started = []
        for r in range(1, ndev):
            bx, by, bc = (r >> 2) & 1, (r >> 1) & 1, r & 1
            dev = (x ^ bx, y ^ by, c ^ bc)
            cp = pltpu.make_async_remote_copy(p_ref, buf.at[me], send.at[r], recv.at[r],
                                              device_id=dev, device_id_type=MESH)
            cp.start()
            started.append(cp)
        for r in range(1, ndev):
            pltpu.make_async_remote_copy(p_ref, buf.at[me ^ r], send.at[r], recv.at[r],
                                         device_id=(x, y, c), device_id_type=MESH).wait_recv()
        for cp in started:
            cp.wait_send()
        acc = buf[0]
        for d in range(1, ndev):
            acc = acc + buf[d]
        o_ref[...] = acc

    vm = pl.BlockSpec(memory_space=pltpu.VMEM)
    return _pcall(
        body, name="allsum_small", in_specs=[vm], out_specs=vm,
        out_shape=jax.ShapeDtypeStruct((rows, cols), F32),
        scratch_shapes=[pltpu.VMEM((ndev, rows, cols), F32), pltpu.SemaphoreType.DMA((ndev,)),
                        pltpu.SemaphoreType.DMA((ndev,))],
    )(pack)


def _adamw(parts, w, m, v, layer, prev, name):
    _, rows, cols = w.shape
    tr = ROW_TILE if rows % ROW_TILE == 0 else rows
    counts = [p.shape[0] for p in parts]
    n_parts = len(parts)
    n_prev = 0 if prev is None else 4

    def body(*refs):
        part_refs = refs[:n_parts]
        w_ref, m_ref, v_ref = refs[n_parts:n_parts + 3]
        g_ref, d_ref, nm_ref, nv_ref = refs[n_parts + 3 + n_prev:]
        g = None
        for p_ref, cnt in zip(part_refs, counts):
            s = p_ref[0].astype(F32)
            for k in range(1, cnt):
                s = s + p_ref[k].astype(F32)
            g = s if g is None else g + s
        m2 = ADAM_B1 * m_ref[0] + (1.0 - ADAM_B1) * g
        v2 = ADAM_B2 * v_ref[0] + (1.0 - ADAM_B2) * (g * g)
        m_hat = m2 / (1.0 - ADAM_B1 ** ADAM_STEP)
        v_hat = v2 / (1.0 - ADAM_B2 ** ADAM_STEP)
        g_ref[0] = g
        d_ref[0] = -ADAM_LR * (m_hat / (jnp.sqrt(v_hat) + ADAM_EPS) + ADAM_WD * w_ref[0])
        nm_ref[0] = m2
        nv_ref[0] = v2

    blk = pl.BlockSpec((1, tr, cols), lambda i: (layer, i, 0))
    shp = jax.ShapeDtypeStruct(w.shape, F32)
    return _pcall(
        body, name=name, grid=(rows // tr,),
        in_specs=[pl.BlockSpec((cnt, tr, cols), lambda i: (0, i, 0)) for cnt in counts] + [blk] * 3 + [ANY] * n_prev,
        out_specs=[blk] * 4, out_shape=[shp] * 4,
        input_output_aliases={n_parts + 3 + k: k for k in range(n_prev)},
        compiler_params=_params("parallel"),
    )(*parts, w, m, v, *(prev or ()))


def kernel(x, meta_tokens, pre_norm_g, post_norm_g, w_in, conv_w, conv_b, conv_ln_g, conv_ln_b, w_pw2, b_pw2, w_out, loss_target, m_meta_tokens, m_pre_norm_g, m_post_norm_g, m_w_in, m_conv_w, m_conv_b, m_conv_ln_g, m_conv_ln_b, m_w_pw2, m_b_pw2, m_w_out, v_meta_tokens, v_pre_norm_g, v_post_norm_g, v_w_in, v_conv_w, v_conv_b, v_conv_ln_g, v_conv_ln_b, v_w_pw2, v_b_pw2, v_w_out):
    seq, d = x.shape[1], x.shape[2]
    depth = w_in.shape[0]
    length = N_META + seq
    lp = -(-length // ATT_BLOCK) * ATT_BLOCK
    tap_pad = ((0, 0), (0, CONV_PAD - CONV_WIDTH), (0, 0))

    shards = (w_in.astype(BF16), w_pw2.astype(BF16), w_out.astype(BF16), jnp.pad(conv_w, tap_pad))
    dims = [(1, w_in.shape[2]), (0, w_pw2.shape[1]), (0, w_out.shape[1]), (1, conv_w.shape[2])]
    layer_shards = lambda l: [s[l] for s in shards]

    w_in0, meta_f = _run_exchange(_gather_plan([shards[0][0], meta_tokens], [dims[0], (1, meta_tokens.shape[1])]),
                                  "gather_weights")

    h0 = jnp.concatenate([meta_f, x[0], jnp.zeros((lp - length, d), F32)], axis=0)
    target_p = jnp.pad(loss_target[0], ((N_META, lp - length), (0, 0)))
    vecs = (pre_norm_g, post_norm_g, conv_b, conv_ln_g, conv_ln_b, b_pw2)
    loss, dh0, vec_grads, pieces = _local_step(
        h0, target_p, seq, vecs, depth, w_in0=w_in0,
        gather_w_in=lambda l: _gather_plan([shards[0][l]], dims[:1]),
        gather_rest=lambda l: _gather_plan(layer_shards(l)[1:], dims[1:]),
        reduce_layer=lambda grads: _reduce_plan(list(grads), dims))

    def update(k, w, m, v, name):
        outs = None
        for l in reversed(range(depth)):
            outs = _adamw([pieces[l][k], pieces[l][4 + k]], w, m, v, l, outs, name)
        return outs

    up_w_in = update(0, w_in, m_w_in, v_w_in, "adamw_w_in")
    up_w_pw2 = update(1, w_pw2, m_w_pw2, v_w_pw2, "adamw_w_pw2")
    up_w_out = update(2, w_out, m_w_out, v_w_out, "adamw_w_out")
    up_conv_w = [o[:, :CONV_WIDTH] for o in update(3, jnp.pad(conv_w, tap_pad), jnp.pad(m_conv_w, tap_pad),
                                                   jnp.pad(v_conv_w, tap_pad, constant_values=1.0), "adamw_conv_w")]

    two = lambda a: a.reshape(-1, d)
    vec_rows = [two(g) for g in vec_grads]
    n_vec = sum(a.shape[0] for a in vec_rows)
    pack = jnp.concatenate(vec_rows + [dh0[:N_META], jnp.full((8, d), loss, F32)], axis=0)
    pack = jnp.pad(pack, ((0, -pack.shape[0] % 8), (0, 0)))
    tot = _allsum_small(pack)
    loss_all = tot[n_vec + N_META, 0]

    cat = lambda arrs: jnp.concatenate([two(t) for t in arrs], axis=0)[None]
    small_m = (m_pre_norm_g, m_post_norm_g, m_conv_b, m_conv_ln_g, m_conv_ln_b, m_b_pw2)
    small_v = (v_pre_norm_g, v_post_norm_g, v_conv_b, v_conv_ln_g, v_conv_ln_b, v_b_pw2)
    up_small = _adamw([tot[None, :n_vec]], cat(vecs), cat(small_m), cat(small_v), 0, None, "adamw_vectors")

    def unpack(o):
        res, r0 = [], 0
        for t in vecs:
            nrow = t.size // d
            res.append(o[0, r0:r0 + nrow].reshape(t.shape))
            r0 += nrow
        return res

    up_small = [unpack(o) for o in up_small]
    chip = 2 * lax.axis_index("x") + lax.axis_index("y")
    mcols = meta_tokens.shape[1]
    g_meta = lax.dynamic_slice_in_dim(tot[n_vec:n_vec + N_META], chip * mcols, mcols, axis=1)
    up_meta = [o[0] for o in _adamw([g_meta[None]], meta_tokens[None], m_meta_tokens[None], v_meta_tokens[None],
                                    0, None, "adamw_meta")]

    grad_x = dh0[N_META:length][None]
    outs = [loss_all, grad_x]
    for j in range(4):
        pre, post, cb, lg, lb, bp = up_small[j]
        outs += [up_meta[j], pre, post, up_w_in[j], up_conv_w[j], cb, lg, lb, up_w_pw2[j], bp, up_w_out[j]]
    return tuple(outs)
```

```python
from typing import Callable, NamedTuple

import jax
import jax.numpy as jnp
from jax import lax
from jax.experimental import pallas as pl
from jax.experimental.pallas import tpu as pltpu

F32 = jnp.float32
BF16 = jnp.bfloat16

N_META = 16
D_CONV = 512
D_SB = 512
HEAD_DIM = 64
CONV_WIDTH = 31
CONV_PAD = 32
CONV_ROWS = 128
RMS_EPS = 1e-6
LN_EPS = 1e-5
Q_SCALE = HEAD_DIM ** -0.5

ADAM_LR = 0.001
ADAM_B1 = 0.9
ADAM_B2 = 0.999
ADAM_EPS = 1e-08
ADAM_WD = 0.01
ADAM_STEP = 10

LANES = 128
ROW_TILE = 256
MM_TILE_MAX = 544
ATT_BLOCK = 256
ATT_PAIRS = 4
ATT_PAIRS_FWD = 4
VMEM_LIMIT = 56 * 1024 * 1024
EXP_ZERO = -104.0
COUNT_LANE = LANES - 1
RELAY_AT = 0.75

MESH = pl.DeviceIdType.MESH


def _pcall(body, **kw):
    return pl.pallas_call(body, **kw)


def _params(*sem):
    return pltpu.CompilerParams(dimension_semantics=sem, vmem_limit_bytes=VMEM_LIMIT)


def _sigmoid(x):
    return 1.0 / (1.0 + jnp.exp(-x))


def _silu_fwd_bwd(x):
    s = _sigmoid(x)
    return x * s, s * (1.0 + x * (1.0 - s))


def _nt(a, b):
    return lax.dot_general(a, b, (((1,), (1,)), ((), ())), preferred_element_type=F32)


def _tn(a, b):
    return lax.dot_general(a, b, (((0,), (0,)), ((), ())), preferred_element_type=F32)


def _nn(a, b):
    return jnp.dot(a, b, preferred_element_type=F32)


def _mm_tile(rows):
    return max(t for t in range(16, MM_TILE_MAX + 1, 16) if rows % t == 0)


def _host_begin(plan, parts, steps):
    if plan is not None:
        x_in, _, x_out, _, x_sems = parts

        @pl.when(pl.program_id(0) == 0)
        def _():
            plan.start(x_in, x_out, x_sems)

        @pl.when(pl.program_id(0) == int(RELAY_AT * steps))
        def _():
            plan.relay(x_in, x_out, x_sems)


def _host_end(plan, parts, steps):
    if plan is not None:
        x_in, _, x_out, _, x_sems = parts

        @pl.when(pl.program_id(0) == steps - 1)
        def _():
            plan.finish(x_in, x_out, x_sems)


def _inproj(h, g, w, plan=None):
    lp, d = h.shape
    n = w.shape[1]
    tm = _mm_tile(lp)
    steps = lp // tm
    split, x_args, x_in_specs, x_out_specs, x_out_shapes, x_scratch = _hosted(plan, 2, 0)

    def body(h_ref, g_ref, w_ref, *rest):
        parts = split(rest)
        ew_ref, qkv_ref = parts[1]
        _host_begin(plan, parts, steps)
        x = h_ref[...]
        rstd = lax.rsqrt(jnp.mean(x * x, axis=-1, keepdims=True) + RMS_EPS)
        u = ((x * rstd) * g_ref[...]).astype(BF16)
        p = _nn(u, w_ref[...])
        ew_ref[:, 0:1536] = p[:, 0:1536].astype(BF16)
        ew_ref[:, 1536:2048] = p[:, 3072:3584].astype(BF16)
        qkv_ref[:, 0:512] = (p[:, 1536:2048] * Q_SCALE).astype(BF16)
        qkv_ref[:, 512:1536] = p[:, 2048:3072].astype(BF16)
        _host_end(plan, parts, steps)

    outs = _pcall(
        body, name="inproj_fwd" if plan is None else "inproj_fwd_gather", grid=(steps,),
        in_specs=[pl.BlockSpec((tm, d), lambda i: (i, 0)),
                  pl.BlockSpec((1, d), lambda i: (0, 0)),
                  pl.BlockSpec((d, n), lambda i: (0, 0))] + x_in_specs,
        out_specs=[pl.BlockSpec((tm, 2048), lambda i: (i, 0)),
                   pl.BlockSpec((tm, 1536), lambda i: (i, 0))] + x_out_specs,
        out_shape=[jax.ShapeDtypeStruct((lp, 2048), BF16), jax.ShapeDtypeStruct((lp, 1536), BF16)] + x_out_shapes,
        scratch_shapes=x_scratch,
        compiler_params=_params("parallel" if plan is None else "arbitrary"),
    )(h, g, w, *x_args)
    return outs[0], outs[1], outs[2:]


def _layer_norm_stats(c1):
    mu = jnp.mean(c1, axis=-1, keepdims=True)
    xc = c1 - mu
    var = jnp.mean(xc * xc, axis=-1, keepdims=True)
    rstd = lax.rsqrt(var + LN_EPS)
    return xc * rstd, rstd


def _shifted_copies(window, cols, shifted, tm):
    shifted[0] = window[:, cols]
    rows = tm + CONV_PAD - 8
    for b in range(1, 8):
        shifted[b, 0:rows, :] = window[pl.ds(b, rows), cols]


def _shifted_rows(shifted, shift, tm):
    b = shift % 8
    return shifted[b, pl.ds(pl.multiple_of(shift - b, 8), tm), :]


def _weighted(cw8_ref, j, cols, rows):
    w8 = cw8_ref[pl.ds(pl.multiple_of(j * 8, 8), 8), cols]
    r = rows.shape[0]
    return (rows.reshape(r // 8, 8, LANES) * w8[None]).reshape(r, LANES)


def _conv_fwd(ew, cw, cb, lng, lnb, wpw2, bpw2, plan=None):
    lp = ew.shape[0]
    tm = ROW_TILE
    steps = lp // tm
    split, x_args, x_in_specs, x_out_specs, x_out_shapes, x_scratch = _hosted(plan, 3, 2)

    def body(ew_ref, cw_ref, cb_ref, lng_ref, lnb_ref, w_ref, b_ref, *rest):
        parts = split(rest)
        (c1_ref, c4_ref, c5_ref), (xbuf, shifted) = parts[1], parts[3]
        _host_begin(plan, parts, steps)
        @pl.when(pl.program_id(0) == 0)
        def _():
            xbuf[0:CONV_PAD, :] = jnp.zeros((CONV_PAD, D_CONV), F32)

        ga = ew_ref[:, 0:512].astype(F32)
        gb = ew_ref[:, 512:1024].astype(F32)
        cg = ew_ref[:, 1024:1536].astype(F32)
        xbuf[CONV_PAD:CONV_PAD + tm, :] = ga * _sigmoid(gb)
        for blk in range(D_CONV // LANES):
            cs = slice(blk * LANES, (blk + 1) * LANES)
            _shifted_copies(xbuf, cs, shifted, tm)
            for r0 in range(0, tm, CONV_ROWS):
                acc = jnp.zeros((CONV_ROWS, LANES), F32) + cb_ref[:, cs]
                for j in range(CONV_WIDTH):
                    acc = acc + _weighted(cw_ref, j, cs, _shifted_rows(
                        shifted, r0 + CONV_PAD - (CONV_WIDTH - 1) + j, CONV_ROWS))
                c1_ref[r0:r0 + CONV_ROWS, cs] = acc
        xbuf[0:CONV_PAD, :] = xbuf[tm:tm + CONV_PAD, :]
        xhat, _ = _layer_norm_stats(c1_ref[...])
        c2 = xhat * lng_ref[...] + lnb_ref[...]
        c3 = c2 * _sigmoid(c2)
        c4 = _nn(c3.astype(BF16), w_ref[...]) + b_ref[...]
        c4_ref[...] = c4
        c5_ref[...] = (c4 * (cg * _sigmoid(cg))).astype(BF16)
        _host_end(plan, parts, steps)

    vec = pl.BlockSpec((1, D_CONV), lambda i: (0, 0))
    row = pl.BlockSpec((tm, D_CONV), lambda i: (i, 0))
    outs = _pcall(
        body, name="conv_fwd" if plan is None else "conv_fwd_gather", grid=(steps,),
        in_specs=[pl.BlockSpec((tm, 1536), lambda i: (i, 0)),
                  pl.BlockSpec((8 * CONV_PAD, D_CONV), lambda i: (0, 0)),
                  vec, vec, vec,
                  pl.BlockSpec((D_CONV, D_CONV), lambda i: (0, 0)),
                  vec] + x_in_specs,
        out_specs=[row, row, row] + x_out_specs,
        out_shape=[jax.ShapeDtypeStruct((lp, D_CONV), F32), jax.ShapeDtypeStruct((lp, D_CONV), F32),
                   jax.ShapeDtypeStruct((lp, D_CONV), BF16)] + x_out_shapes,
        scratch_shapes=[pltpu.VMEM((tm + CONV_PAD, D_CONV), F32),
                        pltpu.VMEM((8, tm + CONV_PAD, LANES), F32)] + x_scratch,
        compiler_params=_params("arbitrary"),
    )(ew, cw, cb, lng, lnb, wpw2, bpw2, *x_args)
    return outs[0], outs[1], outs[2], outs[3:]


def _block_sums(x, m01):
    return _nn(x.astype(BF16), m01)


def _attn_masks():
    lane = lax.broadcasted_iota(jnp.int32, (1, LANES), 1)
    row = lax.broadcasted_iota(jnp.int32, (2 * ATT_BLOCK, ATT_BLOCK), 0)
    col = lax.broadcasted_iota(jnp.int32, (2 * ATT_BLOCK, ATT_BLOCK), 1)
    return lane < HEAD_DIM, col < (row & (ATT_BLOCK - 1))


def _stack_heads(x, first_head):
    zero = jnp.zeros_like(x)
    return jnp.concatenate([jnp.where(first_head, x, zero), jnp.where(first_head, zero, x)], axis=0)


def _unstack_heads(x2, first_head):
    rows = x2.shape[0] // 2
    return jnp.where(first_head, x2[:rows], x2[rows:])


def _hosted(plan, n_out, n_scratch):
    n_in = 0 if plan is None else len(plan.inputs)
    n_x = 0 if plan is None else len(plan.out_shapes)

    def split(rest):
        a, b, c = n_in + n_out, n_in + n_out + n_x, n_in + n_out + n_x + n_scratch
        return rest[:n_in], rest[n_in:a], rest[a:b], rest[b:c], rest[c:]

    if plan is None:
        return split, [], [], [], [], []
    return split, list(plan.inputs), [ANY] * n_in, [ANY] * n_x, list(plan.out_shapes), list(plan.scratch)


def _attn_fwd(qkv, tri, plan=None):
    lp = qkv.shape[0]
    bq = ATT_BLOCK
    ngrp = ATT_PAIRS_FWD
    nstep = D_SB // (LANES * ngrp)
    nq = lp // bq
    assert nq <= COUNT_LANE
    split, x_args, x_in_specs, x_out_specs, x_out_shapes, x_scratch = _hosted(plan, 2, 3)

    def body(q_ref, k_ref, v_ref, tri_ref, *rest):
        x_in, (o_ref, carry_ref), x_out, (c_s, acc_s, cm_s), x_sems = split(rest)
        i = pl.program_id(1)
        if plan is not None:
            @pl.when(jnp.logical_and(pl.program_id(0) == 0, i == 0))
            def _():
                plan.start(x_in, x_out, x_sems)

            @pl.when(jnp.logical_and(pl.program_id(0) == nstep - 1, i == int(RELAY_AT * nq)))
            def _():
                plan.relay(x_in, x_out, x_sems)

        first_head, vis = _attn_masks()
        lane = lax.broadcasted_iota(jnp.int32, (1, LANES), 1)
        cols = [slice(g * LANES, (g + 1) * LANES) for g in range(ngrp)]
        q2s = [_stack_heads(q_ref[:, cs], first_head) for cs in cols]
        tri_m = tri_ref[...]

        c_s[...] = jnp.zeros_like(c_s)
        acc_s[...] = jnp.zeros_like(acc_s)
        cm_s[...] = jnp.zeros_like(cm_s)

        def blocks(js, masks):
            offs = [pl.multiple_of(j * bq, bq) for j in js]
            work = [(g, b) for b in range(len(js)) for g in range(ngrp)]
            zs = {(g, b): _nt(q2s[g], k_ref[pl.ds(offs[b], bq), cols[g]]) for g, b in work}
            lss = {}
            for g, b in work:
                z = zs[g, b]
                ls = -(jnp.maximum(z, 0.0) + jnp.log(1.0 + jnp.exp(-jnp.abs(z))))
                lss[g, b] = ls if masks[b] is None else jnp.where(masks[b], ls, 0.0)
            tails = {gb: _block_sums(lss[gb], tri_m) for gb in work}
            probs = {}
            carry = [c_s[g] for g in range(ngrp)]
            saved = [cm_s[g] for g in range(ngrp)]
            for g, b in work:
                a = jnp.exp(zs[g, b] + lss[g, b] + tails[g, b] + carry[g])
                probs[g, b] = (a if masks[b] is None else jnp.where(masks[b], a, 0.0)).astype(BF16)
                saved[g] = jnp.where(lane == js[b], carry[g], saved[g])
                carry[g] = carry[g] + tails[g, b][:, 0:1] + lss[g, b][:, 0:1]
            top = None
            for g in range(ngrp):
                c_s[g] = carry[g]
                cm_s[g] = saved[g]
                acc = acc_s[g]
                for b in range(len(js)):
                    acc = acc + _nn(probs[g, b], v_ref[pl.ds(offs[b], bq), cols[g]])
                acc_s[g] = acc
                top = carry[g] if top is None else jnp.maximum(top, carry[g])
            return jnp.max(top) > EXP_ZERO

        alive = lax.cond(i > 0, lambda: blocks([i, i - 1], [vis, None]), lambda: blocks([i], [vis]))
        rest = jnp.maximum(i - 1, 0)

        def pair(carry):
            t, _ = carry
            j = i - 2 - 2 * t
            return t + 1, blocks([j, j - 1], [None, None])

        trips, alive = lax.while_loop(lambda ca: jnp.logical_and(ca[0] < rest // 2, ca[1]), pair, (0, alive))
        last = jnp.logical_and(jnp.logical_and(rest % 2 == 1, trips == rest // 2), alive)

        @pl.when(last)
        def _():
            blocks([0], [None])

        n_done = (jnp.minimum(i + 1, 2) + 2 * trips + last.astype(jnp.int32)).astype(F32)
        for g in range(ngrp):
            cmat = jnp.where(lane == COUNT_LANE, n_done, cm_s[g])
            carry_ref[:, 2 * g * LANES:(2 * g + 1) * LANES] = cmat[:bq]
            carry_ref[:, (2 * g + 1) * LANES:(2 * g + 2) * LANES] = cmat[bq:]
            o_ref[:, cols[g]] = _unstack_heads(acc_s[g], first_head)
        if plan is not None:
            @pl.when(jnp.logical_and(pl.program_id(0) == nstep - 1, i == nq - 1))
            def _():
                plan.finish(x_in, x_out, x_sems)

    width = ngrp * LANES
    outs = _pcall(
        body, name="attn_fwd" if plan is None else "attn_fwd_gather", grid=(nstep, nq),
        in_specs=[pl.BlockSpec((bq, width), lambda p, i: (i, p)),
                  pl.BlockSpec((lp, width), lambda p, i: (0, nstep + p)),
                  pl.BlockSpec((lp, width), lambda p, i: (0, 2 * nstep + p)),
                  pl.BlockSpec((bq, bq), lambda p, i: (0, 0))] + x_in_specs,
        out_specs=[pl.BlockSpec((bq, width), lambda p, i: (i, p)),
                   pl.BlockSpec((bq, 2 * width), lambda p, i: (i, p))] + x_out_specs,
        out_shape=[jax.ShapeDtypeStruct((lp, D_SB), F32), jax.ShapeDtypeStruct((lp, 2 * D_SB), F32)] + x_out_shapes,
        scratch_shapes=[pltpu.VMEM((ngrp, 2 * bq, 1), F32), pltpu.VMEM((ngrp, 2 * bq, LANES), F32),
                        pltpu.VMEM((ngrp, 2 * bq, LANES), F32)] + x_scratch,
        compiler_params=_params("arbitrary", "arbitrary"),
    )(qkv, qkv, qkv, tri, *x_args)
    return outs[0], outs[1], outs[2:]


def _outproj(c5, att, ew, h, w, g):
    lp, d = h.shape
    tm = _mm_tile(lp)

    def body(c5_ref, att_ref, sg_ref, h_ref, w_ref, g_ref, hn_ref, cat_ref, mix_ref):
        sg = sg_ref[...].astype(F32)
        s = att_ref[...] * (sg * _sigmoid(sg))
        cat_ref[:, 0:D_CONV] = c5_ref[...]
        cat_ref[:, D_CONV:] = s.astype(BF16)
        mixed = _nn(cat_ref[...], w_ref[...])
        mix_ref[...] = mixed
        rstd = lax.rsqrt(jnp.mean(mixed * mixed, axis=-1, keepdims=True) + RMS_EPS)
        hn_ref[...] = h_ref[...] + (mixed * rstd) * g_ref[...]

    half = pl.BlockSpec((tm, 512), lambda i: (i, 0))
    full = pl.BlockSpec((tm, d), lambda i: (i, 0))
    return _pcall(
        body, name="outproj_fwd", grid=(lp // tm,),
        in_specs=[half, half, pl.BlockSpec((tm, 512), lambda i: (i, 3)), full,
                  pl.BlockSpec((d, d), lambda i: (0, 0)), pl.BlockSpec((1, d), lambda i: (0, 0))],
        out_specs=[full, full, full],
        out_shape=[jax.ShapeDtypeStruct((lp, d), F32), jax.ShapeDtypeStruct((lp, d), BF16),
                   jax.ShapeDtypeStruct((lp, d), F32)],
        compiler_params=_params("parallel"),
    )(c5, att, ew, h, w, g)


def _loss_head(h, target, seq):
    lp, d = h.shape
    tm = ROW_TILE

    def body(h_ref, t_ref, dh_ref, loss_ref):
        i = pl.program_id(0)

        @pl.when(i == 0)
        def _():
            loss_ref[...] = jnp.zeros_like(loss_ref)

        row = i * tm + lax.broadcasted_iota(jnp.int32, (tm, 1), 0)
        real = jnp.logical_and(row >= N_META, row < N_META + seq)
        diff = jnp.where(real, h_ref[...] - t_ref[...], 0.0)
        dh_ref[...] = diff * (1.0 / d)
        loss_ref[...] += 0.5 * jnp.sum(jnp.sum(diff * diff, axis=-1, keepdims=True) * (1.0 / d))

    full = pl.BlockSpec((tm, d), lambda i: (i, 0))
    return _pcall(
        body, name="loss_head", grid=(lp // tm,),
        in_specs=[full, full],
        out_specs=[full, pl.BlockSpec((8, LANES), lambda i: (0, 0))],
        out_shape=[jax.ShapeDtypeStruct((lp, d), F32), jax.ShapeDtypeStruct((8, LANES), F32)],
        compiler_params=_params("arbitrary"),
    )(h, target)


def _outproj_bwd(dh, mixed, g, w, att, ew, c4):
    lp, d = dh.shape
    tm = _mm_tile(lp)

    def body(dh_ref, mix_ref, g_ref, w_ref, att_ref, cg_ref, sg_ref, c4_ref,
             dmix_ref, datt_ref, dsg_ref, dc4_ref, dcg_ref, dg_ref, db_ref):
        @pl.when(pl.program_id(0) == 0)
        def _():
            dg_ref[...] = jnp.zeros_like(dg_ref)
            db_ref[...] = jnp.zeros_like(db_ref)

        mixed = mix_ref[...]
        dhv = dh_ref[...]
        rstd = lax.rsqrt(jnp.mean(mixed * mixed, axis=-1, keepdims=True) + RMS_EPS)
        n = mixed * rstd
        dg_ref[...] += jnp.sum(dhv * n, axis=0, keepdims=True)
        dn = dhv * g_ref[...]
        dmix = (rstd * (dn - n * jnp.mean(dn * n, axis=-1, keepdims=True))).astype(BF16)
        dmix_ref[...] = dmix
        dcat = _nt(dmix, w_ref[...])
        dc5 = dcat[:, 0:D_CONV]
        ds = dcat[:, D_CONV:]
        silu_sg, dsilu_sg = _silu_fwd_bwd(sg_ref[...].astype(F32))
        datt_ref[...] = (ds * silu_sg).astype(BF16)
        dsg_ref[...] = (ds * att_ref[...] * dsilu_sg).astype(BF16)
        silu_cg, dsilu_cg = _silu_fwd_bwd(cg_ref[...].astype(F32))
        dc4 = dc5 * silu_cg
        db_ref[...] += jnp.sum(dc4, axis=0, keepdims=True)
        dc4_ref[...] = dc4.astype(BF16)
        dcg_ref[...] = (dc5 * c4_ref[...] * dsilu_cg).astype(BF16)

    half = pl.BlockSpec((tm, 512), lambda i: (i, 0))
    full = pl.BlockSpec((tm, d), lambda i: (i, 0))
    hb = jax.ShapeDtypeStruct((lp, 512), BF16)
    return _pcall(
        body, name="outproj_bwd", grid=(lp // tm,),
        in_specs=[full, full, pl.BlockSpec((1, d), lambda i: (0, 0)), pl.BlockSpec((d, d), lambda i: (0, 0)),
                  half, pl.BlockSpec((tm, 512), lambda i: (i, 2)), pl.BlockSpec((tm, 512), lambda i: (i, 3)), half],
        out_specs=[full, half, half, half, half,
                   pl.BlockSpec((1, d), lambda i: (0, 0)), pl.BlockSpec((1, 512), lambda i: (0, 0))],
        out_shape=[jax.ShapeDtypeStruct((lp, d), BF16), hb, hb, hb, hb,
                   jax.ShapeDtypeStruct((1, d), F32), jax.ShapeDtypeStruct((1, 512), F32)],
        compiler_params=_params("arbitrary"),
    )(dh, mixed, g, w, att, ew, ew, c4)


def _attn_bwd(qkv, carries, datt, tri, upper, plan=None):
    lp = qkv.shape[0]
    bq = ATT_BLOCK
    ngrp = ATT_PAIRS
    nstep = D_SB // (LANES * ngrp)
    nq = lp // bq
    split, x_args, x_in_specs, x_out_specs, x_out_shapes, x_scratch = _hosted(plan, 3, 2)

    def body(q_ref, k_ref, v_ref, carry_ref, do_ref, tri_ref, upper_ref, *rest):
        x_in, (dq_ref, dk_ref, dv_ref), x_out, (run_s, dq_s), x_sems = split(rest)
        i = pl.program_id(1)
        if plan is not None:
            @pl.when(jnp.logical_and(pl.program_id(0) == 0, i == 0))
            def _():
                plan.start(x_in, x_out, x_sems)

            @pl.when(jnp.logical_and(pl.program_id(0) == nstep - 1, i == int(RELAY_AT * nq)))
            def _():
                plan.relay(x_in, x_out, x_sems)

        @pl.when(i == 0)
        def _():
            dk_ref[...] = jnp.zeros_like(dk_ref)
            dv_ref[...] = jnp.zeros_like(dv_ref)

        first_head, vis = _attn_masks()
        lane = lax.broadcasted_iota(jnp.int32, (1, LANES), 1)
        cols = [slice(g * LANES, (g + 1) * LANES) for g in range(ngrp)]
        q2s = [_stack_heads(q_ref[:, cs], first_head) for cs in cols]
        do2s = [_stack_heads(do_ref[:, cs], first_head) for cs in cols]
        cmats = [jnp.concatenate([carry_ref[:, 2 * g * LANES:(2 * g + 1) * LANES],
                                  carry_ref[:, (2 * g + 1) * LANES:(2 * g + 2) * LANES]], axis=0)
                 for g in range(ngrp)]
        tri_m = tri_ref[...]
        upper_m = upper_ref[...]

        def blocks(js, masks):
            offs = [pl.multiple_of(j * bq, bq) for j in js]
            work = [(g, b) for b in range(len(js)) for g in range(ngrp)]
            zs = {(g, b): _nt(q2s[g], k_ref[pl.ds(offs[b], bq), cols[g]]) for g, b in work}
            lss = {}
            for g, b in work:
                z = zs[g, b]
                ls = -(jnp.maximum(z, 0.0) + jnp.log(1.0 + jnp.exp(-jnp.abs(z))))
                lss[g, b] = ls if masks[b] is None else jnp.where(masks[b], ls, 0.0)
            tails = {gb: _block_sums(lss[gb], tri_m) for gb in work}
            das = {(g, b): _nt(do2s[g], v_ref[pl.ds(offs[b], bq), cols[g]]) for g, b in work}
            probs, des = {}, {}
            for g, b in work:
                c = jnp.sum(jnp.where(lane == js[b], cmats[g], 0.0), axis=-1, keepdims=True)
                a = jnp.exp(zs[g, b] + lss[g, b] + tails[g, b] + c)
                a = a if masks[b] is None else jnp.where(masks[b], a, 0.0)
                probs[g, b] = a.astype(BF16)
                des[g, b] = das[g, b] * a
            prefixes = {gb: _block_sums(des[gb], upper_m) for gb in work}
            runs = [run_s[g] for g in range(ngrp)]
            dzs = {}
            for g, b in work:
                beta = jnp.exp(zs[g, b] + lss[g, b])
                dz = des[g, b] - beta * (des[g, b] + runs[g] + prefixes[g, b])
                dzs[g, b] = (dz if masks[b] is None else jnp.where(masks[b], dz, 0.0)).astype(BF16)
                runs[g] = runs[g] + prefixes[g, b][:, bq - 1:bq] + des[g, b][:, bq - 1:bq]
            for g in range(ngrp):
                run_s[g] = runs[g]
                dq = dq_s[g]
                for b in range(len(js)):
                    rows = pl.ds(offs[b], bq)
                    dq = dq + _nn(dzs[g, b], k_ref[rows, cols[g]])
                    dk_ref[rows, cols[g]] += _tn(dzs[g, b], q2s[g])
                    dv_ref[rows, cols[g]] += _tn(probs[g, b], do2s[g])
                dq_s[g] = dq

        n_done = jnp.max(carry_ref[:, COUNT_LANE:COUNT_LANE + 1]).astype(jnp.int32)
        n_done = jnp.clip(n_done, 1, i + 1)
        before = jnp.maximum(n_done - 2, 0)
        j0 = i - n_done + 1
        odd = before % 2
        run_s[...] = jnp.zeros_like(run_s)
        dq_s[...] = jnp.zeros_like(dq_s)

        @pl.when(odd == 1)
        def _():
            blocks([j0], [None])

        @pl.loop(0, before // 2)
        def _(t):
            blocks([j0 + odd + 2 * t, j0 + odd + 2 * t + 1], [None, None])

        @pl.when(n_done > 1)
        def _():
            blocks([i - 1, i], [None, vis])

        @pl.when(n_done <= 1)
        def _():
            blocks([i], [vis])

        for g in range(ngrp):
            dq_ref[:, cols[g]] = (_unstack_heads(dq_s[g], first_head) * Q_SCALE).astype(BF16)
        if plan is not None:
            @pl.when(jnp.logical_and(pl.program_id(0) == nstep - 1, i == nq - 1))
            def _():
                plan.finish(x_in, x_out, x_sems)

    width = ngrp * LANES
    once = pl.Buffered(1)
    qb = pl.BlockSpec((bq, width), lambda p, i: (i, p))
    colb = pl.BlockSpec((lp, width), lambda p, i: (0, p), pipeline_mode=once)
    sq = pl.BlockSpec((bq, bq), lambda p, i: (0, 0))
    outs = _pcall(
        body, name="attn_bwd" if plan is None else "attn_bwd_reduce", grid=(nstep, nq),
        in_specs=[qb,
                  pl.BlockSpec((lp, width), lambda p, i: (0, nstep + p), pipeline_mode=once),
                  pl.BlockSpec((lp, width), lambda p, i: (0, 2 * nstep + p), pipeline_mode=once),
                  pl.BlockSpec((bq, 2 * width), lambda p, i: (i, p)), qb, sq, sq] + x_in_specs,
        out_specs=[qb, colb, colb] + x_out_specs,
        out_shape=[jax.ShapeDtypeStruct((lp, D_SB), BF16), jax.ShapeDtypeStruct((lp, D_SB), F32),
                   jax.ShapeDtypeStruct((lp, D_SB), F32)] + x_out_shapes,
        scratch_shapes=[pltpu.VMEM((ngrp, 2 * bq, 1), F32), pltpu.VMEM((ngrp, 2 * bq, LANES), F32)] + x_scratch,
        compiler_params=_params("arbitrary", "arbitrary"),
    )(qkv, qkv, qkv, carries, datt, tri, upper, *x_args)
    return outs[0], outs[1], outs[2], outs[3:]


def _conv_bwd(dc4, c1, ew, cw, lng, lnb, wpw2):
    lp = ew.shape[0]
    tm = ROW_TILE
    nt = lp // tm
    halo_per_tile = tm // CONV_PAD

    def body(dc4_ref, c1_ref, ew_ref, halo_ref, cw_ref, lng_ref, lnb_ref, w_ref,
             dga_ref, dgb_ref, c3_ref, dcw_ref, dcb_ref, dlng_ref, dlnb_ref, xbuf, dbuf, shifted, wacc):
        step = pl.program_id(0)

        @pl.when(step == 0)
        def _():
            wacc[...] = jnp.zeros_like(wacc)
            dcb_ref[...] = jnp.zeros_like(dcb_ref)
            dlng_ref[...] = jnp.zeros_like(dlng_ref)
            dlnb_ref[...] = jnp.zeros_like(dlnb_ref)
            dbuf[tm:tm + CONV_PAD, :] = jnp.zeros((CONV_PAD, D_CONV), F32)

        dc3 = _nt(dc4_ref[...], w_ref[...])
        xhat, rstd = _layer_norm_stats(c1_ref[...])
        c2 = xhat * lng_ref[...] + lnb_ref[...]
        c3, dsilu = _silu_fwd_bwd(c2)
        c3_ref[...] = c3.astype(BF16)
        dc2 = dc3 * dsilu
        dlng_ref[...] += jnp.sum(dc2 * xhat, axis=0, keepdims=True)
        dlnb_ref[...] += jnp.sum(dc2, axis=0, keepdims=True)
        dxhat = dc2 * lng_ref[...]
        dc1 = rstd * (dxhat - jnp.mean(dxhat, axis=-1, keepdims=True)
                      - xhat * jnp.mean(dxhat * xhat, axis=-1, keepdims=True))
        dcb_ref[...] += jnp.sum(dc1, axis=0, keepdims=True)
        dbuf[0:tm, :] = dc1

        ga = ew_ref[:, 0:512].astype(F32)
        sgb = _sigmoid(ew_ref[:, 512:1024].astype(F32))
        xbuf[CONV_PAD:CONV_PAD + tm, :] = ga * sgb
        first_tile = step == nt - 1
        halo = halo_ref[:, 0:512].astype(F32) * _sigmoid(halo_ref[:, 512:1024].astype(F32))
        xbuf[0:CONV_PAD, :] = jnp.where(first_tile, 0.0, halo)

        for cb in range(D_CONV // LANES):
            cs = slice(cb * LANES, (cb + 1) * LANES)
            _shifted_copies(dbuf, cs, shifted, tm)
            for r0 in range(0, tm, CONV_ROWS):
                rs = slice(r0, r0 + CONV_ROWS)
                dc0 = jnp.zeros((CONV_ROWS, LANES), F32)
                for j in range(CONV_WIDTH):
                    dc0 = dc0 + _weighted(cw_ref, j, cs, _shifted_rows(
                        shifted, r0 + CONV_WIDTH - 1 - j, CONV_ROWS))
                dga_ref[rs, cs] = (dc0 * sgb[rs, cs]).astype(BF16)
                dgb_ref[rs, cs] = (dc0 * ga[rs, cs] * sgb[rs, cs] * (1.0 - sgb[rs, cs])).astype(BF16)
            _shifted_copies(xbuf, cs, shifted, tm)
            for r0 in range(0, tm, CONV_ROWS):
                d1 = dbuf[r0:r0 + CONV_ROWS, cs]

                for j in range(CONV_WIDTH):
                    prod = d1 * _shifted_rows(shifted, r0 + CONV_PAD - (CONV_WIDTH - 1) + j, CONV_ROWS)
                    wacc[j * 8:(j + 1) * 8, cs] += jnp.sum(prod.reshape(CONV_ROWS // 8, 8, LANES), axis=0)
        dbuf[tm:tm + CONV_PAD, :] = dbuf[0:CONV_PAD, :]

        @pl.when(step == nt - 1)
        def _():
            dcw_ref[...] = jnp.sum(wacc[...].reshape(CONV_PAD, 8, D_CONV), axis=1)

    rev = lambda i: (nt - 1 - i, 0)
    row = pl.BlockSpec((tm, D_CONV), rev)
    vec = pl.BlockSpec((1, D_CONV), lambda i: (0, 0))
    hb = jax.ShapeDtypeStruct((lp, D_CONV), BF16)
    vs = jax.ShapeDtypeStruct((1, D_CONV), F32)
    return _pcall(
        body, name="conv_bwd", grid=(nt,),
        in_specs=[row, row, pl.BlockSpec((tm, 1024), rev),
                  pl.BlockSpec((CONV_PAD, 1024), lambda i: (jnp.maximum((nt - 1 - i) * halo_per_tile - 1, 0), 0)),
                  pl.BlockSpec((8 * CONV_PAD, D_CONV), lambda i: (0, 0)), vec, vec,
                  pl.BlockSpec((D_CONV, D_CONV), lambda i: (0, 0))],
        out_specs=[row, row, row, pl.BlockSpec((CONV_PAD, D_CONV), lambda i: (0, 0)), vec, vec, vec],
        out_shape=[hb, hb, hb, jax.ShapeDtypeStruct((CONV_PAD, D_CONV), F32), vs, vs, vs],
        scratch_shapes=[pltpu.VMEM((tm + CONV_PAD, D_CONV), F32), pltpu.VMEM((tm + CONV_PAD, D_CONV), F32),
                        pltpu.VMEM((8, tm + CONV_PAD, LANES), F32), pltpu.VMEM((8 * CONV_PAD, D_CONV), F32)],
        compiler_params=_params("arbitrary"),
    )(dc4, c1, ew, ew, cw, lng, lnb, wpw2)


def _inproj_bwd(dga, dgb, dcg, dq, dk, dv, dsg, h, g, w, dh_out):
    lp, d = h.shape
    n = w.shape[1]
    tm = _mm_tile(lp)

    def body(dga_ref, dgb_ref, dcg_ref, dq_ref, dk_ref, dv_ref, dsg_ref, h_ref, g_ref, w_ref, dho_ref,
             dh_ref, dproj_ref, u_ref, dg_ref):
        @pl.when(pl.program_id(0) == 0)
        def _():
            dg_ref[...] = jnp.zeros_like(dg_ref)

        dproj_ref[:, 0:512] = dga_ref[...]
        dproj_ref[:, 512:1024] = dgb_ref[...]
        dproj_ref[:, 1024:1536] = dcg_ref[...]
        dproj_ref[:, 1536:2048] = dq_ref[...]
        dproj_ref[:, 2048:2560] = dk_ref[...].astype(BF16)
        dproj_ref[:, 2560:3072] = dv_ref[...].astype(BF16)
        dproj_ref[:, 3072:3584] = dsg_ref[...]
        du = _nt(dproj_ref[...], w_ref[...])
        x = h_ref[...]
        rstd = lax.rsqrt(jnp.mean(x * x, axis=-1, keepdims=True) + RMS_EPS)
        nrm = x * rstd
        u_ref[...] = (nrm * g_ref[...]).astype(BF16)
        dg_ref[...] += jnp.sum(du * nrm, axis=0, keepdims=True)
        dn = du * g_ref[...]
        dh_ref[...] = dho_ref[...] + rstd * (dn - nrm * jnp.mean(dn * nrm, axis=-1, keepdims=True))

    half = pl.BlockSpec((tm, 512), lambda i: (i, 0))
    full = pl.BlockSpec((tm, d), lambda i: (i, 0))
    return _pcall(
        body, name="inproj_bwd", grid=(lp // tm,),
        in_specs=[half] * 7 + [full, pl.BlockSpec((1, d), lambda i: (0, 0)),
                               pl.BlockSpec((d, n), lambda i: (0, 0)), full],
        out_specs=[full, pl.BlockSpec((tm, n), lambda i: (i, 0)), full, pl.BlockSpec((1, d), lambda i: (0, 0))],
        out_shape=[jax.ShapeDtypeStruct((lp, d), F32), jax.ShapeDtypeStruct((lp, n), BF16),
                   jax.ShapeDtypeStruct((lp, d), BF16), jax.ShapeDtypeStruct((1, d), F32)],
        compiler_params=_params("arbitrary"),
    )(dga, dgb, dcg, dq, dk, dv, dsg, h, g, w, dh_out)


def _row_split(m, parts):
    tm = m // parts
    assert tm * parts == m and tm % 16 == 0, (m, parts)
    return tm


def _matmul_tn(x, dy, tn, name):
    m, k = x.shape
    n = dy.shape[1]
    steps = 4 if m % 64 == 0 else 1
    tm = _row_split(m, steps)

    def body(x_ref, dy_ref, o_ref, acc_ref):
        r = pl.program_id(1)

        @pl.when(r == 0)
        def _():
            acc_ref[...] = jnp.zeros_like(acc_ref)

        acc_ref[...] += _tn(x_ref[...], dy_ref[...])

        @pl.when(r == steps - 1)
        def _():
            o_ref[...] = acc_ref[...].astype(BF16)

    return _pcall(
        body, name=name, grid=(n // tn, steps),
        in_specs=[pl.BlockSpec((tm, k), lambda j, r: (r, 0)), pl.BlockSpec((tm, tn), lambda j, r: (r, j))],
        out_specs=pl.BlockSpec((k, tn), lambda j, r: (0, j)),
        out_shape=jax.ShapeDtypeStruct((k, n), BF16),
        scratch_shapes=[pltpu.VMEM((k, tn), F32)],
        compiler_params=_params("parallel", "arbitrary"),
    )(x, dy)


def _local_step(h0, target_p, seq, vecs, depth, all_weights=None, w_in0=None, gather_w_in=None, gather_rest=None,
                reduce_layer=None):
    pre_g, post_g, conv_b, ln_g, ln_b, b_pw2 = vecs
    ar = jnp.arange(ATT_BLOCK)
    tri = (ar[:, None] > ar[None, :]).astype(BF16)
    upper = (ar[:, None] < ar[None, :]).astype(BF16)
    row = lambda a, l: a[l][None, :]

    hosted = all_weights is None
    weights = [None] * depth if hosted else list(all_weights)
    next_w_in, next_rest = w_in0, None
    saved = []
    h = h0
    for l in range(depth):
        more = hosted and l + 1 < depth
        if hosted:
            ew, qkv, rest = _inproj(h, row(pre_g, l), next_w_in, gather_rest(0) if l == 0 else None)
            w_in, (w_pw2, w_out, conv_w) = next_w_in, (rest if l == 0 else next_rest)
        else:
            w_in, w_pw2, w_out, conv_w = weights[l]
            ew, qkv, _ = _inproj(h, row(pre_g, l), w_in)
        conv_w = jnp.repeat(conv_w, 8, axis=0)
        weights[l] = (w_in, w_pw2, w_out, conv_w)
        c1, c4, c5, next_rest = _conv_fwd(ew, conv_w, row(conv_b, l), row(ln_g, l), row(ln_b, l), w_pw2,
                                          row(b_pw2, l), gather_rest(l + 1) if more else None)
        att, carries, gathered = _attn_fwd(qkv, tri, gather_w_in(l + 1) if more else None)
        if more:
            next_w_in = gathered[0]
        hn, cat, mixed = _outproj(c5, att, ew, h, w_out, row(post_g, l))
        saved.append((h, ew, qkv, c1, c4, att, carries, cat, mixed))
        h = hn

    dh, loss = _loss_head(h, target_p, seq)

    vec_grads = [None] * depth
    mat_grads = [None] * depth
    pending = None
    for l in reversed(range(depth)):
        w_in, w_pw2, w_out, conv_w = weights[l]
        h_in, ew, qkv, c1, c4, att, carries, cat, mixed = saved[l]
        dmix, datt, dsg, dc4, dcg, dpost, dbpw2 = _outproj_bwd(dh, mixed, row(post_g, l), w_out, att, ew, c4)
        dw_out = _matmul_tn(cat, dmix, 512, "dw_out")
        dq, dk, dv, landed = _attn_bwd(qkv, carries, datt, tri, upper, pending)
        if pending is not None:
            mat_grads[l + 1] = landed
        dga, dgb, c3, dcw, dcb, dlng, dlnb = _conv_bwd(dc4, c1, ew, conv_w, row(ln_g, l), row(ln_b, l), w_pw2)
        dw_pw2 = _matmul_tn(c3, dc4, 512, "dw_pw2")
        dh, dproj, u, dpre = _inproj_bwd(dga, dgb, dcg, dq, dk, dv, dsg, h_in, row(pre_g, l), w_in, dh)
        dw_in = _matmul_tn(u, dproj, 1792, "dw_in")
        vec_grads[l] = (dpre[0], dpost[0], dcb[0], dlng[0], dlnb[0], dbpw2[0])
        mats = (dw_in, dw_pw2, dw_out, dcw)
        if reduce_layer is None:
            mat_grads[l] = mats
        else:
            pending = reduce_layer(mats)
    if pending is not None:
        mat_grads[0] = _run_exchange(pending, "reduce_grads")

    vec_grads = [jnp.stack([g[k] for g in vec_grads]) for k in range(len(vecs))]
    return loss[0, 0], dh, vec_grads, mat_grads


N_CHIPS = 4
ANY = pl.BlockSpec(memory_space=pl.ANY)


def _chip_peers():
    x, y, c = lax.axis_index("x"), lax.axis_index("y"), lax.axis_index("c")
    return x, y, c, [(x, 1 - y), (1 - x, y), (1 - x, 1 - y)]


def _shard_slices(refs, dims, idx):
    out = []
    for ref, (axis, size) in zip(refs, dims):
        assert size % LANES == 0
        start = pl.multiple_of(idx * size, LANES)
        sl = [slice(None)] * len(ref.shape)
        sl[axis] = pl.ds(start, size)
        out.append(ref.at[tuple(sl)])
    return out


class _Exchange(NamedTuple):
    inputs: list
    out_shapes: list
    scratch: list
    start: Callable
    relay: Callable
    finish: Callable


def _run_exchange(plan, name):
    n_in, n_out = len(plan.inputs), len(plan.out_shapes)

    def body(*refs):
        parts = refs[:n_in], refs[n_in:n_in + n_out], refs[n_in + n_out:]
        plan.start(*parts)
        plan.relay(*parts)
        plan.finish(*parts)

    return _pcall(body, name=name, in_specs=[ANY] * n_in, out_specs=[ANY] * n_out, out_shape=plan.out_shapes,
                  scratch_shapes=plan.scratch)(*plan.inputs)


def _gather_plan(shards, dims):
    n = len(shards)
    full_shapes = []
    halves = []
    for s, (axis, size) in zip(shards, dims):
        shp = list(s.shape)
        shp[axis] = size * N_CHIPS
        full_shapes.append(jax.ShapeDtypeStruct(tuple(shp), s.dtype))
        tile_rows = 32 // s.dtype.itemsize
        assert s.shape[0] % (2 * tile_rows) == 0
        halves.append((s.shape[0] // 2, tile_rows))

    def half(refs, which):
        return [r.at[pl.ds(pl.multiple_of(which * h, t), h)] for r, (h, t) in zip(refs, halves)]

    def copies(srcs, outs, sems):
        send, recv, loc = sems
        x, y, c, peers = _chip_peers()
        sibling = (x, y, 1 - c)
        mine = _shard_slices(outs, dims, 2 * x + y)
        local = [pltpu.make_async_copy(s, d, loc.at[a]) for a, (s, d) in enumerate(zip(srcs, mine))]

        def remote(src, dst, slot, a, dev):
            return pltpu.make_async_remote_copy(src, dst, send.at[slot, a], recv.at[slot, a],
                                                device_id=dev, device_id_type=MESH)

        sends = [remote(s, d, k, a, (px, py, c))
                 for k, (px, py) in enumerate(peers) for a, (s, d) in enumerate(zip(half(srcs, c), half(mine, c)))]
        theirs = [_shard_slices(outs, dims, 2 * px + py) for px, py in peers]
        arrivals = [remote(s, d, k, a, (px, py, c))
                    for k, (px, py) in enumerate(peers)
                    for a, (s, d) in enumerate(zip(half(srcs, c), half(theirs[k], c)))]
        passed_on = [remote(d, d, 3 + k, a, sibling) for k in range(3) for a, d in enumerate(half(theirs[k], c))]
        from_sibling = [remote(d, d, 3 + k, a, sibling)
                        for k in range(3) for a, d in enumerate(half(theirs[k], 1 - c))]
        return local, sends, arrivals, passed_on, from_sibling

    def start(srcs, outs, sems):
        local, sends = copies(srcs, outs, sems)[:2]
        for cp in local + sends:
            cp.start()

    def relay(srcs, outs, sems):
        _, _, arrivals, passed_on, _ = copies(srcs, outs, sems)
        for arrived, onward in zip(arrivals, passed_on):
            arrived.wait_recv()
            onward.start()

    def finish(srcs, outs, sems):
        local, sends, _, passed_on, from_sibling = copies(srcs, outs, sems)
        for cp in from_sibling:
            cp.wait_recv()
        for cp in sends + passed_on:
            cp.wait_send()
        for cp in local:
            cp.wait()

    scratch = [pltpu.SemaphoreType.DMA((6, n)), pltpu.SemaphoreType.DMA((6, n)), pltpu.SemaphoreType.DMA((n,))]
    return _Exchange(list(shards), full_shapes, scratch, start, relay, finish)


def _reduce_plan(grads, dims):
    n = len(grads)
    piece_shapes = []
    for g, (axis, size) in zip(grads, dims):
        shp = list(g.shape)
        shp[axis] = size
        piece_shapes.append(jax.ShapeDtypeStruct((N_CHIPS,) + tuple(shp), g.dtype))

    def copies(srcs, outs, sems):
        mine, theirs = outs[:n], outs[n:]
        send, recv, loc = sems
        x, y, c, peers = _chip_peers()
        sibling = (x, y, 1 - c)
        own = _shard_slices(srcs, dims, 2 * x + y)

        def remote(src, dst, slot, a, dev):
            return pltpu.make_async_remote_copy(src, dst, send.at[slot, a], recv.at[slot, a],
                                                device_id=dev, device_id_type=MESH)

        local = [pltpu.make_async_copy(own[a], mine[a].at[3], loc.at[a]) for a in range(n)]
        to_sibling = [remote(own[a], theirs[a].at[3], 3, a, sibling) for a in range(n)]
        to_chips = [remote(src, mine[a].at[k], k, a, (px, py, c))
                    for k, (px, py) in enumerate(peers)
                    for a, src in enumerate(_shard_slices(srcs, dims, 2 * px + py))]
        passed_on = [remote(mine[a].at[k], theirs[a].at[k], 4 + k, a, sibling) for k in range(3) for a in range(n)]
        return local, to_sibling, to_chips, passed_on

    def start(srcs, outs, sems):
        local, to_sibling, to_chips, _ = copies(srcs, outs, sems)
        for cp in local + to_sibling + to_chips:
            cp.start()

    def relay(srcs, outs, sems):
        _, _, to_chips, passed_on = copies(srcs, outs, sems)
        for arrived, onward in zip(to_chips, passed_on):
            arrived.wait_recv()
            onward.start()

    def finish(srcs, outs, sems):
        local, to_sibling, to_chips, passed_on = copies(srcs, outs, sems)
        for cp in to_sibling + passed_on:
            cp.wait_recv()
        for cp in to_sibling + to_chips + passed_on:
            cp.wait_send()
        for cp in local:
            cp.wait()

    scratch = [pltpu.SemaphoreType.DMA((7, n)), pltpu.SemaphoreType.DMA((7, n)), pltpu.SemaphoreType.DMA((n,))]
    return _Exchange(list(grads), piece_shapes * 2, scratch, start, relay, finish)


def _allsum_small(pack):
    rows, cols = pack.shape
    ndev = 8

    def body(p_ref, o_ref, buf, send, recv):
        x, y, c = lax.axis_index("x"), lax.axis_index("y"), lax.axis_index("c")
        me = 4 * x + 2 * y + c
        buf[me] = p_ref[...]
        started = []
        for r in range(1, ndev):
            bx, by, bc = (r >> 2) & 1, (r >> 1) & 1, r & 1
            dev = (x ^ bx, y ^ by, c ^ bc)
            cp = pltpu.make_async_remote_copy(p_ref, buf.at[me], send.at[r], recv.at[r],
                                              device_id=dev, device_id_type=MESH)
            cp.start()
            started.append(cp)
        for r in range(1, ndev):
            pltpu.make_async_remote_copy(p_ref, buf.at[me ^ r], send.at[r], recv.at[r],
                                         device_id=(x, y, c), device_id_type=MESH).wait_recv()
        for cp in started:
            cp.wait_send()
        acc = buf[0]
        for d in range(1, ndev):
            acc = acc + buf[d]
        o_ref[...] = acc

    vm = pl.BlockSpec(memory_space=pltpu.VMEM)
    return _pcall(
        body, name="allsum_small", in_specs=[vm], out_specs=vm,
        out_shape=jax.ShapeDtypeStruct((rows, cols), F32),
        scratch_shapes=[pltpu.VMEM((ndev, rows, cols), F32), pltpu.SemaphoreType.DMA((ndev,)),
                        pltpu.SemaphoreType.DMA((ndev,))],
    )(pack)


def _adamw(parts, w, m, v, layer, prev, name):
    _, rows, cols = w.shape
    tr = ROW_TILE if rows % ROW_TILE == 0 else rows
    counts = [p.shape[0] for p in parts]
    n_parts = len(parts)
    n_prev = 0 if prev is None else 4

    def body(*refs):
        part_refs = refs[:n_parts]
        w_ref, m_ref, v_ref = refs[n_parts:n_parts + 3]
        g_ref, d_ref, nm_ref, nv_ref = refs[n_parts + 3 + n_prev:]
        g = None
        for p_ref, cnt in zip(part_refs, counts):
            s = p_ref[0].astype(F32)
            for k in range(1, cnt):
                s = s + p_ref[k].astype(F32)
            g = s if g is None else g + s
        m2 = ADAM_B1 * m_ref[0] + (1.0 - ADAM_B1) * g
        v2 = ADAM_B2 * v_ref[0] + (1.0 - ADAM_B2) * (g * g)
        m_hat = m2 / (1.0 - ADAM_B1 ** ADAM_STEP)
        v_hat = v2 / (1.0 - ADAM_B2 ** ADAM_STEP)
        g_ref[0] = g
        d_ref[0] = -ADAM_LR * (m_hat / (jnp.sqrt(v_hat) + ADAM_EPS) + ADAM_WD * w_ref[0])
        nm_ref[0] = m2
        nv_ref[0] = v2

    blk = pl.BlockSpec((1, tr, cols), lambda i: (layer, i, 0))
    shp = jax.ShapeDtypeStruct(w.shape, F32)
    return _pcall(
        body, name=name, grid=(rows // tr,),
        in_specs=[pl.BlockSpec((cnt, tr, cols), lambda i: (0, i, 0)) for cnt in counts] + [blk] * 3 + [ANY] * n_prev,
        out_specs=[blk] * 4, out_shape=[shp] * 4,
        input_output_aliases={n_parts + 3 + k: k for k in range(n_prev)},
        compiler_params=_params("parallel"),
    )(*parts, w, m, v, *(prev or ()))


def kernel(x, meta_tokens, pre_norm_g, post_norm_g, w_in, conv_w, conv_b, conv_ln_g, conv_ln_b, w_pw2, b_pw2, w_out, loss_target, m_meta_tokens, m_pre_norm_g, m_post_norm_g, m_w_in, m_conv_w, m_conv_b, m_conv_ln_g, m_conv_ln_b, m_w_pw2, m_b_pw2, m_w_out, v_meta_tokens, v_pre_norm_g, v_post_norm_g, v_w_in, v_conv_w, v_conv_b, v_conv_ln_g, v_conv_ln_b, v_w_pw2, v_b_pw2, v_w_out):
    seq, d = x.shape[1], x.shape[2]
    depth = w_in.shape[0]
    length = N_META + seq
    lp = -(-length // ATT_BLOCK) * ATT_BLOCK
    tap_pad = ((0, 0), (0, CONV_PAD - CONV_WIDTH), (0, 0))

    shards = (w_in.astype(BF16), w_pw2.astype(BF16), w_out.astype(BF16), jnp.pad(conv_w, tap_pad))
    dims = [(1, w_in.shape[2]), (0, w_pw2.shape[1]), (0, w_out.shape[1]), (1, conv_w.shape[2])]
    layer_shards = lambda l: [s[l] for s in shards]

    w_in0, meta_f = _run_exchange(_gather_plan([shards[0][0], meta_tokens], [dims[0], (1, meta_tokens.shape[1])]),
                                  "gather_weights")

    h0 = jnp.concatenate([meta_f, x[0], jnp.zeros((lp - length, d), F32)], axis=0)
    target_p = jnp.pad(loss_target[0], ((N_META, lp - length), (0, 0)))
    vecs = (pre_norm_g, post_norm_g, conv_b, conv_ln_g, conv_ln_b, b_pw2)
    loss, dh0, vec_grads, pieces = _local_step(
        h0, target_p, seq, vecs, depth, w_in0=w_in0,
        gather_w_in=lambda l: _gather_plan([shards[0][l]], dims[:1]),
        gather_rest=lambda l: _gather_plan(layer_shards(l)[1:], dims[1:]),
        reduce_layer=lambda grads: _reduce_plan(list(grads), dims))

    def update(k, w, m, v, name):
        outs = None
        for l in reversed(range(depth)):
            outs = _adamw([pieces[l][k], pieces[l][4 + k]], w, m, v, l, outs, name)
        return outs

    up_w_in = update(0, w_in, m_w_in, v_w_in, "adamw_w_in")
    up_w_pw2 = update(1, w_pw2, m_w_pw2, v_w_pw2, "adamw_w_pw2")
    up_w_out = update(2, w_out, m_w_out, v_w_out, "adamw_w_out")
    up_conv_w = [o[:, :CONV_WIDTH] for o in update(3, jnp.pad(conv_w, tap_pad), jnp.pad(m_conv_w, tap_pad),
                                                   jnp.pad(v_conv_w, tap_pad, constant_values=1.0), "adamw_conv_w")]

    two = lambda a: a.reshape(-1, d)
    vec_rows = [two(g) for g in vec_grads]
    n_vec = sum(a.shape[0] for a in vec_rows)
    pack = jnp.concatenate(vec_rows + [dh0[:N_META], jnp.full((8, d), loss, F32)], axis=0)
    pack = jnp.pad(pack, ((0, -pack.shape[0] % 8), (0, 0)))
    tot = _allsum_small(pack)
    loss_all = tot[n_vec + N_META, 0]

    cat = lambda arrs: jnp.concatenate([two(t) for t in arrs], axis=0)[None]
    small_m = (m_pre_norm_g, m_post_norm_g, m_conv_b, m_conv_ln_g, m_conv_ln_b, m_b_pw2)
    small_v = (v_pre_norm_g, v_post_norm_g, v_conv_b, v_conv_ln_g, v_conv_ln_b, v_b_pw2)
    up_small = _adamw([tot[None, :n_vec]], cat(vecs), cat(small_m), cat(small_v), 0, None, "adamw_vectors")

    def unpack(o):
        res, r0 = [], 0
        for t in vecs:
            nrow = t.size // d
            res.append(o[0, r0:r0 + nrow].reshape(t.shape))
            r0 += nrow
        return res

    up_small = [unpack(o) for o in up_small]
    chip = 2 * lax.axis_index("x") + lax.axis_index("y")
    mcols = meta_tokens.shape[1]
    g_meta = lax.dynamic_slice_in_dim(tot[n_vec:n_vec + N_META], chip * mcols, mcols, axis=1)
    up_meta = [o[0] for o in _adamw([g_meta[None]], meta_tokens[None], m_meta_tokens[None], v_meta_tokens[None],
                                    0, None, "adamw_meta")]

    grad_x = dh0[N_META:length][None]
    outs = [loss_all, grad_x]
    for j in range(4):
        pre, post, cb, lg, lb, bp = up_small[j]
        outs += [up_meta[j], pre, post, up_w_in[j], up_conv_w[j], cb, lg, lb, up_w_pw2[j], bp, up_w_out[j]]
    return tuple(outs)
```

```python
from typing import Callable, NamedTuple

import jax
import jax.numpy as jnp
from jax import lax
from jax.experimental import pallas as pl
from jax.experimental.pallas import tpu as pltpu

F32 = jnp.float32
BF16 = jnp.bfloat16

N_META = 16
D_CONV = 512
D_SB = 512
HEAD_DIM = 64
CONV_WIDTH = 31
CONV_PAD = 32
CONV_ROWS = 128
RMS_EPS = 1e-6
LN_EPS = 1e-5
Q_SCALE = HEAD_DIM ** -0.5

ADAM_LR = 0.001
ADAM_B1 = 0.9
ADAM_B2 = 0.999
ADAM_EPS = 1e-08
ADAM_WD = 0.01
ADAM_STEP = 10

LANES = 128
ROW_TILE = 256
MM_TILE_MAX = 544
ATT_BLOCK = 256
ATT_PAIRS = 4
ATT_PAIRS_FWD = 4
VMEM_LIMIT = 56 * 1024 * 1024
EXP_ZERO = -104.0
COUNT_LANE = LANES - 1
RELAY_AT = 0.75

MESH = pl.DeviceIdType.MESH


def _pcall(body, **kw):
    return pl.pallas_call(body, **kw)


def _params(*sem):
    return pltpu.CompilerParams(dimension_semantics=sem, vmem_limit_bytes=VMEM_LIMIT)


def _sigmoid(x):
    return 1.0 / (1.0 + jnp.exp(-x))


def _silu_fwd_bwd(x):
    s = _sigmoid(x)
    return x * s, s * (1.0 + x * (1.0 - s))


def _nt(a, b):
    return lax.dot_general(a, b, (((1,), (1,)), ((), ())), preferred_element_type=F32)


def _tn(a, b):
    return lax.dot_general(a, b, (((0,), (0,)), ((), ())), preferred_element_type=F32)


def _nn(a, b):
    return jnp.dot(a, b, preferred_element_type=F32)


def _mm_tile(rows):
    return max(t for t in range(16, MM_TILE_MAX + 1, 16) if rows % t == 0)


def _host_begin(plan, parts, steps):
    if plan is not None:
        x_in, _, x_out, _, x_sems = parts

        @pl.when(pl.program_id(0) == 0)
        def _():
            plan.start(x_in, x_out, x_sems)

        @pl.when(pl.program_id(0) == int(RELAY_AT * steps))
        def _():
            plan.relay(x_in, x_out, x_sems)


def _host_end(plan, parts, steps):
    if plan is not None:
        x_in, _, x_out, _, x_sems = parts

        @pl.when(pl.program_id(0) == steps - 1)
        def _():
            plan.finish(x_in, x_out, x_sems)


def _inproj(h, g, w, plan=None):
    lp, d = h.shape
    n = w.shape[1]
    tm = _mm_tile(lp)
    steps = lp // tm
    split, x_args, x_in_specs, x_out_specs, x_out_shapes, x_scratch = _hosted(plan, 2, 0)

    def body(h_ref, g_ref, w_ref, *rest):
        parts = split(rest)
        ew_ref, qkv_ref = parts[1]
        _host_begin(plan, parts, steps)
        x = h_ref[...]
        rstd = lax.rsqrt(jnp.mean(x * x, axis=-1, keepdims=True) + RMS_EPS)
        u = ((x * rstd) * g_ref[...]).astype(BF16)
        p = _nn(u, w_ref[...])
        ew_ref[:, 0:1536] = p[:, 0:1536].astype(BF16)
        ew_ref[:, 1536:2048] = p[:, 3072:3584].astype(BF16)
        qkv_ref[:, 0:512] = (p[:, 1536:2048] * Q_SCALE).astype(BF16)
        qkv_ref[:, 512:1536] = p[:, 2048:3072].astype(BF16)
        _host_end(plan, parts, steps)

    outs = _pcall(
        body, name="inproj_fwd" if plan is None else "inproj_fwd_gather", grid=(steps,),
        in_specs=[pl.BlockSpec((tm, d), lambda i: (i, 0)),
                  pl.BlockSpec((1, d), lambda i: (0, 0)),
                  pl.BlockSpec((d, n), lambda i: (0, 0))] + x_in_specs,
        out_specs=[pl.BlockSpec((tm, 2048), lambda i: (i, 0)),
                   pl.BlockSpec((tm, 1536), lambda i: (i, 0))] + x_out_specs,
        out_shape=[jax.ShapeDtypeStruct((lp, 2048), BF16), jax.ShapeDtypeStruct((lp, 1536), BF16)] + x_out_shapes,
        scratch_shapes=x_scratch,
        compiler_params=_params("parallel" if plan is None else "arbitrary"),
    )(h, g, w, *x_args)
    return outs[0], outs[1], outs[2:]


def _layer_norm_stats(c1):
    mu = jnp.mean(c1, axis=-1, keepdims=True)
    xc = c1 - mu
    var = jnp.mean(xc * xc, axis=-1, keepdims=True)
    rstd = lax.rsqrt(var + LN_EPS)
    return xc * rstd, rstd


def _shifted_copies(window, cols, shifted, tm):
    shifted[0] = window[:, cols]
    rows = tm + CONV_PAD - 8
    for b in range(1, 8):
        shifted[b, 0:rows, :] = window[pl.ds(b, rows), cols]


def _shifted_rows(shifted, shift, tm):
    b = shift % 8
    return shifted[b, pl.ds(pl.multiple_of(shift - b, 8), tm), :]


def _weighted(cw8_ref, j, cols, rows):
    w8 = cw8_ref[pl.ds(pl.multiple_of(j * 8, 8), 8), cols]
    r = rows.shape[0]
    return (rows.reshape(r // 8, 8, LANES) * w8[None]).reshape(r, LANES)


def _conv_fwd(ew, cw, cb, lng, lnb, wpw2, bpw2, plan=None):
    lp = ew.shape[0]
    tm = ROW_TILE
    steps = lp // tm
    split, x_args, x_in_specs, x_out_specs, x_out_shapes, x_scratch = _hosted(plan, 3, 2)

    def body(ew_ref, cw_ref, cb_ref, lng_ref, lnb_ref, w_ref, b_ref, *rest):
        parts = split(rest)
        (c1_ref, c4_ref, c5_ref), (xbuf, shifted) = parts[1], parts[3]
        _host_begin(plan, parts, steps)
        @pl.when(pl.program_id(0) == 0)
        def _():
            xbuf[0:CONV_PAD, :] = jnp.zeros((CONV_PAD, D_CONV), F32)

        ga = ew_ref[:, 0:512].astype(F32)
        gb = ew_ref[:, 512:1024].astype(F32)
        cg = ew_ref[:, 1024:1536].astype(F32)
        xbuf[CONV_PAD:CONV_PAD + tm, :] = ga * _sigmoid(gb)
        for blk in range(D_CONV // LANES):
            cs = slice(blk * LANES, (blk + 1) * LANES)
            _shifted_copies(xbuf, cs, shifted, tm)
            for r0 in range(0, tm, CONV_ROWS):
                acc = jnp.zeros((CONV_ROWS, LANES), F32) + cb_ref[:, cs]
                for j in range(CONV_WIDTH):
                    acc = acc + _weighted(cw_ref, j, cs, _shifted_rows(
                        shifted, r0 + CONV_PAD - (CONV_WIDTH - 1) + j, CONV_ROWS))
                c1_ref[r0:r0 + CONV_ROWS, cs] = acc
        xbuf[0:CONV_PAD, :] = xbuf[tm:tm + CONV_PAD, :]
        xhat, _ = _layer_norm_stats(c1_ref[...])
        c2 = xhat * lng_ref[...] + lnb_ref[...]
        c3 = c2 * _sigmoid(c2)
        c4 = _nn(c3.astype(BF16), w_ref[...]) + b_ref[...]
        c4_ref[...] = c4
        c5_ref[...] = (c4 * (cg * _sigmoid(cg))).astype(BF16)
        _host_end(plan, parts, steps)

    vec = pl.BlockSpec((1, D_CONV), lambda i: (0, 0))
    row = pl.BlockSpec((tm, D_CONV), lambda i: (i, 0))
    outs = _pcall(
        body, name="conv_fwd" if plan is None else "conv_fwd_gather", grid=(steps,),
        in_specs=[pl.BlockSpec((tm, 1536), lambda i: (i, 0)),
                  pl.BlockSpec((8 * CONV_PAD, D_CONV), lambda i: (0, 0)),
                  vec, vec, vec,
                  pl.BlockSpec((D_CONV, D_CONV), lambda i: (0, 0)),
                  vec] + x_in_specs,
        out_specs=[row, row, row] + x_out_specs,
        out_shape=[jax.ShapeDtypeStruct((lp, D_CONV), F32), jax.ShapeDtypeStruct((lp, D_CONV), F32),
                   jax.ShapeDtypeStruct((lp, D_CONV), BF16)] + x_out_shapes,
        scratch_shapes=[pltpu.VMEM((tm + CONV_PAD, D_CONV), F32),
                        pltpu.VMEM((8, tm + CONV_PAD, LANES), F32)] + x_scratch,
        compiler_params=_params("arbitrary"),
    )(ew, cw, cb, lng, lnb, wpw2, bpw2, *x_args)
    return outs[0], outs[1], outs[2], outs[3:]


def _block_sums(x, m01):
    return _nn(x.astype(BF16), m01)


def _attn_masks():
    lane = lax.broadcasted_iota(jnp.int32, (1, LANES), 1)
    row = lax.broadcasted_iota(jnp.int32, (2 * ATT_BLOCK, ATT_BLOCK), 0)
    col = lax.broadcasted_iota(jnp.int32, (2 * ATT_BLOCK, ATT_BLOCK), 1)
    return lane < HEAD_DIM, col < (row & (ATT_BLOCK - 1))


def _stack_heads(x, first_head):
    zero = jnp.zeros_like(x)
    return jnp.concatenate([jnp.where(first_head, x, zero), jnp.where(first_head, zero, x)], axis=0)


def _unstack_heads(x2, first_head):
    rows = x2.shape[0] // 2
    return jnp.where(first_head, x2[:rows], x2[rows:])


def _hosted(plan, n_out, n_scratch):
    n_in = 0 if plan is None else len(plan.inputs)
    n_x = 0 if plan is None else len(plan.out_shapes)

    def split(rest):
        a, b, c = n_in + n_out, n_in + n_out + n_x, n_in + n_out + n_x + n_scratch
        return rest[:n_in], rest[n_in:a], rest[a:b], rest[b:c], rest[c:]

    if plan is None:
        return split, [], [], [], [], []
    return split, list(plan.inputs), [ANY] * n_in, [ANY] * n_x, list(plan.out_shapes), list(plan.scratch)


def _attn_fwd(qkv, tri, plan=None):
    lp = qkv.shape[0]
    bq = ATT_BLOCK
    ngrp = ATT_PAIRS_FWD
    nstep = D_SB // (LANES * ngrp)
    nq = lp // bq
    assert nq <= COUNT_LANE
    split, x_args, x_in_specs, x_out_specs, x_out_shapes, x_scratch = _hosted(plan, 2, 3)

    def body(q_ref, k_ref, v_ref, tri_ref, *rest):
        x_in, (o_ref, carry_ref), x_out, (c_s, acc_s, cm_s), x_sems = split(rest)
        i = pl.program_id(1)
        if plan is not None:
            @pl.when(jnp.logical_and(pl.program_id(0) == 0, i == 0))
            def _():
                plan.start(x_in, x_out, x_sems)

            @pl.when(jnp.logical_and(pl.program_id(0) == nstep - 1, i == int(RELAY_AT * nq)))
            def _():
                plan.relay(x_in, x_out, x_sems)

        first_head, vis = _attn_masks()
        lane = lax.broadcasted_iota(jnp.int32, (1, LANES), 1)
        cols = [slice(g * LANES, (g + 1) * LANES) for g in range(ngrp)]
        q2s = [_stack_heads(q_ref[:, cs], first_head) for cs in cols]
        tri_m = tri_ref[...]

        c_s[...] = jnp.zeros_like(c_s)
        acc_s[...] = jnp.zeros_like(acc_s)
        cm_s[...] = jnp.zeros_like(cm_s)

        def blocks(js, masks):
            offs = [pl.multiple_of(j * bq, bq) for j in js]
            work = [(g, b) for b in range(len(js)) for g in range(ngrp)]
            zs = {(g, b): _nt(q2s[g], k_ref[pl.ds(offs[b], bq), cols[g]]) for g, b in work}
            lss = {}
            for g, b in work:
                z = zs[g, b]
                ls = -(jnp.maximum(z, 0.0) + jnp.log(1.0 + jnp.exp(-jnp.abs(z))))
                lss[g, b] = ls if masks[b] is None else jnp.where(masks[b], ls, 0.0)
            tails = {gb: _block_sums(lss[gb], tri_m) for gb in work}
            probs = {}
            carry = [c_s[g] for g in range(ngrp)]
            saved = [cm_s[g] for g in range(ngrp)]
            for g, b in work:
                a = jnp.exp(zs[g, b] + lss[g, b] + tails[g, b] + carry[g])
                probs[g, b] = (a if masks[b] is None else jnp.where(masks[b], a, 0.0)).astype(BF16)
                saved[g] = jnp.where(lane == js[b], carry[g], saved[g])
                carry[g] = carry[g] + tails[g, b][:, 0:1] + lss[g, b][:, 0:1]
            top = None
            for g in range(ngrp):
                c_s[g] = carry[g]
                cm_s[g] = saved[g]
                acc = acc_s[g]
                for b in range(len(js)):
                    acc = acc + _nn(probs[g, b], v_ref[pl.ds(offs[b], bq), cols[g]])
                acc_s[g] = acc
                top = carry[g] if top is None else jnp.maximum(top, carry[g])
            return jnp.max(top) > EXP_ZERO

        alive = lax.cond(i > 0, lambda: blocks([i, i - 1], [vis, None]), lambda: blocks([i], [vis]))
        rest = jnp.maximum(i - 1, 0)

        def pair(carry):
            t, _ = carry
            j = i - 2 - 2 * t
            return t + 1, blocks([j, j - 1], [None, None])

        trips, alive = lax.while_loop(lambda ca: jnp.logical_and(ca[0] < rest // 2, ca[1]), pair, (0, alive))
        last = jnp.logical_and(jnp.logical_and(rest % 2 == 1, trips == rest // 2), alive)

        @pl.when(last)
        def _():
            blocks([0], [None])

        n_done = (jnp.minimum(i + 1, 2) + 2 * trips + last.astype(jnp.int32)).astype(F32)
        for g in range(ngrp):
            cmat = jnp.where(lane == COUNT_LANE, n_done, cm_s[g])
            carry_ref[:, 2 * g * LANES:(2 * g + 1) * LANES] = cmat[:bq]
            carry_ref[:, (2 * g + 1) * LANES:(2 * g + 2) * LANES] = cmat[bq:]
            o_ref[:, cols[g]] = _unstack_heads(acc_s[g], first_head)
        if plan is not None:
            @pl.when(jnp.logical_and(pl.program_id(0) == nstep - 1, i == nq - 1))
            def _():
                plan.finish(x_in, x_out, x_sems)

    width = ngrp * LANES
    outs = _pcall(
        body, name="attn_fwd" if plan is None else "attn_fwd_gather", grid=(nstep, nq),
        in_specs=[pl.BlockSpec((bq, width), lambda p, i: (i, p)),
                  pl.BlockSpec((lp, width), lambda p, i: (0, nstep + p)),
                  pl.BlockSpec((lp, width), lambda p, i: (0, 2 * nstep + p)),
                  pl.BlockSpec((bq, bq), lambda p, i: (0, 0))] + x_in_specs,
        out_specs=[pl.BlockSpec((bq, width), lambda p, i: (i, p)),
                   pl.BlockSpec((bq, 2 * width), lambda p, i: (i, p))] + x_out_specs,
        out_shape=[jax.ShapeDtypeStruct((lp, D_SB), F32), jax.ShapeDtypeStruct((lp, 2 * D_SB), F32)] + x_out_shapes,
        scratch_shapes=[pltpu.VMEM((ngrp, 2 * bq, 1), F32), pltpu.VMEM((ngrp, 2 * bq, LANES), F32),
                        pltpu.VMEM((ngrp, 2 * bq, LANES), F32)] + x_scratch,
        compiler_params=_params("arbitrary", "arbitrary"),
    )(qkv, qkv, qkv, tri, *x_args)
    return outs[0], outs[1], outs[2:]


def _outproj(c5, att, ew, h, w, g):
    lp, d = h.shape
    tm = _mm_tile(lp)

    def body(c5_ref, att_ref, sg_ref, h_ref, w_ref, g_ref, hn_ref, cat_ref, mix_ref):
        sg = sg_ref[...].astype(F32)
        s = att_ref[...] * (sg * _sigmoid(sg))
        cat_ref[:, 0:D_CONV] = c5_ref[...]
        cat_ref[:, D_CONV:] = s.astype(BF16)
        mixed = _nn(cat_ref[...], w_ref[...])
        mix_ref[...] = mixed
        rstd = lax.rsqrt(jnp.mean(mixed * mixed, axis=-1, keepdims=True) + RMS_EPS)
        hn_ref[...] = h_ref[...] + (mixed * rstd) * g_ref[...]

    half = pl.BlockSpec((tm, 512), lambda i: (i, 0))
    full = pl.BlockSpec((tm, d), lambda i: (i, 0))
    return _pcall(
        body, name="outproj_fwd", grid=(lp // tm,),
        in_specs=[half, half, pl.BlockSpec((tm, 512), lambda i: (i, 3)), full,
                  pl.BlockSpec((d, d), lambda i: (0, 0)), pl.BlockSpec((1, d), lambda i: (0, 0))],
        out_specs=[full, full, full],
        out_shape=[jax.ShapeDtypeStruct((lp, d), F32), jax.ShapeDtypeStruct((lp, d), BF16),
                   jax.ShapeDtypeStruct((lp, d), F32)],
        compiler_params=_params("parallel"),
    )(c5, att, ew, h, w, g)


def _loss_head(h, target, seq):
    lp, d = h.shape
    tm = ROW_TILE

    def body(h_ref, t_ref, dh_ref, loss_ref):
        i = pl.program_id(0)

        @pl.when(i == 0)
        def _():
            loss_ref[...] = jnp.zeros_like(loss_ref)

        row = i * tm + lax.broadcasted_iota(jnp.int32, (tm, 1), 0)
        real = jnp.logical_and(row >= N_META, row < N_META + seq)
        diff = jnp.where(real, h_ref[...] - t_ref[...], 0.0)
        dh_ref[...] = diff * (1.0 / d)
        loss_ref[...] += 0.5 * jnp.sum(jnp.sum(diff * diff, axis=-1, keepdims=True) * (1.0 / d))

    full = pl.BlockSpec((tm, d), lambda i: (i, 0))
    return _pcall(
        body, name="loss_head", grid=(lp // tm,),
        in_specs=[full, full],
        out_specs=[full, pl.BlockSpec((8, LANES), lambda i: (0, 0))],
        out_shape=[jax.ShapeDtypeStruct((lp, d), F32), jax.ShapeDtypeStruct((8, LANES), F32)],
        compiler_params=_params("arbitrary"),
    )(h, target)


def _outproj_bwd(dh, mixed, g, w, att, ew, c4):
    lp, d = dh.shape
    tm = _mm_tile(lp)

    def body(dh_ref, mix_ref, g_ref, w_ref, att_ref, cg_ref, sg_ref, c4_ref,
             dmix_ref, datt_ref, dsg_ref, dc4_ref, dcg_ref, dg_ref, db_ref):
        @pl.when(pl.program_id(0) == 0)
        def _():
            dg_ref[...] = jnp.zeros_like(dg_ref)
            db_ref[...] = jnp.zeros_like(db_ref)

        mixed = mix_ref[...]
        dhv = dh_ref[...]
        rstd = lax.rsqrt(jnp.mean(mixed * mixed, axis=-1, keepdims=True) + RMS_EPS)
        n = mixed * rstd
        dg_ref[...] += jnp.sum(dhv * n, axis=0, keepdims=True)
        dn = dhv * g_ref[...]
        dmix = (rstd * (dn - n * jnp.mean(dn * n, axis=-1, keepdims=True))).astype(BF16)
        dmix_ref[...] = dmix
        dcat = _nt(dmix, w_ref[...])
        dc5 = dcat[:, 0:D_CONV]
        ds = dcat[:, D_CONV:]
        silu_sg, dsilu_sg = _silu_fwd_bwd(sg_ref[...].astype(F32))
        datt_ref[...] = (ds * silu_sg).astype(BF16)
        dsg_ref[...] = (ds * att_ref[...] * dsilu_sg).astype(BF16)
        silu_cg, dsilu_cg = _silu_fwd_bwd(cg_ref[...].astype(F32))
        dc4 = dc5 * silu_cg
        db_ref[...] += jnp.sum(dc4, axis=0, keepdims=True)
        dc4_ref[...] = dc4.astype(BF16)
        dcg_ref[...] = (dc5 * c4_ref[...] * dsilu_cg).astype(BF16)

    half = pl.BlockSpec((tm, 512), lambda i: (i, 0))
    full = pl.BlockSpec((tm, d), lambda i: (i, 0))
    hb = jax.ShapeDtypeStruct((lp, 512), BF16)
    return _pcall(
        body, name="outproj_bwd", grid=(lp // tm,),
        in_specs=[full, full, pl.BlockSpec((1, d), lambda i: (0, 0)), pl.BlockSpec((d, d), lambda i: (0, 0)),
                  half, pl.BlockSpec((tm, 512), lambda i: (i, 2)), pl.BlockSpec((tm, 512), lambda i: (i, 3)), half],
        out_specs=[full, half, half, half, half,
                   pl.BlockSpec((1, d), lambda i: (0, 0)), pl.BlockSpec((1, 512), lambda i: (0, 0))],
        out_shape=[jax.ShapeDtypeStruct((lp, d), BF16), hb, hb, hb, hb,
                   jax.ShapeDtypeStruct((1, d), F32), jax.ShapeDtypeStruct((1, 512), F32)],
        compiler_params=_params("arbitrary"),
    )(dh, mixed, g, w, att, ew, ew, c4)


def _attn_bwd(qkv, carries, datt, tri, upper, plan=None):
    lp = qkv.shape[0]
    bq = ATT_BLOCK
    ngrp = ATT_PAIRS
    nstep = D_SB // (LANES * ngrp)
    nq = lp // bq
    split, x_args, x_in_specs, x_out_specs, x_out_shapes, x_scratch = _hosted(plan, 3, 2)

    def body(q_ref, k_ref, v_ref, carry_ref, do_ref, tri_ref, upper_ref, *rest):
        x_in, (dq_ref, dk_ref, dv_ref), x_out, (run_s, dq_s), x_sems = split(rest)
        i = pl.program_id(1)
        if plan is not None:
            @pl.when(jnp.logical_and(pl.program_id(0) == 0, i == 0))
            def _():
                plan.start(x_in, x_out, x_sems)

            @pl.when(jnp.logical_and(pl.program_id(0) == nstep - 1, i == max(nq - 2, 0)))
            def _():
                plan.relay(x_in, x_out, x_sems)

        @pl.when(i == 0)
        def _():
            dk_ref[...] = jnp.zeros_like(dk_ref)
            dv_ref[...] = jnp.zeros_like(dv_ref)

        first_head, vis = _attn_masks()
        lane = lax.broadcasted_iota(jnp.int32, (1, LANES), 1)
        cols = [slice(g * LANES, (g + 1) * LANES) for g in range(ngrp)]
        q2s = [_stack_heads(q_ref[:, cs], first_head) for cs in cols]
        do2s = [_stack_heads(do_ref[:, cs], first_head) for cs in cols]
        cmats = [jnp.concatenate([carry_ref[:, 2 * g * LANES:(2 * g + 1) * LANES],
                                  carry_ref[:, (2 * g + 1) * LANES:(2 * g + 2) * LANES]], axis=0)
                 for g in range(ngrp)]
        tri_m = tri_ref[...]
        upper_m = upper_ref[...]

        def blocks(js, masks):
            offs = [pl.multiple_of(j * bq, bq) for j in js]
            work = [(g, b) for b in range(len(js)) for g in range(ngrp)]
            zs = {(g, b): _nt(q2s[g], k_ref[pl.ds(offs[b], bq), cols[g]]) for g, b in work}
            lss = {}
            for g, b in work:
                z = zs[g, b]
                ls = -(jnp.maximum(z, 0.0) + jnp.log(1.0 + jnp.exp(-jnp.abs(z))))
                lss[g, b] = ls if masks[b] is None else jnp.where(masks[b], ls, 0.0)
            tails = {gb: _block_sums(lss[gb], tri_m) for gb in work}
            das = {(g, b): _nt(do2s[g], v_ref[pl.ds(offs[b], bq), cols[g]]) for g, b in work}
            probs, des = {}, {}
            for g, b in work:
                c = jnp.sum(jnp.where(lane == js[b], cmats[g], 0.0), axis=-1, keepdims=True)
                a = jnp.exp(zs[g, b] + lss[g, b] + tails[g, b] + c)
                a = a if masks[b] is None else jnp.where(masks[b], a, 0.0)
                probs[g, b] = a.astype(BF16)
                des[g, b] = das[g, b] * a
            prefixes = {gb: _block_sums(des[gb], upper_m) for gb in work}
            runs = [run_s[g] for g in range(ngrp)]
            dzs = {}
            for g, b in work:
                beta = jnp.exp(zs[g, b] + lss[g, b])
                dz = des[g, b] - beta * (des[g, b] + runs[g] + prefixes[g, b])
                dzs[g, b] = (dz if masks[b] is None else jnp.where(masks[b], dz, 0.0)).astype(BF16)
                runs[g] = runs[g] + prefixes[g, b][:, bq - 1:bq] + des[g, b][:, bq - 1:bq]
            for g in range(ngrp):
                run_s[g] = runs[g]
                dq = dq_s[g]
                for b in range(len(js)):
                    rows = pl.ds(offs[b], bq)
                    dq = dq + _nn(dzs[g, b], k_ref[rows, cols[g]])
                    dk_ref[rows, cols[g]] += _tn(dzs[g, b], q2s[g])
                    dv_ref[rows, cols[g]] += _tn(probs[g, b], do2s[g])
                dq_s[g] = dq

        n_done = jnp.max(carry_ref[:, COUNT_LANE:COUNT_LANE + 1]).astype(jnp.int32)
        n_done = jnp.clip(n_done, 1, i + 1)
        before = jnp.maximum(n_done - 2, 0)
        j0 = i - n_done + 1
        odd = before % 2
        run_s[...] = jnp.zeros_like(run_s)
        dq_s[...] = jnp.zeros_like(dq_s)

        @pl.when(odd == 1)
        def _():
            blocks([j0], [None])

        @pl.loop(0, before // 2)
        def _(t):
            blocks([j0 + odd + 2 * t, j0 + odd + 2 * t + 1], [None, None])

        @pl.when(n_done > 1)
        def _():
            blocks([i - 1, i], [None, vis])

        @pl.when(n_done <= 1)
        def _():
            blocks([i], [vis])

        for g in range(ngrp):
            dq_ref[:, cols[g]] = (_unstack_heads(dq_s[g], first_head) * Q_SCALE).astype(BF16)
        if plan is not None:
            @pl.when(jnp.logical_and(pl.program_id(0) == nstep - 1, i == nq - 1))
            def _():
                plan.finish(x_in, x_out, x_sems)

    width = ngrp * LANES
    once = pl.Buffered(1)
    qb = pl.BlockSpec((bq, width), lambda p, i: (i, p))
    colb = pl.BlockSpec((lp, width), lambda p, i: (0, p), pipeline_mode=once)
    sq = pl.BlockSpec((bq, bq), lambda p, i: (0, 0))
    outs = _pcall(
        body, name="attn_bwd" if plan is None else "attn_bwd_reduce", grid=(nstep, nq),
        in_specs=[qb,
                  pl.BlockSpec((lp, width), lambda p, i: (0, nstep + p), pipeline_mode=once),
                  pl.BlockSpec((lp, width), lambda p, i: (0, 2 * nstep + p), pipeline_mode=once),
                  pl.BlockSpec((bq, 2 * width), lambda p, i: (i, p)), qb, sq, sq] + x_in_specs,
        out_specs=[qb, colb, colb] + x_out_specs,
        out_shape=[jax.ShapeDtypeStruct((lp, D_SB), BF16), jax.ShapeDtypeStruct((lp, D_SB), F32),
                   jax.ShapeDtypeStruct((lp, D_SB), F32)] + x_out_shapes,
        scratch_shapes=[pltpu.VMEM((ngrp, 2 * bq, 1), F32), pltpu.VMEM((ngrp, 2 * bq, LANES), F32)] + x_scratch,
        compiler_params=_params("arbitrary", "arbitrary"),
    )(qkv, qkv, qkv, carries, datt, tri, upper, *x_args)
    return outs[0], outs[1], outs[2], outs[3:]


def _conv_bwd(dc4, c1, ew, cw, lng, lnb, wpw2):
    lp = ew.shape[0]
    tm = ROW_TILE
    nt = lp // tm
    halo_per_tile = tm // CONV_PAD

    def body(dc4_ref, c1_ref, ew_ref, halo_ref, cw_ref, lng_ref, lnb_ref, w_ref,
             dga_ref, dgb_ref, c3_ref, dcw_ref, dcb_ref, dlng_ref, dlnb_ref, xbuf, dbuf, shifted, wacc):
        step = pl.program_id(0)

        @pl.when(step == 0)
        def _():
            wacc[...] = jnp.zeros_like(wacc)
            dcb_ref[...] = jnp.zeros_like(dcb_ref)
            dlng_ref[...] = jnp.zeros_like(dlng_ref)
            dlnb_ref[...] = jnp.zeros_like(dlnb_ref)
            dbuf[tm:tm + CONV_PAD, :] = jnp.zeros((CONV_PAD, D_CONV), F32)

        dc3 = _nt(dc4_ref[...], w_ref[...])
        xhat, rstd = _layer_norm_stats(c1_ref[...])
        c2 = xhat * lng_ref[...] + lnb_ref[...]
        c3, dsilu = _silu_fwd_bwd(c2)
        c3_ref[...] = c3.astype(BF16)
        dc2 = dc3 * dsilu
        dlng_ref[...] += jnp.sum(dc2 * xhat, axis=0, keepdims=True)
        dlnb_ref[...] += jnp.sum(dc2, axis=0, keepdims=True)
        dxhat = dc2 * lng_ref[...]
        dc1 = rstd * (dxhat - jnp.mean(dxhat, axis=-1, keepdims=True)
                      - xhat * jnp.mean(dxhat * xhat, axis=-1, keepdims=True))
        dcb_ref[...] += jnp.sum(dc1, axis=0, keepdims=True)
        dbuf[0:tm, :] = dc1

        ga = ew_ref[:, 0:512].astype(F32)
        sgb = _sigmoid(ew_ref[:, 512:1024].astype(F32))
        xbuf[CONV_PAD:CONV_PAD + tm, :] = ga * sgb
        first_tile = step == nt - 1
        halo = halo_ref[:, 0:512].astype(F32) * _sigmoid(halo_ref[:, 512:1024].astype(F32))
        xbuf[0:CONV_PAD, :] = jnp.where(first_tile, 0.0, halo)

        for cb in range(D_CONV // LANES):
            cs = slice(cb * LANES, (cb + 1) * LANES)
            _shifted_copies(dbuf, cs, shifted, tm)
            for r0 in range(0, tm, CONV_ROWS):
                rs = slice(r0, r0 + CONV_ROWS)
                dc0 = jnp.zeros((CONV_ROWS, LANES), F32)
                for j in range(CONV_WIDTH):
                    dc0 = dc0 + _weighted(cw_ref, j, cs, _shifted_rows(
                        shifted, r0 + CONV_WIDTH - 1 - j, CONV_ROWS))
                dga_ref[rs, cs] = (dc0 * sgb[rs, cs]).astype(BF16)
                dgb_ref[rs, cs] = (dc0 * ga[rs, cs] * sgb[rs, cs] * (1.0 - sgb[rs, cs])).astype(BF16)
            _shifted_copies(xbuf, cs, shifted, tm)
            for r0 in range(0, tm, CONV_ROWS):
                d1 = dbuf[r0:r0 + CONV_ROWS, cs]

                for j in range(CONV_WIDTH):
                    prod = d1 * _shifted_rows(shifted, r0 + CONV_PAD - (CONV_WIDTH - 1) + j, CONV_ROWS)
                    wacc[j * 8:(j + 1) * 8, cs] += jnp.sum(prod.reshape(CONV_ROWS // 8, 8, LANES), axis=0)
        dbuf[tm:tm + CONV_PAD, :] = dbuf[0:CONV_PAD, :]

        @pl.when(step == nt - 1)
        def _():
            dcw_ref[...] = jnp.sum(wacc[...].reshape(CONV_PAD, 8, D_CONV), axis=1)

    rev = lambda i: (nt - 1 - i, 0)
    row = pl.BlockSpec((tm, D_CONV), rev)
    vec = pl.BlockSpec((1, D_CONV), lambda i: (0, 0))
    hb = jax.ShapeDtypeStruct((lp, D_CONV), BF16)
    vs = jax.ShapeDtypeStruct((1, D_CONV), F32)
    return _pcall(
        body, name="conv_bwd", grid=(nt,),
        in_specs=[row, row, pl.BlockSpec((tm, 1024), rev),
                  pl.BlockSpec((CONV_PAD, 1024), lambda i: (jnp.maximum((nt - 1 - i) * halo_per_tile - 1, 0), 0)),
                  pl.BlockSpec((8 * CONV_PAD, D_CONV), lambda i: (0, 0)), vec, vec,
                  pl.BlockSpec((D_CONV, D_CONV), lambda i: (0, 0))],
        out_specs=[row, row, row, pl.BlockSpec((CONV_PAD, D_CONV), lambda i: (0, 0)), vec, vec, vec],
        out_shape=[hb, hb, hb, jax.ShapeDtypeStruct((CONV_PAD, D_CONV), F32), vs, vs, vs],
        scratch_shapes=[pltpu.VMEM((tm + CONV_PAD, D_CONV), F32), pltpu.VMEM((tm + CONV_PAD, D_CONV), F32),
                        pltpu.VMEM((8, tm + CONV_PAD, LANES), F32), pltpu.VMEM((8 * CONV_PAD, D_CONV), F32)],
        compiler_params=_params("arbitrary"),
    )(dc4, c1, ew, ew, cw, lng, lnb, wpw2)


def _inproj_bwd(dga, dgb, dcg, dq, dk, dv, dsg, h, g, w, dh_out):
    lp, d = h.shape
    n = w.shape[1]
    tm = _mm_tile(lp)

    def body(dga_ref, dgb_ref, dcg_ref, dq_ref, dk_ref, dv_ref, dsg_ref, h_ref, g_ref, w_ref, dho_ref,
             dh_ref, dproj_ref, u_ref, dg_ref):
        @pl.when(pl.program_id(0) == 0)
        def _():
            dg_ref[...] = jnp.zeros_like(dg_ref)

        dproj_ref[:, 0:512] = dga_ref[...]
        dproj_ref[:, 512:1024] = dgb_ref[...]
        dproj_ref[:, 1024:1536] = dcg_ref[...]
        dproj_ref[:, 1536:2048] = dq_ref[...]
        dproj_ref[:, 2048:2560] = dk_ref[...].astype(BF16)
        dproj_ref[:, 2560:3072] = dv_ref[...].astype(BF16)
        dproj_ref[:, 3072:3584] = dsg_ref[...]
        du = _nt(dproj_ref[...], w_ref[...])
        x = h_ref[...]
        rstd = lax.rsqrt(jnp.mean(x * x, axis=-1, keepdims=True) + RMS_EPS)
        nrm = x * rstd
        u_ref[...] = (nrm * g_ref[...]).astype(BF16)
        dg_ref[...] += jnp.sum(du * nrm, axis=0, keepdims=True)
        dn = du * g_ref[...]
        dh_ref[...] = dho_ref[...] + rstd * (dn - nrm * jnp.mean(dn * nrm, axis=-1, keepdims=True))

    half = pl.BlockSpec((tm, 512), lambda i: (i, 0))
    full = pl.BlockSpec((tm, d), lambda i: (i, 0))
    return _pcall(
        body, name="inproj_bwd", grid=(lp // tm,),
        in_specs=[half] * 7 + [full, pl.BlockSpec((1, d), lambda i: (0, 0)),
                               pl.BlockSpec((d, n), lambda i: (0, 0)), full],
        out_specs=[full, pl.BlockSpec((tm, n), lambda i: (i, 0)), full, pl.BlockSpec((1, d), lambda i: (0, 0))],
        out_shape=[jax.ShapeDtypeStruct((lp, d), F32), jax.ShapeDtypeStruct((lp, n), BF16),
                   jax.ShapeDtypeStruct((lp, d), BF16), jax.ShapeDtypeStruct((1, d), F32)],
        compiler_params=_params("arbitrary"),
    )(dga, dgb, dcg, dq, dk, dv, dsg, h, g, w, dh_out)


def _row_split(m, parts):
    tm = m // parts
    assert tm * parts == m and tm % 16 == 0, (m, parts)
    return tm


def _matmul_tn(x, dy, tn, name):
    m, k = x.shape
    n = dy.shape[1]
    steps = 4 if m % 64 == 0 else 1
    tm = _row_split(m, steps)

    def body(x_ref, dy_ref, o_ref, acc_ref):
        r = pl.program_id(1)

        @pl.when(r == 0)
        def _():
            acc_ref[...] = jnp.zeros_like(acc_ref)

        acc_ref[...] += _tn(x_ref[...], dy_ref[...])

        @pl.when(r == steps - 1)
        def _():
            o_ref[...] = acc_ref[...].astype(BF16)

    return _pcall(
        body, name=name, grid=(n // tn, steps),
        in_specs=[pl.BlockSpec((tm, k), lambda j, r: (r, 0)), pl.BlockSpec((tm, tn), lambda j, r: (r, j))],
        out_specs=pl.BlockSpec((k, tn), lambda j, r: (0, j)),
        out_shape=jax.ShapeDtypeStruct((k, n), BF16),
        scratch_shapes=[pltpu.VMEM((k, tn), F32)],
        compiler_params=_params("parallel", "arbitrary"),
    )(x, dy)


def _local_step(h0, target_p, seq, vecs, depth, all_weights=None, w_in0=None, gather_w_in=None, gather_rest=None,
                reduce_layer=None):
    pre_g, post_g, conv_b, ln_g, ln_b, b_pw2 = vecs
    ar = jnp.arange(ATT_BLOCK)
    tri = (ar[:, None] > ar[None, :]).astype(BF16)
    upper = (ar[:, None] < ar[None, :]).astype(BF16)
    row = lambda a, l: a[l][None, :]

    hosted = all_weights is None
    weights = [None] * depth if hosted else list(all_weights)
    next_w_in, next_rest = w_in0, None
    saved = []
    h = h0
    for l in range(depth):
        more = hosted and l + 1 < depth
        if hosted:
            ew, qkv, rest = _inproj(h, row(pre_g, l), next_w_in, gather_rest(0) if l == 0 else None)
            w_in, (w_pw2, w_out, conv_w) = next_w_in, (rest if l == 0 else next_rest)
        else:
            w_in, w_pw2, w_out, conv_w = weights[l]
            ew, qkv, _ = _inproj(h, row(pre_g, l), w_in)
        conv_w = jnp.repeat(conv_w, 8, axis=0)
        weights[l] = (w_in, w_pw2, w_out, conv_w)
        c1, c4, c5, next_rest = _conv_fwd(ew, conv_w, row(conv_b, l), row(ln_g, l), row(ln_b, l), w_pw2,
                                          row(b_pw2, l), gather_rest(l + 1) if more else None)
        att, carries, gathered = _attn_fwd(qkv, tri, gather_w_in(l + 1) if more else None)
        if more:
            next_w_in = gathered[0]
        hn, cat, mixed = _outproj(c5, att, ew, h, w_out, row(post_g, l))
        saved.append((h, ew, qkv, c1, c4, att, carries, cat, mixed))
        h = hn

    dh, loss = _loss_head(h, target_p, seq)

    vec_grads = [None] * depth
    mat_grads = [None] * depth
    pending = None
    for l in reversed(range(depth)):
        w_in, w_pw2, w_out, conv_w = weights[l]
        h_in, ew, qkv, c1, c4, att, carries, cat, mixed = saved[l]
        dmix, datt, dsg, dc4, dcg, dpost, dbpw2 = _outproj_bwd(dh, mixed, row(post_g, l), w_out, att, ew, c4)
        dw_out = _matmul_tn(cat, dmix, 512, "dw_out")
        dq, dk, dv, landed = _attn_bwd(qkv, carries, datt, tri, upper, pending)
        if pending is not None:
            mat_grads[l + 1] = landed
        dga, dgb, c3, dcw, dcb, dlng, dlnb = _conv_bwd(dc4, c1, ew, conv_w, row(ln_g, l), row(ln_b, l), w_pw2)
        dw_pw2 = _matmul_tn(c3, dc4, 512, "dw_pw2")
        dh, dproj, u, dpre = _inproj_bwd(dga, dgb, dcg, dq, dk, dv, dsg, h_in, row(pre_g, l), w_in, dh)
        dw_in = _matmul_tn(u, dproj, 1792, "dw_in")
        vec_grads[l] = (dpre[0], dpost[0], dcb[0], dlng[0], dlnb[0], dbpw2[0])
        mats = (dw_in, dw_pw2, dw_out, dcw)
        if reduce_layer is None:
            mat_grads[l] = mats
        else:
            pending = reduce_layer(mats)
    if pending is not None:
        mat_grads[0] = _run_exchange(pending, "reduce_grads")

    vec_grads = [jnp.stack([g[k] for g in vec_grads]) for k in range(len(vecs))]
    return loss[0, 0], dh, vec_grads, mat_grads


N_CHIPS = 4
ANY = pl.BlockSpec(memory_space=pl.ANY)


def _chip_peers():
    x, y, c = lax.axis_index("x"), lax.axis_index("y"), lax.axis_index("c")
    return x, y, c, [(x, 1 - y), (1 - x, y), (1 - x, 1 - y)]


def _shard_slices(refs, dims, idx):
    out = []
    for ref, (axis, size) in zip(refs, dims):
        assert size % LANES == 0
        start = pl.multiple_of(idx * size, LANES)
        sl = [slice(None)] * len(ref.shape)
        sl[axis] = pl.ds(start, size)
        out.append(ref.at[tuple(sl)])
    return out


class _Exchange(NamedTuple):
    inputs: list
    out_shapes: list
    scratch: list
    start: Callable
    relay: Callable
    finish: Callable


def _run_exchange(plan, name):
    n_in, n_out = len(plan.inputs), len(plan.out_shapes)

    def body(*refs):
        parts = refs[:n_in], refs[n_in:n_in + n_out], refs[n_in + n_out:]
        plan.start(*parts)
        plan.relay(*parts)
        plan.finish(*parts)

    return _pcall(body, name=name, in_specs=[ANY] * n_in, out_specs=[ANY] * n_out, out_shape=plan.out_shapes,
                  scratch_shapes=plan.scratch)(*plan.inputs)


def _gather_plan(shards, dims):
    n = len(shards)
    full_shapes = []
    halves = []
    for s, (axis, size) in zip(shards, dims):
        shp = list(s.shape)
        shp[axis] = size * N_CHIPS
        full_shapes.append(jax.ShapeDtypeStruct(tuple(shp), s.dtype))
        tile_rows = 32 // s.dtype.itemsize
        assert s.shape[0] % (2 * tile_rows) == 0
        halves.append((s.shape[0] // 2, tile_rows))

    def half(refs, which):
        return [r.at[pl.ds(pl.multiple_of(which * h, t), h)] for r, (h, t) in zip(refs, halves)]

    def copies(srcs, outs, sems):
        send, recv, loc = sems
        x, y, c, peers = _chip_peers()
        sibling = (x, y, 1 - c)
        mine = _shard_slices(outs, dims, 2 * x + y)
        local = [pltpu.make_async_copy(s, d, loc.at[a]) for a, (s, d) in enumerate(zip(srcs, mine))]

        def remote(src, dst, slot, a, dev):
            return pltpu.make_async_remote_copy(src, dst, send.at[slot, a], recv.at[slot, a],
                                                device_id=dev, device_id_type=MESH)

        sends = [remote(s, d, k, a, (px, py, c))
                 for k, (px, py) in enumerate(peers) for a, (s, d) in enumerate(zip(half(srcs, c), half(mine, c)))]
        theirs = [_shard_slices(outs, dims, 2 * px + py) for px, py in peers]
        arrivals = [remote(s, d, k, a, (px, py, c))
                    for k, (px, py) in enumerate(peers)
                    for a, (s, d) in enumerate(zip(half(srcs, c), half(theirs[k], c)))]
        passed_on = [remote(d, d, 3 + k, a, sibling) for k in range(3) for a, d in enumerate(half(theirs[k], c))]
        from_sibling = [remote(d, d, 3 + k, a, sibling)
                        for k in range(3) for a, d in enumerate(half(theirs[k], 1 - c))]
        return local, sends, arrivals, passed_on, from_sibling

    def start(srcs, outs, sems):
        local, sends = copies(srcs, outs, sems)[:2]
        for cp in local + sends:
            cp.start()

    def relay(srcs, outs, sems):
        _, _, arrivals, passed_on, _ = copies(srcs, outs, sems)
        for arrived, onward in zip(arrivals, passed_on):
            arrived.wait_recv()
            onward.start()

    def finish(srcs, outs, sems):
        local, sends, _, passed_on, from_sibling = copies(srcs, outs, sems)
        for cp in from_sibling:
            cp.wait_recv()
        for cp in sends + passed_on:
            cp.wait_send()
        for cp in local:
            cp.wait()

    scratch = [pltpu.SemaphoreType.DMA((6, n)), pltpu.SemaphoreType.DMA((6, n)), pltpu.SemaphoreType.DMA((n,))]
    return _Exchange(list(shards), full_shapes, scratch, start, relay, finish)


def _reduce_plan(grads, dims):
    n = len(grads)
    piece_shapes = []
    for g, (axis, size) in zip(grads, dims):
        shp = list(g.shape)
        shp[axis] = size
        piece_shapes.append(jax.ShapeDtypeStruct((N_CHIPS,) + tuple(shp), g.dtype))

    def copies(srcs, outs, sems):
        mine, theirs = outs[:n], outs[n:]
        send, recv, loc = sems
        x, y, c, peers = _chip_peers()
        sibling = (x, y, 1 - c)
        own = _shard_slices(srcs, dims, 2 * x + y)

        def remote(src, dst, slot, a, dev):
            return pltpu.make_async_remote_copy(src, dst, send.at[slot, a], recv.at[slot, a],
                                                device_id=dev, device_id_type=MESH)

        local = [pltpu.make_async_copy(own[a], mine[a].at[3], loc.at[a]) for a in range(n)]
        to_sibling = [remote(own[a], theirs[a].at[3], 3, a, sibling) for a in range(n)]
        to_chips = [remote(src, mine[a].at[k], k, a, (px, py, c))
                    for k, (px, py) in enumerate(peers)
                    for a, src in enumerate(_shard_slices(srcs, dims, 2 * px + py))]
        passed_on = [remote(mine[a].at[k], theirs[a].at[k], 4 + k, a, sibling) for k in range(3) for a in range(n)]
        return local, to_sibling, to_chips, passed_on

    def start(srcs, outs, sems):
        local, to_sibling, to_chips, _ = copies(srcs, outs, sems)
        for cp in local + to_sibling + to_chips:
            cp.start()

    def relay(srcs, outs, sems):
        _, _, to_chips, passed_on = copies(srcs, outs, sems)
        for arrived, onward in zip(to_chips, passed_on):
            arrived.wait_recv()
            onward.start()

    def finish(srcs, outs, sems):
        local, to_sibling, to_chips, passed_on = copies(srcs, outs, sems)
        for cp in to_sibling + passed_on:
            cp.wait_recv()
        for cp in to_sibling + to_chips + passed_on:
            cp.wait_send()
        for cp in local:
            cp.wait()

    scratch = [pltpu.SemaphoreType.DMA((7, n)), pltpu.SemaphoreType.DMA((7, n)), pltpu.SemaphoreType.DMA((n,))]
    return _Exchange(list(grads), piece_shapes * 2, scratch, start, relay, finish)


def _allsum_small(pack):
    rows, cols = pack.shape
    ndev = 8

    def body(p_ref, o_ref, buf, send, recv):
        x, y, c = lax.axis_index("x"), lax.axis_index("y"), lax.axis_index("c")
        me = 4 * x + 2 * y + c
        buf[me] = p_ref[...]
        started = []
        for r in range(1, ndev):
            bx, by, bc = (r >> 2) & 1, (r >> 1) & 1, r & 1
            dev = (x ^ bx, y ^ by, c ^ bc)
            cp = pltpu.make_async_remote_copy(p_ref, buf.at[me], send.at[r], recv.at[r],
                                              device_id=dev, device_id_type=MESH)
            cp.start()
            started.append(cp)
        for r in range(1, ndev):
            pltpu.make_async_remote_copy(p_ref, buf.at[me ^ r], send.at[r], recv.at[r],
                                         device_id=(x, y, c), device_id_type=MESH).wait_recv()
        for cp in started:
            cp.wait_send()
        acc = buf[0]
        for d in range(1, ndev):
            acc = acc + buf[d]
        o_ref[...] = acc

    vm = pl.BlockSpec(memory_space=pltpu.VMEM)
    return _pcall(
        body, name="allsum_small", in_specs=[vm], out_specs=vm,
        out_shape=jax.ShapeDtypeStruct((rows, cols), F32),
        scratch_shapes=[pltpu.VMEM((ndev, rows, cols), F32), pltpu.SemaphoreType.DMA((ndev,)),
                        pltpu.SemaphoreType.DMA((ndev,))],
    )(pack)


def _adamw(parts, w, m, v, layer, prev, name):
    _, rows, cols = w.shape
    tr = ROW_TILE if rows % ROW_TILE == 0 else rows
    counts = [p.shape[0] for p in parts]
    n_parts = len(parts)
    n_prev = 0 if prev is None else 4

    def body(*refs):
        part_refs = refs[:n_parts]
        w_ref, m_ref, v_ref = refs[n_parts:n_parts + 3]
        g_ref, d_ref, nm_ref, nv_ref = refs[n_parts + 3 + n_prev:]
        g = None
        for p_ref, cnt in zip(part_refs, counts):
            s = p_ref[0].astype(F32)
            for k in range(1, cnt):
                s = s + p_ref[k].astype(F32)
            g = s if g is None else g + s
        m2 = ADAM_B1 * m_ref[0] + (1.0 - ADAM_B1) * g
        v2 = ADAM_B2 * v_ref[0] + (1.0 - ADAM_B2) * (g * g)
        m_hat = m2 / (1.0 - ADAM_B1 ** ADAM_STEP)
        v_hat = v2 / (1.0 - ADAM_B2 ** ADAM_STEP)
        g_ref[0] = g
        d_ref[0] = -ADAM_LR * (m_hat / (jnp.sqrt(v_hat) + ADAM_EPS) + ADAM_WD * w_ref[0])
        nm_ref[0] = m2
        nv_ref[0] = v2

    blk = pl.BlockSpec((1, tr, cols), lambda i: (layer, i, 0))
    shp = jax.ShapeDtypeStruct(w.shape, F32)
    return _pcall(
        body, name=name, grid=(rows // tr,),
        in_specs=[pl.BlockSpec((cnt, tr, cols), lambda i: (0, i, 0)) for cnt in counts] + [blk] * 3 + [ANY] * n_prev,
        out_specs=[blk] * 4, out_shape=[shp] * 4,
        input_output_aliases={n_parts + 3 + k: k for k in range(n_prev)},
        compiler_params=_params("parallel"),
    )(*parts, w, m, v, *(prev or ()))


def kernel(x, meta_tokens, pre_norm_g, post_norm_g, w_in, conv_w, conv_b, conv_ln_g, conv_ln_b, w_pw2, b_pw2, w_out, loss_target, m_meta_tokens, m_pre_norm_g, m_post_norm_g, m_w_in, m_conv_w, m_conv_b, m_conv_ln_g, m_conv_ln_b, m_w_pw2, m_b_pw2, m_w_out, v_meta_tokens, v_pre_norm_g, v_post_norm_g, v_w_in, v_conv_w, v_conv_b, v_conv_ln_g, v_conv_ln_b, v_w_pw2, v_b_pw2, v_w_out):
    seq, d = x.shape[1], x.shape[2]
    depth = w_in.shape[0]
    length = N_META + seq
    lp = -(-length // ATT_BLOCK) * ATT_BLOCK
    tap_pad = ((0, 0), (0, CONV_PAD - CONV_WIDTH), (0, 0))

    shards = (w_in.astype(BF16), w_pw2.astype(BF16), w_out.astype(BF16), jnp.pad(conv_w, tap_pad))
    dims = [(1, w_in.shape[2]), (0, w_pw2.shape[1]), (0, w_out.shape[1]), (1, conv_w.shape[2])]
    layer_shards = lambda l: [s[l] for s in shards]

    w_in0, meta_f = _run_exchange(_gather_plan([shards[0][0], meta_tokens], [dims[0], (1, meta_tokens.shape[1])]),
                                  "gather_weights")

    h0 = jnp.concatenate([meta_f, x[0], jnp.zeros((lp - length, d), F32)], axis=0)
    target_p = jnp.pad(loss_target[0], ((N_META, lp - length), (0, 0)))
    vecs = (pre_norm_g, post_norm_g, conv_b, conv_ln_g, conv_ln_b, b_pw2)
    loss, dh0, vec_grads, pieces = _local_step(
        h0, target_p, seq, vecs, depth, w_in0=w_in0,
        gather_w_in=lambda l: _gather_plan([shards[0][l]], dims[:1]),
        gather_rest=lambda l: _gather_plan(layer_shards(l)[1:], dims[1:]),
        reduce_layer=lambda grads: _reduce_plan(list(grads), dims))

    def update(k, w, m, v, name):
        outs = None
        for l in reversed(range(depth)):
            outs = _adamw([pieces[l][k], pieces[l][4 + k]], w, m, v, l, outs, name)
        return outs

    up_w_in = update(0, w_in, m_w_in, v_w_in, "adamw_w_in")
    up_w_pw2 = update(1, w_pw2, m_w_pw2, v_w_pw2, "adamw_w_pw2")
    up_w_out = update(2, w_out, m_w_out, v_w_out, "adamw_w_out")
    up_conv_w = [o[:, :CONV_WIDTH] for o in update(3, jnp.pad(conv_w, tap_pad), jnp.pad(m_conv_w, tap_pad),
                                                   jnp.pad(v_conv_w, tap_pad, constant_values=1.0), "adamw_conv_w")]

    two = lambda a: a.reshape(-1, d)
    vec_rows = [two(g) for g in vec_grads]
    n_vec = sum(a.shape[0] for a in vec_rows)
    pack = jnp.concatenate(vec_rows + [dh0[:N_META], jnp.full((8, d), loss, F32)], axis=0)
    pack = jnp.pad(pack, ((0, -pack.shape[0] % 8), (0, 0)))
    tot = _allsum_small(pack)
    loss_all = tot[n_vec + N_META, 0]

    cat = lambda arrs: jnp.concatenate([two(t) for t in arrs], axis=0)[None]
    small_m = (m_pre_norm_g, m_post_norm_g, m_conv_b, m_conv_ln_g, m_conv_ln_b, m_b_pw2)
    small_v = (v_pre_norm_g, v_post_norm_g, v_conv_b, v_conv_ln_g, v_conv_ln_b, v_b_pw2)
    up_small = _adamw([tot[None, :n_vec]], cat(vecs), cat(small_m), cat(small_v), 0, None, "adamw_vectors")

    def unpack(o):
        res, r0 = [], 0
        for t in vecs:
            nrow = t.size // d
            res.append(o[0, r0:r0 + nrow].reshape(t.shape))
            r0 += nrow
        return res

    up_small = [unpack(o) for o in up_small]
    chip = 2 * lax.axis_index("x") + lax.axis_index("y")
    mcols = meta_tokens.shape[1]
    g_meta = lax.dynamic_slice_in_dim(tot[n_vec:n_vec + N_META], chip * mcols, mcols, axis=1)
    up_meta = [o[0] for o in _adamw([g_meta[None]], meta_tokens[None], m_meta_tokens[None], v_meta_tokens[None],
                                    0, None, "adamw_meta")]

    grad_x = dh0[N_META:length][None]
    outs = [loss_all, grad_x]
    for j in range(4):
        pre, post, cb, lg, lb, bp = up_small[j]
        outs += [up_meta[j], pre, post, up_w_in[j], up_conv_w[j], cb, lg, lb, up_w_pw2[j], bp, up_w_out[j]]
    return tuple(outs)
```

```python
from typing import Callable, NamedTuple

import jax
import jax.numpy as jnp
from jax import lax
from jax.experimental import pallas as pl
from jax.experimental.pallas import tpu as pltpu

F32 = jnp.float32
BF16 = jnp.bfloat16

N_META = 16
D_CONV = 512
D_SB = 512
HEAD_DIM = 64
CONV_WIDTH = 31
CONV_PAD = 32
CONV_ROWS = 128
RMS_EPS = 1e-6
LN_EPS = 1e-5
Q_SCALE = HEAD_DIM ** -0.5

ADAM_LR = 0.001
ADAM_B1 = 0.9
ADAM_B2 = 0.999
ADAM_EPS = 1e-08
ADAM_WD = 0.01
ADAM_STEP = 10

LANES = 128
ROW_TILE = 256
MM_TILE_MAX = 544
ATT_BLOCK = 256
ATT_PAIRS = 4
ATT_PAIRS_FWD = 4
VMEM_LIMIT = 56 * 1024 * 1024
EXP_ZERO = -104.0
COUNT_LANE = LANES - 1
RELAY_AT = 0.75

MESH = pl.DeviceIdType.MESH


def _pcall(body, **kw):
    return pl.pallas_call(body, **kw)


def _params(*sem):
    return pltpu.CompilerParams(dimension_semantics=sem, vmem_limit_bytes=VMEM_LIMIT)


def _sigmoid(x):
    return 1.0 / (1.0 + jnp.exp(-x))


def _silu_fwd_bwd(x):
    s = _sigmoid(x)
    return x * s, s * (1.0 + x * (1.0 - s))


def _nt(a, b):
    return lax.dot_general(a, b, (((1,), (1,)), ((), ())), preferred_element_type=F32)


def _tn(a, b):
    return lax.dot_general(a, b, (((0,), (0,)), ((), ())), preferred_element_type=F32)


def _nn(a, b):
    return jnp.dot(a, b, preferred_element_type=F32)


def _mm_tile(rows):
    return max(t for t in range(16, MM_TILE_MAX + 1, 16) if rows % t == 0)


def _host_begin(plan, parts, steps):
    if plan is not None:
        x_in, _, x_out, _, x_sems = parts

        @pl.when(pl.program_id(0) == 0)
        def _():
            plan.start(x_in, x_out, x_sems)

        @pl.when(pl.program_id(0) == int(RELAY_AT * steps))
        def _():
            plan.relay(x_in, x_out, x_sems)


def _host_end(plan, parts, steps):
    if plan is not None:
        x_in, _, x_out, _, x_sems = parts

        @pl.when(pl.program_id(0) == steps - 1)
        def _():
            plan.finish(x_in, x_out, x_sems)


def _inproj(h, g, w, plan=None):
    lp, d = h.shape
    n = w.shape[1]
    tm = _mm_tile(lp)
    steps = lp // tm
    split, x_args, x_in_specs, x_out_specs, x_out_shapes, x_scratch = _hosted(plan, 2, 0)

    def body(h_ref, g_ref, w_ref, *rest):
        parts = split(rest)
        ew_ref, qkv_ref = parts[1]
        _host_begin(plan, parts, steps)
        x = h_ref[...]
        rstd = lax.rsqrt(jnp.mean(x * x, axis=-1, keepdims=True) + RMS_EPS)
        u = ((x * rstd) * g_ref[...]).astype(BF16)
        p = _nn(u, w_ref[...])
        ew_ref[:, 0:1536] = p[:, 0:1536].astype(BF16)
        ew_ref[:, 1536:2048] = p[:, 3072:3584].astype(BF16)
        qkv_ref[:, 0:512] = (p[:, 1536:2048] * Q_SCALE).astype(BF16)
        qkv_ref[:, 512:1536] = p[:, 2048:3072].astype(BF16)
        _host_end(plan, parts, steps)

    outs = _pcall(
        body, name="inproj_fwd" if plan is None else "inproj_fwd_gather", grid=(steps,),
        in_specs=[pl.BlockSpec((tm, d), lambda i: (i, 0)),
                  pl.BlockSpec((1, d), lambda i: (0, 0)),
                  pl.BlockSpec((d, n), lambda i: (0, 0))] + x_in_specs,
        out_specs=[pl.BlockSpec((tm, 2048), lambda i: (i, 0)),
                   pl.BlockSpec((tm, 1536), lambda i: (i, 0))] + x_out_specs,
        out_shape=[jax.ShapeDtypeStruct((lp, 2048), BF16), jax.ShapeDtypeStruct((lp, 1536), BF16)] + x_out_shapes,
        scratch_shapes=x_scratch,
        compiler_params=_params("parallel" if plan is None else "arbitrary"),
    )(h, g, w, *x_args)
    return outs[0], outs[1], outs[2:]


def _layer_norm_stats(c1):
    mu = jnp.mean(c1, axis=-1, keepdims=True)
    xc = c1 - mu
    var = jnp.mean(xc * xc, axis=-1, keepdims=True)
    rstd = lax.rsqrt(var + LN_EPS)
    return xc * rstd, rstd


def _shifted_copies(window, cols, shifted, tm):
    shifted[0] = window[:, cols]
    rows = tm + CONV_PAD - 8
    for b in range(1, 8):
        shifted[b, 0:rows, :] = window[pl.ds(b, rows), cols]


def _shifted_rows(shifted, shift, tm):
    b = shift % 8
    return shifted[b, pl.ds(pl.multiple_of(shift - b, 8), tm), :]


def _weighted(cw8_ref, j, cols, rows):
    w8 = cw8_ref[pl.ds(pl.multiple_of(j * 8, 8), 8), cols]
    r = rows.shape[0]
    return (rows.reshape(r // 8, 8, LANES) * w8[None]).reshape(r, LANES)


def _conv_fwd(ew, cw, cb, lng, lnb, wpw2, bpw2, plan=None):
    lp = ew.shape[0]
    tm = ROW_TILE
    steps = lp // tm
    split, x_args, x_in_specs, x_out_specs, x_out_shapes, x_scratch = _hosted(plan, 3, 2)

    def body(ew_ref, cw_ref, cb_ref, lng_ref, lnb_ref, w_ref, b_ref, *rest):
        parts = split(rest)
        (c1_ref, c4_ref, c5_ref), (xbuf, shifted) = parts[1], parts[3]
        _host_begin(plan, parts, steps)
        @pl.when(pl.program_id(0) == 0)
        def _():
            xbuf[0:CONV_PAD, :] = jnp.zeros((CONV_PAD, D_CONV), F32)

        ga = ew_ref[:, 0:512].astype(F32)
        gb = ew_ref[:, 512:1024].astype(F32)
        cg = ew_ref[:, 1024:1536].astype(F32)
        xbuf[CONV_PAD:CONV_PAD + tm, :] = ga * _sigmoid(gb)
        for blk in range(D_CONV // LANES):
            cs = slice(blk * LANES, (blk + 1) * LANES)
            _shifted_copies(xbuf, cs, shifted, tm)
            for r0 in range(0, tm, CONV_ROWS):
                acc = jnp.zeros((CONV_ROWS, LANES), F32) + cb_ref[:, cs]
                for j in range(CONV_WIDTH):
                    acc = acc + _weighted(cw_ref, j, cs, _shifted_rows(
                        shifted, r0 + CONV_PAD - (CONV_WIDTH - 1) + j, CONV_ROWS))
                c1_ref[r0:r0 + CONV_ROWS, cs] = acc
        xbuf[0:CONV_PAD, :] = xbuf[tm:tm + CONV_PAD, :]
        xhat, _ = _layer_norm_stats(c1_ref[...])
        c2 = xhat * lng_ref[...] + lnb_ref[...]
        c3 = c2 * _sigmoid(c2)
        c4 = _nn(c3.astype(BF16), w_ref[...]) + b_ref[...]
        c4_ref[...] = c4
        c5_ref[...] = (c4 * (cg * _sigmoid(cg))).astype(BF16)
        _host_end(plan, parts, steps)

    vec = pl.BlockSpec((1, D_CONV), lambda i: (0, 0))
    row = pl.BlockSpec((tm, D_CONV), lambda i: (i, 0))
    outs = _pcall(
        body, name="conv_fwd" if plan is None else "conv_fwd_gather", grid=(steps,),
        in_specs=[pl.BlockSpec((tm, 1536), lambda i: (i, 0)),
                  pl.BlockSpec((8 * CONV_PAD, D_CONV), lambda i: (0, 0)),
                  vec, vec, vec,
                  pl.BlockSpec((D_CONV, D_CONV), lambda i: (0, 0)),
                  vec] + x_in_specs,
        out_specs=[row, row, row] + x_out_specs,
        out_shape=[jax.ShapeDtypeStruct((lp, D_CONV), F32), jax.ShapeDtypeStruct((lp, D_CONV), F32),
                   jax.ShapeDtypeStruct((lp, D_CONV), BF16)] + x_out_shapes,
        scratch_shapes=[pltpu.VMEM((tm + CONV_PAD, D_CONV), F32),
                        pltpu.VMEM((8, tm + CONV_PAD, LANES), F32)] + x_scratch,
        compiler_params=_params("arbitrary"),
    )(ew, cw, cb, lng, lnb, wpw2, bpw2, *x_args)
    return outs[0], outs[1], outs[2], outs[3:]


def _block_sums(x, m01):
    return _nn(x.astype(BF16), m01)


def _attn_masks():
    lane = lax.broadcasted_iota(jnp.int32, (1, LANES), 1)
    row = lax.broadcasted_iota(jnp.int32, (2 * ATT_BLOCK, ATT_BLOCK), 0)
    col = lax.broadcasted_iota(jnp.int32, (2 * ATT_BLOCK, ATT_BLOCK), 1)
    return lane < HEAD_DIM, col < (row & (ATT_BLOCK - 1))


def _stack_heads(x, first_head):
    zero = jnp.zeros_like(x)
    return jnp.concatenate([jnp.where(first_head, x, zero), jnp.where(first_head, zero, x)], axis=0)


def _unstack_heads(x2, first_head):
    rows = x2.shape[0] // 2
    return jnp.where(first_head, x2[:rows], x2[rows:])


def _hosted(plan, n_out, n_scratch):
    n_in = 0 if plan is None else len(plan.inputs)
    n_x = 0 if plan is None else len(plan.out_shapes)

    def split(rest):
        a, b, c = n_in + n_out, n_in + n_out + n_x, n_in + n_out + n_x + n_scratch
        return rest[:n_in], rest[n_in:a], rest[a:b], rest[b:c], rest[c:]

    if plan is None:
        return split, [], [], [], [], []
    return split, list(plan.inputs), [ANY] * n_in, [ANY] * n_x, list(plan.out_shapes), list(plan.scratch)


def _attn_fwd(qkv, tri, plan=None):
    lp = qkv.shape[0]
    bq = ATT_BLOCK
    ngrp = ATT_PAIRS_FWD
    nstep = D_SB // (LANES * ngrp)
    nq = lp // bq
    assert nq <= COUNT_LANE
    split, x_args, x_in_specs, x_out_specs, x_out_shapes, x_scratch = _hosted(plan, 2, 3)

    def body(q_ref, k_ref, v_ref, tri_ref, *rest):
        x_in, (o_ref, carry_ref), x_out, (c_s, acc_s, cm_s), x_sems = split(rest)
        i = pl.program_id(1)
        if plan is not None:
            @pl.when(jnp.logical_and(pl.program_id(0) == 0, i == 0))
            def _():
                plan.start(x_in, x_out, x_sems)

            @pl.when(jnp.logical_and(pl.program_id(0) == nstep - 1, i == int(RELAY_AT * nq)))
            def _():
                plan.relay(x_in, x_out, x_sems)

        first_head, vis = _attn_masks()
        lane = lax.broadcasted_iota(jnp.int32, (1, LANES), 1)
        cols = [slice(g * LANES, (g + 1) * LANES) for g in range(ngrp)]
        q2s = [_stack_heads(q_ref[:, cs], first_head) for cs in cols]
        tri_m = tri_ref[...]

        c_s[...] = jnp.zeros_like(c_s)
        acc_s[...] = jnp.zeros_like(acc_s)
        cm_s[...] = jnp.zeros_like(cm_s)

        def blocks(js, masks):
            offs = [pl.multiple_of(j * bq, bq) for j in js]
            work = [(g, b) for b in range(len(js)) for g in range(ngrp)]
            zs = {(g, b): _nt(q2s[g], k_ref[pl.ds(offs[b], bq), cols[g]]) for g, b in work}
            lss = {}
            for g, b in work:
                z = zs[g, b]
                ls = -(jnp.maximum(z, 0.0) + jnp.log(1.0 + jnp.exp(-jnp.abs(z))))
                lss[g, b] = ls if masks[b] is None else jnp.where(masks[b], ls, 0.0)
            tails = {gb: _block_sums(lss[gb], tri_m) for gb in work}
            probs = {}
            carry = [c_s[g] for g in range(ngrp)]
            saved = [cm_s[g] for g in range(ngrp)]
            for g, b in work:
                a = jnp.exp(zs[g, b] + lss[g, b] + tails[g, b] + carry[g])
                probs[g, b] = (a if masks[b] is None else jnp.where(masks[b], a, 0.0)).astype(BF16)
                saved[g] = jnp.where(lane == js[b], carry[g], saved[g])
                carry[g] = carry[g] + tails[g, b][:, 0:1] + lss[g, b][:, 0:1]
            top = None
            for g in range(ngrp):
                c_s[g] = carry[g]
                cm_s[g] = saved[g]
                acc = acc_s[g]
                for b in range(len(js)):
                    acc = acc + _nn(probs[g, b], v_ref[pl.ds(offs[b], bq), cols[g]])
                acc_s[g] = acc
                top = carry[g] if top is None else jnp.maximum(top, carry[g])
            return jnp.max(top) > EXP_ZERO

        alive = lax.cond(i > 0, lambda: blocks([i, i - 1], [vis, None]), lambda: blocks([i], [vis]))
        rest = jnp.maximum(i - 1, 0)

        def pair(carry):
            t, _ = carry
            j = i - 2 - 2 * t
            return t + 1, blocks([j, j - 1], [None, None])

        trips, alive = lax.while_loop(lambda ca: jnp.logical_and(ca[0] < rest // 2, ca[1]), pair, (0, alive))
        last = jnp.logical_and(jnp.logical_and(rest % 2 == 1, trips == rest // 2), alive)

        @pl.when(last)
        def _():
            blocks([0], [None])

        n_done = (jnp.minimum(i + 1, 2) + 2 * trips + last.astype(jnp.int32)).astype(F32)
        for g in range(ngrp):
            cmat = jnp.where(lane == COUNT_LANE, n_done, cm_s[g])
            carry_ref[:, 2 * g * LANES:(2 * g + 1) * LANES] = cmat[:bq]
            carry_ref[:, (2 * g + 1) * LANES:(2 * g + 2) * LANES] = cmat[bq:]
            o_ref[:, cols[g]] = _unstack_heads(acc_s[g], first_head)
        if plan is not None:
            @pl.when(jnp.logical_and(pl.program_id(0) == nstep - 1, i == nq - 1))
            def _():
                plan.finish(x_in, x_out, x_sems)

    width = ngrp * LANES
    outs = _pcall(
        body, name="attn_fwd" if plan is None else "attn_fwd_gather", grid=(nstep, nq),
        in_specs=[pl.BlockSpec((bq, width), lambda p, i: (i, p)),
                  pl.BlockSpec((lp, width), lambda p, i: (0, nstep + p)),
                  pl.BlockSpec((lp, width), lambda p, i: (0, 2 * nstep + p)),
                  pl.BlockSpec((bq, bq), lambda p, i: (0, 0))] + x_in_specs,
        out_specs=[pl.BlockSpec((bq, width), lambda p, i: (i, p)),
                   pl.BlockSpec((bq, 2 * width), lambda p, i: (i, p))] + x_out_specs,
        out_shape=[jax.ShapeDtypeStruct((lp, D_SB), F32), jax.ShapeDtypeStruct((lp, 2 * D_SB), F32)] + x_out_shapes,
        scratch_shapes=[pltpu.VMEM((ngrp, 2 * bq, 1), F32), pltpu.VMEM((ngrp, 2 * bq, LANES), F32),
                        pltpu.VMEM((ngrp, 2 * bq, LANES), F32)] + x_scratch,
        compiler_params=_params("arbitrary", "arbitrary"),
    )(qkv, qkv, qkv, tri, *x_args)
    return outs[0], outs[1], outs[2:]


def _outproj(c5, att, ew, h, w, g):
    lp, d = h.shape
    tm = _mm_tile(lp)

    def body(c5_ref, att_ref, sg_ref, h_ref, w_ref, g_ref, hn_ref, cat_ref, mix_ref):
        sg = sg_ref[...].astype(F32)
        s = att_ref[...] * (sg * _sigmoid(sg))
        cat_ref[:, 0:D_CONV] = c5_ref[...]
        cat_ref[:, D_CONV:] = s.astype(BF16)
        mixed = _nn(cat_ref[...], w_ref[...])
        mix_ref[...] = mixed
        rstd = lax.rsqrt(jnp.mean(mixed * mixed, axis=-1, keepdims=True) + RMS_EPS)
        hn_ref[...] = h_ref[...] + (mixed * rstd) * g_ref[...]

    half = pl.BlockSpec((tm, 512), lambda i: (i, 0))
    full = pl.BlockSpec((tm, d), lambda i: (i, 0))
    return _pcall(
        body, name="outproj_fwd", grid=(lp // tm,),
        in_specs=[half, half, pl.BlockSpec((tm, 512), lambda i: (i, 3)), full,
                  pl.BlockSpec((d, d), lambda i: (0, 0)), pl.BlockSpec((1, d), lambda i: (0, 0))],
        out_specs=[full, full, full],
        out_shape=[jax.ShapeDtypeStruct((lp, d), F32), jax.ShapeDtypeStruct((lp, d), BF16),
                   jax.ShapeDtypeStruct((lp, d), F32)],
        compiler_params=_params("parallel"),
    )(c5, att, ew, h, w, g)


def _loss_head(h, target, seq):
    lp, d = h.shape
    tm = ROW_TILE

    def body(h_ref, t_ref, dh_ref, loss_ref):
        i = pl.program_id(0)

        @pl.when(i == 0)
        def _():
            loss_ref[...] = jnp.zeros_like(loss_ref)

        row = i * tm + lax.broadcasted_iota(jnp.int32, (tm, 1), 0)
        real = jnp.logical_and(row >= N_META, row < N_META + seq)
        diff = jnp.where(real, h_ref[...] - t_ref[...], 0.0)
        dh_ref[...] = diff * (1.0 / d)
        loss_ref[...] += 0.5 * jnp.sum(jnp.sum(diff * diff, axis=-1, keepdims=True) * (1.0 / d))

    full = pl.BlockSpec((tm, d), lambda i: (i, 0))
    return _pcall(
        body, name="loss_head", grid=(lp // tm,),
        in_specs=[full, full],
        out_specs=[full, pl.BlockSpec((8, LANES), lambda i: (0, 0))],
        out_shape=[jax.ShapeDtypeStruct((lp, d), F32), jax.ShapeDtypeStruct((8, LANES), F32)],
        compiler_params=_params("arbitrary"),
    )(h, target)


def _outproj_bwd(dh, mixed, g, w, att, ew, c4):
    lp, d = dh.shape
    tm = _mm_tile(lp)

    def body(dh_ref, mix_ref, g_ref, w_ref, att_ref, cg_ref, sg_ref, c4_ref,
             dmix_ref, datt_ref, dsg_ref, dc4_ref, dcg_ref, dg_ref, db_ref):
        @pl.when(pl.program_id(0) == 0)
        def _():
            dg_ref[...] = jnp.zeros_like(dg_ref)
            db_ref[...] = jnp.zeros_like(db_ref)

        mixed = mix_ref[...]
        dhv = dh_ref[...]
        rstd = lax.rsqrt(jnp.mean(mixed * mixed, axis=-1, keepdims=True) + RMS_EPS)
        n = mixed * rstd
        dg_ref[...] += jnp.sum(dhv * n, axis=0, keepdims=True)
        dn = dhv * g_ref[...]
        dmix = (rstd * (dn - n * jnp.mean(dn * n, axis=-1, keepdims=True))).astype(BF16)
        dmix_ref[...] = dmix
        dcat = _nt(dmix, w_ref[...])
        dc5 = dcat[:, 0:D_CONV]
        ds = dcat[:, D_CONV:]
        silu_sg, dsilu_sg = _silu_fwd_bwd(sg_ref[...].astype(F32))
        datt_ref[...] = (ds * silu_sg).astype(BF16)
        dsg_ref[...] = (ds * att_ref[...] * dsilu_sg).astype(BF16)
        silu_cg, dsilu_cg = _silu_fwd_bwd(cg_ref[...].astype(F32))
        dc4 = dc5 * silu_cg
        db_ref[...] += jnp.sum(dc4, axis=0, keepdims=True)
        dc4_ref[...] = dc4.astype(BF16)
        dcg_ref[...] = (dc5 * c4_ref[...] * dsilu_cg).astype(BF16)

    half = pl.BlockSpec((tm, 512), lambda i: (i, 0))
    full = pl.BlockSpec((tm, d), lambda i: (i, 0))
    hb = jax.ShapeDtypeStruct((lp, 512), BF16)
    return _pcall(
        body, name="outproj_bwd", grid=(lp // tm,),
        in_specs=[full, full, pl.BlockSpec((1, d), lambda i: (0, 0)), pl.BlockSpec((d, d), lambda i: (0, 0)),
                  half, pl.BlockSpec((tm, 512), lambda i: (i, 2)), pl.BlockSpec((tm, 512), lambda i: (i, 3)), half],
        out_specs=[full, half, half, half, half,
                   pl.BlockSpec((1, d), lambda i: (0, 0)), pl.BlockSpec((1, 512), lambda i: (0, 0))],
        out_shape=[jax.ShapeDtypeStruct((lp, d), BF16), hb, hb, hb, hb,
                   jax.ShapeDtypeStruct((1, d), F32), jax.ShapeDtypeStruct((1, 512), F32)],
        compiler_params=_params("arbitrary"),
    )(dh, mixed, g, w, att, ew, ew, c4)


def _attn_bwd(qkv, carries, datt, tri, upper, plan=None):
    lp = qkv.shape[0]
    bq = ATT_BLOCK
    ngrp = ATT_PAIRS
    nstep = D_SB // (LANES * ngrp)
    nq = lp // bq
    split, x_args, x_in_specs, x_out_specs, x_out_shapes, x_scratch = _hosted(plan, 3, 2)

    def body(q_ref, k_ref, v_ref, carry_ref, do_ref, tri_ref, upper_ref, *rest):
        x_in, (dq_ref, dk_ref, dv_ref), x_out, (run_s, dq_s), x_sems = split(rest)
        i = pl.program_id(1)
        if plan is not None:
            @pl.when(jnp.logical_and(pl.program_id(0) == 0, i == 0))
            def _():
                plan.start(x_in, x_out, x_sems)

            @pl.when(jnp.logical_and(pl.program_id(0) == nstep - 1, i == max(nq - 2, 0)))
            def _():
                plan.relay(x_in, x_out, x_sems)

        @pl.when(i == 0)
        def _():
            dk_ref[...] = jnp.zeros_like(dk_ref)
            dv_ref[...] = jnp.zeros_like(dv_ref)

        first_head, vis = _attn_masks()
        lane = lax.broadcasted_iota(jnp.int32, (1, LANES), 1)
        cols = [slice(g * LANES, (g + 1) * LANES) for g in range(ngrp)]
        q2s = [_stack_heads(q_ref[:, cs], first_head) for cs in cols]
        do2s = [_stack_heads(do_ref[:, cs], first_head) for cs in cols]
        cmats = [jnp.concatenate([carry_ref[:, 2 * g * LANES:(2 * g + 1) * LANES],
                                  carry_ref[:, (2 * g + 1) * LANES:(2 * g + 2) * LANES]], axis=0)
                 for g in range(ngrp)]
        tri_m = tri_ref[...]
        upper_m = upper_ref[...]

        def blocks(js, masks):
            offs = [pl.multiple_of(j * bq, bq) for j in js]
            work = [(g, b) for b in range(len(js)) for g in range(ngrp)]
            zs = {(g, b): _nt(q2s[g], k_ref[pl.ds(offs[b], bq), cols[g]]) for g, b in work}
            lss = {}
            for g, b in work:
                z = zs[g, b]
                ls = -(jnp.maximum(z, 0.0) + jnp.log(1.0 + jnp.exp(-jnp.abs(z))))
                lss[g, b] = ls if masks[b] is None else jnp.where(masks[b], ls, 0.0)
            tails = {gb: _block_sums(lss[gb], tri_m) for gb in work}
            das = {(g, b): _nt(do2s[g], v_ref[pl.ds(offs[b], bq), cols[g]]) for g, b in work}
            probs, des = {}, {}
            for g, b in work:
                c = jnp.sum(jnp.where(lane == js[b], cmats[g], 0.0), axis=-1, keepdims=True)
                a = jnp.exp(zs[g, b] + lss[g, b] + tails[g, b] + c)
                a = a if masks[b] is None else jnp.where(masks[b], a, 0.0)
                probs[g, b] = a.astype(BF16)
                des[g, b] = das[g, b] * a
            prefixes = {gb: _block_sums(des[gb], upper_m) for gb in work}
            runs = [run_s[g] for g in range(ngrp)]
            dzs = {}
            for g, b in work:
                beta = jnp.exp(zs[g, b] + lss[g, b])
                dz = des[g, b] - beta * (des[g, b] + runs[g] + prefixes[g, b])
                dzs[g, b] = (dz if masks[b] is None else jnp.where(masks[b], dz, 0.0)).astype(BF16)
                runs[g] = runs[g] + prefixes[g, b][:, bq - 1:bq] + des[g, b][:, bq - 1:bq]
            for g in range(ngrp):
                run_s[g] = runs[g]
                dq = dq_s[g]
                for b in range(len(js)):
                    rows = pl.ds(offs[b], bq)
                    dq = dq + _nn(dzs[g, b], k_ref[rows, cols[g]])
                    dk_ref[rows, cols[g]] += _tn(dzs[g, b], q2s[g])
                    dv_ref[rows, cols[g]] += _tn(probs[g, b], do2s[g])
                dq_s[g] = dq

        n_done = jnp.max(carry_ref[:, COUNT_LANE:COUNT_LANE + 1]).astype(jnp.int32)
        n_done = jnp.clip(n_done, 1, i + 1)
        before = jnp.maximum(n_done - 2, 0)
        j0 = i - n_done + 1
        odd = before % 2
        run_s[...] = jnp.zeros_like(run_s)
        dq_s[...] = jnp.zeros_like(dq_s)

        @pl.when(odd == 1)
        def _():
            blocks([j0], [None])

        @pl.loop(0, before // 2)
        def _(t):
            blocks([j0 + odd + 2 * t, j0 + odd + 2 * t + 1], [None, None])

        @pl.when(n_done > 1)
        def _():
            blocks([i - 1, i], [None, vis])

        @pl.when(n_done <= 1)
        def _():
            blocks([i], [vis])

        for g in range(ngrp):
            dq_ref[:, cols[g]] = (_unstack_heads(dq_s[g], first_head) * Q_SCALE).astype(BF16)
        if plan is not None:
            @pl.when(jnp.logical_and(pl.program_id(0) == nstep - 1, i == nq - 1))
            def _():
                plan.finish(x_in, x_out, x_sems)

    width = ngrp * LANES
    once = pl.Buffered(1)
    qb = pl.BlockSpec((bq, width), lambda p, i: (i, p))
    colb = pl.BlockSpec((lp, width), lambda p, i: (0, p), pipeline_mode=once)
    sq = pl.BlockSpec((bq, bq), lambda p, i: (0, 0))
    outs = _pcall(
        body, name="attn_bwd" if plan is None else "attn_bwd_reduce", grid=(nstep, nq),
        in_specs=[qb,
                  pl.BlockSpec((lp, width), lambda p, i: (0, nstep + p), pipeline_mode=once),
                  pl.BlockSpec((lp, width), lambda p, i: (0, 2 * nstep + p), pipeline_mode=once),
                  pl.BlockSpec((bq, 2 * width), lambda p, i: (i, p)), qb, sq, sq] + x_in_specs,
        out_specs=[qb, colb, colb] + x_out_specs,
        out_shape=[jax.ShapeDtypeStruct((lp, D_SB), BF16), jax.ShapeDtypeStruct((lp, D_SB), F32),
                   jax.ShapeDtypeStruct((lp, D_SB), F32)] + x_out_shapes,
        scratch_shapes=[pltpu.VMEM((ngrp, 2 * bq, 1), F32), pltpu.VMEM((ngrp, 2 * bq, LANES), F32)] + x_scratch,
        compiler_params=_params("arbitrary", "arbitrary"),
    )(qkv, qkv, qkv, carries, datt, tri, upper, *x_args)
    return outs[0], outs[1], outs[2], outs[3:]


def _conv_bwd(dc4, c1, ew, cw, lng, lnb, wpw2):
    lp = ew.shape[0]
    tm = ROW_TILE
    nt = lp // tm
    halo_per_tile = tm // CONV_PAD

    def body(dc4_ref, c1_ref, ew_ref, halo_ref, cw_ref, lng_ref, lnb_ref, w_ref,
             dga_ref, dgb_ref, c3_ref, dcw_ref, dcb_ref, dlng_ref, dlnb_ref, xbuf, dbuf, shifted, wacc):
        step = pl.program_id(0)

        @pl.when(step == 0)
        def _():
            wacc[...] = jnp.zeros_like(wacc)
            dcb_ref[...] = jnp.zeros_like(dcb_ref)
            dlng_ref[...] = jnp.zeros_like(dlng_ref)
            dlnb_ref[...] = jnp.zeros_like(dlnb_ref)
            dbuf[tm:tm + CONV_PAD, :] = jnp.zeros((CONV_PAD, D_CONV), F32)

        dc3 = _nt(dc4_ref[...], w_ref[...])
        xhat, rstd = _layer_norm_stats(c1_ref[...])
        c2 = xhat * lng_ref[...] + lnb_ref[...]
        c3, dsilu = _silu_fwd_bwd(c2)
        c3_ref[...] = c3.astype(BF16)
        dc2 = dc3 * dsilu
        dlng_ref[...] += jnp.sum(dc2 * xhat, axis=0, keepdims=True)
        dlnb_ref[...] += jnp.sum(dc2, axis=0, keepdims=True)
        dxhat = dc2 * lng_ref[...]
        dc1 = rstd * (dxhat - jnp.mean(dxhat, axis=-1, keepdims=True)
                      - xhat * jnp.mean(dxhat * xhat, axis=-1, keepdims=True))
        dcb_ref[...] += jnp.sum(dc1, axis=0, keepdims=True)
        dbuf[0:tm, :] = dc1

        ga = ew_ref[:, 0:512].astype(F32)
        sgb = _sigmoid(ew_ref[:, 512:1024].astype(F32))
        xbuf[CONV_PAD:CONV_PAD + tm, :] = ga * sgb
        first_tile = step == nt - 1
        halo = halo_ref[:, 0:512].astype(F32) * _sigmoid(halo_ref[:, 512:1024].astype(F32))
        xbuf[0:CONV_PAD, :] = jnp.where(first_tile, 0.0, halo)

        for cb in range(D_CONV // LANES):
            cs = slice(cb * LANES, (cb + 1) * LANES)
            _shifted_copies(dbuf, cs, shifted, tm)
            for r0 in range(0, tm, CONV_ROWS):
                rs = slice(r0, r0 + CONV_ROWS)
                dc0 = jnp.zeros((CONV_ROWS, LANES), F32)
                for j in range(CONV_WIDTH):
                    dc0 = dc0 + _weighted(cw_ref, j, cs, _shifted_rows(
                        shifted, r0 + CONV_WIDTH - 1 - j, CONV_ROWS))
                dga_ref[rs, cs] = (dc0 * sgb[rs, cs]).astype(BF16)
                dgb_ref[rs, cs] = (dc0 * ga[rs, cs] * sgb[rs, cs] * (1.0 - sgb[rs, cs])).astype(BF16)
            _shifted_copies(xbuf, cs, shifted, tm)
            for r0 in range(0, tm, CONV_ROWS):
                d1 = dbuf[r0:r0 + CONV_ROWS, cs]

                for j in range(CONV_WIDTH):
                    prod = d1 * _shifted_rows(shifted, r0 + CONV_PAD - (CONV_WIDTH - 1) + j, CONV_ROWS)
                    wacc[j * 8:(j + 1) * 8, cs] += jnp.sum(prod.reshape(CONV_ROWS // 8, 8, LANES), axis=0)
        dbuf[tm:tm + CONV_PAD, :] = dbuf[0:CONV_PAD, :]

        @pl.when(step == nt - 1)
        def _():
            dcw_ref[...] = jnp.sum(wacc[...].reshape(CONV_PAD, 8, D_CONV), axis=1)

    rev = lambda i: (nt - 1 - i, 0)
    row = pl.BlockSpec((tm, D_CONV), rev)
    vec = pl.BlockSpec((1, D_CONV), lambda i: (0, 0))
    hb = jax.ShapeDtypeStruct((lp, D_CONV), BF16)
    vs = jax.ShapeDtypeStruct((1, D_CONV), F32)
    return _pcall(
        body, name="conv_bwd", grid=(nt,),
        in_specs=[row, row, pl.BlockSpec((tm, 1024), rev),
                  pl.BlockSpec((CONV_PAD, 1024), lambda i: (jnp.maximum((nt - 1 - i) * halo_per_tile - 1, 0), 0)),
                  pl.BlockSpec((8 * CONV_PAD, D_CONV), lambda i: (0, 0)), vec, vec,
                  pl.BlockSpec((D_CONV, D_CONV), lambda i: (0, 0))],
        out_specs=[row, row, row, pl.BlockSpec((CONV_PAD, D_CONV), lambda i: (0, 0)), vec, vec, vec],
        out_shape=[hb, hb, hb, jax.ShapeDtypeStruct((CONV_PAD, D_CONV), F32), vs, vs, vs],
        scratch_shapes=[pltpu.VMEM((tm + CONV_PAD, D_CONV), F32), pltpu.VMEM((tm + CONV_PAD, D_CONV), F32),
                        pltpu.VMEM((8, tm + CONV_PAD, LANES), F32), pltpu.VMEM((8 * CONV_PAD, D_CONV), F32)],
        compiler_params=_params("arbitrary"),
    )(dc4, c1, ew, ew, cw, lng, lnb, wpw2)


def _inproj_bwd(dga, dgb, dcg, dq, dk, dv, dsg, h, g, w, dh_out, plan=None):
    lp, d = h.shape
    n = w.shape[1]
    tm = _mm_tile(lp)
    steps = lp // tm
    split, x_args, x_in_specs, x_out_specs, x_out_shapes, x_scratch = _hosted(plan, 4, 0)

    def body(dga_ref, dgb_ref, dcg_ref, dq_ref, dk_ref, dv_ref, dsg_ref, h_ref, g_ref, w_ref, dho_ref, *rest):
        parts = split(rest)
        dh_ref, dproj_ref, u_ref, dg_ref = parts[1]
        _host_begin(plan, parts, steps)

        @pl.when(pl.program_id(0) == 0)
        def _():
            dg_ref[...] = jnp.zeros_like(dg_ref)

        dproj_ref[:, 0:512] = dga_ref[...]
        dproj_ref[:, 512:1024] = dgb_ref[...]
        dproj_ref[:, 1024:1536] = dcg_ref[...]
        dproj_ref[:, 1536:2048] = dq_ref[...]
        dproj_ref[:, 2048:2560] = dk_ref[...].astype(BF16)
        dproj_ref[:, 2560:3072] = dv_ref[...].astype(BF16)
        dproj_ref[:, 3072:3584] = dsg_ref[...]
        du = _nt(dproj_ref[...], w_ref[...])
        x = h_ref[...]
        rstd = lax.rsqrt(jnp.mean(x * x, axis=-1, keepdims=True) + RMS_EPS)
        nrm = x * rstd
        u_ref[...] = (nrm * g_ref[...]).astype(BF16)
        dg_ref[...] += jnp.sum(du * nrm, axis=0, keepdims=True)
        dn = du * g_ref[...]
        dh_ref[...] = dho_ref[...] + rstd * (dn - nrm * jnp.mean(dn * nrm, axis=-1, keepdims=True))
        _host_end(plan, parts, steps)

    half = pl.BlockSpec((tm, 512), lambda i: (i, 0))
    full = pl.BlockSpec((tm, d), lambda i: (i, 0))
    outs = _pcall(
        body, name="inproj_bwd" if plan is None else "inproj_bwd_reduce", grid=(steps,),
        in_specs=[half] * 7 + [full, pl.BlockSpec((1, d), lambda i: (0, 0)),
                               pl.BlockSpec((d, n), lambda i: (0, 0)), full] + x_in_specs,
        out_specs=[full, pl.BlockSpec((tm, n), lambda i: (i, 0)), full,
                   pl.BlockSpec((1, d), lambda i: (0, 0))] + x_out_specs,
        out_shape=[jax.ShapeDtypeStruct((lp, d), F32), jax.ShapeDtypeStruct((lp, n), BF16),
                   jax.ShapeDtypeStruct((lp, d), BF16), jax.ShapeDtypeStruct((1, d), F32)] + x_out_shapes,
        scratch_shapes=x_scratch,
        compiler_params=_params("arbitrary"),
    )(dga, dgb, dcg, dq, dk, dv, dsg, h, g, w, dh_out, *x_args)
    return outs[0], outs[1], outs[2], outs[3], outs[4:]


def _row_split(m, parts):
    tm = m // parts
    assert tm * parts == m and tm % 16 == 0, (m, parts)
    return tm


def _matmul_tn(x, dy, tn, name):
    m, k = x.shape
    n = dy.shape[1]
    steps = 4 if m % 64 == 0 else 1
    tm = _row_split(m, steps)

    def body(x_ref, dy_ref, o_ref, acc_ref):
        r = pl.program_id(1)

        @pl.when(r == 0)
        def _():
            acc_ref[...] = jnp.zeros_like(acc_ref)

        acc_ref[...] += _tn(x_ref[...], dy_ref[...])

        @pl.when(r == steps - 1)
        def _():
            o_ref[...] = acc_ref[...].astype(BF16)

    return _pcall(
        body, name=name, grid=(n // tn, steps),
        in_specs=[pl.BlockSpec((tm, k), lambda j, r: (r, 0)), pl.BlockSpec((tm, tn), lambda j, r: (r, j))],
        out_specs=pl.BlockSpec((k, tn), lambda j, r: (0, j)),
        out_shape=jax.ShapeDtypeStruct((k, n), BF16),
        scratch_shapes=[pltpu.VMEM((k, tn), F32)],
        compiler_params=_params("parallel", "arbitrary"),
    )(x, dy)


def _local_step(h0, target_p, seq, vecs, depth, all_weights=None, w_in0=None, gather_w_in=None, gather_rest=None,
                reduce_layer=None):
    pre_g, post_g, conv_b, ln_g, ln_b, b_pw2 = vecs
    ar = jnp.arange(ATT_BLOCK)
    tri = (ar[:, None] > ar[None, :]).astype(BF16)
    upper = (ar[:, None] < ar[None, :]).astype(BF16)
    row = lambda a, l: a[l][None, :]

    hosted = all_weights is None
    weights = [None] * depth if hosted else list(all_weights)
    next_w_in, next_rest = w_in0, None
    saved = []
    h = h0
    for l in range(depth):
        more = hosted and l + 1 < depth
        if hosted:
            ew, qkv, rest = _inproj(h, row(pre_g, l), next_w_in, gather_rest(0) if l == 0 else None)
            w_in, (w_pw2, w_out, conv_w) = next_w_in, (rest if l == 0 else next_rest)
        else:
            w_in, w_pw2, w_out, conv_w = weights[l]
            ew, qkv, _ = _inproj(h, row(pre_g, l), w_in)
        conv_w = jnp.repeat(conv_w, 8, axis=0)
        weights[l] = (w_in, w_pw2, w_out, conv_w)
        c1, c4, c5, next_rest = _conv_fwd(ew, conv_w, row(conv_b, l), row(ln_g, l), row(ln_b, l), w_pw2,
                                          row(b_pw2, l), gather_rest(l + 1) if more else None)
        att, carries, gathered = _attn_fwd(qkv, tri, gather_w_in(l + 1) if more else None)
        if more:
            next_w_in = gathered[0]
        hn, cat, mixed = _outproj(c5, att, ew, h, w_out, row(post_g, l))
        saved.append((h, ew, qkv, c1, c4, att, carries, cat, mixed))
        h = hn

    dh, loss = _loss_head(h, target_p, seq)

    vec_grads = [None] * depth
    mat_grads = [None] * depth
    pending = None

    def pieces(w_in_pieces, rest_pieces):
        n_rest = len(rest_pieces) // 2
        return [w_in_pieces[0], *rest_pieces[:n_rest], w_in_pieces[1], *rest_pieces[n_rest:]]

    for l in reversed(range(depth)):
        w_in, w_pw2, w_out, conv_w = weights[l]
        h_in, ew, qkv, c1, c4, att, carries, cat, mixed = saved[l]
        dmix, datt, dsg, dc4, dcg, dpost, dbpw2 = _outproj_bwd(dh, mixed, row(post_g, l), w_out, att, ew, c4)
        dw_out = _matmul_tn(cat, dmix, 512, "dw_out")
        dq, dk, dv, landed = _attn_bwd(qkv, carries, datt, tri, upper, pending)
        if pending is not None:
            mat_grads[l + 1] = pieces(landed, rest_landed)
        dga, dgb, c3, dcw, dcb, dlng, dlnb = _conv_bwd(dc4, c1, ew, conv_w, row(ln_g, l), row(ln_b, l), w_pw2)
        dw_pw2 = _matmul_tn(c3, dc4, 512, "dw_pw2")
        rest_plan = None if reduce_layer is None else reduce_layer([dw_pw2, dw_out, dcw], 1)
        dh, dproj, u, dpre, rest_landed = _inproj_bwd(dga, dgb, dcg, dq, dk, dv, dsg, h_in, row(pre_g, l), w_in, dh,
                                                      rest_plan)
        dw_in = _matmul_tn(u, dproj, 1792, "dw_in")
        vec_grads[l] = (dpre[0], dpost[0], dcb[0], dlng[0], dlnb[0], dbpw2[0])
        if reduce_layer is None:
            mat_grads[l] = (dw_in, dw_pw2, dw_out, dcw)
        else:
            pending = reduce_layer([dw_in], 0)
    if pending is not None:
        mat_grads[0] = pieces(_run_exchange(pending, "reduce_grads"), rest_landed)

    vec_grads = [jnp.stack([g[k] for g in vec_grads]) for k in range(len(vecs))]
    return loss[0, 0], dh, vec_grads, mat_grads


N_CHIPS = 4
ANY = pl.BlockSpec(memory_space=pl.ANY)


def _chip_peers():
    x, y, c = lax.axis_index("x"), lax.axis_index("y"), lax.axis_index("c")
    return x, y, c, [(x, 1 - y), (1 - x, y), (1 - x, 1 - y)]


def _shard_slices(refs, dims, idx):
    out = []
    for ref, (axis, size) in zip(refs, dims):
        assert size % LANES == 0
        start = pl.multiple_of(idx * size, LANES)
        sl = [slice(None)] * len(ref.shape)
        sl[axis] = pl.ds(start, size)
        out.append(ref.at[tuple(sl)])
    return out


class _Exchange(NamedTuple):
    inputs: list
    out_shapes: list
    scratch: list
    start: Callable
    relay: Callable
    finish: Callable


def _run_exchange(plan, name):
    n_in, n_out = len(plan.inputs), len(plan.out_shapes)

    def body(*refs):
        parts = refs[:n_in], refs[n_in:n_in + n_out], refs[n_in + n_out:]
        plan.start(*parts)
        plan.relay(*parts)
        plan.finish(*parts)

    return _pcall(body, name=name, in_specs=[ANY] * n_in, out_specs=[ANY] * n_out, out_shape=plan.out_shapes,
                  scratch_shapes=plan.scratch)(*plan.inputs)


def _gather_plan(shards, dims):
    n = len(shards)
    full_shapes = []
    halves = []
    for s, (axis, size) in zip(shards, dims):
        shp = list(s.shape)
        shp[axis] = size * N_CHIPS
        full_shapes.append(jax.ShapeDtypeStruct(tuple(shp), s.dtype))
        tile_rows = 32 // s.dtype.itemsize
        assert s.shape[0] % (2 * tile_rows) == 0
        halves.append((s.shape[0] // 2, tile_rows))

    def half(refs, which):
        return [r.at[pl.ds(pl.multiple_of(which * h, t), h)] for r, (h, t) in zip(refs, halves)]

    def copies(srcs, outs, sems):
        send, recv, loc = sems
        x, y, c, peers = _chip_peers()
        sibling = (x, y, 1 - c)
        mine = _shard_slices(outs, dims, 2 * x + y)
        local = [pltpu.make_async_copy(s, d, loc.at[a]) for a, (s, d) in enumerate(zip(srcs, mine))]

        def remote(src, dst, slot, a, dev):
            return pltpu.make_async_remote_copy(src, dst, send.at[slot, a], recv.at[slot, a],
                                                device_id=dev, device_id_type=MESH)

        sends = [remote(s, d, k, a, (px, py, c))
                 for k, (px, py) in enumerate(peers) for a, (s, d) in enumerate(zip(half(srcs, c), half(mine, c)))]
        theirs = [_shard_slices(outs, dims, 2 * px + py) for px, py in peers]
        arrivals = [remote(s, d, k, a, (px, py, c))
                    for k, (px, py) in enumerate(peers)
                    for a, (s, d) in enumerate(zip(half(srcs, c), half(theirs[k], c)))]
        passed_on = [remote(d, d, 3 + k, a, sibling) for k in range(3) for a, d in enumerate(half(theirs[k], c))]
        from_sibling = [remote(d, d, 3 + k, a, sibling)
                        for k in range(3) for a, d in enumerate(half(theirs[k], 1 - c))]
        return local, sends, arrivals, passed_on, from_sibling

    def start(srcs, outs, sems):
        local, sends = copies(srcs, outs, sems)[:2]
        for cp in local + sends:
            cp.start()

    def relay(srcs, outs, sems):
        _, _, arrivals, passed_on, _ = copies(srcs, outs, sems)
        for arrived, onward in zip(arrivals, passed_on):
            arrived.wait_recv()
            onward.start()

    def finish(srcs, outs, sems):
        local, sends, _, passed_on, from_sibling = copies(srcs, outs, sems)
        for cp in from_sibling:
            cp.wait_recv()
        for cp in sends + passed_on:
            cp.wait_send()
        for cp in local:
            cp.wait()

    scratch = [pltpu.SemaphoreType.DMA((6, n)), pltpu.SemaphoreType.DMA((6, n)), pltpu.SemaphoreType.DMA((n,))]
    return _Exchange(list(shards), full_shapes, scratch, start, relay, finish)


def _reduce_plan(grads, dims):
    n = len(grads)
    piece_shapes = []
    for g, (axis, size) in zip(grads, dims):
        shp = list(g.shape)
        shp[axis] = size
        piece_shapes.append(jax.ShapeDtypeStruct((N_CHIPS,) + tuple(shp), g.dtype))

    def copies(srcs, outs, sems):
        mine, theirs = outs[:n], outs[n:]
        send, recv, loc = sems
        x, y, c, peers = _chip_peers()
        sibling = (x, y, 1 - c)
        own = _shard_slices(srcs, dims, 2 * x + y)

        def remote(src, dst, slot, a, dev):
            return pltpu.make_async_remote_copy(src, dst, send.at[slot, a], recv.at[slot, a],
                                                device_id=dev, device_id_type=MESH)

        local = [pltpu.make_async_copy(own[a], mine[a].at[3], loc.at[a]) for a in range(n)]
        to_sibling = [remote(own[a], theirs[a].at[3], 3, a, sibling) for a in range(n)]
        to_chips = [remote(src, mine[a].at[k], k, a, (px, py, c))
                    for k, (px, py) in enumerate(peers)
                    for a, src in enumerate(_shard_slices(srcs, dims, 2 * px + py))]
        passed_on = [remote(mine[a].at[k], theirs[a].at[k], 4 + k, a, sibling) for k in range(3) for a in range(n)]
        return local, to_sibling, to_chips, passed_on

    def start(srcs, outs, sems):
        local, to_sibling, to_chips, _ = copies(srcs, outs, sems)
        for cp in local + to_sibling + to_chips:
            cp.start()

    def relay(srcs, outs, sems):
        _, _, to_chips, passed_on = copies(srcs, outs, sems)
        for arrived, onward in zip(to_chips, passed_on):
            arrived.wait_recv()
            onward.start()

    def finish(srcs, outs, sems):
        local, to_sibling, to_chips, passed_on = copies(srcs, outs, sems)
        for cp in to_sibling + passed_on:
            cp.wait_recv()
        for cp in to_sibling + to_chips + passed_on:
            cp.wait_send()
        for cp in local:
            cp.wait()

    scratch = [pltpu.SemaphoreType.DMA((7, n)), pltpu.SemaphoreType.DMA((7, n)), pltpu.SemaphoreType.DMA((n,))]
    return _Exchange(list(grads), piece_shapes * 2, scratch, start, relay, finish)


def _allsum_small(pack):
    rows, cols = pack.shape
    ndev = 8

    def body(p_ref, o_ref, buf, send, recv):
        x, y, c = lax.axis_index("x"), lax.axis_index("y"), lax.axis_index("c")
        me = 4 * x + 2 * y + c
        buf[me] = p_ref[...]
        started = []
        for r in range(1, ndev):
            bx, by, bc = (r >> 2) & 1, (r >> 1) & 1, r & 1
            dev = (x ^ bx, y ^ by, c ^ bc)
            cp = pltpu.make_async_remote_copy(p_ref, buf.at[me], send.at[r], recv.at[r],
                                              device_id=dev, device_id_type=MESH)
            cp.start()
            started.append(cp)
        for r in range(1, ndev):
            pltpu.make_async_remote_copy(p_ref, buf.at[me ^ r], send.at[r], recv.at[r],
                                         device_id=(x, y, c), device_id_type=MESH).wait_recv()
        for cp in started:
            cp.wait_send()
        acc = buf[0]
        for d in range(1, ndev):
            acc = acc + buf[d]
        o_ref[...] = acc

    vm = pl.BlockSpec(memory_space=pltpu.VMEM)
    return _pcall(
        body, name="allsum_small", in_specs=[vm], out_specs=vm,
        out_shape=jax.ShapeDtypeStruct((rows, cols), F32),
        scratch_shapes=[pltpu.VMEM((ndev, rows, cols), F32), pltpu.SemaphoreType.DMA((ndev,)),
                        pltpu.SemaphoreType.DMA((ndev,))],
    )(pack)


def _adamw(parts, w, m, v, layer, prev, name):
    _, rows, cols = w.shape
    tr = ROW_TILE if rows % ROW_TILE == 0 else rows
    counts = [p.shape[0] for p in parts]
    n_parts = len(parts)
    n_prev = 0 if prev is None else 4

    def body(*refs):
        part_refs = refs[:n_parts]
        w_ref, m_ref, v_ref = refs[n_parts:n_parts + 3]
        g_ref, d_ref, nm_ref, nv_ref = refs[n_parts + 3 + n_prev:]
        g = None
        for p_ref, cnt in zip(part_refs, counts):
            s = p_ref[0].astype(F32)
            for k in range(1, cnt):
                s = s + p_ref[k].astype(F32)
            g = s if g is None else g + s
        m2 = ADAM_B1 * m_ref[0] + (1.0 - ADAM_B1) * g
        v2 = ADAM_B2 * v_ref[0] + (1.0 - ADAM_B2) * (g * g)
        m_hat = m2 / (1.0 - ADAM_B1 ** ADAM_STEP)
        v_hat = v2 / (1.0 - ADAM_B2 ** ADAM_STEP)
        g_ref[0] = g
        d_ref[0] = -ADAM_LR * (m_hat / (jnp.sqrt(v_hat) + ADAM_EPS) + ADAM_WD * w_ref[0])
        nm_ref[0] = m2
        nv_ref[0] = v2

    blk = pl.BlockSpec((1, tr, cols), lambda i: (layer, i, 0))
    shp = jax.ShapeDtypeStruct(w.shape, F32)
    return _pcall(
        body, name=name, grid=(rows // tr,),
        in_specs=[pl.BlockSpec((cnt, tr, cols), lambda i: (0, i, 0)) for cnt in counts] + [blk] * 3 + [ANY] * n_prev,
        out_specs=[blk] * 4, out_shape=[shp] * 4,
        input_output_aliases={n_parts + 3 + k: k for k in range(n_prev)},
        compiler_params=_params("parallel"),
    )(*parts, w, m, v, *(prev or ()))


def kernel(x, meta_tokens, pre_norm_g, post_norm_g, w_in, conv_w, conv_b, conv_ln_g, conv_ln_b, w_pw2, b_pw2, w_out, loss_target, m_meta_tokens, m_pre_norm_g, m_post_norm_g, m_w_in, m_conv_w, m_conv_b, m_conv_ln_g, m_conv_ln_b, m_w_pw2, m_b_pw2, m_w_out, v_meta_tokens, v_pre_norm_g, v_post_norm_g, v_w_in, v_conv_w, v_conv_b, v_conv_ln_g, v_conv_ln_b, v_w_pw2, v_b_pw2, v_w_out):
    seq, d = x.shape[1], x.shape[2]
    depth = w_in.shape[0]
    length = N_META + seq
    lp = -(-length // ATT_BLOCK) * ATT_BLOCK
    tap_pad = ((0, 0), (0, CONV_PAD - CONV_WIDTH), (0, 0))

    shards = (w_in.astype(BF16), w_pw2.astype(BF16), w_out.astype(BF16), jnp.pad(conv_w, tap_pad))
    dims = [(1, w_in.shape[2]), (0, w_pw2.shape[1]), (0, w_out.shape[1]), (1, conv_w.shape[2])]
    layer_shards = lambda l: [s[l] for s in shards]

    w_in0, meta_f = _run_exchange(_gather_plan([shards[0][0], meta_tokens], [dims[0], (1, meta_tokens.shape[1])]),
                                  "gather_weights")

    h0 = jnp.concatenate([meta_f, x[0], jnp.zeros((lp - length, d), F32)], axis=0)
    target_p = jnp.pad(loss_target[0], ((N_META, lp - length), (0, 0)))
    vecs = (pre_norm_g, post_norm_g, conv_b, conv_ln_g, conv_ln_b, b_pw2)
    loss, dh0, vec_grads, pieces = _local_step(
        h0, target_p, seq, vecs, depth, w_in0=w_in0,
        gather_w_in=lambda l: _gather_plan([shards[0][l]], dims[:1]),
        gather_rest=lambda l: _gather_plan(layer_shards(l)[1:], dims[1:]),
        reduce_layer=lambda grads, first: _reduce_plan(list(grads), dims[first:first + len(grads)]))

    def update(k, w, m, v, name):
        outs = None
        for l in reversed(range(depth)):
            outs = _adamw([pieces[l][k], pieces[l][4 + k]], w, m, v, l, outs, name)
        return outs

    up_w_in = update(0, w_in, m_w_in, v_w_in, "adamw_w_in")
    up_w_pw2 = update(1, w_pw2, m_w_pw2, v_w_pw2, "adamw_w_pw2")
    up_w_out = update(2, w_out, m_w_out, v_w_out, "adamw_w_out")
    up_conv_w = [o[:, :CONV_WIDTH] for o in update(3, jnp.pad(conv_w, tap_pad), jnp.pad(m_conv_w, tap_pad),
                                                   jnp.pad(v_conv_w, tap_pad, constant_values=1.0), "adamw_conv_w")]

    two = lambda a: a.reshape(-1, d)
    vec_rows = [two(g) for g in vec_grads]
    n_vec = sum(a.shape[0] for a in vec_rows)
    pack = jnp.concatenate(vec_rows + [dh0[:N_META], jnp.full((8, d), loss, F32)], axis=0)
    pack = jnp.pad(pack, ((0, -pack.shape[0] % 8), (0, 0)))
    tot = _allsum_small(pack)
    loss_all = tot[n_vec + N_META, 0]

    cat = lambda arrs: jnp.concatenate([two(t) for t in arrs], axis=0)[None]
    small_m = (m_pre_norm_g, m_post_norm_g, m_conv_b, m_conv_ln_g, m_conv_ln_b, m_b_pw2)
    small_v = (v_pre_norm_g, v_post_norm_g, v_conv_b, v_conv_ln_g, v_conv_ln_b, v_b_pw2)
    up_small = _adamw([tot[None, :n_vec]], cat(vecs), cat(small_m), cat(small_v), 0, None, "adamw_vectors")

    def unpack(o):
        res, r0 = [], 0
        for t in vecs:
            nrow = t.size // d
            res.append(o[0, r0:r0 + nrow].reshape(t.shape))
            r0 += nrow
        return res

    up_small = [unpack(o) for o in up_small]
    chip = 2 * lax.axis_index("x") + lax.axis_index("y")
    mcols = meta_tokens.shape[1]
    g_meta = lax.dynamic_slice_in_dim(tot[n_vec:n_vec + N_META], chip * mcols, mcols, axis=1)
    up_meta = [o[0] for o in _adamw([g_meta[None]], meta_tokens[None], m_meta_tokens[None], v_meta_tokens[None],
                                    0, None, "adamw_meta")]

    grad_x = dh0[N_META:length][None]
    outs = [loss_all, grad_x]
    for j in range(4):
        pre, post, cb, lg, lb, bp = up_small[j]
        outs += [up_meta[j], pre, post, up_w_in[j], up_conv_w[j], cb, lg, lb, up_w_pw2[j], bp, up_w_out[j]]
    return tuple(outs)
```

```python
from typing import Callable, NamedTuple

import jax
import jax.numpy as jnp
from jax import lax
from jax.experimental import pallas as pl
from jax.experimental.pallas import tpu as pltpu

F32 = jnp.float32
BF16 = jnp.bfloat16

N_META = 16
D_CONV = 512
D_SB = 512
HEAD_DIM = 64
CONV_WIDTH = 31
CONV_PAD = 32
CONV_ROWS = 128
RMS_EPS = 1e-6
LN_EPS = 1e-5
Q_SCALE = HEAD_DIM ** -0.5

ADAM_LR = 0.001
ADAM_B1 = 0.9
ADAM_B2 = 0.999
ADAM_EPS = 1e-08
ADAM_WD = 0.01
ADAM_STEP = 10

LANES = 128
ROW_TILE = 256
MM_TILE_MAX = 544
ATT_BLOCK = 256
ATT_PAIRS = 4
ATT_PAIRS_FWD = 4
VMEM_LIMIT = 56 * 1024 * 1024
EXP_ZERO = -1e30
COUNT_LANE = LANES - 1
RELAY_AT = 0.75

MESH = pl.DeviceIdType.MESH


def _pcall(body, **kw):
    return pl.pallas_call(body, **kw)


def _params(*sem):
    return pltpu.CompilerParams(dimension_semantics=sem, vmem_limit_bytes=VMEM_LIMIT)


def _sigmoid(x):
    return 1.0 / (1.0 + jnp.exp(-x))


def _silu_fwd_bwd(x):
    s = _sigmoid(x)
    return x * s, s * (1.0 + x * (1.0 - s))


def _nt(a, b):
    return lax.dot_general(a, b, (((1,), (1,)), ((), ())), preferred_element_type=F32)


def _tn(a, b):
    return lax.dot_general(a, b, (((0,), (0,)), ((), ())), preferred_element_type=F32)


def _nn(a, b):
    return jnp.dot(a, b, preferred_element_type=F32)


def _mm_tile(rows):
    return max(t for t in range(16, MM_TILE_MAX + 1, 16) if rows % t == 0)


def _host_begin(plan, parts, steps):
    if plan is not None:
        x_in, _, x_out, _, x_sems = parts

        @pl.when(pl.program_id(0) == 0)
        def _():
            plan.start(x_in, x_out, x_sems)

        @pl.when(pl.program_id(0) == int(RELAY_AT * steps))
        def _():
            plan.relay(x_in, x_out, x_sems)


def _host_end(plan, parts, steps):
    if plan is not None:
        x_in, _, x_out, _, x_sems = parts

        @pl.when(pl.program_id(0) == steps - 1)
        def _():
            plan.finish(x_in, x_out, x_sems)


def _inproj(h, g, w, plan=None):
    lp, d = h.shape
    n = w.shape[1]
    tm = _mm_tile(lp)
    steps = lp // tm
    split, x_args, x_in_specs, x_out_specs, x_out_shapes, x_scratch = _hosted(plan, 2, 0)

    def body(h_ref, g_ref, w_ref, *rest):
        parts = split(rest)
        ew_ref, qkv_ref = parts[1]
        _host_begin(plan, parts, steps)
        x = h_ref[...]
        rstd = lax.rsqrt(jnp.mean(x * x, axis=-1, keepdims=True) + RMS_EPS)
        u = ((x * rstd) * g_ref[...]).astype(BF16)
        p = _nn(u, w_ref[...])
        ew_ref[:, 0:1536] = p[:, 0:1536].astype(BF16)
        ew_ref[:, 1536:2048] = p[:, 3072:3584].astype(BF16)
        qkv_ref[:, 0:512] = (p[:, 1536:2048] * Q_SCALE).astype(BF16)
        qkv_ref[:, 512:1536] = p[:, 2048:3072].astype(BF16)
        _host_end(plan, parts, steps)

    outs = _pcall(
        body, name="inproj_fwd" if plan is None else "inproj_fwd_gather", grid=(steps,),
        in_specs=[pl.BlockSpec((tm, d), lambda i: (i, 0)),
                  pl.BlockSpec((1, d), lambda i: (0, 0)),
                  pl.BlockSpec((d, n), lambda i: (0, 0))] + x_in_specs,
        out_specs=[pl.BlockSpec((tm, 2048), lambda i: (i, 0)),
                   pl.BlockSpec((tm, 1536), lambda i: (i, 0))] + x_out_specs,
        out_shape=[jax.ShapeDtypeStruct((lp, 2048), BF16), jax.ShapeDtypeStruct((lp, 1536), BF16)] + x_out_shapes,
        scratch_shapes=x_scratch,
        compiler_params=_params("parallel" if plan is None else "arbitrary"),
    )(h, g, w, *x_args)
    return outs[0], outs[1], outs[2:]


def _layer_norm_stats(c1):
    mu = jnp.mean(c1, axis=-1, keepdims=True)
    xc = c1 - mu
    var = jnp.mean(xc * xc, axis=-1, keepdims=True)
    rstd = lax.rsqrt(var + LN_EPS)
    return xc * rstd, rstd


def _shifted_copies(window, cols, shifted, tm):
    shifted[0] = window[:, cols]
    rows = tm + CONV_PAD - 8
    for b in range(1, 8):
        shifted[b, 0:rows, :] = window[pl.ds(b, rows), cols]


def _shifted_rows(shifted, shift, tm):
    b = shift % 8
    return shifted[b, pl.ds(pl.multiple_of(shift - b, 8), tm), :]


def _weighted(cw8_ref, j, cols, rows):
    w8 = cw8_ref[pl.ds(pl.multiple_of(j * 8, 8), 8), cols]
    r = rows.shape[0]
    return (rows.reshape(r // 8, 8, LANES) * w8[None]).reshape(r, LANES)


def _conv_fwd(ew, cw, cb, lng, lnb, wpw2, bpw2, plan=None):
    lp = ew.shape[0]
    tm = ROW_TILE
    steps = lp // tm
    split, x_args, x_in_specs, x_out_specs, x_out_shapes, x_scratch = _hosted(plan, 3, 2)

    def body(ew_ref, cw_ref, cb_ref, lng_ref, lnb_ref, w_ref, b_ref, *rest):
        parts = split(rest)
        (c1_ref, c4_ref, c5_ref), (xbuf, shifted) = parts[1], parts[3]
        _host_begin(plan, parts, steps)
        @pl.when(pl.program_id(0) == 0)
        def _():
            xbuf[0:CONV_PAD, :] = jnp.zeros((CONV_PAD, D_CONV), F32)

        ga = ew_ref[:, 0:512].astype(F32)
        gb = ew_ref[:, 512:1024].astype(F32)
        cg = ew_ref[:, 1024:1536].astype(F32)
        xbuf[CONV_PAD:CONV_PAD + tm, :] = ga * _sigmoid(gb)
        for blk in range(D_CONV // LANES):
            cs = slice(blk * LANES, (blk + 1) * LANES)
            _shifted_copies(xbuf, cs, shifted, tm)
            for r0 in range(0, tm, CONV_ROWS):
                acc = jnp.zeros((CONV_ROWS, LANES), F32) + cb_ref[:, cs]
                for j in range(CONV_WIDTH):
                    acc = acc + _weighted(cw_ref, j, cs, _shifted_rows(
                        shifted, r0 + CONV_PAD - (CONV_WIDTH - 1) + j, CONV_ROWS))
                c1_ref[r0:r0 + CONV_ROWS, cs] = acc
        xbuf[0:CONV_PAD, :] = xbuf[tm:tm + CONV_PAD, :]
        xhat, _ = _layer_norm_stats(c1_ref[...])
        c2 = xhat * lng_ref[...] + lnb_ref[...]
        c3 = c2 * _sigmoid(c2)
        c4 = _nn(c3.astype(BF16), w_ref[...]) + b_ref[...]
        c4_ref[...] = c4
        c5_ref[...] = (c4 * (cg * _sigmoid(cg))).astype(BF16)
        _host_end(plan, parts, steps)

    vec = pl.BlockSpec((1, D_CONV), lambda i: (0, 0))
    row = pl.BlockSpec((tm, D_CONV), lambda i: (i, 0))
    outs = _pcall(
        body, name="conv_fwd" if plan is None else "conv_fwd_gather", grid=(steps,),
        in_specs=[pl.BlockSpec((tm, 1536), lambda i: (i, 0)),
                  pl.BlockSpec((8 * CONV_PAD, D_CONV), lambda i: (0, 0)),
                  vec, vec, vec,
                  pl.BlockSpec((D_CONV, D_CONV), lambda i: (0, 0)),
                  vec] + x_in_specs,
        out_specs=[row, row, row] + x_out_specs,
        out_shape=[jax.ShapeDtypeStruct((lp, D_CONV), F32), jax.ShapeDtypeStruct((lp, D_CONV), F32),
                   jax.ShapeDtypeStruct((lp, D_CONV), BF16)] + x_out_shapes,
        scratch_shapes=[pltpu.VMEM((tm + CONV_PAD, D_CONV), F32),
                        pltpu.VMEM((8, tm + CONV_PAD, LANES), F32)] + x_scratch,
        compiler_params=_params("arbitrary"),
    )(ew, cw, cb, lng, lnb, wpw2, bpw2, *x_args)
    return outs[0], outs[1], outs[2], outs[3:]


def _block_sums(x, m01):
    return _nn(x.astype(BF16), m01)


def _attn_masks():
    lane = lax.broadcasted_iota(jnp.int32, (1, LANES), 1)
    row = lax.broadcasted_iota(jnp.int32, (2 * ATT_BLOCK, ATT_BLOCK), 0)
    col = lax.broadcasted_iota(jnp.int32, (2 * ATT_BLOCK, ATT_BLOCK), 1)
    return lane < HEAD_DIM, col < (row & (ATT_BLOCK - 1))


def _stack_heads(x, first_head):
    zero = jnp.zeros_like(x)
    return jnp.concatenate([jnp.where(first_head, x, zero), jnp.where(first_head, zero, x)], axis=0)


def _unstack_heads(x2, first_head):
    rows = x2.shape[0] // 2
    return jnp.where(first_head, x2[:rows], x2[rows:])


def _hosted(plan, n_out, n_scratch):
    n_in = 0 if plan is None else len(plan.inputs)
    n_x = 0 if plan is None else len(plan.out_shapes)

    def split(rest):
        a, b, c = n_in + n_out, n_in + n_out + n_x, n_in + n_out + n_x + n_scratch
        return rest[:n_in], rest[n_in:a], rest[a:b], rest[b:c], rest[c:]

    if plan is None:
        return split, [], [], [], [], []
    return split, list(plan.inputs), [ANY] * n_in, [ANY] * n_x, list(plan.out_shapes), list(plan.scratch)


def _attn_fwd(qkv, tri, plan=None):
    lp = qkv.shape[0]
    bq = ATT_BLOCK
    ngrp = ATT_PAIRS_FWD
    nstep = D_SB // (LANES * ngrp)
    nq = lp // bq
    assert nq <= COUNT_LANE
    split, x_args, x_in_specs, x_out_specs, x_out_shapes, x_scratch = _hosted(plan, 2, 3)

    def body(q_ref, k_ref, v_ref, tri_ref, *rest):
        x_in, (o_ref, carry_ref), x_out, (c_s, acc_s, cm_s), x_sems = split(rest)
        i = pl.program_id(1)
        if plan is not None:
            @pl.when(jnp.logical_and(pl.program_id(0) == 0, i == 0))
            def _():
                plan.start(x_in, x_out, x_sems)

            @pl.when(jnp.logical_and(pl.program_id(0) == nstep - 1, i == int(RELAY_AT * nq)))
            def _():
                plan.relay(x_in, x_out, x_sems)

        first_head, vis = _attn_masks()
        lane = lax.broadcasted_iota(jnp.int32, (1, LANES), 1)
        cols = [slice(g * LANES, (g + 1) * LANES) for g in range(ngrp)]
        q2s = [_stack_heads(q_ref[:, cs], first_head) for cs in cols]
        tri_m = tri_ref[...]

        c_s[...] = jnp.zeros_like(c_s)
        acc_s[...] = jnp.zeros_like(acc_s)
        cm_s[...] = jnp.zeros_like(cm_s)

        def blocks(js, masks):
            offs = [pl.multiple_of(j * bq, bq) for j in js]
            work = [(g, b) for b in range(len(js)) for g in range(ngrp)]
            zs = {(g, b): _nt(q2s[g], k_ref[pl.ds(offs[b], bq), cols[g]]) for g, b in work}
            lss = {}
            for g, b in work:
                z = zs[g, b]
                ls = -(jnp.maximum(z, 0.0) + jnp.log(1.0 + jnp.exp(-jnp.abs(z))))
                lss[g, b] = ls if masks[b] is None else jnp.where(masks[b], ls, 0.0)
            tails = {gb: _block_sums(lss[gb], tri_m) for gb in work}
            probs = {}
            carry = [c_s[g] for g in range(ngrp)]
            saved = [cm_s[g] for g in range(ngrp)]
            for g, b in work:
                a = jnp.exp(zs[g, b] + lss[g, b] + tails[g, b] + carry[g])
                probs[g, b] = (a if masks[b] is None else jnp.where(masks[b], a, 0.0)).astype(BF16)
                saved[g] = jnp.where(lane == js[b], carry[g], saved[g])
                carry[g] = carry[g] + tails[g, b][:, 0:1] + lss[g, b][:, 0:1]
            top = None
            for g in range(ngrp):
                c_s[g] = carry[g]
                cm_s[g] = saved[g]
                acc = acc_s[g]
                for b in range(len(js)):
                    acc = acc + _nn(probs[g, b], v_ref[pl.ds(offs[b], bq), cols[g]])
                acc_s[g] = acc
                top = carry[g] if top is None else jnp.maximum(top, carry[g])
            return jnp.max(top) > EXP_ZERO

        alive = lax.cond(i > 0, lambda: blocks([i, i - 1], [vis, None]), lambda: blocks([i], [vis]))
        rest = jnp.maximum(i - 1, 0)

        def pair(carry):
            t, _ = carry
            j = i - 2 - 2 * t
            return t + 1, blocks([j, j - 1], [None, None])

        trips, alive = lax.while_loop(lambda ca: jnp.logical_and(ca[0] < rest // 2, ca[1]), pair, (0, alive))
        last = jnp.logical_and(jnp.logical_and(rest % 2 == 1, trips == rest // 2), alive)

        @pl.when(last)
        def _():
            blocks([0], [None])

        n_done = (jnp.minimum(i + 1, 2) + 2 * trips + last.astype(jnp.int32)).astype(F32)
        for g in range(ngrp):
            cmat = jnp.where(lane == COUNT_LANE, n_done, cm_s[g])
            carry_ref[:, 2 * g * LANES:(2 * g + 1) * LANES] = cmat[:bq]
            carry_ref[:, (2 * g + 1) * LANES:(2 * g + 2) * LANES] = cmat[bq:]
            o_ref[:, cols[g]] = _unstack_heads(acc_s[g], first_head)
        if plan is not None:
            @pl.when(jnp.logical_and(pl.program_id(0) == nstep - 1, i == nq - 1))
            def _():
                plan.finish(x_in, x_out, x_sems)

    width = ngrp * LANES
    outs = _pcall(
        body, name="attn_fwd" if plan is None else "attn_fwd_gather", grid=(nstep, nq),
        in_specs=[pl.BlockSpec((bq, width), lambda p, i: (i, p)),
                  pl.BlockSpec((lp, width), lambda p, i: (0, nstep + p)),
                  pl.BlockSpec((lp, width), lambda p, i: (0, 2 * nstep + p)),
                  pl.BlockSpec((bq, bq), lambda p, i: (0, 0))] + x_in_specs,
        out_specs=[pl.BlockSpec((bq, width), lambda p, i: (i, p)),
                   pl.BlockSpec((bq, 2 * width), lambda p, i: (i, p))] + x_out_specs,
        out_shape=[jax.ShapeDtypeStruct((lp, D_SB), F32), jax.ShapeDtypeStruct((lp, 2 * D_SB), F32)] + x_out_shapes,
        scratch_shapes=[pltpu.VMEM((ngrp, 2 * bq, 1), F32), pltpu.VMEM((ngrp, 2 * bq, LANES), F32),
                        pltpu.VMEM((ngrp, 2 * bq, LANES), F32)] + x_scratch,
        compiler_params=_params("arbitrary", "arbitrary"),
    )(qkv, qkv, qkv, tri, *x_args)
    return outs[0], outs[1], outs[2:]


def _outproj(c5, att, ew, h, w, g):
    lp, d = h.shape
    tm = _mm_tile(lp)

    def body(c5_ref, att_ref, sg_ref, h_ref, w_ref, g_ref, hn_ref, cat_ref, mix_ref):
        sg = sg_ref[...].astype(F32)
        s = att_ref[...] * (sg * _sigmoid(sg))
        cat_ref[:, 0:D_CONV] = c5_ref[...]
        cat_ref[:, D_CONV:] = s.astype(BF16)
        mixed = _nn(cat_ref[...], w_ref[...])
        mix_ref[...] = mixed
        rstd = lax.rsqrt(jnp.mean(mixed * mixed, axis=-1, keepdims=True) + RMS_EPS)
        hn_ref[...] = h_ref[...] + (mixed * rstd) * g_ref[...]

    half = pl.BlockSpec((tm, 512), lambda i: (i, 0))
    full = pl.BlockSpec((tm, d), lambda i: (i, 0))
    return _pcall(
        body, name="outproj_fwd", grid=(lp // tm,),
        in_specs=[half, half, pl.BlockSpec((tm, 512), lambda i: (i, 3)), full,
                  pl.BlockSpec((d, d), lambda i: (0, 0)), pl.BlockSpec((1, d), lambda i: (0, 0))],
        out_specs=[full, full, full],
        out_shape=[jax.ShapeDtypeStruct((lp, d), F32), jax.ShapeDtypeStruct((lp, d), BF16),
                   jax.ShapeDtypeStruct((lp, d), F32)],
        compiler_params=_params("parallel"),
    )(c5, att, ew, h, w, g)


def _loss_head(h, target, seq):
    lp, d = h.shape
    tm = ROW_TILE

    def body(h_ref, t_ref, dh_ref, loss_ref):
        i = pl.program_id(0)

        @pl.when(i == 0)
        def _():
            loss_ref[...] = jnp.zeros_like(loss_ref)

        row = i * tm + lax.broadcasted_iota(jnp.int32, (tm, 1), 0)
        real = jnp.logical_and(row >= N_META, row < N_META + seq)
        diff = jnp.where(real, h_ref[...] - t_ref[...], 0.0)
        dh_ref[...] = diff * (1.0 / d)
        loss_ref[...] += 0.5 * jnp.sum(jnp.sum(diff * diff, axis=-1, keepdims=True) * (1.0 / d))

    full = pl.BlockSpec((tm, d), lambda i: (i, 0))
    return _pcall(
        body, name="loss_head", grid=(lp // tm,),
        in_specs=[full, full],
        out_specs=[full, pl.BlockSpec((8, LANES), lambda i: (0, 0))],
        out_shape=[jax.ShapeDtypeStruct((lp, d), F32), jax.ShapeDtypeStruct((8, LANES), F32)],
        compiler_params=_params("arbitrary"),
    )(h, target)


def _outproj_bwd(dh, mixed, g, w, att, ew, c4):
    lp, d = dh.shape
    tm = _mm_tile(lp)

    def body(dh_ref, mix_ref, g_ref, w_ref, att_ref, cg_ref, sg_ref, c4_ref,
             dmix_ref, datt_ref, dsg_ref, dc4_ref, dcg_ref, dg_ref, db_ref):
        @pl.when(pl.program_id(0) == 0)
        def _():
            dg_ref[...] = jnp.zeros_like(dg_ref)
            db_ref[...] = jnp.zeros_like(db_ref)

        mixed = mix_ref[...]
        dhv = dh_ref[...]
        rstd = lax.rsqrt(jnp.mean(mixed * mixed, axis=-1, keepdims=True) + RMS_EPS)
        n = mixed * rstd
        dg_ref[...] += jnp.sum(dhv * n, axis=0, keepdims=True)
        dn = dhv * g_ref[...]
        dmix = (rstd * (dn - n * jnp.mean(dn * n, axis=-1, keepdims=True))).astype(BF16)
        dmix_ref[...] = dmix
        dcat = _nt(dmix, w_ref[...])
        dc5 = dcat[:, 0:D_CONV]
        ds = dcat[:, D_CONV:]
        silu_sg, dsilu_sg = _silu_fwd_bwd(sg_ref[...].astype(F32))
        datt_ref[...] = (ds * silu_sg).astype(BF16)
        dsg_ref[...] = (ds * att_ref[...] * dsilu_sg).astype(BF16)
        silu_cg, dsilu_cg = _silu_fwd_bwd(cg_ref[...].astype(F32))
        dc4 = dc5 * silu_cg
        db_ref[...] += jnp.sum(dc4, axis=0, keepdims=True)
        dc4_ref[...] = dc4.astype(BF16)
        dcg_ref[...] = (dc5 * c4_ref[...] * dsilu_cg).astype(BF16)

    half = pl.BlockSpec((tm, 512), lambda i: (i, 0))
    full = pl.BlockSpec((tm, d), lambda i: (i, 0))
    hb = jax.ShapeDtypeStruct((lp, 512), BF16)
    return _pcall(
        body, name="outproj_bwd", grid=(lp // tm,),
        in_specs=[full, full, pl.BlockSpec((1, d), lambda i: (0, 0)), pl.BlockSpec((d, d), lambda i: (0, 0)),
                  half, pl.BlockSpec((tm, 512), lambda i: (i, 2)), pl.BlockSpec((tm, 512), lambda i: (i, 3)), half],
        out_specs=[full, half, half, half, half,
                   pl.BlockSpec((1, d), lambda i: (0, 0)), pl.BlockSpec((1, 512), lambda i: (0, 0))],
        out_shape=[jax.ShapeDtypeStruct((lp, d), BF16), hb, hb, hb, hb,
                   jax.ShapeDtypeStruct((1, d), F32), jax.ShapeDtypeStruct((1, 512), F32)],
        compiler_params=_params("arbitrary"),
    )(dh, mixed, g, w, att, ew, ew, c4)


def _attn_bwd(qkv, carries, datt, tri, upper, plan=None):
    lp = qkv.shape[0]
    bq = ATT_BLOCK
    ngrp = ATT_PAIRS
    nstep = D_SB // (LANES * ngrp)
    nq = lp // bq
    split, x_args, x_in_specs, x_out_specs, x_out_shapes, x_scratch = _hosted(plan, 3, 2)

    def body(q_ref, k_ref, v_ref, carry_ref, do_ref, tri_ref, upper_ref, *rest):
        x_in, (dq_ref, dk_ref, dv_ref), x_out, (run_s, dq_s), x_sems = split(rest)
        i = pl.program_id(1)
        if plan is not None:
            @pl.when(jnp.logical_and(pl.program_id(0) == 0, i == 0))
            def _():
                plan.start(x_in, x_out, x_sems)

            @pl.when(jnp.logical_and(pl.program_id(0) == nstep - 1, i == max(nq - 2, 0)))
            def _():
                plan.relay(x_in, x_out, x_sems)

        @pl.when(i == 0)
        def _():
            dk_ref[...] = jnp.zeros_like(dk_ref)
            dv_ref[...] = jnp.zeros_like(dv_ref)

        first_head, vis = _attn_masks()
        lane = lax.broadcasted_iota(jnp.int32, (1, LANES), 1)
        cols = [slice(g * LANES, (g + 1) * LANES) for g in range(ngrp)]
        q2s = [_stack_heads(q_ref[:, cs], first_head) for cs in cols]
        do2s = [_stack_heads(do_ref[:, cs], first_head) for cs in cols]
        cmats = [jnp.concatenate([carry_ref[:, 2 * g * LANES:(2 * g + 1) * LANES],
                                  carry_ref[:, (2 * g + 1) * LANES:(2 * g + 2) * LANES]], axis=0)
                 for g in range(ngrp)]
        tri_m = tri_ref[...]
        upper_m = upper_ref[...]

        def blocks(js, masks):
            offs = [pl.multiple_of(j * bq, bq) for j in js]
            work = [(g, b) for b in range(len(js)) for g in range(ngrp)]
            zs = {(g, b): _nt(q2s[g], k_ref[pl.ds(offs[b], bq), cols[g]]) for g, b in work}
            lss = {}
            for g, b in work:
                z = zs[g, b]
                ls = -(jnp.maximum(z, 0.0) + jnp.log(1.0 + jnp.exp(-jnp.abs(z))))
                lss[g, b] = ls if masks[b] is None else jnp.where(masks[b], ls, 0.0)
            tails = {gb: _block_sums(lss[gb], tri_m) for gb in work}
            das = {(g, b): _nt(do2s[g], v_ref[pl.ds(offs[b], bq), cols[g]]) for g, b in work}
            probs, des = {}, {}
            for g, b in work:
                c = jnp.sum(jnp.where(lane == js[b], cmats[g], 0.0), axis=-1, keepdims=True)
                a = jnp.exp(zs[g, b] + lss[g, b] + tails[g, b] + c)
                a = a if masks[b] is None else jnp.where(masks[b], a, 0.0)
                probs[g, b] = a.astype(BF16)
                des[g, b] = das[g, b] * a
            prefixes = {gb: _block_sums(des[gb], upper_m) for gb in work}
            runs = [run_s[g] for g in range(ngrp)]
            dzs = {}
            for g, b in work:
                beta = jnp.exp(zs[g, b] + lss[g, b])
                dz = des[g, b] - beta * (des[g, b] + runs[g] + prefixes[g, b])
                dzs[g, b] = (dz if masks[b] is None else jnp.where(masks[b], dz, 0.0)).astype(BF16)
                runs[g] = runs[g] + prefixes[g, b][:, bq - 1:bq] + des[g, b][:, bq - 1:bq]
            for g in range(ngrp):
                run_s[g] = runs[g]
                dq = dq_s[g]
                for b in range(len(js)):
                    rows = pl.ds(offs[b], bq)
                    dq = dq + _nn(dzs[g, b], k_ref[rows, cols[g]])
                    dk_ref[rows, cols[g]] += _tn(dzs[g, b], q2s[g])
                    dv_ref[rows, cols[g]] += _tn(probs[g, b], do2s[g])
                dq_s[g] = dq

        n_done = jnp.max(carry_ref[:, COUNT_LANE:COUNT_LANE + 1]).astype(jnp.int32)
        n_done = jnp.clip(n_done, 1, i + 1)
        before = jnp.maximum(n_done - 2, 0)
        j0 = i - n_done + 1
        odd = before % 2
        run_s[...] = jnp.zeros_like(run_s)
        dq_s[...] = jnp.zeros_like(dq_s)

        @pl.when(odd == 1)
        def _():
            blocks([j0], [None])

        @pl.loop(0, before // 2)
        def _(t):
            blocks([j0 + odd + 2 * t, j0 + odd + 2 * t + 1], [None, None])

        @pl.when(n_done > 1)
        def _():
            blocks([i - 1, i], [None, vis])

        @pl.when(n_done <= 1)
        def _():
            blocks([i], [vis])

        for g in range(ngrp):
            dq_ref[:, cols[g]] = (_unstack_heads(dq_s[g], first_head) * Q_SCALE).astype(BF16)
        if plan is not None:
            @pl.when(jnp.logical_and(pl.program_id(0) == nstep - 1, i == nq - 1))
            def _():
                plan.finish(x_in, x_out, x_sems)

    width = ngrp * LANES
    once = pl.Buffered(1)
    qb = pl.BlockSpec((bq, width), lambda p, i: (i, p))
    colb = pl.BlockSpec((lp, width), lambda p, i: (0, p), pipeline_mode=once)
    sq = pl.BlockSpec((bq, bq), lambda p, i: (0, 0))
    outs = _pcall(
        body, name="attn_bwd" if plan is None else "attn_bwd_reduce", grid=(nstep, nq),
        in_specs=[qb,
                  pl.BlockSpec((lp, width), lambda p, i: (0, nstep + p), pipeline_mode=once),
                  pl.BlockSpec((lp, width), lambda p, i: (0, 2 * nstep + p), pipeline_mode=once),
                  pl.BlockSpec((bq, 2 * width), lambda p, i: (i, p)), qb, sq, sq] + x_in_specs,
        out_specs=[qb, colb, colb] + x_out_specs,
        out_shape=[jax.ShapeDtypeStruct((lp, D_SB), BF16), jax.ShapeDtypeStruct((lp, D_SB), F32),
                   jax.ShapeDtypeStruct((lp, D_SB), F32)] + x_out_shapes,
        scratch_shapes=[pltpu.VMEM((ngrp, 2 * bq, 1), F32), pltpu.VMEM((ngrp, 2 * bq, LANES), F32)] + x_scratch,
        compiler_params=_params("arbitrary", "arbitrary"),
    )(qkv, qkv, qkv, carries, datt, tri, upper, *x_args)
    return outs[0], outs[1], outs[2], outs[3:]


def _conv_bwd(dc4, c1, ew, cw, lng, lnb, wpw2):
    lp = ew.shape[0]
    tm = ROW_TILE
    nt = lp // tm
    halo_per_tile = tm // CONV_PAD

    def body(dc4_ref, c1_ref, ew_ref, halo_ref, cw_ref, lng_ref, lnb_ref, w_ref,
             dga_ref, dgb_ref, c3_ref, dcw_ref, dcb_ref, dlng_ref, dlnb_ref, xbuf, dbuf, shifted, wacc):
        step = pl.program_id(0)

        @pl.when(step == 0)
        def _():
            wacc[...] = jnp.zeros_like(wacc)
            dcb_ref[...] = jnp.zeros_like(dcb_ref)
            dlng_ref[...] = jnp.zeros_like(dlng_ref)
            dlnb_ref[...] = jnp.zeros_like(dlnb_ref)
            dbuf[tm:tm + CONV_PAD, :] = jnp.zeros((CONV_PAD, D_CONV), F32)

        dc3 = _nt(dc4_ref[...], w_ref[...])
        xhat, rstd = _layer_norm_stats(c1_ref[...])
        c2 = xhat * lng_ref[...] + lnb_ref[...]
        c3, dsilu = _silu_fwd_bwd(c2)
        c3_ref[...] = c3.astype(BF16)
        dc2 = dc3 * dsilu
        dlng_ref[...] += jnp.sum(dc2 * xhat, axis=0, keepdims=True)
        dlnb_ref[...] += jnp.sum(dc2, axis=0, keepdims=True)
        dxhat = dc2 * lng_ref[...]
        dc1 = rstd * (dxhat - jnp.mean(dxhat, axis=-1, keepdims=True)
                      - xhat * jnp.mean(dxhat * xhat, axis=-1, keepdims=True))
        dcb_ref[...] += jnp.sum(dc1, axis=0, keepdims=True)
        dbuf[0:tm, :] = dc1

        ga = ew_ref[:, 0:512].astype(F32)
        sgb = _sigmoid(ew_ref[:, 512:1024].astype(F32))
        xbuf[CONV_PAD:CONV_PAD + tm, :] = ga * sgb
        first_tile = step == nt - 1
        halo = halo_ref[:, 0:512].astype(F32) * _sigmoid(halo_ref[:, 512:1024].astype(F32))
        xbuf[0:CONV_PAD, :] = jnp.where(first_tile, 0.0, halo)

        for cb in range(D_CONV // LANES):
            cs = slice(cb * LANES, (cb + 1) * LANES)
            _shifted_copies(dbuf, cs, shifted, tm)
            for r0 in range(0, tm, CONV_ROWS):
                rs = slice(r0, r0 + CONV_ROWS)
                dc0 = jnp.zeros((CONV_ROWS, LANES), F32)
                for j in range(CONV_WIDTH):
                    dc0 = dc0 + _weighted(cw_ref, j, cs, _shifted_rows(
                        shifted, r0 + CONV_WIDTH - 1 - j, CONV_ROWS))
                dga_ref[rs, cs] = (dc0 * sgb[rs, cs]).astype(BF16)
                dgb_ref[rs, cs] = (dc0 * ga[rs, cs] * sgb[rs, cs] * (1.0 - sgb[rs, cs])).astype(BF16)
            _shifted_copies(xbuf, cs, shifted, tm)
            for r0 in range(0, tm, CONV_ROWS):
                d1 = dbuf[r0:r0 + CONV_ROWS, cs]

                for j in range(CONV_WIDTH):
                    prod = d1 * _shifted_rows(shifted, r0 + CONV_PAD - (CONV_WIDTH - 1) + j, CONV_ROWS)
                    wacc[j * 8:(j + 1) * 8, cs] += jnp.sum(prod.reshape(CONV_ROWS // 8, 8, LANES), axis=0)
        dbuf[tm:tm + CONV_PAD, :] = dbuf[0:CONV_PAD, :]

        @pl.when(step == nt - 1)
        def _():
            dcw_ref[...] = jnp.sum(wacc[...].reshape(CONV_PAD, 8, D_CONV), axis=1)

    rev = lambda i: (nt - 1 - i, 0)
    row = pl.BlockSpec((tm, D_CONV), rev)
    vec = pl.BlockSpec((1, D_CONV), lambda i: (0, 0))
    hb = jax.ShapeDtypeStruct((lp, D_CONV), BF16)
    vs = jax.ShapeDtypeStruct((1, D_CONV), F32)
    return _pcall(
        body, name="conv_bwd", grid=(nt,),
        in_specs=[row, row, pl.BlockSpec((tm, 1024), rev),
                  pl.BlockSpec((CONV_PAD, 1024), lambda i: (jnp.maximum((nt - 1 - i) * halo_per_tile - 1, 0), 0)),
                  pl.BlockSpec((8 * CONV_PAD, D_CONV), lambda i: (0, 0)), vec, vec,
                  pl.BlockSpec((D_CONV, D_CONV), lambda i: (0, 0))],
        out_specs=[row, row, row, pl.BlockSpec((CONV_PAD, D_CONV), lambda i: (0, 0)), vec, vec, vec],
        out_shape=[hb, hb, hb, jax.ShapeDtypeStruct((CONV_PAD, D_CONV), F32), vs, vs, vs],
        scratch_shapes=[pltpu.VMEM((tm + CONV_PAD, D_CONV), F32), pltpu.VMEM((tm + CONV_PAD, D_CONV), F32),
                        pltpu.VMEM((8, tm + CONV_PAD, LANES), F32), pltpu.VMEM((8 * CONV_PAD, D_CONV), F32)],
        compiler_params=_params("arbitrary"),
    )(dc4, c1, ew, ew, cw, lng, lnb, wpw2)


def _inproj_bwd(dga, dgb, dcg, dq, dk, dv, dsg, h, g, w, dh_out, plan=None):
    lp, d = h.shape
    n = w.shape[1]
    tm = _mm_tile(lp)
    steps = lp // tm
    split, x_args, x_in_specs, x_out_specs, x_out_shapes, x_scratch = _hosted(plan, 4, 0)

    def body(dga_ref, dgb_ref, dcg_ref, dq_ref, dk_ref, dv_ref, dsg_ref, h_ref, g_ref, w_ref, dho_ref, *rest):
        parts = split(rest)
        dh_ref, dproj_ref, u_ref, dg_ref = parts[1]
        _host_begin(plan, parts, steps)

        @pl.when(pl.program_id(0) == 0)
        def _():
            dg_ref[...] = jnp.zeros_like(dg_ref)

        dproj_ref[:, 0:512] = dga_ref[...]
        dproj_ref[:, 512:1024] = dgb_ref[...]
        dproj_ref[:, 1024:1536] = dcg_ref[...]
        dproj_ref[:, 1536:2048] = dq_ref[...]
        dproj_ref[:, 2048:2560] = dk_ref[...].astype(BF16)
        dproj_ref[:, 2560:3072] = dv_ref[...].astype(BF16)
        dproj_ref[:, 3072:3584] = dsg_ref[...]
        du = _nt(dproj_ref[...], w_ref[...])
        x = h_ref[...]
        rstd = lax.rsqrt(jnp.mean(x * x, axis=-1, keepdims=True) + RMS_EPS)
        nrm = x * rstd
        u_ref[...] = (nrm * g_ref[...]).astype(BF16)
        dg_ref[...] += jnp.sum(du * nrm, axis=0, keepdims=True)
        dn = du * g_ref[...]
        dh_ref[...] = dho_ref[...] + rstd * (dn - nrm * jnp.mean(dn * nrm, axis=-1, keepdims=True))
        _host_end(plan, parts, steps)

    half = pl.BlockSpec((tm, 512), lambda i: (i, 0))
    full = pl.BlockSpec((tm, d), lambda i: (i, 0))
    outs = _pcall(
        body, name="inproj_bwd" if plan is None else "inproj_bwd_reduce", grid=(steps,),
        in_specs=[half] * 7 + [full, pl.BlockSpec((1, d), lambda i: (0, 0)),
                               pl.BlockSpec((d, n), lambda i: (0, 0)), full] + x_in_specs,
        out_specs=[full, pl.BlockSpec((tm, n), lambda i: (i, 0)), full,
                   pl.BlockSpec((1, d), lambda i: (0, 0))] + x_out_specs,
        out_shape=[jax.ShapeDtypeStruct((lp, d), F32), jax.ShapeDtypeStruct((lp, n), BF16),
                   jax.ShapeDtypeStruct((lp, d), BF16), jax.ShapeDtypeStruct((1, d), F32)] + x_out_shapes,
        scratch_shapes=x_scratch,
        compiler_params=_params("arbitrary"),
    )(dga, dgb, dcg, dq, dk, dv, dsg, h, g, w, dh_out, *x_args)
    return outs[0], outs[1], outs[2], outs[3], outs[4:]


def _row_split(m, parts):
    tm = m // parts
    assert tm * parts == m and tm % 16 == 0, (m, parts)
    return tm


def _matmul_tn(x, dy, tn, name):
    m, k = x.shape
    n = dy.shape[1]
    steps = 4 if m % 64 == 0 else 1
    tm = _row_split(m, steps)

    def body(x_ref, dy_ref, o_ref, acc_ref):
        r = pl.program_id(1)

        @pl.when(r == 0)
        def _():
            acc_ref[...] = jnp.zeros_like(acc_ref)

        acc_ref[...] += _tn(x_ref[...], dy_ref[...])

        @pl.when(r == steps - 1)
        def _():
            o_ref[...] = acc_ref[...].astype(BF16)

    return _pcall(
        body, name=name, grid=(n // tn, steps),
        in_specs=[pl.BlockSpec((tm, k), lambda j, r: (r, 0)), pl.BlockSpec((tm, tn), lambda j, r: (r, j))],
        out_specs=pl.BlockSpec((k, tn), lambda j, r: (0, j)),
        out_shape=jax.ShapeDtypeStruct((k, n), BF16),
        scratch_shapes=[pltpu.VMEM((k, tn), F32)],
        compiler_params=_params("parallel", "arbitrary"),
    )(x, dy)


def _local_step(h0, target_p, seq, vecs, depth, all_weights=None, w_in0=None, gather_w_in=None, gather_rest=None,
                reduce_layer=None):
    pre_g, post_g, conv_b, ln_g, ln_b, b_pw2 = vecs
    ar = jnp.arange(ATT_BLOCK)
    tri = (ar[:, None] > ar[None, :]).astype(BF16)
    upper = (ar[:, None] < ar[None, :]).astype(BF16)
    row = lambda a, l: a[l][None, :]

    hosted = all_weights is None
    weights = [None] * depth if hosted else list(all_weights)
    next_w_in, next_rest = w_in0, None
    saved = []
    h = h0
    for l in range(depth):
        more = hosted and l + 1 < depth
        if hosted:
            ew, qkv, rest = _inproj(h, row(pre_g, l), next_w_in, gather_rest(0) if l == 0 else None)
            w_in, (w_pw2, w_out, conv_w) = next_w_in, (rest if l == 0 else next_rest)
        else:
            w_in, w_pw2, w_out, conv_w = weights[l]
            ew, qkv, _ = _inproj(h, row(pre_g, l), w_in)
        conv_w = jnp.repeat(conv_w, 8, axis=0)
        weights[l] = (w_in, w_pw2, w_out, conv_w)
        c1, c4, c5, next_rest = _conv_fwd(ew, conv_w, row(conv_b, l), row(ln_g, l), row(ln_b, l), w_pw2,
                                          row(b_pw2, l), gather_rest(l + 1) if more else None)
        att, carries, gathered = _attn_fwd(qkv, tri, gather_w_in(l + 1) if more else None)
        if more:
            next_w_in = gathered[0]
        hn, cat, mixed = _outproj(c5, att, ew, h, w_out, row(post_g, l))
        saved.append((h, ew, qkv, c1, c4, att, carries, cat, mixed))
        h = hn

    dh, loss = _loss_head(h, target_p, seq)

    vec_grads = [None] * depth
    mat_grads = [None] * depth
    pending = None

    def pieces(w_in_pieces, rest_pieces):
        n_rest = len(rest_pieces) // 2
        return [w_in_pieces[0], *rest_pieces[:n_rest], w_in_pieces[1], *rest_pieces[n_rest:]]

    for l in reversed(range(depth)):
        w_in, w_pw2, w_out, conv_w = weights[l]
        h_in, ew, qkv, c1, c4, att, carries, cat, mixed = saved[l]
        dmix, datt, dsg, dc4, dcg, dpost, dbpw2 = _outproj_bwd(dh, mixed, row(post_g, l), w_out, att, ew, c4)
        dw_out = _matmul_tn(cat, dmix, 512, "dw_out")
        dq, dk, dv, landed = _attn_bwd(qkv, carries, datt, tri, upper, pending)
        if pending is not None:
            mat_grads[l + 1] = pieces(landed, rest_landed)
        dga, dgb, c3, dcw, dcb, dlng, dlnb = _conv_bwd(dc4, c1, ew, conv_w, row(ln_g, l), row(ln_b, l), w_pw2)
        dw_pw2 = _matmul_tn(c3, dc4, 512, "dw_pw2")
        rest_plan = None if reduce_layer is None else reduce_layer([dw_pw2, dw_out, dcw], 1)
        dh, dproj, u, dpre, rest_landed = _inproj_bwd(dga, dgb, dcg, dq, dk, dv, dsg, h_in, row(pre_g, l), w_in, dh,
                                                      rest_plan)
        dw_in = _matmul_tn(u, dproj, 1792, "dw_in")
        vec_grads[l] = (dpre[0], dpost[0], dcb[0], dlng[0], dlnb[0], dbpw2[0])
        if reduce_layer is None:
            mat_grads[l] = (dw_in, dw_pw2, dw_out, dcw)
        else:
            pending = reduce_layer([dw_in], 0)
    if pending is not None:
        mat_grads[0] = pieces(_run_exchange(pending, "reduce_grads"), rest_landed)

    vec_grads = [jnp.stack([g[k] for g in vec_grads]) for k in range(len(vecs))]
    return loss[0, 0], dh, vec_grads, mat_grads


N_CHIPS = 4
ANY = pl.BlockSpec(memory_space=pl.ANY)


def _chip_peers():
    x, y, c = lax.axis_index("x"), lax.axis_index("y"), lax.axis_index("c")
    return x, y, c, [(x, 1 - y), (1 - x, y), (1 - x, 1 - y)]


def _shard_slices(refs, dims, idx):
    out = []
    for ref, (axis, size) in zip(refs, dims):
        assert size % LANES == 0
        start = pl.multiple_of(idx * size, LANES)
        sl = [slice(None)] * len(ref.shape)
        sl[axis] = pl.ds(start, size)
        out.append(ref.at[tuple(sl)])
    return out


class _Exchange(NamedTuple):
    inputs: list
    out_shapes: list
    scratch: list
    start: Callable
    relay: Callable
    finish: Callable


def _run_exchange(plan, name):
    n_in, n_out = len(plan.inputs), len(plan.out_shapes)

    def body(*refs):
        parts = refs[:n_in], refs[n_in:n_in + n_out], refs[n_in + n_out:]
        plan.start(*parts)
        plan.relay(*parts)
        plan.finish(*parts)

    return _pcall(body, name=name, in_specs=[ANY] * n_in, out_specs=[ANY] * n_out, out_shape=plan.out_shapes,
                  scratch_shapes=plan.scratch)(*plan.inputs)


def _gather_plan(shards, dims):
    n = len(shards)
    full_shapes = []
    halves = []
    for s, (axis, size) in zip(shards, dims):
        shp = list(s.shape)
        shp[axis] = size * N_CHIPS
        full_shapes.append(jax.ShapeDtypeStruct(tuple(shp), s.dtype))
        tile_rows = 32 // s.dtype.itemsize
        assert s.shape[0] % (2 * tile_rows) == 0
        halves.append((s.shape[0] // 2, tile_rows))

    def half(refs, which):
        return [r.at[pl.ds(pl.multiple_of(which * h, t), h)] for r, (h, t) in zip(refs, halves)]

    def copies(srcs, outs, sems):
        send, recv, loc = sems
        x, y, c, peers = _chip_peers()
        sibling = (x, y, 1 - c)
        mine = _shard_slices(outs, dims, 2 * x + y)
        local = [pltpu.make_async_copy(s, d, loc.at[a]) for a, (s, d) in enumerate(zip(srcs, mine))]

        def remote(src, dst, slot, a, dev):
            return pltpu.make_async_remote_copy(src, dst, send.at[slot, a], recv.at[slot, a],
                                                device_id=dev, device_id_type=MESH)

        sends = [remote(s, d, k, a, (px, py, c))
                 for k, (px, py) in enumerate(peers) for a, (s, d) in enumerate(zip(half(srcs, c), half(mine, c)))]
        theirs = [_shard_slices(outs, dims, 2 * px + py) for px, py in peers]
        arrivals = [remote(s, d, k, a, (px, py, c))
                    for k, (px, py) in enumerate(peers)
                    for a, (s, d) in enumerate(zip(half(srcs, c), half(theirs[k], c)))]
        passed_on = [remote(d, d, 3 + k, a, sibling) for k in range(3) for a, d in enumerate(half(theirs[k], c))]
        from_sibling = [remote(d, d, 3 + k, a, sibling)
                        for k in range(3) for a, d in enumerate(half(theirs[k], 1 - c))]
        return local, sends, arrivals, passed_on, from_sibling

    def start(srcs, outs, sems):
        local, sends = copies(srcs, outs, sems)[:2]
        for cp in local + sends:
            cp.start()

    def relay(srcs, outs, sems):
        _, _, arrivals, passed_on, _ = copies(srcs, outs, sems)
        for arrived, onward in zip(arrivals, passed_on):
            arrived.wait_recv()
            onward.start()

    def finish(srcs, outs, sems):
        local, sends, _, passed_on, from_sibling = copies(srcs, outs, sems)
        for cp in from_sibling:
            cp.wait_recv()
        for cp in sends + passed_on:
            cp.wait_send()
        for cp in local:
            cp.wait()

    scratch = [pltpu.SemaphoreType.DMA((6, n)), pltpu.SemaphoreType.DMA((6, n)), pltpu.SemaphoreType.DMA((n,))]
    return _Exchange(list(shards), full_shapes, scratch, start, relay, finish)


def _reduce_plan(grads, dims):
    n = len(grads)
    piece_shapes = []
    for g, (axis, size) in zip(grads, dims):
        shp = list(g.shape)
        shp[axis] = size
        piece_shapes.append(jax.ShapeDtypeStruct((N_CHIPS,) + tuple(shp), g.dtype))

    def copies(srcs, outs, sems):
        mine, theirs = outs[:n], outs[n:]
        send, recv, loc = sems
        x, y, c, peers = _chip_peers()
        sibling = (x, y, 1 - c)
        own = _shard_slices(srcs, dims, 2 * x + y)

        def remote(src, dst, slot, a, dev):
            return pltpu.make_async_remote_copy(src, dst, send.at[slot, a], recv.at[slot, a],
                                                device_id=dev, device_id_type=MESH)

        local = [pltpu.make_async_copy(own[a], mine[a].at[3], loc.at[a]) for a in range(n)]
        to_sibling = [remote(own[a], theirs[a].at[3], 3, a, sibling) for a in range(n)]
        to_chips = [remote(src, mine[a].at[k], k, a, (px, py, c))
                    for k, (px, py) in enumerate(peers)
                    for a, src in enumerate(_shard_slices(srcs, dims, 2 * px + py))]
        passed_on = [remote(mine[a].at[k], theirs[a].at[k], 4 + k, a, sibling) for k in range(3) for a in range(n)]
        return local, to_sibling, to_chips, passed_on

    def start(srcs, outs, sems):
        local, to_sibling, to_chips, _ = copies(srcs, outs, sems)
        for cp in local + to_sibling + to_chips:
            cp.start()

    def relay(srcs, outs, sems):
        _, _, to_chips, passed_on = copies(srcs, outs, sems)
        for arrived, onward in zip(to_chips, passed_on):
            arrived.wait_recv()
            onward.start()

    def finish(srcs, outs, sems):
        local, to_sibling, to_chips, passed_on = copies(srcs, outs, sems)
        for cp in to_sibling + passed_on:
            cp.wait_recv()
        for cp in to_sibling + to_chips + passed_on:
            cp.wait_send()
        for cp in local:
            cp.wait()

    scratch = [pltpu.SemaphoreType.DMA((7, n)), pltpu.SemaphoreType.DMA((7, n)), pltpu.SemaphoreType.DMA((n,))]
    return _Exchange(list(grads), piece_shapes * 2, scratch, start, relay, finish)


def _allsum_small(pack):
    rows, cols = pack.shape
    ndev = 8

    def body(p_ref, o_ref, buf, send, recv):
        x, y, c = lax.axis_index("x"), lax.axis_index("y"), lax.axis_index("c")
        me = 4 * x + 2 * y + c
        buf[me] = p_ref[...]
        started = []
        for r in range(1, ndev):
            bx, by, bc = (r >> 2) & 1, (r >> 1) & 1, r & 1
            dev = (x ^ bx, y ^ by, c ^ bc)
            cp = pltpu.make_async_remote_copy(p_ref, buf.at[me], send.at[r], recv.at[r],
                                              device_id=dev, device_id_type=MESH)
            cp.start()
            started.append(cp)
        for r in range(1, ndev):
            pltpu.make_async_remote_copy(p_ref, buf.at[me ^ r], send.at[r], recv.at[r],
                                         device_id=(x, y, c), device_id_type=MESH).wait_recv()
        for cp in started:
            cp.wait_send()
        acc = buf[0]
        for d in range(1, ndev):
            acc = acc + buf[d]
        o_ref[...] = acc

    vm = pl.BlockSpec(memory_space=pltpu.VMEM)
    return _pcall(
        body, name="allsum_small", in_specs=[vm], out_specs=vm,
        out_shape=jax.ShapeDtypeStruct((rows, cols), F32),
        scratch_shapes=[pltpu.VMEM((ndev, rows, cols), F32), pltpu.SemaphoreType.DMA((ndev,)),
                        pltpu.SemaphoreType.DMA((ndev,))],
    )(pack)


def _adamw(parts, w, m, v, layer, prev, name):
    _, rows, cols = w.shape
    tr = ROW_TILE if rows % ROW_TILE == 0 else rows
    counts = [p.shape[0] for p in parts]
    n_parts = len(parts)
    n_prev = 0 if prev is None else 4

    def body(*refs):
        part_refs = refs[:n_parts]
        w_ref, m_ref, v_ref = refs[n_parts:n_parts + 3]
        g_ref, d_ref, nm_ref, nv_ref = refs[n_parts + 3 + n_prev:]
        g = None
        for p_ref, cnt in zip(part_refs, counts):
            s = p_ref[0].astype(F32)
            for k in range(1, cnt):
                s = s + p_ref[k].astype(F32)
            g = s if g is None else g + s
        m2 = ADAM_B1 * m_ref[0] + (1.0 - ADAM_B1) * g
        v2 = ADAM_B2 * v_ref[0] + (1.0 - ADAM_B2) * (g * g)
        m_hat = m2 / (1.0 - ADAM_B1 ** ADAM_STEP)
        v_hat = v2 / (1.0 - ADAM_B2 ** ADAM_STEP)
        g_ref[0] = g
        d_ref[0] = -ADAM_LR * (m_hat / (jnp.sqrt(v_hat) + ADAM_EPS) + ADAM_WD * w_ref[0])
        nm_ref[0] = m2
        nv_ref[0] = v2

    blk = pl.BlockSpec((1, tr, cols), lambda i: (layer, i, 0))
    shp = jax.ShapeDtypeStruct(w.shape, F32)
    return _pcall(
        body, name=name, grid=(rows // tr,),
        in_specs=[pl.BlockSpec((cnt, tr, cols), lambda i: (0, i, 0)) for cnt in counts] + [blk] * 3 + [ANY] * n_prev,
        out_specs=[blk] * 4, out_shape=[shp] * 4,
        input_output_aliases={n_parts + 3 + k: k for k in range(n_prev)},
        compiler_params=_params("parallel"),
    )(*parts, w, m, v, *(prev or ()))


def kernel(x, meta_tokens, pre_norm_g, post_norm_g, w_in, conv_w, conv_b, conv_ln_g, conv_ln_b, w_pw2, b_pw2, w_out, loss_target, m_meta_tokens, m_pre_norm_g, m_post_norm_g, m_w_in, m_conv_w, m_conv_b, m_conv_ln_g, m_conv_ln_b, m_w_pw2, m_b_pw2, m_w_out, v_meta_tokens, v_pre_norm_g, v_post_norm_g, v_w_in, v_conv_w, v_conv_b, v_conv_ln_g, v_conv_ln_b, v_w_pw2, v_b_pw2, v_w_out):
    seq, d = x.shape[1], x.shape[2]
    depth = w_in.shape[0]
    length = N_META + seq
    lp = -(-length // ATT_BLOCK) * ATT_BLOCK
    tap_pad = ((0, 0), (0, CONV_PAD - CONV_WIDTH), (0, 0))

    shards = (w_in.astype(BF16), w_pw2.astype(BF16), w_out.astype(BF16), jnp.pad(conv_w, tap_pad))
    dims = [(1, w_in.shape[2]), (0, w_pw2.shape[1]), (0, w_out.shape[1]), (1, conv_w.shape[2])]
    layer_shards = lambda l: [s[l] for s in shards]

    w_in0, meta_f = _run_exchange(_gather_plan([shards[0][0], meta_tokens], [dims[0], (1, meta_tokens.shape[1])]),
                                  "gather_weights")

    h0 = jnp.concatenate([meta_f, x[0], jnp.zeros((lp - length, d), F32)], axis=0)
    target_p = jnp.pad(loss_target[0], ((N_META, lp - length), (0, 0)))
    vecs = (pre_norm_g, post_norm_g, conv_b, conv_ln_g, conv_ln_b, b_pw2)
    loss, dh0, vec_grads, pieces = _local_step(
        h0, target_p, seq, vecs, depth, w_in0=w_in0,
        gather_w_in=lambda l: _gather_plan([shards[0][l]], dims[:1]),
        gather_rest=lambda l: _gather_plan(layer_shards(l)[1:], dims[1:]),
        reduce_layer=lambda grads, first: _reduce_plan(list(grads), dims[first:first + len(grads)]))

    def update(k, w, m, v, name):
        outs = None
        for l in reversed(range(depth)):
            outs = _adamw([pieces[l][k], pieces[l][4 + k]], w, m, v, l, outs, name)
        return outs

    up_w_in = update(0, w_in, m_w_in, v_w_in, "adamw_w_in")
    up_w_pw2 = update(1, w_pw2, m_w_pw2, v_w_pw2, "adamw_w_pw2")
    up_w_out = update(2, w_out, m_w_out, v_w_out, "adamw_w_out")
    up_conv_w = [o[:, :CONV_WIDTH] for o in update(3, jnp.pad(conv_w, tap_pad), jnp.pad(m_conv_w, tap_pad),
                                                   jnp.pad(v_conv_w, tap_pad, constant_values=1.0), "adamw_conv_w")]

    two = lambda a: a.reshape(-1, d)
    vec_rows = [two(g) for g in vec_grads]
    n_vec = sum(a.shape[0] for a in vec_rows)
    pack = jnp.concatenate(vec_rows + [dh0[:N_META], jnp.full((8, d), loss, F32)], axis=0)
    pack = jnp.pad(pack, ((0, -pack.shape[0] % 8), (0, 0)))
    tot = _allsum_small(pack)
    loss_all = tot[n_vec + N_META, 0]

    cat = lambda arrs: jnp.concatenate([two(t) for t in arrs], axis=0)[None]
    small_m = (m_pre_norm_g, m_post_norm_g, m_conv_b, m_conv_ln_g, m_conv_ln_b, m_b_pw2)
    small_v = (v_pre_norm_g, v_post_norm_g, v_conv_b, v_conv_ln_g, v_conv_ln_b, v_b_pw2)
    up_small = _adamw([tot[None, :n_vec]], cat(vecs), cat(small_m), cat(small_v), 0, None, "adamw_vectors")

    def unpack(o):
        res, r0 = [], 0
        for t in vecs:
            nrow = t.size // d
            res.append(o[0, r0:r0 + nrow].reshape(t.shape))
            r0 += nrow
        return res

    up_small = [unpack(o) for o in up_small]
    chip = 2 * lax.axis_index("x") + lax.axis_index("y")
    mcols = meta_tokens.shape[1]
    g_meta = lax.dynamic_slice_in_dim(tot[n_vec:n_vec + N_META], chip * mcols, mcols, axis=1)
    up_meta = [o[0] for o in _adamw([g_meta[None]], meta_tokens[None], m_meta_tokens[None], v_meta_tokens[None],
                                    0, None, "adamw_meta")]

    grad_x = dh0[N_META:length][None]
    outs = [loss_all, grad_x]
    for j in range(4):
        pre, post, cb, lg, lb, bp = up_small[j]
        outs += [up_meta[j], pre, post, up_w_in[j], up_conv_w[j], cb, lg, lb, up_w_pw2[j], bp, up_w_out[j]]
    return tuple(outs)
```

```python
from typing import Callable, NamedTuple

import jax
import jax.numpy as jnp
from jax import lax
from jax.experimental import pallas as pl
from jax.experimental.pallas import tpu as pltpu

F32 = jnp.float32
BF16 = jnp.bfloat16

N_META = 16
D_CONV = 512
D_SB = 512
HEAD_DIM = 64
CONV_WIDTH = 31
CONV_PAD = 32
CONV_ROWS = 128
CONV_ROWS_DX = 16
CONV_ROWS_DW = 16
RMS_EPS = 1e-6
LN_EPS = 1e-5
Q_SCALE = HEAD_DIM ** -0.5

ADAM_LR = 0.001
ADAM_B1 = 0.9
ADAM_B2 = 0.999
ADAM_EPS = 1e-08
ADAM_WD = 0.01
ADAM_STEP = 10

LANES = 128
ROW_TILE = 256
MM_TILE_MAX = 544
ATT_BLOCK = 256
ATT_PAIRS = 4
ATT_PAIRS_FWD = 4
VMEM_LIMIT = 56 * 1024 * 1024
EXP_ZERO = -104.0
COUNT_LANE = LANES - 1
RELAY_AT = 0.75

MESH = pl.DeviceIdType.MESH


def _pcall(body, **kw):
    return pl.pallas_call(body, **kw)


def _params(*sem):
    return pltpu.CompilerParams(dimension_semantics=sem, vmem_limit_bytes=VMEM_LIMIT)


def _sigmoid(x):
    return 1.0 / (1.0 + jnp.exp(-x))


def _silu_fwd_bwd(x):
    s = _sigmoid(x)
    return x * s, s * (1.0 + x * (1.0 - s))


def _nt(a, b):
    return lax.dot_general(a, b, (((1,), (1,)), ((), ())), preferred_element_type=F32)


def _tn(a, b):
    return lax.dot_general(a, b, (((0,), (0,)), ((), ())), preferred_element_type=F32)


def _nn(a, b):
    return jnp.dot(a, b, preferred_element_type=F32)


def _mm_tile(rows):
    return max(t for t in range(16, MM_TILE_MAX + 1, 16) if rows % t == 0)


def _host_begin(plan, parts, steps):
    if plan is not None:
        x_in, _, x_out, _, x_sems = parts

        @pl.when(pl.program_id(0) == 0)
        def _():
            plan.start(x_in, x_out, x_sems)

        @pl.when(pl.program_id(0) == int(RELAY_AT * steps))
        def _():
            plan.relay(x_in, x_out, x_sems)


def _host_end(plan, parts, steps):
    if plan is not None:
        x_in, _, x_out, _, x_sems = parts

        @pl.when(pl.program_id(0) == steps - 1)
        def _():
            plan.finish(x_in, x_out, x_sems)


def _inproj(h, g, w, plan=None):
    lp, d = h.shape
    n = w.shape[1]
    tm = _mm_tile(lp)
    steps = lp // tm
    split, x_args, x_in_specs, x_out_specs, x_out_shapes, x_scratch = _hosted(plan, 2, 0)

    def body(h_ref, g_ref, w_ref, *rest):
        parts = split(rest)
        ew_ref, qkv_ref = parts[1]
        _host_begin(plan, parts, steps)
        x = h_ref[...]
        rstd = lax.rsqrt(jnp.mean(x * x, axis=-1, keepdims=True) + RMS_EPS)
        u = ((x * rstd) * g_ref[...]).astype(BF16)
        p = _nn(u, w_ref[...])
        ew_ref[:, 0:1536] = p[:, 0:1536].astype(BF16)
        ew_ref[:, 1536:2048] = p[:, 3072:3584].astype(BF16)
        qkv_ref[:, 0:512] = (p[:, 1536:2048] * Q_SCALE).astype(BF16)
        qkv_ref[:, 512:1536] = p[:, 2048:3072].astype(BF16)
        _host_end(plan, parts, steps)

    outs = _pcall(
        body, name="inproj_fwd" if plan is None else "inproj_fwd_gather", grid=(steps,),
        in_specs=[pl.BlockSpec((tm, d), lambda i: (i, 0)),
                  pl.BlockSpec((1, d), lambda i: (0, 0)),
                  pl.BlockSpec((d, n), lambda i: (0, 0))] + x_in_specs,
        out_specs=[pl.BlockSpec((tm, 2048), lambda i: (i, 0)),
                   pl.BlockSpec((tm, 1536), lambda i: (i, 0))] + x_out_specs,
        out_shape=[jax.ShapeDtypeStruct((lp, 2048), BF16), jax.ShapeDtypeStruct((lp, 1536), BF16)] + x_out_shapes,
        scratch_shapes=x_scratch,
        compiler_params=_params("parallel" if plan is None else "arbitrary"),
    )(h, g, w, *x_args)
    return outs[0], outs[1], outs[2:]


def _layer_norm_stats(c1):
    mu = jnp.mean(c1, axis=-1, keepdims=True)
    xc = c1 - mu
    var = jnp.mean(xc * xc, axis=-1, keepdims=True)
    rstd = lax.rsqrt(var + LN_EPS)
    return xc * rstd, rstd


def _shifted_copies(window, cols, shifted, tm):
    shifted[0] = window[:, cols]
    rows = tm + CONV_PAD - 8
    for b in range(1, 8):
        shifted[b, 0:rows, :] = window[pl.ds(b, rows), cols]


def _shifted_rows(shifted, shift, tm):
    b = shift % 8
    return shifted[b, pl.ds(pl.multiple_of(shift - b, 8), tm), :]


def _weighted(cw8_ref, j, cols, rows):
    w8 = cw8_ref[pl.ds(pl.multiple_of(j * 8, 8), 8), cols]
    r = rows.shape[0]
    return (rows.reshape(r // 8, 8, LANES) * w8[None]).reshape(r, LANES)


def _conv_fwd(ew, cw, cb, lng, lnb, wpw2, bpw2, plan=None):
    lp = ew.shape[0]
    tm = ROW_TILE
    steps = lp // tm
    split, x_args, x_in_specs, x_out_specs, x_out_shapes, x_scratch = _hosted(plan, 3, 2)

    def body(ew_ref, cw_ref, cb_ref, lng_ref, lnb_ref, w_ref, b_ref, *rest):
        parts = split(rest)
        (c1_ref, c4_ref, c5_ref), (xbuf, shifted) = parts[1], parts[3]
        _host_begin(plan, parts, steps)
        @pl.when(pl.program_id(0) == 0)
        def _():
            xbuf[0:CONV_PAD, :] = jnp.zeros((CONV_PAD, D_CONV), F32)

        ga = ew_ref[:, 0:512].astype(F32)
        gb = ew_ref[:, 512:1024].astype(F32)
        cg = ew_ref[:, 1024:1536].astype(F32)
        xbuf[CONV_PAD:CONV_PAD + tm, :] = ga * _sigmoid(gb)
        for blk in range(D_CONV // LANES):
            cs = slice(blk * LANES, (blk + 1) * LANES)
            _shifted_copies(xbuf, cs, shifted, tm)
            for r0 in range(0, tm, CONV_ROWS):
                acc = jnp.zeros((CONV_ROWS, LANES), F32) + cb_ref[:, cs]
                for j in range(CONV_WIDTH):
                    acc = acc + _weighted(cw_ref, j, cs, _shifted_rows(
                        shifted, r0 + CONV_PAD - (CONV_WIDTH - 1) + j, CONV_ROWS))
                c1_ref[r0:r0 + CONV_ROWS, cs] = acc
        xbuf[0:CONV_PAD, :] = xbuf[tm:tm + CONV_PAD, :]
        xhat, _ = _layer_norm_stats(c1_ref[...])
        c2 = xhat * lng_ref[...] + lnb_ref[...]
        c3 = c2 * _sigmoid(c2)
        c4 = _nn(c3.astype(BF16), w_ref[...]) + b_ref[...]
        c4_ref[...] = c4
        c5_ref[...] = (c4 * (cg * _sigmoid(cg))).astype(BF16)
        _host_end(plan, parts, steps)

    vec = pl.BlockSpec((1, D_CONV), lambda i: (0, 0))
    row = pl.BlockSpec((tm, D_CONV), lambda i: (i, 0))
    outs = _pcall(
        body, name="conv_fwd" if plan is None else "conv_fwd_gather", grid=(steps,),
        in_specs=[pl.BlockSpec((tm, 1536), lambda i: (i, 0)),
                  pl.BlockSpec((8 * CONV_PAD, D_CONV), lambda i: (0, 0)),
                  vec, vec, vec,
                  pl.BlockSpec((D_CONV, D_CONV), lambda i: (0, 0)),
                  vec] + x_in_specs,
        out_specs=[row, row, row] + x_out_specs,
        out_shape=[jax.ShapeDtypeStruct((lp, D_CONV), F32), jax.ShapeDtypeStruct((lp, D_CONV), F32),
                   jax.ShapeDtypeStruct((lp, D_CONV), BF16)] + x_out_shapes,
        scratch_shapes=[pltpu.VMEM((tm + CONV_PAD, D_CONV), F32),
                        pltpu.VMEM((8, tm + CONV_PAD, LANES), F32)] + x_scratch,
        compiler_params=_params("arbitrary"),
    )(ew, cw, cb, lng, lnb, wpw2, bpw2, *x_args)
    return outs[0], outs[1], outs[2], outs[3:]


def _block_sums(x, m01):
    return _nn(x.astype(BF16), m01)


def _attn_masks():
    lane = lax.broadcasted_iota(jnp.int32, (1, LANES), 1)
    row = lax.broadcasted_iota(jnp.int32, (2 * ATT_BLOCK, ATT_BLOCK), 0)
    col = lax.broadcasted_iota(jnp.int32, (2 * ATT_BLOCK, ATT_BLOCK), 1)
    return lane < HEAD_DIM, col < (row & (ATT_BLOCK - 1))


def _stack_heads(x, first_head):
    zero = jnp.zeros_like(x)
    return jnp.concatenate([jnp.where(first_head, x, zero), jnp.where(first_head, zero, x)], axis=0)


def _unstack_heads(x2, first_head):
    rows = x2.shape[0] // 2
    return jnp.where(first_head, x2[:rows], x2[rows:])


def _hosted(plan, n_out, n_scratch):
    n_in = 0 if plan is None else len(plan.inputs)
    n_x = 0 if plan is None else len(plan.out_shapes)

    def split(rest):
        a, b, c = n_in + n_out, n_in + n_out + n_x, n_in + n_out + n_x + n_scratch
        return rest[:n_in], rest[n_in:a], rest[a:b], rest[b:c], rest[c:]

    if plan is None:
        return split, [], [], [], [], []
    return split, list(plan.inputs), [ANY] * n_in, [ANY] * n_x, list(plan.out_shapes), list(plan.scratch)


def _attn_fwd(qkv, tri, plan=None):
    lp = qkv.shape[0]
    bq = ATT_BLOCK
    ngrp = ATT_PAIRS_FWD
    nstep = D_SB // (LANES * ngrp)
    nq = lp // bq
    assert nq <= COUNT_LANE
    split, x_args, x_in_specs, x_out_specs, x_out_shapes, x_scratch = _hosted(plan, 2, 3)

    def body(q_ref, k_ref, v_ref, tri_ref, *rest):
        x_in, (o_ref, carry_ref), x_out, (c_s, acc_s, cm_s), x_sems = split(rest)
        i = pl.program_id(1)
        if plan is not None:
            @pl.when(jnp.logical_and(pl.program_id(0) == 0, i == 0))
            def _():
                plan.start(x_in, x_out, x_sems)

            @pl.when(jnp.logical_and(pl.program_id(0) == nstep - 1, i == int(RELAY_AT * nq)))
            def _():
                plan.relay(x_in, x_out, x_sems)

        first_head, vis = _attn_masks()
        lane = lax.broadcasted_iota(jnp.int32, (1, LANES), 1)
        cols = [slice(g * LANES, (g + 1) * LANES) for g in range(ngrp)]
        q2s = [_stack_heads(q_ref[:, cs], first_head) for cs in cols]
        tri_m = tri_ref[...]

        c_s[...] = jnp.zeros_like(c_s)
        acc_s[...] = jnp.zeros_like(acc_s)
        cm_s[...] = jnp.zeros_like(cm_s)

        def blocks(js, masks):
            offs = [pl.multiple_of(j * bq, bq) for j in js]
            work = [(g, b) for b in range(len(js)) for g in range(ngrp)]
            zs = {(g, b): _nt(q2s[g], k_ref[pl.ds(offs[b], bq), cols[g]]) for g, b in work}
            lss = {}
            for g, b in work:
                z = zs[g, b]
                ls = -(jnp.maximum(z, 0.0) + jnp.log(1.0 + jnp.exp(-jnp.abs(z))))
                lss[g, b] = ls if masks[b] is None else jnp.where(masks[b], ls, 0.0)
            tails = {gb: _block_sums(lss[gb], tri_m) for gb in work}
            probs = {}
            carry = [c_s[g] for g in range(ngrp)]
            saved = [cm_s[g] for g in range(ngrp)]
            for g, b in work:
                a = jnp.exp(zs[g, b] + lss[g, b] + tails[g, b] + carry[g])
                probs[g, b] = (a if masks[b] is None else jnp.where(masks[b], a, 0.0)).astype(BF16)
                saved[g] = jnp.where(lane == js[b], carry[g], saved[g])
                carry[g] = carry[g] + tails[g, b][:, 0:1] + lss[g, b][:, 0:1]
            top = None
            for g in range(ngrp):
                c_s[g] = carry[g]
                cm_s[g] = saved[g]
                acc = acc_s[g]
                for b in range(len(js)):
                    acc = acc + _nn(probs[g, b], v_ref[pl.ds(offs[b], bq), cols[g]])
                acc_s[g] = acc
                top = carry[g] if top is None else jnp.maximum(top, carry[g])
            return jnp.max(top) > EXP_ZERO

        alive = lax.cond(i > 0, lambda: blocks([i, i - 1], [vis, None]), lambda: blocks([i], [vis]))
        rest = jnp.maximum(i - 1, 0)

        def pair(carry):
            t, _ = carry
            j = i - 2 - 2 * t
            return t + 1, blocks([j, j - 1], [None, None])

        trips, alive = lax.while_loop(lambda ca: jnp.logical_and(ca[0] < rest // 2, ca[1]), pair, (0, alive))
        last = jnp.logical_and(jnp.logical_and(rest % 2 == 1, trips == rest // 2), alive)

        @pl.when(last)
        def _():
            blocks([0], [None])

        n_done = (jnp.minimum(i + 1, 2) + 2 * trips + last.astype(jnp.int32)).astype(F32)
        for g in range(ngrp):
            cmat = jnp.where(lane == COUNT_LANE, n_done, cm_s[g])
            carry_ref[:, 2 * g * LANES:(2 * g + 1) * LANES] = cmat[:bq]
            carry_ref[:, (2 * g + 1) * LANES:(2 * g + 2) * LANES] = cmat[bq:]
            o_ref[:, cols[g]] = _unstack_heads(acc_s[g], first_head)
        if plan is not None:
            @pl.when(jnp.logical_and(pl.program_id(0) == nstep - 1, i == nq - 1))
            def _():
                plan.finish(x_in, x_out, x_sems)

    width = ngrp * LANES
    outs = _pcall(
        body, name="attn_fwd" if plan is None else "attn_fwd_gather", grid=(nstep, nq),
        in_specs=[pl.BlockSpec((bq, width), lambda p, i: (i, p)),
                  pl.BlockSpec((lp, width), lambda p, i: (0, nstep + p)),
                  pl.BlockSpec((lp, width), lambda p, i: (0, 2 * nstep + p)),
                  pl.BlockSpec((bq, bq), lambda p, i: (0, 0))] + x_in_specs,
        out_specs=[pl.BlockSpec((bq, width), lambda p, i: (i, p)),
                   pl.BlockSpec((bq, 2 * width), lambda p, i: (i, p))] + x_out_specs,
        out_shape=[jax.ShapeDtypeStruct((lp, D_SB), F32), jax.ShapeDtypeStruct((lp, 2 * D_SB), F32)] + x_out_shapes,
        scratch_shapes=[pltpu.VMEM((ngrp, 2 * bq, 1), F32), pltpu.VMEM((ngrp, 2 * bq, LANES), F32),
                        pltpu.VMEM((ngrp, 2 * bq, LANES), F32)] + x_scratch,
        compiler_params=_params("arbitrary", "arbitrary"),
    )(qkv, qkv, qkv, tri, *x_args)
    return outs[0], outs[1], outs[2:]


def _outproj(c5, att, ew, h, w, g):
    lp, d = h.shape
    tm = _mm_tile(lp)

    def body(c5_ref, att_ref, sg_ref, h_ref, w_ref, g_ref, hn_ref, cat_ref, mix_ref):
        sg = sg_ref[...].astype(F32)
        s = att_ref[...] * (sg * _sigmoid(sg))
        cat_ref[:, 0:D_CONV] = c5_ref[...]
        cat_ref[:, D_CONV:] = s.astype(BF16)
        mixed = _nn(cat_ref[...], w_ref[...])
        mix_ref[...] = mixed
        rstd = lax.rsqrt(jnp.mean(mixed * mixed, axis=-1, keepdims=True) + RMS_EPS)
        hn_ref[...] = h_ref[...] + (mixed * rstd) * g_ref[...]

    half = pl.BlockSpec((tm, 512), lambda i: (i, 0))
    full = pl.BlockSpec((tm, d), lambda i: (i, 0))
    return _pcall(
        body, name="outproj_fwd", grid=(lp // tm,),
        in_specs=[half, half, pl.BlockSpec((tm, 512), lambda i: (i, 3)), full,
                  pl.BlockSpec((d, d), lambda i: (0, 0)), pl.BlockSpec((1, d), lambda i: (0, 0))],
        out_specs=[full, full, full],
        out_shape=[jax.ShapeDtypeStruct((lp, d), F32), jax.ShapeDtypeStruct((lp, d), BF16),
                   jax.ShapeDtypeStruct((lp, d), F32)],
        compiler_params=_params("parallel"),
    )(c5, att, ew, h, w, g)


def _loss_head(h, target, seq):
    lp, d = h.shape
    tm = ROW_TILE

    def body(h_ref, t_ref, dh_ref, loss_ref):
        i = pl.program_id(0)

        @pl.when(i == 0)
        def _():
            loss_ref[...] = jnp.zeros_like(loss_ref)

        row = i * tm + lax.broadcasted_iota(jnp.int32, (tm, 1), 0)
        real = jnp.logical_and(row >= N_META, row < N_META + seq)
        diff = jnp.where(real, h_ref[...] - t_ref[...], 0.0)
        dh_ref[...] = diff * (1.0 / d)
        loss_ref[...] += 0.5 * jnp.sum(jnp.sum(diff * diff, axis=-1, keepdims=True) * (1.0 / d))

    full = pl.BlockSpec((tm, d), lambda i: (i, 0))
    return _pcall(
        body, name="loss_head", grid=(lp // tm,),
        in_specs=[full, full],
        out_specs=[full, pl.BlockSpec((8, LANES), lambda i: (0, 0))],
        out_shape=[jax.ShapeDtypeStruct((lp, d), F32), jax.ShapeDtypeStruct((8, LANES), F32)],
        compiler_params=_params("arbitrary"),
    )(h, target)


def _outproj_bwd(dh, mixed, g, w, att, ew, c4):
    lp, d = dh.shape
    tm = _mm_tile(lp)

    def body(dh_ref, mix_ref, g_ref, w_ref, att_ref, cg_ref, sg_ref, c4_ref,
             dmix_ref, datt_ref, dsg_ref, dc4_ref, dcg_ref, dg_ref, db_ref):
        @pl.when(pl.program_id(0) == 0)
        def _():
            dg_ref[...] = jnp.zeros_like(dg_ref)
            db_ref[...] = jnp.zeros_like(db_ref)

        mixed = mix_ref[...]
        dhv = dh_ref[...]
        rstd = lax.rsqrt(jnp.mean(mixed * mixed, axis=-1, keepdims=True) + RMS_EPS)
        n = mixed * rstd
        dg_ref[...] += jnp.sum(dhv * n, axis=0, keepdims=True)
        dn = dhv * g_ref[...]
        dmix = (rstd * (dn - n * jnp.mean(dn * n, axis=-1, keepdims=True))).astype(BF16)
        dmix_ref[...] = dmix
        dcat = _nt(dmix, w_ref[...])
        dc5 = dcat[:, 0:D_CONV]
        ds = dcat[:, D_CONV:]
        silu_sg, dsilu_sg = _silu_fwd_bwd(sg_ref[...].astype(F32))
        datt_ref[...] = (ds * silu_sg).astype(BF16)
        dsg_ref[...] = (ds * att_ref[...] * dsilu_sg).astype(BF16)
        silu_cg, dsilu_cg = _silu_fwd_bwd(cg_ref[...].astype(F32))
        dc4 = dc5 * silu_cg
        db_ref[...] += jnp.sum(dc4, axis=0, keepdims=True)
        dc4_ref[...] = dc4.astype(BF16)
        dcg_ref[...] = (dc5 * c4_ref[...] * dsilu_cg).astype(BF16)

    half = pl.BlockSpec((tm, 512), lambda i: (i, 0))
    full = pl.BlockSpec((tm, d), lambda i: (i, 0))
    hb = jax.ShapeDtypeStruct((lp, 512), BF16)
    return _pcall(
        body, name="outproj_bwd", grid=(lp // tm,),
        in_specs=[full, full, pl.BlockSpec((1, d), lambda i: (0, 0)), pl.BlockSpec((d, d), lambda i: (0, 0)),
                  half, pl.BlockSpec((tm, 512), lambda i: (i, 2)), pl.BlockSpec((tm, 512), lambda i: (i, 3)), half],
        out_specs=[full, half, half, half, half,
                   pl.BlockSpec((1, d), lambda i: (0, 0)), pl.BlockSpec((1, 512), lambda i: (0, 0))],
        out_shape=[jax.ShapeDtypeStruct((lp, d), BF16), hb, hb, hb, hb,
                   jax.ShapeDtypeStruct((1, d), F32), jax.ShapeDtypeStruct((1, 512), F32)],
        compiler_params=_params("arbitrary"),
    )(dh, mixed, g, w, att, ew, ew, c4)


def _attn_bwd(qkv, carries, datt, tri, upper, plan=None):
    lp = qkv.shape[0]
    bq = ATT_BLOCK
    ngrp = ATT_PAIRS
    nstep = D_SB // (LANES * ngrp)
    nq = lp // bq
    split, x_args, x_in_specs, x_out_specs, x_out_shapes, x_scratch = _hosted(plan, 3, 2)

    def body(q_ref, k_ref, v_ref, carry_ref, do_ref, tri_ref, upper_ref, *rest):
        x_in, (dq_ref, dk_ref, dv_ref), x_out, (run_s, dq_s), x_sems = split(rest)
        i = pl.program_id(1)
        if plan is not None:
            @pl.when(jnp.logical_and(pl.program_id(0) == 0, i == 0))
            def _():
                plan.start(x_in, x_out, x_sems)

            @pl.when(jnp.logical_and(pl.program_id(0) == nstep - 1, i == max(nq - 2, 0)))
            def _():
                plan.relay(x_in, x_out, x_sems)

        @pl.when(i == 0)
        def _():
            dk_ref[...] = jnp.zeros_like(dk_ref)
            dv_ref[...] = jnp.zeros_like(dv_ref)

        first_head, vis = _attn_masks()
        lane = lax.broadcasted_iota(jnp.int32, (1, LANES), 1)
        cols = [slice(g * LANES, (g + 1) * LANES) for g in range(ngrp)]
        q2s = [_stack_heads(q_ref[:, cs], first_head) for cs in cols]
        do2s = [_stack_heads(do_ref[:, cs], first_head) for cs in cols]
        cmats = [jnp.concatenate([carry_ref[:, 2 * g * LANES:(2 * g + 1) * LANES],
                                  carry_ref[:, (2 * g + 1) * LANES:(2 * g + 2) * LANES]], axis=0)
                 for g in range(ngrp)]
        tri_m = tri_ref[...]
        upper_m = upper_ref[...]

        def blocks(js, masks):
            offs = [pl.multiple_of(j * bq, bq) for j in js]
            work = [(g, b) for b in range(len(js)) for g in range(ngrp)]
            zs = {(g, b): _nt(q2s[g], k_ref[pl.ds(offs[b], bq), cols[g]]) for g, b in work}
            lss = {}
            for g, b in work:
                z = zs[g, b]
                ls = -(jnp.maximum(z, 0.0) + jnp.log(1.0 + jnp.exp(-jnp.abs(z))))
                lss[g, b] = ls if masks[b] is None else jnp.where(masks[b], ls, 0.0)
            tails = {gb: _block_sums(lss[gb], tri_m) for gb in work}
            das = {(g, b): _nt(do2s[g], v_ref[pl.ds(offs[b], bq), cols[g]]) for g, b in work}
            probs, des = {}, {}
            for g, b in work:
                c = jnp.sum(jnp.where(lane == js[b], cmats[g], 0.0), axis=-1, keepdims=True)
                a = jnp.exp(zs[g, b] + lss[g, b] + tails[g, b] + c)
                a = a if masks[b] is None else jnp.where(masks[b], a, 0.0)
                probs[g, b] = a.astype(BF16)
                des[g, b] = das[g, b] * a
            prefixes = {gb: _block_sums(des[gb], upper_m) for gb in work}
            runs = [run_s[g] for g in range(ngrp)]
            dzs = {}
            for g, b in work:
                beta = jnp.exp(zs[g, b] + lss[g, b])
                dz = des[g, b] - beta * (des[g, b] + runs[g] + prefixes[g, b])
                dzs[g, b] = (dz if masks[b] is None else jnp.where(masks[b], dz, 0.0)).astype(BF16)
                runs[g] = runs[g] + prefixes[g, b][:, bq - 1:bq] + des[g, b][:, bq - 1:bq]
            for g in range(ngrp):
                run_s[g] = runs[g]
                dq = dq_s[g]
                for b in range(len(js)):
                    rows = pl.ds(offs[b], bq)
                    dq = dq + _nn(dzs[g, b], k_ref[rows, cols[g]])
                    dk_ref[rows, cols[g]] += _tn(dzs[g, b], q2s[g])
                    dv_ref[rows, cols[g]] += _tn(probs[g, b], do2s[g])
                dq_s[g] = dq

        n_done = jnp.max(carry_ref[:, COUNT_LANE:COUNT_LANE + 1]).astype(jnp.int32)
        n_done = jnp.clip(n_done, 1, i + 1)
        before = jnp.maximum(n_done - 2, 0)
        j0 = i - n_done + 1
        odd = before % 2
        run_s[...] = jnp.zeros_like(run_s)
        dq_s[...] = jnp.zeros_like(dq_s)

        @pl.when(odd == 1)
        def _():
            blocks([j0], [None])

        @pl.loop(0, before // 2)
        def _(t):
            blocks([j0 + odd + 2 * t, j0 + odd + 2 * t + 1], [None, None])

        @pl.when(n_done > 1)
        def _():
            blocks([i - 1, i], [None, vis])

        @pl.when(n_done <= 1)
        def _():
            blocks([i], [vis])

        for g in range(ngrp):
            dq_ref[:, cols[g]] = (_unstack_heads(dq_s[g], first_head) * Q_SCALE).astype(BF16)
        if plan is not None:
            @pl.when(jnp.logical_and(pl.program_id(0) == nstep - 1, i == nq - 1))
            def _():
                plan.finish(x_in, x_out, x_sems)

    width = ngrp * LANES
    once = pl.Buffered(1)
    qb = pl.BlockSpec((bq, width), lambda p, i: (i, p))
    colb = pl.BlockSpec((lp, width), lambda p, i: (0, p), pipeline_mode=once)
    sq = pl.BlockSpec((bq, bq), lambda p, i: (0, 0))
    outs = _pcall(
        body, name="attn_bwd" if plan is None else "attn_bwd_reduce", grid=(nstep, nq),
        in_specs=[qb,
                  pl.BlockSpec((lp, width), lambda p, i: (0, nstep + p), pipeline_mode=once),
                  pl.BlockSpec((lp, width), lambda p, i: (0, 2 * nstep + p), pipeline_mode=once),
                  pl.BlockSpec((bq, 2 * width), lambda p, i: (i, p)), qb, sq, sq] + x_in_specs,
        out_specs=[qb, colb, colb] + x_out_specs,
        out_shape=[jax.ShapeDtypeStruct((lp, D_SB), BF16), jax.ShapeDtypeStruct((lp, D_SB), F32),
                   jax.ShapeDtypeStruct((lp, D_SB), F32)] + x_out_shapes,
        scratch_shapes=[pltpu.VMEM((ngrp, 2 * bq, 1), F32), pltpu.VMEM((ngrp, 2 * bq, LANES), F32)] + x_scratch,
        compiler_params=_params("arbitrary", "arbitrary"),
    )(qkv, qkv, qkv, carries, datt, tri, upper, *x_args)
    return outs[0], outs[1], outs[2], outs[3:]


def _conv_bwd(dc4, c1, ew, cw, lng, lnb, wpw2):
    lp = ew.shape[0]
    tm = ROW_TILE
    nt = lp // tm
    halo_per_tile = tm // CONV_PAD

    def body(dc4_ref, c1_ref, ew_ref, halo_ref, cw_ref, lng_ref, lnb_ref, w_ref,
             dga_ref, dgb_ref, c3_ref, dcw_ref, dcb_ref, dlng_ref, dlnb_ref, xbuf, dbuf, shifted, wacc):
        step = pl.program_id(0)

        @pl.when(step == 0)
        def _():
            wacc[...] = jnp.zeros_like(wacc)
            dcb_ref[...] = jnp.zeros_like(dcb_ref)
            dlng_ref[...] = jnp.zeros_like(dlng_ref)
            dlnb_ref[...] = jnp.zeros_like(dlnb_ref)
            dbuf[tm:tm + CONV_PAD, :] = jnp.zeros((CONV_PAD, D_CONV), F32)

        dc3 = _nt(dc4_ref[...], w_ref[...])
        xhat, rstd = _layer_norm_stats(c1_ref[...])
        c2 = xhat * lng_ref[...] + lnb_ref[...]
        c3, dsilu = _silu_fwd_bwd(c2)
        c3_ref[...] = c3.astype(BF16)
        dc2 = dc3 * dsilu
        dlng_ref[...] += jnp.sum(dc2 * xhat, axis=0, keepdims=True)
        dlnb_ref[...] += jnp.sum(dc2, axis=0, keepdims=True)
        dxhat = dc2 * lng_ref[...]
        dc1 = rstd * (dxhat - jnp.mean(dxhat, axis=-1, keepdims=True)
                      - xhat * jnp.mean(dxhat * xhat, axis=-1, keepdims=True))
        dcb_ref[...] += jnp.sum(dc1, axis=0, keepdims=True)
        dbuf[0:tm, :] = dc1

        ga = ew_ref[:, 0:512].astype(F32)
        sgb = _sigmoid(ew_ref[:, 512:1024].astype(F32))
        xbuf[CONV_PAD:CONV_PAD + tm, :] = ga * sgb
        first_tile = step == nt - 1
        halo = halo_ref[:, 0:512].astype(F32) * _sigmoid(halo_ref[:, 512:1024].astype(F32))
        xbuf[0:CONV_PAD, :] = jnp.where(first_tile, 0.0, halo)

        for cb in range(D_CONV // LANES):
            cs = slice(cb * LANES, (cb + 1) * LANES)
            _shifted_copies(dbuf, cs, shifted, tm)
            for r0 in range(0, tm, CONV_ROWS_DX):
                rs = slice(r0, r0 + CONV_ROWS_DX)
                dc0 = jnp.zeros((CONV_ROWS_DX, LANES), F32)
                for j in range(CONV_WIDTH):
                    dc0 = dc0 + _weighted(cw_ref, j, cs, _shifted_rows(
                        shifted, r0 + CONV_WIDTH - 1 - j, CONV_ROWS_DX))
                dga_ref[rs, cs] = (dc0 * sgb[rs, cs]).astype(BF16)
                dgb_ref[rs, cs] = (dc0 * ga[rs, cs] * sgb[rs, cs] * (1.0 - sgb[rs, cs])).astype(BF16)
            _shifted_copies(xbuf, cs, shifted, tm)
            for r0 in range(0, tm, CONV_ROWS_DW):
                d1 = dbuf[r0:r0 + CONV_ROWS_DW, cs]

                for j in range(CONV_WIDTH):
                    prod = d1 * _shifted_rows(shifted, r0 + CONV_PAD - (CONV_WIDTH - 1) + j, CONV_ROWS_DW)
                    wacc[j * 8:(j + 1) * 8, cs] += jnp.sum(prod.reshape(CONV_ROWS_DW // 8, 8, LANES), axis=0)
        dbuf[tm:tm + CONV_PAD, :] = dbuf[0:CONV_PAD, :]

        @pl.when(step == nt - 1)
        def _():
            dcw_ref[...] = jnp.sum(wacc[...].reshape(CONV_PAD, 8, D_CONV), axis=1)

    rev = lambda i: (nt - 1 - i, 0)
    row = pl.BlockSpec((tm, D_CONV), rev)
    vec = pl.BlockSpec((1, D_CONV), lambda i: (0, 0))
    hb = jax.ShapeDtypeStruct((lp, D_CONV), BF16)
    vs = jax.ShapeDtypeStruct((1, D_CONV), F32)
    return _pcall(
        body, name="conv_bwd", grid=(nt,),
        in_specs=[row, row, pl.BlockSpec((tm, 1024), rev),
                  pl.BlockSpec((CONV_PAD, 1024), lambda i: (jnp.maximum((nt - 1 - i) * halo_per_tile - 1, 0), 0)),
                  pl.BlockSpec((8 * CONV_PAD, D_CONV), lambda i: (0, 0)), vec, vec,
                  pl.BlockSpec((D_CONV, D_CONV), lambda i: (0, 0))],
        out_specs=[row, row, row, pl.BlockSpec((CONV_PAD, D_CONV), lambda i: (0, 0)), vec, vec, vec],
        out_shape=[hb, hb, hb, jax.ShapeDtypeStruct((CONV_PAD, D_CONV), F32), vs, vs, vs],
        scratch_shapes=[pltpu.VMEM((tm + CONV_PAD, D_CONV), F32), pltpu.VMEM((tm + CONV_PAD, D_CONV), F32),
                        pltpu.VMEM((8, tm + CONV_PAD, LANES), F32), pltpu.VMEM((8 * CONV_PAD, D_CONV), F32)],
        compiler_params=_params("arbitrary"),
    )(dc4, c1, ew, ew, cw, lng, lnb, wpw2)


def _inproj_bwd(dga, dgb, dcg, dq, dk, dv, dsg, h, g, w, dh_out, plan=None):
    lp, d = h.shape
    n = w.shape[1]
    tm = _mm_tile(lp)
    steps = lp // tm
    split, x_args, x_in_specs, x_out_specs, x_out_shapes, x_scratch = _hosted(plan, 4, 0)

    def body(dga_ref, dgb_ref, dcg_ref, dq_ref, dk_ref, dv_ref, dsg_ref, h_ref, g_ref, w_ref, dho_ref, *rest):
        parts = split(rest)
        dh_ref, dproj_ref, u_ref, dg_ref = parts[1]
        _host_begin(plan, parts, steps)

        @pl.when(pl.program_id(0) == 0)
        def _():
            dg_ref[...] = jnp.zeros_like(dg_ref)

        dproj_ref[:, 0:512] = dga_ref[...]
        dproj_ref[:, 512:1024] = dgb_ref[...]
        dproj_ref[:, 1024:1536] = dcg_ref[...]
        dproj_ref[:, 1536:2048] = dq_ref[...]
        dproj_ref[:, 2048:2560] = dk_ref[...].astype(BF16)
        dproj_ref[:, 2560:3072] = dv_ref[...].astype(BF16)
        dproj_ref[:, 3072:3584] = dsg_ref[...]
        du = _nt(dproj_ref[...], w_ref[...])
        x = h_ref[...]
        rstd = lax.rsqrt(jnp.mean(x * x, axis=-1, keepdims=True) + RMS_EPS)
        nrm = x * rstd
        u_ref[...] = (nrm * g_ref[...]).astype(BF16)
        dg_ref[...] += jnp.sum(du * nrm, axis=0, keepdims=True)
        dn = du * g_ref[...]
        dh_ref[...] = dho_ref[...] + rstd * (dn - nrm * jnp.mean(dn * nrm, axis=-1, keepdims=True))
        _host_end(plan, parts, steps)

    half = pl.BlockSpec((tm, 512), lambda i: (i, 0))
    full = pl.BlockSpec((tm, d), lambda i: (i, 0))
    outs = _pcall(
        body, name="inproj_bwd" if plan is None else "inproj_bwd_reduce", grid=(steps,),
        in_specs=[half] * 7 + [full, pl.BlockSpec((1, d), lambda i: (0, 0)),
                               pl.BlockSpec((d, n), lambda i: (0, 0)), full] + x_in_specs,
        out_specs=[full, pl.BlockSpec((tm, n), lambda i: (i, 0)), full,
                   pl.BlockSpec((1, d), lambda i: (0, 0))] + x_out_specs,
        out_shape=[jax.ShapeDtypeStruct((lp, d), F32), jax.ShapeDtypeStruct((lp, n), BF16),
                   jax.ShapeDtypeStruct((lp, d), BF16), jax.ShapeDtypeStruct((1, d), F32)] + x_out_shapes,
        scratch_shapes=x_scratch,
        compiler_params=_params("arbitrary"),
    )(dga, dgb, dcg, dq, dk, dv, dsg, h, g, w, dh_out, *x_args)
    return outs[0], outs[1], outs[2], outs[3], outs[4:]


def _row_split(m, parts):
    tm = m // parts
    assert tm * parts == m and tm % 16 == 0, (m, parts)
    return tm


def _matmul_tn(x, dy, tn, name):
    m, k = x.shape
    n = dy.shape[1]
    steps = 4 if m % 64 == 0 else 1
    tm = _row_split(m, steps)

    def body(x_ref, dy_ref, o_ref, acc_ref):
        r = pl.program_id(1)

        @pl.when(r == 0)
        def _():
            acc_ref[...] = jnp.zeros_like(acc_ref)

        acc_ref[...] += _tn(x_ref[...], dy_ref[...])

        @pl.when(r == steps - 1)
        def _():
            o_ref[...] = acc_ref[...].astype(BF16)

    return _pcall(
        body, name=name, grid=(n // tn, steps),
        in_specs=[pl.BlockSpec((tm, k), lambda j, r: (r, 0)), pl.BlockSpec((tm, tn), lambda j, r: (r, j))],
        out_specs=pl.BlockSpec((k, tn), lambda j, r: (0, j)),
        out_shape=jax.ShapeDtypeStruct((k, n), BF16),
        scratch_shapes=[pltpu.VMEM((k, tn), F32)],
        compiler_params=_params("parallel", "arbitrary"),
    )(x, dy)


def _local_step(h0, target_p, seq, vecs, depth, all_weights=None, w_in0=None, gather_w_in=None, gather_rest=None,
                reduce_layer=None):
    pre_g, post_g, conv_b, ln_g, ln_b, b_pw2 = vecs
    ar = jnp.arange(ATT_BLOCK)
    tri = (ar[:, None] > ar[None, :]).astype(BF16)
    upper = (ar[:, None] < ar[None, :]).astype(BF16)
    row = lambda a, l: a[l][None, :]

    hosted = all_weights is None
    weights = [None] * depth if hosted else list(all_weights)
    next_w_in, next_rest = w_in0, None
    saved = []
    h = h0
    for l in range(depth):
        more = hosted and l + 1 < depth
        if hosted:
            ew, qkv, rest = _inproj(h, row(pre_g, l), next_w_in, gather_rest(0) if l == 0 else None)
            w_in, (w_pw2, w_out, conv_w) = next_w_in, (rest if l == 0 else next_rest)
        else:
            w_in, w_pw2, w_out, conv_w = weights[l]
            ew, qkv, _ = _inproj(h, row(pre_g, l), w_in)
        conv_w = jnp.repeat(conv_w, 8, axis=0)
        weights[l] = (w_in, w_pw2, w_out, conv_w)
        c1, c4, c5, next_rest = _conv_fwd(ew, conv_w, row(conv_b, l), row(ln_g, l), row(ln_b, l), w_pw2,
                                          row(b_pw2, l), gather_rest(l + 1) if more else None)
        att, carries, gathered = _attn_fwd(qkv, tri, gather_w_in(l + 1) if more else None)
        if more:
            next_w_in = gathered[0]
        hn, cat, mixed = _outproj(c5, att, ew, h, w_out, row(post_g, l))
        saved.append((h, ew, qkv, c1, c4, att, carries, cat, mixed))
        h = hn

    dh, loss = _loss_head(h, target_p, seq)

    vec_grads = [None] * depth
    mat_grads = [None] * depth
    pending = None

    def pieces(w_in_pieces, rest_pieces):
        n_rest = len(rest_pieces) // 2
        return [w_in_pieces[0], *rest_pieces[:n_rest], w_in_pieces[1], *rest_pieces[n_rest:]]

    for l in reversed(range(depth)):
        w_in, w_pw2, w_out, conv_w = weights[l]
        h_in, ew, qkv, c1, c4, att, carries, cat, mixed = saved[l]
        dmix, datt, dsg, dc4, dcg, dpost, dbpw2 = _outproj_bwd(dh, mixed, row(post_g, l), w_out, att, ew, c4)
        dw_out = _matmul_tn(cat, dmix, 512, "dw_out")
        dq, dk, dv, landed = _attn_bwd(qkv, carries, datt, tri, upper, pending)
        if pending is not None:
            mat_grads[l + 1] = pieces(landed, rest_landed)
        dga, dgb, c3, dcw, dcb, dlng, dlnb = _conv_bwd(dc4, c1, ew, conv_w, row(ln_g, l), row(ln_b, l), w_pw2)
        dw_pw2 = _matmul_tn(c3, dc4, 512, "dw_pw2")
        rest_plan = None if reduce_layer is None else reduce_layer([dw_pw2, dw_out, dcw], 1)
        dh, dproj, u, dpre, rest_landed = _inproj_bwd(dga, dgb, dcg, dq, dk, dv, dsg, h_in, row(pre_g, l), w_in, dh,
                                                      rest_plan)
        dw_in = _matmul_tn(u, dproj, 1792, "dw_in")
        vec_grads[l] = (dpre[0], dpost[0], dcb[0], dlng[0], dlnb[0], dbpw2[0])
        if reduce_layer is None:
            mat_grads[l] = (dw_in, dw_pw2, dw_out, dcw)
        else:
            pending = reduce_layer([dw_in], 0)
    if pending is not None:
        mat_grads[0] = pieces(_run_exchange(pending, "reduce_grads"), rest_landed)

    vec_grads = [jnp.stack([g[k] for g in vec_grads]) for k in range(len(vecs))]
    return loss[0, 0], dh, vec_grads, mat_grads


N_CHIPS = 4
ANY = pl.BlockSpec(memory_space=pl.ANY)


def _chip_peers():
    x, y, c = lax.axis_index("x"), lax.axis_index("y"), lax.axis_index("c")
    return x, y, c, [(x, 1 - y), (1 - x, y), (1 - x, 1 - y)]


def _shard_slices(refs, dims, idx):
    out = []
    for ref, (axis, size) in zip(refs, dims):
        assert size % LANES == 0
        start = pl.multiple_of(idx * size, LANES)
        sl = [slice(None)] * len(ref.shape)
        sl[axis] = pl.ds(start, size)
        out.append(ref.at[tuple(sl)])
    return out


class _Exchange(NamedTuple):
    inputs: list
    out_shapes: list
    scratch: list
    start: Callable
    relay: Callable
    finish: Callable


def _run_exchange(plan, name):
    n_in, n_out = len(plan.inputs), len(plan.out_shapes)

    def body(*refs):
        parts = refs[:n_in], refs[n_in:n_in + n_out], refs[n_in + n_out:]
        plan.start(*parts)
        plan.relay(*parts)
        plan.finish(*parts)

    return _pcall(body, name=name, in_specs=[ANY] * n_in, out_specs=[ANY] * n_out, out_shape=plan.out_shapes,
                  scratch_shapes=plan.scratch)(*plan.inputs)


def _gather_plan(shards, dims):
    n = len(shards)
    full_shapes = []
    halves = []
    for s, (axis, size) in zip(shards, dims):
        shp = list(s.shape)
        shp[axis] = size * N_CHIPS
        full_shapes.append(jax.ShapeDtypeStruct(tuple(shp), s.dtype))
        tile_rows = 32 // s.dtype.itemsize
        assert s.shape[0] % (2 * tile_rows) == 0
        halves.append((s.shape[0] // 2, tile_rows))

    def half(refs, which):
        return [r.at[pl.ds(pl.multiple_of(which * h, t), h)] for r, (h, t) in zip(refs, halves)]

    def copies(srcs, outs, sems):
        send, recv, loc = sems
        x, y, c, peers = _chip_peers()
        sibling = (x, y, 1 - c)
        mine = _shard_slices(outs, dims, 2 * x + y)
        local = [pltpu.make_async_copy(s, d, loc.at[a]) for a, (s, d) in enumerate(zip(srcs, mine))]

        def remote(src, dst, slot, a, dev):
            return pltpu.make_async_remote_copy(src, dst, send.at[slot, a], recv.at[slot, a],
                                                device_id=dev, device_id_type=MESH)

        sends = [remote(s, d, k, a, (px, py, c))
                 for k, (px, py) in enumerate(peers) for a, (s, d) in enumerate(zip(half(srcs, c), half(mine, c)))]
        theirs = [_shard_slices(outs, dims, 2 * px + py) for px, py in peers]
        arrivals = [remote(s, d, k, a, (px, py, c))
                    for k, (px, py) in enumerate(peers)
                    for a, (s, d) in enumerate(zip(half(srcs, c), half(theirs[k], c)))]
        passed_on = [remote(d, d, 3 + k, a, sibling) for k in range(3) for a, d in enumerate(half(theirs[k], c))]
        from_sibling = [remote(d, d, 3 + k, a, sibling)
                        for k in range(3) for a, d in enumerate(half(theirs[k], 1 - c))]
        return local, sends, arrivals, passed_on, from_sibling

    def start(srcs, outs, sems):
        local, sends = copies(srcs, outs, sems)[:2]
        for cp in local + sends:
            cp.start()

    def relay(srcs, outs, sems):
        _, _, arrivals, passed_on, _ = copies(srcs, outs, sems)
        for arrived, onward in zip(arrivals, passed_on):
            arrived.wait_recv()
            onward.start()

    def finish(srcs, outs, sems):
        local, sends, _, passed_on, from_sibling = copies(srcs, outs, sems)
        for cp in from_sibling:
            cp.wait_recv()
        for cp in sends + passed_on:
            cp.wait_send()
        for cp in local:
            cp.wait()

    scratch = [pltpu.SemaphoreType.DMA((6, n)), pltpu.SemaphoreType.DMA((6, n)), pltpu.SemaphoreType.DMA((n,))]
    return _Exchange(list(shards), full_shapes, scratch, start, relay, finish)


def _reduce_plan(grads, dims):
    n = len(grads)
    piece_shapes = []
    for g, (axis, size) in zip(grads, dims):
        shp = list(g.shape)
        shp[axis] = size
        piece_shapes.append(jax.ShapeDtypeStruct((N_CHIPS,) + tuple(shp), g.dtype))

    def copies(srcs, outs, sems):
        mine, theirs = outs[:n], outs[n:]
        send, recv, loc = sems
        x, y, c, peers = _chip_peers()
        sibling = (x, y, 1 - c)
        own = _shard_slices(srcs, dims, 2 * x + y)

        def remote(src, dst, slot, a, dev):
            return pltpu.make_async_remote_copy(src, dst, send.at[slot, a], recv.at[slot, a],
                                                device_id=dev, device_id_type=MESH)

        local = [pltpu.make_async_copy(own[a], mine[a].at[3], loc.at[a]) for a in range(n)]
        to_sibling = [remote(own[a], theirs[a].at[3], 3, a, sibling) for a in range(n)]
        to_chips = [remote(src, mine[a].at[k], k, a, (px, py, c))
                    for k, (px, py) in enumerate(peers)
                    for a, src in enumerate(_shard_slices(srcs, dims, 2 * px + py))]
        passed_on = [remote(mine[a].at[k], theirs[a].at[k], 4 + k, a, sibling) for k in range(3) for a in range(n)]
        return local, to_sibling, to_chips, passed_on

    def start(srcs, outs, sems):
        local, to_sibling, to_chips, _ = copies(srcs, outs, sems)
        for cp in local + to_sibling + to_chips:
            cp.start()

    def relay(srcs, outs, sems):
        _, _, to_chips, passed_on = copies(srcs, outs, sems)
        for arrived, onward in zip(to_chips, passed_on):
            arrived.wait_recv()
            onward.start()

    def finish(srcs, outs, sems):
        local, to_sibling, to_chips, passed_on = copies(srcs, outs, sems)
        for cp in to_sibling + passed_on:
            cp.wait_recv()
        for cp in to_sibling + to_chips + passed_on:
            cp.wait_send()
        for cp in local:
            cp.wait()

    scratch = [pltpu.SemaphoreType.DMA((7, n)), pltpu.SemaphoreType.DMA((7, n)), pltpu.SemaphoreType.DMA((n,))]
    return _Exchange(list(grads), piece_shapes * 2, scratch, start, relay, finish)


def _allsum_small(pack):
    rows, cols = pack.shape
    ndev = 8

    def body(p_ref, o_ref, buf, send, recv):
        x, y, c = lax.axis_index("x"), lax.axis_index("y"), lax.axis_index("c")
        me = 4 * x + 2 * y + c
        buf[me] = p_ref[...]
        started = []
        for r in range(1, ndev):
            bx, by, bc = (r >> 2) & 1, (r >> 1) & 1, r & 1
            dev = (x ^ bx, y ^ by, c ^ bc)
            cp = pltpu.make_async_remote_copy(p_ref, buf.at[me], send.at[r], recv.at[r],
                                              device_id=dev, device_id_type=MESH)
            cp.start()
            started.append(cp)
        for r in range(1, ndev):
            pltpu.make_async_remote_copy(p_ref, buf.at[me ^ r], send.at[r], recv.at[r],
                                         device_id=(x, y, c), device_id_type=MESH).wait_recv()
        for cp in started:
            cp.wait_send()
        acc = buf[0]
        for d in range(1, ndev):
            acc = acc + buf[d]
        o_ref[...] = acc

    vm = pl.BlockSpec(memory_space=pltpu.VMEM)
    return _pcall(
        body, name="allsum_small", in_specs=[vm], out_specs=vm,
        out_shape=jax.ShapeDtypeStruct((rows, cols), F32),
        scratch_shapes=[pltpu.VMEM((ndev, rows, cols), F32), pltpu.SemaphoreType.DMA((ndev,)),
                        pltpu.SemaphoreType.DMA((ndev,))],
    )(pack)


def _adamw(parts, w, m, v, layer, prev, name):
    _, rows, cols = w.shape
    tr = ROW_TILE if rows % ROW_TILE == 0 else rows
    counts = [p.shape[0] for p in parts]
    n_parts = len(parts)
    n_prev = 0 if prev is None else 4

    def body(*refs):
        part_refs = refs[:n_parts]
        w_ref, m_ref, v_ref = refs[n_parts:n_parts + 3]
        g_ref, d_ref, nm_ref, nv_ref = refs[n_parts + 3 + n_prev:]
        g = None
        for p_ref, cnt in zip(part_refs, counts):
            s = p_ref[0].astype(F32)
            for k in range(1, cnt):
                s = s + p_ref[k].astype(F32)
            g = s if g is None else g + s
        m2 = ADAM_B1 * m_ref[0] + (1.0 - ADAM_B1) * g
        v2 = ADAM_B2 * v_ref[0] + (1.0 - ADAM_B2) * (g * g)
        m_hat = m2 / (1.0 - ADAM_B1 ** ADAM_STEP)
        v_hat = v2 / (1.0 - ADAM_B2 ** ADAM_STEP)
        g_ref[0] = g
        d_ref[0] = -ADAM_LR * (m_hat / (jnp.sqrt(v_hat) + ADAM_EPS) + ADAM_WD * w_ref[0])
        nm_ref[0] = m2
        nv_ref[0] = v2

    blk = pl.BlockSpec((1, tr, cols), lambda i: (layer, i, 0))
    shp = jax.ShapeDtypeStruct(w.shape, F32)
    return _pcall(
        body, name=name, grid=(rows // tr,),
        in_specs=[pl.BlockSpec((cnt, tr, cols), lambda i: (0, i, 0)) for cnt in counts] + [blk] * 3 + [ANY] * n_prev,
        out_specs=[blk] * 4, out_shape=[shp] * 4,
        input_output_aliases={n_parts + 3 + k: k for k in range(n_prev)},
        compiler_params=_params("parallel"),
    )(*parts, w, m, v, *(prev or ()))


def kernel(x, meta_tokens, pre_norm_g, post_norm_g, w_in, conv_w, conv_b, conv_ln_g, conv_ln_b, w_pw2, b_pw2, w_out, loss_target, m_meta_tokens, m_pre_norm_g, m_post_norm_g, m_w_in, m_conv_w, m_conv_b, m_conv_ln_g, m_conv_ln_b, m_w_pw2, m_b_pw2, m_w_out, v_meta_tokens, v_pre_norm_g, v_post_norm_g, v_w_in, v_conv_w, v_conv_b, v_conv_ln_g, v_conv_ln_b, v_w_pw2, v_b_pw2, v_w_out):
    seq, d = x.shape[1], x.shape[2]
    depth = w_in.shape[0]
    length = N_META + seq
    lp = -(-length // ATT_BLOCK) * ATT_BLOCK
    tap_pad = ((0, 0), (0, CONV_PAD - CONV_WIDTH), (0, 0))

    shards = (w_in.astype(BF16), w_pw2.astype(BF16), w_out.astype(BF16), jnp.pad(conv_w, tap_pad))
    dims = [(1, w_in.shape[2]), (0, w_pw2.shape[1]), (0, w_out.shape[1]), (1, conv_w.shape[2])]
    layer_shards = lambda l: [s[l] for s in shards]

    w_in0, meta_f = _run_exchange(_gather_plan([shards[0][0], meta_tokens], [dims[0], (1, meta_tokens.shape[1])]),
                                  "gather_weights")

    h0 = jnp.concatenate([meta_f, x[0], jnp.zeros((lp - length, d), F32)], axis=0)
    target_p = jnp.pad(loss_target[0], ((N_META, lp - length), (0, 0)))
    vecs = (pre_norm_g, post_norm_g, conv_b, conv_ln_g, conv_ln_b, b_pw2)
    loss, dh0, vec_grads, pieces = _local_step(
        h0, target_p, seq, vecs, depth, w_in0=w_in0,
        gather_w_in=lambda l: _gather_plan([shards[0][l]], dims[:1]),
        gather_rest=lambda l: _gather_plan(layer_shards(l)[1:], dims[1:]),
        reduce_layer=lambda grads, first: _reduce_plan(list(grads), dims[first:first + len(grads)]))

    def update(k, w, m, v, name):
        outs = None
        for l in reversed(range(depth)):
            outs = _adamw([pieces[l][k], pieces[l][4 + k]], w, m, v, l, outs, name)
        return outs

    up_w_in = update(0, w_in, m_w_in, v_w_in, "adamw_w_in")
    up_w_pw2 = update(1, w_pw2, m_w_pw2, v_w_pw2, "adamw_w_pw2")
    up_w_out = update(2, w_out, m_w_out, v_w_out, "adamw_w_out")
    up_conv_w = [o[:, :CONV_WIDTH] for o in update(3, jnp.pad(conv_w, tap_pad), jnp.pad(m_conv_w, tap_pad),
                                                   jnp.pad(v_conv_w, tap_pad, constant_values=1.0), "adamw_conv_w")]

    two = lambda a: a.reshape(-1, d)
    vec_rows = [two(g) for g in vec_grads]
    n_vec = sum(a.shape[0] for a in vec_rows)
    pack = jnp.concatenate(vec_rows + [dh0[:N_META], jnp.full((8, d), loss, F32)], axis=0)
    pack = jnp.pad(pack, ((0, -pack.shape[0] % 8), (0, 0)))
    tot = _allsum_small(pack)
    loss_all = tot[n_vec + N_META, 0]

    cat = lambda arrs: jnp.concatenate([two(t) for t in arrs], axis=0)[None]
    small_m = (m_pre_norm_g, m_post_norm_g, m_conv_b, m_conv_ln_g, m_conv_ln_b, m_b_pw2)
    small_v = (v_pre_norm_g, v_post_norm_g, v_conv_b, v_conv_ln_g, v_conv_ln_b, v_b_pw2)
    up_small = _adamw([tot[None, :n_vec]], cat(vecs), cat(small_m), cat(small_v), 0, None, "adamw_vectors")

    def unpack(o):
        res, r0 = [], 0
        for t in vecs:
            nrow = t.size // d
            res.append(o[0, r0:r0 + nrow].reshape(t.shape))
            r0 += nrow
        return res

    up_small = [unpack(o) for o in up_small]
    chip = 2 * lax.axis_index("x") + lax.axis_index("y")
    mcols = meta_tokens.shape[1]
    g_meta = lax.dynamic_slice_in_dim(tot[n_vec:n_vec + N_META], chip * mcols, mcols, axis=1)
    up_meta = [o[0] for o in _adamw([g_meta[None]], meta_tokens[None], m_meta_tokens[None], v_meta_tokens[None],
                                    0, None, "adamw_meta")]

    grad_x = dh0[N_META:length][None]
    outs = [loss_all, grad_x]
    for j in range(4):
        pre, post, cb, lg, lb, bp = up_small[j]
        outs += [up_meta[j], pre, post, up_w_in[j], up_conv_w[j], cb, lg, lb, up_w_pw2[j], bp, up_w_out[j]]
    return tuple(outs)
```

```python
from typing import Callable, NamedTuple

import jax
import jax.numpy as jnp
from jax import lax
from jax.experimental import pallas as pl
from jax.experimental.pallas import tpu as pltpu

F32 = jnp.float32
BF16 = jnp.bfloat16

N_META = 16
D_CONV = 512
D_SB = 512
HEAD_DIM = 64
CONV_WIDTH = 31
CONV_PAD = 32
CONV_ROWS = 128
CONV_ROWS_DX = 16
CONV_ROWS_DW = 16
RMS_EPS = 1e-6
LN_EPS = 1e-5
Q_SCALE = HEAD_DIM ** -0.5

ADAM_LR = 0.001
ADAM_B1 = 0.9
ADAM_B2 = 0.999
ADAM_EPS = 1e-08
ADAM_WD = 0.01
ADAM_STEP = 10

LANES = 128
ROW_TILE = 256
MM_TILE_MAX = 544
OUT_TILE_MAX = 1088
ATT_BLOCK = 256
ATT_PAIRS = 4
ATT_PAIRS_FWD = 4
VMEM_LIMIT = 56 * 1024 * 1024
EXP_ZERO = -104.0
COUNT_LANE = LANES - 1
RELAY_AT = 0.75

MESH = pl.DeviceIdType.MESH


def _pcall(body, **kw):
    return pl.pallas_call(body, **kw)


def _params(*sem):
    return pltpu.CompilerParams(dimension_semantics=sem, vmem_limit_bytes=VMEM_LIMIT)


def _sigmoid(x):
    return 1.0 / (1.0 + jnp.exp(-x))


def _silu_fwd_bwd(x):
    s = _sigmoid(x)
    return x * s, s * (1.0 + x * (1.0 - s))


def _nt(a, b):
    return lax.dot_general(a, b, (((1,), (1,)), ((), ())), preferred_element_type=F32)


def _tn(a, b):
    return lax.dot_general(a, b, (((0,), (0,)), ((), ())), preferred_element_type=F32)


def _nn(a, b):
    return jnp.dot(a, b, preferred_element_type=F32)


def _mm_tile(rows, cap=None):
    return max(t for t in range(16, (cap or MM_TILE_MAX) + 1, 16) if rows % t == 0)


def _host_begin(plan, parts, steps):
    if plan is not None:
        x_in, _, x_out, _, x_sems = parts

        @pl.when(pl.program_id(0) == 0)
        def _():
            plan.start(x_in, x_out, x_sems)

        @pl.when(pl.program_id(0) == int(RELAY_AT * steps))
        def _():
            plan.relay(x_in, x_out, x_sems)


def _host_end(plan, parts, steps):
    if plan is not None:
        x_in, _, x_out, _, x_sems = parts

        @pl.when(pl.program_id(0) == steps - 1)
        def _():
            plan.finish(x_in, x_out, x_sems)


def _inproj(h, g, w, plan=None):
    lp, d = h.shape
    n = w.shape[1]
    tm = _mm_tile(lp)
    steps = lp // tm
    split, x_args, x_in_specs, x_out_specs, x_out_shapes, x_scratch = _hosted(plan, 2, 0)

    def body(h_ref, g_ref, w_ref, *rest):
        parts = split(rest)
        ew_ref, qkv_ref = parts[1]
        _host_begin(plan, parts, steps)
        x = h_ref[...]
        rstd = lax.rsqrt(jnp.mean(x * x, axis=-1, keepdims=True) + RMS_EPS)
        u = ((x * rstd) * g_ref[...]).astype(BF16)
        p = _nn(u, w_ref[...])
        ew_ref[:, 0:1536] = p[:, 0:1536].astype(BF16)
        ew_ref[:, 1536:2048] = p[:, 3072:3584].astype(BF16)
        qkv_ref[:, 0:512] = (p[:, 1536:2048] * Q_SCALE).astype(BF16)
        qkv_ref[:, 512:1536] = p[:, 2048:3072].astype(BF16)
        _host_end(plan, parts, steps)

    outs = _pcall(
        body, name="inproj_fwd" if plan is None else "inproj_fwd_gather", grid=(steps,),
        in_specs=[pl.BlockSpec((tm, d), lambda i: (i, 0)),
                  pl.BlockSpec((1, d), lambda i: (0, 0)),
                  pl.BlockSpec((d, n), lambda i: (0, 0))] + x_in_specs,
        out_specs=[pl.BlockSpec((tm, 2048), lambda i: (i, 0)),
                   pl.BlockSpec((tm, 1536), lambda i: (i, 0))] + x_out_specs,
        out_shape=[jax.ShapeDtypeStruct((lp, 2048), BF16), jax.ShapeDtypeStruct((lp, 1536), BF16)] + x_out_shapes,
        scratch_shapes=x_scratch,
        compiler_params=_params("parallel" if plan is None else "arbitrary"),
    )(h, g, w, *x_args)
    return outs[0], outs[1], outs[2:]


def _layer_norm_stats(c1):
    mu = jnp.mean(c1, axis=-1, keepdims=True)
    xc = c1 - mu
    var = jnp.mean(xc * xc, axis=-1, keepdims=True)
    rstd = lax.rsqrt(var + LN_EPS)
    return xc * rstd, rstd


def _shifted_copies(window, cols, shifted, tm):
    shifted[0] = window[:, cols]
    rows = tm + CONV_PAD - 8
    for b in range(1, 8):
        shifted[b, 0:rows, :] = window[pl.ds(b, rows), cols]


def _shifted_rows(shifted, shift, tm):
    b = shift % 8
    return shifted[b, pl.ds(pl.multiple_of(shift - b, 8), tm), :]


def _weighted(cw8_ref, j, cols, rows):
    w8 = cw8_ref[pl.ds(pl.multiple_of(j * 8, 8), 8), cols]
    r = rows.shape[0]
    return (rows.reshape(r // 8, 8, LANES) * w8[None]).reshape(r, LANES)


def _conv_fwd(ew, cw, cb, lng, lnb, wpw2, bpw2, plan=None):
    lp = ew.shape[0]
    tm = ROW_TILE
    steps = lp // tm
    split, x_args, x_in_specs, x_out_specs, x_out_shapes, x_scratch = _hosted(plan, 3, 2)

    def body(ew_ref, cw_ref, cb_ref, lng_ref, lnb_ref, w_ref, b_ref, *rest):
        parts = split(rest)
        (c1_ref, c4_ref, c5_ref), (xbuf, shifted) = parts[1], parts[3]
        _host_begin(plan, parts, steps)
        @pl.when(pl.program_id(0) == 0)
        def _():
            xbuf[0:CONV_PAD, :] = jnp.zeros((CONV_PAD, D_CONV), F32)

        ga = ew_ref[:, 0:512].astype(F32)
        gb = ew_ref[:, 512:1024].astype(F32)
        cg = ew_ref[:, 1024:1536].astype(F32)
        xbuf[CONV_PAD:CONV_PAD + tm, :] = ga * _sigmoid(gb)
        for blk in range(D_CONV // LANES):
            cs = slice(blk * LANES, (blk + 1) * LANES)
            _shifted_copies(xbuf, cs, shifted, tm)
            for r0 in range(0, tm, CONV_ROWS):
                acc = jnp.zeros((CONV_ROWS, LANES), F32) + cb_ref[:, cs]
                for j in range(CONV_WIDTH):
                    acc = acc + _weighted(cw_ref, j, cs, _shifted_rows(
                        shifted, r0 + CONV_PAD - (CONV_WIDTH - 1) + j, CONV_ROWS))
                c1_ref[r0:r0 + CONV_ROWS, cs] = acc
        xbuf[0:CONV_PAD, :] = xbuf[tm:tm + CONV_PAD, :]
        xhat, _ = _layer_norm_stats(c1_ref[...])
        c2 = xhat * lng_ref[...] + lnb_ref[...]
        c3 = c2 * _sigmoid(c2)
        c4 = _nn(c3.astype(BF16), w_ref[...]) + b_ref[...]
        c4_ref[...] = c4
        c5_ref[...] = (c4 * (cg * _sigmoid(cg))).astype(BF16)
        _host_end(plan, parts, steps)

    vec = pl.BlockSpec((1, D_CONV), lambda i: (0, 0))
    row = pl.BlockSpec((tm, D_CONV), lambda i: (i, 0))
    outs = _pcall(
        body, name="conv_fwd" if plan is None else "conv_fwd_gather", grid=(steps,),
        in_specs=[pl.BlockSpec((tm, 1536), lambda i: (i, 0)),
                  pl.BlockSpec((8 * CONV_PAD, D_CONV), lambda i: (0, 0)),
                  vec, vec, vec,
                  pl.BlockSpec((D_CONV, D_CONV), lambda i: (0, 0)),
                  vec] + x_in_specs,
        out_specs=[row, row, row] + x_out_specs,
        out_shape=[jax.ShapeDtypeStruct((lp, D_CONV), F32), jax.ShapeDtypeStruct((lp, D_CONV), F32),
                   jax.ShapeDtypeStruct((lp, D_CONV), BF16)] + x_out_shapes,
        scratch_shapes=[pltpu.VMEM((tm + CONV_PAD, D_CONV), F32),
                        pltpu.VMEM((8, tm + CONV_PAD, LANES), F32)] + x_scratch,
        compiler_params=_params("arbitrary"),
    )(ew, cw, cb, lng, lnb, wpw2, bpw2, *x_args)
    return outs[0], outs[1], outs[2], outs[3:]


def _block_sums(x, m01):
    return _nn(x.astype(BF16), m01)


def _attn_masks():
    lane = lax.broadcasted_iota(jnp.int32, (1, LANES), 1)
    row = lax.broadcasted_iota(jnp.int32, (2 * ATT_BLOCK, ATT_BLOCK), 0)
    col = lax.broadcasted_iota(jnp.int32, (2 * ATT_BLOCK, ATT_BLOCK), 1)
    return lane < HEAD_DIM, col < (row & (ATT_BLOCK - 1))


def _stack_heads(x, first_head):
    zero = jnp.zeros_like(x)
    return jnp.concatenate([jnp.where(first_head, x, zero), jnp.where(first_head, zero, x)], axis=0)


def _unstack_heads(x2, first_head):
    rows = x2.shape[0] // 2
    return jnp.where(first_head, x2[:rows], x2[rows:])


def _hosted(plan, n_out, n_scratch):
    n_in = 0 if plan is None else len(plan.inputs)
    n_x = 0 if plan is None else len(plan.out_shapes)

    def split(rest):
        a, b, c = n_in + n_out, n_in + n_out + n_x, n_in + n_out + n_x + n_scratch
        return rest[:n_in], rest[n_in:a], rest[a:b], rest[b:c], rest[c:]

    if plan is None:
        return split, [], [], [], [], []
    return split, list(plan.inputs), [ANY] * n_in, [ANY] * n_x, list(plan.out_shapes), list(plan.scratch)


def _attn_fwd(qkv, tri, plan=None):
    lp = qkv.shape[0]
    bq = ATT_BLOCK
    ngrp = ATT_PAIRS_FWD
    nstep = D_SB // (LANES * ngrp)
    nq = lp // bq
    assert nq <= COUNT_LANE
    split, x_args, x_in_specs, x_out_specs, x_out_shapes, x_scratch = _hosted(plan, 2, 3)

    def body(q_ref, k_ref, v_ref, tri_ref, *rest):
        x_in, (o_ref, carry_ref), x_out, (c_s, acc_s, cm_s), x_sems = split(rest)
        i = pl.program_id(1)
        if plan is not None:
            @pl.when(jnp.logical_and(pl.program_id(0) == 0, i == 0))
            def _():
                plan.start(x_in, x_out, x_sems)

            @pl.when(jnp.logical_and(pl.program_id(0) == nstep - 1, i == int(RELAY_AT * nq)))
            def _():
                plan.relay(x_in, x_out, x_sems)

        first_head, vis = _attn_masks()
        lane = lax.broadcasted_iota(jnp.int32, (1, LANES), 1)
        cols = [slice(g * LANES, (g + 1) * LANES) for g in range(ngrp)]
        q2s = [_stack_heads(q_ref[:, cs], first_head) for cs in cols]
        tri_m = tri_ref[...]

        c_s[...] = jnp.zeros_like(c_s)
        acc_s[...] = jnp.zeros_like(acc_s)
        cm_s[...] = jnp.zeros_like(cm_s)

        def blocks(js, masks):
            offs = [pl.multiple_of(j * bq, bq) for j in js]
            work = [(g, b) for b in range(len(js)) for g in range(ngrp)]
            zs = {(g, b): _nt(q2s[g], k_ref[pl.ds(offs[b], bq), cols[g]]) for g, b in work}
            lss = {}
            for g, b in work:
                z = zs[g, b]
                ls = -(jnp.maximum(z, 0.0) + jnp.log(1.0 + jnp.exp(-jnp.abs(z))))
                lss[g, b] = ls if masks[b] is None else jnp.where(masks[b], ls, 0.0)
            tails = {gb: _block_sums(lss[gb], tri_m) for gb in work}
            probs = {}
            carry = [c_s[g] for g in range(ngrp)]
            saved = [cm_s[g] for g in range(ngrp)]
            for g, b in work:
                a = jnp.exp(zs[g, b] + lss[g, b] + tails[g, b] + carry[g])
                probs[g, b] = (a if masks[b] is None else jnp.where(masks[b], a, 0.0)).astype(BF16)
                saved[g] = jnp.where(lane == js[b], carry[g], saved[g])
                carry[g] = carry[g] + tails[g, b][:, 0:1] + lss[g, b][:, 0:1]
            top = None
            for g in range(ngrp):
                c_s[g] = carry[g]
                cm_s[g] = saved[g]
                acc = acc_s[g]
                for b in range(len(js)):
                    acc = acc + _nn(probs[g, b], v_ref[pl.ds(offs[b], bq), cols[g]])
                acc_s[g] = acc
                top = carry[g] if top is None else jnp.maximum(top, carry[g])
            return jnp.max(top) > EXP_ZERO

        alive = lax.cond(i > 0, lambda: blocks([i, i - 1], [vis, None]), lambda: blocks([i], [vis]))
        rest = jnp.maximum(i - 1, 0)

        def pair(carry):
            t, _ = carry
            j = i - 2 - 2 * t
            return t + 1, blocks([j, j - 1], [None, None])

        trips, alive = lax.while_loop(lambda ca: jnp.logical_and(ca[0] < rest // 2, ca[1]), pair, (0, alive))
        last = jnp.logical_and(jnp.logical_and(rest % 2 == 1, trips == rest // 2), alive)

        @pl.when(last)
        def _():
            blocks([0], [None])

        n_done = (jnp.minimum(i + 1, 2) + 2 * trips + last.astype(jnp.int32)).astype(F32)
        for g in range(ngrp):
            cmat = jnp.where(lane == COUNT_LANE, n_done, cm_s[g])
            carry_ref[:, 2 * g * LANES:(2 * g + 1) * LANES] = cmat[:bq]
            carry_ref[:, (2 * g + 1) * LANES:(2 * g + 2) * LANES] = cmat[bq:]
            o_ref[:, cols[g]] = _unstack_heads(acc_s[g], first_head)
        if plan is not None:
            @pl.when(jnp.logical_and(pl.program_id(0) == nstep - 1, i == nq - 1))
            def _():
                plan.finish(x_in, x_out, x_sems)

    width = ngrp * LANES
    outs = _pcall(
        body, name="attn_fwd" if plan is None else "attn_fwd_gather", grid=(nstep, nq),
        in_specs=[pl.BlockSpec((bq, width), lambda p, i: (i, p)),
                  pl.BlockSpec((lp, width), lambda p, i: (0, nstep + p)),
                  pl.BlockSpec((lp, width), lambda p, i: (0, 2 * nstep + p)),
                  pl.BlockSpec((bq, bq), lambda p, i: (0, 0))] + x_in_specs,
        out_specs=[pl.BlockSpec((bq, width), lambda p, i: (i, p)),
                   pl.BlockSpec((bq, 2 * width), lambda p, i: (i, p))] + x_out_specs,
        out_shape=[jax.ShapeDtypeStruct((lp, D_SB), F32), jax.ShapeDtypeStruct((lp, 2 * D_SB), F32)] + x_out_shapes,
        scratch_shapes=[pltpu.VMEM((ngrp, 2 * bq, 1), F32), pltpu.VMEM((ngrp, 2 * bq, LANES), F32),
                        pltpu.VMEM((ngrp, 2 * bq, LANES), F32)] + x_scratch,
        compiler_params=_params("arbitrary", "arbitrary"),
    )(qkv, qkv, qkv, tri, *x_args)
    return outs[0], outs[1], outs[2:]


def _outproj(c5, att, ew, h, w, g):
    lp, d = h.shape
    tm = _mm_tile(lp, OUT_TILE_MAX)

    def body(c5_ref, att_ref, sg_ref, h_ref, w_ref, g_ref, hn_ref, cat_ref, mix_ref):
        sg = sg_ref[...].astype(F32)
        s = att_ref[...] * (sg * _sigmoid(sg))
        cat_ref[:, 0:D_CONV] = c5_ref[...]
        cat_ref[:, D_CONV:] = s.astype(BF16)
        mixed = _nn(cat_ref[...], w_ref[...])
        mix_ref[...] = mixed
        rstd = lax.rsqrt(jnp.mean(mixed * mixed, axis=-1, keepdims=True) + RMS_EPS)
        hn_ref[...] = h_ref[...] + (mixed * rstd) * g_ref[...]

    half = pl.BlockSpec((tm, 512), lambda i: (i, 0))
    full = pl.BlockSpec((tm, d), lambda i: (i, 0))
    return _pcall(
        body, name="outproj_fwd", grid=(lp // tm,),
        in_specs=[half, half, pl.BlockSpec((tm, 512), lambda i: (i, 3)), full,
                  pl.BlockSpec((d, d), lambda i: (0, 0)), pl.BlockSpec((1, d), lambda i: (0, 0))],
        out_specs=[full, full, full],
        out_shape=[jax.ShapeDtypeStruct((lp, d), F32), jax.ShapeDtypeStruct((lp, d), BF16),
                   jax.ShapeDtypeStruct((lp, d), F32)],
        compiler_params=_params("parallel"),
    )(c5, att, ew, h, w, g)


def _loss_head(h, target, seq):
    lp, d = h.shape
    tm = ROW_TILE

    def body(h_ref, t_ref, dh_ref, loss_ref):
        i = pl.program_id(0)

        @pl.when(i == 0)
        def _():
            loss_ref[...] = jnp.zeros_like(loss_ref)

        row = i * tm + lax.broadcasted_iota(jnp.int32, (tm, 1), 0)
        real = jnp.logical_and(row >= N_META, row < N_META + seq)
        diff = jnp.where(real, h_ref[...] - t_ref[...], 0.0)
        dh_ref[...] = diff * (1.0 / d)
        loss_ref[...] += 0.5 * jnp.sum(jnp.sum(diff * diff, axis=-1, keepdims=True) * (1.0 / d))

    full = pl.BlockSpec((tm, d), lambda i: (i, 0))
    return _pcall(
        body, name="loss_head", grid=(lp // tm,),
        in_specs=[full, full],
        out_specs=[full, pl.BlockSpec((8, LANES), lambda i: (0, 0))],
        out_shape=[jax.ShapeDtypeStruct((lp, d), F32), jax.ShapeDtypeStruct((8, LANES), F32)],
        compiler_params=_params("arbitrary"),
    )(h, target)


def _outproj_bwd(dh, mixed, g, w, att, ew, c4):
    lp, d = dh.shape
    tm = _mm_tile(lp, OUT_TILE_MAX)

    def body(dh_ref, mix_ref, g_ref, w_ref, att_ref, cg_ref, sg_ref, c4_ref,
             dmix_ref, datt_ref, dsg_ref, dc4_ref, dcg_ref, dg_ref, db_ref):
        @pl.when(pl.program_id(0) == 0)
        def _():
            dg_ref[...] = jnp.zeros_like(dg_ref)
            db_ref[...] = jnp.zeros_like(db_ref)

        mixed = mix_ref[...]
        dhv = dh_ref[...]
        rstd = lax.rsqrt(jnp.mean(mixed * mixed, axis=-1, keepdims=True) + RMS_EPS)
        n = mixed * rstd
        dg_ref[...] += jnp.sum(dhv * n, axis=0, keepdims=True)
        dn = dhv * g_ref[...]
        dmix = (rstd * (dn - n * jnp.mean(dn * n, axis=-1, keepdims=True))).astype(BF16)
        dmix_ref[...] = dmix
        dcat = _nt(dmix, w_ref[...])
        dc5 = dcat[:, 0:D_CONV]
        ds = dcat[:, D_CONV:]
        silu_sg, dsilu_sg = _silu_fwd_bwd(sg_ref[...].astype(F32))
        datt_ref[...] = (ds * silu_sg).astype(BF16)
        dsg_ref[...] = (ds * att_ref[...] * dsilu_sg).astype(BF16)
        silu_cg, dsilu_cg = _silu_fwd_bwd(cg_ref[...].astype(F32))
        dc4 = dc5 * silu_cg
        db_ref[...] += jnp.sum(dc4, axis=0, keepdims=True)
        dc4_ref[...] = dc4.astype(BF16)
        dcg_ref[...] = (dc5 * c4_ref[...] * dsilu_cg).astype(BF16)

    half = pl.BlockSpec((tm, 512), lambda i: (i, 0))
    full = pl.BlockSpec((tm, d), lambda i: (i, 0))
    hb = jax.ShapeDtypeStruct((lp, 512), BF16)
    return _pcall(
        body, name="outproj_bwd", grid=(lp // tm,),
        in_specs=[full, full, pl.BlockSpec((1, d), lambda i: (0, 0)), pl.BlockSpec((d, d), lambda i: (0, 0)),
                  half, pl.BlockSpec((tm, 512), lambda i: (i, 2)), pl.BlockSpec((tm, 512), lambda i: (i, 3)), half],
        out_specs=[full, half, half, half, half,
                   pl.BlockSpec((1, d), lambda i: (0, 0)), pl.BlockSpec((1, 512), lambda i: (0, 0))],
        out_shape=[jax.ShapeDtypeStruct((lp, d), BF16), hb, hb, hb, hb,
                   jax.ShapeDtypeStruct((1, d), F32), jax.ShapeDtypeStruct((1, 512), F32)],
        compiler_params=_params("arbitrary"),
    )(dh, mixed, g, w, att, ew, ew, c4)


def _attn_bwd(qkv, carries, datt, tri, upper, plan=None):
    lp = qkv.shape[0]
    bq = ATT_BLOCK
    ngrp = ATT_PAIRS
    nstep = D_SB // (LANES * ngrp)
    nq = lp // bq
    split, x_args, x_in_specs, x_out_specs, x_out_shapes, x_scratch = _hosted(plan, 3, 2)

    def body(q_ref, k_ref, v_ref, carry_ref, do_ref, tri_ref, upper_ref, *rest):
        x_in, (dq_ref, dk_ref, dv_ref), x_out, (run_s, dq_s), x_sems = split(rest)
        i = pl.program_id(1)
        if plan is not None:
            @pl.when(jnp.logical_and(pl.program_id(0) == 0, i == 0))
            def _():
                plan.start(x_in, x_out, x_sems)

            @pl.when(jnp.logical_and(pl.program_id(0) == nstep - 1, i == max(nq - 2, 0)))
            def _():
                plan.relay(x_in, x_out, x_sems)

        @pl.when(i == 0)
        def _():
            dk_ref[...] = jnp.zeros_like(dk_ref)
            dv_ref[...] = jnp.zeros_like(dv_ref)

        first_head, vis = _attn_masks()
        lane = lax.broadcasted_iota(jnp.int32, (1, LANES), 1)
        cols = [slice(g * LANES, (g + 1) * LANES) for g in range(ngrp)]
        q2s = [_stack_heads(q_ref[:, cs], first_head) for cs in cols]
        do2s = [_stack_heads(do_ref[:, cs], first_head) for cs in cols]
        cmats = [jnp.concatenate([carry_ref[:, 2 * g * LANES:(2 * g + 1) * LANES],
                                  carry_ref[:, (2 * g + 1) * LANES:(2 * g + 2) * LANES]], axis=0)
                 for g in range(ngrp)]
        tri_m = tri_ref[...]
        upper_m = upper_ref[...]

        def blocks(js, masks):
            offs = [pl.multiple_of(j * bq, bq) for j in js]
            work = [(g, b) for b in range(len(js)) for g in range(ngrp)]
            zs = {(g, b): _nt(q2s[g], k_ref[pl.ds(offs[b], bq), cols[g]]) for g, b in work}
            lss = {}
            for g, b in work:
                z = zs[g, b]
                ls = -(jnp.maximum(z, 0.0) + jnp.log(1.0 + jnp.exp(-jnp.abs(z))))
                lss[g, b] = ls if masks[b] is None else jnp.where(masks[b], ls, 0.0)
            tails = {gb: _block_sums(lss[gb], tri_m) for gb in work}
            das = {(g, b): _nt(do2s[g], v_ref[pl.ds(offs[b], bq), cols[g]]) for g, b in work}
            probs, des = {}, {}
            for g, b in work:
                c = jnp.sum(jnp.where(lane == js[b], cmats[g], 0.0), axis=-1, keepdims=True)
                a = jnp.exp(zs[g, b] + lss[g, b] + tails[g, b] + c)
                a = a if masks[b] is None else jnp.where(masks[b], a, 0.0)
                probs[g, b] = a.astype(BF16)
                des[g, b] = das[g, b] * a
            prefixes = {gb: _block_sums(des[gb], upper_m) for gb in work}
            runs = [run_s[g] for g in range(ngrp)]
            dzs = {}
            for g, b in work:
                beta = jnp.exp(zs[g, b] + lss[g, b])
                dz = des[g, b] - beta * (des[g, b] + runs[g] + prefixes[g, b])
                dzs[g, b] = (dz if masks[b] is None else jnp.where(masks[b], dz, 0.0)).astype(BF16)
                runs[g] = runs[g] + prefixes[g, b][:, bq - 1:bq] + des[g, b][:, bq - 1:bq]
            for g in range(ngrp):
                run_s[g] = runs[g]
                dq = dq_s[g]
                for b in range(len(js)):
                    rows = pl.ds(offs[b], bq)
                    dq = dq + _nn(dzs[g, b], k_ref[rows, cols[g]])
                    dk_ref[rows, cols[g]] += _tn(dzs[g, b], q2s[g])
                    dv_ref[rows, cols[g]] += _tn(probs[g, b], do2s[g])
                dq_s[g] = dq

        n_done = jnp.max(carry_ref[:, COUNT_LANE:COUNT_LANE + 1]).astype(jnp.int32)
        n_done = jnp.clip(n_done, 1, i + 1)
        before = jnp.maximum(n_done - 2, 0)
        j0 = i - n_done + 1
        odd = before % 2
        run_s[...] = jnp.zeros_like(run_s)
        dq_s[...] = jnp.zeros_like(dq_s)

        @pl.when(odd == 1)
        def _():
            blocks([j0], [None])

        @pl.loop(0, before // 2)
        def _(t):
            blocks([j0 + odd + 2 * t, j0 + odd + 2 * t + 1], [None, None])

        @pl.when(n_done > 1)
        def _():
            blocks([i - 1, i], [None, vis])

        @pl.when(n_done <= 1)
        def _():
            blocks([i], [vis])

        for g in range(ngrp):
            dq_ref[:, cols[g]] = (_unstack_heads(dq_s[g], first_head) * Q_SCALE).astype(BF16)
        if plan is not None:
            @pl.when(jnp.logical_and(pl.program_id(0) == nstep - 1, i == nq - 1))
            def _():
                plan.finish(x_in, x_out, x_sems)

    width = ngrp * LANES
    once = pl.Buffered(1)
    qb = pl.BlockSpec((bq, width), lambda p, i: (i, p))
    colb = pl.BlockSpec((lp, width), lambda p, i: (0, p), pipeline_mode=once)
    sq = pl.BlockSpec((bq, bq), lambda p, i: (0, 0))
    outs = _pcall(
        body, name="attn_bwd" if plan is None else "attn_bwd_reduce", grid=(nstep, nq),
        in_specs=[qb,
                  pl.BlockSpec((lp, width), lambda p, i: (0, nstep + p), pipeline_mode=once),
                  pl.BlockSpec((lp, width), lambda p, i: (0, 2 * nstep + p), pipeline_mode=once),
                  pl.BlockSpec((bq, 2 * width), lambda p, i: (i, p)), qb, sq, sq] + x_in_specs,
        out_specs=[qb, colb, colb] + x_out_specs,
        out_shape=[jax.ShapeDtypeStruct((lp, D_SB), BF16), jax.ShapeDtypeStruct((lp, D_SB), F32),
                   jax.ShapeDtypeStruct((lp, D_SB), F32)] + x_out_shapes,
        scratch_shapes=[pltpu.VMEM((ngrp, 2 * bq, 1), F32), pltpu.VMEM((ngrp, 2 * bq, LANES), F32)] + x_scratch,
        compiler_params=_params("arbitrary", "arbitrary"),
    )(qkv, qkv, qkv, carries, datt, tri, upper, *x_args)
    return outs[0], outs[1], outs[2], outs[3:]


def _conv_bwd(dc4, c1, ew, cw, lng, lnb, wpw2):
    lp = ew.shape[0]
    tm = ROW_TILE
    nt = lp // tm
    halo_per_tile = tm // CONV_PAD

    def body(dc4_ref, c1_ref, ew_ref, halo_ref, cw_ref, lng_ref, lnb_ref, w_ref,
             dga_ref, dgb_ref, c3_ref, dcw_ref, dcb_ref, dlng_ref, dlnb_ref, xbuf, dbuf, shifted, wacc):
        step = pl.program_id(0)

        @pl.when(step == 0)
        def _():
            wacc[...] = jnp.zeros_like(wacc)
            dcb_ref[...] = jnp.zeros_like(dcb_ref)
            dlng_ref[...] = jnp.zeros_like(dlng_ref)
            dlnb_ref[...] = jnp.zeros_like(dlnb_ref)
            dbuf[tm:tm + CONV_PAD, :] = jnp.zeros((CONV_PAD, D_CONV), F32)

        dc3 = _nt(dc4_ref[...], w_ref[...])
        xhat, rstd = _layer_norm_stats(c1_ref[...])
        c2 = xhat * lng_ref[...] + lnb_ref[...]
        c3, dsilu = _silu_fwd_bwd(c2)
        c3_ref[...] = c3.astype(BF16)
        dc2 = dc3 * dsilu
        dlng_ref[...] += jnp.sum(dc2 * xhat, axis=0, keepdims=True)
        dlnb_ref[...] += jnp.sum(dc2, axis=0, keepdims=True)
        dxhat = dc2 * lng_ref[...]
        dc1 = rstd * (dxhat - jnp.mean(dxhat, axis=-1, keepdims=True)
                      - xhat * jnp.mean(dxhat * xhat, axis=-1, keepdims=True))
        dcb_ref[...] += jnp.sum(dc1, axis=0, keepdims=True)
        dbuf[0:tm, :] = dc1

        ga = ew_ref[:, 0:512].astype(F32)
        sgb = _sigmoid(ew_ref[:, 512:1024].astype(F32))
        xbuf[CONV_PAD:CONV_PAD + tm, :] = ga * sgb
        first_tile = step == nt - 1
        halo = halo_ref[:, 0:512].astype(F32) * _sigmoid(halo_ref[:, 512:1024].astype(F32))
        xbuf[0:CONV_PAD, :] = jnp.where(first_tile, 0.0, halo)

        for cb in range(D_CONV // LANES):
            cs = slice(cb * LANES, (cb + 1) * LANES)
            _shifted_copies(dbuf, cs, shifted, tm)
            for r0 in range(0, tm, CONV_ROWS_DX):
                rs = slice(r0, r0 + CONV_ROWS_DX)
                dc0 = jnp.zeros((CONV_ROWS_DX, LANES), F32)
                for j in range(CONV_WIDTH):
                    dc0 = dc0 + _weighted(cw_ref, j, cs, _shifted_rows(
                        shifted, r0 + CONV_WIDTH - 1 - j, CONV_ROWS_DX))
                dga_ref[rs, cs] = (dc0 * sgb[rs, cs]).astype(BF16)
                dgb_ref[rs, cs] = (dc0 * ga[rs, cs] * sgb[rs, cs] * (1.0 - sgb[rs, cs])).astype(BF16)
            _shifted_copies(xbuf, cs, shifted, tm)
            for r0 in range(0, tm, CONV_ROWS_DW):
                d1 = dbuf[r0:r0 + CONV_ROWS_DW, cs]

                for j in range(CONV_WIDTH):
                    prod = d1 * _shifted_rows(shifted, r0 + CONV_PAD - (CONV_WIDTH - 1) + j, CONV_ROWS_DW)
                    wacc[j * 8:(j + 1) * 8, cs] += jnp.sum(prod.reshape(CONV_ROWS_DW // 8, 8, LANES), axis=0)
        dbuf[tm:tm + CONV_PAD, :] = dbuf[0:CONV_PAD, :]

        @pl.when(step == nt - 1)
        def _():
            dcw_ref[...] = jnp.sum(wacc[...].reshape(CONV_PAD, 8, D_CONV), axis=1)

    rev = lambda i: (nt - 1 - i, 0)
    row = pl.BlockSpec((tm, D_CONV), rev)
    vec = pl.BlockSpec((1, D_CONV), lambda i: (0, 0))
    hb = jax.ShapeDtypeStruct((lp, D_CONV), BF16)
    vs = jax.ShapeDtypeStruct((1, D_CONV), F32)
    return _pcall(
        body, name="conv_bwd", grid=(nt,),
        in_specs=[row, row, pl.BlockSpec((tm, 1024), rev),
                  pl.BlockSpec((CONV_PAD, 1024), lambda i: (jnp.maximum((nt - 1 - i) * halo_per_tile - 1, 0), 0)),
                  pl.BlockSpec((8 * CONV_PAD, D_CONV), lambda i: (0, 0)), vec, vec,
                  pl.BlockSpec((D_CONV, D_CONV), lambda i: (0, 0))],
        out_specs=[row, row, row, pl.BlockSpec((CONV_PAD, D_CONV), lambda i: (0, 0)), vec, vec, vec],
        out_shape=[hb, hb, hb, jax.ShapeDtypeStruct((CONV_PAD, D_CONV), F32), vs, vs, vs],
        scratch_shapes=[pltpu.VMEM((tm + CONV_PAD, D_CONV), F32), pltpu.VMEM((tm + CONV_PAD, D_CONV), F32),
                        pltpu.VMEM((8, tm + CONV_PAD, LANES), F32), pltpu.VMEM((8 * CONV_PAD, D_CONV), F32)],
        compiler_params=_params("arbitrary"),
    )(dc4, c1, ew, ew, cw, lng, lnb, wpw2)


def _inproj_bwd(dga, dgb, dcg, dq, dk, dv, dsg, h, g, w, dh_out, plan=None):
    lp, d = h.shape
    n = w.shape[1]
    tm = _mm_tile(lp)
    steps = lp // tm
    split, x_args, x_in_specs, x_out_specs, x_out_shapes, x_scratch = _hosted(plan, 4, 0)

    def body(dga_ref, dgb_ref, dcg_ref, dq_ref, dk_ref, dv_ref, dsg_ref, h_ref, g_ref, w_ref, dho_ref, *rest):
        parts = split(rest)
        dh_ref, dproj_ref, u_ref, dg_ref = parts[1]
        _host_begin(plan, parts, steps)

        @pl.when(pl.program_id(0) == 0)
        def _():
            dg_ref[...] = jnp.zeros_like(dg_ref)

        dproj_ref[:, 0:512] = dga_ref[...]
        dproj_ref[:, 512:1024] = dgb_ref[...]
        dproj_ref[:, 1024:1536] = dcg_ref[...]
        dproj_ref[:, 1536:2048] = dq_ref[...]
        dproj_ref[:, 2048:2560] = dk_ref[...].astype(BF16)
        dproj_ref[:, 2560:3072] = dv_ref[...].astype(BF16)
        dproj_ref[:, 3072:3584] = dsg_ref[...]
        du = _nt(dproj_ref[...], w_ref[...])
        x = h_ref[...]
        rstd = lax.rsqrt(jnp.mean(x * x, axis=-1, keepdims=True) + RMS_EPS)
        nrm = x * rstd
        u_ref[...] = (nrm * g_ref[...]).astype(BF16)
        dg_ref[...] += jnp.sum(du * nrm, axis=0, keepdims=True)
        dn = du * g_ref[...]
        dh_ref[...] = dho_ref[...] + rstd * (dn - nrm * jnp.mean(dn * nrm, axis=-1, keepdims=True))
        _host_end(plan, parts, steps)

    half = pl.BlockSpec((tm, 512), lambda i: (i, 0))
    full = pl.BlockSpec((tm, d), lambda i: (i, 0))
    outs = _pcall(
        body, name="inproj_bwd" if plan is None else "inproj_bwd_reduce", grid=(steps,),
        in_specs=[half] * 7 + [full, pl.BlockSpec((1, d), lambda i: (0, 0)),
                               pl.BlockSpec((d, n), lambda i: (0, 0)), full] + x_in_specs,
        out_specs=[full, pl.BlockSpec((tm, n), lambda i: (i, 0)), full,
                   pl.BlockSpec((1, d), lambda i: (0, 0))] + x_out_specs,
        out_shape=[jax.ShapeDtypeStruct((lp, d), F32), jax.ShapeDtypeStruct((lp, n), BF16),
                   jax.ShapeDtypeStruct((lp, d), BF16), jax.ShapeDtypeStruct((1, d), F32)] + x_out_shapes,
        scratch_shapes=x_scratch,
        compiler_params=_params("arbitrary"),
    )(dga, dgb, dcg, dq, dk, dv, dsg, h, g, w, dh_out, *x_args)
    return outs[0], outs[1], outs[2], outs[3], outs[4:]


def _row_split(m, parts):
    tm = m // parts
    assert tm * parts == m and tm % 16 == 0, (m, parts)
    return tm


def _matmul_tn(x, dy, tn, name):
    m, k = x.shape
    n = dy.shape[1]
    steps = 4 if m % 64 == 0 else 1
    tm = _row_split(m, steps)

    def body(x_ref, dy_ref, o_ref, acc_ref):
        r = pl.program_id(1)

        @pl.when(r == 0)
        def _():
            acc_ref[...] = jnp.zeros_like(acc_ref)

        acc_ref[...] += _tn(x_ref[...], dy_ref[...])

        @pl.when(r == steps - 1)
        def _():
            o_ref[...] = acc_ref[...].astype(BF16)

    return _pcall(
        body, name=name, grid=(n // tn, steps),
        in_specs=[pl.BlockSpec((tm, k), lambda j, r: (r, 0)), pl.BlockSpec((tm, tn), lambda j, r: (r, j))],
        out_specs=pl.BlockSpec((k, tn), lambda j, r: (0, j)),
        out_shape=jax.ShapeDtypeStruct((k, n), BF16),
        scratch_shapes=[pltpu.VMEM((k, tn), F32)],
        compiler_params=_params("parallel", "arbitrary"),
    )(x, dy)


def _local_step(h0, target_p, seq, vecs, depth, all_weights=None, w_in0=None, gather_w_in=None, gather_rest=None,
                reduce_layer=None):
    pre_g, post_g, conv_b, ln_g, ln_b, b_pw2 = vecs
    ar = jnp.arange(ATT_BLOCK)
    tri = (ar[:, None] > ar[None, :]).astype(BF16)
    upper = (ar[:, None] < ar[None, :]).astype(BF16)
    row = lambda a, l: a[l][None, :]

    hosted = all_weights is None
    weights = [None] * depth if hosted else list(all_weights)
    next_w_in, next_rest = w_in0, None
    saved = []
    h = h0
    for l in range(depth):
        more = hosted and l + 1 < depth
        if hosted:
            ew, qkv, rest = _inproj(h, row(pre_g, l), next_w_in, gather_rest(0) if l == 0 else None)
            w_in, (w_pw2, w_out, conv_w) = next_w_in, (rest if l == 0 else next_rest)
        else:
            w_in, w_pw2, w_out, conv_w = weights[l]
            ew, qkv, _ = _inproj(h, row(pre_g, l), w_in)
        conv_w = jnp.repeat(conv_w, 8, axis=0)
        weights[l] = (w_in, w_pw2, w_out, conv_w)
        c1, c4, c5, next_rest = _conv_fwd(ew, conv_w, row(conv_b, l), row(ln_g, l), row(ln_b, l), w_pw2,
                                          row(b_pw2, l), gather_rest(l + 1) if more else None)
        att, carries, gathered = _attn_fwd(qkv, tri, gather_w_in(l + 1) if more else None)
        if more:
            next_w_in = gathered[0]
        hn, cat, mixed = _outproj(c5, att, ew, h, w_out, row(post_g, l))
        saved.append((h, ew, qkv, c1, c4, att, carries, cat, mixed))
        h = hn

    dh, loss = _loss_head(h, target_p, seq)

    vec_grads = [None] * depth
    mat_grads = [None] * depth
    pending = None

    def pieces(w_in_pieces, rest_pieces):
        n_rest = len(rest_pieces) // 2
        return [w_in_pieces[0], *rest_pieces[:n_rest], w_in_pieces[1], *rest_pieces[n_rest:]]

    for l in reversed(range(depth)):
        w_in, w_pw2, w_out, conv_w = weights[l]
        h_in, ew, qkv, c1, c4, att, carries, cat, mixed = saved[l]
        dmix, datt, dsg, dc4, dcg, dpost, dbpw2 = _outproj_bwd(dh, mixed, row(post_g, l), w_out, att, ew, c4)
        dw_out = _matmul_tn(cat, dmix, 1024, "dw_out")
        dq, dk, dv, landed = _attn_bwd(qkv, carries, datt, tri, upper, pending)
        if pending is not None:
            mat_grads[l + 1] = pieces(landed, rest_landed)
        dga, dgb, c3, dcw, dcb, dlng, dlnb = _conv_bwd(dc4, c1, ew, conv_w, row(ln_g, l), row(ln_b, l), w_pw2)
        dw_pw2 = _matmul_tn(c3, dc4, 512, "dw_pw2")
        rest_plan = None if reduce_layer is None else reduce_layer([dw_pw2, dw_out, dcw], 1)
        dh, dproj, u, dpre, rest_landed = _inproj_bwd(dga, dgb, dcg, dq, dk, dv, dsg, h_in, row(pre_g, l), w_in, dh,
                                                      rest_plan)
        dw_in = _matmul_tn(u, dproj, 1792, "dw_in")
        vec_grads[l] = (dpre[0], dpost[0], dcb[0], dlng[0], dlnb[0], dbpw2[0])
        if reduce_layer is None:
            mat_grads[l] = (dw_in, dw_pw2, dw_out, dcw)
        else:
            pending = reduce_layer([dw_in], 0)
    if pending is not None:
        mat_grads[0] = pieces(_run_exchange(pending, "reduce_grads"), rest_landed)

    vec_grads = [jnp.stack([g[k] for g in vec_grads]) for k in range(len(vecs))]
    return loss[0, 0], dh, vec_grads, mat_grads


N_CHIPS = 4
ANY = pl.BlockSpec(memory_space=pl.ANY)


def _chip_peers():
    x, y, c = lax.axis_index("x"), lax.axis_index("y"), lax.axis_index("c")
    return x, y, c, [(x, 1 - y), (1 - x, y), (1 - x, 1 - y)]


def _shard_slices(refs, dims, idx):
    out = []
    for ref, (axis, size) in zip(refs, dims):
        assert size % LANES == 0
        start = pl.multiple_of(idx * size, LANES)
        sl = [slice(None)] * len(ref.shape)
        sl[axis] = pl.ds(start, size)
        out.append(ref.at[tuple(sl)])
    return out


class _Exchange(NamedTuple):
    inputs: list
    out_shapes: list
    scratch: list
    start: Callable
    relay: Callable
    finish: Callable


def _run_exchange(plan, name):
    n_in, n_out = len(plan.inputs), len(plan.out_shapes)

    def body(*refs):
        parts = refs[:n_in], refs[n_in:n_in + n_out], refs[n_in + n_out:]
        plan.start(*parts)
        plan.relay(*parts)
        plan.finish(*parts)

    return _pcall(body, name=name, in_specs=[ANY] * n_in, out_specs=[ANY] * n_out, out_shape=plan.out_shapes,
                  scratch_shapes=plan.scratch)(*plan.inputs)


def _gather_plan(shards, dims):
    n = len(shards)
    full_shapes = []
    halves = []
    for s, (axis, size) in zip(shards, dims):
        shp = list(s.shape)
        shp[axis] = size * N_CHIPS
        full_shapes.append(jax.ShapeDtypeStruct(tuple(shp), s.dtype))
        tile_rows = 32 // s.dtype.itemsize
        assert s.shape[0] % (2 * tile_rows) == 0
        halves.append((s.shape[0] // 2, tile_rows))

    def half(refs, which):
        return [r.at[pl.ds(pl.multiple_of(which * h, t), h)] for r, (h, t) in zip(refs, halves)]

    def copies(srcs, outs, sems):
        send, recv, loc = sems
        x, y, c, peers = _chip_peers()
        sibling = (x, y, 1 - c)
        mine = _shard_slices(outs, dims, 2 * x + y)
        local = [pltpu.make_async_copy(s, d, loc.at[a]) for a, (s, d) in enumerate(zip(srcs, mine))]

        def remote(src, dst, slot, a, dev):
            return pltpu.make_async_remote_copy(src, dst, send.at[slot, a], recv.at[slot, a],
                                                device_id=dev, device_id_type=MESH)

        sends = [remote(s, d, k, a, (px, py, c))
                 for k, (px, py) in enumerate(peers) for a, (s, d) in enumerate(zip(half(srcs, c), half(mine, c)))]
        theirs = [_shard_slices(outs, dims, 2 * px + py) for px, py in peers]
        arrivals = [remote(s, d, k, a, (px, py, c))
                    for k, (px, py) in enumerate(peers)
                    for a, (s, d) in enumerate(zip(half(srcs, c), half(theirs[k], c)))]
        passed_on = [remote(d, d, 3 + k, a, sibling) for k in range(3) for a, d in enumerate(half(theirs[k], c))]
        from_sibling = [remote(d, d, 3 + k, a, sibling)
                        for k in range(3) for a, d in enumerate(half(theirs[k], 1 - c))]
        return local, sends, arrivals, passed_on, from_sibling

    def start(srcs, outs, sems):
        local, sends = copies(srcs, outs, sems)[:2]
        for cp in local + sends:
            cp.start()

    def relay(srcs, outs, sems):
        _, _, arrivals, passed_on, _ = copies(srcs, outs, sems)
        for arrived, onward in zip(arrivals, passed_on):
            arrived.wait_recv()
            onward.start()

    def finish(srcs, outs, sems):
        local, sends, _, passed_on, from_sibling = copies(srcs, outs, sems)
        for cp in from_sibling:
            cp.wait_recv()
        for cp in sends + passed_on:
            cp.wait_send()
        for cp in local:
            cp.wait()

    scratch = [pltpu.SemaphoreType.DMA((6, n)), pltpu.SemaphoreType.DMA((6, n)), pltpu.SemaphoreType.DMA((n,))]
    return _Exchange(list(shards), full_shapes, scratch, start, relay, finish)


def _reduce_plan(grads, dims):
    n = len(grads)
    piece_shapes = []
    for g, (axis, size) in zip(grads, dims):
        shp = list(g.shape)
        shp[axis] = size
        piece_shapes.append(jax.ShapeDtypeStruct((N_CHIPS,) + tuple(shp), g.dtype))

    def copies(srcs, outs, sems):
        mine, theirs = outs[:n], outs[n:]
        send, recv, loc = sems
        x, y, c, peers = _chip_peers()
        sibling = (x, y, 1 - c)
        own = _shard_slices(srcs, dims, 2 * x + y)

        def remote(src, dst, slot, a, dev):
            return pltpu.make_async_remote_copy(src, dst, send.at[slot, a], recv.at[slot, a],
                                                device_id=dev, device_id_type=MESH)

        local = [pltpu.make_async_copy(own[a], mine[a].at[3], loc.at[a]) for a in range(n)]
        to_sibling = [remote(own[a], theirs[a].at[3], 3, a, sibling) for a in range(n)]
        to_chips = [remote(src, mine[a].at[k], k, a, (px, py, c))
                    for k, (px, py) in enumerate(peers)
                    for a, src in enumerate(_shard_slices(srcs, dims, 2 * px + py))]
        passed_on = [remote(mine[a].at[k], theirs[a].at[k], 4 + k, a, sibling) for k in range(3) for a in range(n)]
        return local, to_sibling, to_chips, passed_on

    def start(srcs, outs, sems):
        local, to_sibling, to_chips, _ = copies(srcs, outs, sems)
        for cp in local + to_sibling + to_chips:
            cp.start()

    def relay(srcs, outs, sems):
        _, _, to_chips, passed_on = copies(srcs, outs, sems)
        for arrived, onward in zip(to_chips, passed_on):
            arrived.wait_recv()
            onward.start()

    def finish(srcs, outs, sems):
        local, to_sibling, to_chips, passed_on = copies(srcs, outs, sems)
        for cp in to_sibling + passed_on:
            cp.wait_recv()
        for cp in to_sibling + to_chips + passed_on:
            cp.wait_send()
        for cp in local:
            cp.wait()

    scratch = [pltpu.SemaphoreType.DMA((7, n)), pltpu.SemaphoreType.DMA((7, n)), pltpu.SemaphoreType.DMA((n,))]
    return _Exchange(list(grads), piece_shapes * 2, scratch, start, relay, finish)


def _allsum_small(pack):
    rows, cols = pack.shape
    ndev = 8

    def body(p_ref, o_ref, buf, send, recv):
        x, y, c = lax.axis_index("x"), lax.axis_index("y"), lax.axis_index("c")
        me = 4 * x + 2 * y + c
        buf[me] = p_ref[...]
        started = []
        for r in range(1, ndev):
            bx, by, bc = (r >> 2) & 1, (r >> 1) & 1, r & 1
            dev = (x ^ bx, y ^ by, c ^ bc)
            cp = pltpu.make_async_remote_copy(p_ref, buf.at[me], send.at[r], recv.at[r],
                                              device_id=dev, device_id_type=MESH)
            cp.start()
            started.append(cp)
        for r in range(1, ndev):
            pltpu.make_async_remote_copy(p_ref, buf.at[me ^ r], send.at[r], recv.at[r],
                                         device_id=(x, y, c), device_id_type=MESH).wait_recv()
        for cp in started:
            cp.wait_send()
        acc = buf[0]
        for d in range(1, ndev):
            acc = acc + buf[d]
        o_ref[...] = acc

    vm = pl.BlockSpec(memory_space=pltpu.VMEM)
    return _pcall(
        body, name="allsum_small", in_specs=[vm], out_specs=vm,
        out_shape=jax.ShapeDtypeStruct((rows, cols), F32),
        scratch_shapes=[pltpu.VMEM((ndev, rows, cols), F32), pltpu.SemaphoreType.DMA((ndev,)),
                        pltpu.SemaphoreType.DMA((ndev,))],
    )(pack)


def _adamw(parts, w, m, v, layer, prev, name):
    _, rows, cols = w.shape
    tr = ROW_TILE if rows % ROW_TILE == 0 else rows
    counts = [p.shape[0] for p in parts]
    n_parts = len(parts)
    n_prev = 0 if prev is None else 4

    def body(*refs):
        part_refs = refs[:n_parts]
        w_ref, m_ref, v_ref = refs[n_parts:n_parts + 3]
        g_ref, d_ref, nm_ref, nv_ref = refs[n_parts + 3 + n_prev:]
        g = None
        for p_ref, cnt in zip(part_refs, counts):
            s = p_ref[0].astype(F32)
            for k in range(1, cnt):
                s = s + p_ref[k].astype(F32)
            g = s if g is None else g + s
        m2 = ADAM_B1 * m_ref[0] + (1.0 - ADAM_B1) * g
        v2 = ADAM_B2 * v_ref[0] + (1.0 - ADAM_B2) * (g * g)
        m_hat = m2 / (1.0 - ADAM_B1 ** ADAM_STEP)
        v_hat = v2 / (1.0 - ADAM_B2 ** ADAM_STEP)
        g_ref[0] = g
        d_ref[0] = -ADAM_LR * (m_hat / (jnp.sqrt(v_hat) + ADAM_EPS) + ADAM_WD * w_ref[0])
        nm_ref[0] = m2
        nv_ref[0] = v2

    blk = pl.BlockSpec((1, tr, cols), lambda i: (layer, i, 0))
    shp = jax.ShapeDtypeStruct(w.shape, F32)
    return _pcall(
        body, name=name, grid=(rows // tr,),
        in_specs=[pl.BlockSpec((cnt, tr, cols), lambda i: (0, i, 0)) for cnt in counts] + [blk] * 3 + [ANY] * n_prev,
        out_specs=[blk] * 4, out_shape=[shp] * 4,
        input_output_aliases={n_parts + 3 + k: k for k in range(n_prev)},
        compiler_params=_params("parallel"),
    )(*parts, w, m, v, *(prev or ()))


def kernel(x, meta_tokens, pre_norm_g, post_norm_g, w_in, conv_w, conv_b, conv_ln_g, conv_ln_b, w_pw2, b_pw2, w_out, loss_target, m_meta_tokens, m_pre_norm_g, m_post_norm_g, m_w_in, m_conv_w, m_conv_b, m_conv_ln_g, m_conv_ln_b, m_w_pw2, m_b_pw2, m_w_out, v_meta_tokens, v_pre_norm_g, v_post_norm_g, v_w_in, v_conv_w, v_conv_b, v_conv_ln_g, v_conv_ln_b, v_w_pw2, v_b_pw2, v_w_out):
    seq, d = x.shape[1], x.shape[2]
    depth = w_in.shape[0]
    length = N_META + seq
    lp = -(-length // ATT_BLOCK) * ATT_BLOCK
    tap_pad = ((0, 0), (0, CONV_PAD - CONV_WIDTH), (0, 0))

    shards = (w_in.astype(BF16), w_pw2.astype(BF16), w_out.astype(BF16), jnp.pad(conv_w, tap_pad))
    dims = [(1, w_in.shape[2]), (0, w_pw2.shape[1]), (0, w_out.shape[1]), (1, conv_w.shape[2])]
    layer_shards = lambda l: [s[l] for s in shards]

    w_in0, meta_f = _run_exchange(_gather_plan([shards[0][0], meta_tokens], [dims[0], (1, meta_tokens.shape[1])]),
                                  "gather_weights")

    h0 = jnp.concatenate([meta_f, x[0], jnp.zeros((lp - length, d), F32)], axis=0)
    target_p = jnp.pad(loss_target[0], ((N_META, lp - length), (0, 0)))
    vecs = (pre_norm_g, post_norm_g, conv_b, conv_ln_g, conv_ln_b, b_pw2)
    loss, dh0, vec_grads, pieces = _local_step(
        h0, target_p, seq, vecs, depth, w_in0=w_in0,
        gather_w_in=lambda l: _gather_plan([shards[0][l]], dims[:1]),
        gather_rest=lambda l: _gather_plan(layer_shards(l)[1:], dims[1:]),
        reduce_layer=lambda grads, first: _reduce_plan(list(grads), dims[first:first + len(grads)]))

    def update(k, w, m, v, name):
        outs = None
        for l in reversed(range(depth)):
            outs = _adamw([pieces[l][k], pieces[l][4 + k]], w, m, v, l, outs, name)
        return outs

    up_w_in = update(0, w_in, m_w_in, v_w_in, "adamw_w_in")
    up_w_pw2 = update(1, w_pw2, m_w_pw2, v_w_pw2, "adamw_w_pw2")
    up_w_out = update(2, w_out, m_w_out, v_w_out, "adamw_w_out")
    up_conv_w = [o[:, :CONV_WIDTH] for o in update(3, jnp.pad(conv_w, tap_pad), jnp.pad(m_conv_w, tap_pad),
                                                   jnp.pad(v_conv_w, tap_pad, constant_values=1.0), "adamw_conv_w")]

    two = lambda a: a.reshape(-1, d)
    vec_rows = [two(g) for g in vec_grads]
    n_vec = sum(a.shape[0] for a in vec_rows)
    pack = jnp.concatenate(vec_rows + [dh0[:N_META], jnp.full((8, d), loss, F32)], axis=0)
    pack = jnp.pad(pack, ((0, -pack.shape[0] % 8), (0, 0)))
    tot = _allsum_small(pack)
    loss_all = tot[n_vec + N_META, 0]

    cat = lambda arrs: jnp.concatenate([two(t) for t in arrs], axis=0)[None]
    small_m = (m_pre_norm_g, m_post_norm_g, m_conv_b, m_conv_ln_g, m_conv_ln_b, m_b_pw2)
    small_v = (v_pre_norm_g, v_post_norm_g, v_conv_b, v_conv_ln_g, v_conv_ln_b, v_b_pw2)
    up_small = _adamw([tot[None, :n_vec]], cat(vecs), cat(small_m), cat(small_v), 0, None, "adamw_vectors")

    def unpack(o):
        res, r0 = [], 0
        for t in vecs:
            nrow = t.size // d
            res.append(o[0, r0:r0 + nrow].reshape(t.shape))
            r0 += nrow
        return res

    up_small = [unpack(o) for o in up_small]
    chip = 2 * lax.axis_index("x") + lax.axis_index("y")
    mcols = meta_tokens.shape[1]
    g_meta = lax.dynamic_slice_in_dim(tot[n_vec:n_vec + N_META], chip * mcols, mcols, axis=1)
    up_meta = [o[0] for o in _adamw([g_meta[None]], meta_tokens[None], m_meta_tokens[None], v_meta_tokens[None],
                                    0, None, "adamw_meta")]

    grad_x = dh0[N_META:length][None]
    outs = [loss_all, grad_x]
    for j in range(4):
        pre, post, cb, lg, lb, bp = up_small[j]
        outs += [up_meta[j], pre, post, up_w_in[j], up_conv_w[j], cb, lg, lb, up_w_pw2[j], bp, up_w_out[j]]
    return tuple(outs)
```

```python
from typing import Callable, NamedTuple

import jax
import jax.numpy as jnp
from jax import lax
from jax.experimental import pallas as pl
from jax.experimental.pallas import tpu as pltpu

F32 = jnp.float32
BF16 = jnp.bfloat16

N_META = 16
D_CONV = 512
D_SB = 512
HEAD_DIM = 64
CONV_WIDTH = 31
CONV_PAD = 32
CONV_ROWS = 128
CONV_ROWS_DX = 16
CONV_ROWS_DW = 16
RMS_EPS = 1e-6
LN_EPS = 1e-5
Q_SCALE = HEAD_DIM ** -0.5

ADAM_LR = 0.001
ADAM_B1 = 0.9
ADAM_B2 = 0.999
ADAM_EPS = 1e-08
ADAM_WD = 0.01
ADAM_STEP = 10

LANES = 128
ROW_TILE = 256
MM_TILE_MAX = 544
OUT_TILE_MAX = 1088
ATT_BLOCK = 256
ATT_PAIRS = 4
ATT_PAIRS_FWD = 4
VMEM_LIMIT = 56 * 1024 * 1024
EXP_ZERO = -104.0
COUNT_LANE = LANES - 1
RELAY_AT = 0.75

MESH = pl.DeviceIdType.MESH


def _pcall(body, **kw):
    return pl.pallas_call(body, **kw)


def _params(*sem):
    return pltpu.CompilerParams(dimension_semantics=sem, vmem_limit_bytes=VMEM_LIMIT)


def _sigmoid(x):
    return 1.0 / (1.0 + jnp.exp(-x))


def _silu_fwd_bwd(x):
    s = _sigmoid(x)
    return x * s, s * (1.0 + x * (1.0 - s))


def _nt(a, b):
    return lax.dot_general(a, b, (((1,), (1,)), ((), ())), preferred_element_type=F32)


def _tn(a, b):
    return lax.dot_general(a, b, (((0,), (0,)), ((), ())), preferred_element_type=F32)


def _nn(a, b):
    return jnp.dot(a, b, preferred_element_type=F32)


def _mm_tile(rows, cap=None):
    return max(t for t in range(16, (cap or MM_TILE_MAX) + 1, 16) if rows % t == 0)


def _host_begin(plan, parts, steps):
    if plan is not None:
        x_in, _, x_out, _, x_sems = parts

        @pl.when(pl.program_id(0) == 0)
        def _():
            plan.start(x_in, x_out, x_sems)

        @pl.when(pl.program_id(0) == int(RELAY_AT * steps))
        def _():
            plan.relay(x_in, x_out, x_sems)


def _host_end(plan, parts, steps):
    if plan is not None:
        x_in, _, x_out, _, x_sems = parts

        @pl.when(pl.program_id(0) == steps - 1)
        def _():
            plan.finish(x_in, x_out, x_sems)


def _inproj(h, g, w, plan=None):
    lp, d = h.shape
    n = w.shape[1]
    tm = _mm_tile(lp)
    steps = lp // tm
    split, x_args, x_in_specs, x_out_specs, x_out_shapes, x_scratch = _hosted(plan, 2, 0)

    def body(h_ref, g_ref, w_ref, *rest):
        parts = split(rest)
        ew_ref, qkv_ref = parts[1]
        _host_begin(plan, parts, steps)
        x = h_ref[...]
        rstd = lax.rsqrt(jnp.mean(x * x, axis=-1, keepdims=True) + RMS_EPS)
        u = ((x * rstd) * g_ref[...]).astype(BF16)
        p = _nn(u, w_ref[...])
        ew_ref[:, 0:1536] = p[:, 0:1536].astype(BF16)
        ew_ref[:, 1536:2048] = p[:, 3072:3584].astype(BF16)
        qkv_ref[:, 0:512] = (p[:, 1536:2048] * Q_SCALE).astype(BF16)
        qkv_ref[:, 512:1536] = p[:, 2048:3072].astype(BF16)
        _host_end(plan, parts, steps)

    outs = _pcall(
        body, name="inproj_fwd" if plan is None else "inproj_fwd_gather", grid=(steps,),
        in_specs=[pl.BlockSpec((tm, d), lambda i: (i, 0)),
                  pl.BlockSpec((1, d), lambda i: (0, 0)),
                  pl.BlockSpec((d, n), lambda i: (0, 0))] + x_in_specs,
        out_specs=[pl.BlockSpec((tm, 2048), lambda i: (i, 0)),
                   pl.BlockSpec((tm, 1536), lambda i: (i, 0))] + x_out_specs,
        out_shape=[jax.ShapeDtypeStruct((lp, 2048), BF16), jax.ShapeDtypeStruct((lp, 1536), BF16)] + x_out_shapes,
        scratch_shapes=x_scratch,
        compiler_params=_params("parallel" if plan is None else "arbitrary"),
    )(h, g, w, *x_args)
    return outs[0], outs[1], outs[2:]


def _layer_norm_stats(c1):
    mu = jnp.mean(c1, axis=-1, keepdims=True)
    xc = c1 - mu
    var = jnp.mean(xc * xc, axis=-1, keepdims=True)
    rstd = lax.rsqrt(var + LN_EPS)
    return xc * rstd, rstd


def _shifted_copies(window, cols, shifted, tm):
    shifted[0] = window[:, cols]
    rows = tm + CONV_PAD - 8
    for b in range(1, 8):
        shifted[b, 0:rows, :] = window[pl.ds(b, rows), cols]


def _shifted_rows(shifted, shift, tm):
    b = shift % 8
    return shifted[b, pl.ds(pl.multiple_of(shift - b, 8), tm), :]


def _weighted(cw8_ref, j, cols, rows):
    w8 = cw8_ref[pl.ds(pl.multiple_of(j * 8, 8), 8), cols]
    r = rows.shape[0]
    return (rows.reshape(r // 8, 8, LANES) * w8[None]).reshape(r, LANES)


def _conv_fwd(ew, cw, cb, lng, lnb, wpw2, bpw2, plan=None):
    lp = ew.shape[0]
    tm = ROW_TILE
    steps = lp // tm
    split, x_args, x_in_specs, x_out_specs, x_out_shapes, x_scratch = _hosted(plan, 3, 2)

    def body(ew_ref, cw_ref, cb_ref, lng_ref, lnb_ref, w_ref, b_ref, *rest):
        parts = split(rest)
        (c1_ref, c4_ref, c5_ref), (xbuf, shifted) = parts[1], parts[3]
        _host_begin(plan, parts, steps)
        @pl.when(pl.program_id(0) == 0)
        def _():
            xbuf[0:CONV_PAD, :] = jnp.zeros((CONV_PAD, D_CONV), F32)

        ga = ew_ref[:, 0:512].astype(F32)
        gb = ew_ref[:, 512:1024].astype(F32)
        cg = ew_ref[:, 1024:1536].astype(F32)
        xbuf[CONV_PAD:CONV_PAD + tm, :] = ga * _sigmoid(gb)
        for blk in range(D_CONV // LANES):
            cs = slice(blk * LANES, (blk + 1) * LANES)
            _shifted_copies(xbuf, cs, shifted, tm)
            for r0 in range(0, tm, CONV_ROWS):
                acc = jnp.zeros((CONV_ROWS, LANES), F32) + cb_ref[:, cs]
                for j in range(CONV_WIDTH):
                    acc = acc + _weighted(cw_ref, j, cs, _shifted_rows(
                        shifted, r0 + CONV_PAD - (CONV_WIDTH - 1) + j, CONV_ROWS))
                c1_ref[r0:r0 + CONV_ROWS, cs] = acc
        xbuf[0:CONV_PAD, :] = xbuf[tm:tm + CONV_PAD, :]
        xhat, _ = _layer_norm_stats(c1_ref[...])
        c2 = xhat * lng_ref[...] + lnb_ref[...]
        c3 = c2 * _sigmoid(c2)
        c4 = _nn(c3.astype(BF16), w_ref[...]) + b_ref[...]
        c4_ref[...] = c4
        c5_ref[...] = (c4 * (cg * _sigmoid(cg))).astype(BF16)
        _host_end(plan, parts, steps)

    vec = pl.BlockSpec((1, D_CONV), lambda i: (0, 0))
    row = pl.BlockSpec((tm, D_CONV), lambda i: (i, 0))
    outs = _pcall(
        body, name="conv_fwd" if plan is None else "conv_fwd_gather", grid=(steps,),
        in_specs=[pl.BlockSpec((tm, 1536), lambda i: (i, 0)),
                  pl.BlockSpec((8 * CONV_PAD, D_CONV), lambda i: (0, 0)),
                  vec, vec, vec,
                  pl.BlockSpec((D_CONV, D_CONV), lambda i: (0, 0)),
                  vec] + x_in_specs,
        out_specs=[row, row, row] + x_out_specs,
        out_shape=[jax.ShapeDtypeStruct((lp, D_CONV), F32), jax.ShapeDtypeStruct((lp, D_CONV), F32),
                   jax.ShapeDtypeStruct((lp, D_CONV), BF16)] + x_out_shapes,
        scratch_shapes=[pltpu.VMEM((tm + CONV_PAD, D_CONV), F32),
                        pltpu.VMEM((8, tm + CONV_PAD, LANES), F32)] + x_scratch,
        compiler_params=_params("arbitrary"),
    )(ew, cw, cb, lng, lnb, wpw2, bpw2, *x_args)
    return outs[0], outs[1], outs[2], outs[3:]


def _block_sums(x, m01):
    return _nn(x.astype(BF16), m01)


def _attn_masks():
    lane = lax.broadcasted_iota(jnp.int32, (1, LANES), 1)
    row = lax.broadcasted_iota(jnp.int32, (2 * ATT_BLOCK, ATT_BLOCK), 0)
    col = lax.broadcasted_iota(jnp.int32, (2 * ATT_BLOCK, ATT_BLOCK), 1)
    return lane < HEAD_DIM, col < (row & (ATT_BLOCK - 1))


def _stack_heads(x, first_head):
    zero = jnp.zeros_like(x)
    return jnp.concatenate([jnp.where(first_head, x, zero), jnp.where(first_head, zero, x)], axis=0)


def _unstack_heads(x2, first_head):
    rows = x2.shape[0] // 2
    return jnp.where(first_head, x2[:rows], x2[rows:])


def _hosted(plan, n_out, n_scratch):
    n_in = 0 if plan is None else len(plan.inputs)
    n_x = 0 if plan is None else len(plan.out_shapes)

    def split(rest):
        a, b, c = n_in + n_out, n_in + n_out + n_x, n_in + n_out + n_x + n_scratch
        return rest[:n_in], rest[n_in:a], rest[a:b], rest[b:c], rest[c:]

    if plan is None:
        return split, [], [], [], [], []
    return split, list(plan.inputs), [ANY] * n_in, [ANY] * n_x, list(plan.out_shapes), list(plan.scratch)


def _attn_fwd(qkv, tri, plan=None):
    lp = qkv.shape[0]
    bq = ATT_BLOCK
    ngrp = ATT_PAIRS_FWD
    nstep = D_SB // (LANES * ngrp)
    nq = lp // bq
    assert nq <= COUNT_LANE
    split, x_args, x_in_specs, x_out_specs, x_out_shapes, x_scratch = _hosted(plan, 2, 3)

    def body(q_ref, k_ref, v_ref, tri_ref, *rest):
        x_in, (o_ref, carry_ref), x_out, (c_s, acc_s, cm_s), x_sems = split(rest)
        i = pl.program_id(1)
        if plan is not None:
            @pl.when(jnp.logical_and(pl.program_id(0) == 0, i == 0))
            def _():
                plan.start(x_in, x_out, x_sems)

            @pl.when(jnp.logical_and(pl.program_id(0) == nstep - 1, i == int(RELAY_AT * nq)))
            def _():
                plan.relay(x_in, x_out, x_sems)

        first_head, vis = _attn_masks()
        lane = lax.broadcasted_iota(jnp.int32, (1, LANES), 1)
        cols = [slice(g * LANES, (g + 1) * LANES) for g in range(ngrp)]
        q2s = [_stack_heads(q_ref[:, cs], first_head) for cs in cols]
        tri_m = tri_ref[...]

        c_s[...] = jnp.zeros_like(c_s)
        acc_s[...] = jnp.zeros_like(acc_s)
        cm_s[...] = jnp.zeros_like(cm_s)

        def blocks(js, masks):
            offs = [pl.multiple_of(j * bq, bq) for j in js]
            work = [(g, b) for b in range(len(js)) for g in range(ngrp)]
            zs = {(g, b): _nt(q2s[g], k_ref[pl.ds(offs[b], bq), cols[g]]) for g, b in work}
            lss = {}
            for g, b in work:
                z = zs[g, b]
                ls = -(jnp.maximum(z, 0.0) + jnp.log(1.0 + jnp.exp(-jnp.abs(z))))
                lss[g, b] = ls if masks[b] is None else jnp.where(masks[b], ls, 0.0)
            tails = {gb: _block_sums(lss[gb], tri_m) for gb in work}
            probs = {}
            carry = [c_s[g] for g in range(ngrp)]
            saved = [cm_s[g] for g in range(ngrp)]
            for g, b in work:
                a = jnp.exp(zs[g, b] + lss[g, b] + tails[g, b] + carry[g])
                probs[g, b] = (a if masks[b] is None else jnp.where(masks[b], a, 0.0)).astype(BF16)
                saved[g] = jnp.where(lane == js[b], carry[g], saved[g])
                carry[g] = carry[g] + tails[g, b][:, 0:1] + lss[g, b][:, 0:1]
            top = None
            for g in range(ngrp):
                c_s[g] = carry[g]
                cm_s[g] = saved[g]
                acc = acc_s[g]
                for b in range(len(js)):
                    acc = acc + _nn(probs[g, b], v_ref[pl.ds(offs[b], bq), cols[g]])
                acc_s[g] = acc
                top = carry[g] if top is None else jnp.maximum(top, carry[g])
            return jnp.max(top) > EXP_ZERO

        alive = lax.cond(i > 0, lambda: blocks([i, i - 1], [vis, None]), lambda: blocks([i], [vis]))
        rest = jnp.maximum(i - 1, 0)

        def pair(carry):
            t, _ = carry
            j = i - 2 - 2 * t
            return t + 1, blocks([j, j - 1], [None, None])

        trips, alive = lax.while_loop(lambda ca: jnp.logical_and(ca[0] < rest // 2, ca[1]), pair, (0, alive))
        last = jnp.logical_and(jnp.logical_and(rest % 2 == 1, trips == rest // 2), alive)

        @pl.when(last)
        def _():
            blocks([0], [None])

        n_done = (jnp.minimum(i + 1, 2) + 2 * trips + last.astype(jnp.int32)).astype(F32)
        for g in range(ngrp):
            cmat = jnp.where(lane == COUNT_LANE, n_done, cm_s[g])
            carry_ref[:, 2 * g * LANES:(2 * g + 1) * LANES] = cmat[:bq]
            carry_ref[:, (2 * g + 1) * LANES:(2 * g + 2) * LANES] = cmat[bq:]
            o_ref[:, cols[g]] = _unstack_heads(acc_s[g], first_head)
        if plan is not None:
            @pl.when(jnp.logical_and(pl.program_id(0) == nstep - 1, i == nq - 1))
            def _():
                plan.finish(x_in, x_out, x_sems)

    width = ngrp * LANES
    outs = _pcall(
        body, name="attn_fwd" if plan is None else "attn_fwd_gather", grid=(nstep, nq),
        in_specs=[pl.BlockSpec((bq, width), lambda p, i: (i, p)),
                  pl.BlockSpec((lp, width), lambda p, i: (0, nstep + p)),
                  pl.BlockSpec((lp, width), lambda p, i: (0, 2 * nstep + p)),
                  pl.BlockSpec((bq, bq), lambda p, i: (0, 0))] + x_in_specs,
        out_specs=[pl.BlockSpec((bq, width), lambda p, i: (i, p)),
                   pl.BlockSpec((bq, 2 * width), lambda p, i: (i, p))] + x_out_specs,
        out_shape=[jax.ShapeDtypeStruct((lp, D_SB), F32), jax.ShapeDtypeStruct((lp, 2 * D_SB), F32)] + x_out_shapes,
        scratch_shapes=[pltpu.VMEM((ngrp, 2 * bq, 1), F32), pltpu.VMEM((ngrp, 2 * bq, LANES), F32),
                        pltpu.VMEM((ngrp, 2 * bq, LANES), F32)] + x_scratch,
        compiler_params=_params("arbitrary", "arbitrary"),
    )(qkv, qkv, qkv, tri, *x_args)
    return outs[0], outs[1], outs[2:]


def _outproj(c5, att, ew, h, w, g):
    lp, d = h.shape
    tm = _mm_tile(lp, OUT_TILE_MAX)

    def body(c5_ref, att_ref, sg_ref, h_ref, w_ref, g_ref, hn_ref, cat_ref, mix_ref):
        sg = sg_ref[...].astype(F32)
        s = att_ref[...] * (sg * _sigmoid(sg))
        cat_ref[:, 0:D_CONV] = c5_ref[...]
        cat_ref[:, D_CONV:] = s.astype(BF16)
        mixed = _nn(cat_ref[...], w_ref[...])
        mix_ref[...] = mixed
        rstd = lax.rsqrt(jnp.mean(mixed * mixed, axis=-1, keepdims=True) + RMS_EPS)
        hn_ref[...] = h_ref[...] + (mixed * rstd) * g_ref[...]

    half = pl.BlockSpec((tm, 512), lambda i: (i, 0))
    full = pl.BlockSpec((tm, d), lambda i: (i, 0))
    return _pcall(
        body, name="outproj_fwd", grid=(lp // tm,),
        in_specs=[half, half, pl.BlockSpec((tm, 512), lambda i: (i, 3)), full,
                  pl.BlockSpec((d, d), lambda i: (0, 0)), pl.BlockSpec((1, d), lambda i: (0, 0))],
        out_specs=[full, full, full],
        out_shape=[jax.ShapeDtypeStruct((lp, d), F32), jax.ShapeDtypeStruct((lp, d), BF16),
                   jax.ShapeDtypeStruct((lp, d), F32)],
        compiler_params=_params("parallel"),
    )(c5, att, ew, h, w, g)


def _loss_head(h, target, seq):
    lp, d = h.shape
    tm = ROW_TILE

    def body(h_ref, t_ref, dh_ref, loss_ref):
        i = pl.program_id(0)

        @pl.when(i == 0)
        def _():
            loss_ref[...] = jnp.zeros_like(loss_ref)

        row = i * tm + lax.broadcasted_iota(jnp.int32, (tm, 1), 0)
        real = jnp.logical_and(row >= N_META, row < N_META + seq)
        diff = jnp.where(real, h_ref[...] - t_ref[...], 0.0)
        dh_ref[...] = diff * (1.0 / d)
        loss_ref[...] += 0.5 * jnp.sum(jnp.sum(diff * diff, axis=-1, keepdims=True) * (1.0 / d))

    full = pl.BlockSpec((tm, d), lambda i: (i, 0))
    return _pcall(
        body, name="loss_head", grid=(lp // tm,),
        in_specs=[full, full],
        out_specs=[full, pl.BlockSpec((8, LANES), lambda i: (0, 0))],
        out_shape=[jax.ShapeDtypeStruct((lp, d), F32), jax.ShapeDtypeStruct((8, LANES), F32)],
        compiler_params=_params("arbitrary"),
    )(h, target)


def _outproj_bwd(dh, mixed, g, w, att, ew, c4):
    lp, d = dh.shape
    tm = _mm_tile(lp, OUT_TILE_MAX)

    def body(dh_ref, mix_ref, g_ref, w_ref, att_ref, cg_ref, sg_ref, c4_ref,
             dmix_ref, datt_ref, dsg_ref, dc4_ref, dcg_ref, dg_ref, db_ref):
        @pl.when(pl.program_id(0) == 0)
        def _():
            dg_ref[...] = jnp.zeros_like(dg_ref)
            db_ref[...] = jnp.zeros_like(db_ref)

        mixed = mix_ref[...]
        dhv = dh_ref[...]
        rstd = lax.rsqrt(jnp.mean(mixed * mixed, axis=-1, keepdims=True) + RMS_EPS)
        n = mixed * rstd
        dg_ref[...] += jnp.sum(dhv * n, axis=0, keepdims=True)
        dn = dhv * g_ref[...]
        dmix = (rstd * (dn - n * jnp.mean(dn * n, axis=-1, keepdims=True))).astype(BF16)
        dmix_ref[...] = dmix
        dcat = _nt(dmix, w_ref[...])
        dc5 = dcat[:, 0:D_CONV]
        ds = dcat[:, D_CONV:]
        silu_sg, dsilu_sg = _silu_fwd_bwd(sg_ref[...].astype(F32))
        datt_ref[...] = (ds * silu_sg).astype(BF16)
        dsg_ref[...] = (ds * att_ref[...] * dsilu_sg).astype(BF16)
        silu_cg, dsilu_cg = _silu_fwd_bwd(cg_ref[...].astype(F32))
        dc4 = dc5 * silu_cg
        db_ref[...] += jnp.sum(dc4, axis=0, keepdims=True)
        dc4_ref[...] = dc4.astype(BF16)
        dcg_ref[...] = (dc5 * c4_ref[...] * dsilu_cg).astype(BF16)

    half = pl.BlockSpec((tm, 512), lambda i: (i, 0))
    full = pl.BlockSpec((tm, d), lambda i: (i, 0))
    hb = jax.ShapeDtypeStruct((lp, 512), BF16)
    return _pcall(
        body, name="outproj_bwd", grid=(lp // tm,),
        in_specs=[full, full, pl.BlockSpec((1, d), lambda i: (0, 0)), pl.BlockSpec((d, d), lambda i: (0, 0)),
                  half, pl.BlockSpec((tm, 512), lambda i: (i, 2)), pl.BlockSpec((tm, 512), lambda i: (i, 3)), half],
        out_specs=[full, half, half, half, half,
                   pl.BlockSpec((1, d), lambda i: (0, 0)), pl.BlockSpec((1, 512), lambda i: (0, 0))],
        out_shape=[jax.ShapeDtypeStruct((lp, d), BF16), hb, hb, hb, hb,
                   jax.ShapeDtypeStruct((1, d), F32), jax.ShapeDtypeStruct((1, 512), F32)],
        compiler_params=_params("arbitrary"),
    )(dh, mixed, g, w, att, ew, ew, c4)


def _attn_bwd(qkv, carries, datt, tri, upper, plan=None):
    lp = qkv.shape[0]
    bq = ATT_BLOCK
    ngrp = ATT_PAIRS
    nstep = D_SB // (LANES * ngrp)
    nq = lp // bq
    split, x_args, x_in_specs, x_out_specs, x_out_shapes, x_scratch = _hosted(plan, 3, 2)

    def body(q_ref, k_ref, v_ref, carry_ref, do_ref, tri_ref, upper_ref, *rest):
        x_in, (dq_ref, dk_ref, dv_ref), x_out, (run_s, dq_s), x_sems = split(rest)
        i = pl.program_id(1)
        if plan is not None:
            @pl.when(jnp.logical_and(pl.program_id(0) == 0, i == 0))
            def _():
                plan.start(x_in, x_out, x_sems)

            @pl.when(jnp.logical_and(pl.program_id(0) == nstep - 1, i == max(nq - 2, 0)))
            def _():
                plan.relay(x_in, x_out, x_sems)

        @pl.when(i == 0)
        def _():
            dk_ref[...] = jnp.zeros_like(dk_ref)
            dv_ref[...] = jnp.zeros_like(dv_ref)

        first_head, vis = _attn_masks()
        lane = lax.broadcasted_iota(jnp.int32, (1, LANES), 1)
        cols = [slice(g * LANES, (g + 1) * LANES) for g in range(ngrp)]
        q2s = [_stack_heads(q_ref[:, cs], first_head) for cs in cols]
        do2s = [_stack_heads(do_ref[:, cs], first_head) for cs in cols]
        cmats = [jnp.concatenate([carry_ref[:, 2 * g * LANES:(2 * g + 1) * LANES],
                                  carry_ref[:, (2 * g + 1) * LANES:(2 * g + 2) * LANES]], axis=0)
                 for g in range(ngrp)]
        tri_m = tri_ref[...]
        upper_m = upper_ref[...]

        def blocks(js, masks):
            offs = [pl.multiple_of(j * bq, bq) for j in js]
            work = [(g, b) for b in range(len(js)) for g in range(ngrp)]
            zs = {(g, b): _nt(q2s[g], k_ref[pl.ds(offs[b], bq), cols[g]]) for g, b in work}
            lss = {}
            for g, b in work:
                z = zs[g, b]
                ls = -(jnp.maximum(z, 0.0) + jnp.log(1.0 + jnp.exp(-jnp.abs(z))))
                lss[g, b] = ls if masks[b] is None else jnp.where(masks[b], ls, 0.0)
            tails = {gb: _block_sums(lss[gb], tri_m) for gb in work}
            das = {(g, b): _nt(do2s[g], v_ref[pl.ds(offs[b], bq), cols[g]]) for g, b in work}
            probs, des = {}, {}
            for g, b in work:
                c = jnp.sum(jnp.where(lane == js[b], cmats[g], 0.0), axis=-1, keepdims=True)
                a = jnp.exp(zs[g, b] + lss[g, b] + tails[g, b] + c)
                a = a if masks[b] is None else jnp.where(masks[b], a, 0.0)
                probs[g, b] = a.astype(BF16)
                des[g, b] = das[g, b] * a
            prefixes = {gb: _block_sums(des[gb], upper_m) for gb in work}
            runs = [run_s[g] for g in range(ngrp)]
            dzs = {}
            for g, b in work:
                beta = jnp.exp(zs[g, b] + lss[g, b])
                dz = des[g, b] - beta * (des[g, b] + runs[g] + prefixes[g, b])
                dzs[g, b] = (dz if masks[b] is None else jnp.where(masks[b], dz, 0.0)).astype(BF16)
                runs[g] = runs[g] + prefixes[g, b][:, bq - 1:bq] + des[g, b][:, bq - 1:bq]
            for g in range(ngrp):
                run_s[g] = runs[g]
                dq = dq_s[g]
                for b in range(len(js)):
                    rows = pl.ds(offs[b], bq)
                    dq = dq + _nn(dzs[g, b], k_ref[rows, cols[g]])
                    dk_ref[rows, cols[g]] += _tn(dzs[g, b], q2s[g])
                    dv_ref[rows, cols[g]] += _tn(probs[g, b], do2s[g])
                dq_s[g] = dq

        n_done = jnp.max(carry_ref[:, COUNT_LANE:COUNT_LANE + 1]).astype(jnp.int32)
        n_done = jnp.clip(n_done, 1, i + 1)
        before = jnp.maximum(n_done - 2, 0)
        j0 = i - n_done + 1
        odd = before % 2
        run_s[...] = jnp.zeros_like(run_s)
        dq_s[...] = jnp.zeros_like(dq_s)

        @pl.when(odd == 1)
        def _():
            blocks([j0], [None])

        @pl.loop(0, before // 2)
        def _(t):
            blocks([j0 + odd + 2 * t, j0 + odd + 2 * t + 1], [None, None])

        @pl.when(n_done > 1)
        def _():
            blocks([i - 1, i], [None, vis])

        @pl.when(n_done <= 1)
        def _():
            blocks([i], [vis])

        for g in range(ngrp):
            dq_ref[:, cols[g]] = (_unstack_heads(dq_s[g], first_head) * Q_SCALE).astype(BF16)
        if plan is not None:
            @pl.when(jnp.logical_and(pl.program_id(0) == nstep - 1, i == nq - 1))
            def _():
                plan.finish(x_in, x_out, x_sems)

    width = ngrp * LANES
    once = pl.Buffered(1)
    qb = pl.BlockSpec((bq, width), lambda p, i: (i, p))
    colb = pl.BlockSpec((lp, width), lambda p, i: (0, p), pipeline_mode=once)
    sq = pl.BlockSpec((bq, bq), lambda p, i: (0, 0))
    outs = _pcall(
        body, name="attn_bwd" if plan is None else "attn_bwd_reduce", grid=(nstep, nq),
        in_specs=[qb,
                  pl.BlockSpec((lp, width), lambda p, i: (0, nstep + p), pipeline_mode=once),
                  pl.BlockSpec((lp, width), lambda p, i: (0, 2 * nstep + p), pipeline_mode=once),
                  pl.BlockSpec((bq, 2 * width), lambda p, i: (i, p)), qb, sq, sq] + x_in_specs,
        out_specs=[qb, colb, colb] + x_out_specs,
        out_shape=[jax.ShapeDtypeStruct((lp, D_SB), BF16), jax.ShapeDtypeStruct((lp, D_SB), F32),
                   jax.ShapeDtypeStruct((lp, D_SB), F32)] + x_out_shapes,
        scratch_shapes=[pltpu.VMEM((ngrp, 2 * bq, 1), F32), pltpu.VMEM((ngrp, 2 * bq, LANES), F32)] + x_scratch,
        compiler_params=_params("arbitrary", "arbitrary"),
    )(qkv, qkv, qkv, carries, datt, tri, upper, *x_args)
    return outs[0], outs[1], outs[2], outs[3:]


def _conv_bwd(dc4, c1, ew, cw, lng, lnb, wpw2):
    lp = ew.shape[0]
    tm = ROW_TILE
    nt = lp // tm
    halo_per_tile = tm // CONV_PAD

    def body(dc4_ref, c1_ref, ew_ref, halo_ref, cw_ref, lng_ref, lnb_ref, w_ref,
             dga_ref, dgb_ref, c3_ref, dcw_ref, dcb_ref, dlng_ref, dlnb_ref, xbuf, dbuf, shifted, wacc):
        step = pl.program_id(0)

        @pl.when(step == 0)
        def _():
            wacc[...] = jnp.zeros_like(wacc)
            dcb_ref[...] = jnp.zeros_like(dcb_ref)
            dlng_ref[...] = jnp.zeros_like(dlng_ref)
            dlnb_ref[...] = jnp.zeros_like(dlnb_ref)
            dbuf[tm:tm + CONV_PAD, :] = jnp.zeros((CONV_PAD, D_CONV), F32)

        dc3 = _nt(dc4_ref[...], w_ref[...])
        xhat, rstd = _layer_norm_stats(c1_ref[...])
        c2 = xhat * lng_ref[...] + lnb_ref[...]
        c3, dsilu = _silu_fwd_bwd(c2)
        c3_ref[...] = c3.astype(BF16)
        dc2 = dc3 * dsilu
        dlng_ref[...] += jnp.sum(dc2 * xhat, axis=0, keepdims=True)
        dlnb_ref[...] += jnp.sum(dc2, axis=0, keepdims=True)
        dxhat = dc2 * lng_ref[...]
        dc1 = rstd * (dxhat - jnp.mean(dxhat, axis=-1, keepdims=True)
                      - xhat * jnp.mean(dxhat * xhat, axis=-1, keepdims=True))
        dcb_ref[...] += jnp.sum(dc1, axis=0, keepdims=True)
        dbuf[0:tm, :] = dc1

        ga = ew_ref[:, 0:512].astype(F32)
        sgb = _sigmoid(ew_ref[:, 512:1024].astype(F32))
        xbuf[CONV_PAD:CONV_PAD + tm, :] = ga * sgb
        first_tile = step == nt - 1
        halo = halo_ref[:, 0:512].astype(F32) * _sigmoid(halo_ref[:, 512:1024].astype(F32))
        xbuf[0:CONV_PAD, :] = jnp.where(first_tile, 0.0, halo)

        for cb in range(D_CONV // LANES):
            cs = slice(cb * LANES, (cb + 1) * LANES)
            _shifted_copies(dbuf, cs, shifted, tm)
            for r0 in range(0, tm, CONV_ROWS_DX):
                rs = slice(r0, r0 + CONV_ROWS_DX)
                dc0 = jnp.zeros((CONV_ROWS_DX, LANES), F32)
                for j in range(CONV_WIDTH):
                    dc0 = dc0 + _weighted(cw_ref, j, cs, _shifted_rows(
                        shifted, r0 + CONV_WIDTH - 1 - j, CONV_ROWS_DX))
                dga_ref[rs, cs] = (dc0 * sgb[rs, cs]).astype(BF16)
                dgb_ref[rs, cs] = (dc0 * ga[rs, cs] * sgb[rs, cs] * (1.0 - sgb[rs, cs])).astype(BF16)
            _shifted_copies(xbuf, cs, shifted, tm)
            for r0 in range(0, tm, CONV_ROWS_DW):
                d1 = dbuf[r0:r0 + CONV_ROWS_DW, cs]

                for j in range(CONV_WIDTH):
                    prod = d1 * _shifted_rows(shifted, r0 + CONV_PAD - (CONV_WIDTH - 1) + j, CONV_ROWS_DW)
                    wacc[j * 8:(j + 1) * 8, cs] += jnp.sum(prod.reshape(CONV_ROWS_DW // 8, 8, LANES), axis=0)
        dbuf[tm:tm + CONV_PAD, :] = dbuf[0:CONV_PAD, :]

        @pl.when(step == nt - 1)
        def _():
            dcw_ref[...] = jnp.sum(wacc[...].reshape(CONV_PAD, 8, D_CONV), axis=1)

    rev = lambda i: (nt - 1 - i, 0)
    row = pl.BlockSpec((tm, D_CONV), rev)
    vec = pl.BlockSpec((1, D_CONV), lambda i: (0, 0))
    hb = jax.ShapeDtypeStruct((lp, D_CONV), BF16)
    vs = jax.ShapeDtypeStruct((1, D_CONV), F32)
    return _pcall(
        body, name="conv_bwd", grid=(nt,),
        in_specs=[row, row, pl.BlockSpec((tm, 1024), rev),
                  pl.BlockSpec((CONV_PAD, 1024), lambda i: (jnp.maximum((nt - 1 - i) * halo_per_tile - 1, 0), 0)),
                  pl.BlockSpec((8 * CONV_PAD, D_CONV), lambda i: (0, 0)), vec, vec,
                  pl.BlockSpec((D_CONV, D_CONV), lambda i: (0, 0))],
        out_specs=[row, row, row, pl.BlockSpec((CONV_PAD, D_CONV), lambda i: (0, 0)), vec, vec, vec],
        out_shape=[hb, hb, hb, jax.ShapeDtypeStruct((CONV_PAD, D_CONV), F32), vs, vs, vs],
        scratch_shapes=[pltpu.VMEM((tm + CONV_PAD, D_CONV), F32), pltpu.VMEM((tm + CONV_PAD, D_CONV), F32),
                        pltpu.VMEM((8, tm + CONV_PAD, LANES), F32), pltpu.VMEM((8 * CONV_PAD, D_CONV), F32)],
        compiler_params=_params("arbitrary"),
    )(dc4, c1, ew, ew, cw, lng, lnb, wpw2)


def _inproj_bwd(dga, dgb, dcg, dq, dk, dv, dsg, h, g, w, dh_out, plan=None):
    lp, d = h.shape
    n = w.shape[1]
    tm = _mm_tile(lp)
    steps = lp // tm
    split, x_args, x_in_specs, x_out_specs, x_out_shapes, x_scratch = _hosted(plan, 4, 0)

    def body(dga_ref, dgb_ref, dcg_ref, dq_ref, dk_ref, dv_ref, dsg_ref, h_ref, g_ref, w_ref, dho_ref, *rest):
        parts = split(rest)
        dh_ref, dproj_ref, u_ref, dg_ref = parts[1]
        _host_begin(plan, parts, steps)

        @pl.when(pl.program_id(0) == 0)
        def _():
            dg_ref[...] = jnp.zeros_like(dg_ref)

        dproj_ref[:, 0:512] = dga_ref[...]
        dproj_ref[:, 512:1024] = dgb_ref[...]
        dproj_ref[:, 1024:1536] = dcg_ref[...]
        dproj_ref[:, 1536:2048] = dq_ref[...]
        dproj_ref[:, 2048:2560] = dk_ref[...].astype(BF16)
        dproj_ref[:, 2560:3072] = dv_ref[...].astype(BF16)
        dproj_ref[:, 3072:3584] = dsg_ref[...]
        du = _nt(dproj_ref[...], w_ref[...])
        x = h_ref[...]
        rstd = lax.rsqrt(jnp.mean(x * x, axis=-1, keepdims=True) + RMS_EPS)
        nrm = x * rstd
        u_ref[...] = (nrm * g_ref[...]).astype(BF16)
        dg_ref[...] += jnp.sum(du * nrm, axis=0, keepdims=True)
        dn = du * g_ref[...]
        dh_ref[...] = dho_ref[...] + rstd * (dn - nrm * jnp.mean(dn * nrm, axis=-1, keepdims=True))
        _host_end(plan, parts, steps)

    half = pl.BlockSpec((tm, 512), lambda i: (i, 0))
    full = pl.BlockSpec((tm, d), lambda i: (i, 0))
    outs = _pcall(
        body, name="inproj_bwd" if plan is None else "inproj_bwd_reduce", grid=(steps,),
        in_specs=[half] * 7 + [full, pl.BlockSpec((1, d), lambda i: (0, 0)),
                               pl.BlockSpec((d, n), lambda i: (0, 0)), full] + x_in_specs,
        out_specs=[full, pl.BlockSpec((tm, n), lambda i: (i, 0)), full,
                   pl.BlockSpec((1, d), lambda i: (0, 0))] + x_out_specs,
        out_shape=[jax.ShapeDtypeStruct((lp, d), F32), jax.ShapeDtypeStruct((lp, n), BF16),
                   jax.ShapeDtypeStruct((lp, d), BF16), jax.ShapeDtypeStruct((1, d), F32)] + x_out_shapes,
        scratch_shapes=x_scratch,
        compiler_params=_params("arbitrary"),
    )(dga, dgb, dcg, dq, dk, dv, dsg, h, g, w, dh_out, *x_args)
    return outs[0], outs[1], outs[2], outs[3], outs[4:]


def _row_split(m, parts):
    tm = m // parts
    assert tm * parts == m and tm % 16 == 0, (m, parts)
    return tm


def _matmul_tn(x, dy, tn, name):
    m, k = x.shape
    n = dy.shape[1]
    steps = 2 if m % 32 == 0 else 1
    tm = _row_split(m, steps)

    def body(x_ref, dy_ref, o_ref, acc_ref):
        r = pl.program_id(1)

        @pl.when(r == 0)
        def _():
            acc_ref[...] = jnp.zeros_like(acc_ref)

        acc_ref[...] += _tn(x_ref[...], dy_ref[...])

        @pl.when(r == steps - 1)
        def _():
            o_ref[...] = acc_ref[...].astype(BF16)

    return _pcall(
        body, name=name, grid=(n // tn, steps),
        in_specs=[pl.BlockSpec((tm, k), lambda j, r: (r, 0)), pl.BlockSpec((tm, tn), lambda j, r: (r, j))],
        out_specs=pl.BlockSpec((k, tn), lambda j, r: (0, j)),
        out_shape=jax.ShapeDtypeStruct((k, n), BF16),
        scratch_shapes=[pltpu.VMEM((k, tn), F32)],
        compiler_params=_params("parallel", "arbitrary"),
    )(x, dy)


def _local_step(h0, target_p, seq, vecs, depth, all_weights=None, w_in0=None, gather_w_in=None, gather_rest=None,
                reduce_layer=None):
    pre_g, post_g, conv_b, ln_g, ln_b, b_pw2 = vecs
    ar = jnp.arange(ATT_BLOCK)
    tri = (ar[:, None] > ar[None, :]).astype(BF16)
    upper = (ar[:, None] < ar[None, :]).astype(BF16)
    row = lambda a, l: a[l][None, :]

    hosted = all_weights is None
    weights = [None] * depth if hosted else list(all_weights)
    next_w_in, next_rest = w_in0, None
    saved = []
    h = h0
    for l in range(depth):
        more = hosted and l + 1 < depth
        if hosted:
            ew, qkv, rest = _inproj(h, row(pre_g, l), next_w_in, gather_rest(0) if l == 0 else None)
            w_in, (w_pw2, w_out, conv_w) = next_w_in, (rest if l == 0 else next_rest)
        else:
            w_in, w_pw2, w_out, conv_w = weights[l]
            ew, qkv, _ = _inproj(h, row(pre_g, l), w_in)
        conv_w = jnp.repeat(conv_w, 8, axis=0)
        weights[l] = (w_in, w_pw2, w_out, conv_w)
        c1, c4, c5, next_rest = _conv_fwd(ew, conv_w, row(conv_b, l), row(ln_g, l), row(ln_b, l), w_pw2,
                                          row(b_pw2, l), gather_rest(l + 1) if more else None)
        att, carries, gathered = _attn_fwd(qkv, tri, gather_w_in(l + 1) if more else None)
        if more:
            next_w_in = gathered[0]
        hn, cat, mixed = _outproj(c5, att, ew, h, w_out, row(post_g, l))
        saved.append((h, ew, qkv, c1, c4, att, carries, cat, mixed))
        h = hn

    dh, loss = _loss_head(h, target_p, seq)

    vec_grads = [None] * depth
    mat_grads = [None] * depth
    pending = None

    def pieces(w_in_pieces, rest_pieces):
        n_rest = len(rest_pieces) // 2
        return [w_in_pieces[0], *rest_pieces[:n_rest], w_in_pieces[1], *rest_pieces[n_rest:]]

    for l in reversed(range(depth)):
        w_in, w_pw2, w_out, conv_w = weights[l]
        h_in, ew, qkv, c1, c4, att, carries, cat, mixed = saved[l]
        dmix, datt, dsg, dc4, dcg, dpost, dbpw2 = _outproj_bwd(dh, mixed, row(post_g, l), w_out, att, ew, c4)
        dw_out = _matmul_tn(cat, dmix, 1024, "dw_out")
        dq, dk, dv, landed = _attn_bwd(qkv, carries, datt, tri, upper, pending)
        if pending is not None:
            mat_grads[l + 1] = pieces(landed, rest_landed)
        dga, dgb, c3, dcw, dcb, dlng, dlnb = _conv_bwd(dc4, c1, ew, conv_w, row(ln_g, l), row(ln_b, l), w_pw2)
        dw_pw2 = _matmul_tn(c3, dc4, 512, "dw_pw2")
        rest_plan = None if reduce_layer is None else reduce_layer([dw_pw2, dw_out, dcw], 1)
        dh, dproj, u, dpre, rest_landed = _inproj_bwd(dga, dgb, dcg, dq, dk, dv, dsg, h_in, row(pre_g, l), w_in, dh,
                                                      rest_plan)
        dw_in = _matmul_tn(u, dproj, 1792, "dw_in")
        vec_grads[l] = (dpre[0], dpost[0], dcb[0], dlng[0], dlnb[0], dbpw2[0])
        if reduce_layer is None:
            mat_grads[l] = (dw_in, dw_pw2, dw_out, dcw)
        else:
            pending = reduce_layer([dw_in], 0)
    if pending is not None:
        mat_grads[0] = pieces(_run_exchange(pending, "reduce_grads"), rest_landed)

    vec_grads = [jnp.stack([g[k] for g in vec_grads]) for k in range(len(vecs))]
    return loss[0, 0], dh, vec_grads, mat_grads


N_CHIPS = 4
ANY = pl.BlockSpec(memory_space=pl.ANY)


def _chip_peers():
    x, y, c = lax.axis_index("x"), lax.axis_index("y"), lax.axis_index("c")
    return x, y, c, [(x, 1 - y), (1 - x, y), (1 - x, 1 - y)]


def _shard_slices(refs, dims, idx):
    out = []
    for ref, (axis, size) in zip(refs, dims):
        assert size % LANES == 0
        start = pl.multiple_of(idx * size, LANES)
        sl = [slice(None)] * len(ref.shape)
        sl[axis] = pl.ds(start, size)
        out.append(ref.at[tuple(sl)])
    return out


class _Exchange(NamedTuple):
    inputs: list
    out_shapes: list
    scratch: list
    start: Callable
    relay: Callable
    finish: Callable


def _run_exchange(plan, name):
    n_in, n_out = len(plan.inputs), len(plan.out_shapes)

    def body(*refs):
        parts = refs[:n_in], refs[n_in:n_in + n_out], refs[n_in + n_out:]
        plan.start(*parts)
        plan.relay(*parts)
        plan.finish(*parts)

    return _pcall(body, name=name, in_specs=[ANY] * n_in, out_specs=[ANY] * n_out, out_shape=plan.out_shapes,
                  scratch_shapes=plan.scratch)(*plan.inputs)


def _gather_plan(shards, dims):
    n = len(shards)
    full_shapes = []
    halves = []
    for s, (axis, size) in zip(shards, dims):
        shp = list(s.shape)
        shp[axis] = size * N_CHIPS
        full_shapes.append(jax.ShapeDtypeStruct(tuple(shp), s.dtype))
        tile_rows = 32 // s.dtype.itemsize
        assert s.shape[0] % (2 * tile_rows) == 0
        halves.append((s.shape[0] // 2, tile_rows))

    def half(refs, which):
        return [r.at[pl.ds(pl.multiple_of(which * h, t), h)] for r, (h, t) in zip(refs, halves)]

    def copies(srcs, outs, sems):
        send, recv, loc = sems
        x, y, c, peers = _chip_peers()
        sibling = (x, y, 1 - c)
        mine = _shard_slices(outs, dims, 2 * x + y)
        local = [pltpu.make_async_copy(s, d, loc.at[a]) for a, (s, d) in enumerate(zip(srcs, mine))]

        def remote(src, dst, slot, a, dev):
            return pltpu.make_async_remote_copy(src, dst, send.at[slot, a], recv.at[slot, a],
                                                device_id=dev, device_id_type=MESH)

        sends = [remote(s, d, k, a, (px, py, c))
                 for k, (px, py) in enumerate(peers) for a, (s, d) in enumerate(zip(half(srcs, c), half(mine, c)))]
        theirs = [_shard_slices(outs, dims, 2 * px + py) for px, py in peers]
        arrivals = [remote(s, d, k, a, (px, py, c))
                    for k, (px, py) in enumerate(peers)
                    for a, (s, d) in enumerate(zip(half(srcs, c), half(theirs[k], c)))]
        passed_on = [remote(d, d, 3 + k, a, sibling) for k in range(3) for a, d in enumerate(half(theirs[k], c))]
        from_sibling = [remote(d, d, 3 + k, a, sibling)
                        for k in range(3) for a, d in enumerate(half(theirs[k], 1 - c))]
        return local, sends, arrivals, passed_on, from_sibling

    def start(srcs, outs, sems):
        local, sends = copies(srcs, outs, sems)[:2]
        for cp in local + sends:
            cp.start()

    def relay(srcs, outs, sems):
        _, _, arrivals, passed_on, _ = copies(srcs, outs, sems)
        for arrived, onward in zip(arrivals, passed_on):
            arrived.wait_recv()
            onward.start()

    def finish(srcs, outs, sems):
        local, sends, _, passed_on, from_sibling = copies(srcs, outs, sems)
        for cp in from_sibling:
            cp.wait_recv()
        for cp in sends + passed_on:
            cp.wait_send()
        for cp in local:
            cp.wait()

    scratch = [pltpu.SemaphoreType.DMA((6, n)), pltpu.SemaphoreType.DMA((6, n)), pltpu.SemaphoreType.DMA((n,))]
    return _Exchange(list(shards), full_shapes, scratch, start, relay, finish)


def _reduce_plan(grads, dims):
    n = len(grads)
    piece_shapes = []
    for g, (axis, size) in zip(grads, dims):
        shp = list(g.shape)
        shp[axis] = size
        piece_shapes.append(jax.ShapeDtypeStruct((N_CHIPS,) + tuple(shp), g.dtype))

    def copies(srcs, outs, sems):
        mine, theirs = outs[:n], outs[n:]
        send, recv, loc = sems
        x, y, c, peers = _chip_peers()
        sibling = (x, y, 1 - c)
        own = _shard_slices(srcs, dims, 2 * x + y)

        def remote(src, dst, slot, a, dev):
            return pltpu.make_async_remote_copy(src, dst, send.at[slot, a], recv.at[slot, a],
                                                device_id=dev, device_id_type=MESH)

        local = [pltpu.make_async_copy(own[a], mine[a].at[3], loc.at[a]) for a in range(n)]
        to_sibling = [remote(own[a], theirs[a].at[3], 3, a, sibling) for a in range(n)]
        to_chips = [remote(src, mine[a].at[k], k, a, (px, py, c))
                    for k, (px, py) in enumerate(peers)
                    for a, src in enumerate(_shard_slices(srcs, dims, 2 * px + py))]
        passed_on = [remote(mine[a].at[k], theirs[a].at[k], 4 + k, a, sibling) for k in range(3) for a in range(n)]
        return local, to_sibling, to_chips, passed_on

    def start(srcs, outs, sems):
        local, to_sibling, to_chips, _ = copies(srcs, outs, sems)
        for cp in local + to_sibling + to_chips:
            cp.start()

    def relay(srcs, outs, sems):
        _, _, to_chips, passed_on = copies(srcs, outs, sems)
        for arrived, onward in zip(to_chips, passed_on):
            arrived.wait_recv()
            onward.start()

    def finish(srcs, outs, sems):
        local, to_sibling, to_chips, passed_on = copies(srcs, outs, sems)
        for cp in to_sibling + passed_on:
            cp.wait_recv()
        for cp in to_sibling + to_chips + passed_on:
            cp.wait_send()
        for cp in local:
            cp.wait()

    scratch = [pltpu.SemaphoreType.DMA((7, n)), pltpu.SemaphoreType.DMA((7, n)), pltpu.SemaphoreType.DMA((n,))]
    return _Exchange(list(grads), piece_shapes * 2, scratch, start, relay, finish)


def _allsum_small(pack):
    rows, cols = pack.shape
    ndev = 8

    def body(p_ref, o_ref, buf, send, recv):
        x, y, c = lax.axis_index("x"), lax.axis_index("y"), lax.axis_index("c")
        me = 4 * x + 2 * y + c
        buf[me] = p_ref[...]
        started = []
        for r in range(1, ndev):
            bx, by, bc = (r >> 2) & 1, (r >> 1) & 1, r & 1
            dev = (x ^ bx, y ^ by, c ^ bc)
            cp = pltpu.make_async_remote_copy(p_ref, buf.at[me], send.at[r], recv.at[r],
                                              device_id=dev, device_id_type=MESH)
            cp.start()
            started.append(cp)
        for r in range(1, ndev):
            pltpu.make_async_remote_copy(p_ref, buf.at[me ^ r], send.at[r], recv.at[r],
                                         device_id=(x, y, c), device_id_type=MESH).wait_recv()
        for cp in started:
            cp.wait_send()
        acc = buf[0]
        for d in range(1, ndev):
            acc = acc + buf[d]
        o_ref[...] = acc

    vm = pl.BlockSpec(memory_space=pltpu.VMEM)
    return _pcall(
        body, name="allsum_small", in_specs=[vm], out_specs=vm,
        out_shape=jax.ShapeDtypeStruct((rows, cols), F32),
        scratch_shapes=[pltpu.VMEM((ndev, rows, cols), F32), pltpu.SemaphoreType.DMA((ndev,)),
                        pltpu.SemaphoreType.DMA((ndev,))],
    )(pack)


def _adamw(parts, w, m, v, layer, prev, name):
    _, rows, cols = w.shape
    tr = ROW_TILE if rows % ROW_TILE == 0 else rows
    counts = [p.shape[0] for p in parts]
    n_parts = len(parts)
    n_prev = 0 if prev is None else 4

    def body(*refs):
        part_refs = refs[:n_parts]
        w_ref, m_ref, v_ref = refs[n_parts:n_parts + 3]
        g_ref, d_ref, nm_ref, nv_ref = refs[n_parts + 3 + n_prev:]
        g = None
        for p_ref, cnt in zip(part_refs, counts):
            s = p_ref[0].astype(F32)
            for k in range(1, cnt):
                s = s + p_ref[k].astype(F32)
            g = s if g is None else g + s
        m2 = ADAM_B1 * m_ref[0] + (1.0 - ADAM_B1) * g
        v2 = ADAM_B2 * v_ref[0] + (1.0 - ADAM_B2) * (g * g)
        m_hat = m2 / (1.0 - ADAM_B1 ** ADAM_STEP)
        v_hat = v2 / (1.0 - ADAM_B2 ** ADAM_STEP)
        g_ref[0] = g
        d_ref[0] = -ADAM_LR * (m_hat / (jnp.sqrt(v_hat) + ADAM_EPS) + ADAM_WD * w_ref[0])
        nm_ref[0] = m2
        nv_ref[0] = v2

    blk = pl.BlockSpec((1, tr, cols), lambda i: (layer, i, 0))
    shp = jax.ShapeDtypeStruct(w.shape, F32)
    return _pcall(
        body, name=name, grid=(rows // tr,),
        in_specs=[pl.BlockSpec((cnt, tr, cols), lambda i: (0, i, 0)) for cnt in counts] + [blk] * 3 + [ANY] * n_prev,
        out_specs=[blk] * 4, out_shape=[shp] * 4,
        input_output_aliases={n_parts + 3 + k: k for k in range(n_prev)},
        compiler_params=_params("parallel"),
    )(*parts, w, m, v, *(prev or ()))


def kernel(x, meta_tokens, pre_norm_g, post_norm_g, w_in, conv_w, conv_b, conv_ln_g, conv_ln_b, w_pw2, b_pw2, w_out, loss_target, m_meta_tokens, m_pre_norm_g, m_post_norm_g, m_w_in, m_conv_w, m_conv_b, m_conv_ln_g, m_conv_ln_b, m_w_pw2, m_b_pw2, m_w_out, v_meta_tokens, v_pre_norm_g, v_post_norm_g, v_w_in, v_conv_w, v_conv_b, v_conv_ln_g, v_conv_ln_b, v_w_pw2, v_b_pw2, v_w_out):
    seq, d = x.shape[1], x.shape[2]
    depth = w_in.shape[0]
    length = N_META + seq
    lp = -(-length // ATT_BLOCK) * ATT_BLOCK
    tap_pad = ((0, 0), (0, CONV_PAD - CONV_WIDTH), (0, 0))

    shards = (w_in.astype(BF16), w_pw2.astype(BF16), w_out.astype(BF16), jnp.pad(conv_w, tap_pad))
    dims = [(1, w_in.shape[2]), (0, w_pw2.shape[1]), (0, w_out.shape[1]), (1, conv_w.shape[2])]
    layer_shards = lambda l: [s[l] for s in shards]

    w_in0, meta_f = _run_exchange(_gather_plan([shards[0][0], meta_tokens], [dims[0], (1, meta_tokens.shape[1])]),
                                  "gather_weights")

    h0 = jnp.concatenate([meta_f, x[0], jnp.zeros((lp - length, d), F32)], axis=0)
    target_p = jnp.pad(loss_target[0], ((N_META, lp - length), (0, 0)))
    vecs = (pre_norm_g, post_norm_g, conv_b, conv_ln_g, conv_ln_b, b_pw2)
    loss, dh0, vec_grads, pieces = _local_step(
        h0, target_p, seq, vecs, depth, w_in0=w_in0,
        gather_w_in=lambda l: _gather_plan([shards[0][l]], dims[:1]),
        gather_rest=lambda l: _gather_plan(layer_shards(l)[1:], dims[1:]),
        reduce_layer=lambda grads, first: _reduce_plan(list(grads), dims[first:first + len(grads)]))

    def update(k, w, m, v, name):
        outs = None
        for l in reversed(range(depth)):
            outs = _adamw([pieces[l][k], pieces[l][4 + k]], w, m, v, l, outs, name)
        return outs

    up_w_in = update(0, w_in, m_w_in, v_w_in, "adamw_w_in")
    up_w_pw2 = update(1, w_pw2, m_w_pw2, v_w_pw2, "adamw_w_pw2")
    up_w_out = update(2, w_out, m_w_out, v_w_out, "adamw_w_out")
    up_conv_w = [o[:, :CONV_WIDTH] for o in update(3, jnp.pad(conv_w, tap_pad), jnp.pad(m_conv_w, tap_pad),
                                                   jnp.pad(v_conv_w, tap_pad, constant_values=1.0), "adamw_conv_w")]

    two = lambda a: a.reshape(-1, d)
    vec_rows = [two(g) for g in vec_grads]
    n_vec = sum(a.shape[0] for a in vec_rows)
    pack = jnp.concatenate(vec_rows + [dh0[:N_META], jnp.full((8, d), loss, F32)], axis=0)
    pack = jnp.pad(pack, ((0, -pack.shape[0] % 8), (0, 0)))
    tot = _allsum_small(pack)
    loss_all = tot[n_vec + N_META, 0]

    cat = lambda arrs: jnp.concatenate([two(t) for t in arrs], axis=0)[None]
    small_m = (m_pre_norm_g, m_post_norm_g, m_conv_b, m_conv_ln_g, m_conv_ln_b, m_b_pw2)
    small_v = (v_pre_norm_g, v_post_norm_g, v_conv_b, v_conv_ln_g, v_conv_ln_b, v_b_pw2)
    up_small = _adamw([tot[None, :n_vec]], cat(vecs), cat(small_m), cat(small_v), 0, None, "adamw_vectors")

    def unpack(o):
        res, r0 = [], 0
        for t in vecs:
            nrow = t.size // d
            res.append(o[0, r0:r0 + nrow].reshape(t.shape))
            r0 += nrow
        return res

    up_small = [unpack(o) for o in up_small]
    chip = 2 * lax.axis_index("x") + lax.axis_index("y")
    mcols = meta_tokens.shape[1]
    g_meta = lax.dynamic_slice_in_dim(tot[n_vec:n_vec + N_META], chip * mcols, mcols, axis=1)
    up_meta = [o[0] for o in _adamw([g_meta[None]], meta_tokens[None], m_meta_tokens[None], v_meta_tokens[None],
                                    0, None, "adamw_meta")]

    grad_x = dh0[N_META:length][None]
    outs = [loss_all, grad_x]
    for j in range(4):
        pre, post, cb, lg, lb, bp = up_small[j]
        outs += [up_meta[j], pre, post, up_w_in[j], up_conv_w[j], cb, lg, lb, up_w_pw2[j], bp, up_w_out[j]]
    return tuple(outs)
```

```python
from typing import Callable, NamedTuple

import jax
import jax.numpy as jnp
from jax import lax
from jax.experimental import pallas as pl
from jax.experimental.pallas import tpu as pltpu

F32 = jnp.float32
BF16 = jnp.bfloat16

N_META = 16
D_CONV = 512
D_SB = 512
HEAD_DIM = 64
CONV_WIDTH = 31
CONV_PAD = 32
CONV_ROWS = 128
CONV_ROWS_DX = 16
CONV_ROWS_DW = 16
RMS_EPS = 1e-6
LN_EPS = 1e-5
Q_SCALE = HEAD_DIM ** -0.5

ADAM_LR = 0.001
ADAM_B1 = 0.9
ADAM_B2 = 0.999
ADAM_EPS = 1e-08
ADAM_WD = 0.01
ADAM_STEP = 10

LANES = 128
ROW_TILE = 256
MM_TILE_MAX = 544
OUT_TILE_MAX = 1088
ATT_BLOCK = 256
ATT_PAIRS = 4
ATT_PAIRS_FWD = 4
VMEM_LIMIT = 56 * 1024 * 1024
EXP_ZERO = -104.0
COUNT_LANE = LANES - 1
RELAY_AT = 0.75

MESH = pl.DeviceIdType.MESH


def _pcall(body, **kw):
    return pl.pallas_call(body, **kw)


def _params(*sem):
    return pltpu.CompilerParams(dimension_semantics=sem, vmem_limit_bytes=VMEM_LIMIT)


def _sigmoid(x):
    return 1.0 / (1.0 + jnp.exp(-x))


def _silu_fwd_bwd(x):
    s = _sigmoid(x)
    return x * s, s * (1.0 + x * (1.0 - s))


def _nt(a, b):
    return lax.dot_general(a, b, (((1,), (1,)), ((), ())), preferred_element_type=F32)


def _tn(a, b):
    return lax.dot_general(a, b, (((0,), (0,)), ((), ())), preferred_element_type=F32)


def _nn(a, b):
    return jnp.dot(a, b, preferred_element_type=F32)


def _mm_tile(rows, cap=None):
    return max(t for t in range(16, (cap or MM_TILE_MAX) + 1, 16) if rows % t == 0)


def _host_begin(plan, parts, steps):
    if plan is not None:
        x_in, _, x_out, _, x_sems = parts

        @pl.when(pl.program_id(0) == 0)
        def _():
            plan.start(x_in, x_out, x_sems)

        @pl.when(pl.program_id(0) == int(RELAY_AT * steps))
        def _():
            plan.relay(x_in, x_out, x_sems)


def _host_end(plan, parts, steps):
    if plan is not None:
        x_in, _, x_out, _, x_sems = parts

        @pl.when(pl.program_id(0) == steps - 1)
        def _():
            plan.finish(x_in, x_out, x_sems)


def _inproj(h, g, w, plan=None):
    lp, d = h.shape
    n = w.shape[1]
    tm = _mm_tile(lp)
    steps = lp // tm
    split, x_args, x_in_specs, x_out_specs, x_out_shapes, x_scratch = _hosted(plan, 2, 0)

    def body(h_ref, g_ref, w_ref, *rest):
        parts = split(rest)
        ew_ref, qkv_ref = parts[1]
        _host_begin(plan, parts, steps)
        x = h_ref[...]
        rstd = lax.rsqrt(jnp.mean(x * x, axis=-1, keepdims=True) + RMS_EPS)
        u = ((x * rstd) * g_ref[...]).astype(BF16)
        p = _nn(u, w_ref[...])
        ew_ref[:, 0:1536] = p[:, 0:1536].astype(BF16)
        ew_ref[:, 1536:2048] = p[:, 3072:3584].astype(BF16)
        qkv_ref[:, 0:512] = (p[:, 1536:2048] * Q_SCALE).astype(BF16)
        qkv_ref[:, 512:1536] = p[:, 2048:3072].astype(BF16)
        _host_end(plan, parts, steps)

    outs = _pcall(
        body, name="inproj_fwd" if plan is None else "inproj_fwd_gather", grid=(steps,),
        in_specs=[pl.BlockSpec((tm, d), lambda i: (i, 0)),
                  pl.BlockSpec((1, d), lambda i: (0, 0)),
                  pl.BlockSpec((d, n), lambda i: (0, 0))] + x_in_specs,
        out_specs=[pl.BlockSpec((tm, 2048), lambda i: (i, 0)),
                   pl.BlockSpec((tm, 1536), lambda i: (i, 0))] + x_out_specs,
        out_shape=[jax.ShapeDtypeStruct((lp, 2048), BF16), jax.ShapeDtypeStruct((lp, 1536), BF16)] + x_out_shapes,
        scratch_shapes=x_scratch,
        compiler_params=_params("parallel" if plan is None else "arbitrary"),
    )(h, g, w, *x_args)
    return outs[0], outs[1], outs[2:]


def _layer_norm_stats(c1):
    mu = jnp.mean(c1, axis=-1, keepdims=True)
    xc = c1 - mu
    var = jnp.mean(xc * xc, axis=-1, keepdims=True)
    rstd = lax.rsqrt(var + LN_EPS)
    return xc * rstd, rstd


def _shifted_copies(window, cols, shifted, tm):
    shifted[0] = window[:, cols]
    rows = tm + CONV_PAD - 8
    for b in range(1, 8):
        shifted[b, 0:rows, :] = window[pl.ds(b, rows), cols]


def _shifted_rows(shifted, shift, tm):
    b = shift % 8
    return shifted[b, pl.ds(pl.multiple_of(shift - b, 8), tm), :]


def _weighted(cw8_ref, j, cols, rows):
    w8 = cw8_ref[pl.ds(pl.multiple_of(j * 8, 8), 8), cols]
    r = rows.shape[0]
    return (rows.reshape(r // 8, 8, LANES) * w8[None]).reshape(r, LANES)


def _conv_fwd(ew, cw, cb, lng, lnb, wpw2, bpw2, plan=None):
    lp = ew.shape[0]
    tm = ROW_TILE
    steps = lp // tm
    split, x_args, x_in_specs, x_out_specs, x_out_shapes, x_scratch = _hosted(plan, 3, 2)

    def body(ew_ref, cw_ref, cb_ref, lng_ref, lnb_ref, w_ref, b_ref, *rest):
        parts = split(rest)
        (c1_ref, c4_ref, c5_ref), (xbuf, shifted) = parts[1], parts[3]
        _host_begin(plan, parts, steps)
        @pl.when(pl.program_id(0) == 0)
        def _():
            xbuf[0:CONV_PAD, :] = jnp.zeros((CONV_PAD, D_CONV), F32)

        ga = ew_ref[:, 0:512].astype(F32)
        gb = ew_ref[:, 512:1024].astype(F32)
        cg = ew_ref[:, 1024:1536].astype(F32)
        xbuf[CONV_PAD:CONV_PAD + tm, :] = ga * _sigmoid(gb)
        for blk in range(D_CONV // LANES):
            cs = slice(blk * LANES, (blk + 1) * LANES)
            _shifted_copies(xbuf, cs, shifted, tm)
            for r0 in range(0, tm, CONV_ROWS):
                acc = jnp.zeros((CONV_ROWS, LANES), F32) + cb_ref[:, cs]
                for j in range(CONV_WIDTH):
                    acc = acc + _weighted(cw_ref, j, cs, _shifted_rows(
                        shifted, r0 + CONV_PAD - (CONV_WIDTH - 1) + j, CONV_ROWS))
                c1_ref[r0:r0 + CONV_ROWS, cs] = acc
        xbuf[0:CONV_PAD, :] = xbuf[tm:tm + CONV_PAD, :]
        xhat, _ = _layer_norm_stats(c1_ref[...])
        c2 = xhat * lng_ref[...] + lnb_ref[...]
        c3 = c2 * _sigmoid(c2)
        c4 = _nn(c3.astype(BF16), w_ref[...]) + b_ref[...]
        c4_ref[...] = c4
        c5_ref[...] = (c4 * (cg * _sigmoid(cg))).astype(BF16)
        _host_end(plan, parts, steps)

    vec = pl.BlockSpec((1, D_CONV), lambda i: (0, 0))
    row = pl.BlockSpec((tm, D_CONV), lambda i: (i, 0))
    outs = _pcall(
        body, name="conv_fwd" if plan is None else "conv_fwd_gather", grid=(steps,),
        in_specs=[pl.BlockSpec((tm, 1536), lambda i: (i, 0)),
                  pl.BlockSpec((8 * CONV_PAD, D_CONV), lambda i: (0, 0)),
                  vec, vec, vec,
                  pl.BlockSpec((D_CONV, D_CONV), lambda i: (0, 0)),
                  vec] + x_in_specs,
        out_specs=[row, row, row] + x_out_specs,
        out_shape=[jax.ShapeDtypeStruct((lp, D_CONV), F32), jax.ShapeDtypeStruct((lp, D_CONV), F32),
                   jax.ShapeDtypeStruct((lp, D_CONV), BF16)] + x_out_shapes,
        scratch_shapes=[pltpu.VMEM((tm + CONV_PAD, D_CONV), F32),
                        pltpu.VMEM((8, tm + CONV_PAD, LANES), F32)] + x_scratch,
        compiler_params=_params("arbitrary"),
    )(ew, cw, cb, lng, lnb, wpw2, bpw2, *x_args)
    return outs[0], outs[1], outs[2], outs[3:]


def _block_sums(x, m01):
    return _nn(x.astype(BF16), m01)


def _attn_masks():
    lane = lax.broadcasted_iota(jnp.int32, (1, LANES), 1)
    row = lax.broadcasted_iota(jnp.int32, (2 * ATT_BLOCK, ATT_BLOCK), 0)
    col = lax.broadcasted_iota(jnp.int32, (2 * ATT_BLOCK, ATT_BLOCK), 1)
    return lane < HEAD_DIM, col < (row & (ATT_BLOCK - 1))


def _stack_heads(x, first_head):
    zero = jnp.zeros_like(x)
    return jnp.concatenate([jnp.where(first_head, x, zero), jnp.where(first_head, zero, x)], axis=0)


def _unstack_heads(x2, first_head):
    rows = x2.shape[0] // 2
    return jnp.where(first_head, x2[:rows], x2[rows:])


def _hosted(plan, n_out, n_scratch):
    n_in = 0 if plan is None else len(plan.inputs)
    n_x = 0 if plan is None else len(plan.out_shapes)

    def split(rest):
        a, b, c = n_in + n_out, n_in + n_out + n_x, n_in + n_out + n_x + n_scratch
        return rest[:n_in], rest[n_in:a], rest[a:b], rest[b:c], rest[c:]

    if plan is None:
        return split, [], [], [], [], []
    return split, list(plan.inputs), [ANY] * n_in, [ANY] * n_x, list(plan.out_shapes), list(plan.scratch)


def _attn_fwd(qkv, tri, plan=None):
    lp = qkv.shape[0]
    bq = ATT_BLOCK
    ngrp = ATT_PAIRS_FWD
    nstep = D_SB // (LANES * ngrp)
    nq = lp // bq
    assert nq <= COUNT_LANE
    split, x_args, x_in_specs, x_out_specs, x_out_shapes, x_scratch = _hosted(plan, 2, 3)

    def body(q_ref, k_ref, v_ref, tri_ref, *rest):
        x_in, (o_ref, carry_ref), x_out, (c_s, acc_s, cm_s), x_sems = split(rest)
        i = pl.program_id(1)
        if plan is not None:
            @pl.when(jnp.logical_and(pl.program_id(0) == 0, i == 0))
            def _():
                plan.start(x_in, x_out, x_sems)

            @pl.when(jnp.logical_and(pl.program_id(0) == nstep - 1, i == int(RELAY_AT * nq)))
            def _():
                plan.relay(x_in, x_out, x_sems)

        first_head, vis = _attn_masks()
        lane = lax.broadcasted_iota(jnp.int32, (1, LANES), 1)
        cols = [slice(g * LANES, (g + 1) * LANES) for g in range(ngrp)]
        q2s = [_stack_heads(q_ref[:, cs], first_head) for cs in cols]
        tri_m = tri_ref[...]

        c_s[...] = jnp.zeros_like(c_s)
        acc_s[...] = jnp.zeros_like(acc_s)
        cm_s[...] = jnp.zeros_like(cm_s)

        def blocks(js, masks):
            offs = [pl.multiple_of(j * bq, bq) for j in js]
            work = [(g, b) for b in range(len(js)) for g in range(ngrp)]
            zs = {(g, b): _nt(q2s[g], k_ref[pl.ds(offs[b], bq), cols[g]]) for g, b in work}
            lss = {}
            for g, b in work:
                z = zs[g, b]
                ls = -(jnp.maximum(z, 0.0) + jnp.log(1.0 + jnp.exp(-jnp.abs(z))))
                lss[g, b] = ls if masks[b] is None else jnp.where(masks[b], ls, 0.0)
            tails = {gb: _block_sums(lss[gb], tri_m) for gb in work}
            probs = {}
            carry = [c_s[g] for g in range(ngrp)]
            saved = [cm_s[g] for g in range(ngrp)]
            for g, b in work:
                a = jnp.exp(zs[g, b] + lss[g, b] + tails[g, b] + carry[g])
                probs[g, b] = (a if masks[b] is None else jnp.where(masks[b], a, 0.0)).astype(BF16)
                saved[g] = jnp.where(lane == js[b], carry[g], saved[g])
                carry[g] = carry[g] + tails[g, b][:, 0:1] + lss[g, b][:, 0:1]
            top = None
            for g in range(ngrp):
                c_s[g] = carry[g]
                cm_s[g] = saved[g]
                acc = acc_s[g]
                for b in range(len(js)):
                    acc = acc + _nn(probs[g, b], v_ref[pl.ds(offs[b], bq), cols[g]])
                acc_s[g] = acc
                top = carry[g] if top is None else jnp.maximum(top, carry[g])
            return jnp.max(top) > EXP_ZERO

        alive = lax.cond(i > 0, lambda: blocks([i, i - 1], [vis, None]), lambda: blocks([i], [vis]))
        rest = jnp.maximum(i - 1, 0)

        def pair(carry):
            t, _ = carry
            j = i - 2 - 2 * t
            return t + 1, blocks([j, j - 1], [None, None])

        trips, alive = lax.while_loop(lambda ca: jnp.logical_and(ca[0] < rest // 2, ca[1]), pair, (0, alive))
        last = jnp.logical_and(jnp.logical_and(rest % 2 == 1, trips == rest // 2), alive)

        @pl.when(last)
        def _():
            blocks([0], [None])

        n_done = (jnp.minimum(i + 1, 2) + 2 * trips + last.astype(jnp.int32)).astype(F32)
        for g in range(ngrp):
            cmat = jnp.where(lane == COUNT_LANE, n_done, cm_s[g])
            carry_ref[:, 2 * g * LANES:(2 * g + 1) * LANES] = cmat[:bq]
            carry_ref[:, (2 * g + 1) * LANES:(2 * g + 2) * LANES] = cmat[bq:]
            o_ref[:, cols[g]] = _unstack_heads(acc_s[g], first_head)
        if plan is not None:
            @pl.when(jnp.logical_and(pl.program_id(0) == nstep - 1, i == nq - 1))
            def _():
                plan.finish(x_in, x_out, x_sems)

    width = ngrp * LANES
    outs = _pcall(
        body, name="attn_fwd" if plan is None else "attn_fwd_gather", grid=(nstep, nq),
        in_specs=[pl.BlockSpec((bq, width), lambda p, i: (i, p)),
                  pl.BlockSpec((lp, width), lambda p, i: (0, nstep + p)),
                  pl.BlockSpec((lp, width), lambda p, i: (0, 2 * nstep + p)),
                  pl.BlockSpec((bq, bq), lambda p, i: (0, 0))] + x_in_specs,
        out_specs=[pl.BlockSpec((bq, width), lambda p, i: (i, p)),
                   pl.BlockSpec((bq, 2 * width), lambda p, i: (i, p))] + x_out_specs,
        out_shape=[jax.ShapeDtypeStruct((lp, D_SB), F32), jax.ShapeDtypeStruct((lp, 2 * D_SB), F32)] + x_out_shapes,
        scratch_shapes=[pltpu.VMEM((ngrp, 2 * bq, 1), F32), pltpu.VMEM((ngrp, 2 * bq, LANES), F32),
                        pltpu.VMEM((ngrp, 2 * bq, LANES), F32)] + x_scratch,
        compiler_params=_params("arbitrary", "arbitrary"),
    )(qkv, qkv, qkv, tri, *x_args)
    return outs[0], outs[1], outs[2:]


def _outproj(c5, att, ew, h, w, g):
    lp, d = h.shape
    tm = _mm_tile(lp, OUT_TILE_MAX)

    def body(c5_ref, att_ref, sg_ref, h_ref, w_ref, g_ref, hn_ref, cat_ref, mix_ref):
        sg = sg_ref[...].astype(F32)
        s = att_ref[...] * (sg * _sigmoid(sg))
        cat_ref[:, 0:D_CONV] = c5_ref[...]
        cat_ref[:, D_CONV:] = s.astype(BF16)
        mixed = _nn(cat_ref[...], w_ref[...])
        mix_ref[...] = mixed
        rstd = lax.rsqrt(jnp.mean(mixed * mixed, axis=-1, keepdims=True) + RMS_EPS)
        hn_ref[...] = h_ref[...] + (mixed * rstd) * g_ref[...]

    half = pl.BlockSpec((tm, 512), lambda i: (i, 0))
    full = pl.BlockSpec((tm, d), lambda i: (i, 0))
    return _pcall(
        body, name="outproj_fwd", grid=(lp // tm,),
        in_specs=[half, half, pl.BlockSpec((tm, 512), lambda i: (i, 3)), full,
                  pl.BlockSpec((d, d), lambda i: (0, 0)), pl.BlockSpec((1, d), lambda i: (0, 0))],
        out_specs=[full, full, full],
        out_shape=[jax.ShapeDtypeStruct((lp, d), F32), jax.ShapeDtypeStruct((lp, d), BF16),
                   jax.ShapeDtypeStruct((lp, d), F32)],
        compiler_params=_params("parallel"),
    )(c5, att, ew, h, w, g)


def _loss_head(h, target, seq):
    lp, d = h.shape
    tm = ROW_TILE

    def body(h_ref, t_ref, dh_ref, loss_ref):
        i = pl.program_id(0)

        @pl.when(i == 0)
        def _():
            loss_ref[...] = jnp.zeros_like(loss_ref)

        row = i * tm + lax.broadcasted_iota(jnp.int32, (tm, 1), 0)
        real = jnp.logical_and(row >= N_META, row < N_META + seq)
        diff = jnp.where(real, h_ref[...] - t_ref[...], 0.0)
        dh_ref[...] = diff * (1.0 / d)
        loss_ref[...] += 0.5 * jnp.sum(jnp.sum(diff * diff, axis=-1, keepdims=True) * (1.0 / d))

    full = pl.BlockSpec((tm, d), lambda i: (i, 0))
    return _pcall(
        body, name="loss_head", grid=(lp // tm,),
        in_specs=[full, full],
        out_specs=[full, pl.BlockSpec((8, LANES), lambda i: (0, 0))],
        out_shape=[jax.ShapeDtypeStruct((lp, d), F32), jax.ShapeDtypeStruct((8, LANES), F32)],
        compiler_params=_params("arbitrary"),
    )(h, target)


def _outproj_bwd(dh, mixed, g, w, att, ew, c4):
    lp, d = dh.shape
    tm = _mm_tile(lp, OUT_TILE_MAX)

    def body(dh_ref, mix_ref, g_ref, w_ref, att_ref, cg_ref, sg_ref, c4_ref,
             dmix_ref, datt_ref, dsg_ref, dc4_ref, dcg_ref, dg_ref, db_ref):
        @pl.when(pl.program_id(0) == 0)
        def _():
            dg_ref[...] = jnp.zeros_like(dg_ref)
            db_ref[...] = jnp.zeros_like(db_ref)

        mixed = mix_ref[...]
        dhv = dh_ref[...]
        rstd = lax.rsqrt(jnp.mean(mixed * mixed, axis=-1, keepdims=True) + RMS_EPS)
        n = mixed * rstd
        dg_ref[...] += jnp.sum(dhv * n, axis=0, keepdims=True)
        dn = dhv * g_ref[...]
        dmix = (rstd * (dn - n * jnp.mean(dn * n, axis=-1, keepdims=True))).astype(BF16)
        dmix_ref[...] = dmix
        dcat = _nt(dmix, w_ref[...])
        dc5 = dcat[:, 0:D_CONV]
        ds = dcat[:, D_CONV:]
        silu_sg, dsilu_sg = _silu_fwd_bwd(sg_ref[...].astype(F32))
        datt_ref[...] = (ds * silu_sg).astype(BF16)
        dsg_ref[...] = (ds * att_ref[...] * dsilu_sg).astype(BF16)
        silu_cg, dsilu_cg = _silu_fwd_bwd(cg_ref[...].astype(F32))
        dc4 = dc5 * silu_cg
        db_ref[...] += jnp.sum(dc4, axis=0, keepdims=True)
        dc4_ref[...] = dc4.astype(BF16)
        dcg_ref[...] = (dc5 * c4_ref[...] * dsilu_cg).astype(BF16)

    half = pl.BlockSpec((tm, 512), lambda i: (i, 0))
    full = pl.BlockSpec((tm, d), lambda i: (i, 0))
    hb = jax.ShapeDtypeStruct((lp, 512), BF16)
    return _pcall(
        body, name="outproj_bwd", grid=(lp // tm,),
        in_specs=[full, full, pl.BlockSpec((1, d), lambda i: (0, 0)), pl.BlockSpec((d, d), lambda i: (0, 0)),
                  half, pl.BlockSpec((tm, 512), lambda i: (i, 2)), pl.BlockSpec((tm, 512), lambda i: (i, 3)), half],
        out_specs=[full, half, half, half, half,
                   pl.BlockSpec((1, d), lambda i: (0, 0)), pl.BlockSpec((1, 512), lambda i: (0, 0))],
        out_shape=[jax.ShapeDtypeStruct((lp, d), BF16), hb, hb, hb, hb,
                   jax.ShapeDtypeStruct((1, d), F32), jax.ShapeDtypeStruct((1, 512), F32)],
        compiler_params=_params("arbitrary"),
    )(dh, mixed, g, w, att, ew, ew, c4)


def _attn_bwd(qkv, carries, datt, tri, upper, plan=None):
    lp = qkv.shape[0]
    bq = ATT_BLOCK
    ngrp = ATT_PAIRS
    nstep = D_SB // (LANES * ngrp)
    nq = lp // bq
    split, x_args, x_in_specs, x_out_specs, x_out_shapes, x_scratch = _hosted(plan, 3, 2)

    def body(q_ref, k_ref, v_ref, carry_ref, do_ref, tri_ref, upper_ref, *rest):
        x_in, (dq_ref, dk_ref, dv_ref), x_out, (run_s, dq_s), x_sems = split(rest)
        i = pl.program_id(1)
        if plan is not None:
            @pl.when(jnp.logical_and(pl.program_id(0) == 0, i == 0))
            def _():
                plan.start(x_in, x_out, x_sems)

            @pl.when(jnp.logical_and(pl.program_id(0) == nstep - 1, i == max(nq - 2, 0)))
            def _():
                plan.relay(x_in, x_out, x_sems)

        @pl.when(i == 0)
        def _():
            dk_ref[...] = jnp.zeros_like(dk_ref)
            dv_ref[...] = jnp.zeros_like(dv_ref)

        first_head, vis = _attn_masks()
        lane = lax.broadcasted_iota(jnp.int32, (1, LANES), 1)
        cols = [slice(g * LANES, (g + 1) * LANES) for g in range(ngrp)]
        q2s = [_stack_heads(q_ref[:, cs], first_head) for cs in cols]
        do2s = [_stack_heads(do_ref[:, cs], first_head) for cs in cols]
        cmats = [jnp.concatenate([carry_ref[:, 2 * g * LANES:(2 * g + 1) * LANES],
                                  carry_ref[:, (2 * g + 1) * LANES:(2 * g + 2) * LANES]], axis=0)
                 for g in range(ngrp)]
        tri_m = tri_ref[...]
        upper_m = upper_ref[...]

        def blocks(js, masks):
            offs = [pl.multiple_of(j * bq, bq) for j in js]
            work = [(g, b) for b in range(len(js)) for g in range(ngrp)]
            zs = {(g, b): _nt(q2s[g], k_ref[pl.ds(offs[b], bq), cols[g]]) for g, b in work}
            lss = {}
            for g, b in work:
                z = zs[g, b]
                ls = -(jnp.maximum(z, 0.0) + jnp.log(1.0 + jnp.exp(-jnp.abs(z))))
                lss[g, b] = ls if masks[b] is None else jnp.where(masks[b], ls, 0.0)
            tails = {gb: _block_sums(lss[gb], tri_m) for gb in work}
            das = {(g, b): _nt(do2s[g], v_ref[pl.ds(offs[b], bq), cols[g]]) for g, b in work}
            probs, des = {}, {}
            for g, b in work:
                c = jnp.sum(jnp.where(lane == js[b], cmats[g], 0.0), axis=-1, keepdims=True)
                a = jnp.exp(zs[g, b] + lss[g, b] + tails[g, b] + c)
                a = a if masks[b] is None else jnp.where(masks[b], a, 0.0)
                probs[g, b] = a.astype(BF16)
                des[g, b] = das[g, b] * a
            prefixes = {gb: _block_sums(des[gb], upper_m) for gb in work}
            runs = [run_s[g] for g in range(ngrp)]
            dzs = {}
            for g, b in work:
                beta = jnp.exp(zs[g, b] + lss[g, b])
                dz = des[g, b] - beta * (des[g, b] + runs[g] + prefixes[g, b])
                dzs[g, b] = (dz if masks[b] is None else jnp.where(masks[b], dz, 0.0)).astype(BF16)
                runs[g] = runs[g] + prefixes[g, b][:, bq - 1:bq] + des[g, b][:, bq - 1:bq]
            for g in range(ngrp):
                run_s[g] = runs[g]
                dq = dq_s[g]
                for b in range(len(js)):
                    rows = pl.ds(offs[b], bq)
                    dq = dq + _nn(dzs[g, b], k_ref[rows, cols[g]])
                    dk_ref[rows, cols[g]] += _tn(dzs[g, b], q2s[g])
                    dv_ref[rows, cols[g]] += _tn(probs[g, b], do2s[g])
                dq_s[g] = dq

        n_done = jnp.max(carry_ref[:, COUNT_LANE:COUNT_LANE + 1]).astype(jnp.int32)
        n_done = jnp.clip(n_done, 1, i + 1)
        before = jnp.maximum(n_done - 2, 0)
        j0 = i - n_done + 1
        odd = before % 2
        run_s[...] = jnp.zeros_like(run_s)
        dq_s[...] = jnp.zeros_like(dq_s)

        @pl.when(odd == 1)
        def _():
            blocks([j0], [None])

        @pl.loop(0, before // 2)
        def _(t):
            blocks([j0 + odd + 2 * t, j0 + odd + 2 * t + 1], [None, None])

        @pl.when(n_done > 1)
        def _():
            blocks([i - 1, i], [None, vis])

        @pl.when(n_done <= 1)
        def _():
            blocks([i], [vis])

        for g in range(ngrp):
            dq_ref[:, cols[g]] = (_unstack_heads(dq_s[g], first_head) * Q_SCALE).astype(BF16)
        if plan is not None:
            @pl.when(jnp.logical_and(pl.program_id(0) == nstep - 1, i == nq - 1))
            def _():
                plan.finish(x_in, x_out, x_sems)

    width = ngrp * LANES
    once = pl.Buffered(1)
    qb = pl.BlockSpec((bq, width), lambda p, i: (i, p))
    colb = pl.BlockSpec((lp, width), lambda p, i: (0, p), pipeline_mode=once)
    sq = pl.BlockSpec((bq, bq), lambda p, i: (0, 0))
    outs = _pcall(
        body, name="attn_bwd" if plan is None else "attn_bwd_reduce", grid=(nstep, nq),
        in_specs=[qb,
                  pl.BlockSpec((lp, width), lambda p, i: (0, nstep + p), pipeline_mode=once),
                  pl.BlockSpec((lp, width), lambda p, i: (0, 2 * nstep + p), pipeline_mode=once),
                  pl.BlockSpec((bq, 2 * width), lambda p, i: (i, p)), qb, sq, sq] + x_in_specs,
        out_specs=[qb, colb, colb] + x_out_specs,
        out_shape=[jax.ShapeDtypeStruct((lp, D_SB), BF16), jax.ShapeDtypeStruct((lp, D_SB), F32),
                   jax.ShapeDtypeStruct((lp, D_SB), F32)] + x_out_shapes,
        scratch_shapes=[pltpu.VMEM((ngrp, 2 * bq, 1), F32), pltpu.VMEM((ngrp, 2 * bq, LANES), F32)] + x_scratch,
        compiler_params=_params("arbitrary", "arbitrary"),
    )(qkv, qkv, qkv, carries, datt, tri, upper, *x_args)
    return outs[0], outs[1], outs[2], outs[3:]


def _conv_bwd(dc4, c1, ew, cw, lng, lnb, wpw2):
    lp = ew.shape[0]
    tm = ROW_TILE
    nt = lp // tm
    halo_per_tile = tm // CONV_PAD

    def body(dc4_ref, c1_ref, ew_ref, halo_ref, cw_ref, lng_ref, lnb_ref, w_ref,
             dga_ref, dgb_ref, dwp_ref, dcw_ref, dcb_ref, dlng_ref, dlnb_ref, xbuf, dbuf, shifted, wacc):
        step = pl.program_id(0)

        @pl.when(step == 0)
        def _():
            wacc[...] = jnp.zeros_like(wacc)
            dcb_ref[...] = jnp.zeros_like(dcb_ref)
            dlng_ref[...] = jnp.zeros_like(dlng_ref)
            dlnb_ref[...] = jnp.zeros_like(dlnb_ref)
            dwp_ref[...] = jnp.zeros_like(dwp_ref)
            dbuf[tm:tm + CONV_PAD, :] = jnp.zeros((CONV_PAD, D_CONV), F32)

        dc3 = _nt(dc4_ref[...], w_ref[...])
        xhat, rstd = _layer_norm_stats(c1_ref[...])
        c2 = xhat * lng_ref[...] + lnb_ref[...]
        c3, dsilu = _silu_fwd_bwd(c2)
        dwp_ref[...] += _tn(c3.astype(BF16), dc4_ref[...])
        dc2 = dc3 * dsilu
        dlng_ref[...] += jnp.sum(dc2 * xhat, axis=0, keepdims=True)
        dlnb_ref[...] += jnp.sum(dc2, axis=0, keepdims=True)
        dxhat = dc2 * lng_ref[...]
        dc1 = rstd * (dxhat - jnp.mean(dxhat, axis=-1, keepdims=True)
                      - xhat * jnp.mean(dxhat * xhat, axis=-1, keepdims=True))
        dcb_ref[...] += jnp.sum(dc1, axis=0, keepdims=True)
        dbuf[0:tm, :] = dc1

        ga = ew_ref[:, 0:512].astype(F32)
        sgb = _sigmoid(ew_ref[:, 512:1024].astype(F32))
        xbuf[CONV_PAD:CONV_PAD + tm, :] = ga * sgb
        first_tile = step == nt - 1
        halo = halo_ref[:, 0:512].astype(F32) * _sigmoid(halo_ref[:, 512:1024].astype(F32))
        xbuf[0:CONV_PAD, :] = jnp.where(first_tile, 0.0, halo)

        for cb in range(D_CONV // LANES):
            cs = slice(cb * LANES, (cb + 1) * LANES)
            _shifted_copies(dbuf, cs, shifted, tm)
            for r0 in range(0, tm, CONV_ROWS_DX):
                rs = slice(r0, r0 + CONV_ROWS_DX)
                dc0 = jnp.zeros((CONV_ROWS_DX, LANES), F32)
                for j in range(CONV_WIDTH):
                    dc0 = dc0 + _weighted(cw_ref, j, cs, _shifted_rows(
                        shifted, r0 + CONV_WIDTH - 1 - j, CONV_ROWS_DX))
                dga_ref[rs, cs] = (dc0 * sgb[rs, cs]).astype(BF16)
                dgb_ref[rs, cs] = (dc0 * ga[rs, cs] * sgb[rs, cs] * (1.0 - sgb[rs, cs])).astype(BF16)
            _shifted_copies(xbuf, cs, shifted, tm)
            for r0 in range(0, tm, CONV_ROWS_DW):
                d1 = dbuf[r0:r0 + CONV_ROWS_DW, cs]

                for j in range(CONV_WIDTH):
                    prod = d1 * _shifted_rows(shifted, r0 + CONV_PAD - (CONV_WIDTH - 1) + j, CONV_ROWS_DW)
                    wacc[j * 8:(j + 1) * 8, cs] += jnp.sum(prod.reshape(CONV_ROWS_DW // 8, 8, LANES), axis=0)
        dbuf[tm:tm + CONV_PAD, :] = dbuf[0:CONV_PAD, :]

        @pl.when(step == nt - 1)
        def _():
            dcw_ref[...] = jnp.sum(wacc[...].reshape(CONV_PAD, 8, D_CONV), axis=1)

    rev = lambda i: (nt - 1 - i, 0)
    row = pl.BlockSpec((tm, D_CONV), rev)
    vec = pl.BlockSpec((1, D_CONV), lambda i: (0, 0))
    hb = jax.ShapeDtypeStruct((lp, D_CONV), BF16)
    vs = jax.ShapeDtypeStruct((1, D_CONV), F32)
    return _pcall(
        body, name="conv_bwd", grid=(nt,),
        in_specs=[row, row, pl.BlockSpec((tm, 1024), rev),
                  pl.BlockSpec((CONV_PAD, 1024), lambda i: (jnp.maximum((nt - 1 - i) * halo_per_tile - 1, 0), 0)),
                  pl.BlockSpec((8 * CONV_PAD, D_CONV), lambda i: (0, 0)), vec, vec,
                  pl.BlockSpec((D_CONV, D_CONV), lambda i: (0, 0))],
        out_specs=[row, row, pl.BlockSpec((D_CONV, D_CONV), lambda i: (0, 0)),
                   pl.BlockSpec((CONV_PAD, D_CONV), lambda i: (0, 0)), vec, vec, vec],
        out_shape=[hb, hb, jax.ShapeDtypeStruct((D_CONV, D_CONV), F32),
                   jax.ShapeDtypeStruct((CONV_PAD, D_CONV), F32), vs, vs, vs],
        scratch_shapes=[pltpu.VMEM((tm + CONV_PAD, D_CONV), F32), pltpu.VMEM((tm + CONV_PAD, D_CONV), F32),
                        pltpu.VMEM((8, tm + CONV_PAD, LANES), F32), pltpu.VMEM((8 * CONV_PAD, D_CONV), F32)],
        compiler_params=_params("arbitrary"),
    )(dc4, c1, ew, ew, cw, lng, lnb, wpw2)


def _inproj_bwd(dga, dgb, dcg, dq, dk, dv, dsg, h, g, w, dh_out, plan=None):
    lp, d = h.shape
    n = w.shape[1]
    tm = _mm_tile(lp)
    steps = lp // tm
    split, x_args, x_in_specs, x_out_specs, x_out_shapes, x_scratch = _hosted(plan, 4, 0)

    def body(dga_ref, dgb_ref, dcg_ref, dq_ref, dk_ref, dv_ref, dsg_ref, h_ref, g_ref, w_ref, dho_ref, *rest):
        parts = split(rest)
        dh_ref, dproj_ref, u_ref, dg_ref = parts[1]
        _host_begin(plan, parts, steps)

        @pl.when(pl.program_id(0) == 0)
        def _():
            dg_ref[...] = jnp.zeros_like(dg_ref)

        dproj_ref[:, 0:512] = dga_ref[...]
        dproj_ref[:, 512:1024] = dgb_ref[...]
        dproj_ref[:, 1024:1536] = dcg_ref[...]
        dproj_ref[:, 1536:2048] = dq_ref[...]
        dproj_ref[:, 2048:2560] = dk_ref[...].astype(BF16)
        dproj_ref[:, 2560:3072] = dv_ref[...].astype(BF16)
        dproj_ref[:, 3072:3584] = dsg_ref[...]
        du = _nt(dproj_ref[...], w_ref[...])
        x = h_ref[...]
        rstd = lax.rsqrt(jnp.mean(x * x, axis=-1, keepdims=True) + RMS_EPS)
        nrm = x * rstd
        u_ref[...] = (nrm * g_ref[...]).astype(BF16)
        dg_ref[...] += jnp.sum(du * nrm, axis=0, keepdims=True)
        dn = du * g_ref[...]
        dh_ref[...] = dho_ref[...] + rstd * (dn - nrm * jnp.mean(dn * nrm, axis=-1, keepdims=True))
        _host_end(plan, parts, steps)

    half = pl.BlockSpec((tm, 512), lambda i: (i, 0))
    full = pl.BlockSpec((tm, d), lambda i: (i, 0))
    outs = _pcall(
        body, name="inproj_bwd" if plan is None else "inproj_bwd_reduce", grid=(steps,),
        in_specs=[half] * 7 + [full, pl.BlockSpec((1, d), lambda i: (0, 0)),
                               pl.BlockSpec((d, n), lambda i: (0, 0)), full] + x_in_specs,
        out_specs=[full, pl.BlockSpec((tm, n), lambda i: (i, 0)), full,
                   pl.BlockSpec((1, d), lambda i: (0, 0))] + x_out_specs,
        out_shape=[jax.ShapeDtypeStruct((lp, d), F32), jax.ShapeDtypeStruct((lp, n), BF16),
                   jax.ShapeDtypeStruct((lp, d), BF16), jax.ShapeDtypeStruct((1, d), F32)] + x_out_shapes,
        scratch_shapes=x_scratch,
        compiler_params=_params("arbitrary"),
    )(dga, dgb, dcg, dq, dk, dv, dsg, h, g, w, dh_out, *x_args)
    return outs[0], outs[1], outs[2], outs[3], outs[4:]


def _row_split(m, parts):
    tm = m // parts
    assert tm * parts == m and tm % 16 == 0, (m, parts)
    return tm


def _matmul_tn(x, dy, tn, name):
    m, k = x.shape
    n = dy.shape[1]
    steps = 2 if m % 32 == 0 else 1
    tm = _row_split(m, steps)

    def body(x_ref, dy_ref, o_ref, acc_ref):
        r = pl.program_id(1)

        @pl.when(r == 0)
        def _():
            acc_ref[...] = jnp.zeros_like(acc_ref)

        acc_ref[...] += _tn(x_ref[...], dy_ref[...])

        @pl.when(r == steps - 1)
        def _():
            o_ref[...] = acc_ref[...].astype(BF16)

    return _pcall(
        body, name=name, grid=(n // tn, steps),
        in_specs=[pl.BlockSpec((tm, k), lambda j, r: (r, 0)), pl.BlockSpec((tm, tn), lambda j, r: (r, j))],
        out_specs=pl.BlockSpec((k, tn), lambda j, r: (0, j)),
        out_shape=jax.ShapeDtypeStruct((k, n), BF16),
        scratch_shapes=[pltpu.VMEM((k, tn), F32)],
        compiler_params=_params("parallel", "arbitrary"),
    )(x, dy)


def _local_step(h0, target_p, seq, vecs, depth, all_weights=None, w_in0=None, gather_w_in=None, gather_rest=None,
                reduce_layer=None):
    pre_g, post_g, conv_b, ln_g, ln_b, b_pw2 = vecs
    ar = jnp.arange(ATT_BLOCK)
    tri = (ar[:, None] > ar[None, :]).astype(BF16)
    upper = (ar[:, None] < ar[None, :]).astype(BF16)
    row = lambda a, l: a[l][None, :]

    hosted = all_weights is None
    weights = [None] * depth if hosted else list(all_weights)
    next_w_in, next_rest = w_in0, None
    saved = []
    h = h0
    for l in range(depth):
        more = hosted and l + 1 < depth
        if hosted:
            ew, qkv, rest = _inproj(h, row(pre_g, l), next_w_in, gather_rest(0) if l == 0 else None)
            w_in, (w_pw2, w_out, conv_w) = next_w_in, (rest if l == 0 else next_rest)
        else:
            w_in, w_pw2, w_out, conv_w = weights[l]
            ew, qkv, _ = _inproj(h, row(pre_g, l), w_in)
        conv_w = jnp.repeat(conv_w, 8, axis=0)
        weights[l] = (w_in, w_pw2, w_out, conv_w)
        c1, c4, c5, next_rest = _conv_fwd(ew, conv_w, row(conv_b, l), row(ln_g, l), row(ln_b, l), w_pw2,
                                          row(b_pw2, l), gather_rest(l + 1) if more else None)
        att, carries, gathered = _attn_fwd(qkv, tri, gather_w_in(l + 1) if more else None)
        if more:
            next_w_in = gathered[0]
        hn, cat, mixed = _outproj(c5, att, ew, h, w_out, row(post_g, l))
        saved.append((h, ew, qkv, c1, c4, att, carries, cat, mixed))
        h = hn

    dh, loss = _loss_head(h, target_p, seq)

    vec_grads = [None] * depth
    mat_grads = [None] * depth
    pending = None

    def pieces(w_in_pieces, rest_pieces):
        n_rest = len(rest_pieces) // 2
        return [w_in_pieces[0], *rest_pieces[:n_rest], w_in_pieces[1], *rest_pieces[n_rest:]]

    for l in reversed(range(depth)):
        w_in, w_pw2, w_out, conv_w = weights[l]
        h_in, ew, qkv, c1, c4, att, carries, cat, mixed = saved[l]
        dmix, datt, dsg, dc4, dcg, dpost, dbpw2 = _outproj_bwd(dh, mixed, row(post_g, l), w_out, att, ew, c4)
        dw_out = _matmul_tn(cat, dmix, 1024, "dw_out")
        dq, dk, dv, landed = _attn_bwd(qkv, carries, datt, tri, upper, pending)
        if pending is not None:
            mat_grads[l + 1] = pieces(landed, rest_landed)
        dga, dgb, dw_pw2, dcw, dcb, dlng, dlnb = _conv_bwd(dc4, c1, ew, conv_w, row(ln_g, l), row(ln_b, l), w_pw2)
        dw_pw2 = dw_pw2.astype(BF16)
        rest_plan = None if reduce_layer is None else reduce_layer([dw_pw2, dw_out, dcw], 1)
        dh, dproj, u, dpre, rest_landed = _inproj_bwd(dga, dgb, dcg, dq, dk, dv, dsg, h_in, row(pre_g, l), w_in, dh,
                                                      rest_plan)
        dw_in = _matmul_tn(u, dproj, 1792, "dw_in")
        vec_grads[l] = (dpre[0], dpost[0], dcb[0], dlng[0], dlnb[0], dbpw2[0])
        if reduce_layer is None:
            mat_grads[l] = (dw_in, dw_pw2, dw_out, dcw)
        else:
            pending = reduce_layer([dw_in], 0)
    if pending is not None:
        mat_grads[0] = pieces(_run_exchange(pending, "reduce_grads"), rest_landed)

    vec_grads = [jnp.stack([g[k] for g in vec_grads]) for k in range(len(vecs))]
    return loss[0, 0], dh, vec_grads, mat_grads


N_CHIPS = 4
ANY = pl.BlockSpec(memory_space=pl.ANY)


def _chip_peers():
    x, y, c = lax.axis_index("x"), lax.axis_index("y"), lax.axis_index("c")
    return x, y, c, [(x, 1 - y), (1 - x, y), (1 - x, 1 - y)]


def _shard_slices(refs, dims, idx):
    out = []
    for ref, (axis, size) in zip(refs, dims):
        assert size % LANES == 0
        start = pl.multiple_of(idx * size, LANES)
        sl = [slice(None)] * len(ref.shape)
        sl[axis] = pl.ds(start, size)
        out.append(ref.at[tuple(sl)])
    return out


class _Exchange(NamedTuple):
    inputs: list
    out_shapes: list
    scratch: list
    start: Callable
    relay: Callable
    finish: Callable


def _run_exchange(plan, name):
    n_in, n_out = len(plan.inputs), len(plan.out_shapes)

    def body(*refs):
        parts = refs[:n_in], refs[n_in:n_in + n_out], refs[n_in + n_out:]
        plan.start(*parts)
        plan.relay(*parts)
        plan.finish(*parts)

    return _pcall(body, name=name, in_specs=[ANY] * n_in, out_specs=[ANY] * n_out, out_shape=plan.out_shapes,
                  scratch_shapes=plan.scratch)(*plan.inputs)


def _gather_plan(shards, dims):
    n = len(shards)
    full_shapes = []
    halves = []
    for s, (axis, size) in zip(shards, dims):
        shp = list(s.shape)
        shp[axis] = size * N_CHIPS
        full_shapes.append(jax.ShapeDtypeStruct(tuple(shp), s.dtype))
        tile_rows = 32 // s.dtype.itemsize
        assert s.shape[0] % (2 * tile_rows) == 0
        halves.append((s.shape[0] // 2, tile_rows))

    def half(refs, which):
        return [r.at[pl.ds(pl.multiple_of(which * h, t), h)] for r, (h, t) in zip(refs, halves)]

    def copies(srcs, outs, sems):
        send, recv, loc = sems
        x, y, c, peers = _chip_peers()
        sibling = (x, y, 1 - c)
        mine = _shard_slices(outs, dims, 2 * x + y)
        local = [pltpu.make_async_copy(s, d, loc.at[a]) for a, (s, d) in enumerate(zip(srcs, mine))]

        def remote(src, dst, slot, a, dev):
            return pltpu.make_async_remote_copy(src, dst, send.at[slot, a], recv.at[slot, a],
                                                device_id=dev, device_id_type=MESH)

        sends = [remote(s, d, k, a, (px, py, c))
                 for k, (px, py) in enumerate(peers) for a, (s, d) in enumerate(zip(half(srcs, c), half(mine, c)))]
        theirs = [_shard_slices(outs, dims, 2 * px + py) for px, py in peers]
        arrivals = [remote(s, d, k, a, (px, py, c))
                    for k, (px, py) in enumerate(peers)
                    for a, (s, d) in enumerate(zip(half(srcs, c), half(theirs[k], c)))]
        passed_on = [remote(d, d, 3 + k, a, sibling) for k in range(3) for a, d in enumerate(half(theirs[k], c))]
        from_sibling = [remote(d, d, 3 + k, a, sibling)
                        for k in range(3) for a, d in enumerate(half(theirs[k], 1 - c))]
        return local, sends, arrivals, passed_on, from_sibling

    def start(srcs, outs, sems):
        local, sends = copies(srcs, outs, sems)[:2]
        for cp in local + sends:
            cp.start()

    def relay(srcs, outs, sems):
        _, _, arrivals, passed_on, _ = copies(srcs, outs, sems)
        for arrived, onward in zip(arrivals, passed_on):
            arrived.wait_recv()
            onward.start()

    def finish(srcs, outs, sems):
        local, sends, _, passed_on, from_sibling = copies(srcs, outs, sems)
        for cp in from_sibling:
            cp.wait_recv()
        for cp in sends + passed_on:
            cp.wait_send()
        for cp in local:
            cp.wait()

    scratch = [pltpu.SemaphoreType.DMA((6, n)), pltpu.SemaphoreType.DMA((6, n)), pltpu.SemaphoreType.DMA((n,))]
    return _Exchange(list(shards), full_shapes, scratch, start, relay, finish)


def _reduce_plan(grads, dims):
    n = len(grads)
    piece_shapes = []
    for g, (axis, size) in zip(grads, dims):
        shp = list(g.shape)
        shp[axis] = size
        piece_shapes.append(jax.ShapeDtypeStruct((N_CHIPS,) + tuple(shp), g.dtype))

    def copies(srcs, outs, sems):
        mine, theirs = outs[:n], outs[n:]
        send, recv, loc = sems
        x, y, c, peers = _chip_peers()
        sibling = (x, y, 1 - c)
        own = _shard_slices(srcs, dims, 2 * x + y)

        def remote(src, dst, slot, a, dev):
            return pltpu.make_async_remote_copy(src, dst, send.at[slot, a], recv.at[slot, a],
                                                device_id=dev, device_id_type=MESH)

        local = [pltpu.make_async_copy(own[a], mine[a].at[3], loc.at[a]) for a in range(n)]
        to_sibling = [remote(own[a], theirs[a].at[3], 3, a, sibling) for a in range(n)]
        to_chips = [remote(src, mine[a].at[k], k, a, (px, py, c))
                    for k, (px, py) in enumerate(peers)
                    for a, src in enumerate(_shard_slices(srcs, dims, 2 * px + py))]
        passed_on = [remote(mine[a].at[k], theirs[a].at[k], 4 + k, a, sibling) for k in range(3) for a in range(n)]
        return local, to_sibling, to_chips, passed_on

    def start(srcs, outs, sems):
        local, to_sibling, to_chips, _ = copies(srcs, outs, sems)
        for cp in local + to_sibling + to_chips:
            cp.start()

    def relay(srcs, outs, sems):
        _, _, to_chips, passed_on = copies(srcs, outs, sems)
        for arrived, onward in zip(to_chips, passed_on):
            arrived.wait_recv()
            onward.start()

    def finish(srcs, outs, sems):
        local, to_sibling, to_chips, passed_on = copies(srcs, outs, sems)
        for cp in to_sibling + passed_on:
            cp.wait_recv()
        for cp in to_sibling + to_chips + passed_on:
            cp.wait_send()
        for cp in local:
            cp.wait()

    scratch = [pltpu.SemaphoreType.DMA((7, n)), pltpu.SemaphoreType.DMA((7, n)), pltpu.SemaphoreType.DMA((n,))]
    return _Exchange(list(grads), piece_shapes * 2, scratch, start, relay, finish)


def _allsum_small(pack):
    rows, cols = pack.shape
    ndev = 8

    def body(p_ref, o_ref, buf, send, recv):
        x, y, c = lax.axis_index("x"), lax.axis_index("y"), lax.axis_index("c")
        me = 4 * x + 2 * y + c
        buf[me] = p_ref[...]
        started = []
        for r in range(1, ndev):
            bx, by, bc = (r >> 2) & 1, (r >> 1) & 1, r & 1
            dev = (x ^ bx, y ^ by, c ^ bc)
            cp = pltpu.make_async_remote_copy(p_ref, buf.at[me], send.at[r], recv.at[r],
                                              device_id=dev, device_id_type=MESH)
            cp.start()
            started.append(cp)
        for r in range(1, ndev):
            pltpu.make_async_remote_copy(p_ref, buf.at[me ^ r], send.at[r], recv.at[r],
                                         device_id=(x, y, c), device_id_type=MESH).wait_recv()
        for cp in started:
            cp.wait_send()
        acc = buf[0]
        for d in range(1, ndev):
            acc = acc + buf[d]
        o_ref[...] = acc

    vm = pl.BlockSpec(memory_space=pltpu.VMEM)
    return _pcall(
        body, name="allsum_small", in_specs=[vm], out_specs=vm,
        out_shape=jax.ShapeDtypeStruct((rows, cols), F32),
        scratch_shapes=[pltpu.VMEM((ndev, rows, cols), F32), pltpu.SemaphoreType.DMA((ndev,)),
                        pltpu.SemaphoreType.DMA((ndev,))],
    )(pack)


def _adamw(parts, w, m, v, layer, prev, name):
    _, rows, cols = w.shape
    tr = ROW_TILE if rows % ROW_TILE == 0 else rows
    counts = [p.shape[0] for p in parts]
    n_parts = len(parts)
    n_prev = 0 if prev is None else 4

    def body(*refs):
        part_refs = refs[:n_parts]
        w_ref, m_ref, v_ref = refs[n_parts:n_parts + 3]
        g_ref, d_ref, nm_ref, nv_ref = refs[n_parts + 3 + n_prev:]
        g = None
        for p_ref, cnt in zip(part_refs, counts):
            s = p_ref[0].astype(F32)
            for k in range(1, cnt):
                s = s + p_ref[k].astype(F32)
            g = s if g is None else g + s
        m2 = ADAM_B1 * m_ref[0] + (1.0 - ADAM_B1) * g
        v2 = ADAM_B2 * v_ref[0] + (1.0 - ADAM_B2) * (g * g)
        m_hat = m2 / (1.0 - ADAM_B1 ** ADAM_STEP)
        v_hat = v2 / (1.0 - ADAM_B2 ** ADAM_STEP)
        g_ref[0] = g
        d_ref[0] = -ADAM_LR * (m_hat / (jnp.sqrt(v_hat) + ADAM_EPS) + ADAM_WD * w_ref[0])
        nm_ref[0] = m2
        nv_ref[0] = v2

    blk = pl.BlockSpec((1, tr, cols), lambda i: (layer, i, 0))
    shp = jax.ShapeDtypeStruct(w.shape, F32)
    return _pcall(
        body, name=name, grid=(rows // tr,),
        in_specs=[pl.BlockSpec((cnt, tr, cols), lambda i: (0, i, 0)) for cnt in counts] + [blk] * 3 + [ANY] * n_prev,
        out_specs=[blk] * 4, out_shape=[shp] * 4,
        input_output_aliases={n_parts + 3 + k: k for k in range(n_prev)},
        compiler_params=_params("parallel"),
    )(*parts, w, m, v, *(prev or ()))


def kernel(x, meta_tokens, pre_norm_g, post_norm_g, w_in, conv_w, conv_b, conv_ln_g, conv_ln_b, w_pw2, b_pw2, w_out, loss_target, m_meta_tokens, m_pre_norm_g, m_post_norm_g, m_w_in, m_conv_w, m_conv_b, m_conv_ln_g, m_conv_ln_b, m_w_pw2, m_b_pw2, m_w_out, v_meta_tokens, v_pre_norm_g, v_post_norm_g, v_w_in, v_conv_w, v_conv_b, v_conv_ln_g, v_conv_ln_b, v_w_pw2, v_b_pw2, v_w_out):
    seq, d = x.shape[1], x.shape[2]
    depth = w_in.shape[0]
    length = N_META + seq
    lp = -(-length // ATT_BLOCK) * ATT_BLOCK
    tap_pad = ((0, 0), (0, CONV_PAD - CONV_WIDTH), (0, 0))

    shards = (w_in.astype(BF16), w_pw2.astype(BF16), w_out.astype(BF16), jnp.pad(conv_w, tap_pad))
    dims = [(1, w_in.shape[2]), (0, w_pw2.shape[1]), (0, w_out.shape[1]), (1, conv_w.shape[2])]
    layer_shards = lambda l: [s[l] for s in shards]

    w_in0, meta_f = _run_exchange(_gather_plan([shards[0][0], meta_tokens], [dims[0], (1, meta_tokens.shape[1])]),
                                  "gather_weights")

    h0 = jnp.concatenate([meta_f, x[0], jnp.zeros((lp - length, d), F32)], axis=0)
    target_p = jnp.pad(loss_target[0], ((N_META, lp - length), (0, 0)))
    vecs = (pre_norm_g, post_norm_g, conv_b, conv_ln_g, conv_ln_b, b_pw2)
    loss, dh0, vec_grads, pieces = _local_step(
        h0, target_p, seq, vecs, depth, w_in0=w_in0,
        gather_w_in=lambda l: _gather_plan([shards[0][l]], dims[:1]),
        gather_rest=lambda l: _gather_plan(layer_shards(l)[1:], dims[1:]),
        reduce_layer=lambda grads, first: _reduce_plan(list(grads), dims[first:first + len(grads)]))

    def update(k, w, m, v, name):
        outs = None
        for l in reversed(range(depth)):
            outs = _adamw([pieces[l][k], pieces[l][4 + k]], w, m, v, l, outs, name)
        return outs

    up_w_in = update(0, w_in, m_w_in, v_w_in, "adamw_w_in")
    up_w_pw2 = update(1, w_pw2, m_w_pw2, v_w_pw2, "adamw_w_pw2")
    up_w_out = update(2, w_out, m_w_out, v_w_out, "adamw_w_out")
    up_conv_w = [o[:, :CONV_WIDTH] for o in update(3, jnp.pad(conv_w, tap_pad), jnp.pad(m_conv_w, tap_pad),
                                                   jnp.pad(v_conv_w, tap_pad, constant_values=1.0), "adamw_conv_w")]

    two = lambda a: a.reshape(-1, d)
    vec_rows = [two(g) for g in vec_grads]
    n_vec = sum(a.shape[0] for a in vec_rows)
    pack = jnp.concatenate(vec_rows + [dh0[:N_META], jnp.full((8, d), loss, F32)], axis=0)
    pack = jnp.pad(pack, ((0, -pack.shape[0] % 8), (0, 0)))
    tot = _allsum_small(pack)
    loss_all = tot[n_vec + N_META, 0]

    cat = lambda arrs: jnp.concatenate([two(t) for t in arrs], axis=0)[None]
    small_m = (m_pre_norm_g, m_post_norm_g, m_conv_b, m_conv_ln_g, m_conv_ln_b, m_b_pw2)
    small_v = (v_pre_norm_g, v_post_norm_g, v_conv_b, v_conv_ln_g, v_conv_ln_b, v_b_pw2)
    up_small = _adamw([tot[None, :n_vec]], cat(vecs), cat(small_m), cat(small_v), 0, None, "adamw_vectors")

    def unpack(o):
        res, r0 = [], 0
        for t in vecs:
            nrow = t.size // d
            res.append(o[0, r0:r0 + nrow].reshape(t.shape))
            r0 += nrow
        return res

    up_small = [unpack(o) for o in up_small]
    chip = 2 * lax.axis_index("x") + lax.axis_index("y")
    mcols = meta_tokens.shape[1]
    g_meta = lax.dynamic_slice_in_dim(tot[n_vec:n_vec + N_META], chip * mcols, mcols, axis=1)
    up_meta = [o[0] for o in _adamw([g_meta[None]], meta_tokens[None], m_meta_tokens[None], v_meta_tokens[None],
                                    0, None, "adamw_meta")]

    grad_x = dh0[N_META:length][None]
    outs = [loss_all, grad_x]
    for j in range(4):
        pre, post, cb, lg, lb, bp = up_small[j]
        outs += [up_meta[j], pre, post, up_w_in[j], up_conv_w[j], cb, lg, lb, up_w_pw2[j], bp, up_w_out[j]]
    return tuple(outs)
```
